```python
import math
import jax, jax.numpy as jnp
from jax import lax
import numpy as np

D_MODEL = 1024
BATCH = 8
SEQ = 8192
DEPTH = 4

N_MIXERS = 4
D_FF = 2816
NORM_EPS = 1e-6
N_SUBLAYERS = 3
N_ADA = 3 * N_SUBLAYERS
POOL_WINDOWS = (2, 4, 8, 16)
POOL_GROUP = D_MODEL // len(POOL_WINDOWS)
POOL_MAX_W = max(POOL_WINDOWS)
FOX_HEADS = 16
FOX_HEAD_DIM = D_MODEL // FOX_HEADS
FOX_BLOCK = 128
S5_GROUP = 16
S5_GROUPS = D_MODEL // S5_GROUP
S5_STATE = 64
S5_DT_MIN = 1e-3
S5_DT_MAX = 1e-1
CONV_WIDTH = 3
LAYERS_PER_MIXER = tuple(len(range(m, DEPTH, N_MIXERS)) for m in range(N_MIXERS))

kernel_name = "hybrid_pool_fox_s5_conv_macaron"


def rms_norm(x, gain):
    xf = x.astype(jnp.float32)
    y = xf * lax.rsqrt(jnp.mean(xf * xf, axis=-1, keepdims=True) + NORM_EPS)
    return (y * gain.astype(jnp.float32)).astype(x.dtype)


def adaln(x, gain, shift, scale):
    return rms_norm(x, gain) * (1.0 + scale[:, None, :]) + shift[:, None, :]


def swiglu(h, w_in, w_out):
    g, u = jnp.split(h @ w_in, 2, axis=-1)
    return (jax.nn.silu(g) * u) @ w_out


def pool_mixer(h, w_grp, scale):
    b_, s_, d_ = h.shape
    hf = h.astype(jnp.float32)
    cs = jnp.cumsum(hf, axis=1)
    cs_pad = jnp.pad(cs, ((0, 0), (POOL_MAX_W, 0), (0, 0)))
    pos = jnp.arange(s_)
    outs = []
    for gi, w in enumerate(POOL_WINDOWS):
        sl = slice(gi * POOL_GROUP, (gi + 1) * POOL_GROUP)
        prev = cs_pad[:, POOL_MAX_W - w:POOL_MAX_W - w + s_, sl]
        cnt = jnp.minimum(pos + 1, w).astype(jnp.float32)[None, :, None]
        outs.append((cs[..., sl] - prev) / cnt - hf[..., sl])
    pooled = jnp.stack(outs, axis=2)
    mixed = jnp.einsum('bsgc,gcd->bsgd', pooled, w_grp.astype(jnp.float32))
    return (mixed.reshape(b_, s_, d_) * scale.astype(jnp.float32)).astype(h.dtype)


def fox_attention(h, w_in, b_f, q_gain, k_gain, w_o):
    b_, s_, d_ = h.shape
    proj = h @ w_in
    q, k, v, f_logit = jnp.split(proj, [d_, 2 * d_, 3 * d_], axis=-1)
    q = rms_norm(q.reshape(b_, s_, FOX_HEADS, FOX_HEAD_DIM), q_gain) * (FOX_HEAD_DIM ** -0.5)
    k = rms_norm(k.reshape(b_, s_, FOX_HEADS, FOX_HEAD_DIM), k_gain)
    v = v.reshape(b_, s_, FOX_HEADS, FOX_HEAD_DIM)
    q, k, v = (t.transpose(0, 2, 1, 3) for t in (q, k, v))
    log_f = jax.nn.log_sigmoid((f_logit + b_f).astype(jnp.float32))
    cum_f = jnp.cumsum(log_f, axis=1).transpose(0, 2, 1)
    q_idx = jnp.arange(FOX_BLOCK)
    outs = []
    for blk in range(s_ // FOX_BLOCK):
        q0 = blk * FOX_BLOCK
        kv_end = q0 + FOX_BLOCK
        s = jnp.einsum('bhqd,bhkd->bhqk', q[:, :, q0:kv_end], k[:, :, :kv_end]).astype(jnp.float32)
        s = s + cum_f[:, :, q0:kv_end, None] - cum_f[:, :, None, :kv_end]
        mask = (q0 + q_idx)[:, None] >= jnp.arange(kv_end)[None, :]
        p = jax.nn.softmax(jnp.where(mask, s, -jnp.inf), axis=-1)
        outs.append(jnp.einsum('bhqk,bhkd->bhqd', p.astype(v.dtype), v[:, :, :kv_end]))
    o = jnp.concatenate(outs, axis=2).transpose(0, 2, 1, 3).reshape(b_, s_, d_)
    return o @ w_o


def s5_mixer(h, lam_re, lam_im, log_dt, b_re, b_im, c_re, c_im, d_skip, w_glu):
    f32 = jnp.float32
    b_, s_, d_ = h.shape
    u = h.astype(f32).reshape(b_, s_, S5_GROUPS, S5_GROUP)
    dt = jnp.exp(log_dt.astype(f32))[:, None]
    ar, ai = lam_re.astype(f32), lam_im.astype(f32)
    mag = jnp.exp(ar * dt)
    lb_re, lb_im = mag * jnp.cos(ai * dt), mag * jnp.sin(ai * dt)
    den = ar * ar + ai * ai
    nr, ni = lb_re - 1.0, lb_im
    k_re = (nr * ar + ni * ai) / den
    k_im = (ni * ar - nr * ai) / den
    br, bi = b_re.astype(f32), b_im.astype(f32)
    bb_re = k_re[..., None] * br - k_im[..., None] * bi
    bb_im = k_re[..., None] * bi + k_im[..., None] * br
    x_re = jnp.einsum('bsgi,gni->sbgn', u, bb_re)
    x_im = jnp.einsum('bsgi,gni->sbgn', u, bb_im)
    a_re = jnp.broadcast_to(lb_re, (s_, 1) + lb_re.shape)
    a_im = jnp.broadcast_to(lb_im, (s_, 1) + lb_im.shape)

    def combine(e1, e2):
        a1r, a1i, b1r, b1i = e1
        a2r, a2i, b2r, b2i = e2
        return (a1r * a2r - a1i * a2i, a1r * a2i + a1i * a2r,
                a2r * b1r - a2i * b1i + b2r, a2r * b1i + a2i * b1r + b2i)

    _, _, st_re, st_im = lax.associative_scan(combine, (a_re, a_im, x_re, x_im), axis=0)
    y = (jnp.einsum('sbgn,gin->bsgi', st_re, c_re.astype(f32))
         - jnp.einsum('sbgn,gin->bsgi', st_im, c_im.astype(f32)))
    y = y + d_skip.astype(f32).reshape(S5_GROUPS, S5_GROUP) * u
    g = jax.nn.gelu(y.reshape(b_, s_, d_).astype(h.dtype))
    return g * jax.nn.sigmoid(g @ w_glu)


def short_conv_mixer(h, w_in, conv_w, w_out):
    b_gate, c_gate, z = jnp.split(h @ w_in, 3, axis=-1)
    conv = lax.conv_general_dilated(c_gate * z, conv_w, window_strides=(1,),
                                    padding=((CONV_WIDTH - 1, 0),),
                                    dimension_numbers=('NWC', 'WIO', 'NWC'),
                                    feature_group_count=D_MODEL)
    return (b_gate * conv) @ w_out


def _fwd_setup_inputs(seed: int = 0) -> dict:
    key = jax.random.key(seed)
    ks = iter(jax.random.split(key, 32))
    f32 = jnp.float32
    D = D_MODEL
    n_a, n_b, n_c, n_d = LAYERS_PER_MIXER

    def nrm(shape, std):
        return jax.random.normal(next(ks), shape, f32) * std

    x = nrm((BATCH, SEQ, D), 1.0)
    c = nrm((BATCH, D), 1.0)
    ada_w = nrm((DEPTH, D, N_ADA * D), 0.1 * D ** -0.5)
    ada_b = nrm((DEPTH, N_ADA * D), 0.01)
    norm_g = 1.0 + nrm((DEPTH, N_SUBLAYERS, D), 0.01)
    ffn_w_in = nrm((DEPTH, 2, D, 2 * D_FF), D ** -0.5)
    ffn_w_out = nrm((DEPTH, 2, D_FF, D), D_FF ** -0.5)
    pool_w = nrm((n_a, len(POOL_WINDOWS), POOL_GROUP, POOL_GROUP), POOL_GROUP ** -0.5)
    pool_scale = 1.0 + nrm((n_a, D), 0.02)
    fox_w_in = nrm((n_b, D, 3 * D + FOX_HEADS), D ** -0.5)
    fox_b_f = 2.0 + nrm((n_b, FOX_HEADS), 0.5)
    fox_q_gain = 1.0 + nrm((n_b, FOX_HEAD_DIM), 0.01)
    fox_k_gain = 1.0 + nrm((n_b, FOX_HEAD_DIM), 0.01)
    fox_w_o = nrm((n_b, D, D), D ** -0.5)
    n_idx = jnp.arange(S5_STATE, dtype=f32)
    s5_lam_re = -0.5 + nrm((n_c, S5_GROUPS, S5_STATE), 0.01)
    s5_lam_im = math.pi * n_idx + nrm((n_c, S5_GROUPS, S5_STATE), 0.01)
    s5_log_dt = jax.random.uniform(next(ks), (n_c, S5_GROUPS), f32,
                                   math.log(S5_DT_MIN), math.log(S5_DT_MAX))
    s5_b_re = nrm((n_c, S5_GROUPS, S5_STATE, S5_GROUP), (2 * S5_GROUP) ** -0.5)
    s5_b_im = nrm((n_c, S5_GROUPS, S5_STATE, S5_GROUP), (2 * S5_GROUP) ** -0.5)
    s5_c_re = nrm((n_c, S5_GROUPS, S5_GROUP, S5_STATE), 2.0 * S5_STATE ** -0.5)
    s5_c_im = nrm((n_c, S5_GROUPS, S5_GROUP, S5_STATE), 2.0 * S5_STATE ** -0.5)
    s5_d = nrm((n_c, D), 0.5)
    s5_w_glu = nrm((n_c, D, D), D ** -0.5)
    conv_w_in = nrm((n_d, D, 3 * D), D ** -0.5)
    conv_w = nrm((n_d, CONV_WIDTH, 1, D), CONV_WIDTH ** -0.5)
    conv_w_out = nrm((n_d, D, D), D ** -0.5)
    return {"x": x, "c": c, "ada_w": ada_w, "ada_b": ada_b, "norm_g": norm_g,
            "ffn_w_in": ffn_w_in, "ffn_w_out": ffn_w_out,
            "pool_w": pool_w, "pool_scale": pool_scale,
            "fox_w_in": fox_w_in, "fox_b_f": fox_b_f, "fox_q_gain": fox_q_gain,
            "fox_k_gain": fox_k_gain, "fox_w_o": fox_w_o,
            "s5_lam_re": s5_lam_re, "s5_lam_im": s5_lam_im, "s5_log_dt": s5_log_dt,
            "s5_b_re": s5_b_re, "s5_b_im": s5_b_im, "s5_c_re": s5_c_re, "s5_c_im": s5_c_im,
            "s5_d": s5_d, "s5_w_glu": s5_w_glu,
            "conv_w_in": conv_w_in, "conv_w": conv_w, "conv_w_out": conv_w_out}


def _fwd_reference(x, c, ada_w, ada_b, norm_g, ffn_w_in, ffn_w_out, pool_w, pool_scale,
              fox_w_in, fox_b_f, fox_q_gain, fox_k_gain, fox_w_o,
              s5_lam_re, s5_lam_im, s5_log_dt, s5_b_re, s5_b_im, s5_c_re, s5_c_im,
              s5_d, s5_w_glu, conv_w_in, conv_w, conv_w_out):
    b_ = x.shape[0]
    cond = jax.nn.silu(c)
    for i in range(DEPTH):
        mod = (cond @ ada_w[i] + ada_b[i]).reshape(b_, N_SUBLAYERS, 3, D_MODEL)
        h = adaln(x, norm_g[i, 0], mod[:, 0, 0], mod[:, 0, 1])
        x = x + 0.5 * (1.0 + mod[:, 0, 2][:, None, :]) * swiglu(h, ffn_w_in[i, 0], ffn_w_out[i, 0])
        h = adaln(x, norm_g[i, 1], mod[:, 1, 0], mod[:, 1, 1])
        m, r = i % N_MIXERS, i // N_MIXERS
        if m == 0:
            y = pool_mixer(h, pool_w[r], pool_scale[r])
        elif m == 1:
            y = fox_attention(h, fox_w_in[r], fox_b_f[r], fox_q_gain[r], fox_k_gain[r], fox_w_o[r])
        elif m == 2:
            y = s5_mixer(h, s5_lam_re[r], s5_lam_im[r], s5_log_dt[r], s5_b_re[r], s5_b_im[r],
                         s5_c_re[r], s5_c_im[r], s5_d[r], s5_w_glu[r])
        else:
            y = short_conv_mixer(h, conv_w_in[r], conv_w[r], conv_w_out[r])
        x = x + (1.0 + mod[:, 1, 2][:, None, :]) * y
        h = adaln(x, norm_g[i, 2], mod[:, 2, 0], mod[:, 2, 1])
        x = x + 0.5 * (1.0 + mod[:, 2, 2][:, None, :]) * swiglu(h, ffn_w_in[i, 1], ffn_w_out[i, 1])
    return x


import jax as _jax
import jax.numpy as _jnp

TWIN_FORMAT = 'train_step'
FWD_PARAMS = ['x', 'c', 'ada_w', 'ada_b', 'norm_g', 'ffn_w_in', 'ffn_w_out', 'pool_w', 'pool_scale', 'fox_w_in', 'fox_b_f', 'fox_q_gain', 'fox_k_gain', 'fox_w_o', 's5_lam_re', 's5_lam_im', 's5_log_dt', 's5_b_re', 's5_b_im', 's5_c_re', 's5_c_im', 's5_d', 's5_w_glu', 'conv_w_in', 'conv_w', 'conv_w_out']
TWIN_WEIGHTS = ['ada_w', 'ada_b', 'norm_g', 'ffn_w_in', 'ffn_w_out', 'pool_w', 'pool_scale', 'fox_w_in', 'fox_b_f', 'fox_q_gain', 'fox_k_gain', 'fox_w_o', 's5_lam_re', 's5_lam_im', 's5_log_dt', 's5_b_re', 's5_b_im', 's5_c_re', 's5_c_im', 's5_d', 's5_w_glu', 'conv_w_in', 'conv_w', 'conv_w_out']
TWIN_DIFF_INPUT = 'x'
TWIN_INPUTS = ['x', 'c', 'ada_w', 'ada_b', 'norm_g', 'ffn_w_in', 'ffn_w_out', 'pool_w', 'pool_scale', 'fox_w_in', 'fox_b_f', 'fox_q_gain', 'fox_k_gain', 'fox_w_o', 's5_lam_re', 's5_lam_im', 's5_log_dt', 's5_b_re', 's5_b_im', 's5_c_re', 's5_c_im', 's5_d', 's5_w_glu', 'conv_w_in', 'conv_w', 'conv_w_out', 'loss_target', 'm_ada_w', 'm_ada_b', 'm_norm_g', 'm_ffn_w_in', 'm_ffn_w_out', 'm_pool_w', 'm_pool_scale', 'm_fox_w_in', 'm_fox_b_f', 'm_fox_q_gain', 'm_fox_k_gain', 'm_fox_w_o', 'm_s5_lam_re', 'm_s5_lam_im', 'm_s5_log_dt', 'm_s5_b_re', 'm_s5_b_im', 'm_s5_c_re', 'm_s5_c_im', 'm_s5_d', 'm_s5_w_glu', 'm_conv_w_in', 'm_conv_w', 'm_conv_w_out', 'v_ada_w', 'v_ada_b', 'v_norm_g', 'v_ffn_w_in', 'v_ffn_w_out', 'v_pool_w', 'v_pool_scale', 'v_fox_w_in', 'v_fox_b_f', 'v_fox_q_gain', 'v_fox_k_gain', 'v_fox_w_o', 'v_s5_lam_re', 'v_s5_lam_im', 'v_s5_log_dt', 'v_s5_b_re', 'v_s5_b_im', 'v_s5_c_re', 'v_s5_c_im', 'v_s5_d', 'v_s5_w_glu', 'v_conv_w_in', 'v_conv_w', 'v_conv_w_out']
TWIN_OUTPUTS = ['loss', 'grad_x', 'grad_ada_w', 'grad_ada_b', 'grad_norm_g', 'grad_ffn_w_in', 'grad_ffn_w_out', 'grad_pool_w', 'grad_pool_scale', 'grad_fox_w_in', 'grad_fox_b_f', 'grad_fox_q_gain', 'grad_fox_k_gain', 'grad_fox_w_o', 'grad_s5_lam_re', 'grad_s5_lam_im', 'grad_s5_log_dt', 'grad_s5_b_re', 'grad_s5_b_im', 'grad_s5_c_re', 'grad_s5_c_im', 'grad_s5_d', 'grad_s5_w_glu', 'grad_conv_w_in', 'grad_conv_w', 'grad_conv_w_out', 'delta_ada_w', 'delta_ada_b', 'delta_norm_g', 'delta_ffn_w_in', 'delta_ffn_w_out', 'delta_pool_w', 'delta_pool_scale', 'delta_fox_w_in', 'delta_fox_b_f', 'delta_fox_q_gain', 'delta_fox_k_gain', 'delta_fox_w_o', 'delta_s5_lam_re', 'delta_s5_lam_im', 'delta_s5_log_dt', 'delta_s5_b_re', 'delta_s5_b_im', 'delta_s5_c_re', 'delta_s5_c_im', 'delta_s5_d', 'delta_s5_w_glu', 'delta_conv_w_in', 'delta_conv_w', 'delta_conv_w_out', 'new_m_ada_w', 'new_m_ada_b', 'new_m_norm_g', 'new_m_ffn_w_in', 'new_m_ffn_w_out', 'new_m_pool_w', 'new_m_pool_scale', 'new_m_fox_w_in', 'new_m_fox_b_f', 'new_m_fox_q_gain', 'new_m_fox_k_gain', 'new_m_fox_w_o', 'new_m_s5_lam_re', 'new_m_s5_lam_im', 'new_m_s5_log_dt', 'new_m_s5_b_re', 'new_m_s5_b_im', 'new_m_s5_c_re', 'new_m_s5_c_im', 'new_m_s5_d', 'new_m_s5_w_glu', 'new_m_conv_w_in', 'new_m_conv_w', 'new_m_conv_w_out', 'new_v_ada_w', 'new_v_ada_b', 'new_v_norm_g', 'new_v_ffn_w_in', 'new_v_ffn_w_out', 'new_v_pool_w', 'new_v_pool_scale', 'new_v_fox_w_in', 'new_v_fox_b_f', 'new_v_fox_q_gain', 'new_v_fox_k_gain', 'new_v_fox_w_o', 'new_v_s5_lam_re', 'new_v_s5_lam_im', 'new_v_s5_log_dt', 'new_v_s5_b_re', 'new_v_s5_b_im', 'new_v_s5_c_re', 'new_v_s5_c_im', 'new_v_s5_d', 'new_v_s5_w_glu', 'new_v_conv_w_in', 'new_v_conv_w', 'new_v_conv_w_out']
TWIN_LEAF_KINDS = {'loss': 'loss', 'grad_x': 'grad_x', 'grad_ada_w': 'grad_w', 'grad_ada_b': 'grad_w', 'grad_norm_g': 'grad_w', 'grad_ffn_w_in': 'grad_w', 'grad_ffn_w_out': 'grad_w', 'grad_pool_w': 'grad_w', 'grad_pool_scale': 'grad_w', 'grad_fox_w_in': 'grad_w', 'grad_fox_b_f': 'grad_w', 'grad_fox_q_gain': 'grad_w', 'grad_fox_k_gain': 'grad_w', 'grad_fox_w_o': 'grad_w', 'grad_s5_lam_re': 'grad_w', 'grad_s5_lam_im': 'grad_w', 'grad_s5_log_dt': 'grad_w', 'grad_s5_b_re': 'grad_w', 'grad_s5_b_im': 'grad_w', 'grad_s5_c_re': 'grad_w', 'grad_s5_c_im': 'grad_w', 'grad_s5_d': 'grad_w', 'grad_s5_w_glu': 'grad_w', 'grad_conv_w_in': 'grad_w', 'grad_conv_w': 'grad_w', 'grad_conv_w_out': 'grad_w', 'delta_ada_w': 'delta_w', 'delta_ada_b': 'delta_w', 'delta_norm_g': 'delta_w', 'delta_ffn_w_in': 'delta_w', 'delta_ffn_w_out': 'delta_w', 'delta_pool_w': 'delta_w', 'delta_pool_scale': 'delta_w', 'delta_fox_w_in': 'delta_w', 'delta_fox_b_f': 'delta_w', 'delta_fox_q_gain': 'delta_w', 'delta_fox_k_gain': 'delta_w', 'delta_fox_w_o': 'delta_w', 'delta_s5_lam_re': 'delta_w', 'delta_s5_lam_im': 'delta_w', 'delta_s5_log_dt': 'delta_w', 'delta_s5_b_re': 'delta_w', 'delta_s5_b_im': 'delta_w', 'delta_s5_c_re': 'delta_w', 'delta_s5_c_im': 'delta_w', 'delta_s5_d': 'delta_w', 'delta_s5_w_glu': 'delta_w', 'delta_conv_w_in': 'delta_w', 'delta_conv_w': 'delta_w', 'delta_conv_w_out': 'delta_w', 'new_m_ada_w': 'new_m', 'new_m_ada_b': 'new_m', 'new_m_norm_g': 'new_m', 'new_m_ffn_w_in': 'new_m', 'new_m_ffn_w_out': 'new_m', 'new_m_pool_w': 'new_m', 'new_m_pool_scale': 'new_m', 'new_m_fox_w_in': 'new_m', 'new_m_fox_b_f': 'new_m', 'new_m_fox_q_gain': 'new_m', 'new_m_fox_k_gain': 'new_m', 'new_m_fox_w_o': 'new_m', 'new_m_s5_lam_re': 'new_m', 'new_m_s5_lam_im': 'new_m', 'new_m_s5_log_dt': 'new_m', 'new_m_s5_b_re': 'new_m', 'new_m_s5_b_im': 'new_m', 'new_m_s5_c_re': 'new_m', 'new_m_s5_c_im': 'new_m', 'new_m_s5_d': 'new_m', 'new_m_s5_w_glu': 'new_m', 'new_m_conv_w_in': 'new_m', 'new_m_conv_w': 'new_m', 'new_m_conv_w_out': 'new_m', 'new_v_ada_w': 'new_v', 'new_v_ada_b': 'new_v', 'new_v_norm_g': 'new_v', 'new_v_ffn_w_in': 'new_v', 'new_v_ffn_w_out': 'new_v', 'new_v_pool_w': 'new_v', 'new_v_pool_scale': 'new_v', 'new_v_fox_w_in': 'new_v', 'new_v_fox_b_f': 'new_v', 'new_v_fox_q_gain': 'new_v', 'new_v_fox_k_gain': 'new_v', 'new_v_fox_w_o': 'new_v', 'new_v_s5_lam_re': 'new_v', 'new_v_s5_lam_im': 'new_v', 'new_v_s5_log_dt': 'new_v', 'new_v_s5_b_re': 'new_v', 'new_v_s5_b_im': 'new_v', 'new_v_s5_c_re': 'new_v', 'new_v_s5_c_im': 'new_v', 'new_v_s5_d': 'new_v', 'new_v_s5_w_glu': 'new_v', 'new_v_conv_w_in': 'new_v', 'new_v_conv_w': 'new_v', 'new_v_conv_w_out': 'new_v'}


def _forward(args):
    return _fwd_reference(*[args[k] for k in FWD_PARAMS])


def _output_shape():
    def fwd():
        inp = _fwd_setup_inputs(0)
        return _fwd_reference(*[inp[k] for k in FWD_PARAMS])
    out = _jax.eval_shape(fwd)
    return out.shape, out.dtype

N_MICROBATCH = 1
ADAM_LR = 0.001
ADAM_B1 = 0.9
ADAM_B2 = 0.999
ADAM_EPS = 1e-08
ADAM_WD = 0.01
ADAM_STEP = 10
PER_EXAMPLE_BATCH_AXIS = {'x': 0, 'c': 0, 'loss_target': 0}
SHARED_INPUTS = []
_WEIGHT_DTYPES = {'ada_w': _jnp.float32, 'ada_b': _jnp.float32, 'norm_g': _jnp.float32, 'ffn_w_in': _jnp.float32, 'ffn_w_out': _jnp.float32, 'pool_w': _jnp.float32, 'pool_scale': _jnp.float32, 'fox_w_in': _jnp.float32, 'fox_b_f': _jnp.float32, 'fox_q_gain': _jnp.float32, 'fox_k_gain': _jnp.float32, 'fox_w_o': _jnp.float32, 's5_lam_re': _jnp.float32, 's5_lam_im': _jnp.float32, 's5_log_dt': _jnp.float32, 's5_b_re': _jnp.float32, 's5_b_im': _jnp.float32, 's5_c_re': _jnp.float32, 's5_c_im': _jnp.float32, 's5_d': _jnp.float32, 's5_w_glu': _jnp.float32, 'conv_w_in': _jnp.float32, 'conv_w': _jnp.float32, 'conv_w_out': _jnp.float32}
MOMENT_SCALE = {'ada_w': 1.064270e+01, 'ada_b': 3.762030e+01, 'norm_g': 5.953199e+01, 'ffn_w_in': 3.170010e-01, 'ffn_w_out': 5.552757e-01, 'pool_w': 3.716067e+00, 'pool_scale': 5.127606e+01, 'fox_w_in': 7.394782e-01, 'fox_b_f': 1.995725e+02, 'fox_q_gain': 6.826948e+01, 'fox_k_gain': 6.816318e+01, 'fox_w_o': 8.027844e-01, 's5_lam_re': 2.032978e-01, 's5_lam_im': 2.518609e-01, 's5_log_dt': 2.405751e+01, 's5_b_re': 1.384192e-01, 's5_b_im': 1.271406e-01, 's5_c_re': 8.171607e-02, 's5_c_im': 1.047634e-01, 's5_d': 2.548224e+01, 's5_w_glu': 8.803715e-01, 'conv_w_in': 2.274543e+00, 'conv_w': 3.748237e+01, 'conv_w_out': 1.990361e+00}


def _to_microbatches(a, axis):
    t = _jnp.moveaxis(a, axis, 0)
    t = t.reshape((N_MICROBATCH, t.shape[0] // N_MICROBATCH) + t.shape[1:])
    return _jnp.moveaxis(t, 1, axis + 1)


def setup_inputs(seed: int = 0) -> dict:
    inp = _fwd_setup_inputs(seed)
    key = _jax.random.fold_in(_jax.random.key(seed), 7919)
    shape, _ = _output_shape()
    out = dict(inp)
    out["loss_target"] = _jax.random.normal(_jax.random.fold_in(key, 0), shape, _jnp.float32)
    for i, name in enumerate(TWIN_WEIGHTS):
        w = inp[name].astype(_jnp.float32)
        if MOMENT_SCALE is None:
            s = _jnp.sqrt(_jnp.mean(_jnp.square(w)) + 1e-30)
        else:
            s = MOMENT_SCALE[name]
        km, kv = _jax.random.split(_jax.random.fold_in(key, i + 1))
        out[name] = w
        out["m_" + name] = s * _jax.random.normal(km, w.shape, _jnp.float32)
        out["v_" + name] = (s * s) * _jax.random.uniform(kv, w.shape, _jnp.float32, 0.5, 1.5)
    if N_MICROBATCH > 1:
        for name, axis in PER_EXAMPLE_BATCH_AXIS.items():
            out[name] = _to_microbatches(out[name], axis)
    return {'x': out['x'], 'c': out['c'], 'ada_w': out['ada_w'], 'ada_b': out['ada_b'], 'norm_g': out['norm_g'], 'ffn_w_in': out['ffn_w_in'], 'ffn_w_out': out['ffn_w_out'], 'pool_w': out['pool_w'], 'pool_scale': out['pool_scale'], 'fox_w_in': out['fox_w_in'], 'fox_b_f': out['fox_b_f'], 'fox_q_gain': out['fox_q_gain'], 'fox_k_gain': out['fox_k_gain'], 'fox_w_o': out['fox_w_o'], 's5_lam_re': out['s5_lam_re'], 's5_lam_im': out['s5_lam_im'], 's5_log_dt': out['s5_log_dt'], 's5_b_re': out['s5_b_re'], 's5_b_im': out['s5_b_im'], 's5_c_re': out['s5_c_re'], 's5_c_im': out['s5_c_im'], 's5_d': out['s5_d'], 's5_w_glu': out['s5_w_glu'], 'conv_w_in': out['conv_w_in'], 'conv_w': out['conv_w'], 'conv_w_out': out['conv_w_out'], 'loss_target': out['loss_target'], 'm_ada_w': out['m_ada_w'], 'm_ada_b': out['m_ada_b'], 'm_norm_g': out['m_norm_g'], 'm_ffn_w_in': out['m_ffn_w_in'], 'm_ffn_w_out': out['m_ffn_w_out'], 'm_pool_w': out['m_pool_w'], 'm_pool_scale': out['m_pool_scale'], 'm_fox_w_in': out['m_fox_w_in'], 'm_fox_b_f': out['m_fox_b_f'], 'm_fox_q_gain': out['m_fox_q_gain'], 'm_fox_k_gain': out['m_fox_k_gain'], 'm_fox_w_o': out['m_fox_w_o'], 'm_s5_lam_re': out['m_s5_lam_re'], 'm_s5_lam_im': out['m_s5_lam_im'], 'm_s5_log_dt': out['m_s5_log_dt'], 'm_s5_b_re': out['m_s5_b_re'], 'm_s5_b_im': out['m_s5_b_im'], 'm_s5_c_re': out['m_s5_c_re'], 'm_s5_c_im': out['m_s5_c_im'], 'm_s5_d': out['m_s5_d'], 'm_s5_w_glu': out['m_s5_w_glu'], 'm_conv_w_in': out['m_conv_w_in'], 'm_conv_w': out['m_conv_w'], 'm_conv_w_out': out['m_conv_w_out'], 'v_ada_w': out['v_ada_w'], 'v_ada_b': out['v_ada_b'], 'v_norm_g': out['v_norm_g'], 'v_ffn_w_in': out['v_ffn_w_in'], 'v_ffn_w_out': out['v_ffn_w_out'], 'v_pool_w': out['v_pool_w'], 'v_pool_scale': out['v_pool_scale'], 'v_fox_w_in': out['v_fox_w_in'], 'v_fox_b_f': out['v_fox_b_f'], 'v_fox_q_gain': out['v_fox_q_gain'], 'v_fox_k_gain': out['v_fox_k_gain'], 'v_fox_w_o': out['v_fox_w_o'], 'v_s5_lam_re': out['v_s5_lam_re'], 'v_s5_lam_im': out['v_s5_lam_im'], 'v_s5_log_dt': out['v_s5_log_dt'], 'v_s5_b_re': out['v_s5_b_re'], 'v_s5_b_im': out['v_s5_b_im'], 'v_s5_c_re': out['v_s5_c_re'], 'v_s5_c_im': out['v_s5_c_im'], 'v_s5_d': out['v_s5_d'], 'v_s5_w_glu': out['v_s5_w_glu'], 'v_conv_w_in': out['v_conv_w_in'], 'v_conv_w': out['v_conv_w'], 'v_conv_w_out': out['v_conv_w_out']}


def _loss(weights, diff, rest, loss_target):
    with _jax.named_scope("forward"):
        args = {**rest, TWIN_DIFF_INPUT: diff, **{k: w.astype(_WEIGHT_DTYPES[k]) for k, w in weights.items()}}
        y = _forward(args)
    with _jax.named_scope("loss_head"):
        err = _jnp.square(y.astype(_jnp.float32) - loss_target)
        return 0.5 * _jnp.sum(_jnp.mean(err, axis=-1)) if err.ndim else 0.5 * err


def _adamw(w, g, m, v):
    m = ADAM_B1 * m + (1.0 - ADAM_B1) * g
    v = ADAM_B2 * v + (1.0 - ADAM_B2) * _jnp.square(g)
    m_hat = m / (1.0 - ADAM_B1 ** ADAM_STEP)
    v_hat = v / (1.0 - ADAM_B2 ** ADAM_STEP)
    delta = -ADAM_LR * (m_hat / (_jnp.sqrt(v_hat) + ADAM_EPS) + ADAM_WD * w)
    return delta, m, v


def reference(x, c, ada_w, ada_b, norm_g, ffn_w_in, ffn_w_out, pool_w, pool_scale, fox_w_in, fox_b_f, fox_q_gain, fox_k_gain, fox_w_o, s5_lam_re, s5_lam_im, s5_log_dt, s5_b_re, s5_b_im, s5_c_re, s5_c_im, s5_d, s5_w_glu, conv_w_in, conv_w, conv_w_out, loss_target, m_ada_w, m_ada_b, m_norm_g, m_ffn_w_in, m_ffn_w_out, m_pool_w, m_pool_scale, m_fox_w_in, m_fox_b_f, m_fox_q_gain, m_fox_k_gain, m_fox_w_o, m_s5_lam_re, m_s5_lam_im, m_s5_log_dt, m_s5_b_re, m_s5_b_im, m_s5_c_re, m_s5_c_im, m_s5_d, m_s5_w_glu, m_conv_w_in, m_conv_w, m_conv_w_out, v_ada_w, v_ada_b, v_norm_g, v_ffn_w_in, v_ffn_w_out, v_pool_w, v_pool_scale, v_fox_w_in, v_fox_b_f, v_fox_q_gain, v_fox_k_gain, v_fox_w_o, v_s5_lam_re, v_s5_lam_im, v_s5_log_dt, v_s5_b_re, v_s5_b_im, v_s5_c_re, v_s5_c_im, v_s5_d, v_s5_w_glu, v_conv_w_in, v_conv_w, v_conv_w_out):
    given = dict(x=x, c=c, ada_w=ada_w, ada_b=ada_b, norm_g=norm_g, ffn_w_in=ffn_w_in, ffn_w_out=ffn_w_out, pool_w=pool_w, pool_scale=pool_scale, fox_w_in=fox_w_in, fox_b_f=fox_b_f, fox_q_gain=fox_q_gain, fox_k_gain=fox_k_gain, fox_w_o=fox_w_o, s5_lam_re=s5_lam_re, s5_lam_im=s5_lam_im, s5_log_dt=s5_log_dt, s5_b_re=s5_b_re, s5_b_im=s5_b_im, s5_c_re=s5_c_re, s5_c_im=s5_c_im, s5_d=s5_d, s5_w_glu=s5_w_glu, conv_w_in=conv_w_in, conv_w=conv_w, conv_w_out=conv_w_out, loss_target=loss_target, m_ada_w=m_ada_w, m_ada_b=m_ada_b, m_norm_g=m_norm_g, m_ffn_w_in=m_ffn_w_in, m_ffn_w_out=m_ffn_w_out, m_pool_w=m_pool_w, m_pool_scale=m_pool_scale, m_fox_w_in=m_fox_w_in, m_fox_b_f=m_fox_b_f, m_fox_q_gain=m_fox_q_gain, m_fox_k_gain=m_fox_k_gain, m_fox_w_o=m_fox_w_o, m_s5_lam_re=m_s5_lam_re, m_s5_lam_im=m_s5_lam_im, m_s5_log_dt=m_s5_log_dt, m_s5_b_re=m_s5_b_re, m_s5_b_im=m_s5_b_im, m_s5_c_re=m_s5_c_re, m_s5_c_im=m_s5_c_im, m_s5_d=m_s5_d, m_s5_w_glu=m_s5_w_glu, m_conv_w_in=m_conv_w_in, m_conv_w=m_conv_w, m_conv_w_out=m_conv_w_out, v_ada_w=v_ada_w, v_ada_b=v_ada_b, v_norm_g=v_norm_g, v_ffn_w_in=v_ffn_w_in, v_ffn_w_out=v_ffn_w_out, v_pool_w=v_pool_w, v_pool_scale=v_pool_scale, v_fox_w_in=v_fox_w_in, v_fox_b_f=v_fox_b_f, v_fox_q_gain=v_fox_q_gain, v_fox_k_gain=v_fox_k_gain, v_fox_w_o=v_fox_w_o, v_s5_lam_re=v_s5_lam_re, v_s5_lam_im=v_s5_lam_im, v_s5_log_dt=v_s5_log_dt, v_s5_b_re=v_s5_b_re, v_s5_b_im=v_s5_b_im, v_s5_c_re=v_s5_c_re, v_s5_c_im=v_s5_c_im, v_s5_d=v_s5_d, v_s5_w_glu=v_s5_w_glu, v_conv_w_in=v_conv_w_in, v_conv_w=v_conv_w, v_conv_w_out=v_conv_w_out)
    weights = {n: given[n] for n in TWIN_WEIGHTS}
    shared = {n: given[n] for n in SHARED_INPUTS}
    per_example = {n: given[n] for n in ['x', 'c']}
    grad_fn = _jax.value_and_grad(_loss, argnums=(0, 1))

    def one_microbatch(ex, loss_target):
        ex = dict(ex)
        diff = ex.pop(TWIN_DIFF_INPUT)
        return grad_fn(weights, diff, {**shared, **ex}, loss_target)

    if N_MICROBATCH == 1:
        loss, (grad_w, grad_x) = one_microbatch(per_example, given["loss_target"])
    else:
        def body(carry, xs):
            loss_sum, grad_sum = carry
            l_k, (gw_k, gx_k) = one_microbatch(xs[0], xs[1])
            with _jax.named_scope("update"):
                return (loss_sum + l_k, _jax.tree.map(_jnp.add, grad_sum, gw_k)), gx_k

        init = (_jnp.zeros((), _jnp.float32), _jax.tree.map(_jnp.zeros_like, weights))
        (loss, grad_w), grad_x = _jax.lax.scan(body, init, (per_example, given["loss_target"]))
    with _jax.named_scope("update"):
        delta_w, new_m, new_v = {}, {}, {}
        for n in TWIN_WEIGHTS:
            delta_w[n], new_m[n], new_v[n] = _adamw(weights[n], grad_w[n], given["m_" + n], given["v_" + n])
    return (loss, grad_x, *[grad_w[n] for n in TWIN_WEIGHTS], *[delta_w[n] for n in TWIN_WEIGHTS],
            *[new_m[n] for n in TWIN_WEIGHTS], *[new_v[n] for n in TWIN_WEIGHTS])
```

```python
import math

import jax
import jax.numpy as jnp
from jax import lax
from jax.experimental import pallas as pl
from jax.experimental.pallas import tpu as pltpu

f32 = jnp.float32
bf16 = jnp.bfloat16

D = 1024
D_FF = 2816
DEPTH = 4
NORM_EPS = 1e-6
N_DEV = 8
AXES = ("x", "y", "c")
POOL_WINDOWS = (2, 4, 8, 16)
POOL_GROUP = 256
POOL_HALO = 16
FOX_HEADS = 16
FOX_HEAD_DIM = 64
FOX_PROJ = 3088
FOX_PROJ_PAD = 3200
S5_GROUPS = 64
S5_GROUP = 16
S5_STATE = 64
S5_NSTATE = S5_GROUPS * S5_STATE
S5_BLOCKS = 8
S5_BCH = 128
S5_BST = 512
CONV_HALO = 8
ADAM_LR = 0.001
ADAM_B1 = 0.9
ADAM_B2 = 0.999
ADAM_EPS = 1e-08
ADAM_WD = 0.01
ADAM_STEP = 10
VMEM_LIMIT = 56 * 1024 * 1024
PACK_C = 1024

_ARB = "arbitrary"


def _cp(n_axes):
    return pltpu.CompilerParams(dimension_semantics=(_ARB,) * n_axes, vmem_limit_bytes=VMEM_LIMIT)


def _pick(n, prefs):
    for c in prefs:
        if n % c == 0:
            return c
    return n


_DN = {"nn": (((1,), (0,)), ((), ())), "nt": (((1,), (1,)), ((), ())), "tn": (((0,), (0,)), ((), ()))}


def _mm(a, b, mode, out_dtype, name):
    if mode == "nn":
        (m, k), (_, n) = a.shape, b.shape
    elif mode == "nt":
        (m, k), (n, _) = a.shape, b.shape
    else:
        (k, m), (_, n) = a.shape, b.shape
    big = (1408, 1024, 640, 512, 384, 256, 128)
    tm = _pick(m, big) if mode == "tn" else _pick(m, (1024, 512, 256, 128))
    tn = _pick(n, big)
    if mode == "tn":
        tk = _pick(k, (512, 256, 128))
    else:
        tk = k if k <= 3200 else _pick(k, (2816, 2048, 1024, 512))
    nk = k // tk
    dn = _DN[mode]

    def body(a_ref, b_ref, o_ref, acc_ref):
        p = lax.dot_general(a_ref[...], b_ref[...], dn, preferred_element_type=f32)
        if nk == 1:
            o_ref[...] = p.astype(out_dtype)
        else:
            kk = pl.program_id(2)

            @pl.when(kk == 0)
            def _():
                acc_ref[...] = p

            @pl.when(kk > 0)
            def _():
                acc_ref[...] += p

            @pl.when(kk == nk - 1)
            def _():
                o_ref[...] = acc_ref[...].astype(out_dtype)

    if mode == "nn":
        a_spec = pl.BlockSpec((tm, tk), lambda i, j, kk: (i, kk))
        b_spec = pl.BlockSpec((tk, tn), lambda i, j, kk: (kk, j))
    elif mode == "nt":
        a_spec = pl.BlockSpec((tm, tk), lambda i, j, kk: (i, kk))
        b_spec = pl.BlockSpec((tn, tk), lambda i, j, kk: (j, kk))
    else:
        a_spec = pl.BlockSpec((tk, tm), lambda i, j, kk: (kk, i))
        b_spec = pl.BlockSpec((tk, tn), lambda i, j, kk: (kk, j))
    acc_shape = (tm, tn) if nk > 1 else (8, 128)
    return pl.pallas_call(
        body, name=name, grid=(m // tm, n // tn, nk),
        in_specs=[a_spec, b_spec], out_specs=pl.BlockSpec((tm, tn), lambda i, j, kk: (i, j)),
        out_shape=jax.ShapeDtypeStruct((m, n), out_dtype),
        scratch_shapes=[pltpu.VMEM(acc_shape, f32)],
        compiler_params=_cp(3),
    )(a, b)


def _tok_tile(t):
    return min(t, 512)


def _tok_spec(tm, c, nt=None, reverse=False):
    if reverse:
        return pl.BlockSpec((tm, c), lambda i: (nt - 1 - i, 0))
    return pl.BlockSpec((tm, c), lambda i: (i, 0))


def _row_spec(c, rows=1):
    return pl.BlockSpec((rows, c), lambda i: (0, 0))


def _acc_add(first, ref, val):
    @pl.when(first)
    def _():
        ref[...] = val

    @pl.when(jnp.logical_not(first))
    def _():
        ref[...] += val


def _adaln(x, g, scale, shift):
    y = x * lax.rsqrt(jnp.mean(x * x, axis=-1, keepdims=True) + NORM_EPS)
    return (y * g) * (1.0 + scale) + shift


def _ln_fwd(x, g, scale, shift):
    t = x.shape[0]
    tm = _tok_tile(t)

    def body(x_ref, g_ref, sc_ref, sh_ref, h_ref):
        h_ref[...] = _adaln(x_ref[...], g_ref[...], sc_ref[...], sh_ref[...]).astype(bf16)

    return pl.pallas_call(
        body, name="ln_fwd", grid=(t // tm,),
        in_specs=[_tok_spec(tm, D), _row_spec(D), _row_spec(D), _row_spec(D)],
        out_specs=_tok_spec(tm, D), out_shape=jax.ShapeDtypeStruct((t, D), bf16),
        compiler_params=_cp(1),
    )(x, g, scale, shift)


def _ln_bwd(x, g, scale, shift, dh, dxn):
    t = x.shape[0]
    tm = _tok_tile(t)

    def body(x_ref, g_ref, sc_ref, sh_ref, dh_ref, dxn_ref, dx_ref, dg_ref, dsc_ref, dsh_ref):
        _, vjp = jax.vjp(_adaln, x_ref[...], g_ref[...], sc_ref[...], sh_ref[...])
        dx, dg, dsc, dsh = vjp(dh_ref[...])
        dx_ref[...] = dxn_ref[...] + dx
        first = pl.program_id(0) == 0
        _acc_add(first, dg_ref, dg)
        _acc_add(first, dsc_ref, dsc)
        _acc_add(first, dsh_ref, dsh)

    row = jax.ShapeDtypeStruct((1, D), f32)
    return pl.pallas_call(
        body, name="ln_bwd", grid=(t // tm,),
        in_specs=[_tok_spec(tm, D), _row_spec(D), _row_spec(D), _row_spec(D), _tok_spec(tm, D), _tok_spec(tm, D)],
        out_specs=[_tok_spec(tm, D), _row_spec(D), _row_spec(D), _row_spec(D)],
        out_shape=[jax.ShapeDtypeStruct((t, D), f32), row, row, row],
        compiler_params=_cp(1),
    )(x, g, scale, shift, dh, dxn)


def _swiglu_fwd(gu):
    t = gu.shape[0]
    tm = _tok_tile(t)

    def body(g_ref, u_ref, a_ref):
        a_ref[...] = (jax.nn.silu(g_ref[...]) * u_ref[...]).astype(bf16)

    return pl.pallas_call(
        body, name="swiglu_fwd", grid=(t // tm,),
        in_specs=[pl.BlockSpec((tm, D_FF), lambda i: (i, 0)), pl.BlockSpec((tm, D_FF), lambda i: (i, 1))],
        out_specs=_tok_spec(tm, D_FF), out_shape=jax.ShapeDtypeStruct((t, D_FF), bf16),
        compiler_params=_cp(1),
    )(gu, gu)


def _swiglu_bwd(gu, da):
    t = gu.shape[0]
    tm = _tok_tile(t)

    def body(g_ref, u_ref, da_ref, dg_ref, du_ref):
        _, vjp = jax.vjp(lambda g, u: jax.nn.silu(g) * u, g_ref[...], u_ref[...])
        dg, du = vjp(da_ref[...])
        dg_ref[...] = dg.astype(bf16)
        du_ref[...] = du.astype(bf16)

    half = jax.ShapeDtypeStruct((t, D_FF), bf16)
    dg, du = pl.pallas_call(
        body, name="swiglu_bwd", grid=(t // tm,),
        in_specs=[pl.BlockSpec((tm, D_FF), lambda i: (i, 0)), pl.BlockSpec((tm, D_FF), lambda i: (i, 1)),
                  _tok_spec(tm, D_FF)],
        out_specs=[_tok_spec(tm, D_FF), _tok_spec(tm, D_FF)], out_shape=[half, half],
        compiler_params=_cp(1),
    )(gu, gu, da)
    return jnp.concatenate([dg, du], axis=1)


def _resid_fwd(x, o, gate, coef):
    t = x.shape[0]
    tm = _tok_tile(t)

    def body(x_ref, o_ref, gt_ref, y_ref):
        y_ref[...] = x_ref[...] + (coef * (1.0 + gt_ref[...])) * o_ref[...]

    return pl.pallas_call(
        body, name="resid_fwd", grid=(t // tm,),
        in_specs=[_tok_spec(tm, D), _tok_spec(tm, D), _row_spec(D)],
        out_specs=_tok_spec(tm, D), out_shape=jax.ShapeDtypeStruct((t, D), f32),
        compiler_params=_cp(1),
    )(x, o, gate)


def _resid_bwd(dxn, o, gate, coef):
    t = dxn.shape[0]
    tm = _tok_tile(t)

    def body(dxn_ref, o_ref, gt_ref, do_ref, dgt_ref):
        dxn_v = dxn_ref[...]
        do_ref[...] = ((coef * (1.0 + gt_ref[...])) * dxn_v).astype(bf16)
        _acc_add(pl.program_id(0) == 0, dgt_ref, coef * jnp.sum(dxn_v * o_ref[...], axis=0, keepdims=True))

    return pl.pallas_call(
        body, name="resid_bwd", grid=(t // tm,),
        in_specs=[_tok_spec(tm, D), _tok_spec(tm, D), _row_spec(D)],
        out_specs=[_tok_spec(tm, D), _row_spec(D)],
        out_shape=[jax.ShapeDtypeStruct((t, D), bf16), jax.ShapeDtypeStruct((1, D), f32)],
        compiler_params=_cp(1),
    )(dxn, o, gate)


def _loss_head(y, target):
    t = y.shape[0]
    tm = _tok_tile(t)

    def body(y_ref, t_ref, dy_ref, l_ref):
        err = y_ref[...] - t_ref[...]
        dy_ref[...] = err * (1.0 / D)
        part = jnp.sum(jnp.sum(err * err, axis=0, keepdims=True), axis=1, keepdims=True) * (0.5 / D)
        _acc_add(pl.program_id(0) == 0, l_ref, jnp.broadcast_to(part, (1, 128)))

    return pl.pallas_call(
        body, name="loss_head", grid=(t // tm,),
        in_specs=[_tok_spec(tm, D), _tok_spec(tm, D)],
        out_specs=[_tok_spec(tm, D), _row_spec(128)],
        out_shape=[jax.ShapeDtypeStruct((t, D), f32), jax.ShapeDtypeStruct((1, 128), f32)],
        compiler_params=_cp(1),
    )(y, target)


def _ffn_fwd(x, ln, w_in, w_out, coef):
    g, shift, scale, gate = ln
    h = _ln_fwd(x, g, scale, shift)
    gu = _mm(h, w_in, "nn", f32, "ffn_in")
    a = _swiglu_fwd(gu)
    o = _mm(a, w_out, "nn", f32, "ffn_out")
    xn = _resid_fwd(x, o, gate, coef)
    return xn, (x, h, gu, a, o)


def _ffn_bwd(dxn, saved, ln, w_in, w_out, coef):
    x, h, gu, a, o = saved
    g, shift, scale, gate = ln
    do, dgate = _resid_bwd(dxn, o, gate, coef)
    da = _mm(do, w_out, "nt", f32, "ffn_dact")
    dw_out = _mm(a, do, "tn", f32, "ffn_dwout")
    dgu = _swiglu_bwd(gu, da)
    dh = _mm(dgu, w_in, "nt", f32, "ffn_dh")
    dw_in = _mm(h, dgu, "tn", f32, "ffn_dwin")
    dx, dg, dscale, dshift = _ln_bwd(x, g, scale, shift, dh, dxn)
    return dx, (dg, dshift, dscale, dgate), dw_in, dw_out


def _roll_rows(a, k):
    n = a.shape[0]
    return pltpu.roll(a, k % n, 0)


def _pool_windows(hx, first_row, reverse):
    outs = []
    for gi, w in enumerate(POOL_WINDOWS):
        acc = hx[:, gi * POOL_GROUP:(gi + 1) * POOL_GROUP]
        k = 1
        while k < w:
            acc = acc + _roll_rows(acc, -k if reverse else k)
            k *= 2
        outs.append(acc)
    return outs


def _pool_cnt(t_idx, w):
    return jnp.minimum(t_idx + 1, w).astype(f32)


def _pool_pooled(x_ref, xp_ref, g, scale, shift, i, tm):
    h = _adaln(x_ref[...], g, scale, shift)
    hp = _adaln(xp_ref[...], g, scale, shift)
    hp = jnp.where(i == 0, 0.0, hp)
    hx = jnp.concatenate([hp, h], axis=0)
    sums = _pool_windows(hx, 0, False)
    t_idx = i * tm + lax.broadcasted_iota(jnp.int32, (tm, 1), 0)
    pooled = []
    for gi, w in enumerate(POOL_WINDOWS):
        s = sums[gi][POOL_HALO:, :]
        pooled.append(s / _pool_cnt(t_idx, w) - h[:, gi * POOL_GROUP:(gi + 1) * POOL_GROUP])
    return h, pooled


def _pool_specs(t, tm):
    per = tm // POOL_HALO
    prev = pl.BlockSpec((POOL_HALO, D), lambda i: (jnp.maximum(i * per - 1, 0), 0))
    return [_tok_spec(tm, D), prev, _row_spec(D), _row_spec(D), _row_spec(D),
            pl.BlockSpec((4, POOL_GROUP, POOL_GROUP), lambda i: (0, 0, 0)), _row_spec(D), _row_spec(D)]


def _pool_fwd(x, ln, w, pscale):
    g, shift, scale, gate = ln
    t = x.shape[0]
    tm = _tok_tile(t)

    def body(x_ref, xp_ref, g_ref, sc_ref, sh_ref, w_ref, ps_ref, gt_ref, y_ref):
        i = pl.program_id(0)
        _, pooled = _pool_pooled(x_ref, xp_ref, g_ref[...], sc_ref[...], sh_ref[...], i, tm)
        mixed = [jnp.dot(pooled[gi].astype(bf16), w_ref[gi], preferred_element_type=f32) for gi in range(4)]
        y = jnp.concatenate(mixed, axis=1) * ps_ref[...]
        y_ref[...] = x_ref[...] + (1.0 + gt_ref[...]) * y

    return pl.pallas_call(
        body, name="pool_fwd", grid=(t // tm,), in_specs=_pool_specs(t, tm),
        out_specs=_tok_spec(tm, D), out_shape=jax.ShapeDtypeStruct((t, D), f32),
        compiler_params=_cp(1),
    )(x, x, g, scale, shift, w, pscale, gate)


def _pool_bwd(dxn, x, ln, w, pscale):
    g, shift, scale, gate = ln
    t = x.shape[0]
    tm = _tok_tile(t)
    nt = t // tm
    per = tm // POOL_HALO

    def body_a(x_ref, xp_ref, g_ref, sc_ref, sh_ref, w_ref, ps_ref, gt_ref, dxn_ref,
               dp_ref, dw_ref, dps_ref, dgt_ref):
        i = pl.program_id(0)
        first = i == 0
        _, pooled = _pool_pooled(x_ref, xp_ref, g_ref[...], sc_ref[...], sh_ref[...], i, tm)
        dxn_v = dxn_ref[...]
        dy = (1.0 + gt_ref[...]) * dxn_v
        dmixed = dy * ps_ref[...]
        mixed, dps = [], []
        for gi in range(4):
            sl = slice(gi * POOL_GROUP, (gi + 1) * POOL_GROUP)
            pb = pooled[gi].astype(bf16)
            dmb = dmixed[:, sl].astype(bf16)
            mixed.append(jnp.dot(pb, w_ref[gi], preferred_element_type=f32))
            dp_ref[:, sl] = lax.dot_general(dmb, w_ref[gi], _DN["nt"], preferred_element_type=f32)
            dwg = lax.dot_general(pb, dmb, _DN["tn"], preferred_element_type=f32)

            @pl.when(first)
            def _():
                dw_ref[gi] = dwg

            @pl.when(jnp.logical_not(first))
            def _():
                dw_ref[gi] += dwg
        mixed = jnp.concatenate(mixed, axis=1)
        _acc_add(first, dps_ref, jnp.sum(dy * mixed, axis=0, keepdims=True))
        _acc_add(first, dgt_ref, jnp.sum(dxn_v * (mixed * ps_ref[...]), axis=0, keepdims=True))

    row = jax.ShapeDtypeStruct((1, D), f32)
    dpooled, dw, dps, dgate = pl.pallas_call(
        body_a, name="pool_bwd_a", grid=(nt,), in_specs=_pool_specs(t, tm) + [_tok_spec(tm, D)],
        out_specs=[_tok_spec(tm, D), pl.BlockSpec((4, POOL_GROUP, POOL_GROUP), lambda i: (0, 0, 0)),
                   _row_spec(D), _row_spec(D)],
        out_shape=[jax.ShapeDtypeStruct((t, D), f32), jax.ShapeDtypeStruct((4, POOL_GROUP, POOL_GROUP), f32), row, row],
        compiler_params=_cp(1),
    )(x, x, g, scale, shift, w, pscale, gate, dxn)

    def body_b(dp_ref, dpn_ref, x_ref, g_ref, sc_ref, sh_ref, dxn_ref, dx_ref, dg_ref, dsc_ref, dsh_ref):
        i = pl.program_id(0)
        dp = dp_ref[...]
        dpn = jnp.where(i == nt - 1, 0.0, dpn_ref[...])
        t_idx = i * tm + lax.broadcasted_iota(jnp.int32, (tm + POOL_HALO, 1), 0)
        ex = jnp.concatenate([dp, dpn], axis=0)
        parts = []
        for gi, w_ in enumerate(POOL_WINDOWS):
            parts.append(ex[:, gi * POOL_GROUP:(gi + 1) * POOL_GROUP] / _pool_cnt(t_idx, w_))
        sums = _pool_windows(jnp.concatenate(parts, axis=1), 0, True)
        dh = jnp.concatenate([s[:tm, :] for s in sums], axis=1) - dp
        _, vjp = jax.vjp(_adaln, x_ref[...], g_ref[...], sc_ref[...], sh_ref[...])
        dx, dg, dsc, dsh = vjp(dh)
        dx_ref[...] = dxn_ref[...] + dx
        first = i == 0
        _acc_add(first, dg_ref, dg)
        _acc_add(first, dsc_ref, dsc)
        _acc_add(first, dsh_ref, dsh)

    nxt = pl.BlockSpec((POOL_HALO, D), lambda i: (jnp.minimum((i + 1) * per, t // POOL_HALO - 1), 0))
    dx, dg, dscale, dshift = pl.pallas_call(
        body_b, name="pool_bwd_b", grid=(nt,),
        in_specs=[_tok_spec(tm, D), nxt, _tok_spec(tm, D), _row_spec(D), _row_spec(D), _row_spec(D), _tok_spec(tm, D)],
        out_specs=[_tok_spec(tm, D), _row_spec(D), _row_spec(D), _row_spec(D)],
        out_shape=[jax.ShapeDtypeStruct((t, D), f32), row, row, row],
        compiler_params=_cp(1),
    )(dpooled, dpooled, x, g, scale, shift, dxn)
    return dx, (dg, dshift, dscale, dgate), dw, dps


def _conv_taps(czx, cw):
    return cw[0:1, :] * _roll_rows(czx, 2) + cw[1:2, :] * _roll_rows(czx, 1) + cw[2:3, :] * czx


def _conv_fwd(p, cw):
    t = p.shape[0]
    tm = _tok_tile(t)
    per = tm // CONV_HALO

    def body(p_ref, pp_ref, cw_ref, q_ref):
        i = pl.program_id(0)
        cz = p_ref[:, D:2 * D] * p_ref[:, 2 * D:3 * D]
        czp = jnp.where(i == 0, 0.0, pp_ref[:, D:2 * D] * pp_ref[:, 2 * D:3 * D])
        conv = _conv_taps(jnp.concatenate([czp, cz], axis=0), cw_ref[...])[CONV_HALO:, :]
        q_ref[...] = (p_ref[:, 0:D] * conv).astype(bf16)

    prev = pl.BlockSpec((CONV_HALO, 3 * D), lambda i: (jnp.maximum(i * per - 1, 0), 0))
    return pl.pallas_call(
        body, name="conv_fwd", grid=(t // tm,),
        in_specs=[_tok_spec(tm, 3 * D), prev, _row_spec(D, 8)],
        out_specs=_tok_spec(tm, D), out_shape=jax.ShapeDtypeStruct((t, D), bf16),
        compiler_params=_cp(1),
    )(p, p, cw)


def _conv_bwd(p, cw, dq):
    t = p.shape[0]
    tm = _tok_tile(t)
    nt = t // tm
    per = tm // CONV_HALO

    def body(p_ref, pp_ref, pn_ref, cw_ref, dq_ref, dqn_ref, dp_ref, dcw_ref):
        i = pl.program_id(0)
        cw_v = cw_ref[...]
        b, c, z = p_ref[:, 0:D], p_ref[:, D:2 * D], p_ref[:, 2 * D:3 * D]
        cz = c * z
        czp = jnp.where(i == 0, 0.0, pp_ref[:, D:2 * D] * pp_ref[:, 2 * D:3 * D])
        czx = jnp.concatenate([czp, cz], axis=0)
        conv = _conv_taps(czx, cw_v)[CONV_HALO:, :]
        dq_v = dq_ref[...]
        dconv = dq_v * b
        dconv_n = jnp.where(i == nt - 1, 0.0, dqn_ref[...] * pn_ref[:, 0:D])
        dcx = jnp.concatenate([dconv, dconv_n], axis=0)
        dcz = (cw_v[2:3, :] * dcx + cw_v[1:2, :] * _roll_rows(dcx, -1) + cw_v[0:1, :] * _roll_rows(dcx, -2))[:tm, :]
        dp_ref[:, 0:D] = (dq_v * conv).astype(bf16)
        dp_ref[:, D:2 * D] = (dcz * z).astype(bf16)
        dp_ref[:, 2 * D:3 * D] = (dcz * c).astype(bf16)
        dw2 = jnp.sum(dconv * cz, axis=0, keepdims=True)
        dw1 = jnp.sum(dconv * _roll_rows(czx, 1)[CONV_HALO:, :], axis=0, keepdims=True)
        dw0 = jnp.sum(dconv * _roll_rows(czx, 2)[CONV_HALO:, :], axis=0, keepdims=True)
        _acc_add(i == 0, dcw_ref, jnp.concatenate([dw0, dw1, dw2, jnp.zeros((5, D), f32)], axis=0))

    prev = pl.BlockSpec((CONV_HALO, 3 * D), lambda i: (jnp.maximum(i * per - 1, 0), 0))
    last = t // CONV_HALO - 1
    nxt3 = pl.BlockSpec((CONV_HALO, 3 * D), lambda i: (jnp.minimum((i + 1) * per, last), 0))
    nxt1 = pl.BlockSpec((CONV_HALO, D), lambda i: (jnp.minimum((i + 1) * per, last), 0))
    return pl.pallas_call(
        body, name="conv_bwd", grid=(nt,),
        in_specs=[_tok_spec(tm, 3 * D), prev, nxt3, _row_spec(D, 8), _tok_spec(tm, D), nxt1],
        out_specs=[_tok_spec(tm, 3 * D), _row_spec(D, 8)],
        out_shape=[jax.ShapeDtypeStruct((t, 3 * D), bf16), jax.ShapeDtypeStruct((8, D), f32)],
        compiler_params=_cp(1),
    )(p, p, p, cw, dq, dq)


def _convmix_fwd(x, ln, w_in, cw, w_out):
    g, shift, scale, gate = ln
    h = _ln_fwd(x, g, scale, shift)
    p = _mm(h, w_in, "nn", f32, "conv_in")
    q = _conv_fwd(p, cw)
    y = _mm(q, w_out, "nn", f32, "conv_out")
    return _resid_fwd(x, y, gate, 1.0), (x, h, p, q, y)


def _convmix_bwd(dxn, saved, ln, w_in, cw, w_out):
    x, h, p, q, y = saved
    g, shift, scale, gate = ln
    dy, dgate = _resid_bwd(dxn, y, gate, 1.0)
    dq = _mm(dy, w_out, "nt", f32, "conv_dq")
    dw_out = _mm(q, dy, "tn", f32, "conv_dwout")
    dp, dcw = _conv_bwd(p, cw, dq)
    dh = _mm(dp, w_in, "nt", f32, "conv_dh")
    dw_in = _mm(h, dp, "tn", f32, "conv_dwin")
    dx, dg, dscale, dshift = _ln_bwd(x, g, scale, shift, dh, dxn)
    return dx, (dg, dshift, dscale, dgate), dw_in, dcw[0:3], dw_out


def _exact_dot(tri, v):
    v1 = v.astype(bf16)
    r1 = v - v1.astype(f32)
    v2 = r1.astype(bf16)
    v3 = (r1 - v2.astype(f32)).astype(bf16)
    d = lambda p: jnp.dot(tri, p, preferred_element_type=f32)
    return d(v1) + d(v2) + d(v3)


def _fox_cumf(fl, b_f):
    t = fl.shape[0]
    tc = min(t, 256)

    def body(fl_ref, b_ref, f_ref, carry_ref):
        i = pl.program_id(0)

        @pl.when(i == 0)
        def _():
            carry_ref[...] = jnp.zeros_like(carry_ref)

        lf = jax.nn.log_sigmoid(fl_ref[...] + b_ref[...])
        r = lax.broadcasted_iota(jnp.int32, (tc, tc), 0)
        c = lax.broadcasted_iota(jnp.int32, (tc, tc), 1)
        tri = (r >= c).astype(bf16)
        fc = _exact_dot(tri, lf) + carry_ref[0:1, :]
        f_ref[...] = fc
        carry_ref[0:1, :] = fc[tc - 1:tc, :]

    return pl.pallas_call(
        body, name="fox_cumf", grid=(t // tc,),
        in_specs=[_tok_spec(tc, 128), _row_spec(128)],
        out_specs=_tok_spec(tc, 128), out_shape=jax.ShapeDtypeStruct((t, 128), f32),
        scratch_shapes=[pltpu.VMEM((8, 128), f32)], compiler_params=_cp(1),
    )(fl, b_f)


def _fox_cumf_bwd(df, fl, b_f):
    t = fl.shape[0]
    tc = min(t, 256)
    nt = t // tc

    def body(df_ref, fl_ref, b_ref, dfl_ref, db_ref, carry_ref):
        i = pl.program_id(0)

        @pl.when(i == 0)
        def _():
            carry_ref[...] = jnp.zeros_like(carry_ref)

        r = lax.broadcasted_iota(jnp.int32, (tc, tc), 0)
        c = lax.broadcasted_iota(jnp.int32, (tc, tc), 1)
        tri = (r <= c).astype(bf16)
        dlf = _exact_dot(tri, df_ref[...]) + carry_ref[0:1, :]
        carry_ref[0:1, :] = dlf[0:1, :]
        dfl = dlf * jax.nn.sigmoid(-(fl_ref[...] + b_ref[...]))
        dfl_ref[...] = dfl
        _acc_add(i == 0, db_ref, jnp.sum(dfl, axis=0, keepdims=True))

    return pl.pallas_call(
        body, name="fox_cumf_bwd", grid=(nt,),
        in_specs=[_tok_spec(tc, 128, nt, True), _tok_spec(tc, 128, nt, True), _row_spec(128)],
        out_specs=[_tok_spec(tc, 128, nt, True), _row_spec(128)],
        out_shape=[jax.ShapeDtypeStruct((t, 128), f32), jax.ShapeDtypeStruct((1, 128), f32)],
        scratch_shapes=[pltpu.VMEM((8, 128), f32)], compiler_params=_cp(1),
    )(df, fl, b_f)


def _head_norm(v, gain, mult):
    return v * lax.rsqrt(jnp.mean(v * v, axis=-1, keepdims=True) + NORM_EPS) * gain * mult


def _fox_qknorm(qkv, q_gain, k_gain):
    t = qkv.shape[2]
    tm = _tok_tile(t)

    def body(q_ref, k_ref, v_ref, qg_ref, kg_ref, o_ref):
        o_ref[0, 0] = _head_norm(q_ref[0, 0], qg_ref[...], FOX_HEAD_DIM ** -0.5).astype(bf16)
        o_ref[1, 0] = _head_norm(k_ref[0, 0], kg_ref[...], 1.0).astype(bf16)
        o_ref[2, 0] = v_ref[0, 0].astype(bf16)

    blk = lambda s: pl.BlockSpec((1, 1, tm, FOX_HEAD_DIM), lambda h, i: (s, h, i, 0))
    gspec = pl.BlockSpec((1, FOX_HEAD_DIM), lambda h, i: (0, 0))
    return pl.pallas_call(
        body, name="fox_qknorm", grid=(FOX_HEADS, t // tm),
        in_specs=[blk(0), blk(1), blk(2), gspec, gspec],
        out_specs=pl.BlockSpec((3, 1, tm, FOX_HEAD_DIM), lambda h, i: (0, h, i, 0)),
        out_shape=jax.ShapeDtypeStruct((3, FOX_HEADS, t, FOX_HEAD_DIM), bf16),
        compiler_params=_cp(2),
    )(qkv, qkv, qkv, q_gain, k_gain)


def _fox_qknorm_bwd(qkv, q_gain, k_gain, dqn, dkn):
    t = qkv.shape[2]
    tm = _tok_tile(t)

    def body(q_ref, k_ref, qg_ref, kg_ref, dqn_ref, dkn_ref, o_ref, dqg_ref, dkg_ref):
        first = jnp.logical_and(pl.program_id(0) == 0, pl.program_id(1) == 0)
        _, vq = jax.vjp(lambda v, gn: _head_norm(v, gn, FOX_HEAD_DIM ** -0.5), q_ref[0, 0], qg_ref[...])
        dq, dqg = vq(dqn_ref[0])
        _, vk = jax.vjp(lambda v, gn: _head_norm(v, gn, 1.0), k_ref[0, 0], kg_ref[...])
        dk, dkg = vk(dkn_ref[0])
        o_ref[0, 0] = dq
        o_ref[1, 0] = dk
        _acc_add(first, dqg_ref, dqg)
        _acc_add(first, dkg_ref, dkg)

    blk = lambda s: pl.BlockSpec((1, 1, tm, FOX_HEAD_DIM), lambda h, i: (s, h, i, 0))
    hblk = pl.BlockSpec((1, tm, FOX_HEAD_DIM), lambda h, i: (h, i, 0))
    gspec = pl.BlockSpec((1, FOX_HEAD_DIM), lambda h, i: (0, 0))
    gshape = jax.ShapeDtypeStruct((1, FOX_HEAD_DIM), f32)
    return pl.pallas_call(
        body, name="fox_qknorm_bwd", grid=(FOX_HEADS, t // tm),
        in_specs=[blk(0), blk(1), gspec, gspec, hblk, hblk],
        out_specs=[pl.BlockSpec((2, 1, tm, FOX_HEAD_DIM), lambda h, i: (0, h, i, 0)), gspec, gspec],
        out_shape=[jax.ShapeDtypeStruct((2, FOX_HEADS, t, FOX_HEAD_DIM), f32), gshape, gshape],
        compiler_params=_cp(2),
    )(qkv, qkv, q_gain, k_gain, dqn, dkn)


def _fox_block(t):
    return min(t, 512)


def _fox_scores(q, kj, fref, fcol_j, diag):
    s = lax.dot_general(q, kj, _DN["nt"], preferred_element_type=f32) + (fref - fcol_j)
    if diag:
        tb = s.shape[0]
        r = lax.broadcasted_iota(jnp.int32, (tb, tb), 0)
        c = lax.broadcasted_iota(jnp.int32, (tb, tb), 1)
        s = jnp.where(r >= c, s, -jnp.inf)
    return s


def _fox_attn_fwd(qkvn, fcol, fref):
    t = qkvn.shape[2]
    tb = _fox_block(t)
    nq = t // tb
    dh = FOX_HEAD_DIM

    def body(q_ref, k_ref, v_ref, fc_ref, fr_ref, o_ref, lse_ref):
        i = pl.program_id(1)
        q = q_ref[0, 0]
        fref_i = fr_ref[0, pl.ds(i, 1), 0:1]

        def step(j, carry, diag):
            m, l, acc = carry
            off = pl.multiple_of(j * tb, tb)
            kj = k_ref[0, 0, pl.ds(off, tb), :]
            vj = v_ref[0, 0, pl.ds(off, tb), :]
            s = _fox_scores(q, kj, fref_i, fc_ref[0, :, pl.ds(off, tb)], diag)
            m_new = jnp.maximum(m, jnp.max(s, axis=-1, keepdims=True))
            alpha = jnp.exp(m - m_new)
            p = jnp.exp(s - m_new)
            l = alpha * l + jnp.sum(p, axis=-1, keepdims=True)
            p_hi = p.astype(bf16)
            p_lo = (p - p_hi.astype(f32)).astype(bf16)
            pv = jnp.dot(p_hi, vj, preferred_element_type=f32) + jnp.dot(p_lo, vj, preferred_element_type=f32)
            return m_new, l, alpha * acc + pv

        init = (jnp.full((tb, 1), -jnp.inf, f32), jnp.zeros((tb, 1), f32), jnp.zeros((tb, dh), f32))
        carry = lax.fori_loop(0, i, lambda j, c: step(j, c, False), init)
        m, l, acc = step(i, carry, True)
        o_ref[0] = acc / l
        lse_ref[0] = jnp.broadcast_to(m + jnp.log(l), (tb, 128))

    return pl.pallas_call(
        body, name="fox_attn_fwd", grid=(FOX_HEADS, nq),
        in_specs=[pl.BlockSpec((1, 1, tb, dh), lambda h, i: (0, h, i, 0)),
                  pl.BlockSpec((1, 1, t, dh), lambda h, i: (1, h, 0, 0)),
                  pl.BlockSpec((1, 1, t, dh), lambda h, i: (2, h, 0, 0)),
                  pl.BlockSpec((1, 1, t), lambda h, i: (h, 0, 0)),
                  pl.BlockSpec((1, nq, 128), lambda h, i: (h, 0, 0))],
        out_specs=[pl.BlockSpec((1, tb, dh), lambda h, i: (h, i, 0)),
                   pl.BlockSpec((1, tb, 128), lambda h, i: (h, i, 0))],
        out_shape=[jax.ShapeDtypeStruct((FOX_HEADS, t, dh), f32), jax.ShapeDtypeStruct((FOX_HEADS, t, 128), f32)],
        compiler_params=_cp(2),
    )(qkvn, qkvn, qkvn, fcol, fref)


def _fox_attn_bwd_q(qkvn, fcol, fref, o, lse, do):
    t = qkvn.shape[2]
    tb = _fox_block(t)
    nq = t // tb
    dh = FOX_HEAD_DIM

    def body(q_ref, k_ref, v_ref, fc_ref, fr_ref, o_ref, lse_ref, do_ref, dq_ref, dl_ref):
        i = pl.program_id(1)
        q = q_ref[0, 0]
        fref_i = fr_ref[0, pl.ds(i, 1), 0:1]
        do_v = do_ref[0]
        dob = do_v.astype(bf16)
        delta = jnp.sum(dob.astype(f32) * o_ref[0], axis=-1, keepdims=True)
        lse_i = lse_ref[0, :, 0:1]

        def step(j, dq, diag):
            off = pl.multiple_of(j * tb, tb)
            kj = k_ref[0, 0, pl.ds(off, tb), :]
            vj = v_ref[0, 0, pl.ds(off, tb), :]
            s = _fox_scores(q, kj, fref_i, fc_ref[0, :, pl.ds(off, tb)], diag)
            p = jnp.exp(s - lse_i)
            dp = lax.dot_general(dob, vj, _DN["nt"], preferred_element_type=f32)
            ds = p * (dp - delta)
            return dq + jnp.dot(ds.astype(bf16), kj, preferred_element_type=f32)

        dq = lax.fori_loop(0, i, lambda j, c: step(j, c, False), jnp.zeros((tb, dh), f32))
        dq_ref[0] = step(i, dq, True)
        dl_ref[0] = jnp.broadcast_to(delta, (tb, 128))

    hb = pl.BlockSpec((1, tb, dh), lambda h, i: (h, i, 0))
    hb128 = pl.BlockSpec((1, tb, 128), lambda h, i: (h, i, 0))
    return pl.pallas_call(
        body, name="fox_attn_bwd_q", grid=(FOX_HEADS, nq),
        in_specs=[pl.BlockSpec((1, 1, tb, dh), lambda h, i: (0, h, i, 0)),
                  pl.BlockSpec((1, 1, t, dh), lambda h, i: (1, h, 0, 0)),
                  pl.BlockSpec((1, 1, t, dh), lambda h, i: (2, h, 0, 0)),
                  pl.BlockSpec((1, 1, t), lambda h, i: (h, 0, 0)),
                  pl.BlockSpec((1, nq, 128), lambda h, i: (h, 0, 0)),
                  hb, hb128, hb],
        out_specs=[hb, hb128],
        out_shape=[jax.ShapeDtypeStruct((FOX_HEADS, t, dh), f32), jax.ShapeDtypeStruct((FOX_HEADS, t, 128), f32)],
        compiler_params=_cp(2),
    )(qkvn, qkvn, qkvn, fcol, fref, o, lse, do)


def _fox_attn_bwd_kv(qkvn, fcol, fref, lse, delta, dob):
    t = qkvn.shape[2]
    tb = _fox_block(t)
    nq = t // tb
    dh = FOX_HEAD_DIM

    def body(q_ref, k_ref, v_ref, fc_ref, fr_ref, lse_ref, dl_ref, do_ref, dk_ref, dv_ref, df_ref):
        j = pl.program_id(1)
        kj = k_ref[0, 0]
        vj = v_ref[0, 0]
        fcol_j = fc_ref[0]

        def step(i, carry, diag):
            dk, dv, df = carry
            off = pl.multiple_of(i * tb, tb)
            qi = q_ref[0, 0, pl.ds(off, tb), :]
            doi = do_ref[0, pl.ds(off, tb), :]
            s = _fox_scores(qi, kj, fr_ref[0, pl.ds(i, 1), 0:1], fcol_j, diag)
            p = jnp.exp(s - lse_ref[0, pl.ds(off, tb), 0:1])
            dv = dv + lax.dot_general(p.astype(bf16), doi, _DN["tn"], preferred_element_type=f32)
            dp = lax.dot_general(doi, vj, _DN["nt"], preferred_element_type=f32)
            ds = p * (dp - dl_ref[0, pl.ds(off, tb), 0:1])
            dk = dk + lax.dot_general(ds.astype(bf16), qi, _DN["tn"], preferred_element_type=f32)
            return dk, dv, df - jnp.sum(ds, axis=0, keepdims=True)

        init = (jnp.zeros((tb, dh), f32), jnp.zeros((tb, dh), f32), jnp.zeros((1, tb), f32))
        carry = step(j, init, True)
        dk, dv, df = lax.fori_loop(j + 1, nq, lambda i, c: step(i, c, False), carry)
        dk_ref[0] = dk
        dv_ref[0] = dv
        df_ref[0] = df

    hb = pl.BlockSpec((1, tb, dh), lambda h, j: (h, j, 0))
    full = pl.BlockSpec((1, t, dh), lambda h, j: (h, 0, 0))
    full128 = pl.BlockSpec((1, t, 128), lambda h, j: (h, 0, 0))
    hshape = jax.ShapeDtypeStruct((FOX_HEADS, t, dh), f32)
    return pl.pallas_call(
        body, name="fox_attn_bwd_kv", grid=(FOX_HEADS, nq),
        in_specs=[pl.BlockSpec((1, 1, t, dh), lambda h, j: (0, h, 0, 0)),
                  pl.BlockSpec((1, 1, tb, dh), lambda h, j: (1, h, j, 0)),
                  pl.BlockSpec((1, 1, tb, dh), lambda h, j: (2, h, j, 0)),
                  pl.BlockSpec((1, 1, tb), lambda h, j: (h, 0, j)),
                  pl.BlockSpec((1, nq, 128), lambda h, j: (h, 0, 0)),
                  full128, full128, full],
        out_specs=[hb, hb, pl.BlockSpec((1, 1, tb), lambda h, j: (h, 0, j))],
        out_shape=[hshape, hshape, jax.ShapeDtypeStruct((FOX_HEADS, 1, t), f32)],
        compiler_params=_cp(2),
    )(qkvn, qkvn, qkvn, fcol, fref, lse, delta, dob)


def _heads_of(a, n):
    t = a.shape[0]
    return a.reshape(t, n, FOX_HEADS, FOX_HEAD_DIM).transpose(1, 2, 0, 3)


def _fox_fwd(x, ln, w_in, b_f, q_gain, k_gain, w_o):
    g, shift, scale, gate = ln
    t = x.shape[0]
    tb = _fox_block(t)
    h = _ln_fwd(x, g, scale, shift)
    proj = _mm(h, w_in, "nn", f32, "fox_in")
    qkv = _heads_of(proj[:, :3 * D], 3)
    fl = proj[:, 3 * D:3 * D + 128]
    fcum = _fox_cumf(fl, b_f)
    fcol = fcum[:, :FOX_HEADS].T.reshape(FOX_HEADS, 1, t)
    fref = jnp.broadcast_to(fcol[:, 0, ::tb][:, :, None], (FOX_HEADS, t // tb, 128))
    qkvn = _fox_qknorm(qkv, q_gain, k_gain)
    o, lse = _fox_attn_fwd(qkvn, fcol, fref)
    ob = o.transpose(1, 0, 2).reshape(t, D).astype(bf16)
    y = _mm(ob, w_o, "nn", f32, "fox_out")
    return _resid_fwd(x, y, gate, 1.0), (x, h, qkv, fl, fcol, fref, qkvn, o, lse, ob, y)


def _fox_bwd(dxn, saved, ln, w_in, b_f, q_gain, k_gain, w_o):
    x, h, qkv, fl, fcol, fref, qkvn, o, lse, ob, y = saved
    g, shift, scale, gate = ln
    t = x.shape[0]
    dy, dgate = _resid_bwd(dxn, y, gate, 1.0)
    do_flat = _mm(dy, w_o, "nt", f32, "fox_do")
    dw_o = _mm(ob, dy, "tn", f32, "fox_dwo")
    do = do_flat.reshape(t, FOX_HEADS, FOX_HEAD_DIM).transpose(1, 0, 2)
    dqn, delta = _fox_attn_bwd_q(qkvn, fcol, fref, o, lse, do)
    dkn, dv, dfcol = _fox_attn_bwd_kv(qkvn, fcol, fref, lse, delta, do.astype(bf16))
    dqk, dqg, dkg = _fox_qknorm_bwd(qkv, q_gain, k_gain, dqn, dkn)
    df = jnp.pad(dfcol.reshape(FOX_HEADS, t).T, ((0, 0), (0, 128 - FOX_HEADS)))
    dfl, db_f = _fox_cumf_bwd(df, fl, b_f)
    dqkv = jnp.concatenate([dqk, dv[None]], axis=0).transpose(2, 0, 1, 3).reshape(t, 3 * D)
    dproj = jnp.concatenate([dqkv.astype(bf16), dfl.astype(bf16)], axis=1)
    dh = _mm(dproj, w_in, "nt", f32, "fox_dh")
    dw_in = _mm(h, dproj, "tn", f32, "fox_dwin")
    dx, dg, dscale, dshift = _ln_bwd(x, g, scale, shift, dh, dxn)
    return dx, (dg, dshift, dscale, dgate), dw_in, db_f, dqg, dkg, dw_o


def _s5_disc(lam_re, lam_im, log_dt, b_re, b_im):
    dt = jnp.exp(log_dt)
    mag = jnp.exp(lam_re * dt)
    lb_re, lb_im = mag * jnp.cos(lam_im * dt), mag * jnp.sin(lam_im * dt)
    den = lam_re * lam_re + lam_im * lam_im
    nr, ni = lb_re - 1.0, lb_im
    k_re = (nr * lam_re + ni * lam_im) / den
    k_im = (ni * lam_re - nr * lam_im) / den
    return lb_re, lb_im, k_re * b_re - k_im * b_im, k_re * b_im + k_im * b_re


def _s5_prep(lam_re, lam_im, log_dt, b_re, b_im):
    def body(ar_ref, ai_ref, dt_ref, br_ref, bi_ref, lr_ref, li_ref, bbr_ref, bbi_ref):
        lr, li, bbr, bbi = _s5_disc(ar_ref[...], ai_ref[...], dt_ref[...], br_ref[...], bi_ref[...])
        lr_ref[...] = lr
        li_ref[...] = li
        bbr_ref[...] = bbr
        bbi_ref[...] = bbi

    small = jax.ShapeDtypeStruct(lam_re.shape, f32)
    bigs = jax.ShapeDtypeStruct(b_re.shape, f32)
    return pl.pallas_call(body, name="s5_prep", out_shape=[small, small, bigs, bigs])(lam_re, lam_im, log_dt, b_re, b_im)


def _s5_prep_bwd(lam_re, lam_im, log_dt, b_re, b_im, dlr, dli, dbbr, dbbi):
    def body(ar_ref, ai_ref, dt_ref, br_ref, bi_ref, dlr_ref, dli_ref, dbbr_ref, dbbi_ref,
             dar_ref, dai_ref, ddt_ref, dbr_ref, dbi_ref):
        _, vjp = jax.vjp(_s5_disc, ar_ref[...], ai_ref[...], dt_ref[...], br_ref[...], bi_ref[...])
        dar, dai, ddt, dbr, dbi = vjp((dlr_ref[...], dli_ref[...], dbbr_ref[...], dbbi_ref[...]))
        dar_ref[...] = dar
        dai_ref[...] = dai
        ddt_ref[...] = jnp.broadcast_to(jnp.sum(ddt, axis=-1, keepdims=True), ddt.shape)
        dbr_ref[...] = dbr
        dbi_ref[...] = dbi

    small = jax.ShapeDtypeStruct(lam_re.shape, f32)
    bigs = jax.ShapeDtypeStruct(b_re.shape, f32)
    return pl.pallas_call(body, name="s5_prep_bwd", out_shape=[small, small, small, bigs, bigs])(
        lam_re, lam_im, log_dt, b_re, b_im, dlr, dli, dbbr, dbbi)


def _s5_tile(t):
    return min(t, 128)


def _s5_blk(k, width):
    return slice(k * width, (k + 1) * width)


def _s5_in_bd(bb):
    b4 = bb.reshape(S5_BLOCKS, 8, S5_GROUP, S5_STATE)
    return jnp.einsum("kgin,gh->kgihn", b4, jnp.eye(8, dtype=bb.dtype)).reshape(S5_BLOCKS, S5_BCH, S5_BST)


def _s5_in_bd_diag(bd):
    b5 = bd.reshape(S5_BLOCKS, 8, S5_GROUP, 8, S5_STATE)
    return jnp.einsum("kgihn,gh->kgin", b5, jnp.eye(8, dtype=bd.dtype)).reshape(S5_GROUPS, S5_GROUP, S5_STATE)


def _s5_out_bd(c):
    c4 = c.reshape(S5_BLOCKS, 8, S5_GROUP, S5_STATE)
    return jnp.einsum("kgin,gh->kgnhi", c4, jnp.eye(8, dtype=c.dtype)).reshape(S5_BLOCKS, S5_BST, S5_BCH)


def _s5_out_bd_diag(bd):
    c5 = bd.reshape(S5_BLOCKS, 8, S5_STATE, 8, S5_GROUP)
    return jnp.einsum("kgnhi,gh->kgin", c5, jnp.eye(8, dtype=bd.dtype)).reshape(S5_GROUPS, S5_GROUP, S5_STATE)


def _s5_scan_fwd(x, ln, lb_re, lb_im, bbr_bd, bbi_bd, cr_bd, ci_bd, dskip):
    g, shift, scale, _ = ln
    t = x.shape[0]
    tm = _s5_tile(t)
    ns = S5_NSTATE

    def body(x_ref, g_ref, sc_ref, sh_ref, ar_ref, ai_ref, bbr_ref, bbi_ref, cr_ref, ci_ref, d_ref,
             yy_ref, xr_ref, xi_ref, cre_ref, cim_ref):
        @pl.when(pl.program_id(0) == 0)
        def _():
            cre_ref[...] = jnp.zeros_like(cre_ref)
            cim_ref[...] = jnp.zeros_like(cim_ref)

        h = _adaln(x_ref[...], g_ref[...], sc_ref[...], sh_ref[...])
        ub = h.astype(bf16)
        for k in range(S5_BLOCKS):
            uk = ub[:, _s5_blk(k, S5_BCH)]
            xr_ref[:, _s5_blk(k, S5_BST)] = jnp.dot(uk, bbr_ref[k], preferred_element_type=f32)
            xi_ref[:, _s5_blk(k, S5_BST)] = jnp.dot(uk, bbi_ref[k], preferred_element_type=f32)
        ar, ai = ar_ref[...], ai_ref[...]

        def step(tt, carry):
            sr, si = carry
            row = pl.ds(tt, 1)
            nr = (ar * sr - ai * si) + xr_ref[row, :]
            ni = (ar * si + ai * sr) + xi_ref[row, :]
            xr_ref[row, :] = nr
            xi_ref[row, :] = ni
            return nr, ni

        sr, si = lax.fori_loop(0, tm, step, (cre_ref[0:1, :], cim_ref[0:1, :]), unroll=2)
        cre_ref[0:1, :] = sr
        cim_ref[0:1, :] = si
        for k in range(S5_BLOCKS):
            sb = _s5_blk(k, S5_BST)
            yk = (jnp.dot(xr_ref[:, sb].astype(bf16), cr_ref[k], preferred_element_type=f32)
                  - jnp.dot(xi_ref[:, sb].astype(bf16), ci_ref[k], preferred_element_type=f32))
            cb = _s5_blk(k, S5_BCH)
            yy_ref[:, cb] = yk + d_ref[:, cb] * h[:, cb]

    bd_in = pl.BlockSpec((S5_BLOCKS, S5_BCH, S5_BST), lambda i: (0, 0, 0))
    bd_out = pl.BlockSpec((S5_BLOCKS, S5_BST, S5_BCH), lambda i: (0, 0, 0))
    st = jax.ShapeDtypeStruct((t, ns), f32)
    return pl.pallas_call(
        body, name="s5_scan_fwd", grid=(t // tm,),
        in_specs=[_tok_spec(tm, D), _row_spec(D), _row_spec(D), _row_spec(D), _row_spec(ns), _row_spec(ns),
                  bd_in, bd_in, bd_out, bd_out, _row_spec(D)],
        out_specs=[_tok_spec(tm, D), _tok_spec(tm, ns), _tok_spec(tm, ns)],
        out_shape=[jax.ShapeDtypeStruct((t, D), f32), st, st],
        scratch_shapes=[pltpu.VMEM((8, ns), f32), pltpu.VMEM((8, ns), f32)],
        compiler_params=_cp(1),
    )(x, g, scale, shift, lb_re, lb_im, bbr_bd, bbi_bd, cr_bd, ci_bd, dskip)


def _s5_scan_bwd(dyy, x, ln, xr, xi, lb_re, lb_im, bbr_bd, bbi_bd, cr_bd, ci_bd, dskip):
    g, shift, scale, _ = ln
    t = x.shape[0]
    tm = _s5_tile(t)
    nt = t // tm
    ns = S5_NSTATE
    per = tm // 8

    def body(dyy_ref, x_ref, g_ref, sc_ref, sh_ref, xr_ref, xi_ref, xrp_ref, xip_ref, ar_ref, ai_ref,
             bbr_ref, bbi_ref, cr_ref, ci_ref, d_ref,
             du_ref, dar_ref, dai_ref, dbbr_ref, dbbi_ref, dcr_ref, dci_ref, dd_ref,
             gr_ref, gi_ref, cre_ref, cim_ref):
        i = pl.program_id(0)
        first = i == 0

        @pl.when(first)
        def _():
            cre_ref[...] = jnp.zeros_like(cre_ref)
            cim_ref[...] = jnp.zeros_like(cim_ref)

        h = _adaln(x_ref[...], g_ref[...], sc_ref[...], sh_ref[...])
        ub = h.astype(bf16)
        dyy_v = dyy_ref[...]
        dyb = dyy_v.astype(bf16)
        for k in range(S5_BLOCKS):
            dk = dyb[:, _s5_blk(k, S5_BCH)]
            sb = _s5_blk(k, S5_BST)
            gr_ref[:, sb] = lax.dot_general(dk, cr_ref[k], _DN["nt"], preferred_element_type=f32)
            gi_ref[:, sb] = -lax.dot_general(dk, ci_ref[k], _DN["nt"], preferred_element_type=f32)
        ar, ai = ar_ref[...], ai_ref[...]

        def step(s, carry):
            nr_, ni_ = carry
            row = pl.ds(tm - 1 - s, 1)
            nr = gr_ref[row, :] + (ar * nr_ + ai * ni_)
            ni = gi_ref[row, :] + (ar * ni_ - ai * nr_)
            gr_ref[row, :] = nr
            gi_ref[row, :] = ni
            return nr, ni

        nr, ni = lax.fori_loop(0, tm, step, (cre_ref[0:1, :], cim_ref[0:1, :]), unroll=2)
        cre_ref[0:1, :] = nr
        cim_ref[0:1, :] = ni

        is_first_tile = i == nt - 1
        xr_v, xi_v = xr_ref[...], xi_ref[...]
        xrp = jnp.where(is_first_tile, 0.0, xrp_ref[...])
        xip = jnp.where(is_first_tile, 0.0, xip_ref[...])
        xr_s = _roll_rows(jnp.concatenate([xrp, xr_v], axis=0), 1)[8:, :]
        xi_s = _roll_rows(jnp.concatenate([xip, xi_v], axis=0), 1)[8:, :]
        gr, gi = gr_ref[...], gi_ref[...]
        _acc_add(first, dar_ref, jnp.sum(gr * xr_s + gi * xi_s, axis=0, keepdims=True))
        _acc_add(first, dai_ref, jnp.sum(gi * xr_s - gr * xi_s, axis=0, keepdims=True))
        _acc_add(first, dd_ref, jnp.sum(dyy_v * h, axis=0, keepdims=True))
        grb, gib = gr.astype(bf16), gi.astype(bf16)
        xrb, xib = xr_v.astype(bf16), xi_v.astype(bf16)
        for k in range(S5_BLOCKS):
            cb, sb = _s5_blk(k, S5_BCH), _s5_blk(k, S5_BST)
            uk, dk = ub[:, cb], dyb[:, cb]
            tn = lambda a_, b_: lax.dot_general(a_, b_, _DN["tn"], preferred_element_type=f32)
            vals = (tn(uk, grb[:, sb]), tn(uk, gib[:, sb]), tn(xrb[:, sb], dk), -tn(xib[:, sb], dk))
            for ref, val in zip((dbbr_ref, dbbi_ref, dcr_ref, dci_ref), vals):
                @pl.when(first)
                def _():
                    ref[k] = val

                @pl.when(jnp.logical_not(first))
                def _():
                    ref[k] += val
            du_k = (lax.dot_general(grb[:, sb], bbr_ref[k], _DN["nt"], preferred_element_type=f32)
                    + lax.dot_general(gib[:, sb], bbi_ref[k], _DN["nt"], preferred_element_type=f32))
            du_ref[:, cb] = du_k + d_ref[:, cb] * dyy_v[:, cb]

    rev = lambda c: _tok_spec(tm, c, nt, True)
    prev = pl.BlockSpec((8, ns), lambda i: (jnp.maximum((nt - 1 - i) * per - 1, 0), 0))
    bd_in = pl.BlockSpec((S5_BLOCKS, S5_BCH, S5_BST), lambda i: (0, 0, 0))
    bd_out = pl.BlockSpec((S5_BLOCKS, S5_BST, S5_BCH), lambda i: (0, 0, 0))
    row_ns = jax.ShapeDtypeStruct((1, ns), f32)
    bd_in_s = jax.ShapeDtypeStruct((S5_BLOCKS, S5_BCH, S5_BST), f32)
    bd_out_s = jax.ShapeDtypeStruct((S5_BLOCKS, S5_BST, S5_BCH), f32)
    return pl.pallas_call(
        body, name="s5_scan_bwd", grid=(nt,),
        in_specs=[rev(D), rev(D), _row_spec(D), _row_spec(D), _row_spec(D), rev(ns), rev(ns), prev, prev,
                  _row_spec(ns), _row_spec(ns), bd_in, bd_in, bd_out, bd_out, _row_spec(D)],
        out_specs=[rev(D), _row_spec(ns), _row_spec(ns), bd_in, bd_in, bd_out, bd_out, _row_spec(D)],
        out_shape=[jax.ShapeDtypeStruct((t, D), f32), row_ns, row_ns, bd_in_s, bd_in_s, bd_out_s, bd_out_s,
                   jax.ShapeDtypeStruct((1, D), f32)],
        scratch_shapes=[pltpu.VMEM((tm, ns), f32), pltpu.VMEM((tm, ns), f32),
                        pltpu.VMEM((8, ns), f32), pltpu.VMEM((8, ns), f32)],
        compiler_params=_cp(1),
    )(dyy, x, g, scale, shift, xr, xi, xr, xi, lb_re, lb_im, bbr_bd, bbi_bd, cr_bd, ci_bd, dskip)


def _s5_gelu(yy):
    t = yy.shape[0]
    tm = _tok_tile(t)

    def body(y_ref, o_ref):
        o_ref[...] = jax.nn.gelu(y_ref[...]).astype(bf16)

    return pl.pallas_call(
        body, name="s5_gelu", grid=(t // tm,), in_specs=[_tok_spec(tm, D)], out_specs=_tok_spec(tm, D),
        out_shape=jax.ShapeDtypeStruct((t, D), bf16), compiler_params=_cp(1),
    )(yy)


def _s5_glu(gl, z):
    return gl * jax.nn.sigmoid(z)


def _s5_out(x, yy, z, gate):
    t = x.shape[0]
    tm = _tok_tile(t)

    def body(x_ref, y_ref, z_ref, gt_ref, o_ref):
        o_ref[...] = x_ref[...] + (1.0 + gt_ref[...]) * _s5_glu(jax.nn.gelu(y_ref[...]), z_ref[...])

    return pl.pallas_call(
        body, name="s5_out", grid=(t // tm,),
        in_specs=[_tok_spec(tm, D), _tok_spec(tm, D), _tok_spec(tm, D), _row_spec(D)],
        out_specs=_tok_spec(tm, D), out_shape=jax.ShapeDtypeStruct((t, D), f32), compiler_params=_cp(1),
    )(x, yy, z, gate)


def _s5_out_bwd(dxn, yy, z, gate):
    t = dxn.shape[0]
    tm = _tok_tile(t)

    def body(dxn_ref, y_ref, z_ref, gt_ref, dz_ref, dgl_ref, dgt_ref):
        dxn_v = dxn_ref[...]
        gl = jax.nn.gelu(y_ref[...])
        out, vjp = jax.vjp(_s5_glu, gl, z_ref[...])
        dgl, dz = vjp((1.0 + gt_ref[...]) * dxn_v)
        dz_ref[...] = dz.astype(bf16)
        dgl_ref[...] = dgl
        _acc_add(pl.program_id(0) == 0, dgt_ref, jnp.sum(dxn_v * out, axis=0, keepdims=True))

    return pl.pallas_call(
        body, name="s5_out_bwd", grid=(t // tm,),
        in_specs=[_tok_spec(tm, D), _tok_spec(tm, D), _tok_spec(tm, D), _row_spec(D)],
        out_specs=[_tok_spec(tm, D), _tok_spec(tm, D), _row_spec(D)],
        out_shape=[jax.ShapeDtypeStruct((t, D), bf16), jax.ShapeDtypeStruct((t, D), f32),
                   jax.ShapeDtypeStruct((1, D), f32)],
        compiler_params=_cp(1),
    )(dxn, yy, z, gate)


def _s5_gelu_bwd(yy, dgl_a, dgl_b):
    t = yy.shape[0]
    tm = _tok_tile(t)

    def body(y_ref, a_ref, b_ref, o_ref):
        _, vjp = jax.vjp(jax.nn.gelu, y_ref[...])
        o_ref[...] = vjp(a_ref[...] + b_ref[...])[0]

    return pl.pallas_call(
        body, name="s5_gelu_bwd", grid=(t // tm,),
        in_specs=[_tok_spec(tm, D), _tok_spec(tm, D), _tok_spec(tm, D)],
        out_specs=_tok_spec(tm, D), out_shape=jax.ShapeDtypeStruct((t, D), f32), compiler_params=_cp(1),
    )(yy, dgl_a, dgl_b)


def _s5_params(lam_re, lam_im, log_dt, b_re, b_im):
    bc = lambda a: a.reshape(S5_GROUPS, 1, -1)
    return (bc(lam_re), bc(lam_im), jnp.broadcast_to(log_dt.reshape(S5_GROUPS, 1, 1), (S5_GROUPS, 1, S5_STATE)),
            b_re.transpose(0, 2, 1), b_im.transpose(0, 2, 1))


def _s5_fwd(x, ln, raw, c_re, c_im, dskip, w_glu):
    gate = ln[3]
    lb_re, lb_im, bb_re, bb_im = _s5_prep(*raw)
    lbr, lbi = lb_re.reshape(1, S5_NSTATE), lb_im.reshape(1, S5_NSTATE)
    bds = (_s5_in_bd(bb_re).astype(bf16), _s5_in_bd(bb_im).astype(bf16),
           _s5_out_bd(c_re).astype(bf16), _s5_out_bd(c_im).astype(bf16))
    yy, xr, xi = _s5_scan_fwd(x, ln, lbr, lbi, *bds, dskip)
    gl = _s5_gelu(yy)
    z = _mm(gl, w_glu, "nn", f32, "s5_glu_mm")
    return _s5_out(x, yy, z, gate), (x, lbr, lbi, bds, yy, xr, xi, gl, z)


def _s5_bwd(dxn, saved, ln, raw, dskip, w_glu):
    x, lbr, lbi, bds, yy, xr, xi, gl, z = saved
    g, shift, scale, gate = ln
    dz, dgl_a, dgate = _s5_out_bwd(dxn, yy, z, gate)
    dgl_b = _mm(dz, w_glu, "nt", f32, "s5_dgl")
    dw_glu = _mm(gl, dz, "tn", f32, "s5_dwglu")
    dyy = _s5_gelu_bwd(yy, dgl_a, dgl_b)
    du, dar, dai, dbbr_bd, dbbi_bd, dcr_bd, dci_bd, dd = _s5_scan_bwd(dyy, x, ln, xr, xi, lbr, lbi, *bds, dskip)
    shp = (S5_GROUPS, 1, S5_STATE)
    d_lam_re, d_lam_im, d_dt, d_b_re, d_b_im = _s5_prep_bwd(
        *raw, dar.reshape(shp), dai.reshape(shp), _s5_in_bd_diag(dbbr_bd), _s5_in_bd_diag(dbbi_bd))
    dx, dg, dscale, dshift = _ln_bwd(x, g, scale, shift, du, dxn)
    grads = dict(
        s5_lam_re=d_lam_re.reshape(1, S5_GROUPS, S5_STATE), s5_lam_im=d_lam_im.reshape(1, S5_GROUPS, S5_STATE),
        s5_log_dt=d_dt[:, 0, 0].reshape(1, S5_GROUPS),
        s5_b_re=d_b_re.transpose(0, 2, 1)[None], s5_b_im=d_b_im.transpose(0, 2, 1)[None],
        s5_c_re=_s5_out_bd_diag(dcr_bd)[None], s5_c_im=_s5_out_bd_diag(dci_bd)[None],
        s5_d=dd, s5_w_glu=dw_glu)
    return dx, (dg, dshift, dscale, dgate), grads


_MESH = pl.DeviceIdType.MESH
_ANY = pl.BlockSpec(memory_space=pl.ANY)


def _me():
    return lax.axis_index("x"), lax.axis_index("y"), lax.axis_index("c")


def _dev_index(x, y, c):
    return 4 * x + 2 * y + c


def _all_gather(v, name):
    r, c = v.shape

    def body(v_ref, out_ref, send_sems, recv_sems, local_sem):
        x, y, cc = _me()
        me, sibling = (x, y, cc), (x, y, 1 - cc)
        chips = [(1 - x, y), (x, 1 - y), (1 - x, 1 - y)]

        def rows(px, py, pc):
            return out_ref.at[_dev_index(px, py, pc)]

        def copy(k, block, to, src=None):
            return pltpu.make_async_remote_copy(
                src_ref=rows(*block) if src is None else src, dst_ref=rows(*block),
                send_sem=send_sems.at[k], recv_sem=recv_sems.at[k], device_id=to, device_id_type=_MESH)

        mine = pltpu.make_async_copy(v_ref, rows(*me), local_sem)
        mine.start()
        first = [copy(0, me, sibling, src=v_ref)]
        first += [copy(1 + j, me, (*chip, cc), src=v_ref) for j, chip in enumerate(chips)]
        for cp in first:
            cp.start()
        passed = [copy(4 + j, (*chip, cc), sibling) for j, chip in enumerate(chips)]
        for j, chip in enumerate(chips):
            copy(1 + j, (*chip, cc), me).wait_recv()
            passed[j].start()
        copy(0, sibling, me).wait_recv()
        for j, chip in enumerate(chips):
            copy(4 + j, (*chip, 1 - cc), me).wait_recv()
        for cp in first + passed:
            cp.wait_send()
        mine.wait()

    return pl.pallas_call(
        body, name=name, out_shape=jax.ShapeDtypeStruct((N_DEV, r, c), v.dtype),
        in_specs=[_ANY], out_specs=_ANY,
        scratch_shapes=[pltpu.SemaphoreType.DMA((7,)), pltpu.SemaphoreType.DMA((7,)), pltpu.SemaphoreType.DMA],
    )(v)


def _exchange(v, name):
    _, r, c = v.shape

    def body(v_ref, out_ref, send_sems, recv_sems, local_sem):
        x, y, cc = _me()
        me = _dev_index(x, y, cc)
        own = pltpu.make_async_copy(v_ref.at[me], out_ref.at[me], local_sem)
        own.start()
        peers = []
        for m in (1, 2, 4, 3, 5, 6, 7):
            px = 1 - x if m & 4 else x
            py = 1 - y if m & 2 else y
            pc = 1 - cc if m & 1 else cc
            peers.append((m - 1, (px, py, pc), _dev_index(px, py, pc)))
        sends = []
        for k, peer, pidx in peers:
            cp = pltpu.make_async_remote_copy(
                src_ref=v_ref.at[pidx], dst_ref=out_ref.at[me],
                send_sem=send_sems.at[k], recv_sem=recv_sems.at[k], device_id=peer, device_id_type=_MESH)
            cp.start()
            sends.append(cp)
        for k, peer, pidx in peers:
            pltpu.make_async_remote_copy(
                src_ref=v_ref.at[pidx], dst_ref=out_ref.at[pidx],
                send_sem=send_sems.at[k], recv_sem=recv_sems.at[k], device_id=peer, device_id_type=_MESH).wait_recv()
        for cp in sends:
            cp.wait_send()
        own.wait()

    return pl.pallas_call(
        body, name=name, out_shape=jax.ShapeDtypeStruct((N_DEV, r, c), v.dtype),
        in_specs=[_ANY], out_specs=_ANY,
        scratch_shapes=[pltpu.SemaphoreType.DMA((7,)), pltpu.SemaphoreType.DMA((7,)), pltpu.SemaphoreType.DMA],
    )(v)


def _ada_mod(c_all, ada_w):
    cols = ada_w.shape[2]

    def body(c_ref, w_ref, o_ref):
        cond = jax.nn.silu(c_ref[...]).astype(bf16)
        o_ref[0] = jnp.dot(cond, w_ref[0].astype(bf16), preferred_element_type=f32)

    return pl.pallas_call(
        body, name="ada_mod", grid=(DEPTH,),
        in_specs=[pl.BlockSpec((16, D), lambda i: (0, 0)), pl.BlockSpec((1, D, cols), lambda i: (i, 0, 0))],
        out_specs=pl.BlockSpec((1, 16, cols), lambda i: (i, 0, 0)),
        out_shape=jax.ShapeDtypeStruct((DEPTH, 16, cols), f32), compiler_params=_cp(1),
    )(c_all, ada_w)


def _ada_grad(c_all, dmod):
    cols = dmod.shape[2]

    def body(c_ref, d_ref, o_ref):
        cond = jax.nn.silu(c_ref[...]).astype(bf16)
        o_ref[0] = lax.dot_general(cond, d_ref[0].astype(bf16), _DN["tn"], preferred_element_type=f32)

    return pl.pallas_call(
        body, name="ada_grad", grid=(DEPTH,),
        in_specs=[pl.BlockSpec((16, D), lambda i: (0, 0)), pl.BlockSpec((1, 16, cols), lambda i: (i, 0, 0))],
        out_specs=pl.BlockSpec((1, D, cols), lambda i: (i, 0, 0)),
        out_shape=jax.ShapeDtypeStruct((DEPTH, D, cols), f32), compiler_params=_cp(1),
    )(c_all, dmod)


def _row_tile(r):
    return 512 if r % 512 == 0 else r


def _sum_sources(v, name):
    n, r, c = v.shape
    tr = _row_tile(r)

    def body(v_ref, o_ref):
        acc = v_ref[0]
        for p in range(1, n):
            acc = acc + v_ref[p]
        o_ref[...] = acc.astype(f32)

    return pl.pallas_call(
        body, name=name, grid=(r // tr,),
        in_specs=[pl.BlockSpec((n, tr, c), lambda i: (0, i, 0))], out_specs=pl.BlockSpec((tr, c), lambda i: (i, 0)),
        out_shape=jax.ShapeDtypeStruct((r, c), f32), compiler_params=_cp(1),
    )(v)


def _adamw(g, w, m, v, name):
    r, c = g.shape
    tr = _row_tile(r)
    c1 = 1.0 - ADAM_B1 ** ADAM_STEP
    c2 = 1.0 - ADAM_B2 ** ADAM_STEP

    def body(g_ref, w_ref, m_ref, v_ref, d_ref, mo_ref, vo_ref):
        g_v = g_ref[...]
        m_n = ADAM_B1 * m_ref[...] + (1.0 - ADAM_B1) * g_v
        v_n = ADAM_B2 * v_ref[...] + (1.0 - ADAM_B2) * (g_v * g_v)
        d_ref[...] = -ADAM_LR * ((m_n / c1) / (jnp.sqrt(v_n / c2) + ADAM_EPS) + ADAM_WD * w_ref[...])
        mo_ref[...] = m_n
        vo_ref[...] = v_n

    spec = pl.BlockSpec((tr, c), lambda i: (i, 0))
    shp = jax.ShapeDtypeStruct((r, c), f32)
    return pl.pallas_call(
        body, name=name, grid=(r // tr,), in_specs=[spec] * 4, out_specs=[spec] * 3, out_shape=[shp] * 3,
        compiler_params=_cp(1),
    )(g, w, m, v)


def _pack(parts, lead, row_mult):
    flat = [p.reshape(lead, -1) for p in parts]
    n = sum(f.shape[1] for f in flat)
    r = -(-n // PACK_C)
    r = -(-r // row_mult) * row_mult
    pad = r * PACK_C - n
    if pad:
        flat.append(jnp.zeros((lead, pad), flat[0].dtype))
    return jnp.concatenate(flat, axis=1).reshape(lead, r, PACK_C)


def _unpack(packed, shapes):
    lead = packed.shape[0]
    flat = packed.reshape(lead, -1)
    out, off = [], 0
    for s in shapes:
        n = math.prod(s)
        out.append(flat[:, off:off + n].reshape((lead,) + tuple(s)))
        off += n
    return out


def _unshard(g8, axis):
    local = g8.shape[1:]
    moved = jnp.moveaxis(g8, 0, axis)
    return moved.reshape(local[:axis] + (N_DEV * local[axis],) + local[axis + 1:])


def _shard8(full, axis):
    s = full.shape
    split = full.reshape(s[:axis] + (N_DEV, s[axis] // N_DEV) + s[axis + 1:])
    return jnp.moveaxis(split, axis, 0)


_BIG = dict(ffn_w_in=3, ffn_w_out=2, pool_w=2, fox_w_in=2, fox_w_o=1, s5_w_glu=1, conv_w_in=2, conv_w_out=1)
_SMALL_SHARDED = dict(norm_g=2, s5_d=1, conv_w=3)
_REPLICATED = ("ada_b", "pool_scale", "fox_b_f", "fox_q_gain", "fox_k_gain", "s5_lam_re", "s5_lam_im", "s5_log_dt",
               "s5_b_re", "s5_b_im", "s5_c_re", "s5_c_im")
_WEIGHTS = ("ada_w", "ada_b", "norm_g", "ffn_w_in", "ffn_w_out", "pool_w", "pool_scale", "fox_w_in", "fox_b_f",
            "fox_q_gain", "fox_k_gain", "fox_w_o", "s5_lam_re", "s5_lam_im", "s5_log_dt", "s5_b_re", "s5_b_im",
            "s5_c_re", "s5_c_im", "s5_d", "s5_w_glu", "conv_w_in", "conv_w", "conv_w_out")


def _step(x, c, target, w, m, v):
    t = x.shape[1]
    xi_, yi_, ci_ = _me()
    me = _dev_index(xi_, yi_, ci_)

    small_in = _pack([c] + [w[n].reshape(1, -1) for n in _SMALL_SHARDED], 1, 8)[0]
    small_all = _all_gather(small_in, "gather_small")
    c_all = small_all[:, 0, :]
    sm_shapes = [w[n].shape for n in _SMALL_SHARDED]
    sm_parts = _unpack(small_all.reshape(N_DEV, 1, -1)[:, :, PACK_C:].reshape(N_DEV, -1, PACK_C), sm_shapes)
    full = {n: _unshard(p, ax + 0) for (n, ax), p in zip(_SMALL_SHARDED.items(), sm_parts)}

    big_shapes = [w[n].shape for n in _BIG]
    big_in = _pack([w[n].astype(bf16).reshape(1, -1) for n in _BIG], 1, 512)[0]
    big_all = _all_gather(big_in, "gather_weights")
    for (n, ax), p in zip(_BIG.items(), _unpack(big_all, big_shapes)):
        full[n] = _unshard(p, ax)

    c16 = jnp.pad(c_all, ((0, 8), (0, 0)))
    cols = w["ada_w"].shape[2]
    mod_sh = _ada_mod(c16, w["ada_w"])
    mod_all = _all_gather(mod_sh.reshape(DEPTH * 16, cols), "gather_mod").reshape(N_DEV, DEPTH, 16, cols)
    mod_mine = lax.dynamic_index_in_dim(mod_all, me, axis=2, keepdims=False)
    mod = (mod_mine.transpose(1, 0, 2).reshape(DEPTH, N_DEV * cols) + w["ada_b"]).reshape(DEPTH, 3, 3, D)

    norm_g = full["norm_g"]

    def ln_of(i, sub):
        return (norm_g[i, sub][None], mod[i, sub, 0][None], mod[i, sub, 1][None], mod[i, sub, 2][None])

    fox_w_in = jnp.pad(full["fox_w_in"][0], ((0, 0), (0, FOX_PROJ_PAD - FOX_PROJ)))
    fox_b_f = jnp.pad(w["fox_b_f"], ((0, 0), (0, 128 - FOX_HEADS)))
    s5_raw = _s5_params(w["s5_lam_re"][0], w["s5_lam_im"][0], w["s5_log_dt"][0], w["s5_b_re"][0], w["s5_b_im"][0])
    s5_c_re, s5_c_im = w["s5_c_re"][0], w["s5_c_im"][0]
    conv_w = jnp.pad(full["conv_w"][0, :, 0, :], ((0, 5), (0, 0)))

    xs = x[0]
    saved = []
    for i in range(DEPTH):
        xs, s0 = _ffn_fwd(xs, ln_of(i, 0), full["ffn_w_in"][i, 0], full["ffn_w_out"][i, 0], 0.5)
        if i == 0:
            s1 = xs
            xs = _pool_fwd(xs, ln_of(i, 1), full["pool_w"][0], w["pool_scale"])
        elif i == 1:
            xs, s1 = _fox_fwd(xs, ln_of(i, 1), fox_w_in, fox_b_f, w["fox_q_gain"], w["fox_k_gain"], full["fox_w_o"][0])
        elif i == 2:
            xs, s1 = _s5_fwd(xs, ln_of(i, 1), s5_raw, s5_c_re, s5_c_im, full["s5_d"], full["s5_w_glu"][0])
        else:
            xs, s1 = _convmix_fwd(xs, ln_of(i, 1), full["conv_w_in"][0], conv_w, full["conv_w_out"][0])
        xs, s2 = _ffn_fwd(xs, ln_of(i, 2), full["ffn_w_in"][i, 1], full["ffn_w_out"][i, 1], 0.5)
        saved.append((s0, s1, s2))
    dx, lpart = _loss_head(xs, target[0])
    loss = lax.psum(lpart[0, 0], AXES)

    grads = {}
    dmod = [[None] * 3 for _ in range(DEPTH)]
    dnorm = [[None] * 3 for _ in range(DEPTH)]
    dffn_in = [[None] * 2 for _ in range(DEPTH)]
    dffn_out = [[None] * 2 for _ in range(DEPTH)]

    def put_ln(i, sub, dln):
        dg, dshift, dscale, dgate = dln
        dnorm[i][sub] = dg
        dmod[i][sub] = jnp.concatenate([dshift, dscale, dgate], axis=0)

    for i in reversed(range(DEPTH)):
        s0, s1, s2 = saved[i]
        dx, dln, dffn_in[i][1], dffn_out[i][1] = _ffn_bwd(
            dx, s2, ln_of(i, 2), full["ffn_w_in"][i, 1], full["ffn_w_out"][i, 1], 0.5)
        put_ln(i, 2, dln)
        if i == 0:
            dx, dln, dpw, dps = _pool_bwd(dx, s1, ln_of(i, 1), full["pool_w"][0], w["pool_scale"])
            grads.update(pool_w=dpw[None], pool_scale=dps)
        elif i == 1:
            dx, dln, dwi, dbf, dqg, dkg, dwo = _fox_bwd(
                dx, s1, ln_of(i, 1), fox_w_in, fox_b_f, w["fox_q_gain"], w["fox_k_gain"], full["fox_w_o"][0])
            grads.update(fox_w_in=dwi[None, :, :FOX_PROJ], fox_b_f=dbf[:, :FOX_HEADS], fox_q_gain=dqg, fox_k_gain=dkg,
                         fox_w_o=dwo[None])
        elif i == 2:
            dx, dln, gs5 = _s5_bwd(dx, s1, ln_of(i, 1), s5_raw, full["s5_d"], full["s5_w_glu"][0])
            gs5["s5_w_glu"] = gs5["s5_w_glu"][None]
            grads.update(gs5)
        else:
            dx, dln, dwi, dcw, dwo = _convmix_bwd(
                dx, s1, ln_of(i, 1), full["conv_w_in"][0], conv_w, full["conv_w_out"][0])
            grads.update(conv_w_in=dwi[None], conv_w=dcw[None, :, None, :], conv_w_out=dwo[None])
        put_ln(i, 1, dln)
        dx, dln, dffn_in[i][0], dffn_out[i][0] = _ffn_bwd(
            dx, s0, ln_of(i, 0), full["ffn_w_in"][i, 0], full["ffn_w_out"][i, 0], 0.5)
        put_ln(i, 0, dln)
    grads["ffn_w_in"] = jnp.stack([jnp.stack(r) for r in dffn_in])
    grads["ffn_w_out"] = jnp.stack([jnp.stack(r) for r in dffn_out])
    grads["norm_g"] = jnp.stack([jnp.concatenate(r, axis=0) for r in dnorm])
    dmod_mine = jnp.stack([jnp.stack(r) for r in dmod]).reshape(DEPTH, 9 * D)

    small_names = list(_REPLICATED[1:]) + list(_SMALL_SHARDED)
    small_parts = [dmod_mine] + [grads[n] for n in small_names]
    small_shapes = [p.shape for p in small_parts]
    small_g = _all_gather(_pack([p.reshape(1, -1) for p in small_parts], 1, 8)[0], "gather_grads")
    small_sum = _sum_sources(small_g, "sum_small")
    summed = dict(zip(["ada_b"] + small_names, [p[0] for p in _unpack(small_sum[None], small_shapes)]))
    for n, ax in _SMALL_SHARDED.items():
        local = w[n].shape[ax]
        summed[n] = lax.dynamic_slice_in_dim(summed[n], me * local, local, axis=ax)

    dmod_all = small_g[:, :DEPTH * 9].reshape(N_DEV, DEPTH, 9 * D)
    dmod_cols = lax.dynamic_slice_in_dim(dmod_all, me * cols, cols, axis=2)
    summed["ada_w"] = _ada_grad(c16, jnp.pad(dmod_cols.transpose(1, 0, 2), ((0, 0), (0, 8), (0, 0))))

    big_g = _pack([_shard8(grads[n], ax + 0).reshape(N_DEV, -1) for n, ax in _BIG.items()], N_DEV, 512)
    big_sum = _sum_sources(_exchange(big_g, "exchange_grads"), "sum_big")
    for n, p in zip(_BIG, _unpack(big_sum[None], big_shapes)):
        summed[n] = p[0]

    delta, new_m, new_v = {}, {}, {}
    two_d = lambda a: a.reshape(-1, a.shape[-1])
    d_, m_, v_ = _adamw(two_d(summed["ada_w"]), two_d(w["ada_w"]), two_d(m["ada_w"]), two_d(v["ada_w"]), "adamw_ada")
    delta["ada_w"], new_m["ada_w"], new_v["ada_w"] = (a.reshape(w["ada_w"].shape) for a in (d_, m_, v_))
    rest = [n for n in _WEIGHTS if n != "ada_w"]
    rest_shapes = [w[n].shape for n in rest]
    pk = lambda d: _pack([d[n].reshape(1, -1) for n in rest], 1, 512)[0]
    outs = _adamw(pk(summed), pk(w), pk(m), pk(v), "adamw_rest")
    for dst, packed in zip((delta, new_m, new_v), outs):
        for n, p in zip(rest, _unpack(packed[None], rest_shapes)):
            dst[n] = p[0]
    grads_out = [summed[n].reshape(w[n].shape) for n in _WEIGHTS]
    return (loss, dx[None], *grads_out, *[delta[n] for n in _WEIGHTS], *[new_m[n] for n in _WEIGHTS],
            *[new_v[n] for n in _WEIGHTS])


def kernel(x, c, ada_w, ada_b, norm_g, ffn_w_in, ffn_w_out, pool_w, pool_scale, fox_w_in, fox_b_f, fox_q_gain, fox_k_gain, fox_w_o, s5_lam_re, s5_lam_im, s5_log_dt, s5_b_re, s5_b_im, s5_c_re, s5_c_im, s5_d, s5_w_glu, conv_w_in, conv_w, conv_w_out, loss_target, m_ada_w, m_ada_b, m_norm_g, m_ffn_w_in, m_ffn_w_out, m_pool_w, m_pool_scale, m_fox_w_in, m_fox_b_f, m_fox_q_gain, m_fox_k_gain, m_fox_w_o, m_s5_lam_re, m_s5_lam_im, m_s5_log_dt, m_s5_b_re, m_s5_b_im, m_s5_c_re, m_s5_c_im, m_s5_d, m_s5_w_glu, m_conv_w_in, m_conv_w, m_conv_w_out, v_ada_w, v_ada_b, v_norm_g, v_ffn_w_in, v_ffn_w_out, v_pool_w, v_pool_scale, v_fox_w_in, v_fox_b_f, v_fox_q_gain, v_fox_k_gain, v_fox_w_o, v_s5_lam_re, v_s5_lam_im, v_s5_log_dt, v_s5_b_re, v_s5_b_im, v_s5_c_re, v_s5_c_im, v_s5_d, v_s5_w_glu, v_conv_w_in, v_conv_w, v_conv_w_out):
    ws = (ada_w, ada_b, norm_g, ffn_w_in, ffn_w_out, pool_w, pool_scale, fox_w_in, fox_b_f, fox_q_gain, fox_k_gain,
          fox_w_o, s5_lam_re, s5_lam_im, s5_log_dt, s5_b_re, s5_b_im, s5_c_re, s5_c_im, s5_d, s5_w_glu, conv_w_in,
          conv_w, conv_w_out)
    ms = (m_ada_w, m_ada_b, m_norm_g, m_ffn_w_in, m_ffn_w_out, m_pool_w, m_pool_scale, m_fox_w_in, m_fox_b_f,
          m_fox_q_gain, m_fox_k_gain, m_fox_w_o, m_s5_lam_re, m_s5_lam_im, m_s5_log_dt, m_s5_b_re, m_s5_b_im,
          m_s5_c_re, m_s5_c_im, m_s5_d, m_s5_w_glu, m_conv_w_in, m_conv_w, m_conv_w_out)
    vs = (v_ada_w, v_ada_b, v_norm_g, v_ffn_w_in, v_ffn_w_out, v_pool_w, v_pool_scale, v_fox_w_in, v_fox_b_f,
          v_fox_q_gain, v_fox_k_gain, v_fox_w_o, v_s5_lam_re, v_s5_lam_im, v_s5_log_dt, v_s5_b_re, v_s5_b_im,
          v_s5_c_re, v_s5_c_im, v_s5_d, v_s5_w_glu, v_conv_w_in, v_conv_w, v_conv_w_out)
    return _step(x, c, loss_target, dict(zip(_WEIGHTS, ws)), dict(zip(_WEIGHTS, ms)), dict(zip(_WEIGHTS, vs)))
```

```python
import math

import jax
import jax.numpy as jnp
from jax import lax
from jax.experimental import pallas as pl
from jax.experimental.pallas import tpu as pltpu

f32 = jnp.float32
bf16 = jnp.bfloat16

D = 1024
D_FF = 2816
FFN_HS = 2 * D_FF // 8
FFN_SLABS = 4
DEPTH = 4
NORM_EPS = 1e-6
N_DEV = 8
AXES = ("x", "y", "c")
POOL_WINDOWS = (2, 4, 8, 16)
POOL_GROUP = 256
POOL_HALO = 16
FOX_HEADS = 16
FOX_HEAD_DIM = 64
FOX_PROJ = 3088
FOX_PROJ_PAD = 3200
S5_GROUPS = 64
S5_GROUP = 16
S5_STATE = 64
S5_NSTATE = S5_GROUPS * S5_STATE
S5_BLOCKS = 8
S5_BCH = 128
S5_BST = 512
CONV_HALO = 8
ADAM_LR = 0.001
ADAM_B1 = 0.9
ADAM_B2 = 0.999
ADAM_EPS = 1e-08
ADAM_WD = 0.01
ADAM_STEP = 10
VMEM_LIMIT = 56 * 1024 * 1024
PACK_C = 1024

_ARB = "arbitrary"


def _cp(n_axes):
    return pltpu.CompilerParams(dimension_semantics=(_ARB,) * n_axes, vmem_limit_bytes=VMEM_LIMIT)


def _pick(n, prefs):
    for c in prefs:
        if n % c == 0:
            return c
    return n


_DN = {"nn": (((1,), (0,)), ((), ())), "nt": (((1,), (1,)), ((), ())), "tn": (((0,), (0,)), ((), ()))}


def _mm(a, b, mode, out_dtype, name):
    if mode == "nn":
        (m, k), (_, n) = a.shape, b.shape
    elif mode == "nt":
        (m, k), (n, _) = a.shape, b.shape
    else:
        (k, m), (_, n) = a.shape, b.shape
    big = (1408, 1024, 640, 512, 384, 256, 128)
    tm = _pick(m, big) if mode == "tn" else _pick(m, (1024, 512, 256, 128))
    tn = _pick(n, big)
    if mode == "tn":
        tk = _pick(k, (512, 256, 128))
    else:
        tk = k if k <= 3200 else _pick(k, (2816, 2048, 1024, 512))
    nk = k // tk
    dn = _DN[mode]

    def body(a_ref, b_ref, o_ref, acc_ref):
        p = lax.dot_general(a_ref[...], b_ref[...], dn, preferred_element_type=f32)
        if nk == 1:
            o_ref[...] = p.astype(out_dtype)
        else:
            kk = pl.program_id(2)

            @pl.when(kk == 0)
            def _():
                acc_ref[...] = p

            @pl.when(kk > 0)
            def _():
                acc_ref[...] += p

            @pl.when(kk == nk - 1)
            def _():
                o_ref[...] = acc_ref[...].astype(out_dtype)

    if mode == "nn":
        a_spec = pl.BlockSpec((tm, tk), lambda i, j, kk: (i, kk))
        b_spec = pl.BlockSpec((tk, tn), lambda i, j, kk: (kk, j))
    elif mode == "nt":
        a_spec = pl.BlockSpec((tm, tk), lambda i, j, kk: (i, kk))
        b_spec = pl.BlockSpec((tn, tk), lambda i, j, kk: (j, kk))
    else:
        a_spec = pl.BlockSpec((tk, tm), lambda i, j, kk: (kk, i))
        b_spec = pl.BlockSpec((tk, tn), lambda i, j, kk: (kk, j))
    acc_shape = (tm, tn) if nk > 1 else (8, 128)
    return pl.pallas_call(
        body, name=name, grid=(m // tm, n // tn, nk),
        in_specs=[a_spec, b_spec], out_specs=pl.BlockSpec((tm, tn), lambda i, j, kk: (i, j)),
        out_shape=jax.ShapeDtypeStruct((m, n), out_dtype),
        scratch_shapes=[pltpu.VMEM(acc_shape, f32)],
        compiler_params=_cp(3),
    )(a, b)


def _mmx(a, b, mode, name, grid, a_spec, b_spec, o_spec, out_shape):
    nk = grid[2]
    dn = _DN[mode]
    out_dtype = out_shape.dtype
    a_blk = (math.prod(a_spec.block_shape[:-1]), a_spec.block_shape[-1])
    b_blk = (math.prod(b_spec.block_shape[:-1]), b_spec.block_shape[-1])
    o_blk = (math.prod(o_spec.block_shape[:-1]), o_spec.block_shape[-1])

    def body(a_ref, b_ref, o_ref, acc_ref):
        p = lax.dot_general(a_ref[...].reshape(a_blk), b_ref[...].reshape(b_blk), dn, preferred_element_type=f32)
        if nk == 1:
            o_ref[...] = p.reshape(o_ref.shape).astype(out_dtype)
        else:
            kk = pl.program_id(2)

            @pl.when(kk == 0)
            def _():
                acc_ref[...] = p

            @pl.when(kk > 0)
            def _():
                acc_ref[...] += p

            @pl.when(kk == nk - 1)
            def _():
                o_ref[...] = acc_ref[...].reshape(o_ref.shape).astype(out_dtype)

    return pl.pallas_call(
        body, name=name, grid=grid, in_specs=[a_spec, b_spec], out_specs=o_spec, out_shape=out_shape,
        scratch_shapes=[pltpu.VMEM(o_blk if nk > 1 else (8, 128), f32)], compiler_params=_cp(3),
    )(a, b)


def _tok_tile(t):
    return min(t, 512)


def _tok_spec(tm, c, nt=None, reverse=False):
    if reverse:
        return pl.BlockSpec((tm, c), lambda i: (nt - 1 - i, 0))
    return pl.BlockSpec((tm, c), lambda i: (i, 0))


def _row_spec(c, rows=1):
    return pl.BlockSpec((rows, c), lambda i: (0, 0))


def _acc_add(first, ref, val):
    @pl.when(first)
    def _():
        ref[...] = val

    @pl.when(jnp.logical_not(first))
    def _():
        ref[...] += val


def _adaln(x, g, scale, shift):
    y = x * lax.rsqrt(jnp.mean(x * x, axis=-1, keepdims=True) + NORM_EPS)
    return (y * g) * (1.0 + scale) + shift


def _ln_fwd(x, g, scale, shift):
    t = x.shape[0]
    tm = _tok_tile(t)

    def body(x_ref, g_ref, sc_ref, sh_ref, h_ref):
        h_ref[...] = _adaln(x_ref[...], g_ref[...], sc_ref[...], sh_ref[...]).astype(bf16)

    return pl.pallas_call(
        body, name="ln_fwd", grid=(t // tm,),
        in_specs=[_tok_spec(tm, D), _row_spec(D), _row_spec(D), _row_spec(D)],
        out_specs=_tok_spec(tm, D), out_shape=jax.ShapeDtypeStruct((t, D), bf16),
        compiler_params=_cp(1),
    )(x, g, scale, shift)


def _ln_bwd(x, g, scale, shift, dh, dxn):
    t = x.shape[0]
    tm = _tok_tile(t)

    def body(x_ref, g_ref, sc_ref, sh_ref, dh_ref, dxn_ref, dx_ref, dg_ref, dsc_ref, dsh_ref):
        _, vjp = jax.vjp(_adaln, x_ref[...], g_ref[...], sc_ref[...], sh_ref[...])
        dx, dg, dsc, dsh = vjp(dh_ref[...])
        dx_ref[...] = dxn_ref[...] + dx
        first = pl.program_id(0) == 0
        _acc_add(first, dg_ref, dg)
        _acc_add(first, dsc_ref, dsc)
        _acc_add(first, dsh_ref, dsh)

    row = jax.ShapeDtypeStruct((1, D), f32)
    return pl.pallas_call(
        body, name="ln_bwd", grid=(t // tm,),
        in_specs=[_tok_spec(tm, D), _row_spec(D), _row_spec(D), _row_spec(D), _tok_spec(tm, D), _tok_spec(tm, D)],
        out_specs=[_tok_spec(tm, D), _row_spec(D), _row_spec(D), _row_spec(D)],
        out_shape=[jax.ShapeDtypeStruct((t, D), f32), row, row, row],
        compiler_params=_cp(1),
    )(x, g, scale, shift, dh, dxn)


def _swiglu_fwd(gu):
    t = gu.shape[1]
    tm = _tok_tile(t)

    def body(g_ref, u_ref, a_ref):
        a_ref[...] = (jax.nn.silu(g_ref[...]) * u_ref[...]).astype(bf16)

    return pl.pallas_call(
        body, name="swiglu_fwd", grid=(t // tm, FFN_SLABS),
        in_specs=[pl.BlockSpec((1, tm, FFN_HS), lambda i, q: (q, i, 0)),
                  pl.BlockSpec((1, tm, FFN_HS), lambda i, q: (q + FFN_SLABS, i, 0))],
        out_specs=pl.BlockSpec((1, tm, FFN_HS), lambda i, q: (q, i, 0)),
        out_shape=jax.ShapeDtypeStruct((FFN_SLABS, t, FFN_HS), bf16), compiler_params=_cp(2),
    )(gu, gu)


def _swiglu_bwd(gu, da):
    t = gu.shape[1]
    tm = _tok_tile(t)

    def body(g_ref, u_ref, da_ref, o_ref):
        _, vjp = jax.vjp(lambda g, u: jax.nn.silu(g) * u, g_ref[...], u_ref[...])
        dg, du = vjp(da_ref[...])
        o_ref[...] = jnp.where(pl.program_id(1) < FFN_SLABS, dg, du).astype(bf16)

    gate = pl.BlockSpec((1, tm, FFN_HS), lambda i, j: (j % FFN_SLABS, i, 0))
    up = pl.BlockSpec((1, tm, FFN_HS), lambda i, j: (j % FFN_SLABS + FFN_SLABS, i, 0))
    return pl.pallas_call(
        body, name="swiglu_bwd", grid=(t // tm, 2 * FFN_SLABS), in_specs=[gate, up, gate],
        out_specs=pl.BlockSpec((1, tm, FFN_HS), lambda i, j: (j, i, 0)),
        out_shape=jax.ShapeDtypeStruct((2 * FFN_SLABS, t, FFN_HS), bf16), compiler_params=_cp(2),
    )(gu, gu, da)


def _resid_fwd(x, o, gate, coef):
    t = x.shape[0]
    tm = _tok_tile(t)

    def body(x_ref, o_ref, gt_ref, y_ref):
        y_ref[...] = x_ref[...] + (coef * (1.0 + gt_ref[...])) * o_ref[...]

    return pl.pallas_call(
        body, name="resid_fwd", grid=(t // tm,),
        in_specs=[_tok_spec(tm, D), _tok_spec(tm, D), _row_spec(D)],
        out_specs=_tok_spec(tm, D), out_shape=jax.ShapeDtypeStruct((t, D), f32),
        compiler_params=_cp(1),
    )(x, o, gate)


def _resid_bwd(dxn, o, gate, coef):
    t = dxn.shape[0]
    tm = _tok_tile(t)

    def body(dxn_ref, o_ref, gt_ref, do_ref, dgt_ref):
        dxn_v = dxn_ref[...]
        do_ref[...] = ((coef * (1.0 + gt_ref[...])) * dxn_v).astype(bf16)
        _acc_add(pl.program_id(0) == 0, dgt_ref, coef * jnp.sum(dxn_v * o_ref[...], axis=0, keepdims=True))

    return pl.pallas_call(
        body, name="resid_bwd", grid=(t // tm,),
        in_specs=[_tok_spec(tm, D), _tok_spec(tm, D), _row_spec(D)],
        out_specs=[_tok_spec(tm, D), _row_spec(D)],
        out_shape=[jax.ShapeDtypeStruct((t, D), bf16), jax.ShapeDtypeStruct((1, D), f32)],
        compiler_params=_cp(1),
    )(dxn, o, gate)


def _loss_head(y, target):
    t = y.shape[0]
    tm = _tok_tile(t)

    def body(y_ref, t_ref, dy_ref, l_ref):
        err = y_ref[...] - t_ref[...]
        dy_ref[...] = err * (1.0 / D)
        part = jnp.sum(jnp.sum(err * err, axis=0, keepdims=True), axis=1, keepdims=True) * (0.5 / D)
        _acc_add(pl.program_id(0) == 0, l_ref, jnp.broadcast_to(part, (1, 128)))

    return pl.pallas_call(
        body, name="loss_head", grid=(t // tm,),
        in_specs=[_tok_spec(tm, D), _tok_spec(tm, D)],
        out_specs=[_tok_spec(tm, D), _row_spec(128)],
        out_shape=[jax.ShapeDtypeStruct((t, D), f32), jax.ShapeDtypeStruct((1, 128), f32)],
        compiler_params=_cp(1),
    )(y, target)


def _mm_tile(t):
    return min(t, 1024)


def _ffn_fwd(x, ln, w_in, w_out, coef):
    g, shift, scale, gate = ln
    t = x.shape[0]
    tm = _mm_tile(t)
    sds = jax.ShapeDtypeStruct
    h = _ln_fwd(x, g, scale, shift)
    gu = _mmx(h, w_in, "nn", "ffn_in", (t // tm, N_DEV, 1),
              pl.BlockSpec((tm, D), lambda i, j, k: (i, 0)),
              pl.BlockSpec((1, D, FFN_HS), lambda i, j, k: (j, 0, 0)),
              pl.BlockSpec((1, tm, FFN_HS), lambda i, j, k: (j, i, 0)), sds((N_DEV, t, FFN_HS), f32))
    a = _swiglu_fwd(gu)
    o = _mmx(a, w_out, "nn", "ffn_out", (t // tm, 1, FFN_SLABS),
             pl.BlockSpec((1, tm, FFN_HS), lambda i, j, k: (k, i, 0)),
             pl.BlockSpec((FFN_HS, D), lambda i, j, k: (k, 0)),
             pl.BlockSpec((tm, D), lambda i, j, k: (i, 0)), sds((t, D), f32))
    xn = _resid_fwd(x, o, gate, coef)
    return xn, (x, h, gu, a, o)


def _ffn_bwd(dxn, saved, ln, w_in, w_out, coef):
    x, h, gu, a, o = saved
    g, shift, scale, gate = ln
    t = x.shape[0]
    tm = _mm_tile(t)
    tk = _pick(t, (512, 256, 128))
    sds = jax.ShapeDtypeStruct
    do, dgate = _resid_bwd(dxn, o, gate, coef)
    da = _mmx(do, w_out, "nt", "ffn_dact", (t // tm, FFN_SLABS, 1),
              pl.BlockSpec((tm, D), lambda i, j, k: (i, 0)),
              pl.BlockSpec((FFN_HS, D), lambda i, j, k: (j, 0)),
              pl.BlockSpec((1, tm, FFN_HS), lambda i, j, k: (j, i, 0)), sds((FFN_SLABS, t, FFN_HS), f32))
    dw_out = _mmx(a, do, "tn", "ffn_dwout", (FFN_SLABS, 1, t // tk),
                  pl.BlockSpec((1, tk, FFN_HS), lambda i, j, k: (i, k, 0)),
                  pl.BlockSpec((tk, D), lambda i, j, k: (k, 0)),
                  pl.BlockSpec((FFN_HS, D), lambda i, j, k: (i, 0)), sds((D_FF, D), f32))
    dgu = _swiglu_bwd(gu, da)
    dh = _mmx(dgu, w_in, "nt", "ffn_dh", (t // tm, 1, N_DEV),
              pl.BlockSpec((1, tm, FFN_HS), lambda i, j, k: (k, i, 0)),
              pl.BlockSpec((1, D, FFN_HS), lambda i, j, k: (k, 0, 0)),
              pl.BlockSpec((tm, D), lambda i, j, k: (i, 0)), sds((t, D), f32))
    dw_in = _mmx(h, dgu, "tn", "ffn_dwin", (N_DEV, 1, t // tk),
                 pl.BlockSpec((tk, D), lambda i, j, k: (k, 0)),
                 pl.BlockSpec((1, tk, FFN_HS), lambda i, j, k: (i, k, 0)),
                 pl.BlockSpec((1, D, FFN_HS), lambda i, j, k: (i, 0, 0)), sds((N_DEV, D, FFN_HS), f32))
    dx, dg, dscale, dshift = _ln_bwd(x, g, scale, shift, dh, dxn)
    return dx, (dg, dshift, dscale, dgate), dw_in, dw_out


def _roll_rows(a, k):
    n = a.shape[0]
    return pltpu.roll(a, k % n, 0)


def _pool_windows(hx, first_row, reverse):
    outs = []
    for gi, w in enumerate(POOL_WINDOWS):
        acc = hx[:, gi * POOL_GROUP:(gi + 1) * POOL_GROUP]
        k = 1
        while k < w:
            acc = acc + _roll_rows(acc, -k if reverse else k)
            k *= 2
        outs.append(acc)
    return outs


def _pool_cnt(t_idx, w):
    return jnp.minimum(t_idx + 1, w).astype(f32)


def _pool_pooled(x_ref, xp_ref, g, scale, shift, i, tm):
    h = _adaln(x_ref[...], g, scale, shift)
    hp = _adaln(xp_ref[...], g, scale, shift)
    hp = jnp.where(i == 0, 0.0, hp)
    hx = jnp.concatenate([hp, h], axis=0)
    sums = _pool_windows(hx, 0, False)
    t_idx = i * tm + lax.broadcasted_iota(jnp.int32, (tm, 1), 0)
    pooled = []
    for gi, w in enumerate(POOL_WINDOWS):
        s = sums[gi][POOL_HALO:, :]
        pooled.append(s / _pool_cnt(t_idx, w) - h[:, gi * POOL_GROUP:(gi + 1) * POOL_GROUP])
    return h, pooled


def _pool_specs(t, tm):
    per = tm // POOL_HALO
    prev = pl.BlockSpec((POOL_HALO, D), lambda i: (jnp.maximum(i * per - 1, 0), 0))
    return [_tok_spec(tm, D), prev, _row_spec(D), _row_spec(D), _row_spec(D),
            pl.BlockSpec((4, POOL_GROUP, POOL_GROUP), lambda i: (0, 0, 0)), _row_spec(D), _row_spec(D)]


def _pool_fwd(x, ln, w, pscale):
    g, shift, scale, gate = ln
    t = x.shape[0]
    tm = _tok_tile(t)

    def body(x_ref, xp_ref, g_ref, sc_ref, sh_ref, w_ref, ps_ref, gt_ref, y_ref):
        i = pl.program_id(0)
        _, pooled = _pool_pooled(x_ref, xp_ref, g_ref[...], sc_ref[...], sh_ref[...], i, tm)
        mixed = [jnp.dot(pooled[gi].astype(bf16), w_ref[gi], preferred_element_type=f32) for gi in range(4)]
        y = jnp.concatenate(mixed, axis=1) * ps_ref[...]
        y_ref[...] = x_ref[...] + (1.0 + gt_ref[...]) * y

    return pl.pallas_call(
        body, name="pool_fwd", grid=(t // tm,), in_specs=_pool_specs(t, tm),
        out_specs=_tok_spec(tm, D), out_shape=jax.ShapeDtypeStruct((t, D), f32),
        compiler_params=_cp(1),
    )(x, x, g, scale, shift, w, pscale, gate)


def _pool_bwd(dxn, x, ln, w, pscale):
    g, shift, scale, gate = ln
    t = x.shape[0]
    tm = _tok_tile(t)
    nt = t // tm
    per = tm // POOL_HALO

    def body_a(x_ref, xp_ref, g_ref, sc_ref, sh_ref, w_ref, ps_ref, gt_ref, dxn_ref,
               dp_ref, dw_ref, dps_ref, dgt_ref):
        i = pl.program_id(0)
        first = i == 0
        _, pooled = _pool_pooled(x_ref, xp_ref, g_ref[...], sc_ref[...], sh_ref[...], i, tm)
        dxn_v = dxn_ref[...]
        dy = (1.0 + gt_ref[...]) * dxn_v
        dmixed = dy * ps_ref[...]
        mixed, dps = [], []
        for gi in range(4):
            sl = slice(gi * POOL_GROUP, (gi + 1) * POOL_GROUP)
            pb = pooled[gi].astype(bf16)
            dmb = dmixed[:, sl].astype(bf16)
            mixed.append(jnp.dot(pb, w_ref[gi], preferred_element_type=f32))
            dp_ref[:, sl] = lax.dot_general(dmb, w_ref[gi], _DN["nt"], preferred_element_type=f32)
            dwg = lax.dot_general(pb, dmb, _DN["tn"], preferred_element_type=f32)

            @pl.when(first)
            def _():
                dw_ref[gi] = dwg

            @pl.when(jnp.logical_not(first))
            def _():
                dw_ref[gi] += dwg
        mixed = jnp.concatenate(mixed, axis=1)
        _acc_add(first, dps_ref, jnp.sum(dy * mixed, axis=0, keepdims=True))
        _acc_add(first, dgt_ref, jnp.sum(dxn_v * (mixed * ps_ref[...]), axis=0, keepdims=True))

    row = jax.ShapeDtypeStruct((1, D), f32)
    dpooled, dw, dps, dgate = pl.pallas_call(
        body_a, name="pool_bwd_a", grid=(nt,), in_specs=_pool_specs(t, tm) + [_tok_spec(tm, D)],
        out_specs=[_tok_spec(tm, D), pl.BlockSpec((4, POOL_GROUP, POOL_GROUP), lambda i: (0, 0, 0)),
                   _row_spec(D), _row_spec(D)],
        out_shape=[jax.ShapeDtypeStruct((t, D), f32), jax.ShapeDtypeStruct((4, POOL_GROUP, POOL_GROUP), f32), row, row],
        compiler_params=_cp(1),
    )(x, x, g, scale, shift, w, pscale, gate, dxn)

    def body_b(dp_ref, dpn_ref, x_ref, g_ref, sc_ref, sh_ref, dxn_ref, dx_ref, dg_ref, dsc_ref, dsh_ref):
        i = pl.program_id(0)
        dp = dp_ref[...]
        dpn = jnp.where(i == nt - 1, 0.0, dpn_ref[...])
        t_idx = i * tm + lax.broadcasted_iota(jnp.int32, (tm + POOL_HALO, 1), 0)
        ex = jnp.concatenate([dp, dpn], axis=0)
        parts = []
        for gi, w_ in enumerate(POOL_WINDOWS):
            parts.append(ex[:, gi * POOL_GROUP:(gi + 1) * POOL_GROUP] / _pool_cnt(t_idx, w_))
        sums = _pool_windows(jnp.concatenate(parts, axis=1), 0, True)
        dh = jnp.concatenate([s[:tm, :] for s in sums], axis=1) - dp
        _, vjp = jax.vjp(_adaln, x_ref[...], g_ref[...], sc_ref[...], sh_ref[...])
        dx, dg, dsc, dsh = vjp(dh)
        dx_ref[...] = dxn_ref[...] + dx
        first = i == 0
        _acc_add(first, dg_ref, dg)
        _acc_add(first, dsc_ref, dsc)
        _acc_add(first, dsh_ref, dsh)

    nxt = pl.BlockSpec((POOL_HALO, D), lambda i: (jnp.minimum((i + 1) * per, t // POOL_HALO - 1), 0))
    dx, dg, dscale, dshift = pl.pallas_call(
        body_b, name="pool_bwd_b", grid=(nt,),
        in_specs=[_tok_spec(tm, D), nxt, _tok_spec(tm, D), _row_spec(D), _row_spec(D), _row_spec(D), _tok_spec(tm, D)],
        out_specs=[_tok_spec(tm, D), _row_spec(D), _row_spec(D), _row_spec(D)],
        out_shape=[jax.ShapeDtypeStruct((t, D), f32), row, row, row],
        compiler_params=_cp(1),
    )(dpooled, dpooled, x, g, scale, shift, dxn)
    return dx, (dg, dshift, dscale, dgate), dw, dps


def _conv_taps(czx, cw):
    return cw[0:1, :] * _roll_rows(czx, 2) + cw[1:2, :] * _roll_rows(czx, 1) + cw[2:3, :] * czx


def _conv_fwd(p, cw):
    t = p.shape[0]
    tm = _tok_tile(t)
    per = tm // CONV_HALO

    def body(p_ref, pp_ref, cw_ref, q_ref):
        i = pl.program_id(0)
        cz = p_ref[:, D:2 * D] * p_ref[:, 2 * D:3 * D]
        czp = jnp.where(i == 0, 0.0, pp_ref[:, D:2 * D] * pp_ref[:, 2 * D:3 * D])
        conv = _conv_taps(jnp.concatenate([czp, cz], axis=0), cw_ref[...])[CONV_HALO:, :]
        q_ref[...] = (p_ref[:, 0:D] * conv).astype(bf16)

    prev = pl.BlockSpec((CONV_HALO, 3 * D), lambda i: (jnp.maximum(i * per - 1, 0), 0))
    return pl.pallas_call(
        body, name="conv_fwd", grid=(t // tm,),
        in_specs=[_tok_spec(tm, 3 * D), prev, _row_spec(D, 8)],
        out_specs=_tok_spec(tm, D), out_shape=jax.ShapeDtypeStruct((t, D), bf16),
        compiler_params=_cp(1),
    )(p, p, cw)


def _conv_bwd(p, cw, dq):
    t = p.shape[0]
    tm = _tok_tile(t)
    nt = t // tm
    per = tm // CONV_HALO

    def body(p_ref, pp_ref, pn_ref, cw_ref, dq_ref, dqn_ref, dp_ref, dcw_ref):
        i = pl.program_id(0)
        cw_v = cw_ref[...]
        b, c, z = p_ref[:, 0:D], p_ref[:, D:2 * D], p_ref[:, 2 * D:3 * D]
        cz = c * z
        czp = jnp.where(i == 0, 0.0, pp_ref[:, D:2 * D] * pp_ref[:, 2 * D:3 * D])
        czx = jnp.concatenate([czp, cz], axis=0)
        conv = _conv_taps(czx, cw_v)[CONV_HALO:, :]
        dq_v = dq_ref[...]
        dconv = dq_v * b
        dconv_n = jnp.where(i == nt - 1, 0.0, dqn_ref[...] * pn_ref[:, 0:D])
        dcx = jnp.concatenate([dconv, dconv_n], axis=0)
        dcz = (cw_v[2:3, :] * dcx + cw_v[1:2, :] * _roll_rows(dcx, -1) + cw_v[0:1, :] * _roll_rows(dcx, -2))[:tm, :]
        dp_ref[:, 0:D] = (dq_v * conv).astype(bf16)
        dp_ref[:, D:2 * D] = (dcz * z).astype(bf16)
        dp_ref[:, 2 * D:3 * D] = (dcz * c).astype(bf16)
        dw2 = jnp.sum(dconv * cz, axis=0, keepdims=True)
        dw1 = jnp.sum(dconv * _roll_rows(czx, 1)[CONV_HALO:, :], axis=0, keepdims=True)
        dw0 = jnp.sum(dconv * _roll_rows(czx, 2)[CONV_HALO:, :], axis=0, keepdims=True)
        _acc_add(i == 0, dcw_ref, jnp.concatenate([dw0, dw1, dw2, jnp.zeros((5, D), f32)], axis=0))

    prev = pl.BlockSpec((CONV_HALO, 3 * D), lambda i: (jnp.maximum(i * per - 1, 0), 0))
    last = t // CONV_HALO - 1
    nxt3 = pl.BlockSpec((CONV_HALO, 3 * D), lambda i: (jnp.minimum((i + 1) * per, last), 0))
    nxt1 = pl.BlockSpec((CONV_HALO, D), lambda i: (jnp.minimum((i + 1) * per, last), 0))
    return pl.pallas_call(
        body, name="conv_bwd", grid=(nt,),
        in_specs=[_tok_spec(tm, 3 * D), prev, nxt3, _row_spec(D, 8), _tok_spec(tm, D), nxt1],
        out_specs=[_tok_spec(tm, 3 * D), _row_spec(D, 8)],
        out_shape=[jax.ShapeDtypeStruct((t, 3 * D), bf16), jax.ShapeDtypeStruct((8, D), f32)],
        compiler_params=_cp(1),
    )(p, p, p, cw, dq, dq)


def _convmix_fwd(x, ln, w_in, cw, w_out):
    g, shift, scale, gate = ln
    t = x.shape[0]
    tm = _mm_tile(t)
    cs = w_in.shape[2]
    h = _ln_fwd(x, g, scale, shift)
    p = _mmx(h, w_in, "nn", "conv_in", (t // tm, N_DEV, 1),
             pl.BlockSpec((tm, D), lambda i, j, k: (i, 0)),
             pl.BlockSpec((1, D, cs), lambda i, j, k: (j, 0, 0)),
             pl.BlockSpec((tm, cs), lambda i, j, k: (i, j)), jax.ShapeDtypeStruct((t, N_DEV * cs), f32))
    q = _conv_fwd(p, cw)
    y = _mm(q, w_out, "nn", f32, "conv_out")
    return _resid_fwd(x, y, gate, 1.0), (x, h, p, q, y)


def _convmix_bwd(dxn, saved, ln, w_in, cw, w_out):
    x, h, p, q, y = saved
    g, shift, scale, gate = ln
    dy, dgate = _resid_bwd(dxn, y, gate, 1.0)
    dq = _mm(dy, w_out, "nt", f32, "conv_dq")
    dw_out = _mm(q, dy, "tn", f32, "conv_dwout")
    dp, dcw = _conv_bwd(p, cw, dq)
    t = x.shape[0]
    tm = _mm_tile(t)
    tk = _pick(t, (512, 256, 128))
    cs = w_in.shape[2]
    dh = _mmx(dp, w_in, "nt", "conv_dh", (t // tm, 1, N_DEV),
              pl.BlockSpec((tm, cs), lambda i, j, k: (i, k)),
              pl.BlockSpec((1, D, cs), lambda i, j, k: (k, 0, 0)),
              pl.BlockSpec((tm, D), lambda i, j, k: (i, 0)), jax.ShapeDtypeStruct((t, D), f32))
    dw_in = _mmx(h, dp, "tn", "conv_dwin", (N_DEV, 1, t // tk),
                 pl.BlockSpec((tk, D), lambda i, j, k: (k, 0)),
                 pl.BlockSpec((tk, cs), lambda i, j, k: (k, i)),
                 pl.BlockSpec((1, D, cs), lambda i, j, k: (i, 0, 0)), jax.ShapeDtypeStruct((N_DEV, D, cs), f32))
    dx, dg, dscale, dshift = _ln_bwd(x, g, scale, shift, dh, dxn)
    return dx, (dg, dshift, dscale, dgate), dw_in, dcw[0:3], dw_out


def _exact_dot(tri, v):
    v1 = v.astype(bf16)
    r1 = v - v1.astype(f32)
    v2 = r1.astype(bf16)
    v3 = (r1 - v2.astype(f32)).astype(bf16)
    d = lambda p: jnp.dot(tri, p, preferred_element_type=f32)
    return d(v1) + d(v2) + d(v3)


def _fox_cumf(fl, b_f):
    t = fl.shape[0]
    tc = min(t, 256)

    def body(fl_ref, b_ref, f_ref, carry_ref):
        i = pl.program_id(0)

        @pl.when(i == 0)
        def _():
            carry_ref[...] = jnp.zeros_like(carry_ref)

        lf = jax.nn.log_sigmoid(fl_ref[...] + b_ref[...])
        r = lax.broadcasted_iota(jnp.int32, (tc, tc), 0)
        c = lax.broadcasted_iota(jnp.int32, (tc, tc), 1)
        tri = (r >= c).astype(bf16)
        fc = _exact_dot(tri, lf) + carry_ref[0:1, :]
        f_ref[...] = fc
        carry_ref[0:1, :] = fc[tc - 1:tc, :]

    return pl.pallas_call(
        body, name="fox_cumf", grid=(t // tc,),
        in_specs=[_tok_spec(tc, 128), _row_spec(128)],
        out_specs=_tok_spec(tc, 128), out_shape=jax.ShapeDtypeStruct((t, 128), f32),
        scratch_shapes=[pltpu.VMEM((8, 128), f32)], compiler_params=_cp(1),
    )(fl, b_f)


def _fox_cumf_bwd(df, fl, b_f):
    t = fl.shape[0]
    tc = min(t, 256)
    nt = t // tc

    def body(df_ref, fl_ref, b_ref, dfl_ref, db_ref, carry_ref):
        i = pl.program_id(0)

        @pl.when(i == 0)
        def _():
            carry_ref[...] = jnp.zeros_like(carry_ref)

        r = lax.broadcasted_iota(jnp.int32, (tc, tc), 0)
        c = lax.broadcasted_iota(jnp.int32, (tc, tc), 1)
        tri = (r <= c).astype(bf16)
        dlf = _exact_dot(tri, df_ref[...]) + carry_ref[0:1, :]
        carry_ref[0:1, :] = dlf[0:1, :]
        dfl = dlf * jax.nn.sigmoid(-(fl_ref[...] + b_ref[...]))
        dfl_ref[...] = dfl
        _acc_add(i == 0, db_ref, jnp.sum(dfl, axis=0, keepdims=True))

    return pl.pallas_call(
        body, name="fox_cumf_bwd", grid=(nt,),
        in_specs=[_tok_spec(tc, 128, nt, True), _tok_spec(tc, 128, nt, True), _row_spec(128)],
        out_specs=[_tok_spec(tc, 128, nt, True), _row_spec(128)],
        out_shape=[jax.ShapeDtypeStruct((t, 128), f32), jax.ShapeDtypeStruct((1, 128), f32)],
        scratch_shapes=[pltpu.VMEM((8, 128), f32)], compiler_params=_cp(1),
    )(df, fl, b_f)


def _head_norm(v, gain, mult):
    return v * lax.rsqrt(jnp.mean(v * v, axis=-1, keepdims=True) + NORM_EPS) * gain * mult


def _fox_qknorm(qkv, q_gain, k_gain):
    t = qkv.shape[2]
    tm = _tok_tile(t)

    def body(q_ref, k_ref, v_ref, qg_ref, kg_ref, o_ref):
        o_ref[0, 0] = _head_norm(q_ref[0, 0], qg_ref[...], FOX_HEAD_DIM ** -0.5).astype(bf16)
        o_ref[1, 0] = _head_norm(k_ref[0, 0], kg_ref[...], 1.0).astype(bf16)
        o_ref[2, 0] = v_ref[0, 0].astype(bf16)

    blk = lambda s: pl.BlockSpec((1, 1, tm, FOX_HEAD_DIM), lambda h, i: (s, h, i, 0))
    gspec = pl.BlockSpec((1, FOX_HEAD_DIM), lambda h, i: (0, 0))
    return pl.pallas_call(
        body, name="fox_qknorm", grid=(FOX_HEADS, t // tm),
        in_specs=[blk(0), blk(1), blk(2), gspec, gspec],
        out_specs=pl.BlockSpec((3, 1, tm, FOX_HEAD_DIM), lambda h, i: (0, h, i, 0)),
        out_shape=jax.ShapeDtypeStruct((3, FOX_HEADS, t, FOX_HEAD_DIM), bf16),
        compiler_params=_cp(2),
    )(qkv, qkv, qkv, q_gain, k_gain)


def _fox_qknorm_bwd(qkv, q_gain, k_gain, dqn, dkn):
    t = qkv.shape[2]
    tm = _tok_tile(t)

    def body(q_ref, k_ref, qg_ref, kg_ref, dqn_ref, dkn_ref, o_ref, dqg_ref, dkg_ref):
        first = jnp.logical_and(pl.program_id(0) == 0, pl.program_id(1) == 0)
        _, vq = jax.vjp(lambda v, gn: _head_norm(v, gn, FOX_HEAD_DIM ** -0.5), q_ref[0, 0], qg_ref[...])
        dq, dqg = vq(dqn_ref[0])
        _, vk = jax.vjp(lambda v, gn: _head_norm(v, gn, 1.0), k_ref[0, 0], kg_ref[...])
        dk, dkg = vk(dkn_ref[0])
        o_ref[0, 0] = dq
        o_ref[1, 0] = dk
        _acc_add(first, dqg_ref, dqg)
        _acc_add(first, dkg_ref, dkg)

    blk = lambda s: pl.BlockSpec((1, 1, tm, FOX_HEAD_DIM), lambda h, i: (s, h, i, 0))
    hblk = pl.BlockSpec((1, tm, FOX_HEAD_DIM), lambda h, i: (h, i, 0))
    gspec = pl.BlockSpec((1, FOX_HEAD_DIM), lambda h, i: (0, 0))
    gshape = jax.ShapeDtypeStruct((1, FOX_HEAD_DIM), f32)
    return pl.pallas_call(
        body, name="fox_qknorm_bwd", grid=(FOX_HEADS, t // tm),
        in_specs=[blk(0), blk(1), gspec, gspec, hblk, hblk],
        out_specs=[pl.BlockSpec((2, 1, tm, FOX_HEAD_DIM), lambda h, i: (0, h, i, 0)), gspec, gspec],
        out_shape=[jax.ShapeDtypeStruct((2, FOX_HEADS, t, FOX_HEAD_DIM), f32), gshape, gshape],
        compiler_params=_cp(2),
    )(qkv, qkv, q_gain, k_gain, dqn, dkn)


def _fox_block(t):
    return min(t, 512)


def _fox_scores(q, kj, fref, fcol_j, diag):
    s = lax.dot_general(q, kj, _DN["nt"], preferred_element_type=f32) + (fref - fcol_j)
    if diag:
        tb = s.shape[0]
        r = lax.broadcasted_iota(jnp.int32, (tb, tb), 0)
        c = lax.broadcasted_iota(jnp.int32, (tb, tb), 1)
        s = jnp.where(r >= c, s, -jnp.inf)
    return s


def _fox_attn_fwd(qkvn, fcol, fref):
    t = qkvn.shape[2]
    tb = _fox_block(t)
    nq = t // tb
    dh = FOX_HEAD_DIM

    def body(q_ref, k_ref, v_ref, fc_ref, fr_ref, o_ref, lse_ref):
        i = pl.program_id(1)
        q = q_ref[0, 0]
        fref_i = fr_ref[0, pl.ds(i, 1), 0:1]

        def step(j, carry, diag):
            m, l, acc = carry
            off = pl.multiple_of(j * tb, tb)
            kj = k_ref[0, 0, pl.ds(off, tb), :]
            vj = v_ref[0, 0, pl.ds(off, tb), :]
            s = _fox_scores(q, kj, fref_i, fc_ref[0, :, pl.ds(off, tb)], diag)
            m_new = jnp.maximum(m, jnp.max(s, axis=-1, keepdims=True))
            alpha = jnp.exp(m - m_new)
            p = jnp.exp(s - m_new)
            l = alpha * l + jnp.sum(p, axis=-1, keepdims=True)
            p_hi = p.astype(bf16)
            p_lo = (p - p_hi.astype(f32)).astype(bf16)
            pv = jnp.dot(p_hi, vj, preferred_element_type=f32) + jnp.dot(p_lo, vj, preferred_element_type=f32)
            return m_new, l, alpha * acc + pv

        init = (jnp.full((tb, 1), -jnp.inf, f32), jnp.zeros((tb, 1), f32), jnp.zeros((tb, dh), f32))
        carry = lax.fori_loop(0, i, lambda j, c: step(j, c, False), init)
        m, l, acc = step(i, carry, True)
        o_ref[0] = acc / l
        lse_ref[0] = jnp.broadcast_to(m + jnp.log(l), (tb, 128))

    return pl.pallas_call(
        body, name="fox_attn_fwd", grid=(FOX_HEADS, nq),
        in_specs=[pl.BlockSpec((1, 1, tb, dh), lambda h, i: (0, h, i, 0)),
                  pl.BlockSpec((1, 1, t, dh), lambda h, i: (1, h, 0, 0)),
                  pl.BlockSpec((1, 1, t, dh), lambda h, i: (2, h, 0, 0)),
                  pl.BlockSpec((1, 1, t), lambda h, i: (h, 0, 0)),
                  pl.BlockSpec((1, nq, 128), lambda h, i: (h, 0, 0))],
        out_specs=[pl.BlockSpec((1, tb, dh), lambda h, i: (h, i, 0)),
                   pl.BlockSpec((1, tb, 128), lambda h, i: (h, i, 0))],
        out_shape=[jax.ShapeDtypeStruct((FOX_HEADS, t, dh), f32), jax.ShapeDtypeStruct((FOX_HEADS, t, 128), f32)],
        compiler_params=_cp(2),
    )(qkvn, qkvn, qkvn, fcol, fref)


def _fox_attn_bwd_q(qkvn, fcol, fref, o, lse, do):
    t = qkvn.shape[2]
    tb = _fox_block(t)
    nq = t // tb
    dh = FOX_HEAD_DIM

    def body(q_ref, k_ref, v_ref, fc_ref, fr_ref, o_ref, lse_ref, do_ref, dq_ref, dl_ref):
        i = pl.program_id(1)
        q = q_ref[0, 0]
        fref_i = fr_ref[0, pl.ds(i, 1), 0:1]
        do_v = do_ref[0]
        dob = do_v.astype(bf16)
        delta = jnp.sum(dob.astype(f32) * o_ref[0], axis=-1, keepdims=True)
        lse_i = lse_ref[0, :, 0:1]

        def step(j, dq, diag):
            off = pl.multiple_of(j * tb, tb)
            kj = k_ref[0, 0, pl.ds(off, tb), :]
            vj = v_ref[0, 0, pl.ds(off, tb), :]
            s = _fox_scores(q, kj, fref_i, fc_ref[0, :, pl.ds(off, tb)], diag)
            p = jnp.exp(s - lse_i)
            dp = lax.dot_general(dob, vj, _DN["nt"], preferred_element_type=f32)
            ds = p * (dp - delta)
            return dq + jnp.dot(ds.astype(bf16), kj, preferred_element_type=f32)

        dq = lax.fori_loop(0, i, lambda j, c: step(j, c, False), jnp.zeros((tb, dh), f32))
        dq_ref[0] = step(i, dq, True)
        dl_ref[0] = jnp.broadcast_to(delta, (tb, 128))

    hb = pl.BlockSpec((1, tb, dh), lambda h, i: (h, i, 0))
    hb128 = pl.BlockSpec((1, tb, 128), lambda h, i: (h, i, 0))
    return pl.pallas_call(
        body, name="fox_attn_bwd_q", grid=(FOX_HEADS, nq),
        in_specs=[pl.BlockSpec((1, 1, tb, dh), lambda h, i: (0, h, i, 0)),
                  pl.BlockSpec((1, 1, t, dh), lambda h, i: (1, h, 0, 0)),
                  pl.BlockSpec((1, 1, t, dh), lambda h, i: (2, h, 0, 0)),
                  pl.BlockSpec((1, 1, t), lambda h, i: (h, 0, 0)),
                  pl.BlockSpec((1, nq, 128), lambda h, i: (h, 0, 0)),
                  hb, hb128, hb],
        out_specs=[hb, hb128],
        out_shape=[jax.ShapeDtypeStruct((FOX_HEADS, t, dh), f32), jax.ShapeDtypeStruct((FOX_HEADS, t, 128), f32)],
        compiler_params=_cp(2),
    )(qkvn, qkvn, qkvn, fcol, fref, o, lse, do)


def _fox_attn_bwd_kv(qkvn, fcol, fref, lse, delta, dob):
    t = qkvn.shape[2]
    tb = _fox_block(t)
    nq = t // tb
    dh = FOX_HEAD_DIM

    def body(q_ref, k_ref, v_ref, fc_ref, fr_ref, lse_ref, dl_ref, do_ref, dk_ref, dv_ref, df_ref):
        j = pl.program_id(1)
        kj = k_ref[0, 0]
        vj = v_ref[0, 0]
        fcol_j = fc_ref[0]

        def step(i, carry, diag):
            dk, dv, df = carry
            off = pl.multiple_of(i * tb, tb)
            qi = q_ref[0, 0, pl.ds(off, tb), :]
            doi = do_ref[0, pl.ds(off, tb), :]
            s = _fox_scores(qi, kj, fr_ref[0, pl.ds(i, 1), 0:1], fcol_j, diag)
            p = jnp.exp(s - lse_ref[0, pl.ds(off, tb), 0:1])
            dv = dv + lax.dot_general(p.astype(bf16), doi, _DN["tn"], preferred_element_type=f32)
            dp = lax.dot_general(doi, vj, _DN["nt"], preferred_element_type=f32)
            ds = p * (dp - dl_ref[0, pl.ds(off, tb), 0:1])
            dk = dk + lax.dot_general(ds.astype(bf16), qi, _DN["tn"], preferred_element_type=f32)
            return dk, dv, df - jnp.sum(ds, axis=0, keepdims=True)

        init = (jnp.zeros((tb, dh), f32), jnp.zeros((tb, dh), f32), jnp.zeros((1, tb), f32))
        carry = step(j, init, True)
        dk, dv, df = lax.fori_loop(j + 1, nq, lambda i, c: step(i, c, False), carry)
        dk_ref[0] = dk
        dv_ref[0] = dv
        df_ref[0] = df

    hb = pl.BlockSpec((1, tb, dh), lambda h, j: (h, j, 0))
    full = pl.BlockSpec((1, t, dh), lambda h, j: (h, 0, 0))
    full128 = pl.BlockSpec((1, t, 128), lambda h, j: (h, 0, 0))
    hshape = jax.ShapeDtypeStruct((FOX_HEADS, t, dh), f32)
    return pl.pallas_call(
        body, name="fox_attn_bwd_kv", grid=(FOX_HEADS, nq),
        in_specs=[pl.BlockSpec((1, 1, t, dh), lambda h, j: (0, h, 0, 0)),
                  pl.BlockSpec((1, 1, tb, dh), lambda h, j: (1, h, j, 0)),
                  pl.BlockSpec((1, 1, tb, dh), lambda h, j: (2, h, j, 0)),
                  pl.BlockSpec((1, 1, tb), lambda h, j: (h, 0, j)),
                  pl.BlockSpec((1, nq, 128), lambda h, j: (h, 0, 0)),
                  full128, full128, full],
        out_specs=[hb, hb, pl.BlockSpec((1, 1, tb), lambda h, j: (h, 0, j))],
        out_shape=[hshape, hshape, jax.ShapeDtypeStruct((FOX_HEADS, 1, t), f32)],
        compiler_params=_cp(2),
    )(qkvn, qkvn, qkvn, fcol, fref, lse, delta, dob)


def _heads_of(a, n):
    t = a.shape[0]
    return a.reshape(t, n, FOX_HEADS, FOX_HEAD_DIM).transpose(1, 2, 0, 3)


def _fox_fwd(x, ln, w_in, b_f, q_gain, k_gain, w_o):
    g, shift, scale, gate = ln
    t = x.shape[0]
    tb = _fox_block(t)
    h = _ln_fwd(x, g, scale, shift)
    proj = _mm(h, w_in, "nn", f32, "fox_in")
    qkv = _heads_of(proj[:, :3 * D], 3)
    fl = proj[:, 3 * D:3 * D + 128]
    fcum = _fox_cumf(fl, b_f)
    fcol = fcum[:, :FOX_HEADS].T.reshape(FOX_HEADS, 1, t)
    fref = jnp.broadcast_to(fcol[:, 0, ::tb][:, :, None], (FOX_HEADS, t // tb, 128))
    qkvn = _fox_qknorm(qkv, q_gain, k_gain)
    o, lse = _fox_attn_fwd(qkvn, fcol, fref)
    ob = o.transpose(1, 0, 2).reshape(t, D).astype(bf16)
    y = _mm(ob, w_o, "nn", f32, "fox_out")
    return _resid_fwd(x, y, gate, 1.0), (x, h, qkv, fl, fcol, fref, qkvn, o, lse, ob, y)


def _fox_bwd(dxn, saved, ln, w_in, b_f, q_gain, k_gain, w_o):
    x, h, qkv, fl, fcol, fref, qkvn, o, lse, ob, y = saved
    g, shift, scale, gate = ln
    t = x.shape[0]
    dy, dgate = _resid_bwd(dxn, y, gate, 1.0)
    do_flat = _mm(dy, w_o, "nt", f32, "fox_do")
    dw_o = _mm(ob, dy, "tn", f32, "fox_dwo")
    do = do_flat.reshape(t, FOX_HEADS, FOX_HEAD_DIM).transpose(1, 0, 2)
    dqn, delta = _fox_attn_bwd_q(qkvn, fcol, fref, o, lse, do)
    dkn, dv, dfcol = _fox_attn_bwd_kv(qkvn, fcol, fref, lse, delta, do.astype(bf16))
    dqk, dqg, dkg = _fox_qknorm_bwd(qkv, q_gain, k_gain, dqn, dkn)
    df = jnp.pad(dfcol.reshape(FOX_HEADS, t).T, ((0, 0), (0, 128 - FOX_HEADS)))
    dfl, db_f = _fox_cumf_bwd(df, fl, b_f)
    dqkv = jnp.concatenate([dqk, dv[None]], axis=0).transpose(2, 0, 1, 3).reshape(t, 3 * D)
    dproj = jnp.concatenate([dqkv.astype(bf16), dfl.astype(bf16)], axis=1)
    dh = _mm(dproj, w_in, "nt", f32, "fox_dh")
    dw_in = _mm(h, dproj, "tn", f32, "fox_dwin")
    dx, dg, dscale, dshift = _ln_bwd(x, g, scale, shift, dh, dxn)
    return dx, (dg, dshift, dscale, dgate), dw_in, db_f, dqg, dkg, dw_o


def _s5_disc(lam_re, lam_im, log_dt, b_re, b_im):
    dt = jnp.exp(log_dt)
    mag = jnp.exp(lam_re * dt)
    lb_re, lb_im = mag * jnp.cos(lam_im * dt), mag * jnp.sin(lam_im * dt)
    den = lam_re * lam_re + lam_im * lam_im
    nr, ni = lb_re - 1.0, lb_im
    k_re = (nr * lam_re + ni * lam_im) / den
    k_im = (ni * lam_re - nr * lam_im) / den
    return lb_re, lb_im, k_re * b_re - k_im * b_im, k_re * b_im + k_im * b_re


def _s5_prep(lam_re, lam_im, log_dt, b_re, b_im):
    def body(ar_ref, ai_ref, dt_ref, br_ref, bi_ref, lr_ref, li_ref, bbr_ref, bbi_ref):
        lr, li, bbr, bbi = _s5_disc(ar_ref[...], ai_ref[...], dt_ref[...], br_ref[...], bi_ref[...])
        lr_ref[...] = lr
        li_ref[...] = li
        bbr_ref[...] = bbr
        bbi_ref[...] = bbi

    small = jax.ShapeDtypeStruct(lam_re.shape, f32)
    bigs = jax.ShapeDtypeStruct(b_re.shape, f32)
    return pl.pallas_call(body, name="s5_prep", out_shape=[small, small, bigs, bigs])(lam_re, lam_im, log_dt, b_re, b_im)


def _s5_prep_bwd(lam_re, lam_im, log_dt, b_re, b_im, dlr, dli, dbbr, dbbi):
    def body(ar_ref, ai_ref, dt_ref, br_ref, bi_ref, dlr_ref, dli_ref, dbbr_ref, dbbi_ref,
             dar_ref, dai_ref, ddt_ref, dbr_ref, dbi_ref):
        _, vjp = jax.vjp(_s5_disc, ar_ref[...], ai_ref[...], dt_ref[...], br_ref[...], bi_ref[...])
        dar, dai, ddt, dbr, dbi = vjp((dlr_ref[...], dli_ref[...], dbbr_ref[...], dbbi_ref[...]))
        dar_ref[...] = dar
        dai_ref[...] = dai
        ddt_ref[...] = jnp.broadcast_to(jnp.sum(ddt, axis=-1, keepdims=True), ddt.shape)
        dbr_ref[...] = dbr
        dbi_ref[...] = dbi

    small = jax.ShapeDtypeStruct(lam_re.shape, f32)
    bigs = jax.ShapeDtypeStruct(b_re.shape, f32)
    return pl.pallas_call(body, name="s5_prep_bwd", out_shape=[small, small, small, bigs, bigs])(
        lam_re, lam_im, log_dt, b_re, b_im, dlr, dli, dbbr, dbbi)


def _s5_tile(t):
    return min(t, 128)


def _s5_blk(k, width):
    return slice(k * width, (k + 1) * width)


def _s5_in_bd(bb):
    b4 = bb.reshape(S5_BLOCKS, 8, S5_GROUP, S5_STATE)
    return jnp.einsum("kgin,gh->kgihn", b4, jnp.eye(8, dtype=bb.dtype)).reshape(S5_BLOCKS, S5_BCH, S5_BST)


def _s5_in_bd_diag(bd):
    b5 = bd.reshape(S5_BLOCKS, 8, S5_GROUP, 8, S5_STATE)
    return jnp.einsum("kgihn,gh->kgin", b5, jnp.eye(8, dtype=bd.dtype)).reshape(S5_GROUPS, S5_GROUP, S5_STATE)


def _s5_out_bd(c):
    c4 = c.reshape(S5_BLOCKS, 8, S5_GROUP, S5_STATE)
    return jnp.einsum("kgin,gh->kgnhi", c4, jnp.eye(8, dtype=c.dtype)).reshape(S5_BLOCKS, S5_BST, S5_BCH)


def _s5_out_bd_diag(bd):
    c5 = bd.reshape(S5_BLOCKS, 8, S5_STATE, 8, S5_GROUP)
    return jnp.einsum("kgnhi,gh->kgin", c5, jnp.eye(8, dtype=bd.dtype)).reshape(S5_GROUPS, S5_GROUP, S5_STATE)


def _s5_scan_fwd(x, ln, lb_re, lb_im, bbr_bd, bbi_bd, cr_bd, ci_bd, dskip):
    g, shift, scale, _ = ln
    t = x.shape[0]
    tm = _s5_tile(t)
    ns = S5_NSTATE

    def body(x_ref, g_ref, sc_ref, sh_ref, ar_ref, ai_ref, bbr_ref, bbi_ref, cr_ref, ci_ref, d_ref,
             yy_ref, xr_ref, xi_ref, cre_ref, cim_ref):
        @pl.when(pl.program_id(0) == 0)
        def _():
            cre_ref[...] = jnp.zeros_like(cre_ref)
            cim_ref[...] = jnp.zeros_like(cim_ref)

        h = _adaln(x_ref[...], g_ref[...], sc_ref[...], sh_ref[...])
        ub = h.astype(bf16)
        for k in range(S5_BLOCKS):
            uk = ub[:, _s5_blk(k, S5_BCH)]
            xr_ref[:, _s5_blk(k, S5_BST)] = jnp.dot(uk, bbr_ref[k], preferred_element_type=f32)
            xi_ref[:, _s5_blk(k, S5_BST)] = jnp.dot(uk, bbi_ref[k], preferred_element_type=f32)
        ar, ai = ar_ref[...], ai_ref[...]

        def step(tt, carry):
            sr, si = carry
            row = pl.ds(tt, 1)
            nr = (ar * sr - ai * si) + xr_ref[row, :]
            ni = (ar * si + ai * sr) + xi_ref[row, :]
            xr_ref[row, :] = nr
            xi_ref[row, :] = ni
            return nr, ni

        sr, si = lax.fori_loop(0, tm, step, (cre_ref[0:1, :], cim_ref[0:1, :]), unroll=2)
        cre_ref[0:1, :] = sr
        cim_ref[0:1, :] = si
        for k in range(S5_BLOCKS):
            sb = _s5_blk(k, S5_BST)
            yk = (jnp.dot(xr_ref[:, sb].astype(bf16), cr_ref[k], preferred_element_type=f32)
                  - jnp.dot(xi_ref[:, sb].astype(bf16), ci_ref[k], preferred_element_type=f32))
            cb = _s5_blk(k, S5_BCH)
            yy_ref[:, cb] = yk + d_ref[:, cb] * h[:, cb]

    bd_in = pl.BlockSpec((S5_BLOCKS, S5_BCH, S5_BST), lambda i: (0, 0, 0))
    bd_out = pl.BlockSpec((S5_BLOCKS, S5_BST, S5_BCH), lambda i: (0, 0, 0))
    st = jax.ShapeDtypeStruct((t, ns), f32)
    return pl.pallas_call(
        body, name="s5_scan_fwd", grid=(t // tm,),
        in_specs=[_tok_spec(tm, D), _row_spec(D), _row_spec(D), _row_spec(D), _row_spec(ns), _row_spec(ns),
                  bd_in, bd_in, bd_out, bd_out, _row_spec(D)],
        out_specs=[_tok_spec(tm, D), _tok_spec(tm, ns), _tok_spec(tm, ns)],
        out_shape=[jax.ShapeDtypeStruct((t, D), f32), st, st],
        scratch_shapes=[pltpu.VMEM((8, ns), f32), pltpu.VMEM((8, ns), f32)],
        compiler_params=_cp(1),
    )(x, g, scale, shift, lb_re, lb_im, bbr_bd, bbi_bd, cr_bd, ci_bd, dskip)


def _s5_scan_bwd(dyy, x, ln, xr, xi, lb_re, lb_im, bbr_bd, bbi_bd, cr_bd, ci_bd, dskip):
    g, shift, scale, _ = ln
    t = x.shape[0]
    tm = _s5_tile(t)
    nt = t // tm
    ns = S5_NSTATE
    per = tm // 8

    def body(dyy_ref, x_ref, g_ref, sc_ref, sh_ref, xr_ref, xi_ref, xrp_ref, xip_ref, ar_ref, ai_ref,
             bbr_ref, bbi_ref, cr_ref, ci_ref, d_ref,
             du_ref, dar_ref, dai_ref, dbbr_ref, dbbi_ref, dcr_ref, dci_ref, dd_ref,
             gr_ref, gi_ref, cre_ref, cim_ref):
        i = pl.program_id(0)
        first = i == 0

        @pl.when(first)
        def _():
            cre_ref[...] = jnp.zeros_like(cre_ref)
            cim_ref[...] = jnp.zeros_like(cim_ref)

        h = _adaln(x_ref[...], g_ref[...], sc_ref[...], sh_ref[...])
        ub = h.astype(bf16)
        dyy_v = dyy_ref[...]
        dyb = dyy_v.astype(bf16)
        for k in range(S5_BLOCKS):
            dk = dyb[:, _s5_blk(k, S5_BCH)]
            sb = _s5_blk(k, S5_BST)
            gr_ref[:, sb] = lax.dot_general(dk, cr_ref[k], _DN["nt"], preferred_element_type=f32)
            gi_ref[:, sb] = -lax.dot_general(dk, ci_ref[k], _DN["nt"], preferred_element_type=f32)
        ar, ai = ar_ref[...], ai_ref[...]

        def step(s, carry):
            nr_, ni_ = carry
            row = pl.ds(tm - 1 - s, 1)
            nr = gr_ref[row, :] + (ar * nr_ + ai * ni_)
            ni = gi_ref[row, :] + (ar * ni_ - ai * nr_)
            gr_ref[row, :] = nr
            gi_ref[row, :] = ni
            return nr, ni

        nr, ni = lax.fori_loop(0, tm, step, (cre_ref[0:1, :], cim_ref[0:1, :]), unroll=2)
        cre_ref[0:1, :] = nr
        cim_ref[0:1, :] = ni

        is_first_tile = i == nt - 1
        xr_v, xi_v = xr_ref[...], xi_ref[...]
        xrp = jnp.where(is_first_tile, 0.0, xrp_ref[...])
        xip = jnp.where(is_first_tile, 0.0, xip_ref[...])
        xr_s = _roll_rows(jnp.concatenate([xrp, xr_v], axis=0), 1)[8:, :]
        xi_s = _roll_rows(jnp.concatenate([xip, xi_v], axis=0), 1)[8:, :]
        gr, gi = gr_ref[...], gi_ref[...]
        _acc_add(first, dar_ref, jnp.sum(gr * xr_s + gi * xi_s, axis=0, keepdims=True))
        _acc_add(first, dai_ref, jnp.sum(gi * xr_s - gr * xi_s, axis=0, keepdims=True))
        _acc_add(first, dd_ref, jnp.sum(dyy_v * h, axis=0, keepdims=True))
        grb, gib = gr.astype(bf16), gi.astype(bf16)
        xrb, xib = xr_v.astype(bf16), xi_v.astype(bf16)
        for k in range(S5_BLOCKS):
            cb, sb = _s5_blk(k, S5_BCH), _s5_blk(k, S5_BST)
            uk, dk = ub[:, cb], dyb[:, cb]
            tn = lambda a_, b_: lax.dot_general(a_, b_, _DN["tn"], preferred_element_type=f32)
            vals = (tn(uk, grb[:, sb]), tn(uk, gib[:, sb]), tn(xrb[:, sb], dk), -tn(xib[:, sb], dk))
            for ref, val in zip((dbbr_ref, dbbi_ref, dcr_ref, dci_ref), vals):
                @pl.when(first)
                def _():
                    ref[k] = val

                @pl.when(jnp.logical_not(first))
                def _():
                    ref[k] += val
            du_k = (lax.dot_general(grb[:, sb], bbr_ref[k], _DN["nt"], preferred_element_type=f32)
                    + lax.dot_general(gib[:, sb], bbi_ref[k], _DN["nt"], preferred_element_type=f32))
            du_ref[:, cb] = du_k + d_ref[:, cb] * dyy_v[:, cb]

    rev = lambda c: _tok_spec(tm, c, nt, True)
    prev = pl.BlockSpec((8, ns), lambda i: (jnp.maximum((nt - 1 - i) * per - 1, 0), 0))
    bd_in = pl.BlockSpec((S5_BLOCKS, S5_BCH, S5_BST), lambda i: (0, 0, 0))
    bd_out = pl.BlockSpec((S5_BLOCKS, S5_BST, S5_BCH), lambda i: (0, 0, 0))
    row_ns = jax.ShapeDtypeStruct((1, ns), f32)
    bd_in_s = jax.ShapeDtypeStruct((S5_BLOCKS, S5_BCH, S5_BST), f32)
    bd_out_s = jax.ShapeDtypeStruct((S5_BLOCKS, S5_BST, S5_BCH), f32)
    return pl.pallas_call(
        body, name="s5_scan_bwd", grid=(nt,),
        in_specs=[rev(D), rev(D), _row_spec(D), _row_spec(D), _row_spec(D), rev(ns), rev(ns), prev, prev,
                  _row_spec(ns), _row_spec(ns), bd_in, bd_in, bd_out, bd_out, _row_spec(D)],
        out_specs=[rev(D), _row_spec(ns), _row_spec(ns), bd_in, bd_in, bd_out, bd_out, _row_spec(D)],
        out_shape=[jax.ShapeDtypeStruct((t, D), f32), row_ns, row_ns, bd_in_s, bd_in_s, bd_out_s, bd_out_s,
                   jax.ShapeDtypeStruct((1, D), f32)],
        scratch_shapes=[pltpu.VMEM((tm, ns), f32), pltpu.VMEM((tm, ns), f32),
                        pltpu.VMEM((8, ns), f32), pltpu.VMEM((8, ns), f32)],
        compiler_params=_cp(1),
    )(dyy, x, g, scale, shift, xr, xi, xr, xi, lb_re, lb_im, bbr_bd, bbi_bd, cr_bd, ci_bd, dskip)


def _s5_gelu(yy):
    t = yy.shape[0]
    tm = _tok_tile(t)

    def body(y_ref, o_ref):
        o_ref[...] = jax.nn.gelu(y_ref[...]).astype(bf16)

    return pl.pallas_call(
        body, name="s5_gelu", grid=(t // tm,), in_specs=[_tok_spec(tm, D)], out_specs=_tok_spec(tm, D),
        out_shape=jax.ShapeDtypeStruct((t, D), bf16), compiler_params=_cp(1),
    )(yy)


def _s5_glu(gl, z):
    return gl * jax.nn.sigmoid(z)


def _s5_out(x, yy, z, gate):
    t = x.shape[0]
    tm = _tok_tile(t)

    def body(x_ref, y_ref, z_ref, gt_ref, o_ref):
        o_ref[...] = x_ref[...] + (1.0 + gt_ref[...]) * _s5_glu(jax.nn.gelu(y_ref[...]), z_ref[...])

    return pl.pallas_call(
        body, name="s5_out", grid=(t // tm,),
        in_specs=[_tok_spec(tm, D), _tok_spec(tm, D), _tok_spec(tm, D), _row_spec(D)],
        out_specs=_tok_spec(tm, D), out_shape=jax.ShapeDtypeStruct((t, D), f32), compiler_params=_cp(1),
    )(x, yy, z, gate)


def _s5_out_bwd(dxn, yy, z, gate):
    t = dxn.shape[0]
    tm = _tok_tile(t)

    def body(dxn_ref, y_ref, z_ref, gt_ref, dz_ref, dgl_ref, dgt_ref):
        dxn_v = dxn_ref[...]
        gl = jax.nn.gelu(y_ref[...])
        out, vjp = jax.vjp(_s5_glu, gl, z_ref[...])
        dgl, dz = vjp((1.0 + gt_ref[...]) * dxn_v)
        dz_ref[...] = dz.astype(bf16)
        dgl_ref[...] = dgl
        _acc_add(pl.program_id(0) == 0, dgt_ref, jnp.sum(dxn_v * out, axis=0, keepdims=True))

    return pl.pallas_call(
        body, name="s5_out_bwd", grid=(t // tm,),
        in_specs=[_tok_spec(tm, D), _tok_spec(tm, D), _tok_spec(tm, D), _row_spec(D)],
        out_specs=[_tok_spec(tm, D), _tok_spec(tm, D), _row_spec(D)],
        out_shape=[jax.ShapeDtypeStruct((t, D), bf16), jax.ShapeDtypeStruct((t, D), f32),
                   jax.ShapeDtypeStruct((1, D), f32)],
        compiler_params=_cp(1),
    )(dxn, yy, z, gate)


def _s5_gelu_bwd(yy, dgl_a, dgl_b):
    t = yy.shape[0]
    tm = _tok_tile(t)

    def body(y_ref, a_ref, b_ref, o_ref):
        _, vjp = jax.vjp(jax.nn.gelu, y_ref[...])
        o_ref[...] = vjp(a_ref[...] + b_ref[...])[0]

    return pl.pallas_call(
        body, name="s5_gelu_bwd", grid=(t // tm,),
        in_specs=[_tok_spec(tm, D), _tok_spec(tm, D), _tok_spec(tm, D)],
        out_specs=_tok_spec(tm, D), out_shape=jax.ShapeDtypeStruct((t, D), f32), compiler_params=_cp(1),
    )(yy, dgl_a, dgl_b)


def _s5_params(lam_re, lam_im, log_dt, b_re, b_im):
    bc = lambda a: a.reshape(S5_GROUPS, 1, -1)
    return (bc(lam_re), bc(lam_im), jnp.broadcast_to(log_dt.reshape(S5_GROUPS, 1, 1), (S5_GROUPS, 1, S5_STATE)),
            b_re.transpose(0, 2, 1), b_im.transpose(0, 2, 1))


def _s5_fwd(x, ln, raw, c_re, c_im, dskip, w_glu):
    gate = ln[3]
    lb_re, lb_im, bb_re, bb_im = _s5_prep(*raw)
    lbr, lbi = lb_re.reshape(1, S5_NSTATE), lb_im.reshape(1, S5_NSTATE)
    bds = (_s5_in_bd(bb_re).astype(bf16), _s5_in_bd(bb_im).astype(bf16),
           _s5_out_bd(c_re).astype(bf16), _s5_out_bd(c_im).astype(bf16))
    yy, xr, xi = _s5_scan_fwd(x, ln, lbr, lbi, *bds, dskip)
    gl = _s5_gelu(yy)
    z = _mm(gl, w_glu, "nn", f32, "s5_glu_mm")
    return _s5_out(x, yy, z, gate), (x, lbr, lbi, bds, yy, xr, xi, gl, z)


def _s5_bwd(dxn, saved, ln, raw, dskip, w_glu):
    x, lbr, lbi, bds, yy, xr, xi, gl, z = saved
    g, shift, scale, gate = ln
    dz, dgl_a, dgate = _s5_out_bwd(dxn, yy, z, gate)
    dgl_b = _mm(dz, w_glu, "nt", f32, "s5_dgl")
    dw_glu = _mm(gl, dz, "tn", f32, "s5_dwglu")
    dyy = _s5_gelu_bwd(yy, dgl_a, dgl_b)
    du, dar, dai, dbbr_bd, dbbi_bd, dcr_bd, dci_bd, dd = _s5_scan_bwd(dyy, x, ln, xr, xi, lbr, lbi, *bds, dskip)
    shp = (S5_GROUPS, 1, S5_STATE)
    d_lam_re, d_lam_im, d_dt, d_b_re, d_b_im = _s5_prep_bwd(
        *raw, dar.reshape(shp), dai.reshape(shp), _s5_in_bd_diag(dbbr_bd), _s5_in_bd_diag(dbbi_bd))
    dx, dg, dscale, dshift = _ln_bwd(x, g, scale, shift, du, dxn)
    grads = dict(
        s5_lam_re=d_lam_re.reshape(1, S5_GROUPS, S5_STATE), s5_lam_im=d_lam_im.reshape(1, S5_GROUPS, S5_STATE),
        s5_log_dt=d_dt[:, 0, 0].reshape(1, S5_GROUPS),
        s5_b_re=d_b_re.transpose(0, 2, 1)[None], s5_b_im=d_b_im.transpose(0, 2, 1)[None],
        s5_c_re=_s5_out_bd_diag(dcr_bd)[None], s5_c_im=_s5_out_bd_diag(dci_bd)[None],
        s5_d=dd, s5_w_glu=dw_glu)
    return dx, (dg, dshift, dscale, dgate), grads


_MESH = pl.DeviceIdType.MESH
_ANY = pl.BlockSpec(memory_space=pl.ANY)


def _me():
    return lax.axis_index("x"), lax.axis_index("y"), lax.axis_index("c")


def _dev_index(x, y, c):
    return 4 * x + 2 * y + c


def _all_gather(vs, name):
    n = len(vs)

    def body(*refs):
        v_refs, out_refs = refs[:n], refs[n:2 * n]
        send_sems, recv_sems, local_sems = refs[2 * n:]
        x, y, cc = _me()
        me, sibling = (x, y, cc), (x, y, 1 - cc)
        chips = [(1 - x, y), (x, 1 - y), (1 - x, 1 - y)]
        sends, local = [], []

        def copy(a, k, block, to, src=None):
            rows = out_refs[a].at[_dev_index(*block)]
            return pltpu.make_async_remote_copy(
                src_ref=rows if src is None else src, dst_ref=rows,
                send_sem=send_sems.at[7 * a + k], recv_sem=recv_sems.at[7 * a + k], device_id=to, device_id_type=_MESH)

        for a in range(n):
            mine = pltpu.make_async_copy(v_refs[a], out_refs[a].at[_dev_index(*me)], local_sems.at[a])
            mine.start()
            local.append(mine)
            first = [copy(a, 0, me, sibling, src=v_refs[a])]
            first += [copy(a, 1 + j, me, (*chip, cc), src=v_refs[a]) for j, chip in enumerate(chips)]
            for cp in first:
                cp.start()
            sends += first
        for a in range(n):
            for j, chip in enumerate(chips):
                copy(a, 1 + j, (*chip, cc), me).wait_recv()
                passed = copy(a, 4 + j, (*chip, cc), sibling)
                passed.start()
                sends.append(passed)
        for a in range(n):
            copy(a, 0, sibling, me).wait_recv()
            for j, chip in enumerate(chips):
                copy(a, 4 + j, (*chip, 1 - cc), me).wait_recv()
        for cp in sends:
            cp.wait_send()
        for cp in local:
            cp.wait()

    return pl.pallas_call(
        body, name=name, out_shape=[jax.ShapeDtypeStruct((N_DEV,) + v.shape, v.dtype) for v in vs],
        in_specs=[_ANY] * n, out_specs=[_ANY] * n,
        scratch_shapes=[pltpu.SemaphoreType.DMA((7 * n,)), pltpu.SemaphoreType.DMA((7 * n,)),
                        pltpu.SemaphoreType.DMA((n,))],
    )(*vs)


def _exchange_pair(vs, name):
    n = len(vs)

    def body(*refs):
        v_refs, out_refs = refs[:n], refs[n:2 * n]
        send_sems, recv_sems = refs[2 * n:]
        x, y, cc = _me()
        sibling = (x, y, 1 - cc)
        copies = []
        for a in range(n):
            for k in range(4):
                cp = pltpu.make_async_remote_copy(
                    src_ref=v_refs[a].at[2 * k + (1 - cc)], dst_ref=out_refs[a].at[k],
                    send_sem=send_sems.at[4 * a + k], recv_sem=recv_sems.at[4 * a + k],
                    device_id=sibling, device_id_type=_MESH)
                cp.start()
                copies.append(cp)
        for cp in copies:
            cp.wait_recv()
        for cp in copies:
            cp.wait_send()

    return pl.pallas_call(
        body, name=name, out_shape=[jax.ShapeDtypeStruct((4,) + v.shape[1:], v.dtype) for v in vs],
        in_specs=[_ANY] * n, out_specs=[_ANY] * n,
        scratch_shapes=[pltpu.SemaphoreType.DMA((4 * n,)), pltpu.SemaphoreType.DMA((4 * n,))],
    )(*vs)


def _pair_sum(v, got):
    _, r, c = v.shape
    tr = _pick(r, (512, 352, 256, 128))
    core = lax.axis_index("c").astype(jnp.int32).reshape(1)

    def body(c_ref, v_ref, g_ref, o_ref):
        o_ref[...] = (v_ref[...] + g_ref[...]).astype(bf16)

    return pl.pallas_call(
        body, name="pair_sum",
        grid_spec=pltpu.PrefetchScalarGridSpec(
            num_scalar_prefetch=1, grid=(4, r // tr),
            in_specs=[pl.BlockSpec((1, tr, c), lambda k, i, c_ref: (2 * k + c_ref[0], i, 0)),
                      pl.BlockSpec((1, tr, c), lambda k, i, c_ref: (k, i, 0))],
            out_specs=pl.BlockSpec((1, tr, c), lambda k, i, c_ref: (k, i, 0))),
        out_shape=jax.ShapeDtypeStruct((4, r, c), bf16), compiler_params=_cp(2),
    )(core, v, got)


def _exchange_chips(vs, name):
    n = len(vs)

    def body(*refs):
        v_refs, out_refs = refs[:n], refs[n:2 * n]
        send_sems, recv_sems, local_sems = refs[2 * n:]
        x, y, cc = _me()
        mine = 2 * x + y
        peers = []
        for mask in (1, 2, 3):
            px = 1 - x if mask & 2 else x
            py = 1 - y if mask & 1 else y
            peers.append((mask - 1, (px, py, cc), 2 * px + py))
        local, sends = [], []
        for a in range(n):
            own = pltpu.make_async_copy(v_refs[a].at[mine], out_refs[a].at[mine], local_sems.at[a])
            own.start()
            local.append(own)
            for k, peer, pchip in peers:
                cp = pltpu.make_async_remote_copy(
                    src_ref=v_refs[a].at[pchip], dst_ref=out_refs[a].at[mine],
                    send_sem=send_sems.at[3 * a + k], recv_sem=recv_sems.at[3 * a + k],
                    device_id=peer, device_id_type=_MESH)
                cp.start()
                sends.append(cp)
        for a in range(n):
            for k, peer, pchip in peers:
                pltpu.make_async_remote_copy(
                    src_ref=v_refs[a].at[pchip], dst_ref=out_refs[a].at[pchip],
                    send_sem=send_sems.at[3 * a + k], recv_sem=recv_sems.at[3 * a + k],
                    device_id=peer, device_id_type=_MESH).wait_recv()
        for cp in sends:
            cp.wait_send()
        for cp in local:
            cp.wait()

    return pl.pallas_call(
        body, name=name, out_shape=[jax.ShapeDtypeStruct(v.shape, v.dtype) for v in vs],
        in_specs=[_ANY] * n, out_specs=[_ANY] * n,
        scratch_shapes=[pltpu.SemaphoreType.DMA((3 * n,)), pltpu.SemaphoreType.DMA((3 * n,)),
                        pltpu.SemaphoreType.DMA((n,))],
    )(*vs)


def _ada_mod(c_all, ada_w):
    cols = ada_w.shape[2]

    def body(c_ref, w_ref, o_ref):
        cond = jax.nn.silu(c_ref[...]).astype(bf16)
        o_ref[0] = jnp.dot(cond, w_ref[0].astype(bf16), preferred_element_type=f32)

    return pl.pallas_call(
        body, name="ada_mod", grid=(DEPTH,),
        in_specs=[pl.BlockSpec((16, D), lambda i: (0, 0)), pl.BlockSpec((1, D, cols), lambda i: (i, 0, 0))],
        out_specs=pl.BlockSpec((1, 16, cols), lambda i: (i, 0, 0)),
        out_shape=jax.ShapeDtypeStruct((DEPTH, 16, cols), f32), compiler_params=_cp(1),
    )(c_all, ada_w)


def _ada_grad(c_all, dmod):
    cols = dmod.shape[2]

    def body(c_ref, d_ref, o_ref):
        cond = jax.nn.silu(c_ref[...]).astype(bf16)
        o_ref[0] = lax.dot_general(cond, d_ref[0].astype(bf16), _DN["tn"], preferred_element_type=f32)

    return pl.pallas_call(
        body, name="ada_grad", grid=(DEPTH,),
        in_specs=[pl.BlockSpec((16, D), lambda i: (0, 0)), pl.BlockSpec((1, 16, cols), lambda i: (i, 0, 0))],
        out_specs=pl.BlockSpec((1, D, cols), lambda i: (i, 0, 0)),
        out_shape=jax.ShapeDtypeStruct((DEPTH, D, cols), f32), compiler_params=_cp(1),
    )(c_all, dmod)


def _row_tile(r):
    return _pick(r, (512, 352, 256, 128)) if r > 512 else r


def _sum_sources(v, name):
    n, r, c = v.shape
    tr = _row_tile(r)

    def body(v_ref, o_ref):
        acc = v_ref[0]
        for p in range(1, n):
            acc = acc + v_ref[p]
        o_ref[...] = acc.astype(f32)

    return pl.pallas_call(
        body, name=name, grid=(r // tr,),
        in_specs=[pl.BlockSpec((n, tr, c), lambda i: (0, i, 0))], out_specs=pl.BlockSpec((tr, c), lambda i: (i, 0)),
        out_shape=jax.ShapeDtypeStruct((r, c), f32), compiler_params=_cp(1),
    )(v)


def _adamw(parts, w, m, v, name):
    n, r, c = parts.shape
    tr = _row_tile(r)
    c1 = 1.0 - ADAM_B1 ** ADAM_STEP
    c2 = 1.0 - ADAM_B2 ** ADAM_STEP

    def body(p_ref, w_ref, m_ref, v_ref, g_ref, d_ref, mo_ref, vo_ref):
        g_v = p_ref[0].astype(f32)
        for p in range(1, n):
            g_v = g_v + p_ref[p].astype(f32)
        m_n = ADAM_B1 * m_ref[...] + (1.0 - ADAM_B1) * g_v
        v_n = ADAM_B2 * v_ref[...] + (1.0 - ADAM_B2) * (g_v * g_v)
        g_ref[...] = g_v
        d_ref[...] = -ADAM_LR * ((m_n / c1) / (jnp.sqrt(v_n / c2) + ADAM_EPS) + ADAM_WD * w_ref[...])
        mo_ref[...] = m_n
        vo_ref[...] = v_n

    spec = pl.BlockSpec((tr, c), lambda i: (i, 0))
    shp = jax.ShapeDtypeStruct((r, c), f32)
    return pl.pallas_call(
        body, name=name, grid=(r // tr,), in_specs=[pl.BlockSpec((n, tr, c), lambda i: (0, i, 0))] + [spec] * 3,
        out_specs=[spec] * 4, out_shape=[shp] * 4, compiler_params=_cp(1),
    )(parts, w, m, v)


def _two_d(shape):
    return (math.prod(shape[:-1]), shape[-1])


def _pack_rows(a):
    n = a.size
    rows = -(-n // (8 * PACK_C)) * 8
    return jnp.pad(a.reshape(-1), (0, rows * PACK_C - n)).reshape(rows, PACK_C)


def _pack(parts):
    return jnp.concatenate([_pack_rows(p) for p in parts], axis=0)


def _unpack(packed, shapes):
    lead = packed.shape[:-2]
    out, off = [], 0
    for s in shapes:
        n = math.prod(s)
        rows = -(-n // (8 * PACK_C)) * 8
        part = packed[..., off:off + rows, :].reshape(lead + (rows * PACK_C,))
        out.append(part[..., :n].reshape(lead + tuple(s)))
        off += rows
    return out


def _unshard(g8, axis):
    local = g8.shape[1:]
    moved = jnp.moveaxis(g8, 0, axis)
    return moved.reshape(local[:axis] + (N_DEV * local[axis],) + local[axis + 1:])


def _shard8(full, axis):
    s = full.shape
    split = full.reshape(s[:axis] + (N_DEV, s[axis] // N_DEV) + s[axis + 1:])
    return jnp.moveaxis(split, axis, 0)


_BIG = dict(ffn_w_in=3, ffn_w_out=2, pool_w=2, fox_w_in=2, fox_w_o=1, s5_w_glu=1, conv_w_in=2, conv_w_out=1)
_SMALL_SHARDED = dict(norm_g=2, s5_d=1, conv_w=3)
_REPLICATED = ("ada_b", "pool_scale", "fox_b_f", "fox_q_gain", "fox_k_gain", "s5_lam_re", "s5_lam_im", "s5_log_dt",
               "s5_b_re", "s5_b_im", "s5_c_re", "s5_c_im")
_WEIGHTS = ("ada_w", "ada_b", "norm_g", "ffn_w_in", "ffn_w_out", "pool_w", "pool_scale", "fox_w_in", "fox_b_f",
            "fox_q_gain", "fox_k_gain", "fox_w_o", "s5_lam_re", "s5_lam_im", "s5_log_dt", "s5_b_re", "s5_b_im",
            "s5_c_re", "s5_c_im", "s5_d", "s5_w_glu", "conv_w_in", "conv_w", "conv_w_out")


def _step(x, c, target, w, m, v):
    t = x.shape[1]
    xi_, yi_, ci_ = _me()
    me = _dev_index(xi_, yi_, ci_)

    sm_shapes = [w[n].shape for n in _SMALL_SHARDED]
    small_all = _all_gather([_pack([c] + [w[n] for n in _SMALL_SHARDED])], "gather_small")[0]
    gathered = _unpack(small_all, [c.shape] + sm_shapes)
    c_all = gathered[0][:, 0, :]
    full = {n: _unshard(p, ax) for (n, ax), p in zip(_SMALL_SHARDED.items(), gathered[1:])}

    big_all = _all_gather([w[n].astype(bf16).reshape(_two_d(w[n].shape)) for n in _BIG], "gather_weights")
    gw = dict(zip(_BIG, big_all))
    ffn_w_in = gw["ffn_w_in"].reshape(N_DEV, 2 * DEPTH, D, FFN_HS)
    ffn_w_out = gw["ffn_w_out"].reshape(N_DEV, 2 * DEPTH, D_FF // N_DEV, D)
    pool_w = gw["pool_w"].reshape(N_DEV, 4, POOL_GROUP // N_DEV, POOL_GROUP).transpose(1, 0, 2, 3)
    pool_w = pool_w.reshape(4, POOL_GROUP, POOL_GROUP)
    fox_w_in = jnp.pad(gw["fox_w_in"].transpose(1, 0, 2).reshape(D, FOX_PROJ), ((0, 0), (0, FOX_PROJ_PAD - FOX_PROJ)))
    fox_w_o, s5_w_glu, conv_w_out = (gw[n].reshape(D, D) for n in ("fox_w_o", "s5_w_glu", "conv_w_out"))
    conv_w_in = gw["conv_w_in"]

    def ffn_w(i, f):
        idx = 2 * i + f
        return ffn_w_in[:, idx], ffn_w_out[:, idx].reshape(D_FF, D)

    c16 = jnp.pad(c_all, ((0, 8), (0, 0)))
    cols = w["ada_w"].shape[2]
    mod_sh = _ada_mod(c16, w["ada_w"])
    mod_all = _all_gather([mod_sh.reshape(DEPTH * 16, cols)], "gather_mod")[0].reshape(N_DEV, DEPTH, 16, cols)
    mod_mine = lax.dynamic_index_in_dim(mod_all, me, axis=2, keepdims=False)
    mod = (mod_mine.transpose(1, 0, 2).reshape(DEPTH, N_DEV * cols) + w["ada_b"]).reshape(DEPTH, 3, 3, D)

    norm_g = full["norm_g"]

    def ln_of(i, sub):
        return (norm_g[i, sub][None], mod[i, sub, 0][None], mod[i, sub, 1][None], mod[i, sub, 2][None])

    fox_b_f = jnp.pad(w["fox_b_f"], ((0, 0), (0, 128 - FOX_HEADS)))
    s5_raw = _s5_params(w["s5_lam_re"][0], w["s5_lam_im"][0], w["s5_log_dt"][0], w["s5_b_re"][0], w["s5_b_im"][0])
    s5_c_re, s5_c_im = w["s5_c_re"][0], w["s5_c_im"][0]
    conv_w = jnp.pad(full["conv_w"][0, :, 0, :], ((0, 5), (0, 0)))

    xs = x[0]
    saved = []
    for i in range(DEPTH):
        xs, s0 = _ffn_fwd(xs, ln_of(i, 0), *ffn_w(i, 0), 0.5)
        if i == 0:
            s1 = xs
            xs = _pool_fwd(xs, ln_of(i, 1), pool_w, w["pool_scale"])
        elif i == 1:
            xs, s1 = _fox_fwd(xs, ln_of(i, 1), fox_w_in, fox_b_f, w["fox_q_gain"], w["fox_k_gain"], fox_w_o)
        elif i == 2:
            xs, s1 = _s5_fwd(xs, ln_of(i, 1), s5_raw, s5_c_re, s5_c_im, full["s5_d"], s5_w_glu)
        else:
            xs, s1 = _convmix_fwd(xs, ln_of(i, 1), conv_w_in, conv_w, conv_w_out)
        xs, s2 = _ffn_fwd(xs, ln_of(i, 2), *ffn_w(i, 1), 0.5)
        saved.append((s0, s1, s2))
    dx, lpart = _loss_head(xs, target[0])
    loss = lax.psum(lpart[0, 0], AXES)

    grads = {}
    dmod = [[None] * 3 for _ in range(DEPTH)]
    dnorm = [[None] * 3 for _ in range(DEPTH)]
    dffn_in = [[None] * 2 for _ in range(DEPTH)]
    dffn_out = [[None] * 2 for _ in range(DEPTH)]

    def put_ln(i, sub, dln):
        dg, dshift, dscale, dgate = dln
        dnorm[i][sub] = dg
        dmod[i][sub] = jnp.concatenate([dshift, dscale, dgate], axis=0)

    for i in reversed(range(DEPTH)):
        s0, s1, s2 = saved[i]
        dx, dln, dffn_in[i][1], dffn_out[i][1] = _ffn_bwd(dx, s2, ln_of(i, 2), *ffn_w(i, 1), 0.5)
        put_ln(i, 2, dln)
        if i == 0:
            dx, dln, dpw, dps = _pool_bwd(dx, s1, ln_of(i, 1), pool_w, w["pool_scale"])
            dpw = dpw.reshape(4, N_DEV, POOL_GROUP // N_DEV, POOL_GROUP).transpose(1, 0, 2, 3)
            grads.update(pool_w=dpw.reshape(N_DEV, 4 * POOL_GROUP // N_DEV, POOL_GROUP), pool_scale=dps)
        elif i == 1:
            dx, dln, dwi, dbf, dqg, dkg, dwo = _fox_bwd(
                dx, s1, ln_of(i, 1), fox_w_in, fox_b_f, w["fox_q_gain"], w["fox_k_gain"], fox_w_o)
            dwi = dwi[:, :FOX_PROJ].reshape(D, N_DEV, FOX_PROJ // N_DEV).transpose(1, 0, 2)
            grads.update(fox_w_in=dwi, fox_b_f=dbf[:, :FOX_HEADS], fox_q_gain=dqg, fox_k_gain=dkg,
                         fox_w_o=dwo.reshape(N_DEV, D // N_DEV, D))
        elif i == 2:
            dx, dln, gs5 = _s5_bwd(dx, s1, ln_of(i, 1), s5_raw, full["s5_d"], s5_w_glu)
            gs5["s5_w_glu"] = gs5["s5_w_glu"].reshape(N_DEV, D // N_DEV, D)
            grads.update(gs5)
        else:
            dx, dln, dwi, dcw, dwo = _convmix_bwd(dx, s1, ln_of(i, 1), conv_w_in, conv_w, conv_w_out)
            grads.update(conv_w_in=dwi, conv_w=dcw[None, :, None, :], conv_w_out=dwo.reshape(N_DEV, D // N_DEV, D))
        put_ln(i, 1, dln)
        dx, dln, dffn_in[i][0], dffn_out[i][0] = _ffn_bwd(dx, s0, ln_of(i, 0), *ffn_w(i, 0), 0.5)
        put_ln(i, 0, dln)
    grads["ffn_w_in"] = jnp.stack([p for r in dffn_in for p in r], axis=1).reshape(N_DEV, 2 * DEPTH * D, FFN_HS)
    grads["ffn_w_out"] = jnp.stack([p.reshape(N_DEV, D_FF // N_DEV, D) for r in dffn_out for p in r], axis=1)
    grads["ffn_w_out"] = grads["ffn_w_out"].reshape(N_DEV, 2 * DEPTH * D_FF // N_DEV, D)
    grads["norm_g"] = jnp.stack([jnp.concatenate(r, axis=0) for r in dnorm])
    dmod_mine = jnp.stack([jnp.stack(r) for r in dmod]).reshape(DEPTH, 9 * D)

    small_names = list(_REPLICATED[1:]) + list(_SMALL_SHARDED)
    small_parts = [dmod_mine] + [grads[n] for n in small_names]
    small_g = _all_gather([_pack(small_parts)], "gather_grads")[0]
    small_sum = _sum_sources(small_g, "sum_small")
    summed = dict(zip(["ada_b"] + small_names, _unpack(small_sum, [p.shape for p in small_parts])))
    for n, ax in _SMALL_SHARDED.items():
        local = w[n].shape[ax]
        summed[n] = lax.dynamic_slice_in_dim(summed[n], me * local, local, axis=ax)

    dmod_all = small_g[:, :DEPTH * 9].reshape(N_DEV, DEPTH, 9 * D)
    dmod_cols = lax.dynamic_slice_in_dim(dmod_all, me * cols, cols, axis=2)
    ada_g = _ada_grad(c16, jnp.pad(dmod_cols.transpose(1, 0, 2), ((0, 0), (0, 8), (0, 0))))

    big_parts = [grads[n] for n in _BIG]
    from_sibling = _exchange_pair(big_parts, "exchange_pair")
    chip_sums = [_pair_sum(p, s) for p, s in zip(big_parts, from_sibling)]
    big_landed = dict(zip(_BIG, _exchange_chips(chip_sums, "exchange_chips")))

    grad, delta, new_m, new_v = {}, {}, {}, {}
    big_landed["ada_w"] = ada_g[None]
    for n, parts in big_landed.items():
        view = _two_d(w[n].shape)
        outs = _adamw(parts.reshape((parts.shape[0],) + view), w[n].reshape(view), m[n].reshape(view),
                      v[n].reshape(view), "adamw_" + n)
        grad[n], delta[n], new_m[n], new_v[n] = (a.reshape(w[n].shape) for a in outs)
    small = [n for n in _WEIGHTS if n not in big_landed]
    small_shapes = [w[n].shape for n in small]
    pk = lambda d: _pack([d[n] for n in small])
    outs = _adamw(pk(summed)[None], pk(w), pk(m), pk(v), "adamw_small")
    for dst, packed in zip((grad, delta, new_m, new_v), outs):
        dst.update(zip(small, _unpack(packed, small_shapes)))
    return (loss, dx[None], *[grad[n] for n in _WEIGHTS], *[delta[n] for n in _WEIGHTS],
            *[new_m[n] for n in _WEIGHTS], *[new_v[n] for n in _WEIGHTS])


def kernel(x, c, ada_w, ada_b, norm_g, ffn_w_in, ffn_w_out, pool_w, pool_scale, fox_w_in, fox_b_f, fox_q_gain, fox_k_gain, fox_w_o, s5_lam_re, s5_lam_im, s5_log_dt, s5_b_re, s5_b_im, s5_c_re, s5_c_im, s5_d, s5_w_glu, conv_w_in, conv_w, conv_w_out, loss_target, m_ada_w, m_ada_b, m_norm_g, m_ffn_w_in, m_ffn_w_out, m_pool_w, m_pool_scale, m_fox_w_in, m_fox_b_f, m_fox_q_gain, m_fox_k_gain, m_fox_w_o, m_s5_lam_re, m_s5_lam_im, m_s5_log_dt, m_s5_b_re, m_s5_b_im, m_s5_c_re, m_s5_c_im, m_s5_d, m_s5_w_glu, m_conv_w_in, m_conv_w, m_conv_w_out, v_ada_w, v_ada_b, v_norm_g, v_ffn_w_in, v_ffn_w_out, v_pool_w, v_pool_scale, v_fox_w_in, v_fox_b_f, v_fox_q_gain, v_fox_k_gain, v_fox_w_o, v_s5_lam_re, v_s5_lam_im, v_s5_log_dt, v_s5_b_re, v_s5_b_im, v_s5_c_re, v_s5_c_im, v_s5_d, v_s5_w_glu, v_conv_w_in, v_conv_w, v_conv_w_out):
    ws = (ada_w, ada_b, norm_g, ffn_w_in, ffn_w_out, pool_w, pool_scale, fox_w_in, fox_b_f, fox_q_gain, fox_k_gain,
          fox_w_o, s5_lam_re, s5_lam_im, s5_log_dt, s5_b_re, s5_b_im, s5_c_re, s5_c_im, s5_d, s5_w_glu, conv_w_in,
          conv_w, conv_w_out)
    ms = (m_ada_w, m_ada_b, m_norm_g, m_ffn_w_in, m_ffn_w_out, m_pool_w, m_pool_scale, m_fox_w_in, m_fox_b_f,
          m_fox_q_gain, m_fox_k_gain, m_fox_w_o, m_s5_lam_re, m_s5_lam_im, m_s5_log_dt, m_s5_b_re, m_s5_b_im,
          m_s5_c_re, m_s5_c_im, m_s5_d, m_s5_w_glu, m_conv_w_in, m_conv_w, m_conv_w_out)
    vs = (v_ada_w, v_ada_b, v_norm_g, v_ffn_w_in, v_ffn_w_out, v_pool_w, v_pool_scale, v_fox_w_in, v_fox_b_f,
          v_fox_q_gain, v_fox_k_gain, v_fox_w_o, v_s5_lam_re, v_s5_lam_im, v_s5_log_dt, v_s5_b_re, v_s5_b_im,
          v_s5_c_re, v_s5_c_im, v_s5_d, v_s5_w_glu, v_conv_w_in, v_conv_w, v_conv_w_out)
    return _step(x, c, loss_target, dict(zip(_WEIGHTS, ws)), dict(zip(_WEIGHTS, ms)), dict(zip(_WEIGHTS, vs)))
```

```python
import math

import jax
import jax.numpy as jnp
from jax import lax
from jax.experimental import pallas as pl
from jax.experimental.pallas import tpu as pltpu

f32 = jnp.float32
bf16 = jnp.bfloat16

D = 1024
D_FF = 2816
FFN_HS = 2 * D_FF // 8
FFN_SLABS = 4
DEPTH = 4
NORM_EPS = 1e-6
N_DEV = 8
AXES = ("x", "y", "c")
POOL_WINDOWS = (2, 4, 8, 16)
POOL_GROUP = 256
POOL_HALO = 16
FOX_HEADS = 16
FOX_HEAD_DIM = 64
FOX_PROJ = 3088
FOX_PROJ_PAD = 3200
S5_GROUPS = 64
S5_GROUP = 16
S5_STATE = 64
S5_NSTATE = S5_GROUPS * S5_STATE
S5_BLOCKS = 8
S5_BCH = 128
S5_BST = 512
CONV_HALO = 8
ADAM_LR = 0.001
ADAM_B1 = 0.9
ADAM_B2 = 0.999
ADAM_EPS = 1e-08
ADAM_WD = 0.01
ADAM_STEP = 10
VMEM_LIMIT = 56 * 1024 * 1024
PACK_C = 1024

_ARB = "arbitrary"


def _cp(n_axes):
    return pltpu.CompilerParams(dimension_semantics=(_ARB,) * n_axes, vmem_limit_bytes=VMEM_LIMIT)


def _pick(n, prefs):
    for c in prefs:
        if n % c == 0:
            return c
    return n


_DN = {"nn": (((1,), (0,)), ((), ())), "nt": (((1,), (1,)), ((), ())), "tn": (((0,), (0,)), ((), ()))}


def _mm(a, b, mode, out_dtype, name):
    if mode == "nn":
        (m, k), (_, n) = a.shape, b.shape
    elif mode == "nt":
        (m, k), (n, _) = a.shape, b.shape
    else:
        (k, m), (_, n) = a.shape, b.shape
    big = (1408, 1024, 640, 512, 384, 256, 128)
    tm = _pick(m, big) if mode == "tn" else _pick(m, (1024, 512, 256, 128))
    tn = _pick(n, big)
    if mode == "tn":
        tk = _pick(k, (512, 256, 128))
    else:
        tk = k if k <= 3200 else _pick(k, (2816, 2048, 1024, 512))
    nk = k // tk
    dn = _DN[mode]

    def body(a_ref, b_ref, o_ref, acc_ref):
        p = lax.dot_general(a_ref[...], b_ref[...], dn, preferred_element_type=f32)
        if nk == 1:
            o_ref[...] = p.astype(out_dtype)
        else:
            kk = pl.program_id(2)

            @pl.when(kk == 0)
            def _():
                acc_ref[...] = p

            @pl.when(kk > 0)
            def _():
                acc_ref[...] += p

            @pl.when(kk == nk - 1)
            def _():
                o_ref[...] = acc_ref[...].astype(out_dtype)

    if mode == "nn":
        a_spec = pl.BlockSpec((tm, tk), lambda i, j, kk: (i, kk))
        b_spec = pl.BlockSpec((tk, tn), lambda i, j, kk: (kk, j))
    elif mode == "nt":
        a_spec = pl.BlockSpec((tm, tk), lambda i, j, kk: (i, kk))
        b_spec = pl.BlockSpec((tn, tk), lambda i, j, kk: (j, kk))
    else:
        a_spec = pl.BlockSpec((tk, tm), lambda i, j, kk: (kk, i))
        b_spec = pl.BlockSpec((tk, tn), lambda i, j, kk: (kk, j))
    acc_shape = (tm, tn) if nk > 1 else (8, 128)
    return pl.pallas_call(
        body, name=name, grid=(m // tm, n // tn, nk),
        in_specs=[a_spec, b_spec], out_specs=pl.BlockSpec((tm, tn), lambda i, j, kk: (i, j)),
        out_shape=jax.ShapeDtypeStruct((m, n), out_dtype),
        scratch_shapes=[pltpu.VMEM(acc_shape, f32)],
        compiler_params=_cp(3),
    )(a, b)


def _mmx(a, b, mode, name, grid, a_spec, b_spec, o_spec, out_shape):
    nk = grid[2]
    dn = _DN[mode]
    out_dtype = out_shape.dtype
    a_blk = (math.prod(a_spec.block_shape[:-1]), a_spec.block_shape[-1])
    b_blk = (math.prod(b_spec.block_shape[:-1]), b_spec.block_shape[-1])
    o_blk = (math.prod(o_spec.block_shape[:-1]), o_spec.block_shape[-1])

    def body(a_ref, b_ref, o_ref, acc_ref):
        p = lax.dot_general(a_ref[...].reshape(a_blk), b_ref[...].reshape(b_blk), dn, preferred_element_type=f32)
        if nk == 1:
            o_ref[...] = p.reshape(o_ref.shape).astype(out_dtype)
        else:
            kk = pl.program_id(2)

            @pl.when(kk == 0)
            def _():
                acc_ref[...] = p

            @pl.when(kk > 0)
            def _():
                acc_ref[...] += p

            @pl.when(kk == nk - 1)
            def _():
                o_ref[...] = acc_ref[...].reshape(o_ref.shape).astype(out_dtype)

    return pl.pallas_call(
        body, name=name, grid=grid, in_specs=[a_spec, b_spec], out_specs=o_spec, out_shape=out_shape,
        scratch_shapes=[pltpu.VMEM(o_blk if nk > 1 else (8, 128), f32)], compiler_params=_cp(3),
    )(a, b)


def _tok_tile(t):
    return min(t, 512)


def _tok_spec(tm, c, nt=None, reverse=False):
    if reverse:
        return pl.BlockSpec((tm, c), lambda i: (nt - 1 - i, 0))
    return pl.BlockSpec((tm, c), lambda i: (i, 0))


def _row_spec(c, rows=1):
    return pl.BlockSpec((rows, c), lambda i: (0, 0))


def _acc_add(first, ref, val):
    @pl.when(first)
    def _():
        ref[...] = val

    @pl.when(jnp.logical_not(first))
    def _():
        ref[...] += val


def _adaln(x, g, scale, shift):
    y = x * lax.rsqrt(jnp.mean(x * x, axis=-1, keepdims=True) + NORM_EPS)
    return (y * g) * (1.0 + scale) + shift


def _ln_fwd(x, g, scale, shift):
    t = x.shape[0]
    tm = _tok_tile(t)

    def body(x_ref, g_ref, sc_ref, sh_ref, h_ref):
        h_ref[...] = _adaln(x_ref[...], g_ref[...], sc_ref[...], sh_ref[...]).astype(bf16)

    return pl.pallas_call(
        body, name="ln_fwd", grid=(t // tm,),
        in_specs=[_tok_spec(tm, D), _row_spec(D), _row_spec(D), _row_spec(D)],
        out_specs=_tok_spec(tm, D), out_shape=jax.ShapeDtypeStruct((t, D), bf16),
        compiler_params=_cp(1),
    )(x, g, scale, shift)


def _ln_bwd(x, g, scale, shift, dh, dxn):
    t = x.shape[0]
    tm = _tok_tile(t)

    def body(x_ref, g_ref, sc_ref, sh_ref, dh_ref, dxn_ref, dx_ref, dg_ref, dsc_ref, dsh_ref):
        _, vjp = jax.vjp(_adaln, x_ref[...], g_ref[...], sc_ref[...], sh_ref[...])
        dx, dg, dsc, dsh = vjp(dh_ref[...])
        dx_ref[...] = dxn_ref[...] + dx
        first = pl.program_id(0) == 0
        _acc_add(first, dg_ref, dg)
        _acc_add(first, dsc_ref, dsc)
        _acc_add(first, dsh_ref, dsh)

    row = jax.ShapeDtypeStruct((1, D), f32)
    return pl.pallas_call(
        body, name="ln_bwd", grid=(t // tm,),
        in_specs=[_tok_spec(tm, D), _row_spec(D), _row_spec(D), _row_spec(D), _tok_spec(tm, D), _tok_spec(tm, D)],
        out_specs=[_tok_spec(tm, D), _row_spec(D), _row_spec(D), _row_spec(D)],
        out_shape=[jax.ShapeDtypeStruct((t, D), f32), row, row, row],
        compiler_params=_cp(1),
    )(x, g, scale, shift, dh, dxn)


def _swiglu(g, u):
    return jax.nn.silu(g) * u


def _ffn_in(h, w_in):
    t = h.shape[0]
    tm = _mm_tile(t)

    def body(h_ref, w_ref, gu_ref, a_ref):
        hv = h_ref[...]
        g = jnp.dot(hv, w_ref[0, 0], preferred_element_type=f32)
        u = jnp.dot(hv, w_ref[1, 0], preferred_element_type=f32)
        gu_ref[0, 0] = g
        gu_ref[1, 0] = u
        a_ref[0] = _swiglu(g, u).astype(bf16)

    return pl.pallas_call(
        body, name="ffn_in", grid=(t // tm, FFN_SLABS),
        in_specs=[pl.BlockSpec((tm, D), lambda i, q: (i, 0)),
                  pl.BlockSpec((2, 1, D, FFN_HS), lambda i, q: (0, q, 0, 0))],
        out_specs=[pl.BlockSpec((2, 1, tm, FFN_HS), lambda i, q: (0, q, i, 0)),
                   pl.BlockSpec((1, tm, FFN_HS), lambda i, q: (q, i, 0))],
        out_shape=[jax.ShapeDtypeStruct((2, FFN_SLABS, t, FFN_HS), f32),
                   jax.ShapeDtypeStruct((FFN_SLABS, t, FFN_HS), bf16)],
        compiler_params=_cp(2),
    )(h, w_in)


def _ffn_dgu(do, w_out, gu):
    t = do.shape[0]
    tm = _mm_tile(t)

    def body(do_ref, w_ref, gu_ref, o_ref):
        da = lax.dot_general(do_ref[...], w_ref[...], _DN["nt"], preferred_element_type=f32)
        _, vjp = jax.vjp(_swiglu, gu_ref[0, 0], gu_ref[1, 0])
        dg, du = vjp(da)
        o_ref[0, 0] = dg.astype(bf16)
        o_ref[1, 0] = du.astype(bf16)

    slab = pl.BlockSpec((2, 1, tm, FFN_HS), lambda i, q: (0, q, i, 0))
    return pl.pallas_call(
        body, name="ffn_dgu", grid=(t // tm, FFN_SLABS),
        in_specs=[pl.BlockSpec((tm, D), lambda i, q: (i, 0)), pl.BlockSpec((FFN_HS, D), lambda i, q: (q, 0)), slab],
        out_specs=slab, out_shape=jax.ShapeDtypeStruct((2, FFN_SLABS, t, FFN_HS), bf16),
        compiler_params=_cp(2),
    )(do, w_out, gu)


def _resid_fwd(x, o, gate, coef):
    t = x.shape[0]
    tm = _tok_tile(t)

    def body(x_ref, o_ref, gt_ref, y_ref):
        y_ref[...] = x_ref[...] + (coef * (1.0 + gt_ref[...])) * o_ref[...]

    return pl.pallas_call(
        body, name="resid_fwd", grid=(t // tm,),
        in_specs=[_tok_spec(tm, D), _tok_spec(tm, D), _row_spec(D)],
        out_specs=_tok_spec(tm, D), out_shape=jax.ShapeDtypeStruct((t, D), f32),
        compiler_params=_cp(1),
    )(x, o, gate)


def _resid_bwd(dxn, o, gate, coef):
    t = dxn.shape[0]
    tm = _tok_tile(t)

    def body(dxn_ref, o_ref, gt_ref, do_ref, dgt_ref):
        dxn_v = dxn_ref[...]
        do_ref[...] = ((coef * (1.0 + gt_ref[...])) * dxn_v).astype(bf16)
        _acc_add(pl.program_id(0) == 0, dgt_ref, coef * jnp.sum(dxn_v * o_ref[...], axis=0, keepdims=True))

    return pl.pallas_call(
        body, name="resid_bwd", grid=(t // tm,),
        in_specs=[_tok_spec(tm, D), _tok_spec(tm, D), _row_spec(D)],
        out_specs=[_tok_spec(tm, D), _row_spec(D)],
        out_shape=[jax.ShapeDtypeStruct((t, D), bf16), jax.ShapeDtypeStruct((1, D), f32)],
        compiler_params=_cp(1),
    )(dxn, o, gate)


def _loss_head(y, target):
    t = y.shape[0]
    tm = _tok_tile(t)

    def body(y_ref, t_ref, dy_ref, l_ref):
        err = y_ref[...] - t_ref[...]
        dy_ref[...] = err * (1.0 / D)
        part = jnp.sum(jnp.sum(err * err, axis=0, keepdims=True), axis=1, keepdims=True) * (0.5 / D)
        _acc_add(pl.program_id(0) == 0, l_ref, jnp.broadcast_to(part, (1, 128)))

    return pl.pallas_call(
        body, name="loss_head", grid=(t // tm,),
        in_specs=[_tok_spec(tm, D), _tok_spec(tm, D)],
        out_specs=[_tok_spec(tm, D), _row_spec(128)],
        out_shape=[jax.ShapeDtypeStruct((t, D), f32), jax.ShapeDtypeStruct((1, 128), f32)],
        compiler_params=_cp(1),
    )(y, target)


def _mm_tile(t):
    return min(t, 1024)


def _ffn_fwd(x, ln, w_in, w_out, coef):
    g, shift, scale, gate = ln
    t = x.shape[0]
    tm = _mm_tile(t)
    sds = jax.ShapeDtypeStruct
    h = _ln_fwd(x, g, scale, shift)
    gu, a = _ffn_in(h, w_in.reshape(2, FFN_SLABS, D, FFN_HS))
    o = _mmx(a, w_out, "nn", "ffn_out", (t // tm, 1, FFN_SLABS),
             pl.BlockSpec((1, tm, FFN_HS), lambda i, j, k: (k, i, 0)),
             pl.BlockSpec((FFN_HS, D), lambda i, j, k: (k, 0)),
             pl.BlockSpec((tm, D), lambda i, j, k: (i, 0)), sds((t, D), f32))
    xn = _resid_fwd(x, o, gate, coef)
    return xn, (x, h, gu, a, o)


def _ffn_bwd(dxn, saved, ln, w_in, w_out, coef):
    x, h, gu, a, o = saved
    g, shift, scale, gate = ln
    t = x.shape[0]
    tm = _mm_tile(t)
    tk = _pick(t, (2048, 1024, 512, 256, 128))
    sds = jax.ShapeDtypeStruct
    do, dgate = _resid_bwd(dxn, o, gate, coef)
    dgu = _ffn_dgu(do, w_out, gu).reshape(N_DEV, t, FFN_HS)
    dw_out = _mmx(a, do, "tn", "ffn_dwout", (FFN_SLABS, 1, t // tk),
                  pl.BlockSpec((1, tk, FFN_HS), lambda i, j, k: (i, k, 0)),
                  pl.BlockSpec((tk, D), lambda i, j, k: (k, 0)),
                  pl.BlockSpec((FFN_HS, D), lambda i, j, k: (i, 0)), sds((D_FF, D), f32))
    dh = _mmx(dgu, w_in, "nt", "ffn_dh", (t // tm, 1, N_DEV),
              pl.BlockSpec((1, tm, FFN_HS), lambda i, j, k: (k, i, 0)),
              pl.BlockSpec((1, D, FFN_HS), lambda i, j, k: (k, 0, 0)),
              pl.BlockSpec((tm, D), lambda i, j, k: (i, 0)), sds((t, D), f32))
    dw_in = _mmx(h, dgu, "tn", "ffn_dwin", (N_DEV, 1, t // tk),
                 pl.BlockSpec((tk, D), lambda i, j, k: (k, 0)),
                 pl.BlockSpec((1, tk, FFN_HS), lambda i, j, k: (i, k, 0)),
                 pl.BlockSpec((1, D, FFN_HS), lambda i, j, k: (i, 0, 0)), sds((N_DEV, D, FFN_HS), f32))
    dx, dg, dscale, dshift = _ln_bwd(x, g, scale, shift, dh, dxn)
    return dx, (dg, dshift, dscale, dgate), dw_in, dw_out


def _roll_rows(a, k):
    n = a.shape[0]
    return pltpu.roll(a, k % n, 0)


def _pool_windows(hx, first_row, reverse):
    outs = []
    for gi, w in enumerate(POOL_WINDOWS):
        acc = hx[:, gi * POOL_GROUP:(gi + 1) * POOL_GROUP]
        k = 1
        while k < w:
            acc = acc + _roll_rows(acc, -k if reverse else k)
            k *= 2
        outs.append(acc)
    return outs


def _pool_cnt(t_idx, w):
    return jnp.minimum(t_idx + 1, w).astype(f32)


def _pool_pooled(x_ref, xp_ref, g, scale, shift, i, tm):
    h = _adaln(x_ref[...], g, scale, shift)
    hp = _adaln(xp_ref[...], g, scale, shift)
    hp = jnp.where(i == 0, 0.0, hp)
    hx = jnp.concatenate([hp, h], axis=0)
    sums = _pool_windows(hx, 0, False)
    t_idx = i * tm + lax.broadcasted_iota(jnp.int32, (tm, 1), 0)
    pooled = []
    for gi, w in enumerate(POOL_WINDOWS):
        s = sums[gi][POOL_HALO:, :]
        pooled.append(s / _pool_cnt(t_idx, w) - h[:, gi * POOL_GROUP:(gi + 1) * POOL_GROUP])
    return h, pooled


def _pool_specs(t, tm):
    per = tm // POOL_HALO
    prev = pl.BlockSpec((POOL_HALO, D), lambda i: (jnp.maximum(i * per - 1, 0), 0))
    return [_tok_spec(tm, D), prev, _row_spec(D), _row_spec(D), _row_spec(D),
            pl.BlockSpec((4, POOL_GROUP, POOL_GROUP), lambda i: (0, 0, 0)), _row_spec(D), _row_spec(D)]


def _pool_fwd(x, ln, w, pscale):
    g, shift, scale, gate = ln
    t = x.shape[0]
    tm = _tok_tile(t)

    def body(x_ref, xp_ref, g_ref, sc_ref, sh_ref, w_ref, ps_ref, gt_ref, y_ref):
        i = pl.program_id(0)
        _, pooled = _pool_pooled(x_ref, xp_ref, g_ref[...], sc_ref[...], sh_ref[...], i, tm)
        mixed = [jnp.dot(pooled[gi].astype(bf16), w_ref[gi], preferred_element_type=f32) for gi in range(4)]
        y = jnp.concatenate(mixed, axis=1) * ps_ref[...]
        y_ref[...] = x_ref[...] + (1.0 + gt_ref[...]) * y

    return pl.pallas_call(
        body, name="pool_fwd", grid=(t // tm,), in_specs=_pool_specs(t, tm),
        out_specs=_tok_spec(tm, D), out_shape=jax.ShapeDtypeStruct((t, D), f32),
        compiler_params=_cp(1),
    )(x, x, g, scale, shift, w, pscale, gate)


def _pool_bwd(dxn, x, ln, w, pscale):
    g, shift, scale, gate = ln
    t = x.shape[0]
    tm = _tok_tile(t)
    nt = t // tm
    per = tm // POOL_HALO

    def body_a(x_ref, xp_ref, g_ref, sc_ref, sh_ref, w_ref, ps_ref, gt_ref, dxn_ref,
               dp_ref, dw_ref, dps_ref, dgt_ref):
        i = pl.program_id(0)
        first = i == 0
        _, pooled = _pool_pooled(x_ref, xp_ref, g_ref[...], sc_ref[...], sh_ref[...], i, tm)
        dxn_v = dxn_ref[...]
        dy = (1.0 + gt_ref[...]) * dxn_v
        dmixed = dy * ps_ref[...]
        mixed, dps = [], []
        for gi in range(4):
            sl = slice(gi * POOL_GROUP, (gi + 1) * POOL_GROUP)
            pb = pooled[gi].astype(bf16)
            dmb = dmixed[:, sl].astype(bf16)
            mixed.append(jnp.dot(pb, w_ref[gi], preferred_element_type=f32))
            dp_ref[:, sl] = lax.dot_general(dmb, w_ref[gi], _DN["nt"], preferred_element_type=f32)
            dwg = lax.dot_general(pb, dmb, _DN["tn"], preferred_element_type=f32)

            @pl.when(first)
            def _():
                dw_ref[gi] = dwg

            @pl.when(jnp.logical_not(first))
            def _():
                dw_ref[gi] += dwg
        mixed = jnp.concatenate(mixed, axis=1)
        _acc_add(first, dps_ref, jnp.sum(dy * mixed, axis=0, keepdims=True))
        _acc_add(first, dgt_ref, jnp.sum(dxn_v * (mixed * ps_ref[...]), axis=0, keepdims=True))

    row = jax.ShapeDtypeStruct((1, D), f32)
    dpooled, dw, dps, dgate = pl.pallas_call(
        body_a, name="pool_bwd_a", grid=(nt,), in_specs=_pool_specs(t, tm) + [_tok_spec(tm, D)],
        out_specs=[_tok_spec(tm, D), pl.BlockSpec((4, POOL_GROUP, POOL_GROUP), lambda i: (0, 0, 0)),
                   _row_spec(D), _row_spec(D)],
        out_shape=[jax.ShapeDtypeStruct((t, D), f32), jax.ShapeDtypeStruct((4, POOL_GROUP, POOL_GROUP), f32), row, row],
        compiler_params=_cp(1),
    )(x, x, g, scale, shift, w, pscale, gate, dxn)

    def body_b(dp_ref, dpn_ref, x_ref, g_ref, sc_ref, sh_ref, dxn_ref, dx_ref, dg_ref, dsc_ref, dsh_ref):
        i = pl.program_id(0)
        dp = dp_ref[...]
        dpn = jnp.where(i == nt - 1, 0.0, dpn_ref[...])
        t_idx = i * tm + lax.broadcasted_iota(jnp.int32, (tm + POOL_HALO, 1), 0)
        ex = jnp.concatenate([dp, dpn], axis=0)
        parts = []
        for gi, w_ in enumerate(POOL_WINDOWS):
            parts.append(ex[:, gi * POOL_GROUP:(gi + 1) * POOL_GROUP] / _pool_cnt(t_idx, w_))
        sums = _pool_windows(jnp.concatenate(parts, axis=1), 0, True)
        dh = jnp.concatenate([s[:tm, :] for s in sums], axis=1) - dp
        _, vjp = jax.vjp(_adaln, x_ref[...], g_ref[...], sc_ref[...], sh_ref[...])
        dx, dg, dsc, dsh = vjp(dh)
        dx_ref[...] = dxn_ref[...] + dx
        first = i == 0
        _acc_add(first, dg_ref, dg)
        _acc_add(first, dsc_ref, dsc)
        _acc_add(first, dsh_ref, dsh)

    nxt = pl.BlockSpec((POOL_HALO, D), lambda i: (jnp.minimum((i + 1) * per, t // POOL_HALO - 1), 0))
    dx, dg, dscale, dshift = pl.pallas_call(
        body_b, name="pool_bwd_b", grid=(nt,),
        in_specs=[_tok_spec(tm, D), nxt, _tok_spec(tm, D), _row_spec(D), _row_spec(D), _row_spec(D), _tok_spec(tm, D)],
        out_specs=[_tok_spec(tm, D), _row_spec(D), _row_spec(D), _row_spec(D)],
        out_shape=[jax.ShapeDtypeStruct((t, D), f32), row, row, row],
        compiler_params=_cp(1),
    )(dpooled, dpooled, x, g, scale, shift, dxn)
    return dx, (dg, dshift, dscale, dgate), dw, dps


def _conv_taps(czx, cw):
    return cw[0:1, :] * _roll_rows(czx, 2) + cw[1:2, :] * _roll_rows(czx, 1) + cw[2:3, :] * czx


def _conv_fwd(p, cw):
    t = p.shape[0]
    tm = _tok_tile(t)
    per = tm // CONV_HALO

    def body(p_ref, pp_ref, cw_ref, q_ref):
        i = pl.program_id(0)
        cz = p_ref[:, D:2 * D] * p_ref[:, 2 * D:3 * D]
        czp = jnp.where(i == 0, 0.0, pp_ref[:, D:2 * D] * pp_ref[:, 2 * D:3 * D])
        conv = _conv_taps(jnp.concatenate([czp, cz], axis=0), cw_ref[...])[CONV_HALO:, :]
        q_ref[...] = (p_ref[:, 0:D] * conv).astype(bf16)

    prev = pl.BlockSpec((CONV_HALO, 3 * D), lambda i: (jnp.maximum(i * per - 1, 0), 0))
    return pl.pallas_call(
        body, name="conv_fwd", grid=(t // tm,),
        in_specs=[_tok_spec(tm, 3 * D), prev, _row_spec(D, 8)],
        out_specs=_tok_spec(tm, D), out_shape=jax.ShapeDtypeStruct((t, D), bf16),
        compiler_params=_cp(1),
    )(p, p, cw)


def _conv_bwd(p, cw, dq):
    t = p.shape[0]
    tm = _tok_tile(t)
    nt = t // tm
    per = tm // CONV_HALO

    def body(p_ref, pp_ref, pn_ref, cw_ref, dq_ref, dqn_ref, dp_ref, dcw_ref):
        i = pl.program_id(0)
        cw_v = cw_ref[...]
        b, c, z = p_ref[:, 0:D], p_ref[:, D:2 * D], p_ref[:, 2 * D:3 * D]
        cz = c * z
        czp = jnp.where(i == 0, 0.0, pp_ref[:, D:2 * D] * pp_ref[:, 2 * D:3 * D])
        czx = jnp.concatenate([czp, cz], axis=0)
        conv = _conv_taps(czx, cw_v)[CONV_HALO:, :]
        dq_v = dq_ref[...]
        dconv = dq_v * b
        dconv_n = jnp.where(i == nt - 1, 0.0, dqn_ref[...] * pn_ref[:, 0:D])
        dcx = jnp.concatenate([dconv, dconv_n], axis=0)
        dcz = (cw_v[2:3, :] * dcx + cw_v[1:2, :] * _roll_rows(dcx, -1) + cw_v[0:1, :] * _roll_rows(dcx, -2))[:tm, :]
        dp_ref[:, 0:D] = (dq_v * conv).astype(bf16)
        dp_ref[:, D:2 * D] = (dcz * z).astype(bf16)
        dp_ref[:, 2 * D:3 * D] = (dcz * c).astype(bf16)
        dw2 = jnp.sum(dconv * cz, axis=0, keepdims=True)
        dw1 = jnp.sum(dconv * _roll_rows(czx, 1)[CONV_HALO:, :], axis=0, keepdims=True)
        dw0 = jnp.sum(dconv * _roll_rows(czx, 2)[CONV_HALO:, :], axis=0, keepdims=True)
        _acc_add(i == 0, dcw_ref, jnp.concatenate([dw0, dw1, dw2, jnp.zeros((5, D), f32)], axis=0))

    prev = pl.BlockSpec((CONV_HALO, 3 * D), lambda i: (jnp.maximum(i * per - 1, 0), 0))
    last = t // CONV_HALO - 1
    nxt3 = pl.BlockSpec((CONV_HALO, 3 * D), lambda i: (jnp.minimum((i + 1) * per, last), 0))
    nxt1 = pl.BlockSpec((CONV_HALO, D), lambda i: (jnp.minimum((i + 1) * per, last), 0))
    return pl.pallas_call(
        body, name="conv_bwd", grid=(nt,),
        in_specs=[_tok_spec(tm, 3 * D), prev, nxt3, _row_spec(D, 8), _tok_spec(tm, D), nxt1],
        out_specs=[_tok_spec(tm, 3 * D), _row_spec(D, 8)],
        out_shape=[jax.ShapeDtypeStruct((t, 3 * D), bf16), jax.ShapeDtypeStruct((8, D), f32)],
        compiler_params=_cp(1),
    )(p, p, p, cw, dq, dq)


def _convmix_fwd(x, ln, w_in, cw, w_out):
    g, shift, scale, gate = ln
    t = x.shape[0]
    tm = _mm_tile(t)
    cs = w_in.shape[2]
    h = _ln_fwd(x, g, scale, shift)
    p = _mmx(h, w_in, "nn", "conv_in", (t // tm, N_DEV, 1),
             pl.BlockSpec((tm, D), lambda i, j, k: (i, 0)),
             pl.BlockSpec((1, D, cs), lambda i, j, k: (j, 0, 0)),
             pl.BlockSpec((tm, cs), lambda i, j, k: (i, j)), jax.ShapeDtypeStruct((t, N_DEV * cs), f32))
    q = _conv_fwd(p, cw)
    y = _mm(q, w_out, "nn", f32, "conv_out")
    return _resid_fwd(x, y, gate, 1.0), (x, h, p, q, y)


def _convmix_bwd(dxn, saved, ln, w_in, cw, w_out):
    x, h, p, q, y = saved
    g, shift, scale, gate = ln
    dy, dgate = _resid_bwd(dxn, y, gate, 1.0)
    dq = _mm(dy, w_out, "nt", f32, "conv_dq")
    dw_out = _mm(q, dy, "tn", f32, "conv_dwout")
    dp, dcw = _conv_bwd(p, cw, dq)
    t = x.shape[0]
    tm = _mm_tile(t)
    tk = _pick(t, (512, 256, 128))
    cs = w_in.shape[2]
    dh = _mmx(dp, w_in, "nt", "conv_dh", (t // tm, 1, N_DEV),
              pl.BlockSpec((tm, cs), lambda i, j, k: (i, k)),
              pl.BlockSpec((1, D, cs), lambda i, j, k: (k, 0, 0)),
              pl.BlockSpec((tm, D), lambda i, j, k: (i, 0)), jax.ShapeDtypeStruct((t, D), f32))
    dw_in = _mmx(h, dp, "tn", "conv_dwin", (N_DEV, 1, t // tk),
                 pl.BlockSpec((tk, D), lambda i, j, k: (k, 0)),
                 pl.BlockSpec((tk, cs), lambda i, j, k: (k, i)),
                 pl.BlockSpec((1, D, cs), lambda i, j, k: (i, 0, 0)), jax.ShapeDtypeStruct((N_DEV, D, cs), f32))
    dx, dg, dscale, dshift = _ln_bwd(x, g, scale, shift, dh, dxn)
    return dx, (dg, dshift, dscale, dgate), dw_in, dcw[0:3], dw_out


def _exact_dot(tri, v):
    v1 = v.astype(bf16)
    r1 = v - v1.astype(f32)
    v2 = r1.astype(bf16)
    v3 = (r1 - v2.astype(f32)).astype(bf16)
    d = lambda p: jnp.dot(tri, p, preferred_element_type=f32)
    return d(v1) + d(v2) + d(v3)


def _fox_cumf(fl, b_f):
    t = fl.shape[0]
    tc = min(t, 256)

    def body(fl_ref, b_ref, f_ref, carry_ref):
        i = pl.program_id(0)

        @pl.when(i == 0)
        def _():
            carry_ref[...] = jnp.zeros_like(carry_ref)

        lf = jax.nn.log_sigmoid(fl_ref[...] + b_ref[...])
        r = lax.broadcasted_iota(jnp.int32, (tc, tc), 0)
        c = lax.broadcasted_iota(jnp.int32, (tc, tc), 1)
        tri = (r >= c).astype(bf16)
        fc = _exact_dot(tri, lf) + carry_ref[0:1, :]
        f_ref[...] = fc
        carry_ref[0:1, :] = fc[tc - 1:tc, :]

    return pl.pallas_call(
        body, name="fox_cumf", grid=(t // tc,),
        in_specs=[_tok_spec(tc, 128), _row_spec(128)],
        out_specs=_tok_spec(tc, 128), out_shape=jax.ShapeDtypeStruct((t, 128), f32),
        scratch_shapes=[pltpu.VMEM((8, 128), f32)], compiler_params=_cp(1),
    )(fl, b_f)


def _fox_cumf_bwd(df, fl, b_f):
    t = fl.shape[0]
    tc = min(t, 256)
    nt = t // tc

    def body(df_ref, fl_ref, b_ref, dfl_ref, db_ref, carry_ref):
        i = pl.program_id(0)

        @pl.when(i == 0)
        def _():
            carry_ref[...] = jnp.zeros_like(carry_ref)

        r = lax.broadcasted_iota(jnp.int32, (tc, tc), 0)
        c = lax.broadcasted_iota(jnp.int32, (tc, tc), 1)
        tri = (r <= c).astype(bf16)
        dlf = _exact_dot(tri, df_ref[...]) + carry_ref[0:1, :]
        carry_ref[0:1, :] = dlf[0:1, :]
        dfl = dlf * jax.nn.sigmoid(-(fl_ref[...] + b_ref[...]))
        dfl_ref[...] = dfl
        _acc_add(i == 0, db_ref, jnp.sum(dfl, axis=0, keepdims=True))

    return pl.pallas_call(
        body, name="fox_cumf_bwd", grid=(nt,),
        in_specs=[_tok_spec(tc, 128, nt, True), _tok_spec(tc, 128, nt, True), _row_spec(128)],
        out_specs=[_tok_spec(tc, 128, nt, True), _row_spec(128)],
        out_shape=[jax.ShapeDtypeStruct((t, 128), f32), jax.ShapeDtypeStruct((1, 128), f32)],
        scratch_shapes=[pltpu.VMEM((8, 128), f32)], compiler_params=_cp(1),
    )(df, fl, b_f)


def _head_norm(v, gain, mult):
    return v * lax.rsqrt(jnp.mean(v * v, axis=-1, keepdims=True) + NORM_EPS) * gain * mult


def _fox_qknorm(qkv, q_gain, k_gain):
    t = qkv.shape[2]
    tm = _tok_tile(t)

    def body(q_ref, k_ref, v_ref, qg_ref, kg_ref, o_ref):
        o_ref[0, 0] = _head_norm(q_ref[0, 0], qg_ref[...], FOX_HEAD_DIM ** -0.5).astype(bf16)
        o_ref[1, 0] = _head_norm(k_ref[0, 0], kg_ref[...], 1.0).astype(bf16)
        o_ref[2, 0] = v_ref[0, 0].astype(bf16)

    blk = lambda s: pl.BlockSpec((1, 1, tm, FOX_HEAD_DIM), lambda h, i: (s, h, i, 0))
    gspec = pl.BlockSpec((1, FOX_HEAD_DIM), lambda h, i: (0, 0))
    return pl.pallas_call(
        body, name="fox_qknorm", grid=(FOX_HEADS, t // tm),
        in_specs=[blk(0), blk(1), blk(2), gspec, gspec],
        out_specs=pl.BlockSpec((3, 1, tm, FOX_HEAD_DIM), lambda h, i: (0, h, i, 0)),
        out_shape=jax.ShapeDtypeStruct((3, FOX_HEADS, t, FOX_HEAD_DIM), bf16),
        compiler_params=_cp(2),
    )(qkv, qkv, qkv, q_gain, k_gain)


def _fox_qknorm_bwd(qkv, q_gain, k_gain, dqn, dkn):
    t = qkv.shape[2]
    tm = _tok_tile(t)

    def body(q_ref, k_ref, qg_ref, kg_ref, dqn_ref, dkn_ref, o_ref, dqg_ref, dkg_ref):
        first = jnp.logical_and(pl.program_id(0) == 0, pl.program_id(1) == 0)
        _, vq = jax.vjp(lambda v, gn: _head_norm(v, gn, FOX_HEAD_DIM ** -0.5), q_ref[0, 0], qg_ref[...])
        dq, dqg = vq(dqn_ref[0])
        _, vk = jax.vjp(lambda v, gn: _head_norm(v, gn, 1.0), k_ref[0, 0], kg_ref[...])
        dk, dkg = vk(dkn_ref[0])
        o_ref[0, 0] = dq
        o_ref[1, 0] = dk
        _acc_add(first, dqg_ref, dqg)
        _acc_add(first, dkg_ref, dkg)

    blk = lambda s: pl.BlockSpec((1, 1, tm, FOX_HEAD_DIM), lambda h, i: (s, h, i, 0))
    hblk = pl.BlockSpec((1, tm, FOX_HEAD_DIM), lambda h, i: (h, i, 0))
    gspec = pl.BlockSpec((1, FOX_HEAD_DIM), lambda h, i: (0, 0))
    gshape = jax.ShapeDtypeStruct((1, FOX_HEAD_DIM), f32)
    return pl.pallas_call(
        body, name="fox_qknorm_bwd", grid=(FOX_HEADS, t // tm),
        in_specs=[blk(0), blk(1), gspec, gspec, hblk, hblk],
        out_specs=[pl.BlockSpec((2, 1, tm, FOX_HEAD_DIM), lambda h, i: (0, h, i, 0)), gspec, gspec],
        out_shape=[jax.ShapeDtypeStruct((2, FOX_HEADS, t, FOX_HEAD_DIM), f32), gshape, gshape],
        compiler_params=_cp(2),
    )(qkv, qkv, q_gain, k_gain, dqn, dkn)


def _fox_block(t):
    return min(t, 512)


FOX_STRIP = 64


def _fox_mask(s, row0):
    r = row0 + lax.broadcasted_iota(jnp.int32, s.shape, 0)
    c = lax.broadcasted_iota(jnp.int32, s.shape, 1)
    return jnp.where(r >= c, s, -jnp.inf)


def _strips(tb):
    ts = min(tb, FOX_STRIP)
    return [(r * ts, slice(r * ts, (r + 1) * ts)) for r in range(tb // ts)]


def _fox_attn_fwd(qkvn, fcol, fref):
    t = qkvn.shape[2]
    tb = _fox_block(t)
    nq = t // tb
    dh = FOX_HEAD_DIM

    def body(q_ref, k_ref, v_ref, fc_ref, fr_ref, o_ref, lse_ref):
        i = pl.program_id(1)
        q = q_ref[0, 0]
        fref_i = fr_ref[0, pl.ds(i, 1), 0:1]

        def step(j, carry, diag):
            m, l, acc = carry
            off = pl.multiple_of(j * tb, tb)
            kj = k_ref[0, 0, pl.ds(off, tb), :]
            vj = v_ref[0, 0, pl.ds(off, tb), :]
            s = lax.dot_general(q, kj, _DN["nt"], preferred_element_type=f32) + (fref_i - fc_ref[0, :, pl.ds(off, tb)])
            if diag:
                s = _fox_mask(s, 0)
            m_new = jnp.maximum(m, jnp.max(s, axis=-1, keepdims=True))
            alpha = jnp.exp(m - m_new)
            p = jnp.exp(s - m_new)
            l = alpha * l + jnp.sum(p, axis=-1, keepdims=True)
            p_hi = p.astype(bf16)
            p_lo = (p - p_hi.astype(f32)).astype(bf16)
            pv = jnp.dot(p_hi, vj, preferred_element_type=f32) + jnp.dot(p_lo, vj, preferred_element_type=f32)
            return m_new, l, alpha * acc + pv

        init = (jnp.full((tb, 1), -jnp.inf, f32), jnp.zeros((tb, 1), f32), jnp.zeros((tb, dh), f32))
        carry = lax.fori_loop(0, i, lambda j, c: step(j, c, False), init)
        m, l, acc = step(i, carry, True)
        o_ref[0] = acc / l
        lse_ref[0] = jnp.broadcast_to(m + jnp.log(l), (tb, 128))

    return pl.pallas_call(
        body, name="fox_attn_fwd", grid=(FOX_HEADS, nq),
        in_specs=[pl.BlockSpec((1, 1, tb, dh), lambda h, i: (0, h, i, 0)),
                  pl.BlockSpec((1, 1, t, dh), lambda h, i: (1, h, 0, 0)),
                  pl.BlockSpec((1, 1, t, dh), lambda h, i: (2, h, 0, 0)),
                  pl.BlockSpec((1, 1, t), lambda h, i: (h, 0, 0)),
                  pl.BlockSpec((1, nq, 128), lambda h, i: (h, 0, 0))],
        out_specs=[pl.BlockSpec((1, tb, dh), lambda h, i: (h, i, 0)),
                   pl.BlockSpec((1, tb, 128), lambda h, i: (h, i, 0))],
        out_shape=[jax.ShapeDtypeStruct((FOX_HEADS, t, dh), f32), jax.ShapeDtypeStruct((FOX_HEADS, t, 128), f32)],
        compiler_params=_cp(2),
    )(qkvn, qkvn, qkvn, fcol, fref)


def _fox_delta(o, do):
    t = o.shape[1]
    tm = _tok_tile(t)
    dh = FOX_HEAD_DIM

    def body(o_ref, do_ref, dl_ref, dob_ref):
        dob = do_ref[0].astype(bf16)
        dob_ref[0] = dob
        dl_ref[0] = jnp.broadcast_to(jnp.sum(dob.astype(f32) * o_ref[0], axis=-1, keepdims=True), (tm, 128))

    hb = pl.BlockSpec((1, tm, dh), lambda h, i: (h, i, 0))
    return pl.pallas_call(
        body, name="fox_delta", grid=(FOX_HEADS, t // tm), in_specs=[hb, hb],
        out_specs=[pl.BlockSpec((1, tm, 128), lambda h, i: (h, i, 0)), hb],
        out_shape=[jax.ShapeDtypeStruct((FOX_HEADS, t, 128), f32), jax.ShapeDtypeStruct((FOX_HEADS, t, dh), bf16)],
        compiler_params=_cp(2),
    )(o, do)


def _fox_attn_bwd_kv(qkvn, fcol, fref, lse, delta, dob):
    t = qkvn.shape[2]
    tb = _fox_block(t)
    nq = t // tb
    dh = FOX_HEAD_DIM

    def body(q_ref, k_ref, v_ref, fc_ref, fr_ref, lse_ref, dl_ref, do_ref, dk_ref, dv_ref, df_ref, dq_ref,
             s_ref, dp_ref, p_ref, ds_ref):
        j = pl.program_id(1)
        kj = k_ref[0, 0]
        vj = v_ref[0, 0]
        fcol_j = fc_ref[0]
        dk_ref[...] = jnp.zeros_like(dk_ref)
        dv_ref[...] = jnp.zeros_like(dv_ref)
        df_ref[...] = jnp.zeros_like(df_ref)

        @pl.when(j == 0)
        def _():
            dq_ref[...] = jnp.zeros_like(dq_ref)

        def step(i, diag):
            off = pl.multiple_of(i * tb, tb)
            qi = q_ref[0, 0, pl.ds(off, tb), :]
            doi = do_ref[0, pl.ds(off, tb), :]
            s_ref[...] = lax.dot_general(qi, kj, _DN["nt"], preferred_element_type=f32)
            dp_ref[...] = lax.dot_general(doi, vj, _DN["nt"], preferred_element_type=f32)
            bias = fr_ref[0, pl.ds(i, 1), 0:1] - fcol_j
            df = jnp.zeros((1, tb), f32)
            for row0, rs in _strips(tb):
                rows = pl.ds(off + row0, rs.stop - rs.start)
                s = s_ref[rs, :] + bias
                if diag:
                    s = _fox_mask(s, row0)
                p = jnp.exp(s - lse_ref[0, rows, 0:1])
                p_ref[rs, :] = p.astype(bf16)
                ds = p * (dp_ref[rs, :] - dl_ref[0, rows, 0:1])
                ds_ref[rs, :] = ds.astype(bf16)
                df = df + jnp.sum(ds, axis=0, keepdims=True)
            dv_ref[0] += lax.dot_general(p_ref[...], doi, _DN["tn"], preferred_element_type=f32)
            dk_ref[0] += lax.dot_general(ds_ref[...], qi, _DN["tn"], preferred_element_type=f32)
            dq_ref[0, pl.ds(off, tb), :] += jnp.dot(ds_ref[...], kj, preferred_element_type=f32)
            df_ref[0] -= df

        def loop_body(i, carry):
            step(i, False)
            return carry

        step(j, True)
        lax.fori_loop(j + 1, nq, loop_body, 0)

    scratch = [pltpu.VMEM((tb, tb), f32), pltpu.VMEM((tb, tb), f32), pltpu.VMEM((tb, tb), bf16),
               pltpu.VMEM((tb, tb), bf16)]
    hb = pl.BlockSpec((1, tb, dh), lambda h, j: (h, j, 0))
    full = pl.BlockSpec((1, t, dh), lambda h, j: (h, 0, 0))
    full128 = pl.BlockSpec((1, t, 128), lambda h, j: (h, 0, 0))
    hshape = jax.ShapeDtypeStruct((FOX_HEADS, t, dh), f32)
    return pl.pallas_call(
        body, name="fox_attn_bwd_kv", grid=(FOX_HEADS, nq),
        in_specs=[pl.BlockSpec((1, 1, t, dh), lambda h, j: (0, h, 0, 0)),
                  pl.BlockSpec((1, 1, tb, dh), lambda h, j: (1, h, j, 0)),
                  pl.BlockSpec((1, 1, tb, dh), lambda h, j: (2, h, j, 0)),
                  pl.BlockSpec((1, 1, tb), lambda h, j: (h, 0, j)),
                  pl.BlockSpec((1, nq, 128), lambda h, j: (h, 0, 0)),
                  full128, full128, full],
        out_specs=[hb, hb, pl.BlockSpec((1, 1, tb), lambda h, j: (h, 0, j)), full],
        out_shape=[hshape, hshape, jax.ShapeDtypeStruct((FOX_HEADS, 1, t), f32), hshape],
        scratch_shapes=scratch, compiler_params=_cp(2),
    )(qkvn, qkvn, qkvn, fcol, fref, lse, delta, dob)


def _heads_of(a, n):
    t = a.shape[0]
    return a.reshape(t, n, FOX_HEADS, FOX_HEAD_DIM).transpose(1, 2, 0, 3)


def _fox_fwd(x, ln, w_in, b_f, q_gain, k_gain, w_o):
    g, shift, scale, gate = ln
    t = x.shape[0]
    tb = _fox_block(t)
    h = _ln_fwd(x, g, scale, shift)
    proj = _mm(h, w_in, "nn", f32, "fox_in")
    qkv = _heads_of(proj[:, :3 * D], 3)
    fl = proj[:, 3 * D:3 * D + 128]
    fcum = _fox_cumf(fl, b_f)
    fcol = fcum[:, :FOX_HEADS].T.reshape(FOX_HEADS, 1, t)
    fref = jnp.broadcast_to(fcol[:, 0, ::tb][:, :, None], (FOX_HEADS, t // tb, 128))
    qkvn = _fox_qknorm(qkv, q_gain, k_gain)
    o, lse = _fox_attn_fwd(qkvn, fcol, fref)
    ob = o.transpose(1, 0, 2).reshape(t, D).astype(bf16)
    y = _mm(ob, w_o, "nn", f32, "fox_out")
    return _resid_fwd(x, y, gate, 1.0), (x, h, qkv, fl, fcol, fref, qkvn, o, lse, ob, y)


def _fox_bwd(dxn, saved, ln, w_in, b_f, q_gain, k_gain, w_o):
    x, h, qkv, fl, fcol, fref, qkvn, o, lse, ob, y = saved
    g, shift, scale, gate = ln
    t = x.shape[0]
    dy, dgate = _resid_bwd(dxn, y, gate, 1.0)
    do_flat = _mm(dy, w_o, "nt", f32, "fox_do")
    dw_o = _mm(ob, dy, "tn", f32, "fox_dwo")
    do = do_flat.reshape(t, FOX_HEADS, FOX_HEAD_DIM).transpose(1, 0, 2)
    delta, dob = _fox_delta(o, do)
    dkn, dv, dfcol, dqn = _fox_attn_bwd_kv(qkvn, fcol, fref, lse, delta, dob)
    dqk, dqg, dkg = _fox_qknorm_bwd(qkv, q_gain, k_gain, dqn, dkn)
    df = jnp.pad(dfcol.reshape(FOX_HEADS, t).T, ((0, 0), (0, 128 - FOX_HEADS)))
    dfl, db_f = _fox_cumf_bwd(df, fl, b_f)
    dqkv = jnp.concatenate([dqk, dv[None]], axis=0).transpose(2, 0, 1, 3).reshape(t, 3 * D)
    dproj = jnp.concatenate([dqkv.astype(bf16), dfl.astype(bf16)], axis=1)
    dh = _mm(dproj, w_in, "nt", f32, "fox_dh")
    dw_in = _mm(h, dproj, "tn", f32, "fox_dwin")
    dx, dg, dscale, dshift = _ln_bwd(x, g, scale, shift, dh, dxn)
    return dx, (dg, dshift, dscale, dgate), dw_in, db_f, dqg, dkg, dw_o


def _s5_disc(lam_re, lam_im, log_dt, b_re, b_im):
    dt = jnp.exp(log_dt)
    mag = jnp.exp(lam_re * dt)
    lb_re, lb_im = mag * jnp.cos(lam_im * dt), mag * jnp.sin(lam_im * dt)
    den = lam_re * lam_re + lam_im * lam_im
    nr, ni = lb_re - 1.0, lb_im
    k_re = (nr * lam_re + ni * lam_im) / den
    k_im = (ni * lam_re - nr * lam_im) / den
    return lb_re, lb_im, k_re * b_re - k_im * b_im, k_re * b_im + k_im * b_re


def _s5_prep(lam_re, lam_im, log_dt, b_re, b_im):
    def body(ar_ref, ai_ref, dt_ref, br_ref, bi_ref, lr_ref, li_ref, bbr_ref, bbi_ref):
        lr, li, bbr, bbi = _s5_disc(ar_ref[...], ai_ref[...], dt_ref[...], br_ref[...], bi_ref[...])
        lr_ref[...] = lr
        li_ref[...] = li
        bbr_ref[...] = bbr
        bbi_ref[...] = bbi

    small = jax.ShapeDtypeStruct(lam_re.shape, f32)
    bigs = jax.ShapeDtypeStruct(b_re.shape, f32)
    return pl.pallas_call(body, name="s5_prep", out_shape=[small, small, bigs, bigs])(lam_re, lam_im, log_dt, b_re, b_im)


def _s5_prep_bwd(lam_re, lam_im, log_dt, b_re, b_im, dlr, dli, dbbr, dbbi):
    def body(ar_ref, ai_ref, dt_ref, br_ref, bi_ref, dlr_ref, dli_ref, dbbr_ref, dbbi_ref,
             dar_ref, dai_ref, ddt_ref, dbr_ref, dbi_ref):
        _, vjp = jax.vjp(_s5_disc, ar_ref[...], ai_ref[...], dt_ref[...], br_ref[...], bi_ref[...])
        dar, dai, ddt, dbr, dbi = vjp((dlr_ref[...], dli_ref[...], dbbr_ref[...], dbbi_ref[...]))
        dar_ref[...] = dar
        dai_ref[...] = dai
        ddt_ref[...] = jnp.broadcast_to(jnp.sum(ddt, axis=-1, keepdims=True), ddt.shape)
        dbr_ref[...] = dbr
        dbi_ref[...] = dbi

    small = jax.ShapeDtypeStruct(lam_re.shape, f32)
    bigs = jax.ShapeDtypeStruct(b_re.shape, f32)
    return pl.pallas_call(body, name="s5_prep_bwd", out_shape=[small, small, small, bigs, bigs])(
        lam_re, lam_im, log_dt, b_re, b_im, dlr, dli, dbbr, dbbi)


def _s5_tile(t):
    return min(t, 128)


def _s5_blk(k, width):
    return slice(k * width, (k + 1) * width)


def _s5_in_bd(bb):
    b4 = bb.reshape(S5_BLOCKS, 8, S5_GROUP, S5_STATE)
    return jnp.einsum("kgin,gh->kgihn", b4, jnp.eye(8, dtype=bb.dtype)).reshape(S5_BLOCKS, S5_BCH, S5_BST)


def _s5_in_bd_diag(bd):
    b5 = bd.reshape(S5_BLOCKS, 8, S5_GROUP, 8, S5_STATE)
    return jnp.einsum("kgihn,gh->kgin", b5, jnp.eye(8, dtype=bd.dtype)).reshape(S5_GROUPS, S5_GROUP, S5_STATE)


def _s5_out_bd(c):
    c4 = c.reshape(S5_BLOCKS, 8, S5_GROUP, S5_STATE)
    return jnp.einsum("kgin,gh->kgnhi", c4, jnp.eye(8, dtype=c.dtype)).reshape(S5_BLOCKS, S5_BST, S5_BCH)


def _s5_out_bd_diag(bd):
    c5 = bd.reshape(S5_BLOCKS, 8, S5_STATE, 8, S5_GROUP)
    return jnp.einsum("kgnhi,gh->kgin", c5, jnp.eye(8, dtype=bd.dtype)).reshape(S5_GROUPS, S5_GROUP, S5_STATE)


def _s5_scan_fwd(x, ln, lb_re, lb_im, bbr_bd, bbi_bd, cr_bd, ci_bd, dskip):
    g, shift, scale, _ = ln
    t = x.shape[0]
    tm = _s5_tile(t)
    ns = S5_NSTATE

    def body(x_ref, g_ref, sc_ref, sh_ref, ar_ref, ai_ref, bbr_ref, bbi_ref, cr_ref, ci_ref, d_ref,
             yy_ref, xr_ref, xi_ref, cre_ref, cim_ref):
        @pl.when(pl.program_id(0) == 0)
        def _():
            cre_ref[...] = jnp.zeros_like(cre_ref)
            cim_ref[...] = jnp.zeros_like(cim_ref)

        h = _adaln(x_ref[...], g_ref[...], sc_ref[...], sh_ref[...])
        ub = h.astype(bf16)
        for k in range(S5_BLOCKS):
            uk = ub[:, _s5_blk(k, S5_BCH)]
            xr_ref[:, _s5_blk(k, S5_BST)] = jnp.dot(uk, bbr_ref[k], preferred_element_type=f32)
            xi_ref[:, _s5_blk(k, S5_BST)] = jnp.dot(uk, bbi_ref[k], preferred_element_type=f32)
        ar, ai = ar_ref[...], ai_ref[...]

        def step(tt, carry):
            sr, si = carry
            row = pl.ds(tt, 1)
            nr = (ar * sr - ai * si) + xr_ref[row, :]
            ni = (ar * si + ai * sr) + xi_ref[row, :]
            xr_ref[row, :] = nr
            xi_ref[row, :] = ni
            return nr, ni

        sr, si = lax.fori_loop(0, tm, step, (cre_ref[0:1, :], cim_ref[0:1, :]), unroll=2)
        cre_ref[0:1, :] = sr
        cim_ref[0:1, :] = si
        for k in range(S5_BLOCKS):
            sb = _s5_blk(k, S5_BST)
            yk = (jnp.dot(xr_ref[:, sb].astype(bf16), cr_ref[k], preferred_element_type=f32)
                  - jnp.dot(xi_ref[:, sb].astype(bf16), ci_ref[k], preferred_element_type=f32))
            cb = _s5_blk(k, S5_BCH)
            yy_ref[:, cb] = yk + d_ref[:, cb] * h[:, cb]

    bd_in = pl.BlockSpec((S5_BLOCKS, S5_BCH, S5_BST), lambda i: (0, 0, 0))
    bd_out = pl.BlockSpec((S5_BLOCKS, S5_BST, S5_BCH), lambda i: (0, 0, 0))
    st = jax.ShapeDtypeStruct((t, ns), f32)
    return pl.pallas_call(
        body, name="s5_scan_fwd", grid=(t // tm,),
        in_specs=[_tok_spec(tm, D), _row_spec(D), _row_spec(D), _row_spec(D), _row_spec(ns), _row_spec(ns),
                  bd_in, bd_in, bd_out, bd_out, _row_spec(D)],
        out_specs=[_tok_spec(tm, D), _tok_spec(tm, ns), _tok_spec(tm, ns)],
        out_shape=[jax.ShapeDtypeStruct((t, D), f32), st, st],
        scratch_shapes=[pltpu.VMEM((8, ns), f32), pltpu.VMEM((8, ns), f32)],
        compiler_params=_cp(1),
    )(x, g, scale, shift, lb_re, lb_im, bbr_bd, bbi_bd, cr_bd, ci_bd, dskip)


def _s5_scan_bwd(dyy, x, ln, xr, xi, lb_re, lb_im, bbr_bd, bbi_bd, cr_bd, ci_bd, dskip):
    g, shift, scale, _ = ln
    t = x.shape[0]
    tm = _s5_tile(t)
    nt = t // tm
    ns = S5_NSTATE
    per = tm // 8

    def body(dyy_ref, x_ref, g_ref, sc_ref, sh_ref, xr_ref, xi_ref, xrp_ref, xip_ref, ar_ref, ai_ref,
             bbr_ref, bbi_ref, cr_ref, ci_ref, d_ref,
             du_ref, dar_ref, dai_ref, dbbr_ref, dbbi_ref, dcr_ref, dci_ref, dd_ref,
             gr_ref, gi_ref, cre_ref, cim_ref):
        i = pl.program_id(0)
        first = i == 0

        @pl.when(first)
        def _():
            cre_ref[...] = jnp.zeros_like(cre_ref)
            cim_ref[...] = jnp.zeros_like(cim_ref)

        h = _adaln(x_ref[...], g_ref[...], sc_ref[...], sh_ref[...])
        ub = h.astype(bf16)
        dyy_v = dyy_ref[...]
        dyb = dyy_v.astype(bf16)
        for k in range(S5_BLOCKS):
            dk = dyb[:, _s5_blk(k, S5_BCH)]
            sb = _s5_blk(k, S5_BST)
            gr_ref[:, sb] = lax.dot_general(dk, cr_ref[k], _DN["nt"], preferred_element_type=f32)
            gi_ref[:, sb] = -lax.dot_general(dk, ci_ref[k], _DN["nt"], preferred_element_type=f32)
        ar, ai = ar_ref[...], ai_ref[...]

        def step(s, carry):
            nr_, ni_ = carry
            row = pl.ds(tm - 1 - s, 1)
            nr = gr_ref[row, :] + (ar * nr_ + ai * ni_)
            ni = gi_ref[row, :] + (ar * ni_ - ai * nr_)
            gr_ref[row, :] = nr
            gi_ref[row, :] = ni
            return nr, ni

        nr, ni = lax.fori_loop(0, tm, step, (cre_ref[0:1, :], cim_ref[0:1, :]), unroll=2)
        cre_ref[0:1, :] = nr
        cim_ref[0:1, :] = ni

        is_first_tile = i == nt - 1
        xr_v, xi_v = xr_ref[...], xi_ref[...]
        xrp = jnp.where(is_first_tile, 0.0, xrp_ref[...])
        xip = jnp.where(is_first_tile, 0.0, xip_ref[...])
        xr_s = _roll_rows(jnp.concatenate([xrp, xr_v], axis=0), 1)[8:, :]
        xi_s = _roll_rows(jnp.concatenate([xip, xi_v], axis=0), 1)[8:, :]
        gr, gi = gr_ref[...], gi_ref[...]
        _acc_add(first, dar_ref, jnp.sum(gr * xr_s + gi * xi_s, axis=0, keepdims=True))
        _acc_add(first, dai_ref, jnp.sum(gi * xr_s - gr * xi_s, axis=0, keepdims=True))
        _acc_add(first, dd_ref, jnp.sum(dyy_v * h, axis=0, keepdims=True))
        grb, gib = gr.astype(bf16), gi.astype(bf16)
        xrb, xib = xr_v.astype(bf16), xi_v.astype(bf16)
        for k in range(S5_BLOCKS):
            cb, sb = _s5_blk(k, S5_BCH), _s5_blk(k, S5_BST)
            uk, dk = ub[:, cb], dyb[:, cb]
            tn = lambda a_, b_: lax.dot_general(a_, b_, _DN["tn"], preferred_element_type=f32)
            vals = (tn(uk, grb[:, sb]), tn(uk, gib[:, sb]), tn(xrb[:, sb], dk), -tn(xib[:, sb], dk))
            for ref, val in zip((dbbr_ref, dbbi_ref, dcr_ref, dci_ref), vals):
                @pl.when(first)
                def _():
                    ref[k] = val

                @pl.when(jnp.logical_not(first))
                def _():
                    ref[k] += val
            du_k = (lax.dot_general(grb[:, sb], bbr_ref[k], _DN["nt"], preferred_element_type=f32)
                    + lax.dot_general(gib[:, sb], bbi_ref[k], _DN["nt"], preferred_element_type=f32))
            du_ref[:, cb] = du_k + d_ref[:, cb] * dyy_v[:, cb]

    rev = lambda c: _tok_spec(tm, c, nt, True)
    prev = pl.BlockSpec((8, ns), lambda i: (jnp.maximum((nt - 1 - i) * per - 1, 0), 0))
    bd_in = pl.BlockSpec((S5_BLOCKS, S5_BCH, S5_BST), lambda i: (0, 0, 0))
    bd_out = pl.BlockSpec((S5_BLOCKS, S5_BST, S5_BCH), lambda i: (0, 0, 0))
    row_ns = jax.ShapeDtypeStruct((1, ns), f32)
    bd_in_s = jax.ShapeDtypeStruct((S5_BLOCKS, S5_BCH, S5_BST), f32)
    bd_out_s = jax.ShapeDtypeStruct((S5_BLOCKS, S5_BST, S5_BCH), f32)
    return pl.pallas_call(
        body, name="s5_scan_bwd", grid=(nt,),
        in_specs=[rev(D), rev(D), _row_spec(D), _row_spec(D), _row_spec(D), rev(ns), rev(ns), prev, prev,
                  _row_spec(ns), _row_spec(ns), bd_in, bd_in, bd_out, bd_out, _row_spec(D)],
        out_specs=[rev(D), _row_spec(ns), _row_spec(ns), bd_in, bd_in, bd_out, bd_out, _row_spec(D)],
        out_shape=[jax.ShapeDtypeStruct((t, D), f32), row_ns, row_ns, bd_in_s, bd_in_s, bd_out_s, bd_out_s,
                   jax.ShapeDtypeStruct((1, D), f32)],
        scratch_shapes=[pltpu.VMEM((tm, ns), f32), pltpu.VMEM((tm, ns), f32),
                        pltpu.VMEM((8, ns), f32), pltpu.VMEM((8, ns), f32)],
        compiler_params=_cp(1),
    )(dyy, x, g, scale, shift, xr, xi, xr, xi, lb_re, lb_im, bbr_bd, bbi_bd, cr_bd, ci_bd, dskip)


def _s5_gelu(yy):
    t = yy.shape[0]
    tm = _tok_tile(t)

    def body(y_ref, o_ref):
        o_ref[...] = jax.nn.gelu(y_ref[...]).astype(bf16)

    return pl.pallas_call(
        body, name="s5_gelu", grid=(t // tm,), in_specs=[_tok_spec(tm, D)], out_specs=_tok_spec(tm, D),
        out_shape=jax.ShapeDtypeStruct((t, D), bf16), compiler_params=_cp(1),
    )(yy)


def _s5_glu(gl, z):
    return gl * jax.nn.sigmoid(z)


def _s5_out(x, yy, z, gate):
    t = x.shape[0]
    tm = _tok_tile(t)

    def body(x_ref, y_ref, z_ref, gt_ref, o_ref):
        o_ref[...] = x_ref[...] + (1.0 + gt_ref[...]) * _s5_glu(jax.nn.gelu(y_ref[...]), z_ref[...])

    return pl.pallas_call(
        body, name="s5_out", grid=(t // tm,),
        in_specs=[_tok_spec(tm, D), _tok_spec(tm, D), _tok_spec(tm, D), _row_spec(D)],
        out_specs=_tok_spec(tm, D), out_shape=jax.ShapeDtypeStruct((t, D), f32), compiler_params=_cp(1),
    )(x, yy, z, gate)


def _s5_out_bwd(dxn, yy, z, gate):
    t = dxn.shape[0]
    tm = _tok_tile(t)

    def body(dxn_ref, y_ref, z_ref, gt_ref, dz_ref, dgl_ref, dgt_ref):
        dxn_v = dxn_ref[...]
        gl = jax.nn.gelu(y_ref[...])
        out, vjp = jax.vjp(_s5_glu, gl, z_ref[...])
        dgl, dz = vjp((1.0 + gt_ref[...]) * dxn_v)
        dz_ref[...] = dz.astype(bf16)
        dgl_ref[...] = dgl
        _acc_add(pl.program_id(0) == 0, dgt_ref, jnp.sum(dxn_v * out, axis=0, keepdims=True))

    return pl.pallas_call(
        body, name="s5_out_bwd", grid=(t // tm,),
        in_specs=[_tok_spec(tm, D), _tok_spec(tm, D), _tok_spec(tm, D), _row_spec(D)],
        out_specs=[_tok_spec(tm, D), _tok_spec(tm, D), _row_spec(D)],
        out_shape=[jax.ShapeDtypeStruct((t, D), bf16), jax.ShapeDtypeStruct((t, D), f32),
                   jax.ShapeDtypeStruct((1, D), f32)],
        compiler_params=_cp(1),
    )(dxn, yy, z, gate)


def _s5_gelu_bwd(yy, dgl_a, dgl_b):
    t = yy.shape[0]
    tm = _tok_tile(t)

    def body(y_ref, a_ref, b_ref, o_ref):
        _, vjp = jax.vjp(jax.nn.gelu, y_ref[...])
        o_ref[...] = vjp(a_ref[...] + b_ref[...])[0]

    return pl.pallas_call(
        body, name="s5_gelu_bwd", grid=(t // tm,),
        in_specs=[_tok_spec(tm, D), _tok_spec(tm, D), _tok_spec(tm, D)],
        out_specs=_tok_spec(tm, D), out_shape=jax.ShapeDtypeStruct((t, D), f32), compiler_params=_cp(1),
    )(yy, dgl_a, dgl_b)


def _s5_params(lam_re, lam_im, log_dt, b_re, b_im):
    bc = lambda a: a.reshape(S5_GROUPS, 1, -1)
    return (bc(lam_re), bc(lam_im), jnp.broadcast_to(log_dt.reshape(S5_GROUPS, 1, 1), (S5_GROUPS, 1, S5_STATE)),
            b_re.transpose(0, 2, 1), b_im.transpose(0, 2, 1))


def _s5_fwd(x, ln, raw, c_re, c_im, dskip, w_glu):
    gate = ln[3]
    lb_re, lb_im, bb_re, bb_im = _s5_prep(*raw)
    lbr, lbi = lb_re.reshape(1, S5_NSTATE), lb_im.reshape(1, S5_NSTATE)
    bds = (_s5_in_bd(bb_re).astype(bf16), _s5_in_bd(bb_im).astype(bf16),
           _s5_out_bd(c_re).astype(bf16), _s5_out_bd(c_im).astype(bf16))
    yy, xr, xi = _s5_scan_fwd(x, ln, lbr, lbi, *bds, dskip)
    gl = _s5_gelu(yy)
    z = _mm(gl, w_glu, "nn", f32, "s5_glu_mm")
    return _s5_out(x, yy, z, gate), (x, lbr, lbi, bds, yy, xr, xi, gl, z)


def _s5_bwd(dxn, saved, ln, raw, dskip, w_glu):
    x, lbr, lbi, bds, yy, xr, xi, gl, z = saved
    g, shift, scale, gate = ln
    dz, dgl_a, dgate = _s5_out_bwd(dxn, yy, z, gate)
    dgl_b = _mm(dz, w_glu, "nt", f32, "s5_dgl")
    dw_glu = _mm(gl, dz, "tn", f32, "s5_dwglu")
    dyy = _s5_gelu_bwd(yy, dgl_a, dgl_b)
    du, dar, dai, dbbr_bd, dbbi_bd, dcr_bd, dci_bd, dd = _s5_scan_bwd(dyy, x, ln, xr, xi, lbr, lbi, *bds, dskip)
    shp = (S5_GROUPS, 1, S5_STATE)
    d_lam_re, d_lam_im, d_dt, d_b_re, d_b_im = _s5_prep_bwd(
        *raw, dar.reshape(shp), dai.reshape(shp), _s5_in_bd_diag(dbbr_bd), _s5_in_bd_diag(dbbi_bd))
    dx, dg, dscale, dshift = _ln_bwd(x, g, scale, shift, du, dxn)
    grads = dict(
        s5_lam_re=d_lam_re.reshape(1, S5_GROUPS, S5_STATE), s5_lam_im=d_lam_im.reshape(1, S5_GROUPS, S5_STATE),
        s5_log_dt=d_dt[:, 0, 0].reshape(1, S5_GROUPS),
        s5_b_re=d_b_re.transpose(0, 2, 1)[None], s5_b_im=d_b_im.transpose(0, 2, 1)[None],
        s5_c_re=_s5_out_bd_diag(dcr_bd)[None], s5_c_im=_s5_out_bd_diag(dci_bd)[None],
        s5_d=dd, s5_w_glu=dw_glu)
    return dx, (dg, dshift, dscale, dgate), grads


_MESH = pl.DeviceIdType.MESH
_ANY = pl.BlockSpec(memory_space=pl.ANY)


def _me():
    return lax.axis_index("x"), lax.axis_index("y"), lax.axis_index("c")


def _dev_index(x, y, c):
    return 4 * x + 2 * y + c


def _all_gather(vs, name):
    n = len(vs)

    def body(*refs):
        v_refs, out_refs = refs[:n], refs[n:2 * n]
        send_sems, recv_sems, local_sems = refs[2 * n:]
        x, y, cc = _me()
        me, sibling = (x, y, cc), (x, y, 1 - cc)
        chips = [(1 - x, y), (x, 1 - y), (1 - x, 1 - y)]
        sends, local = [], []

        def copy(a, k, block, to, src=None):
            rows = out_refs[a].at[_dev_index(*block)]
            return pltpu.make_async_remote_copy(
                src_ref=rows if src is None else src, dst_ref=rows,
                send_sem=send_sems.at[7 * a + k], recv_sem=recv_sems.at[7 * a + k], device_id=to, device_id_type=_MESH)

        for a in range(n):
            mine = pltpu.make_async_copy(v_refs[a], out_refs[a].at[_dev_index(*me)], local_sems.at[a])
            mine.start()
            local.append(mine)
            first = [copy(a, 0, me, sibling, src=v_refs[a])]
            first += [copy(a, 1 + j, me, (*chip, cc), src=v_refs[a]) for j, chip in enumerate(chips)]
            for cp in first:
                cp.start()
            sends += first
        for a in range(n):
            for j, chip in enumerate(chips):
                copy(a, 1 + j, (*chip, cc), me).wait_recv()
                passed = copy(a, 4 + j, (*chip, cc), sibling)
                passed.start()
                sends.append(passed)
        for a in range(n):
            copy(a, 0, sibling, me).wait_recv()
            for j, chip in enumerate(chips):
                copy(a, 4 + j, (*chip, 1 - cc), me).wait_recv()
        for cp in sends:
            cp.wait_send()
        for cp in local:
            cp.wait()

    return pl.pallas_call(
        body, name=name, out_shape=[jax.ShapeDtypeStruct((N_DEV,) + v.shape, v.dtype) for v in vs],
        in_specs=[_ANY] * n, out_specs=[_ANY] * n,
        scratch_shapes=[pltpu.SemaphoreType.DMA((7 * n,)), pltpu.SemaphoreType.DMA((7 * n,)),
                        pltpu.SemaphoreType.DMA((n,))],
    )(*vs)


def _exchange_pair(vs, name):
    n = len(vs)

    def body(*refs):
        v_refs, out_refs = refs[:n], refs[n:2 * n]
        send_sems, recv_sems = refs[2 * n:]
        x, y, cc = _me()
        sibling = (x, y, 1 - cc)
        copies = []
        for a in range(n):
            for k in range(4):
                cp = pltpu.make_async_remote_copy(
                    src_ref=v_refs[a].at[2 * k + (1 - cc)], dst_ref=out_refs[a].at[k],
                    send_sem=send_sems.at[4 * a + k], recv_sem=recv_sems.at[4 * a + k],
                    device_id=sibling, device_id_type=_MESH)
                cp.start()
                copies.append(cp)
        for cp in copies:
            cp.wait_recv()
        for cp in copies:
            cp.wait_send()

    return pl.pallas_call(
        body, name=name, out_shape=[jax.ShapeDtypeStruct((4,) + v.shape[1:], v.dtype) for v in vs],
        in_specs=[_ANY] * n, out_specs=[_ANY] * n,
        scratch_shapes=[pltpu.SemaphoreType.DMA((4 * n,)), pltpu.SemaphoreType.DMA((4 * n,))],
    )(*vs)


def _pair_sum(v, got):
    _, r, c = v.shape
    tr = _pick(r, (512, 352, 256, 128))
    core = lax.axis_index("c").astype(jnp.int32).reshape(1)

    def body(c_ref, v_ref, g_ref, o_ref):
        o_ref[...] = (v_ref[...] + g_ref[...]).astype(bf16)

    return pl.pallas_call(
        body, name="pair_sum",
        grid_spec=pltpu.PrefetchScalarGridSpec(
            num_scalar_prefetch=1, grid=(4, r // tr),
            in_specs=[pl.BlockSpec((1, tr, c), lambda k, i, c_ref: (2 * k + c_ref[0], i, 0)),
                      pl.BlockSpec((1, tr, c), lambda k, i, c_ref: (k, i, 0))],
            out_specs=pl.BlockSpec((1, tr, c), lambda k, i, c_ref: (k, i, 0))),
        out_shape=jax.ShapeDtypeStruct((4, r, c), bf16), compiler_params=_cp(2),
    )(core, v, got)


def _exchange_chips(vs, name):
    n = len(vs)

    def body(*refs):
        v_refs, out_refs = refs[:n], refs[n:2 * n]
        send_sems, recv_sems, local_sems = refs[2 * n:]
        x, y, cc = _me()
        mine = 2 * x + y
        peers = []
        for mask in (1, 2, 3):
            px = 1 - x if mask & 2 else x
            py = 1 - y if mask & 1 else y
            peers.append((mask - 1, (px, py, cc), 2 * px + py))
        local, sends = [], []
        for a in range(n):
            own = pltpu.make_async_copy(v_refs[a].at[mine], out_refs[a].at[mine], local_sems.at[a])
            own.start()
            local.append(own)
            for k, peer, pchip in peers:
                cp = pltpu.make_async_remote_copy(
                    src_ref=v_refs[a].at[pchip], dst_ref=out_refs[a].at[mine],
                    send_sem=send_sems.at[3 * a + k], recv_sem=recv_sems.at[3 * a + k],
                    device_id=peer, device_id_type=_MESH)
                cp.start()
                sends.append(cp)
        for a in range(n):
            for k, peer, pchip in peers:
                pltpu.make_async_remote_copy(
                    src_ref=v_refs[a].at[pchip], dst_ref=out_refs[a].at[pchip],
                    send_sem=send_sems.at[3 * a + k], recv_sem=recv_sems.at[3 * a + k],
                    device_id=peer, device_id_type=_MESH).wait_recv()
        for cp in sends:
            cp.wait_send()
        for cp in local:
            cp.wait()

    return pl.pallas_call(
        body, name=name, out_shape=[jax.ShapeDtypeStruct(v.shape, v.dtype) for v in vs],
        in_specs=[_ANY] * n, out_specs=[_ANY] * n,
        scratch_shapes=[pltpu.SemaphoreType.DMA((3 * n,)), pltpu.SemaphoreType.DMA((3 * n,)),
                        pltpu.SemaphoreType.DMA((n,))],
    )(*vs)


def _ada_mod(c_all, ada_w):
    cols = ada_w.shape[2]

    def body(c_ref, w_ref, o_ref):
        cond = jax.nn.silu(c_ref[...]).astype(bf16)
        o_ref[0] = jnp.dot(cond, w_ref[0].astype(bf16), preferred_element_type=f32)

    return pl.pallas_call(
        body, name="ada_mod", grid=(DEPTH,),
        in_specs=[pl.BlockSpec((16, D), lambda i: (0, 0)), pl.BlockSpec((1, D, cols), lambda i: (i, 0, 0))],
        out_specs=pl.BlockSpec((1, 16, cols), lambda i: (i, 0, 0)),
        out_shape=jax.ShapeDtypeStruct((DEPTH, 16, cols), f32), compiler_params=_cp(1),
    )(c_all, ada_w)


def _ada_grad(c_all, dmod):
    cols = dmod.shape[2]

    def body(c_ref, d_ref, o_ref):
        cond = jax.nn.silu(c_ref[...]).astype(bf16)
        o_ref[0] = lax.dot_general(cond, d_ref[0].astype(bf16), _DN["tn"], preferred_element_type=f32)

    return pl.pallas_call(
        body, name="ada_grad", grid=(DEPTH,),
        in_specs=[pl.BlockSpec((16, D), lambda i: (0, 0)), pl.BlockSpec((1, 16, cols), lambda i: (i, 0, 0))],
        out_specs=pl.BlockSpec((1, D, cols), lambda i: (i, 0, 0)),
        out_shape=jax.ShapeDtypeStruct((DEPTH, D, cols), f32), compiler_params=_cp(1),
    )(c_all, dmod)


def _row_tile(r):
    return _pick(r, (512, 352, 256, 128)) if r > 512 else r


def _sum_sources(v, name):
    n, r, c = v.shape
    tr = _row_tile(r)

    def body(v_ref, o_ref):
        acc = v_ref[0]
        for p in range(1, n):
            acc = acc + v_ref[p]
        o_ref[...] = acc.astype(f32)

    return pl.pallas_call(
        body, name=name, grid=(r // tr,),
        in_specs=[pl.BlockSpec((n, tr, c), lambda i: (0, i, 0))], out_specs=pl.BlockSpec((tr, c), lambda i: (i, 0)),
        out_shape=jax.ShapeDtypeStruct((r, c), f32), compiler_params=_cp(1),
    )(v)


def _adamw(parts, w, m, v, name):
    n, r, c = parts.shape
    tr = _row_tile(r)
    c1 = 1.0 - ADAM_B1 ** ADAM_STEP
    c2 = 1.0 - ADAM_B2 ** ADAM_STEP

    def body(p_ref, w_ref, m_ref, v_ref, g_ref, d_ref, mo_ref, vo_ref):
        g_v = p_ref[0].astype(f32)
        for p in range(1, n):
            g_v = g_v + p_ref[p].astype(f32)
        m_n = ADAM_B1 * m_ref[...] + (1.0 - ADAM_B1) * g_v
        v_n = ADAM_B2 * v_ref[...] + (1.0 - ADAM_B2) * (g_v * g_v)
        g_ref[...] = g_v
        d_ref[...] = -ADAM_LR * ((m_n / c1) / (jnp.sqrt(v_n / c2) + ADAM_EPS) + ADAM_WD * w_ref[...])
        mo_ref[...] = m_n
        vo_ref[...] = v_n

    spec = pl.BlockSpec((tr, c), lambda i: (i, 0))
    shp = jax.ShapeDtypeStruct((r, c), f32)
    return pl.pallas_call(
        body, name=name, grid=(r // tr,), in_specs=[pl.BlockSpec((n, tr, c), lambda i: (0, i, 0))] + [spec] * 3,
        out_specs=[spec] * 4, out_shape=[shp] * 4, compiler_params=_cp(1),
    )(parts, w, m, v)


def _two_d(shape):
    return (math.prod(shape[:-1]), shape[-1])


def _pack_rows(a):
    n = a.size
    rows = -(-n // (8 * PACK_C)) * 8
    return jnp.pad(a.reshape(-1), (0, rows * PACK_C - n)).reshape(rows, PACK_C)


def _pack(parts):
    return jnp.concatenate([_pack_rows(p) for p in parts], axis=0)


def _unpack(packed, shapes):
    lead = packed.shape[:-2]
    out, off = [], 0
    for s in shapes:
        n = math.prod(s)
        rows = -(-n // (8 * PACK_C)) * 8
        part = packed[..., off:off + rows, :].reshape(lead + (rows * PACK_C,))
        out.append(part[..., :n].reshape(lead + tuple(s)))
        off += rows
    return out


def _unshard(g8, axis):
    local = g8.shape[1:]
    moved = jnp.moveaxis(g8, 0, axis)
    return moved.reshape(local[:axis] + (N_DEV * local[axis],) + local[axis + 1:])


def _shard8(full, axis):
    s = full.shape
    split = full.reshape(s[:axis] + (N_DEV, s[axis] // N_DEV) + s[axis + 1:])
    return jnp.moveaxis(split, axis, 0)


_BIG = dict(ffn_w_in=3, ffn_w_out=2, pool_w=2, fox_w_in=2, fox_w_o=1, s5_w_glu=1, conv_w_in=2, conv_w_out=1)
_SMALL_SHARDED = dict(norm_g=2, s5_d=1, conv_w=3)
_REPLICATED = ("ada_b", "pool_scale", "fox_b_f", "fox_q_gain", "fox_k_gain", "s5_lam_re", "s5_lam_im", "s5_log_dt",
               "s5_b_re", "s5_b_im", "s5_c_re", "s5_c_im")
_WEIGHTS = ("ada_w", "ada_b", "norm_g", "ffn_w_in", "ffn_w_out", "pool_w", "pool_scale", "fox_w_in", "fox_b_f",
            "fox_q_gain", "fox_k_gain", "fox_w_o", "s5_lam_re", "s5_lam_im", "s5_log_dt", "s5_b_re", "s5_b_im",
            "s5_c_re", "s5_c_im", "s5_d", "s5_w_glu", "conv_w_in", "conv_w", "conv_w_out")


def _step(x, c, target, w, m, v):
    t = x.shape[1]
    xi_, yi_, ci_ = _me()
    me = _dev_index(xi_, yi_, ci_)

    sm_shapes = [w[n].shape for n in _SMALL_SHARDED]
    small_all = _all_gather([_pack([c] + [w[n] for n in _SMALL_SHARDED])], "gather_small")[0]
    gathered = _unpack(small_all, [c.shape] + sm_shapes)
    c_all = gathered[0][:, 0, :]
    full = {n: _unshard(p, ax) for (n, ax), p in zip(_SMALL_SHARDED.items(), gathered[1:])}

    big_all = _all_gather([w[n].astype(bf16).reshape(_two_d(w[n].shape)) for n in _BIG], "gather_weights")
    gw = dict(zip(_BIG, big_all))
    ffn_w_in = gw["ffn_w_in"].reshape(N_DEV, 2 * DEPTH, D, FFN_HS)
    ffn_w_out = gw["ffn_w_out"].reshape(N_DEV, 2 * DEPTH, D_FF // N_DEV, D)
    pool_w = gw["pool_w"].reshape(N_DEV, 4, POOL_GROUP // N_DEV, POOL_GROUP).transpose(1, 0, 2, 3)
    pool_w = pool_w.reshape(4, POOL_GROUP, POOL_GROUP)
    fox_w_in = jnp.pad(gw["fox_w_in"].transpose(1, 0, 2).reshape(D, FOX_PROJ), ((0, 0), (0, FOX_PROJ_PAD - FOX_PROJ)))
    fox_w_o, s5_w_glu, conv_w_out = (gw[n].reshape(D, D) for n in ("fox_w_o", "s5_w_glu", "conv_w_out"))
    conv_w_in = gw["conv_w_in"]

    def ffn_w(i, f):
        idx = 2 * i + f
        return ffn_w_in[:, idx], ffn_w_out[:, idx].reshape(D_FF, D)

    c16 = jnp.pad(c_all, ((0, 8), (0, 0)))
    cols = w["ada_w"].shape[2]
    mod_sh = _ada_mod(c16, w["ada_w"])
    mod_all = _all_gather([mod_sh.reshape(DEPTH * 16, cols)], "gather_mod")[0].reshape(N_DEV, DEPTH, 16, cols)
    mod_mine = lax.dynamic_index_in_dim(mod_all, me, axis=2, keepdims=False)
    mod = (mod_mine.transpose(1, 0, 2).reshape(DEPTH, N_DEV * cols) + w["ada_b"]).reshape(DEPTH, 3, 3, D)

    norm_g = full["norm_g"]

    def ln_of(i, sub):
        return (norm_g[i, sub][None], mod[i, sub, 0][None], mod[i, sub, 1][None], mod[i, sub, 2][None])

    fox_b_f = jnp.pad(w["fox_b_f"], ((0, 0), (0, 128 - FOX_HEADS)))
    s5_raw = _s5_params(w["s5_lam_re"][0], w["s5_lam_im"][0], w["s5_log_dt"][0], w["s5_b_re"][0], w["s5_b_im"][0])
    s5_c_re, s5_c_im = w["s5_c_re"][0], w["s5_c_im"][0]
    conv_w = jnp.pad(full["conv_w"][0, :, 0, :], ((0, 5), (0, 0)))

    xs = x[0]
    saved = []
    for i in range(DEPTH):
        xs, s0 = _ffn_fwd(xs, ln_of(i, 0), *ffn_w(i, 0), 0.5)
        if i == 0:
            s1 = xs
            xs = _pool_fwd(xs, ln_of(i, 1), pool_w, w["pool_scale"])
        elif i == 1:
            xs, s1 = _fox_fwd(xs, ln_of(i, 1), fox_w_in, fox_b_f, w["fox_q_gain"], w["fox_k_gain"], fox_w_o)
        elif i == 2:
            xs, s1 = _s5_fwd(xs, ln_of(i, 1), s5_raw, s5_c_re, s5_c_im, full["s5_d"], s5_w_glu)
        else:
            xs, s1 = _convmix_fwd(xs, ln_of(i, 1), conv_w_in, conv_w, conv_w_out)
        xs, s2 = _ffn_fwd(xs, ln_of(i, 2), *ffn_w(i, 1), 0.5)
        saved.append((s0, s1, s2))
    dx, lpart = _loss_head(xs, target[0])
    loss = lax.psum(lpart[0, 0], AXES)

    grads = {}
    dmod = [[None] * 3 for _ in range(DEPTH)]
    dnorm = [[None] * 3 for _ in range(DEPTH)]
    dffn_in = [[None] * 2 for _ in range(DEPTH)]
    dffn_out = [[None] * 2 for _ in range(DEPTH)]

    def put_ln(i, sub, dln):
        dg, dshift, dscale, dgate = dln
        dnorm[i][sub] = dg
        dmod[i][sub] = jnp.concatenate([dshift, dscale, dgate], axis=0)

    for i in reversed(range(DEPTH)):
        s0, s1, s2 = saved[i]
        dx, dln, dffn_in[i][1], dffn_out[i][1] = _ffn_bwd(dx, s2, ln_of(i, 2), *ffn_w(i, 1), 0.5)
        put_ln(i, 2, dln)
        if i == 0:
            dx, dln, dpw, dps = _pool_bwd(dx, s1, ln_of(i, 1), pool_w, w["pool_scale"])
            dpw = dpw.reshape(4, N_DEV, POOL_GROUP // N_DEV, POOL_GROUP).transpose(1, 0, 2, 3)
            grads.update(pool_w=dpw.reshape(N_DEV, 4 * POOL_GROUP // N_DEV, POOL_GROUP), pool_scale=dps)
        elif i == 1:
            dx, dln, dwi, dbf, dqg, dkg, dwo = _fox_bwd(
                dx, s1, ln_of(i, 1), fox_w_in, fox_b_f, w["fox_q_gain"], w["fox_k_gain"], fox_w_o)
            dwi = dwi[:, :FOX_PROJ].reshape(D, N_DEV, FOX_PROJ // N_DEV).transpose(1, 0, 2)
            grads.update(fox_w_in=dwi, fox_b_f=dbf[:, :FOX_HEADS], fox_q_gain=dqg, fox_k_gain=dkg,
                         fox_w_o=dwo.reshape(N_DEV, D // N_DEV, D))
        elif i == 2:
            dx, dln, gs5 = _s5_bwd(dx, s1, ln_of(i, 1), s5_raw, full["s5_d"], s5_w_glu)
            gs5["s5_w_glu"] = gs5["s5_w_glu"].reshape(N_DEV, D // N_DEV, D)
            grads.update(gs5)
        else:
            dx, dln, dwi, dcw, dwo = _convmix_bwd(dx, s1, ln_of(i, 1), conv_w_in, conv_w, conv_w_out)
            grads.update(conv_w_in=dwi, conv_w=dcw[None, :, None, :], conv_w_out=dwo.reshape(N_DEV, D // N_DEV, D))
        put_ln(i, 1, dln)
        dx, dln, dffn_in[i][0], dffn_out[i][0] = _ffn_bwd(dx, s0, ln_of(i, 0), *ffn_w(i, 0), 0.5)
        put_ln(i, 0, dln)
    grads["ffn_w_in"] = jnp.stack([p for r in dffn_in for p in r], axis=1).reshape(N_DEV, 2 * DEPTH * D, FFN_HS)
    grads["ffn_w_out"] = jnp.stack([p.reshape(N_DEV, D_FF // N_DEV, D) for r in dffn_out for p in r], axis=1)
    grads["ffn_w_out"] = grads["ffn_w_out"].reshape(N_DEV, 2 * DEPTH * D_FF // N_DEV, D)
    grads["norm_g"] = jnp.stack([jnp.concatenate(r, axis=0) for r in dnorm])
    dmod_mine = jnp.stack([jnp.stack(r) for r in dmod]).reshape(DEPTH, 9 * D)

    small_names = list(_REPLICATED[1:]) + list(_SMALL_SHARDED)
    small_parts = [dmod_mine] + [grads[n] for n in small_names]
    small_g = _all_gather([_pack(small_parts)], "gather_grads")[0]
    small_sum = _sum_sources(small_g, "sum_small")
    summed = dict(zip(["ada_b"] + small_names, _unpack(small_sum, [p.shape for p in small_parts])))
    for n, ax in _SMALL_SHARDED.items():
        local = w[n].shape[ax]
        summed[n] = lax.dynamic_slice_in_dim(summed[n], me * local, local, axis=ax)

    dmod_all = small_g[:, :DEPTH * 9].reshape(N_DEV, DEPTH, 9 * D)
    dmod_cols = lax.dynamic_slice_in_dim(dmod_all, me * cols, cols, axis=2)
    ada_g = _ada_grad(c16, jnp.pad(dmod_cols.transpose(1, 0, 2), ((0, 0), (0, 8), (0, 0))))

    big_parts = [grads[n] for n in _BIG]
    from_sibling = _exchange_pair(big_parts, "exchange_pair")
    chip_sums = [_pair_sum(p, s) for p, s in zip(big_parts, from_sibling)]
    big_landed = dict(zip(_BIG, _exchange_chips(chip_sums, "exchange_chips")))

    grad, delta, new_m, new_v = {}, {}, {}, {}
    big_landed["ada_w"] = ada_g[None]
    for n, parts in big_landed.items():
        view = _two_d(w[n].shape)
        outs = _adamw(parts.reshape((parts.shape[0],) + view), w[n].reshape(view), m[n].reshape(view),
                      v[n].reshape(view), "adamw_" + n)
        grad[n], delta[n], new_m[n], new_v[n] = (a.reshape(w[n].shape) for a in outs)
    small = [n for n in _WEIGHTS if n not in big_landed]
    small_shapes = [w[n].shape for n in small]
    pk = lambda d: _pack([d[n] for n in small])
    outs = _adamw(pk(summed)[None], pk(w), pk(m), pk(v), "adamw_small")
    for dst, packed in zip((grad, delta, new_m, new_v), outs):
        dst.update(zip(small, _unpack(packed, small_shapes)))
    return (loss, dx[None], *[grad[n] for n in _WEIGHTS], *[delta[n] for n in _WEIGHTS],
            *[new_m[n] for n in _WEIGHTS], *[new_v[n] for n in _WEIGHTS])


def kernel(x, c, ada_w, ada_b, norm_g, ffn_w_in, ffn_w_out, pool_w, pool_scale, fox_w_in, fox_b_f, fox_q_gain, fox_k_gain, fox_w_o, s5_lam_re, s5_lam_im, s5_log_dt, s5_b_re, s5_b_im, s5_c_re, s5_c_im, s5_d, s5_w_glu, conv_w_in, conv_w, conv_w_out, loss_target, m_ada_w, m_ada_b, m_norm_g, m_ffn_w_in, m_ffn_w_out, m_pool_w, m_pool_scale, m_fox_w_in, m_fox_b_f, m_fox_q_gain, m_fox_k_gain, m_fox_w_o, m_s5_lam_re, m_s5_lam_im, m_s5_log_dt, m_s5_b_re, m_s5_b_im, m_s5_c_re, m_s5_c_im, m_s5_d, m_s5_w_glu, m_conv_w_in, m_conv_w, m_conv_w_out, v_ada_w, v_ada_b, v_norm_g, v_ffn_w_in, v_ffn_w_out, v_pool_w, v_pool_scale, v_fox_w_in, v_fox_b_f, v_fox_q_gain, v_fox_k_gain, v_fox_w_o, v_s5_lam_re, v_s5_lam_im, v_s5_log_dt, v_s5_b_re, v_s5_b_im, v_s5_c_re, v_s5_c_im, v_s5_d, v_s5_w_glu, v_conv_w_in, v_conv_w, v_conv_w_out):
    ws = (ada_w, ada_b, norm_g, ffn_w_in, ffn_w_out, pool_w, pool_scale, fox_w_in, fox_b_f, fox_q_gain, fox_k_gain,
          fox_w_o, s5_lam_re, s5_lam_im, s5_log_dt, s5_b_re, s5_b_im, s5_c_re, s5_c_im, s5_d, s5_w_glu, conv_w_in,
          conv_w, conv_w_out)
    ms = (m_ada_w, m_ada_b, m_norm_g, m_ffn_w_in, m_ffn_w_out, m_pool_w, m_pool_scale, m_fox_w_in, m_fox_b_f,
          m_fox_q_gain, m_fox_k_gain, m_fox_w_o, m_s5_lam_re, m_s5_lam_im, m_s5_log_dt, m_s5_b_re, m_s5_b_im,
          m_s5_c_re, m_s5_c_im, m_s5_d, m_s5_w_glu, m_conv_w_in, m_conv_w, m_conv_w_out)
    vs = (v_ada_w, v_ada_b, v_norm_g, v_ffn_w_in, v_ffn_w_out, v_pool_w, v_pool_scale, v_fox_w_in, v_fox_b_f,
          v_fox_q_gain, v_fox_k_gain, v_fox_w_o, v_s5_lam_re, v_s5_lam_im, v_s5_log_dt, v_s5_b_re, v_s5_b_im,
          v_s5_c_re, v_s5_c_im, v_s5_d, v_s5_w_glu, v_conv_w_in, v_conv_w, v_conv_w_out)
    return _step(x, c, loss_target, dict(zip(_WEIGHTS, ws)), dict(zip(_WEIGHTS, ms)), dict(zip(_WEIGHTS, vs)))
```

```python
import math

import jax
import jax.numpy as jnp
from jax import lax
from jax.experimental import pallas as pl
from jax.experimental.pallas import tpu as pltpu

f32 = jnp.float32
bf16 = jnp.bfloat16

D = 1024
D_FF = 2816
FFN_HS = 2 * D_FF // 8
FFN_SLABS = 4
DEPTH = 4
NORM_EPS = 1e-6
N_DEV = 8
AXES = ("x", "y", "c")
POOL_WINDOWS = (2, 4, 8, 16)
POOL_GROUP = 256
POOL_HALO = 16
FOX_HEADS = 16
FOX_HEAD_DIM = 64
FOX_PROJ = 3088
FOX_PROJ_PAD = 3200
S5_GROUPS = 64
S5_GROUP = 16
S5_STATE = 64
S5_NSTATE = S5_GROUPS * S5_STATE
S5_BLOCKS = 8
S5_BCH = 128
S5_BST = 512
CONV_HALO = 8
ADAM_LR = 0.001
ADAM_B1 = 0.9
ADAM_B2 = 0.999
ADAM_EPS = 1e-08
ADAM_WD = 0.01
ADAM_STEP = 10
VMEM_LIMIT = 56 * 1024 * 1024
PACK_C = 1024

_ARB = "arbitrary"


def _cp(n_axes):
    return pltpu.CompilerParams(dimension_semantics=(_ARB,) * n_axes, vmem_limit_bytes=VMEM_LIMIT)


def _pick(n, prefs):
    for c in prefs:
        if n % c == 0:
            return c
    return n


_DN = {"nn": (((1,), (0,)), ((), ())), "nt": (((1,), (1,)), ((), ())), "tn": (((0,), (0,)), ((), ()))}


def _mm(a, b, mode, out_dtype, name):
    if mode == "nn":
        (m, k), (_, n) = a.shape, b.shape
    elif mode == "nt":
        (m, k), (n, _) = a.shape, b.shape
    else:
        (k, m), (_, n) = a.shape, b.shape
    big = (1408, 1024, 640, 512, 384, 256, 128)
    tm = _pick(m, big) if mode == "tn" else _pick(m, (1024, 512, 256, 128))
    tn = _pick(n, big)
    if mode == "tn":
        tk = _pick(k, (512, 256, 128))
    else:
        tk = k if k <= 3200 else _pick(k, (2816, 2048, 1024, 512))
    nk = k // tk
    dn = _DN[mode]

    def body(a_ref, b_ref, o_ref, acc_ref):
        p = lax.dot_general(a_ref[...], b_ref[...], dn, preferred_element_type=f32)
        if nk == 1:
            o_ref[...] = p.astype(out_dtype)
        else:
            kk = pl.program_id(2)

            @pl.when(kk == 0)
            def _():
                acc_ref[...] = p

            @pl.when(kk > 0)
            def _():
                acc_ref[...] += p

            @pl.when(kk == nk - 1)
            def _():
                o_ref[...] = acc_ref[...].astype(out_dtype)

    if mode == "nn":
        a_spec = pl.BlockSpec((tm, tk), lambda i, j, kk: (i, kk))
        b_spec = pl.BlockSpec((tk, tn), lambda i, j, kk: (kk, j))
    elif mode == "nt":
        a_spec = pl.BlockSpec((tm, tk), lambda i, j, kk: (i, kk))
        b_spec = pl.BlockSpec((tn, tk), lambda i, j, kk: (j, kk))
    else:
        a_spec = pl.BlockSpec((tk, tm), lambda i, j, kk: (kk, i))
        b_spec = pl.BlockSpec((tk, tn), lambda i, j, kk: (kk, j))
    acc_shape = (tm, tn) if nk > 1 else (8, 128)
    return pl.pallas_call(
        body, name=name, grid=(m // tm, n // tn, nk),
        in_specs=[a_spec, b_spec], out_specs=pl.BlockSpec((tm, tn), lambda i, j, kk: (i, j)),
        out_shape=jax.ShapeDtypeStruct((m, n), out_dtype),
        scratch_shapes=[pltpu.VMEM(acc_shape, f32)],
        compiler_params=_cp(3),
    )(a, b)


def _mmx(a, b, mode, name, grid, a_spec, b_spec, o_spec, out_shape, into=None):
    nk = grid[2]
    dn = _DN[mode]
    out_dtype = out_shape.dtype
    a_blk = (math.prod(a_spec.block_shape[:-1]), a_spec.block_shape[-1])
    b_blk = (math.prod(b_spec.block_shape[:-1]), b_spec.block_shape[-1])
    o_blk = (math.prod(o_spec.block_shape[:-1]), o_spec.block_shape[-1])

    def body(a_ref, b_ref, *rest):
        o_ref, acc_ref = rest[-2:]
        p = lax.dot_general(a_ref[...].reshape(a_blk), b_ref[...].reshape(b_blk), dn, preferred_element_type=f32)
        if nk == 1:
            o_ref[...] = p.reshape(o_ref.shape).astype(out_dtype)
        else:
            kk = pl.program_id(2)

            @pl.when(kk == 0)
            def _():
                acc_ref[...] = p

            @pl.when(kk > 0)
            def _():
                acc_ref[...] += p

            @pl.when(kk == nk - 1)
            def _():
                o_ref[...] = acc_ref[...].reshape(o_ref.shape).astype(out_dtype)

    extra = {} if into is None else dict(input_output_aliases={2: 0})
    operands = (a, b) if into is None else (a, b, into)
    return pl.pallas_call(
        body, name=name, grid=grid, in_specs=[a_spec, b_spec] + ([] if into is None else [_ANY]),
        out_specs=o_spec, out_shape=out_shape,
        scratch_shapes=[pltpu.VMEM(o_blk if nk > 1 else (8, 128), f32)], compiler_params=_cp(3), **extra,
    )(*operands)


def _tok_tile(t):
    return min(t, 512)


def _tok_spec(tm, c, nt=None, reverse=False):
    if reverse:
        return pl.BlockSpec((tm, c), lambda i: (nt - 1 - i, 0))
    return pl.BlockSpec((tm, c), lambda i: (i, 0))


def _row_spec(c, rows=1):
    return pl.BlockSpec((rows, c), lambda i: (0, 0))


def _acc_add(first, ref, val):
    @pl.when(first)
    def _():
        ref[...] = val

    @pl.when(jnp.logical_not(first))
    def _():
        ref[...] += val


def _adaln(x, g, scale, shift):
    y = x * lax.rsqrt(jnp.mean(x * x, axis=-1, keepdims=True) + NORM_EPS)
    return (y * g) * (1.0 + scale) + shift


def _ln_fwd(x, g, scale, shift):
    t = x.shape[0]
    tm = _tok_tile(t)

    def body(x_ref, g_ref, sc_ref, sh_ref, h_ref):
        h_ref[...] = _adaln(x_ref[...], g_ref[...], sc_ref[...], sh_ref[...]).astype(bf16)

    return pl.pallas_call(
        body, name="ln_fwd", grid=(t // tm,),
        in_specs=[_tok_spec(tm, D), _row_spec(D), _row_spec(D), _row_spec(D)],
        out_specs=_tok_spec(tm, D), out_shape=jax.ShapeDtypeStruct((t, D), bf16),
        compiler_params=_cp(1),
    )(x, g, scale, shift)


def _ln_bwd(x, g, scale, shift, dh, dxn):
    t = x.shape[0]
    tm = _tok_tile(t)

    def body(x_ref, g_ref, sc_ref, sh_ref, dh_ref, dxn_ref, dx_ref, dg_ref, dsc_ref, dsh_ref):
        _, vjp = jax.vjp(_adaln, x_ref[...], g_ref[...], sc_ref[...], sh_ref[...])
        dx, dg, dsc, dsh = vjp(dh_ref[...])
        dx_ref[...] = dxn_ref[...] + dx
        first = pl.program_id(0) == 0
        _acc_add(first, dg_ref, dg)
        _acc_add(first, dsc_ref, dsc)
        _acc_add(first, dsh_ref, dsh)

    row = jax.ShapeDtypeStruct((1, D), f32)
    return pl.pallas_call(
        body, name="ln_bwd", grid=(t // tm,),
        in_specs=[_tok_spec(tm, D), _row_spec(D), _row_spec(D), _row_spec(D), _tok_spec(tm, D), _tok_spec(tm, D)],
        out_specs=[_tok_spec(tm, D), _row_spec(D), _row_spec(D), _row_spec(D)],
        out_shape=[jax.ShapeDtypeStruct((t, D), f32), row, row, row],
        compiler_params=_cp(1),
    )(x, g, scale, shift, dh, dxn)


def _swiglu(g, u):
    return jax.nn.silu(g) * u


def _ffn_in(h, w_in, idx):
    t = h.shape[0]
    tm = _mm_tile(t)

    def body(h_ref, w_ref, gu_ref, a_ref):
        hv = h_ref[...]
        g = jnp.dot(hv, w_ref[0, 0, 0], preferred_element_type=f32)
        u = jnp.dot(hv, w_ref[1, 0, 0], preferred_element_type=f32)
        gu_ref[0, 0] = g.astype(bf16)
        gu_ref[1, 0] = u.astype(bf16)
        a_ref[0] = _swiglu(g, u).astype(bf16)

    return pl.pallas_call(
        body, name="ffn_in", grid=(t // tm, FFN_SLABS),
        in_specs=[pl.BlockSpec((tm, D), lambda i, q: (i, 0)),
                  pl.BlockSpec((2, 1, 1, D, FFN_HS), lambda i, q: (0, q, idx, 0, 0))],
        out_specs=[pl.BlockSpec((2, 1, tm, FFN_HS), lambda i, q: (0, q, i, 0)),
                   pl.BlockSpec((1, tm, FFN_HS), lambda i, q: (q, i, 0))],
        out_shape=[jax.ShapeDtypeStruct((2, FFN_SLABS, t, FFN_HS), bf16),
                   jax.ShapeDtypeStruct((FFN_SLABS, t, FFN_HS), bf16)],
        compiler_params=_cp(2),
    )(h, w_in)


def _ffn_dgu(do, w_out, idx, gu):
    t = do.shape[0]
    tm = _mm_tile(t)
    rows = w_out.shape[2]

    def body(do_ref, w_ref, gu_ref, o_ref):
        da = lax.dot_general(do_ref[...], w_ref[...].reshape(FFN_HS, D), _DN["nt"], preferred_element_type=f32)
        _, vjp = jax.vjp(_swiglu, gu_ref[0, 0].astype(f32), gu_ref[1, 0].astype(f32))
        dg, du = vjp(da)
        o_ref[0, 0] = dg.astype(bf16)
        o_ref[1, 0] = du.astype(bf16)

    slab = pl.BlockSpec((2, 1, tm, FFN_HS), lambda i, q: (0, q, i, 0))
    return pl.pallas_call(
        body, name="ffn_dgu", grid=(t // tm, FFN_SLABS),
        in_specs=[pl.BlockSpec((tm, D), lambda i, q: (i, 0)),
                  pl.BlockSpec((2, 1, rows, D), lambda i, q: (q, idx, 0, 0)), slab],
        out_specs=slab, out_shape=jax.ShapeDtypeStruct((2, FFN_SLABS, t, FFN_HS), bf16),
        compiler_params=_cp(2),
    )(do, w_out, gu)


def _resid_fwd(x, o, gate, coef):
    t = x.shape[0]
    tm = _tok_tile(t)

    def body(x_ref, o_ref, gt_ref, y_ref):
        y_ref[...] = x_ref[...] + (coef * (1.0 + gt_ref[...])) * o_ref[...]

    return pl.pallas_call(
        body, name="resid_fwd", grid=(t // tm,),
        in_specs=[_tok_spec(tm, D), _tok_spec(tm, D), _row_spec(D)],
        out_specs=_tok_spec(tm, D), out_shape=jax.ShapeDtypeStruct((t, D), f32),
        compiler_params=_cp(1),
    )(x, o, gate)


def _resid_bwd(dxn, o, gate, coef):
    t = dxn.shape[0]
    tm = _tok_tile(t)

    def body(dxn_ref, o_ref, gt_ref, do_ref, dgt_ref):
        dxn_v = dxn_ref[...]
        do_ref[...] = ((coef * (1.0 + gt_ref[...])) * dxn_v).astype(bf16)
        _acc_add(pl.program_id(0) == 0, dgt_ref, coef * jnp.sum(dxn_v * o_ref[...], axis=0, keepdims=True))

    return pl.pallas_call(
        body, name="resid_bwd", grid=(t // tm,),
        in_specs=[_tok_spec(tm, D), _tok_spec(tm, D), _row_spec(D)],
        out_specs=[_tok_spec(tm, D), _row_spec(D)],
        out_shape=[jax.ShapeDtypeStruct((t, D), bf16), jax.ShapeDtypeStruct((1, D), f32)],
        compiler_params=_cp(1),
    )(dxn, o, gate)


def _loss_head(y, target):
    t = y.shape[0]
    tm = _tok_tile(t)

    def body(y_ref, t_ref, dy_ref, l_ref):
        err = y_ref[...] - t_ref[...]
        dy_ref[...] = err * (1.0 / D)
        part = jnp.sum(jnp.sum(err * err, axis=0, keepdims=True), axis=1, keepdims=True) * (0.5 / D)
        _acc_add(pl.program_id(0) == 0, l_ref, jnp.broadcast_to(part, (1, 128)))

    return pl.pallas_call(
        body, name="loss_head", grid=(t // tm,),
        in_specs=[_tok_spec(tm, D), _tok_spec(tm, D)],
        out_specs=[_tok_spec(tm, D), _row_spec(128)],
        out_shape=[jax.ShapeDtypeStruct((t, D), f32), jax.ShapeDtypeStruct((1, 128), f32)],
        compiler_params=_cp(1),
    )(y, target)


def _mm_tile(t):
    return min(t, 1024)


def _ffn_fwd(x, ln, w_in, w_out, idx, coef):
    g, shift, scale, gate = ln
    t = x.shape[0]
    tm = _mm_tile(t)
    rows = w_out.shape[2]
    h = _ln_fwd(x, g, scale, shift)
    gu, a = _ffn_in(h, w_in.reshape((2, FFN_SLABS) + w_in.shape[1:]), idx)
    o = _mmx(a, w_out, "nn", "ffn_out", (t // tm, 1, FFN_SLABS),
             pl.BlockSpec((1, tm, FFN_HS), lambda i, j, k: (k, i, 0)),
             pl.BlockSpec((2, 1, rows, D), lambda i, j, k: (k, idx, 0, 0)),
             pl.BlockSpec((tm, D), lambda i, j, k: (i, 0)), jax.ShapeDtypeStruct((t, D), f32))
    xn = _resid_fwd(x, o, gate, coef)
    return xn, (x, h, gu, a, o)


def _ffn_bwd(dxn, saved, ln, w_in, w_out, idx, coef, dw_in, dw_out):
    x, h, gu, a, o = saved
    g, shift, scale, gate = ln
    t = x.shape[0]
    tm = _mm_tile(t)
    tk = _pick(t, (2048, 1024, 512, 256, 128))
    rows = w_out.shape[2]
    sds = jax.ShapeDtypeStruct
    do, dgate = _resid_bwd(dxn, o, gate, coef)
    dgu = _ffn_dgu(do, w_out, idx, gu).reshape(N_DEV, t, FFN_HS)
    dw_out = _mmx(a, do, "tn", "ffn_dwout", (FFN_SLABS, 1, t // tk),
                  pl.BlockSpec((1, tk, FFN_HS), lambda i, j, k: (i, k, 0)),
                  pl.BlockSpec((tk, D), lambda i, j, k: (k, 0)),
                  pl.BlockSpec((2, 1, rows, D), lambda i, j, k: (i, idx, 0, 0)),
                  sds(dw_out.shape, f32), into=dw_out)
    dh = _mmx(dgu, w_in, "nt", "ffn_dh", (t // tm, 1, N_DEV),
              pl.BlockSpec((1, tm, FFN_HS), lambda i, j, k: (k, i, 0)),
              pl.BlockSpec((1, 1, D, FFN_HS), lambda i, j, k: (k, idx, 0, 0)),
              pl.BlockSpec((tm, D), lambda i, j, k: (i, 0)), sds((t, D), f32))
    dw_in = _mmx(h, dgu, "tn", "ffn_dwin", (N_DEV, 1, t // tk),
                 pl.BlockSpec((tk, D), lambda i, j, k: (k, 0)),
                 pl.BlockSpec((1, tk, FFN_HS), lambda i, j, k: (i, k, 0)),
                 pl.BlockSpec((1, 1, D, FFN_HS), lambda i, j, k: (i, idx, 0, 0)),
                 sds(dw_in.shape, f32), into=dw_in)
    dx, dg, dscale, dshift = _ln_bwd(x, g, scale, shift, dh, dxn)
    return dx, (dg, dshift, dscale, dgate), dw_in, dw_out


def _roll_rows(a, k):
    n = a.shape[0]
    return pltpu.roll(a, k % n, 0)


def _pool_windows(hx, first_row, reverse):
    outs = []
    for gi, w in enumerate(POOL_WINDOWS):
        acc = hx[:, gi * POOL_GROUP:(gi + 1) * POOL_GROUP]
        k = 1
        while k < w:
            acc = acc + _roll_rows(acc, -k if reverse else k)
            k *= 2
        outs.append(acc)
    return outs


def _pool_cnt(t_idx, w):
    return jnp.minimum(t_idx + 1, w).astype(f32)


def _pool_pooled(x_ref, xp_ref, g, scale, shift, i, tm):
    h = _adaln(x_ref[...], g, scale, shift)
    hp = _adaln(xp_ref[...], g, scale, shift)
    hp = jnp.where(i == 0, 0.0, hp)
    hx = jnp.concatenate([hp, h], axis=0)
    sums = _pool_windows(hx, 0, False)
    t_idx = i * tm + lax.broadcasted_iota(jnp.int32, (tm, 1), 0)
    pooled = []
    for gi, w in enumerate(POOL_WINDOWS):
        s = sums[gi][POOL_HALO:, :]
        pooled.append(s / _pool_cnt(t_idx, w) - h[:, gi * POOL_GROUP:(gi + 1) * POOL_GROUP])
    return h, pooled


def _pool_specs(t, tm):
    per = tm // POOL_HALO
    prev = pl.BlockSpec((POOL_HALO, D), lambda i: (jnp.maximum(i * per - 1, 0), 0))
    return [_tok_spec(tm, D), prev, _row_spec(D), _row_spec(D), _row_spec(D),
            pl.BlockSpec((4, POOL_GROUP, POOL_GROUP), lambda i: (0, 0, 0)), _row_spec(D), _row_spec(D)]


def _pool_fwd(x, ln, w, pscale):
    g, shift, scale, gate = ln
    t = x.shape[0]
    tm = _tok_tile(t)

    def body(x_ref, xp_ref, g_ref, sc_ref, sh_ref, w_ref, ps_ref, gt_ref, y_ref):
        i = pl.program_id(0)
        _, pooled = _pool_pooled(x_ref, xp_ref, g_ref[...], sc_ref[...], sh_ref[...], i, tm)
        mixed = [jnp.dot(pooled[gi].astype(bf16), w_ref[gi], preferred_element_type=f32) for gi in range(4)]
        y = jnp.concatenate(mixed, axis=1) * ps_ref[...]
        y_ref[...] = x_ref[...] + (1.0 + gt_ref[...]) * y

    return pl.pallas_call(
        body, name="pool_fwd", grid=(t // tm,), in_specs=_pool_specs(t, tm),
        out_specs=_tok_spec(tm, D), out_shape=jax.ShapeDtypeStruct((t, D), f32),
        compiler_params=_cp(1),
    )(x, x, g, scale, shift, w, pscale, gate)


def _pool_bwd(dxn, x, ln, w, pscale):
    g, shift, scale, gate = ln
    t = x.shape[0]
    tm = _tok_tile(t)
    nt = t // tm
    per = tm // POOL_HALO

    def body_a(x_ref, xp_ref, g_ref, sc_ref, sh_ref, w_ref, ps_ref, gt_ref, dxn_ref,
               dp_ref, dw_ref, dps_ref, dgt_ref):
        i = pl.program_id(0)
        first = i == 0
        _, pooled = _pool_pooled(x_ref, xp_ref, g_ref[...], sc_ref[...], sh_ref[...], i, tm)
        dxn_v = dxn_ref[...]
        dy = (1.0 + gt_ref[...]) * dxn_v
        dmixed = dy * ps_ref[...]
        mixed, dps = [], []
        for gi in range(4):
            sl = slice(gi * POOL_GROUP, (gi + 1) * POOL_GROUP)
            pb = pooled[gi].astype(bf16)
            dmb = dmixed[:, sl].astype(bf16)
            mixed.append(jnp.dot(pb, w_ref[gi], preferred_element_type=f32))
            dp_ref[:, sl] = lax.dot_general(dmb, w_ref[gi], _DN["nt"], preferred_element_type=f32)
            dwg = lax.dot_general(pb, dmb, _DN["tn"], preferred_element_type=f32)

            @pl.when(first)
            def _():
                dw_ref[gi] = dwg

            @pl.when(jnp.logical_not(first))
            def _():
                dw_ref[gi] += dwg
        mixed = jnp.concatenate(mixed, axis=1)
        _acc_add(first, dps_ref, jnp.sum(dy * mixed, axis=0, keepdims=True))
        _acc_add(first, dgt_ref, jnp.sum(dxn_v * (mixed * ps_ref[...]), axis=0, keepdims=True))

    row = jax.ShapeDtypeStruct((1, D), f32)
    dpooled, dw, dps, dgate = pl.pallas_call(
        body_a, name="pool_bwd_a", grid=(nt,), in_specs=_pool_specs(t, tm) + [_tok_spec(tm, D)],
        out_specs=[_tok_spec(tm, D), pl.BlockSpec((4, POOL_GROUP, POOL_GROUP), lambda i: (0, 0, 0)),
                   _row_spec(D), _row_spec(D)],
        out_shape=[jax.ShapeDtypeStruct((t, D), f32), jax.ShapeDtypeStruct((4, POOL_GROUP, POOL_GROUP), f32), row, row],
        compiler_params=_cp(1),
    )(x, x, g, scale, shift, w, pscale, gate, dxn)

    def body_b(dp_ref, dpn_ref, x_ref, g_ref, sc_ref, sh_ref, dxn_ref, dx_ref, dg_ref, dsc_ref, dsh_ref):
        i = pl.program_id(0)
        dp = dp_ref[...]
        dpn = jnp.where(i == nt - 1, 0.0, dpn_ref[...])
        t_idx = i * tm + lax.broadcasted_iota(jnp.int32, (tm + POOL_HALO, 1), 0)
        ex = jnp.concatenate([dp, dpn], axis=0)
        parts = []
        for gi, w_ in enumerate(POOL_WINDOWS):
            parts.append(ex[:, gi * POOL_GROUP:(gi + 1) * POOL_GROUP] / _pool_cnt(t_idx, w_))
        sums = _pool_windows(jnp.concatenate(parts, axis=1), 0, True)
        dh = jnp.concatenate([s[:tm, :] for s in sums], axis=1) - dp
        _, vjp = jax.vjp(_adaln, x_ref[...], g_ref[...], sc_ref[...], sh_ref[...])
        dx, dg, dsc, dsh = vjp(dh)
        dx_ref[...] = dxn_ref[...] + dx
        first = i == 0
        _acc_add(first, dg_ref, dg)
        _acc_add(first, dsc_ref, dsc)
        _acc_add(first, dsh_ref, dsh)

    nxt = pl.BlockSpec((POOL_HALO, D), lambda i: (jnp.minimum((i + 1) * per, t // POOL_HALO - 1), 0))
    dx, dg, dscale, dshift = pl.pallas_call(
        body_b, name="pool_bwd_b", grid=(nt,),
        in_specs=[_tok_spec(tm, D), nxt, _tok_spec(tm, D), _row_spec(D), _row_spec(D), _row_spec(D), _tok_spec(tm, D)],
        out_specs=[_tok_spec(tm, D), _row_spec(D), _row_spec(D), _row_spec(D)],
        out_shape=[jax.ShapeDtypeStruct((t, D), f32), row, row, row],
        compiler_params=_cp(1),
    )(dpooled, dpooled, x, g, scale, shift, dxn)
    return dx, (dg, dshift, dscale, dgate), dw, dps


def _conv_taps(czx, cw):
    return cw[0:1, :] * _roll_rows(czx, 2) + cw[1:2, :] * _roll_rows(czx, 1) + cw[2:3, :] * czx


def _conv_fwd(p, cw):
    t = p.shape[0]
    tm = _tok_tile(t)
    per = tm // CONV_HALO

    def body(p_ref, pp_ref, cw_ref, q_ref):
        i = pl.program_id(0)
        cz = p_ref[:, D:2 * D] * p_ref[:, 2 * D:3 * D]
        czp = jnp.where(i == 0, 0.0, pp_ref[:, D:2 * D] * pp_ref[:, 2 * D:3 * D])
        conv = _conv_taps(jnp.concatenate([czp, cz], axis=0), cw_ref[...])[CONV_HALO:, :]
        q_ref[...] = (p_ref[:, 0:D] * conv).astype(bf16)

    prev = pl.BlockSpec((CONV_HALO, 3 * D), lambda i: (jnp.maximum(i * per - 1, 0), 0))
    return pl.pallas_call(
        body, name="conv_fwd", grid=(t // tm,),
        in_specs=[_tok_spec(tm, 3 * D), prev, _row_spec(D, 8)],
        out_specs=_tok_spec(tm, D), out_shape=jax.ShapeDtypeStruct((t, D), bf16),
        compiler_params=_cp(1),
    )(p, p, cw)


def _conv_bwd(p, cw, dq):
    t = p.shape[0]
    tm = _tok_tile(t)
    nt = t // tm
    per = tm // CONV_HALO

    def body(p_ref, pp_ref, pn_ref, cw_ref, dq_ref, dqn_ref, dp_ref, dcw_ref):
        i = pl.program_id(0)
        cw_v = cw_ref[...]
        b, c, z = p_ref[:, 0:D], p_ref[:, D:2 * D], p_ref[:, 2 * D:3 * D]
        cz = c * z
        czp = jnp.where(i == 0, 0.0, pp_ref[:, D:2 * D] * pp_ref[:, 2 * D:3 * D])
        czx = jnp.concatenate([czp, cz], axis=0)
        conv = _conv_taps(czx, cw_v)[CONV_HALO:, :]
        dq_v = dq_ref[...]
        dconv = dq_v * b
        dconv_n = jnp.where(i == nt - 1, 0.0, dqn_ref[...] * pn_ref[:, 0:D])
        dcx = jnp.concatenate([dconv, dconv_n], axis=0)
        dcz = (cw_v[2:3, :] * dcx + cw_v[1:2, :] * _roll_rows(dcx, -1) + cw_v[0:1, :] * _roll_rows(dcx, -2))[:tm, :]
        dp_ref[:, 0:D] = (dq_v * conv).astype(bf16)
        dp_ref[:, D:2 * D] = (dcz * z).astype(bf16)
        dp_ref[:, 2 * D:3 * D] = (dcz * c).astype(bf16)
        dw2 = jnp.sum(dconv * cz, axis=0, keepdims=True)
        dw1 = jnp.sum(dconv * _roll_rows(czx, 1)[CONV_HALO:, :], axis=0, keepdims=True)
        dw0 = jnp.sum(dconv * _roll_rows(czx, 2)[CONV_HALO:, :], axis=0, keepdims=True)
        _acc_add(i == 0, dcw_ref, jnp.concatenate([dw0, dw1, dw2, jnp.zeros((5, D), f32)], axis=0))

    prev = pl.BlockSpec((CONV_HALO, 3 * D), lambda i: (jnp.maximum(i * per - 1, 0), 0))
    last = t // CONV_HALO - 1
    nxt3 = pl.BlockSpec((CONV_HALO, 3 * D), lambda i: (jnp.minimum((i + 1) * per, last), 0))
    nxt1 = pl.BlockSpec((CONV_HALO, D), lambda i: (jnp.minimum((i + 1) * per, last), 0))
    return pl.pallas_call(
        body, name="conv_bwd", grid=(nt,),
        in_specs=[_tok_spec(tm, 3 * D), prev, nxt3, _row_spec(D, 8), _tok_spec(tm, D), nxt1],
        out_specs=[_tok_spec(tm, 3 * D), _row_spec(D, 8)],
        out_shape=[jax.ShapeDtypeStruct((t, 3 * D), bf16), jax.ShapeDtypeStruct((8, D), f32)],
        compiler_params=_cp(1),
    )(p, p, p, cw, dq, dq)


def _convmix_fwd(x, ln, w_in, cw, w_out):
    g, shift, scale, gate = ln
    t = x.shape[0]
    tm = _mm_tile(t)
    cs = w_in.shape[2]
    h = _ln_fwd(x, g, scale, shift)
    p = _mmx(h, w_in, "nn", "conv_in", (t // tm, N_DEV, 1),
             pl.BlockSpec((tm, D), lambda i, j, k: (i, 0)),
             pl.BlockSpec((1, D, cs), lambda i, j, k: (j, 0, 0)),
             pl.BlockSpec((tm, cs), lambda i, j, k: (i, j)), jax.ShapeDtypeStruct((t, N_DEV * cs), f32))
    q = _conv_fwd(p, cw)
    y = _mm(q, w_out, "nn", f32, "conv_out")
    return _resid_fwd(x, y, gate, 1.0), (x, h, p, q, y)


def _convmix_bwd(dxn, saved, ln, w_in, cw, w_out):
    x, h, p, q, y = saved
    g, shift, scale, gate = ln
    dy, dgate = _resid_bwd(dxn, y, gate, 1.0)
    dq = _mm(dy, w_out, "nt", f32, "conv_dq")
    dw_out = _mm(q, dy, "tn", f32, "conv_dwout")
    dp, dcw = _conv_bwd(p, cw, dq)
    t = x.shape[0]
    tm = _mm_tile(t)
    tk = _pick(t, (512, 256, 128))
    cs = w_in.shape[2]
    dh = _mmx(dp, w_in, "nt", "conv_dh", (t // tm, 1, N_DEV),
              pl.BlockSpec((tm, cs), lambda i, j, k: (i, k)),
              pl.BlockSpec((1, D, cs), lambda i, j, k: (k, 0, 0)),
              pl.BlockSpec((tm, D), lambda i, j, k: (i, 0)), jax.ShapeDtypeStruct((t, D), f32))
    dw_in = _mmx(h, dp, "tn", "conv_dwin", (N_DEV, 1, t // tk),
                 pl.BlockSpec((tk, D), lambda i, j, k: (k, 0)),
                 pl.BlockSpec((tk, cs), lambda i, j, k: (k, i)),
                 pl.BlockSpec((1, D, cs), lambda i, j, k: (i, 0, 0)), jax.ShapeDtypeStruct((N_DEV, D, cs), f32))
    dx, dg, dscale, dshift = _ln_bwd(x, g, scale, shift, dh, dxn)
    return dx, (dg, dshift, dscale, dgate), dw_in, dcw[0:3], dw_out


def _exact_dot(tri, v):
    v1 = v.astype(bf16)
    r1 = v - v1.astype(f32)
    v2 = r1.astype(bf16)
    v3 = (r1 - v2.astype(f32)).astype(bf16)
    d = lambda p: jnp.dot(tri, p, preferred_element_type=f32)
    return d(v1) + d(v2) + d(v3)


def _fox_cumf(fl, b_f):
    t = fl.shape[0]
    tc = min(t, 256)

    def body(fl_ref, b_ref, f_ref, carry_ref):
        i = pl.program_id(0)

        @pl.when(i == 0)
        def _():
            carry_ref[...] = jnp.zeros_like(carry_ref)

        lf = jax.nn.log_sigmoid(fl_ref[...] + b_ref[...])
        r = lax.broadcasted_iota(jnp.int32, (tc, tc), 0)
        c = lax.broadcasted_iota(jnp.int32, (tc, tc), 1)
        tri = (r >= c).astype(bf16)
        fc = _exact_dot(tri, lf) + carry_ref[0:1, :]
        f_ref[...] = fc
        carry_ref[0:1, :] = fc[tc - 1:tc, :]

    return pl.pallas_call(
        body, name="fox_cumf", grid=(t // tc,),
        in_specs=[_tok_spec(tc, 128), _row_spec(128)],
        out_specs=_tok_spec(tc, 128), out_shape=jax.ShapeDtypeStruct((t, 128), f32),
        scratch_shapes=[pltpu.VMEM((8, 128), f32)], compiler_params=_cp(1),
    )(fl, b_f)


def _fox_cumf_bwd(df, fl, b_f):
    t = fl.shape[0]
    tc = min(t, 256)
    nt = t // tc

    def body(df_ref, fl_ref, b_ref, dfl_ref, db_ref, carry_ref):
        i = pl.program_id(0)

        @pl.when(i == 0)
        def _():
            carry_ref[...] = jnp.zeros_like(carry_ref)

        r = lax.broadcasted_iota(jnp.int32, (tc, tc), 0)
        c = lax.broadcasted_iota(jnp.int32, (tc, tc), 1)
        tri = (r <= c).astype(bf16)
        dlf = _exact_dot(tri, df_ref[...]) + carry_ref[0:1, :]
        carry_ref[0:1, :] = dlf[0:1, :]
        dfl = dlf * jax.nn.sigmoid(-(fl_ref[...] + b_ref[...]))
        dfl_ref[...] = dfl
        _acc_add(i == 0, db_ref, jnp.sum(dfl, axis=0, keepdims=True))

    return pl.pallas_call(
        body, name="fox_cumf_bwd", grid=(nt,),
        in_specs=[_tok_spec(tc, 128, nt, True), _tok_spec(tc, 128, nt, True), _row_spec(128)],
        out_specs=[_tok_spec(tc, 128, nt, True), _row_spec(128)],
        out_shape=[jax.ShapeDtypeStruct((t, 128), f32), jax.ShapeDtypeStruct((1, 128), f32)],
        scratch_shapes=[pltpu.VMEM((8, 128), f32)], compiler_params=_cp(1),
    )(df, fl, b_f)


def _head_norm(v, gain, mult):
    return v * lax.rsqrt(jnp.mean(v * v, axis=-1, keepdims=True) + NORM_EPS) * gain * mult


def _fox_qknorm(qkv, q_gain, k_gain):
    t = qkv.shape[2]
    tm = _tok_tile(t)

    def body(q_ref, k_ref, v_ref, qg_ref, kg_ref, o_ref):
        o_ref[0, 0] = _head_norm(q_ref[0, 0], qg_ref[...], FOX_HEAD_DIM ** -0.5).astype(bf16)
        o_ref[1, 0] = _head_norm(k_ref[0, 0], kg_ref[...], 1.0).astype(bf16)
        o_ref[2, 0] = v_ref[0, 0].astype(bf16)

    blk = lambda s: pl.BlockSpec((1, 1, tm, FOX_HEAD_DIM), lambda h, i: (s, h, i, 0))
    gspec = pl.BlockSpec((1, FOX_HEAD_DIM), lambda h, i: (0, 0))
    return pl.pallas_call(
        body, name="fox_qknorm", grid=(FOX_HEADS, t // tm),
        in_specs=[blk(0), blk(1), blk(2), gspec, gspec],
        out_specs=pl.BlockSpec((3, 1, tm, FOX_HEAD_DIM), lambda h, i: (0, h, i, 0)),
        out_shape=jax.ShapeDtypeStruct((3, FOX_HEADS, t, FOX_HEAD_DIM), bf16),
        compiler_params=_cp(2),
    )(qkv, qkv, qkv, q_gain, k_gain)


def _fox_qknorm_bwd(qkv, q_gain, k_gain, dqn, dkn):
    t = qkv.shape[2]
    tm = _tok_tile(t)

    def body(q_ref, k_ref, qg_ref, kg_ref, dqn_ref, dkn_ref, o_ref, dqg_ref, dkg_ref):
        first = jnp.logical_and(pl.program_id(0) == 0, pl.program_id(1) == 0)
        _, vq = jax.vjp(lambda v, gn: _head_norm(v, gn, FOX_HEAD_DIM ** -0.5), q_ref[0, 0], qg_ref[...])
        dq, dqg = vq(dqn_ref[0])
        _, vk = jax.vjp(lambda v, gn: _head_norm(v, gn, 1.0), k_ref[0, 0], kg_ref[...])
        dk, dkg = vk(dkn_ref[0])
        o_ref[0, 0] = dq
        o_ref[1, 0] = dk
        _acc_add(first, dqg_ref, dqg)
        _acc_add(first, dkg_ref, dkg)

    blk = lambda s: pl.BlockSpec((1, 1, tm, FOX_HEAD_DIM), lambda h, i: (s, h, i, 0))
    hblk = pl.BlockSpec((1, tm, FOX_HEAD_DIM), lambda h, i: (h, i, 0))
    gspec = pl.BlockSpec((1, FOX_HEAD_DIM), lambda h, i: (0, 0))
    gshape = jax.ShapeDtypeStruct((1, FOX_HEAD_DIM), f32)
    return pl.pallas_call(
        body, name="fox_qknorm_bwd", grid=(FOX_HEADS, t // tm),
        in_specs=[blk(0), blk(1), gspec, gspec, hblk, hblk],
        out_specs=[pl.BlockSpec((2, 1, tm, FOX_HEAD_DIM), lambda h, i: (0, h, i, 0)), gspec, gspec],
        out_shape=[jax.ShapeDtypeStruct((2, FOX_HEADS, t, FOX_HEAD_DIM), f32), gshape, gshape],
        compiler_params=_cp(2),
    )(qkv, qkv, q_gain, k_gain, dqn, dkn)


def _fox_block(t):
    return min(t, 512)


FOX_STRIP = 64
FOX_HEADS_PER_STEP = 2


def _fox_mask(s, row0):
    r = row0 + lax.broadcasted_iota(jnp.int32, s.shape, 0)
    c = lax.broadcasted_iota(jnp.int32, s.shape, 1)
    return jnp.where(r >= c, s, -jnp.inf)


def _strips(tb):
    ts = min(tb, FOX_STRIP)
    return [(r * ts, slice(r * ts, (r + 1) * ts)) for r in range(tb // ts)]


def _fox_attn_fwd(qkvn, fcol, fref):
    t = qkvn.shape[2]
    tb = _fox_block(t)
    nq = t // tb
    dh = FOX_HEAD_DIM

    hp = FOX_HEADS_PER_STEP

    def body(q_ref, k_ref, v_ref, fc_ref, fr_ref, o_ref, lse_ref):
        i = pl.program_id(1)

        def step(j, carry, diag):
            off = pl.multiple_of(j * tb, tb)
            out = []
            for hh in range(hp):
                m, l, acc = carry[hh]
                kj = k_ref[0, hh, pl.ds(off, tb), :]
                vj = v_ref[0, hh, pl.ds(off, tb), :]
                bias = fr_ref[hh, pl.ds(i, 1), 0:1] - fc_ref[hh, :, pl.ds(off, tb)]
                s = lax.dot_general(q_ref[0, hh], kj, _DN["nt"], preferred_element_type=f32) + bias
                if diag:
                    s = _fox_mask(s, 0)
                m_new = jnp.maximum(m, jnp.max(s, axis=-1, keepdims=True))
                alpha = jnp.exp(m - m_new)
                p = jnp.exp(s - m_new)
                l = alpha * l + jnp.sum(p, axis=-1, keepdims=True)
                p_hi = p.astype(bf16)
                p_lo = (p - p_hi.astype(f32)).astype(bf16)
                pv = jnp.dot(p_hi, vj, preferred_element_type=f32) + jnp.dot(p_lo, vj, preferred_element_type=f32)
                out.append((m_new, l, alpha * acc + pv))
            return tuple(out)

        one = (jnp.full((tb, 1), -jnp.inf, f32), jnp.zeros((tb, 1), f32), jnp.zeros((tb, dh), f32))
        carry = lax.fori_loop(0, i, lambda j, c: step(j, c, False), (one,) * hp)
        for hh, (m, l, acc) in enumerate(step(i, carry, True)):
            o_ref[hh] = acc / l
            lse_ref[hh] = jnp.broadcast_to(m + jnp.log(l), (tb, 128))

    return pl.pallas_call(
        body, name="fox_attn_fwd", grid=(FOX_HEADS // hp, nq),
        in_specs=[pl.BlockSpec((1, hp, tb, dh), lambda h, i: (0, h, i, 0)),
                  pl.BlockSpec((1, hp, t, dh), lambda h, i: (1, h, 0, 0)),
                  pl.BlockSpec((1, hp, t, dh), lambda h, i: (2, h, 0, 0)),
                  pl.BlockSpec((hp, 1, t), lambda h, i: (h, 0, 0)),
                  pl.BlockSpec((hp, nq, 128), lambda h, i: (h, 0, 0))],
        out_specs=[pl.BlockSpec((hp, tb, dh), lambda h, i: (h, i, 0)),
                   pl.BlockSpec((hp, tb, 128), lambda h, i: (h, i, 0))],
        out_shape=[jax.ShapeDtypeStruct((FOX_HEADS, t, dh), f32), jax.ShapeDtypeStruct((FOX_HEADS, t, 128), f32)],
        compiler_params=_cp(2),
    )(qkvn, qkvn, qkvn, fcol, fref)


def _fox_delta(o, do):
    t = o.shape[1]
    tm = _tok_tile(t)
    dh = FOX_HEAD_DIM

    def body(o_ref, do_ref, dl_ref, dob_ref):
        dob = do_ref[0].astype(bf16)
        dob_ref[0] = dob
        dl_ref[0] = jnp.broadcast_to(jnp.sum(dob.astype(f32) * o_ref[0], axis=-1, keepdims=True), (tm, 128))

    hb = pl.BlockSpec((1, tm, dh), lambda h, i: (h, i, 0))
    return pl.pallas_call(
        body, name="fox_delta", grid=(FOX_HEADS, t // tm), in_specs=[hb, hb],
        out_specs=[pl.BlockSpec((1, tm, 128), lambda h, i: (h, i, 0)), hb],
        out_shape=[jax.ShapeDtypeStruct((FOX_HEADS, t, 128), f32), jax.ShapeDtypeStruct((FOX_HEADS, t, dh), bf16)],
        compiler_params=_cp(2),
    )(o, do)


def _fox_attn_bwd_kv(qkvn, fcol, fref, lse, delta, dob):
    t = qkvn.shape[2]
    tb = _fox_block(t)
    nq = t // tb
    dh = FOX_HEAD_DIM

    def body(q_ref, k_ref, v_ref, fc_ref, fr_ref, lse_ref, dl_ref, do_ref, dk_ref, dv_ref, df_ref, dq_ref,
             s_ref, dp_ref, p_ref, ds_ref):
        j = pl.program_id(1)
        kj = k_ref[0, 0]
        vj = v_ref[0, 0]
        fcol_j = fc_ref[0]
        dk_ref[...] = jnp.zeros_like(dk_ref)
        dv_ref[...] = jnp.zeros_like(dv_ref)
        df_ref[...] = jnp.zeros_like(df_ref)

        @pl.when(j == 0)
        def _():
            dq_ref[...] = jnp.zeros_like(dq_ref)

        def step(i, diag):
            off = pl.multiple_of(i * tb, tb)
            qi = q_ref[0, 0, pl.ds(off, tb), :]
            doi = do_ref[0, pl.ds(off, tb), :]
            s_ref[...] = lax.dot_general(qi, kj, _DN["nt"], preferred_element_type=f32)
            dp_ref[...] = lax.dot_general(doi, vj, _DN["nt"], preferred_element_type=f32)
            bias = fr_ref[0, pl.ds(i, 1), 0:1] - fcol_j
            df = jnp.zeros((1, tb), f32)
            for row0, rs in _strips(tb):
                rows = pl.ds(off + row0, rs.stop - rs.start)
                s = s_ref[rs, :] + bias
                if diag:
                    s = _fox_mask(s, row0)
                p = jnp.exp(s - lse_ref[0, rows, 0:1])
                p_ref[rs, :] = p.astype(bf16)
                ds = p * (dp_ref[rs, :] - dl_ref[0, rows, 0:1])
                ds_ref[rs, :] = ds.astype(bf16)
                df = df + jnp.sum(ds, axis=0, keepdims=True)
            dv_ref[0] += lax.dot_general(p_ref[...], doi, _DN["tn"], preferred_element_type=f32)
            dk_ref[0] += lax.dot_general(ds_ref[...], qi, _DN["tn"], preferred_element_type=f32)
            dq_ref[0, pl.ds(off, tb), :] += jnp.dot(ds_ref[...], kj, preferred_element_type=f32)
            df_ref[0] -= df

        def loop_body(i, carry):
            step(i, False)
            return carry

        step(j, True)
        lax.fori_loop(j + 1, nq, loop_body, 0)

    scratch = [pltpu.VMEM((tb, tb), f32), pltpu.VMEM((tb, tb), f32), pltpu.VMEM((tb, tb), bf16),
               pltpu.VMEM((tb, tb), bf16)]
    hb = pl.BlockSpec((1, tb, dh), lambda h, j: (h, j, 0))
    full = pl.BlockSpec((1, t, dh), lambda h, j: (h, 0, 0))
    full128 = pl.BlockSpec((1, t, 128), lambda h, j: (h, 0, 0))
    hshape = jax.ShapeDtypeStruct((FOX_HEADS, t, dh), f32)
    return pl.pallas_call(
        body, name="fox_attn_bwd_kv", grid=(FOX_HEADS, nq),
        in_specs=[pl.BlockSpec((1, 1, t, dh), lambda h, j: (0, h, 0, 0)),
                  pl.BlockSpec((1, 1, tb, dh), lambda h, j: (1, h, j, 0)),
                  pl.BlockSpec((1, 1, tb, dh), lambda h, j: (2, h, j, 0)),
                  pl.BlockSpec((1, 1, tb), lambda h, j: (h, 0, j)),
                  pl.BlockSpec((1, nq, 128), lambda h, j: (h, 0, 0)),
                  full128, full128, full],
        out_specs=[hb, hb, pl.BlockSpec((1, 1, tb), lambda h, j: (h, 0, j)), full],
        out_shape=[hshape, hshape, jax.ShapeDtypeStruct((FOX_HEADS, 1, t), f32), hshape],
        scratch_shapes=scratch, compiler_params=_cp(2),
    )(qkvn, qkvn, qkvn, fcol, fref, lse, delta, dob)


def _heads_of(a, n):
    t = a.shape[0]
    return a.reshape(t, n, FOX_HEADS, FOX_HEAD_DIM).transpose(1, 2, 0, 3)


def _fox_fwd(x, ln, w_in, b_f, q_gain, k_gain, w_o):
    g, shift, scale, gate = ln
    t = x.shape[0]
    tb = _fox_block(t)
    h = _ln_fwd(x, g, scale, shift)
    proj = _mm(h, w_in, "nn", f32, "fox_in")
    qkv = _heads_of(proj[:, :3 * D], 3)
    fl = proj[:, 3 * D:3 * D + 128]
    fcum = _fox_cumf(fl, b_f)
    fcol = fcum[:, :FOX_HEADS].T.reshape(FOX_HEADS, 1, t)
    fref = jnp.broadcast_to(fcol[:, 0, ::tb][:, :, None], (FOX_HEADS, t // tb, 128))
    qkvn = _fox_qknorm(qkv, q_gain, k_gain)
    o, lse = _fox_attn_fwd(qkvn, fcol, fref)
    ob = o.transpose(1, 0, 2).reshape(t, D).astype(bf16)
    y = _mm(ob, w_o, "nn", f32, "fox_out")
    return _resid_fwd(x, y, gate, 1.0), (x, h, qkv, fl, fcol, fref, qkvn, o, lse, ob, y)


def _fox_bwd(dxn, saved, ln, w_in, b_f, q_gain, k_gain, w_o):
    x, h, qkv, fl, fcol, fref, qkvn, o, lse, ob, y = saved
    g, shift, scale, gate = ln
    t = x.shape[0]
    dy, dgate = _resid_bwd(dxn, y, gate, 1.0)
    do_flat = _mm(dy, w_o, "nt", f32, "fox_do")
    dw_o = _mm(ob, dy, "tn", f32, "fox_dwo")
    do = do_flat.reshape(t, FOX_HEADS, FOX_HEAD_DIM).transpose(1, 0, 2)
    delta, dob = _fox_delta(o, do)
    dkn, dv, dfcol, dqn = _fox_attn_bwd_kv(qkvn, fcol, fref, lse, delta, dob)
    dqk, dqg, dkg = _fox_qknorm_bwd(qkv, q_gain, k_gain, dqn, dkn)
    df = jnp.pad(dfcol.reshape(FOX_HEADS, t).T, ((0, 0), (0, 128 - FOX_HEADS)))
    dfl, db_f = _fox_cumf_bwd(df, fl, b_f)
    dqkv = jnp.concatenate([dqk, dv[None]], axis=0).transpose(2, 0, 1, 3).reshape(t, 3 * D)
    dproj = jnp.concatenate([dqkv.astype(bf16), dfl.astype(bf16)], axis=1)
    dh = _mm(dproj, w_in, "nt", f32, "fox_dh")
    dw_in = _mm(h, dproj, "tn", f32, "fox_dwin")
    dx, dg, dscale, dshift = _ln_bwd(x, g, scale, shift, dh, dxn)
    return dx, (dg, dshift, dscale, dgate), dw_in, db_f, dqg, dkg, dw_o


def _s5_disc(lam_re, lam_im, log_dt, b_re, b_im):
    dt = jnp.exp(log_dt)
    mag = jnp.exp(lam_re * dt)
    lb_re, lb_im = mag * jnp.cos(lam_im * dt), mag * jnp.sin(lam_im * dt)
    den = lam_re * lam_re + lam_im * lam_im
    nr, ni = lb_re - 1.0, lb_im
    k_re = (nr * lam_re + ni * lam_im) / den
    k_im = (ni * lam_re - nr * lam_im) / den
    return lb_re, lb_im, k_re * b_re - k_im * b_im, k_re * b_im + k_im * b_re


def _s5_prep(lam_re, lam_im, log_dt, b_re, b_im):
    def body(ar_ref, ai_ref, dt_ref, br_ref, bi_ref, lr_ref, li_ref, bbr_ref, bbi_ref):
        lr, li, bbr, bbi = _s5_disc(ar_ref[...], ai_ref[...], dt_ref[...], br_ref[...], bi_ref[...])
        lr_ref[...] = lr
        li_ref[...] = li
        bbr_ref[...] = bbr
        bbi_ref[...] = bbi

    small = jax.ShapeDtypeStruct(lam_re.shape, f32)
    bigs = jax.ShapeDtypeStruct(b_re.shape, f32)
    return pl.pallas_call(body, name="s5_prep", out_shape=[small, small, bigs, bigs])(lam_re, lam_im, log_dt, b_re, b_im)


def _s5_prep_bwd(lam_re, lam_im, log_dt, b_re, b_im, dlr, dli, dbbr, dbbi):
    def body(ar_ref, ai_ref, dt_ref, br_ref, bi_ref, dlr_ref, dli_ref, dbbr_ref, dbbi_ref,
             dar_ref, dai_ref, ddt_ref, dbr_ref, dbi_ref):
        _, vjp = jax.vjp(_s5_disc, ar_ref[...], ai_ref[...], dt_ref[...], br_ref[...], bi_ref[...])
        dar, dai, ddt, dbr, dbi = vjp((dlr_ref[...], dli_ref[...], dbbr_ref[...], dbbi_ref[...]))
        dar_ref[...] = dar
        dai_ref[...] = dai
        ddt_ref[...] = jnp.broadcast_to(jnp.sum(ddt, axis=-1, keepdims=True), ddt.shape)
        dbr_ref[...] = dbr
        dbi_ref[...] = dbi

    small = jax.ShapeDtypeStruct(lam_re.shape, f32)
    bigs = jax.ShapeDtypeStruct(b_re.shape, f32)
    return pl.pallas_call(body, name="s5_prep_bwd", out_shape=[small, small, small, bigs, bigs])(
        lam_re, lam_im, log_dt, b_re, b_im, dlr, dli, dbbr, dbbi)


def _s5_tile(t):
    return min(t, 128)


def _s5_blk(k, width):
    return slice(k * width, (k + 1) * width)


def _s5_in_bd(bb):
    b4 = bb.reshape(S5_BLOCKS, 8, S5_GROUP, S5_STATE)
    return jnp.einsum("kgin,gh->kgihn", b4, jnp.eye(8, dtype=bb.dtype)).reshape(S5_BLOCKS, S5_BCH, S5_BST)


def _s5_in_bd_diag(bd):
    b5 = bd.reshape(S5_BLOCKS, 8, S5_GROUP, 8, S5_STATE)
    return jnp.einsum("kgihn,gh->kgin", b5, jnp.eye(8, dtype=bd.dtype)).reshape(S5_GROUPS, S5_GROUP, S5_STATE)


def _s5_out_bd(c):
    c4 = c.reshape(S5_BLOCKS, 8, S5_GROUP, S5_STATE)
    return jnp.einsum("kgin,gh->kgnhi", c4, jnp.eye(8, dtype=c.dtype)).reshape(S5_BLOCKS, S5_BST, S5_BCH)


def _s5_out_bd_diag(bd):
    c5 = bd.reshape(S5_BLOCKS, 8, S5_STATE, 8, S5_GROUP)
    return jnp.einsum("kgnhi,gh->kgin", c5, jnp.eye(8, dtype=bd.dtype)).reshape(S5_GROUPS, S5_GROUP, S5_STATE)


def _s5_scan_fwd(x, ln, lb_re, lb_im, bbr_bd, bbi_bd, cr_bd, ci_bd, dskip):
    g, shift, scale, _ = ln
    t = x.shape[0]
    tm = _s5_tile(t)
    ns = S5_NSTATE

    def body(x_ref, g_ref, sc_ref, sh_ref, ar_ref, ai_ref, bbr_ref, bbi_ref, cr_ref, ci_ref, d_ref,
             yy_ref, xr_ref, xi_ref, cre_ref, cim_ref):
        @pl.when(pl.program_id(0) == 0)
        def _():
            cre_ref[...] = jnp.zeros_like(cre_ref)
            cim_ref[...] = jnp.zeros_like(cim_ref)

        h = _adaln(x_ref[...], g_ref[...], sc_ref[...], sh_ref[...])
        ub = h.astype(bf16)
        for k in range(S5_BLOCKS):
            uk = ub[:, _s5_blk(k, S5_BCH)]
            xr_ref[:, _s5_blk(k, S5_BST)] = jnp.dot(uk, bbr_ref[k], preferred_element_type=f32)
            xi_ref[:, _s5_blk(k, S5_BST)] = jnp.dot(uk, bbi_ref[k], preferred_element_type=f32)
        ar, ai = ar_ref[...], ai_ref[...]

        def step(tt, carry):
            sr, si = carry
            row = pl.ds(tt, 1)
            nr = (ar * sr - ai * si) + xr_ref[row, :]
            ni = (ar * si + ai * sr) + xi_ref[row, :]
            xr_ref[row, :] = nr
            xi_ref[row, :] = ni
            return nr, ni

        sr, si = lax.fori_loop(0, tm, step, (cre_ref[0:1, :], cim_ref[0:1, :]), unroll=2)
        cre_ref[0:1, :] = sr
        cim_ref[0:1, :] = si
        for k in range(S5_BLOCKS):
            sb = _s5_blk(k, S5_BST)
            yk = (jnp.dot(xr_ref[:, sb].astype(bf16), cr_ref[k], preferred_element_type=f32)
                  - jnp.dot(xi_ref[:, sb].astype(bf16), ci_ref[k], preferred_element_type=f32))
            cb = _s5_blk(k, S5_BCH)
            yy_ref[:, cb] = yk + d_ref[:, cb] * h[:, cb]

    bd_in = pl.BlockSpec((S5_BLOCKS, S5_BCH, S5_BST), lambda i: (0, 0, 0))
    bd_out = pl.BlockSpec((S5_BLOCKS, S5_BST, S5_BCH), lambda i: (0, 0, 0))
    st = jax.ShapeDtypeStruct((t, ns), f32)
    return pl.pallas_call(
        body, name="s5_scan_fwd", grid=(t // tm,),
        in_specs=[_tok_spec(tm, D), _row_spec(D), _row_spec(D), _row_spec(D), _row_spec(ns), _row_spec(ns),
                  bd_in, bd_in, bd_out, bd_out, _row_spec(D)],
        out_specs=[_tok_spec(tm, D), _tok_spec(tm, ns), _tok_spec(tm, ns)],
        out_shape=[jax.ShapeDtypeStruct((t, D), f32), st, st],
        scratch_shapes=[pltpu.VMEM((8, ns), f32), pltpu.VMEM((8, ns), f32)],
        compiler_params=_cp(1),
    )(x, g, scale, shift, lb_re, lb_im, bbr_bd, bbi_bd, cr_bd, ci_bd, dskip)


def _s5_scan_bwd(dyy, x, ln, xr, xi, lb_re, lb_im, bbr_bd, bbi_bd, cr_bd, ci_bd, dskip):
    g, shift, scale, _ = ln
    t = x.shape[0]
    tm = _s5_tile(t)
    nt = t // tm
    ns = S5_NSTATE
    per = tm // 8

    def body(dyy_ref, x_ref, g_ref, sc_ref, sh_ref, xr_ref, xi_ref, xrp_ref, xip_ref, ar_ref, ai_ref,
             bbr_ref, bbi_ref, cr_ref, ci_ref, d_ref,
             du_ref, dar_ref, dai_ref, dbbr_ref, dbbi_ref, dcr_ref, dci_ref, dd_ref,
             gr_ref, gi_ref, cre_ref, cim_ref):
        i = pl.program_id(0)
        first = i == 0

        @pl.when(first)
        def _():
            cre_ref[...] = jnp.zeros_like(cre_ref)
            cim_ref[...] = jnp.zeros_like(cim_ref)

        h = _adaln(x_ref[...], g_ref[...], sc_ref[...], sh_ref[...])
        ub = h.astype(bf16)
        dyy_v = dyy_ref[...]
        dyb = dyy_v.astype(bf16)
        for k in range(S5_BLOCKS):
            dk = dyb[:, _s5_blk(k, S5_BCH)]
            sb = _s5_blk(k, S5_BST)
            gr_ref[:, sb] = lax.dot_general(dk, cr_ref[k], _DN["nt"], preferred_element_type=f32)
            gi_ref[:, sb] = -lax.dot_general(dk, ci_ref[k], _DN["nt"], preferred_element_type=f32)
        ar, ai = ar_ref[...], ai_ref[...]

        def step(s, carry):
            nr_, ni_ = carry
            row = pl.ds(tm - 1 - s, 1)
            nr = gr_ref[row, :] + (ar * nr_ + ai * ni_)
            ni = gi_ref[row, :] + (ar * ni_ - ai * nr_)
            gr_ref[row, :] = nr
            gi_ref[row, :] = ni
            return nr, ni

        nr, ni = lax.fori_loop(0, tm, step, (cre_ref[0:1, :], cim_ref[0:1, :]), unroll=2)
        cre_ref[0:1, :] = nr
        cim_ref[0:1, :] = ni

        is_first_tile = i == nt - 1
        xr_v, xi_v = xr_ref[...], xi_ref[...]
        xrp = jnp.where(is_first_tile, 0.0, xrp_ref[...])
        xip = jnp.where(is_first_tile, 0.0, xip_ref[...])
        xr_s = _roll_rows(jnp.concatenate([xrp, xr_v], axis=0), 1)[8:, :]
        xi_s = _roll_rows(jnp.concatenate([xip, xi_v], axis=0), 1)[8:, :]
        gr, gi = gr_ref[...], gi_ref[...]
        _acc_add(first, dar_ref, jnp.sum(gr * xr_s + gi * xi_s, axis=0, keepdims=True))
        _acc_add(first, dai_ref, jnp.sum(gi * xr_s - gr * xi_s, axis=0, keepdims=True))
        _acc_add(first, dd_ref, jnp.sum(dyy_v * h, axis=0, keepdims=True))
        grb, gib = gr.astype(bf16), gi.astype(bf16)
        xrb, xib = xr_v.astype(bf16), xi_v.astype(bf16)
        for k in range(S5_BLOCKS):
            cb, sb = _s5_blk(k, S5_BCH), _s5_blk(k, S5_BST)
            uk, dk = ub[:, cb], dyb[:, cb]
            tn = lambda a_, b_: lax.dot_general(a_, b_, _DN["tn"], preferred_element_type=f32)
            vals = (tn(uk, grb[:, sb]), tn(uk, gib[:, sb]), tn(xrb[:, sb], dk), -tn(xib[:, sb], dk))
            for ref, val in zip((dbbr_ref, dbbi_ref, dcr_ref, dci_ref), vals):
                @pl.when(first)
                def _():
                    ref[k] = val

                @pl.when(jnp.logical_not(first))
                def _():
                    ref[k] += val
            du_k = (lax.dot_general(grb[:, sb], bbr_ref[k], _DN["nt"], preferred_element_type=f32)
                    + lax.dot_general(gib[:, sb], bbi_ref[k], _DN["nt"], preferred_element_type=f32))
            du_ref[:, cb] = du_k + d_ref[:, cb] * dyy_v[:, cb]

    rev = lambda c: _tok_spec(tm, c, nt, True)
    prev = pl.BlockSpec((8, ns), lambda i: (jnp.maximum((nt - 1 - i) * per - 1, 0), 0))
    bd_in = pl.BlockSpec((S5_BLOCKS, S5_BCH, S5_BST), lambda i: (0, 0, 0))
    bd_out = pl.BlockSpec((S5_BLOCKS, S5_BST, S5_BCH), lambda i: (0, 0, 0))
    row_ns = jax.ShapeDtypeStruct((1, ns), f32)
    bd_in_s = jax.ShapeDtypeStruct((S5_BLOCKS, S5_BCH, S5_BST), f32)
    bd_out_s = jax.ShapeDtypeStruct((S5_BLOCKS, S5_BST, S5_BCH), f32)
    return pl.pallas_call(
        body, name="s5_scan_bwd", grid=(nt,),
        in_specs=[rev(D), rev(D), _row_spec(D), _row_spec(D), _row_spec(D), rev(ns), rev(ns), prev, prev,
                  _row_spec(ns), _row_spec(ns), bd_in, bd_in, bd_out, bd_out, _row_spec(D)],
        out_specs=[rev(D), _row_spec(ns), _row_spec(ns), bd_in, bd_in, bd_out, bd_out, _row_spec(D)],
        out_shape=[jax.ShapeDtypeStruct((t, D), f32), row_ns, row_ns, bd_in_s, bd_in_s, bd_out_s, bd_out_s,
                   jax.ShapeDtypeStruct((1, D), f32)],
        scratch_shapes=[pltpu.VMEM((tm, ns), f32), pltpu.VMEM((tm, ns), f32),
                        pltpu.VMEM((8, ns), f32), pltpu.VMEM((8, ns), f32)],
        compiler_params=_cp(1),
    )(dyy, x, g, scale, shift, xr, xi, xr, xi, lb_re, lb_im, bbr_bd, bbi_bd, cr_bd, ci_bd, dskip)


def _s5_gelu(yy):
    t = yy.shape[0]
    tm = _tok_tile(t)

    def body(y_ref, o_ref):
        o_ref[...] = jax.nn.gelu(y_ref[...]).astype(bf16)

    return pl.pallas_call(
        body, name="s5_gelu", grid=(t // tm,), in_specs=[_tok_spec(tm, D)], out_specs=_tok_spec(tm, D),
        out_shape=jax.ShapeDtypeStruct((t, D), bf16), compiler_params=_cp(1),
    )(yy)


def _s5_glu(gl, z):
    return gl * jax.nn.sigmoid(z)


def _s5_out(x, yy, z, gate):
    t = x.shape[0]
    tm = _tok_tile(t)

    def body(x_ref, y_ref, z_ref, gt_ref, o_ref):
        o_ref[...] = x_ref[...] + (1.0 + gt_ref[...]) * _s5_glu(jax.nn.gelu(y_ref[...]), z_ref[...])

    return pl.pallas_call(
        body, name="s5_out", grid=(t // tm,),
        in_specs=[_tok_spec(tm, D), _tok_spec(tm, D), _tok_spec(tm, D), _row_spec(D)],
        out_specs=_tok_spec(tm, D), out_shape=jax.ShapeDtypeStruct((t, D), f32), compiler_params=_cp(1),
    )(x, yy, z, gate)


def _s5_out_bwd(dxn, yy, z, gate):
    t = dxn.shape[0]
    tm = _tok_tile(t)

    def body(dxn_ref, y_ref, z_ref, gt_ref, dz_ref, dgl_ref, dgt_ref):
        dxn_v = dxn_ref[...]
        gl = jax.nn.gelu(y_ref[...])
        out, vjp = jax.vjp(_s5_glu, gl, z_ref[...])
        dgl, dz = vjp((1.0 + gt_ref[...]) * dxn_v)
        dz_ref[...] = dz.astype(bf16)
        dgl_ref[...] = dgl
        _acc_add(pl.program_id(0) == 0, dgt_ref, jnp.sum(dxn_v * out, axis=0, keepdims=True))

    return pl.pallas_call(
        body, name="s5_out_bwd", grid=(t // tm,),
        in_specs=[_tok_spec(tm, D), _tok_spec(tm, D), _tok_spec(tm, D), _row_spec(D)],
        out_specs=[_tok_spec(tm, D), _tok_spec(tm, D), _row_spec(D)],
        out_shape=[jax.ShapeDtypeStruct((t, D), bf16), jax.ShapeDtypeStruct((t, D), f32),
                   jax.ShapeDtypeStruct((1, D), f32)],
        compiler_params=_cp(1),
    )(dxn, yy, z, gate)


def _s5_gelu_bwd(yy, dgl_a, dgl_b):
    t = yy.shape[0]
    tm = _tok_tile(t)

    def body(y_ref, a_ref, b_ref, o_ref):
        _, vjp = jax.vjp(jax.nn.gelu, y_ref[...])
        o_ref[...] = vjp(a_ref[...] + b_ref[...])[0]

    return pl.pallas_call(
        body, name="s5_gelu_bwd", grid=(t // tm,),
        in_specs=[_tok_spec(tm, D), _tok_spec(tm, D), _tok_spec(tm, D)],
        out_specs=_tok_spec(tm, D), out_shape=jax.ShapeDtypeStruct((t, D), f32), compiler_params=_cp(1),
    )(yy, dgl_a, dgl_b)


def _s5_params(lam_re, lam_im, log_dt, b_re, b_im):
    bc = lambda a: a.reshape(S5_GROUPS, 1, -1)
    return (bc(lam_re), bc(lam_im), jnp.broadcast_to(log_dt.reshape(S5_GROUPS, 1, 1), (S5_GROUPS, 1, S5_STATE)),
            b_re.transpose(0, 2, 1), b_im.transpose(0, 2, 1))


def _s5_fwd(x, ln, raw, c_re, c_im, dskip, w_glu):
    gate = ln[3]
    lb_re, lb_im, bb_re, bb_im = _s5_prep(*raw)
    lbr, lbi = lb_re.reshape(1, S5_NSTATE), lb_im.reshape(1, S5_NSTATE)
    bds = (_s5_in_bd(bb_re).astype(bf16), _s5_in_bd(bb_im).astype(bf16),
           _s5_out_bd(c_re).astype(bf16), _s5_out_bd(c_im).astype(bf16))
    yy, xr, xi = _s5_scan_fwd(x, ln, lbr, lbi, *bds, dskip)
    gl = _s5_gelu(yy)
    z = _mm(gl, w_glu, "nn", f32, "s5_glu_mm")
    return _s5_out(x, yy, z, gate), (x, lbr, lbi, bds, yy, xr, xi, gl, z)


def _s5_bwd(dxn, saved, ln, raw, dskip, w_glu):
    x, lbr, lbi, bds, yy, xr, xi, gl, z = saved
    g, shift, scale, gate = ln
    dz, dgl_a, dgate = _s5_out_bwd(dxn, yy, z, gate)
    dgl_b = _mm(dz, w_glu, "nt", f32, "s5_dgl")
    dw_glu = _mm(gl, dz, "tn", f32, "s5_dwglu")
    dyy = _s5_gelu_bwd(yy, dgl_a, dgl_b)
    du, dar, dai, dbbr_bd, dbbi_bd, dcr_bd, dci_bd, dd = _s5_scan_bwd(dyy, x, ln, xr, xi, lbr, lbi, *bds, dskip)
    shp = (S5_GROUPS, 1, S5_STATE)
    d_lam_re, d_lam_im, d_dt, d_b_re, d_b_im = _s5_prep_bwd(
        *raw, dar.reshape(shp), dai.reshape(shp), _s5_in_bd_diag(dbbr_bd), _s5_in_bd_diag(dbbi_bd))
    dx, dg, dscale, dshift = _ln_bwd(x, g, scale, shift, du, dxn)
    grads = dict(
        s5_lam_re=d_lam_re.reshape(1, S5_GROUPS, S5_STATE), s5_lam_im=d_lam_im.reshape(1, S5_GROUPS, S5_STATE),
        s5_log_dt=d_dt[:, 0, 0].reshape(1, S5_GROUPS),
        s5_b_re=d_b_re.transpose(0, 2, 1)[None], s5_b_im=d_b_im.transpose(0, 2, 1)[None],
        s5_c_re=_s5_out_bd_diag(dcr_bd)[None], s5_c_im=_s5_out_bd_diag(dci_bd)[None],
        s5_d=dd, s5_w_glu=dw_glu)
    return dx, (dg, dshift, dscale, dgate), grads


_MESH = pl.DeviceIdType.MESH
_ANY = pl.BlockSpec(memory_space=pl.ANY)


def _me():
    return lax.axis_index("x"), lax.axis_index("y"), lax.axis_index("c")


def _dev_index(x, y, c):
    return 4 * x + 2 * y + c


def _all_gather(vs, name):
    n = len(vs)

    def body(*refs):
        v_refs, out_refs = refs[:n], refs[n:2 * n]
        send_sems, recv_sems, local_sems = refs[2 * n:]
        x, y, cc = _me()
        me, sibling = (x, y, cc), (x, y, 1 - cc)
        chips = [(1 - x, y), (x, 1 - y), (1 - x, 1 - y)]
        sends, local = [], []

        def copy(a, k, block, to, src=None):
            rows = out_refs[a].at[_dev_index(*block)]
            return pltpu.make_async_remote_copy(
                src_ref=rows if src is None else src, dst_ref=rows,
                send_sem=send_sems.at[7 * a + k], recv_sem=recv_sems.at[7 * a + k], device_id=to, device_id_type=_MESH)

        for a in range(n):
            mine = pltpu.make_async_copy(v_refs[a], out_refs[a].at[_dev_index(*me)], local_sems.at[a])
            mine.start()
            local.append(mine)
            first = [copy(a, 0, me, sibling, src=v_refs[a])]
            first += [copy(a, 1 + j, me, (*chip, cc), src=v_refs[a]) for j, chip in enumerate(chips)]
            for cp in first:
                cp.start()
            sends += first
        for a in range(n):
            for j, chip in enumerate(chips):
                copy(a, 1 + j, (*chip, cc), me).wait_recv()
                passed = copy(a, 4 + j, (*chip, cc), sibling)
                passed.start()
                sends.append(passed)
        for a in range(n):
            copy(a, 0, sibling, me).wait_recv()
            for j, chip in enumerate(chips):
                copy(a, 4 + j, (*chip, 1 - cc), me).wait_recv()
        for cp in sends:
            cp.wait_send()
        for cp in local:
            cp.wait()

    return pl.pallas_call(
        body, name=name, out_shape=[jax.ShapeDtypeStruct((N_DEV,) + v.shape, v.dtype) for v in vs],
        in_specs=[_ANY] * n, out_specs=[_ANY] * n,
        scratch_shapes=[pltpu.SemaphoreType.DMA((7 * n,)), pltpu.SemaphoreType.DMA((7 * n,)),
                        pltpu.SemaphoreType.DMA((n,))],
    )(*vs)


def _exchange_pair(vs, name):
    n = len(vs)

    def body(*refs):
        v_refs, out_refs = refs[:n], refs[n:2 * n]
        send_sems, recv_sems = refs[2 * n:]
        x, y, cc = _me()
        sibling = (x, y, 1 - cc)
        copies = []
        for a in range(n):
            for k in range(4):
                cp = pltpu.make_async_remote_copy(
                    src_ref=v_refs[a].at[2 * k + (1 - cc)], dst_ref=out_refs[a].at[k],
                    send_sem=send_sems.at[4 * a + k], recv_sem=recv_sems.at[4 * a + k],
                    device_id=sibling, device_id_type=_MESH)
                cp.start()
                copies.append(cp)
        for cp in copies:
            cp.wait_recv()
        for cp in copies:
            cp.wait_send()

    return pl.pallas_call(
        body, name=name, out_shape=[jax.ShapeDtypeStruct((4,) + v.shape[1:], v.dtype) for v in vs],
        in_specs=[_ANY] * n, out_specs=[_ANY] * n,
        scratch_shapes=[pltpu.SemaphoreType.DMA((4 * n,)), pltpu.SemaphoreType.DMA((4 * n,))],
    )(*vs)


def _pair_sum(v, got):
    _, r, c = v.shape
    tr = _pick(r, (512, 352, 256, 128))
    core = lax.axis_index("c").astype(jnp.int32).reshape(1)

    def body(c_ref, v_ref, g_ref, o_ref):
        o_ref[...] = (v_ref[...] + g_ref[...]).astype(bf16)

    return pl.pallas_call(
        body, name="pair_sum",
        grid_spec=pltpu.PrefetchScalarGridSpec(
            num_scalar_prefetch=1, grid=(4, r // tr),
            in_specs=[pl.BlockSpec((1, tr, c), lambda k, i, c_ref: (2 * k + c_ref[0], i, 0)),
                      pl.BlockSpec((1, tr, c), lambda k, i, c_ref: (k, i, 0))],
            out_specs=pl.BlockSpec((1, tr, c), lambda k, i, c_ref: (k, i, 0))),
        out_shape=jax.ShapeDtypeStruct((4, r, c), bf16), compiler_params=_cp(2),
    )(core, v, got)


def _exchange_chips(vs, name):
    n = len(vs)

    def body(*refs):
        v_refs, out_refs = refs[:n], refs[n:2 * n]
        send_sems, recv_sems, local_sems = refs[2 * n:]
        x, y, cc = _me()
        mine = 2 * x + y
        peers = []
        for mask in (1, 2, 3):
            px = 1 - x if mask & 2 else x
            py = 1 - y if mask & 1 else y
            peers.append((mask - 1, (px, py, cc), 2 * px + py))
        local, sends = [], []
        for a in range(n):
            own = pltpu.make_async_copy(v_refs[a].at[mine], out_refs[a].at[mine], local_sems.at[a])
            own.start()
            local.append(own)
            for k, peer, pchip in peers:
                cp = pltpu.make_async_remote_copy(
                    src_ref=v_refs[a].at[pchip], dst_ref=out_refs[a].at[mine],
                    send_sem=send_sems.at[3 * a + k], recv_sem=recv_sems.at[3 * a + k],
                    device_id=peer, device_id_type=_MESH)
                cp.start()
                sends.append(cp)
        for a in range(n):
            for k, peer, pchip in peers:
                pltpu.make_async_remote_copy(
                    src_ref=v_refs[a].at[pchip], dst_ref=out_refs[a].at[pchip],
                    send_sem=send_sems.at[3 * a + k], recv_sem=recv_sems.at[3 * a + k],
                    device_id=peer, device_id_type=_MESH).wait_recv()
        for cp in sends:
            cp.wait_send()
        for cp in local:
            cp.wait()

    return pl.pallas_call(
        body, name=name, out_shape=[jax.ShapeDtypeStruct(v.shape, v.dtype) for v in vs],
        in_specs=[_ANY] * n, out_specs=[_ANY] * n,
        scratch_shapes=[pltpu.SemaphoreType.DMA((3 * n,)), pltpu.SemaphoreType.DMA((3 * n,)),
                        pltpu.SemaphoreType.DMA((n,))],
    )(*vs)


def _ada_mod(c_all, ada_w):
    cols = ada_w.shape[2]

    def body(c_ref, w_ref, o_ref):
        cond = jax.nn.silu(c_ref[...]).astype(bf16)
        o_ref[0] = jnp.dot(cond, w_ref[0].astype(bf16), preferred_element_type=f32)

    return pl.pallas_call(
        body, name="ada_mod", grid=(DEPTH,),
        in_specs=[pl.BlockSpec((16, D), lambda i: (0, 0)), pl.BlockSpec((1, D, cols), lambda i: (i, 0, 0))],
        out_specs=pl.BlockSpec((1, 16, cols), lambda i: (i, 0, 0)),
        out_shape=jax.ShapeDtypeStruct((DEPTH, 16, cols), f32), compiler_params=_cp(1),
    )(c_all, ada_w)


def _ada_grad(c_all, dmod):
    cols = dmod.shape[2]

    def body(c_ref, d_ref, o_ref):
        cond = jax.nn.silu(c_ref[...]).astype(bf16)
        o_ref[0] = lax.dot_general(cond, d_ref[0].astype(bf16), _DN["tn"], preferred_element_type=f32)

    return pl.pallas_call(
        body, name="ada_grad", grid=(DEPTH,),
        in_specs=[pl.BlockSpec((16, D), lambda i: (0, 0)), pl.BlockSpec((1, 16, cols), lambda i: (i, 0, 0))],
        out_specs=pl.BlockSpec((1, D, cols), lambda i: (i, 0, 0)),
        out_shape=jax.ShapeDtypeStruct((DEPTH, D, cols), f32), compiler_params=_cp(1),
    )(c_all, dmod)


def _row_tile(r):
    return _pick(r, (512, 352, 256, 128)) if r > 512 else r


def _sum_sources(v, name):
    n, r, c = v.shape
    tr = _row_tile(r)

    def body(v_ref, o_ref):
        acc = v_ref[0]
        for p in range(1, n):
            acc = acc + v_ref[p]
        o_ref[...] = acc.astype(f32)

    return pl.pallas_call(
        body, name=name, grid=(r // tr,),
        in_specs=[pl.BlockSpec((n, tr, c), lambda i: (0, i, 0))], out_specs=pl.BlockSpec((tr, c), lambda i: (i, 0)),
        out_shape=jax.ShapeDtypeStruct((r, c), f32), compiler_params=_cp(1),
    )(v)


def _adamw(parts, w, m, v, name):
    n, r, c = parts.shape
    tr = _row_tile(r)
    c1 = 1.0 - ADAM_B1 ** ADAM_STEP
    c2 = 1.0 - ADAM_B2 ** ADAM_STEP

    def body(p_ref, w_ref, m_ref, v_ref, g_ref, d_ref, mo_ref, vo_ref):
        g_v = p_ref[0].astype(f32)
        for p in range(1, n):
            g_v = g_v + p_ref[p].astype(f32)
        m_n = ADAM_B1 * m_ref[...] + (1.0 - ADAM_B1) * g_v
        v_n = ADAM_B2 * v_ref[...] + (1.0 - ADAM_B2) * (g_v * g_v)
        g_ref[...] = g_v
        d_ref[...] = -ADAM_LR * ((m_n / c1) / (jnp.sqrt(v_n / c2) + ADAM_EPS) + ADAM_WD * w_ref[...])
        mo_ref[...] = m_n
        vo_ref[...] = v_n

    spec = pl.BlockSpec((tr, c), lambda i: (i, 0))
    shp = jax.ShapeDtypeStruct((r, c), f32)
    return pl.pallas_call(
        body, name=name, grid=(r // tr,), in_specs=[pl.BlockSpec((n, tr, c), lambda i: (0, i, 0))] + [spec] * 3,
        out_specs=[spec] * 4, out_shape=[shp] * 4, compiler_params=_cp(1),
    )(parts, w, m, v)


def _two_d(shape):
    return (math.prod(shape[:-1]), shape[-1])


def _pack_rows(a):
    n = a.size
    rows = -(-n // (8 * PACK_C)) * 8
    return jnp.pad(a.reshape(-1), (0, rows * PACK_C - n)).reshape(rows, PACK_C)


def _pack(parts):
    return jnp.concatenate([_pack_rows(p) for p in parts], axis=0)


def _unpack(packed, shapes):
    lead = packed.shape[:-2]
    out, off = [], 0
    for s in shapes:
        n = math.prod(s)
        rows = -(-n // (8 * PACK_C)) * 8
        part = packed[..., off:off + rows, :].reshape(lead + (rows * PACK_C,))
        out.append(part[..., :n].reshape(lead + tuple(s)))
        off += rows
    return out


def _unshard(g8, axis):
    local = g8.shape[1:]
    moved = jnp.moveaxis(g8, 0, axis)
    return moved.reshape(local[:axis] + (N_DEV * local[axis],) + local[axis + 1:])


def _shard8(full, axis):
    s = full.shape
    split = full.reshape(s[:axis] + (N_DEV, s[axis] // N_DEV) + s[axis + 1:])
    return jnp.moveaxis(split, axis, 0)


_BIG = dict(ffn_w_in=3, ffn_w_out=2, pool_w=2, fox_w_in=2, fox_w_o=1, s5_w_glu=1, conv_w_in=2, conv_w_out=1)
_SMALL_SHARDED = dict(norm_g=2, s5_d=1, conv_w=3)
_REPLICATED = ("ada_b", "pool_scale", "fox_b_f", "fox_q_gain", "fox_k_gain", "s5_lam_re", "s5_lam_im", "s5_log_dt",
               "s5_b_re", "s5_b_im", "s5_c_re", "s5_c_im")
_WEIGHTS = ("ada_w", "ada_b", "norm_g", "ffn_w_in", "ffn_w_out", "pool_w", "pool_scale", "fox_w_in", "fox_b_f",
            "fox_q_gain", "fox_k_gain", "fox_w_o", "s5_lam_re", "s5_lam_im", "s5_log_dt", "s5_b_re", "s5_b_im",
            "s5_c_re", "s5_c_im", "s5_d", "s5_w_glu", "conv_w_in", "conv_w", "conv_w_out")


def _step(x, c, target, w, m, v):
    t = x.shape[1]
    xi_, yi_, ci_ = _me()
    me = _dev_index(xi_, yi_, ci_)

    sm_shapes = [w[n].shape for n in _SMALL_SHARDED]
    small_all = _all_gather([_pack([c] + [w[n] for n in _SMALL_SHARDED])], "gather_small")[0]
    gathered = _unpack(small_all, [c.shape] + sm_shapes)
    c_all = gathered[0][:, 0, :]
    full = {n: _unshard(p, ax) for (n, ax), p in zip(_SMALL_SHARDED.items(), gathered[1:])}

    big_all = _all_gather([w[n].astype(bf16).reshape(_two_d(w[n].shape)) for n in _BIG], "gather_weights")
    gw = dict(zip(_BIG, big_all))
    ffn_w_in = gw["ffn_w_in"].reshape(N_DEV, 2 * DEPTH, D, FFN_HS)
    ffn_w_out = gw["ffn_w_out"].reshape(N_DEV, 2 * DEPTH, D_FF // N_DEV, D)
    pool_w = gw["pool_w"].reshape(N_DEV, 4, POOL_GROUP // N_DEV, POOL_GROUP).transpose(1, 0, 2, 3)
    pool_w = pool_w.reshape(4, POOL_GROUP, POOL_GROUP)
    fox_w_in = jnp.pad(gw["fox_w_in"].transpose(1, 0, 2).reshape(D, FOX_PROJ), ((0, 0), (0, FOX_PROJ_PAD - FOX_PROJ)))
    fox_w_o, s5_w_glu, conv_w_out = (gw[n].reshape(D, D) for n in ("fox_w_o", "s5_w_glu", "conv_w_out"))
    conv_w_in = gw["conv_w_in"]

    def ffn_w(i, f):
        return ffn_w_in, ffn_w_out, 2 * i + f

    c16 = jnp.pad(c_all, ((0, 8), (0, 0)))
    cols = w["ada_w"].shape[2]
    mod_sh = _ada_mod(c16, w["ada_w"])
    mod_all = _all_gather([mod_sh.reshape(DEPTH * 16, cols)], "gather_mod")[0].reshape(N_DEV, DEPTH, 16, cols)
    mod_mine = lax.dynamic_index_in_dim(mod_all, me, axis=2, keepdims=False)
    mod = (mod_mine.transpose(1, 0, 2).reshape(DEPTH, N_DEV * cols) + w["ada_b"]).reshape(DEPTH, 3, 3, D)

    norm_g = full["norm_g"]

    def ln_of(i, sub):
        return (norm_g[i, sub][None], mod[i, sub, 0][None], mod[i, sub, 1][None], mod[i, sub, 2][None])

    fox_b_f = jnp.pad(w["fox_b_f"], ((0, 0), (0, 128 - FOX_HEADS)))
    s5_raw = _s5_params(w["s5_lam_re"][0], w["s5_lam_im"][0], w["s5_log_dt"][0], w["s5_b_re"][0], w["s5_b_im"][0])
    s5_c_re, s5_c_im = w["s5_c_re"][0], w["s5_c_im"][0]
    conv_w = jnp.pad(full["conv_w"][0, :, 0, :], ((0, 5), (0, 0)))

    xs = x[0]
    saved = []
    for i in range(DEPTH):
        xs, s0 = _ffn_fwd(xs, ln_of(i, 0), *ffn_w(i, 0), 0.5)
        if i == 0:
            s1 = xs
            xs = _pool_fwd(xs, ln_of(i, 1), pool_w, w["pool_scale"])
        elif i == 1:
            xs, s1 = _fox_fwd(xs, ln_of(i, 1), fox_w_in, fox_b_f, w["fox_q_gain"], w["fox_k_gain"], fox_w_o)
        elif i == 2:
            xs, s1 = _s5_fwd(xs, ln_of(i, 1), s5_raw, s5_c_re, s5_c_im, full["s5_d"], s5_w_glu)
        else:
            xs, s1 = _convmix_fwd(xs, ln_of(i, 1), conv_w_in, conv_w, conv_w_out)
        xs, s2 = _ffn_fwd(xs, ln_of(i, 2), *ffn_w(i, 1), 0.5)
        saved.append((s0, s1, s2))
    dx, lpart = _loss_head(xs, target[0])

    grads = {}
    dmod = [[None] * 3 for _ in range(DEPTH)]
    dnorm = [[None] * 3 for _ in range(DEPTH)]
    dffn_in = lax.empty(ffn_w_in.shape, f32)
    dffn_out = lax.empty(ffn_w_out.shape, f32)

    def put_ln(i, sub, dln):
        dg, dshift, dscale, dgate = dln
        dnorm[i][sub] = dg
        dmod[i][sub] = jnp.concatenate([dshift, dscale, dgate], axis=0)

    for i in reversed(range(DEPTH)):
        s0, s1, s2 = saved[i]
        dx, dln, dffn_in, dffn_out = _ffn_bwd(dx, s2, ln_of(i, 2), *ffn_w(i, 1), 0.5, dffn_in, dffn_out)
        put_ln(i, 2, dln)
        if i == 0:
            dx, dln, dpw, dps = _pool_bwd(dx, s1, ln_of(i, 1), pool_w, w["pool_scale"])
            dpw = dpw.reshape(4, N_DEV, POOL_GROUP // N_DEV, POOL_GROUP).transpose(1, 0, 2, 3)
            grads.update(pool_w=dpw.reshape(N_DEV, 4 * POOL_GROUP // N_DEV, POOL_GROUP), pool_scale=dps)
        elif i == 1:
            dx, dln, dwi, dbf, dqg, dkg, dwo = _fox_bwd(
                dx, s1, ln_of(i, 1), fox_w_in, fox_b_f, w["fox_q_gain"], w["fox_k_gain"], fox_w_o)
            dwi = dwi[:, :FOX_PROJ].reshape(D, N_DEV, FOX_PROJ // N_DEV).transpose(1, 0, 2)
            grads.update(fox_w_in=dwi, fox_b_f=dbf[:, :FOX_HEADS], fox_q_gain=dqg, fox_k_gain=dkg,
                         fox_w_o=dwo.reshape(N_DEV, D // N_DEV, D))
        elif i == 2:
            dx, dln, gs5 = _s5_bwd(dx, s1, ln_of(i, 1), s5_raw, full["s5_d"], s5_w_glu)
            gs5["s5_w_glu"] = gs5["s5_w_glu"].reshape(N_DEV, D // N_DEV, D)
            grads.update(gs5)
        else:
            dx, dln, dwi, dcw, dwo = _convmix_bwd(dx, s1, ln_of(i, 1), conv_w_in, conv_w, conv_w_out)
            grads.update(conv_w_in=dwi, conv_w=dcw[None, :, None, :], conv_w_out=dwo.reshape(N_DEV, D // N_DEV, D))
        put_ln(i, 1, dln)
        dx, dln, dffn_in, dffn_out = _ffn_bwd(dx, s0, ln_of(i, 0), *ffn_w(i, 0), 0.5, dffn_in, dffn_out)
        put_ln(i, 0, dln)
    grads["ffn_w_in"] = dffn_in.reshape(N_DEV, 2 * DEPTH * D, FFN_HS)
    grads["ffn_w_out"] = dffn_out.reshape(N_DEV, 2 * DEPTH * D_FF // N_DEV, D)
    grads["norm_g"] = jnp.stack([jnp.concatenate(r, axis=0) for r in dnorm])
    dmod_mine = jnp.stack([jnp.stack(r) for r in dmod]).reshape(DEPTH, 9 * D)

    small_names = list(_REPLICATED[1:]) + list(_SMALL_SHARDED)
    small_parts = [dmod_mine] + [grads[n] for n in small_names] + [lpart[:, 0:1]]
    small_g = _all_gather([_pack(small_parts)], "gather_grads")[0]
    small_sum = _sum_sources(small_g, "sum_small")
    summed = dict(zip(["ada_b"] + small_names + ["loss"], _unpack(small_sum, [p.shape for p in small_parts])))
    loss = summed.pop("loss")[0, 0]
    for n, ax in _SMALL_SHARDED.items():
        local = w[n].shape[ax]
        summed[n] = lax.dynamic_slice_in_dim(summed[n], me * local, local, axis=ax)

    dmod_all = small_g[:, :DEPTH * 9].reshape(N_DEV, DEPTH, 9 * D)
    dmod_cols = lax.dynamic_slice_in_dim(dmod_all, me * cols, cols, axis=2)
    ada_g = _ada_grad(c16, jnp.pad(dmod_cols.transpose(1, 0, 2), ((0, 0), (0, 8), (0, 0))))

    big_parts = [grads[n] for n in _BIG]
    from_sibling = _exchange_pair(big_parts, "exchange_pair")
    chip_sums = [_pair_sum(p, s) for p, s in zip(big_parts, from_sibling)]
    big_landed = dict(zip(_BIG, _exchange_chips(chip_sums, "exchange_chips")))

    grad, delta, new_m, new_v = {}, {}, {}, {}
    big_landed["ada_w"] = ada_g[None]
    for n, parts in big_landed.items():
        view = _two_d(w[n].shape)
        outs = _adamw(parts.reshape((parts.shape[0],) + view), w[n].reshape(view), m[n].reshape(view),
                      v[n].reshape(view), "adamw_" + n)
        grad[n], delta[n], new_m[n], new_v[n] = (a.reshape(w[n].shape) for a in outs)
    small = [n for n in _WEIGHTS if n not in big_landed]
    small_shapes = [w[n].shape for n in small]
    pk = lambda d: _pack([d[n] for n in small])
    outs = _adamw(pk(summed)[None], pk(w), pk(m), pk(v), "adamw_small")
    for dst, packed in zip((grad, delta, new_m, new_v), outs):
        dst.update(zip(small, _unpack(packed, small_shapes)))
    return (loss, dx[None], *[grad[n] for n in _WEIGHTS], *[delta[n] for n in _WEIGHTS],
            *[new_m[n] for n in _WEIGHTS], *[new_v[n] for n in _WEIGHTS])


def kernel(x, c, ada_w, ada_b, norm_g, ffn_w_in, ffn_w_out, pool_w, pool_scale, fox_w_in, fox_b_f, fox_q_gain, fox_k_gain, fox_w_o, s5_lam_re, s5_lam_im, s5_log_dt, s5_b_re, s5_b_im, s5_c_re, s5_c_im, s5_d, s5_w_glu, conv_w_in, conv_w, conv_w_out, loss_target, m_ada_w, m_ada_b, m_norm_g, m_ffn_w_in, m_ffn_w_out, m_pool_w, m_pool_scale, m_fox_w_in, m_fox_b_f, m_fox_q_gain, m_fox_k_gain, m_fox_w_o, m_s5_lam_re, m_s5_lam_im, m_s5_log_dt, m_s5_b_re, m_s5_b_im, m_s5_c_re, m_s5_c_im, m_s5_d, m_s5_w_glu, m_conv_w_in, m_conv_w, m_conv_w_out, v_ada_w, v_ada_b, v_norm_g, v_ffn_w_in, v_ffn_w_out, v_pool_w, v_pool_scale, v_fox_w_in, v_fox_b_f, v_fox_q_gain, v_fox_k_gain, v_fox_w_o, v_s5_lam_re, v_s5_lam_im, v_s5_log_dt, v_s5_b_re, v_s5_b_im, v_s5_c_re, v_s5_c_im, v_s5_d, v_s5_w_glu, v_conv_w_in, v_conv_w, v_conv_w_out):
    ws = (ada_w, ada_b, norm_g, ffn_w_in, ffn_w_out, pool_w, pool_scale, fox_w_in, fox_b_f, fox_q_gain, fox_k_gain,
          fox_w_o, s5_lam_re, s5_lam_im, s5_log_dt, s5_b_re, s5_b_im, s5_c_re, s5_c_im, s5_d, s5_w_glu, conv_w_in,
          conv_w, conv_w_out)
    ms = (m_ada_w, m_ada_b, m_norm_g, m_ffn_w_in, m_ffn_w_out, m_pool_w, m_pool_scale, m_fox_w_in, m_fox_b_f,
          m_fox_q_gain, m_fox_k_gain, m_fox_w_o, m_s5_lam_re, m_s5_lam_im, m_s5_log_dt, m_s5_b_re, m_s5_b_im,
          m_s5_c_re, m_s5_c_im, m_s5_d, m_s5_w_glu, m_conv_w_in, m_conv_w, m_conv_w_out)
    vs = (v_ada_w, v_ada_b, v_norm_g, v_ffn_w_in, v_ffn_w_out, v_pool_w, v_pool_scale, v_fox_w_in, v_fox_b_f,
          v_fox_q_gain, v_fox_k_gain, v_fox_w_o, v_s5_lam_re, v_s5_lam_im, v_s5_log_dt, v_s5_b_re, v_s5_b_im,
          v_s5_c_re, v_s5_c_im, v_s5_d, v_s5_w_glu, v_conv_w_in, v_conv_w, v_conv_w_out)
    return _step(x, c, loss_target, dict(zip(_WEIGHTS, ws)), dict(zip(_WEIGHTS, ms)), dict(zip(_WEIGHTS, vs)))
```

```python
import math

import jax
import jax.numpy as jnp
from jax import lax
from jax.experimental import pallas as pl
from jax.experimental.pallas import tpu as pltpu

f32 = jnp.float32
bf16 = jnp.bfloat16

D = 1024
D_FF = 2816
FFN_HS = 2 * D_FF // 8
FFN_SLABS = 4
DEPTH = 4
NORM_EPS = 1e-6
N_DEV = 8
AXES = ("x", "y", "c")
POOL_WINDOWS = (2, 4, 8, 16)
POOL_GROUP = 256
POOL_HALO = 16
FOX_HEADS = 16
FOX_HEAD_DIM = 64
FOX_PROJ = 3088
FOX_PROJ_PAD = 3200
S5_GROUPS = 64
S5_GROUP = 16
S5_STATE = 64
S5_NSTATE = S5_GROUPS * S5_STATE
S5_BLOCKS = 8
S5_BCH = 128
S5_BST = 512
CONV_HALO = 8
ADAM_LR = 0.001
ADAM_B1 = 0.9
ADAM_B2 = 0.999
ADAM_EPS = 1e-08
ADAM_WD = 0.01
ADAM_STEP = 10
VMEM_LIMIT = 56 * 1024 * 1024
PACK_C = 1024

_ARB = "arbitrary"


def _cp(n_axes):
    return pltpu.CompilerParams(dimension_semantics=(_ARB,) * n_axes, vmem_limit_bytes=VMEM_LIMIT)


def _pick(n, prefs):
    for c in prefs:
        if n % c == 0:
            return c
    return n


_DN = {"nn": (((1,), (0,)), ((), ())), "nt": (((1,), (1,)), ((), ())), "tn": (((0,), (0,)), ((), ()))}


def _mm(a, b, mode, out_dtype, name):
    if mode == "nn":
        (m, k), (_, n) = a.shape, b.shape
    elif mode == "nt":
        (m, k), (n, _) = a.shape, b.shape
    else:
        (k, m), (_, n) = a.shape, b.shape
    big = (1408, 1024, 640, 512, 384, 256, 128)
    tm = _pick(m, big) if mode == "tn" else _pick(m, (1024, 512, 256, 128))
    tn = _pick(n, big)
    if mode == "tn":
        tk = _pick(k, (512, 256, 128))
    else:
        tk = k if k <= 3200 else _pick(k, (2816, 2048, 1024, 512))
    nk = k // tk
    dn = _DN[mode]

    def body(a_ref, b_ref, o_ref, acc_ref):
        p = lax.dot_general(a_ref[...], b_ref[...], dn, preferred_element_type=f32)
        if nk == 1:
            o_ref[...] = p.astype(out_dtype)
        else:
            kk = pl.program_id(2)

            @pl.when(kk == 0)
            def _():
                acc_ref[...] = p

            @pl.when(kk > 0)
            def _():
                acc_ref[...] += p

            @pl.when(kk == nk - 1)
            def _():
                o_ref[...] = acc_ref[...].astype(out_dtype)

    if mode == "nn":
        a_spec = pl.BlockSpec((tm, tk), lambda i, j, kk: (i, kk))
        b_spec = pl.BlockSpec((tk, tn), lambda i, j, kk: (kk, j))
    elif mode == "nt":
        a_spec = pl.BlockSpec((tm, tk), lambda i, j, kk: (i, kk))
        b_spec = pl.BlockSpec((tn, tk), lambda i, j, kk: (j, kk))
    else:
        a_spec = pl.BlockSpec((tk, tm), lambda i, j, kk: (kk, i))
        b_spec = pl.BlockSpec((tk, tn), lambda i, j, kk: (kk, j))
    acc_shape = (tm, tn) if nk > 1 else (8, 128)
    return pl.pallas_call(
        body, name=name, grid=(m // tm, n // tn, nk),
        in_specs=[a_spec, b_spec], out_specs=pl.BlockSpec((tm, tn), lambda i, j, kk: (i, j)),
        out_shape=jax.ShapeDtypeStruct((m, n), out_dtype),
        scratch_shapes=[pltpu.VMEM(acc_shape, f32)],
        compiler_params=_cp(3),
    )(a, b)


def _mmx(a, b, mode, name, grid, a_spec, b_spec, o_spec, out_shape, into=None):
    nk = grid[2]
    dn = _DN[mode]
    out_dtype = out_shape.dtype
    a_blk = (math.prod(a_spec.block_shape[:-1]), a_spec.block_shape[-1])
    b_blk = (math.prod(b_spec.block_shape[:-1]), b_spec.block_shape[-1])
    o_blk = (math.prod(o_spec.block_shape[:-1]), o_spec.block_shape[-1])

    def body(a_ref, b_ref, *rest):
        o_ref, acc_ref = rest[-2:]
        p = lax.dot_general(a_ref[...].reshape(a_blk), b_ref[...].reshape(b_blk), dn, preferred_element_type=f32)
        if nk == 1:
            o_ref[...] = p.reshape(o_ref.shape).astype(out_dtype)
        else:
            kk = pl.program_id(2)

            @pl.when(kk == 0)
            def _():
                acc_ref[...] = p

            @pl.when(kk > 0)
            def _():
                acc_ref[...] += p

            @pl.when(kk == nk - 1)
            def _():
                o_ref[...] = acc_ref[...].reshape(o_ref.shape).astype(out_dtype)

    extra = {} if into is None else dict(input_output_aliases={2: 0})
    operands = (a, b) if into is None else (a, b, into)
    return pl.pallas_call(
        body, name=name, grid=grid, in_specs=[a_spec, b_spec] + ([] if into is None else [_ANY]),
        out_specs=o_spec, out_shape=out_shape,
        scratch_shapes=[pltpu.VMEM(o_blk if nk > 1 else (8, 128), f32)], compiler_params=_cp(3), **extra,
    )(*operands)


def _tok_tile(t):
    return min(t, 512)


def _tok_spec(tm, c, nt=None, reverse=False):
    if reverse:
        return pl.BlockSpec((tm, c), lambda i: (nt - 1 - i, 0))
    return pl.BlockSpec((tm, c), lambda i: (i, 0))


def _row_spec(c, rows=1):
    return pl.BlockSpec((rows, c), lambda i: (0, 0))


def _acc_add(first, ref, val):
    @pl.when(first)
    def _():
        ref[...] = val

    @pl.when(jnp.logical_not(first))
    def _():
        ref[...] += val


def _adaln(x, g, scale, shift):
    y = x * lax.rsqrt(jnp.mean(x * x, axis=-1, keepdims=True) + NORM_EPS)
    return (y * g) * (1.0 + scale) + shift


def _ln_fwd(x, g, scale, shift):
    t = x.shape[0]
    tm = _tok_tile(t)

    def body(x_ref, g_ref, sc_ref, sh_ref, h_ref):
        h_ref[...] = _adaln(x_ref[...], g_ref[...], sc_ref[...], sh_ref[...]).astype(bf16)

    return pl.pallas_call(
        body, name="ln_fwd", grid=(t // tm,),
        in_specs=[_tok_spec(tm, D), _row_spec(D), _row_spec(D), _row_spec(D)],
        out_specs=_tok_spec(tm, D), out_shape=jax.ShapeDtypeStruct((t, D), bf16),
        compiler_params=_cp(1),
    )(x, g, scale, shift)


def _ln_bwd(x, g, scale, shift, dh, dxn):
    t = x.shape[0]
    tm = _tok_tile(t)

    def body(x_ref, g_ref, sc_ref, sh_ref, dh_ref, dxn_ref, dx_ref, dg_ref, dsc_ref, dsh_ref):
        _, vjp = jax.vjp(_adaln, x_ref[...], g_ref[...], sc_ref[...], sh_ref[...])
        dx, dg, dsc, dsh = vjp(dh_ref[...])
        dx_ref[...] = dxn_ref[...] + dx
        first = pl.program_id(0) == 0
        _acc_add(first, dg_ref, dg)
        _acc_add(first, dsc_ref, dsc)
        _acc_add(first, dsh_ref, dsh)

    row = jax.ShapeDtypeStruct((1, D), f32)
    return pl.pallas_call(
        body, name="ln_bwd", grid=(t // tm,),
        in_specs=[_tok_spec(tm, D), _row_spec(D), _row_spec(D), _row_spec(D), _tok_spec(tm, D), _tok_spec(tm, D)],
        out_specs=[_tok_spec(tm, D), _row_spec(D), _row_spec(D), _row_spec(D)],
        out_shape=[jax.ShapeDtypeStruct((t, D), f32), row, row, row],
        compiler_params=_cp(1),
    )(x, g, scale, shift, dh, dxn)


def _swiglu(g, u):
    return jax.nn.silu(g) * u


def _ffn_tile(t):
    return min(t, 512)


def _ffn_specs(tm, idx, rows):
    return (pl.BlockSpec((2, 1, 1, D, FFN_HS), lambda i, q: (0, q, idx, 0, 0)),
            pl.BlockSpec((2, 1, rows, D), lambda i, q: (q, idx, 0, 0)),
            pl.BlockSpec((2, 1, tm, FFN_HS), lambda i, q: (0, q, i, 0)),
            pl.BlockSpec((tm, D), lambda i, q: (i, 0)),
            pl.BlockSpec((1, D), lambda i, q: (0, 0)))


def _ffn_up(x, h, gate, w_in, w_out, idx, coef):
    t = h.shape[0]
    tm = _ffn_tile(t)
    rows = w_out.shape[2]

    def body(h_ref, wi_ref, wo_ref, x_ref, gt_ref, gu_ref, a_ref, o_ref, xn_ref):
        q = pl.program_id(1)
        hv = h_ref[...]
        g = jnp.dot(hv, wi_ref[0, 0, 0], preferred_element_type=f32)
        u = jnp.dot(hv, wi_ref[1, 0, 0], preferred_element_type=f32)
        gu_ref[0, 0] = g.astype(bf16)
        gu_ref[1, 0] = u.astype(bf16)
        a = _swiglu(g, u).astype(bf16)
        a_ref[0] = a
        _acc_add(q == 0, o_ref, jnp.dot(a, wo_ref[...].reshape(FFN_HS, D), preferred_element_type=f32))

        @pl.when(q == FFN_SLABS - 1)
        def _():
            xn_ref[...] = x_ref[...] + (coef * (1.0 + gt_ref[...])) * o_ref[...]

    wi, wo, slab, tok, row = _ffn_specs(tm, idx, rows)
    tok_f32 = jax.ShapeDtypeStruct((t, D), f32)
    return pl.pallas_call(
        body, name="ffn_up", grid=(t // tm, FFN_SLABS), in_specs=[tok, wi, wo, tok, row],
        out_specs=[slab, pl.BlockSpec((1, tm, FFN_HS), lambda i, q: (q, i, 0)), tok, tok],
        out_shape=[jax.ShapeDtypeStruct((2, FFN_SLABS, t, FFN_HS), bf16),
                   jax.ShapeDtypeStruct((FFN_SLABS, t, FFN_HS), bf16), tok_f32, tok_f32],
        compiler_params=_cp(2),
    )(h, w_in, w_out, x, gate)


def _ffn_down(dxn, o, gate, gu, w_out, idx, coef):
    t = dxn.shape[0]
    tm = _ffn_tile(t)
    rows = w_out.shape[2]

    def body(dxn_ref, o_ref, gt_ref, gu_ref, wo_ref, do_ref, dgt_ref, dgu_ref):
        i, q = pl.program_id(0), pl.program_id(1)

        @pl.when(q == 0)
        def _():
            dxn_v = dxn_ref[...]
            do_ref[...] = ((coef * (1.0 + gt_ref[...])) * dxn_v).astype(bf16)
            _acc_add(i == 0, dgt_ref, coef * jnp.sum(dxn_v * o_ref[...], axis=0, keepdims=True))

        da = lax.dot_general(do_ref[...], wo_ref[...].reshape(FFN_HS, D), _DN["nt"], preferred_element_type=f32)
        _, vjp = jax.vjp(_swiglu, gu_ref[0, 0].astype(f32), gu_ref[1, 0].astype(f32))
        dg, du = vjp(da)
        dgu_ref[0, 0] = dg.astype(bf16)
        dgu_ref[1, 0] = du.astype(bf16)

    _, wo, slab, tok, row = _ffn_specs(tm, idx, rows)
    return pl.pallas_call(
        body, name="ffn_down", grid=(t // tm, FFN_SLABS), in_specs=[tok, tok, row, slab, wo],
        out_specs=[tok, row, slab],
        out_shape=[jax.ShapeDtypeStruct((t, D), bf16), jax.ShapeDtypeStruct((1, D), f32),
                   jax.ShapeDtypeStruct((2, FFN_SLABS, t, FFN_HS), bf16)],
        compiler_params=_cp(2),
    )(dxn, o, gate, gu, w_out)


def _resid_fwd(x, o, gate, coef):
    t = x.shape[0]
    tm = _tok_tile(t)

    def body(x_ref, o_ref, gt_ref, y_ref):
        y_ref[...] = x_ref[...] + (coef * (1.0 + gt_ref[...])) * o_ref[...]

    return pl.pallas_call(
        body, name="resid_fwd", grid=(t // tm,),
        in_specs=[_tok_spec(tm, D), _tok_spec(tm, D), _row_spec(D)],
        out_specs=_tok_spec(tm, D), out_shape=jax.ShapeDtypeStruct((t, D), f32),
        compiler_params=_cp(1),
    )(x, o, gate)


def _resid_bwd(dxn, o, gate, coef):
    t = dxn.shape[0]
    tm = _tok_tile(t)

    def body(dxn_ref, o_ref, gt_ref, do_ref, dgt_ref):
        dxn_v = dxn_ref[...]
        do_ref[...] = ((coef * (1.0 + gt_ref[...])) * dxn_v).astype(bf16)
        _acc_add(pl.program_id(0) == 0, dgt_ref, coef * jnp.sum(dxn_v * o_ref[...], axis=0, keepdims=True))

    return pl.pallas_call(
        body, name="resid_bwd", grid=(t // tm,),
        in_specs=[_tok_spec(tm, D), _tok_spec(tm, D), _row_spec(D)],
        out_specs=[_tok_spec(tm, D), _row_spec(D)],
        out_shape=[jax.ShapeDtypeStruct((t, D), bf16), jax.ShapeDtypeStruct((1, D), f32)],
        compiler_params=_cp(1),
    )(dxn, o, gate)


def _loss_head(y, target):
    t = y.shape[0]
    tm = _tok_tile(t)

    def body(y_ref, t_ref, dy_ref, l_ref):
        err = y_ref[...] - t_ref[...]
        dy_ref[...] = err * (1.0 / D)
        part = jnp.sum(jnp.sum(err * err, axis=0, keepdims=True), axis=1, keepdims=True) * (0.5 / D)
        _acc_add(pl.program_id(0) == 0, l_ref, jnp.broadcast_to(part, (1, 128)))

    return pl.pallas_call(
        body, name="loss_head", grid=(t // tm,),
        in_specs=[_tok_spec(tm, D), _tok_spec(tm, D)],
        out_specs=[_tok_spec(tm, D), _row_spec(128)],
        out_shape=[jax.ShapeDtypeStruct((t, D), f32), jax.ShapeDtypeStruct((1, 128), f32)],
        compiler_params=_cp(1),
    )(y, target)


def _mm_tile(t):
    return min(t, 1024)


def _ffn_fwd(x, ln, w_in, w_out, idx, coef):
    g, shift, scale, gate = ln
    h = _ln_fwd(x, g, scale, shift)
    gu, a, o, xn = _ffn_up(x, h, gate, w_in.reshape((2, FFN_SLABS) + w_in.shape[1:]), w_out, idx, coef)
    return xn, (x, h, gu, a, o)


def _ffn_bwd(dxn, saved, ln, w_in, w_out, idx, coef, dw_in, dw_out):
    x, h, gu, a, o = saved
    g, shift, scale, gate = ln
    t = x.shape[0]
    tm = _mm_tile(t)
    tk = _pick(t, (2048, 1024, 512, 256, 128))
    rows = w_out.shape[2]
    sds = jax.ShapeDtypeStruct
    do, dgate, dgu = _ffn_down(dxn, o, gate, gu, w_out, idx, coef)
    dgu = dgu.reshape(N_DEV, t, FFN_HS)
    dh = _mmx(dgu, w_in, "nt", "ffn_dh", (t // tm, 1, N_DEV),
              pl.BlockSpec((1, tm, FFN_HS), lambda i, j, k: (k, i, 0)),
              pl.BlockSpec((1, 1, D, FFN_HS), lambda i, j, k: (k, idx, 0, 0)),
              pl.BlockSpec((tm, D), lambda i, j, k: (i, 0)), sds((t, D), f32))
    dw_out = _mmx(a, do, "tn", "ffn_dwout", (FFN_SLABS, 1, t // tk),
                  pl.BlockSpec((1, tk, FFN_HS), lambda i, j, k: (i, k, 0)),
                  pl.BlockSpec((tk, D), lambda i, j, k: (k, 0)),
                  pl.BlockSpec((2, 1, rows, D), lambda i, j, k: (i, idx, 0, 0)),
                  sds(dw_out.shape, f32), into=dw_out)
    dw_in = _mmx(h, dgu, "tn", "ffn_dwin", (N_DEV, 1, t // tk),
                 pl.BlockSpec((tk, D), lambda i, j, k: (k, 0)),
                 pl.BlockSpec((1, tk, FFN_HS), lambda i, j, k: (i, k, 0)),
                 pl.BlockSpec((1, 1, D, FFN_HS), lambda i, j, k: (i, idx, 0, 0)),
                 sds(dw_in.shape, f32), into=dw_in)
    dx, dg, dscale, dshift = _ln_bwd(x, g, scale, shift, dh, dxn)
    return dx, (dg, dshift, dscale, dgate), dw_in, dw_out


def _roll_rows(a, k):
    n = a.shape[0]
    return pltpu.roll(a, k % n, 0)


def _pool_windows(hx, first_row, reverse):
    outs = []
    for gi, w in enumerate(POOL_WINDOWS):
        acc = hx[:, gi * POOL_GROUP:(gi + 1) * POOL_GROUP]
        k = 1
        while k < w:
            acc = acc + _roll_rows(acc, -k if reverse else k)
            k *= 2
        outs.append(acc)
    return outs


def _pool_cnt(t_idx, w):
    return jnp.minimum(t_idx + 1, w).astype(f32)


def _pool_pooled(x_ref, xp_ref, g, scale, shift, i, tm):
    h = _adaln(x_ref[...], g, scale, shift)
    hp = _adaln(xp_ref[...], g, scale, shift)
    hp = jnp.where(i == 0, 0.0, hp)
    hx = jnp.concatenate([hp, h], axis=0)
    sums = _pool_windows(hx, 0, False)
    t_idx = i * tm + lax.broadcasted_iota(jnp.int32, (tm, 1), 0)
    pooled = []
    for gi, w in enumerate(POOL_WINDOWS):
        s = sums[gi][POOL_HALO:, :]
        pooled.append(s / _pool_cnt(t_idx, w) - h[:, gi * POOL_GROUP:(gi + 1) * POOL_GROUP])
    return h, pooled


def _pool_specs(t, tm):
    per = tm // POOL_HALO
    prev = pl.BlockSpec((POOL_HALO, D), lambda i: (jnp.maximum(i * per - 1, 0), 0))
    return [_tok_spec(tm, D), prev, _row_spec(D), _row_spec(D), _row_spec(D),
            pl.BlockSpec((4, POOL_GROUP, POOL_GROUP), lambda i: (0, 0, 0)), _row_spec(D), _row_spec(D)]


def _pool_fwd(x, ln, w, pscale):
    g, shift, scale, gate = ln
    t = x.shape[0]
    tm = _tok_tile(t)

    def body(x_ref, xp_ref, g_ref, sc_ref, sh_ref, w_ref, ps_ref, gt_ref, y_ref):
        i = pl.program_id(0)
        _, pooled = _pool_pooled(x_ref, xp_ref, g_ref[...], sc_ref[...], sh_ref[...], i, tm)
        mixed = [jnp.dot(pooled[gi].astype(bf16), w_ref[gi], preferred_element_type=f32) for gi in range(4)]
        y = jnp.concatenate(mixed, axis=1) * ps_ref[...]
        y_ref[...] = x_ref[...] + (1.0 + gt_ref[...]) * y

    return pl.pallas_call(
        body, name="pool_fwd", grid=(t // tm,), in_specs=_pool_specs(t, tm),
        out_specs=_tok_spec(tm, D), out_shape=jax.ShapeDtypeStruct((t, D), f32),
        compiler_params=_cp(1),
    )(x, x, g, scale, shift, w, pscale, gate)


def _pool_bwd(dxn, x, ln, w, pscale):
    g, shift, scale, gate = ln
    t = x.shape[0]
    tm = _tok_tile(t)
    nt = t // tm
    per = tm // POOL_HALO

    def body_a(x_ref, xp_ref, g_ref, sc_ref, sh_ref, w_ref, ps_ref, gt_ref, dxn_ref,
               dp_ref, dw_ref, dps_ref, dgt_ref):
        i = pl.program_id(0)
        first = i == 0
        _, pooled = _pool_pooled(x_ref, xp_ref, g_ref[...], sc_ref[...], sh_ref[...], i, tm)
        dxn_v = dxn_ref[...]
        dy = (1.0 + gt_ref[...]) * dxn_v
        dmixed = dy * ps_ref[...]
        mixed, dps = [], []
        for gi in range(4):
            sl = slice(gi * POOL_GROUP, (gi + 1) * POOL_GROUP)
            pb = pooled[gi].astype(bf16)
            dmb = dmixed[:, sl].astype(bf16)
            mixed.append(jnp.dot(pb, w_ref[gi], preferred_element_type=f32))
            dp_ref[:, sl] = lax.dot_general(dmb, w_ref[gi], _DN["nt"], preferred_element_type=f32)
            dwg = lax.dot_general(pb, dmb, _DN["tn"], preferred_element_type=f32)

            @pl.when(first)
            def _():
                dw_ref[gi] = dwg

            @pl.when(jnp.logical_not(first))
            def _():
                dw_ref[gi] += dwg
        mixed = jnp.concatenate(mixed, axis=1)
        _acc_add(first, dps_ref, jnp.sum(dy * mixed, axis=0, keepdims=True))
        _acc_add(first, dgt_ref, jnp.sum(dxn_v * (mixed * ps_ref[...]), axis=0, keepdims=True))

    row = jax.ShapeDtypeStruct((1, D), f32)
    dpooled, dw, dps, dgate = pl.pallas_call(
        body_a, name="pool_bwd_a", grid=(nt,), in_specs=_pool_specs(t, tm) + [_tok_spec(tm, D)],
        out_specs=[_tok_spec(tm, D), pl.BlockSpec((4, POOL_GROUP, POOL_GROUP), lambda i: (0, 0, 0)),
                   _row_spec(D), _row_spec(D)],
        out_shape=[jax.ShapeDtypeStruct((t, D), f32), jax.ShapeDtypeStruct((4, POOL_GROUP, POOL_GROUP), f32), row, row],
        compiler_params=_cp(1),
    )(x, x, g, scale, shift, w, pscale, gate, dxn)

    def body_b(dp_ref, dpn_ref, x_ref, g_ref, sc_ref, sh_ref, dxn_ref, dx_ref, dg_ref, dsc_ref, dsh_ref):
        i = pl.program_id(0)
        dp = dp_ref[...]
        dpn = jnp.where(i == nt - 1, 0.0, dpn_ref[...])
        t_idx = i * tm + lax.broadcasted_iota(jnp.int32, (tm + POOL_HALO, 1), 0)
        ex = jnp.concatenate([dp, dpn], axis=0)
        parts = []
        for gi, w_ in enumerate(POOL_WINDOWS):
            parts.append(ex[:, gi * POOL_GROUP:(gi + 1) * POOL_GROUP] / _pool_cnt(t_idx, w_))
        sums = _pool_windows(jnp.concatenate(parts, axis=1), 0, True)
        dh = jnp.concatenate([s[:tm, :] for s in sums], axis=1) - dp
        _, vjp = jax.vjp(_adaln, x_ref[...], g_ref[...], sc_ref[...], sh_ref[...])
        dx, dg, dsc, dsh = vjp(dh)
        dx_ref[...] = dxn_ref[...] + dx
        first = i == 0
        _acc_add(first, dg_ref, dg)
        _acc_add(first, dsc_ref, dsc)
        _acc_add(first, dsh_ref, dsh)

    nxt = pl.BlockSpec((POOL_HALO, D), lambda i: (jnp.minimum((i + 1) * per, t // POOL_HALO - 1), 0))
    dx, dg, dscale, dshift = pl.pallas_call(
        body_b, name="pool_bwd_b", grid=(nt,),
        in_specs=[_tok_spec(tm, D), nxt, _tok_spec(tm, D), _row_spec(D), _row_spec(D), _row_spec(D), _tok_spec(tm, D)],
        out_specs=[_tok_spec(tm, D), _row_spec(D), _row_spec(D), _row_spec(D)],
        out_shape=[jax.ShapeDtypeStruct((t, D), f32), row, row, row],
        compiler_params=_cp(1),
    )(dpooled, dpooled, x, g, scale, shift, dxn)
    return dx, (dg, dshift, dscale, dgate), dw, dps


def _conv_taps(czx, cw):
    return cw[0:1, :] * _roll_rows(czx, 2) + cw[1:2, :] * _roll_rows(czx, 1) + cw[2:3, :] * czx


def _conv_fwd(p, cw):
    t = p.shape[0]
    tm = _tok_tile(t)
    per = tm // CONV_HALO

    def body(p_ref, pp_ref, cw_ref, q_ref):
        i = pl.program_id(0)
        cz = p_ref[:, D:2 * D] * p_ref[:, 2 * D:3 * D]
        czp = jnp.where(i == 0, 0.0, pp_ref[:, D:2 * D] * pp_ref[:, 2 * D:3 * D])
        conv = _conv_taps(jnp.concatenate([czp, cz], axis=0), cw_ref[...])[CONV_HALO:, :]
        q_ref[...] = (p_ref[:, 0:D] * conv).astype(bf16)

    prev = pl.BlockSpec((CONV_HALO, 3 * D), lambda i: (jnp.maximum(i * per - 1, 0), 0))
    return pl.pallas_call(
        body, name="conv_fwd", grid=(t // tm,),
        in_specs=[_tok_spec(tm, 3 * D), prev, _row_spec(D, 8)],
        out_specs=_tok_spec(tm, D), out_shape=jax.ShapeDtypeStruct((t, D), bf16),
        compiler_params=_cp(1),
    )(p, p, cw)


def _conv_bwd(p, cw, dq):
    t = p.shape[0]
    tm = _tok_tile(t)
    nt = t // tm
    per = tm // CONV_HALO

    def body(p_ref, pp_ref, pn_ref, cw_ref, dq_ref, dqn_ref, dp_ref, dcw_ref):
        i = pl.program_id(0)
        cw_v = cw_ref[...]
        b, c, z = p_ref[:, 0:D], p_ref[:, D:2 * D], p_ref[:, 2 * D:3 * D]
        cz = c * z
        czp = jnp.where(i == 0, 0.0, pp_ref[:, D:2 * D] * pp_ref[:, 2 * D:3 * D])
        czx = jnp.concatenate([czp, cz], axis=0)
        conv = _conv_taps(czx, cw_v)[CONV_HALO:, :]
        dq_v = dq_ref[...]
        dconv = dq_v * b
        dconv_n = jnp.where(i == nt - 1, 0.0, dqn_ref[...] * pn_ref[:, 0:D])
        dcx = jnp.concatenate([dconv, dconv_n], axis=0)
        dcz = (cw_v[2:3, :] * dcx + cw_v[1:2, :] * _roll_rows(dcx, -1) + cw_v[0:1, :] * _roll_rows(dcx, -2))[:tm, :]
        dp_ref[:, 0:D] = (dq_v * conv).astype(bf16)
        dp_ref[:, D:2 * D] = (dcz * z).astype(bf16)
        dp_ref[:, 2 * D:3 * D] = (dcz * c).astype(bf16)
        dw2 = jnp.sum(dconv * cz, axis=0, keepdims=True)
        dw1 = jnp.sum(dconv * _roll_rows(czx, 1)[CONV_HALO:, :], axis=0, keepdims=True)
        dw0 = jnp.sum(dconv * _roll_rows(czx, 2)[CONV_HALO:, :], axis=0, keepdims=True)
        _acc_add(i == 0, dcw_ref, jnp.concatenate([dw0, dw1, dw2, jnp.zeros((5, D), f32)], axis=0))

    prev = pl.BlockSpec((CONV_HALO, 3 * D), lambda i: (jnp.maximum(i * per - 1, 0), 0))
    last = t // CONV_HALO - 1
    nxt3 = pl.BlockSpec((CONV_HALO, 3 * D), lambda i: (jnp.minimum((i + 1) * per, last), 0))
    nxt1 = pl.BlockSpec((CONV_HALO, D), lambda i: (jnp.minimum((i + 1) * per, last), 0))
    return pl.pallas_call(
        body, name="conv_bwd", grid=(nt,),
        in_specs=[_tok_spec(tm, 3 * D), prev, nxt3, _row_spec(D, 8), _tok_spec(tm, D), nxt1],
        out_specs=[_tok_spec(tm, 3 * D), _row_spec(D, 8)],
        out_shape=[jax.ShapeDtypeStruct((t, 3 * D), bf16), jax.ShapeDtypeStruct((8, D), f32)],
        compiler_params=_cp(1),
    )(p, p, p, cw, dq, dq)


def _convmix_fwd(x, ln, w_in, cw, w_out):
    g, shift, scale, gate = ln
    t = x.shape[0]
    tm = _mm_tile(t)
    cs = w_in.shape[2]
    h = _ln_fwd(x, g, scale, shift)
    p = _mmx(h, w_in, "nn", "conv_in", (t // tm, N_DEV, 1),
             pl.BlockSpec((tm, D), lambda i, j, k: (i, 0)),
             pl.BlockSpec((1, D, cs), lambda i, j, k: (j, 0, 0)),
             pl.BlockSpec((tm, cs), lambda i, j, k: (i, j)), jax.ShapeDtypeStruct((t, N_DEV * cs), f32))
    q = _conv_fwd(p, cw)
    y = _mm(q, w_out, "nn", f32, "conv_out")
    return _resid_fwd(x, y, gate, 1.0), (x, h, p, q, y)


def _convmix_bwd(dxn, saved, ln, w_in, cw, w_out):
    x, h, p, q, y = saved
    g, shift, scale, gate = ln
    dy, dgate = _resid_bwd(dxn, y, gate, 1.0)
    dq = _mm(dy, w_out, "nt", f32, "conv_dq")
    dw_out = _mm(q, dy, "tn", f32, "conv_dwout")
    dp, dcw = _conv_bwd(p, cw, dq)
    t = x.shape[0]
    tm = _mm_tile(t)
    tk = _pick(t, (512, 256, 128))
    cs = w_in.shape[2]
    dh = _mmx(dp, w_in, "nt", "conv_dh", (t // tm, 1, N_DEV),
              pl.BlockSpec((tm, cs), lambda i, j, k: (i, k)),
              pl.BlockSpec((1, D, cs), lambda i, j, k: (k, 0, 0)),
              pl.BlockSpec((tm, D), lambda i, j, k: (i, 0)), jax.ShapeDtypeStruct((t, D), f32))
    dw_in = _mmx(h, dp, "tn", "conv_dwin", (N_DEV, 1, t // tk),
                 pl.BlockSpec((tk, D), lambda i, j, k: (k, 0)),
                 pl.BlockSpec((tk, cs), lambda i, j, k: (k, i)),
                 pl.BlockSpec((1, D, cs), lambda i, j, k: (i, 0, 0)), jax.ShapeDtypeStruct((N_DEV, D, cs), f32))
    dx, dg, dscale, dshift = _ln_bwd(x, g, scale, shift, dh, dxn)
    return dx, (dg, dshift, dscale, dgate), dw_in, dcw[0:3], dw_out


def _exact_dot(tri, v):
    v1 = v.astype(bf16)
    r1 = v - v1.astype(f32)
    v2 = r1.astype(bf16)
    v3 = (r1 - v2.astype(f32)).astype(bf16)
    d = lambda p: jnp.dot(tri, p, preferred_element_type=f32)
    return d(v1) + d(v2) + d(v3)


def _fox_cumf(fl, b_f):
    t = fl.shape[0]
    tc = min(t, 256)

    def body(fl_ref, b_ref, f_ref, carry_ref):
        i = pl.program_id(0)

        @pl.when(i == 0)
        def _():
            carry_ref[...] = jnp.zeros_like(carry_ref)

        lf = jax.nn.log_sigmoid(fl_ref[...] + b_ref[...])
        r = lax.broadcasted_iota(jnp.int32, (tc, tc), 0)
        c = lax.broadcasted_iota(jnp.int32, (tc, tc), 1)
        tri = (r >= c).astype(bf16)
        fc = _exact_dot(tri, lf) + carry_ref[0:1, :]
        f_ref[...] = fc
        carry_ref[0:1, :] = fc[tc - 1:tc, :]

    return pl.pallas_call(
        body, name="fox_cumf", grid=(t // tc,),
        in_specs=[_tok_spec(tc, 128), _row_spec(128)],
        out_specs=_tok_spec(tc, 128), out_shape=jax.ShapeDtypeStruct((t, 128), f32),
        scratch_shapes=[pltpu.VMEM((8, 128), f32)], compiler_params=_cp(1),
    )(fl, b_f)


def _fox_cumf_bwd(df, fl, b_f):
    t = fl.shape[0]
    tc = min(t, 256)
    nt = t // tc

    def body(df_ref, fl_ref, b_ref, dfl_ref, db_ref, carry_ref):
        i = pl.program_id(0)

        @pl.when(i == 0)
        def _():
            carry_ref[...] = jnp.zeros_like(carry_ref)

        r = lax.broadcasted_iota(jnp.int32, (tc, tc), 0)
        c = lax.broadcasted_iota(jnp.int32, (tc, tc), 1)
        tri = (r <= c).astype(bf16)
        dlf = _exact_dot(tri, df_ref[...]) + carry_ref[0:1, :]
        carry_ref[0:1, :] = dlf[0:1, :]
        dfl = dlf * jax.nn.sigmoid(-(fl_ref[...] + b_ref[...]))
        dfl_ref[...] = dfl
        _acc_add(i == 0, db_ref, jnp.sum(dfl, axis=0, keepdims=True))

    return pl.pallas_call(
        body, name="fox_cumf_bwd", grid=(nt,),
        in_specs=[_tok_spec(tc, 128, nt, True), _tok_spec(tc, 128, nt, True), _row_spec(128)],
        out_specs=[_tok_spec(tc, 128, nt, True), _row_spec(128)],
        out_shape=[jax.ShapeDtypeStruct((t, 128), f32), jax.ShapeDtypeStruct((1, 128), f32)],
        scratch_shapes=[pltpu.VMEM((8, 128), f32)], compiler_params=_cp(1),
    )(df, fl, b_f)


def _head_norm(v, gain, mult):
    return v * lax.rsqrt(jnp.mean(v * v, axis=-1, keepdims=True) + NORM_EPS) * gain * mult


def _fox_qknorm(qkv, q_gain, k_gain):
    t = qkv.shape[2]
    tm = _tok_tile(t)

    def body(q_ref, k_ref, v_ref, qg_ref, kg_ref, o_ref):
        o_ref[0, 0] = _head_norm(q_ref[0, 0], qg_ref[...], FOX_HEAD_DIM ** -0.5).astype(bf16)
        o_ref[1, 0] = _head_norm(k_ref[0, 0], kg_ref[...], 1.0).astype(bf16)
        o_ref[2, 0] = v_ref[0, 0].astype(bf16)

    blk = lambda s: pl.BlockSpec((1, 1, tm, FOX_HEAD_DIM), lambda h, i: (s, h, i, 0))
    gspec = pl.BlockSpec((1, FOX_HEAD_DIM), lambda h, i: (0, 0))
    return pl.pallas_call(
        body, name="fox_qknorm", grid=(FOX_HEADS, t // tm),
        in_specs=[blk(0), blk(1), blk(2), gspec, gspec],
        out_specs=pl.BlockSpec((3, 1, tm, FOX_HEAD_DIM), lambda h, i: (0, h, i, 0)),
        out_shape=jax.ShapeDtypeStruct((3, FOX_HEADS, t, FOX_HEAD_DIM), bf16),
        compiler_params=_cp(2),
    )(qkv, qkv, qkv, q_gain, k_gain)


def _fox_qknorm_bwd(qkv, q_gain, k_gain, dqn, dkn):
    t = qkv.shape[2]
    tm = _tok_tile(t)

    def body(q_ref, k_ref, qg_ref, kg_ref, dqn_ref, dkn_ref, o_ref, dqg_ref, dkg_ref):
        first = jnp.logical_and(pl.program_id(0) == 0, pl.program_id(1) == 0)
        _, vq = jax.vjp(lambda v, gn: _head_norm(v, gn, FOX_HEAD_DIM ** -0.5), q_ref[0, 0], qg_ref[...])
        dq, dqg = vq(dqn_ref[0])
        _, vk = jax.vjp(lambda v, gn: _head_norm(v, gn, 1.0), k_ref[0, 0], kg_ref[...])
        dk, dkg = vk(dkn_ref[0])
        o_ref[0, 0] = dq
        o_ref[1, 0] = dk
        _acc_add(first, dqg_ref, dqg)
        _acc_add(first, dkg_ref, dkg)

    blk = lambda s: pl.BlockSpec((1, 1, tm, FOX_HEAD_DIM), lambda h, i: (s, h, i, 0))
    hblk = pl.BlockSpec((1, tm, FOX_HEAD_DIM), lambda h, i: (h, i, 0))
    gspec = pl.BlockSpec((1, FOX_HEAD_DIM), lambda h, i: (0, 0))
    gshape = jax.ShapeDtypeStruct((1, FOX_HEAD_DIM), f32)
    return pl.pallas_call(
        body, name="fox_qknorm_bwd", grid=(FOX_HEADS, t // tm),
        in_specs=[blk(0), blk(1), gspec, gspec, hblk, hblk],
        out_specs=[pl.BlockSpec((2, 1, tm, FOX_HEAD_DIM), lambda h, i: (0, h, i, 0)), gspec, gspec],
        out_shape=[jax.ShapeDtypeStruct((2, FOX_HEADS, t, FOX_HEAD_DIM), f32), gshape, gshape],
        compiler_params=_cp(2),
    )(qkv, qkv, q_gain, k_gain, dqn, dkn)


def _fox_block(t):
    return min(t, 512)


FOX_STRIP = 64
FOX_HEADS_PER_STEP = 2


def _fox_mask(s, row0):
    r = row0 + lax.broadcasted_iota(jnp.int32, s.shape, 0)
    c = lax.broadcasted_iota(jnp.int32, s.shape, 1)
    return jnp.where(r >= c, s, -jnp.inf)


def _strips(tb):
    ts = min(tb, FOX_STRIP)
    return [(r * ts, slice(r * ts, (r + 1) * ts)) for r in range(tb // ts)]


def _fox_attn_fwd(qkvn, fcol, fref):
    t = qkvn.shape[2]
    tb = _fox_block(t)
    nq = t // tb
    dh = FOX_HEAD_DIM

    hp = FOX_HEADS_PER_STEP

    def body(q_ref, k_ref, v_ref, fc_ref, fr_ref, o_ref, lse_ref):
        i = pl.program_id(1)

        def step(j, carry, diag):
            off = pl.multiple_of(j * tb, tb)
            out = []
            for hh in range(hp):
                m, l, acc = carry[hh]
                kj = k_ref[0, hh, pl.ds(off, tb), :]
                vj = v_ref[0, hh, pl.ds(off, tb), :]
                bias = fr_ref[hh, pl.ds(i, 1), 0:1] - fc_ref[hh, :, pl.ds(off, tb)]
                s = lax.dot_general(q_ref[0, hh], kj, _DN["nt"], preferred_element_type=f32) + bias
                if diag:
                    s = _fox_mask(s, 0)
                m_new = jnp.maximum(m, jnp.max(s, axis=-1, keepdims=True))
                alpha = jnp.exp(m - m_new)
                p = jnp.exp(s - m_new)
                l = alpha * l + jnp.sum(p, axis=-1, keepdims=True)
                p_hi = p.astype(bf16)
                p_lo = (p - p_hi.astype(f32)).astype(bf16)
                pv = jnp.dot(p_hi, vj, preferred_element_type=f32) + jnp.dot(p_lo, vj, preferred_element_type=f32)
                out.append((m_new, l, alpha * acc + pv))
            return tuple(out)

        one = (jnp.full((tb, 1), -jnp.inf, f32), jnp.zeros((tb, 1), f32), jnp.zeros((tb, dh), f32))
        carry = lax.fori_loop(0, i, lambda j, c: step(j, c, False), (one,) * hp)
        for hh, (m, l, acc) in enumerate(step(i, carry, True)):
            o_ref[hh] = acc / l
            lse_ref[hh] = jnp.broadcast_to(m + jnp.log(l), (tb, 128))

    return pl.pallas_call(
        body, name="fox_attn_fwd", grid=(FOX_HEADS // hp, nq),
        in_specs=[pl.BlockSpec((1, hp, tb, dh), lambda h, i: (0, h, i, 0)),
                  pl.BlockSpec((1, hp, t, dh), lambda h, i: (1, h, 0, 0)),
                  pl.BlockSpec((1, hp, t, dh), lambda h, i: (2, h, 0, 0)),
                  pl.BlockSpec((hp, 1, t), lambda h, i: (h, 0, 0)),
                  pl.BlockSpec((hp, nq, 128), lambda h, i: (h, 0, 0))],
        out_specs=[pl.BlockSpec((hp, tb, dh), lambda h, i: (h, i, 0)),
                   pl.BlockSpec((hp, tb, 128), lambda h, i: (h, i, 0))],
        out_shape=[jax.ShapeDtypeStruct((FOX_HEADS, t, dh), f32), jax.ShapeDtypeStruct((FOX_HEADS, t, 128), f32)],
        compiler_params=_cp(2),
    )(qkvn, qkvn, qkvn, fcol, fref)


def _fox_delta(o, do):
    t = o.shape[1]
    tm = _tok_tile(t)
    dh = FOX_HEAD_DIM

    def body(o_ref, do_ref, dl_ref, dob_ref):
        dob = do_ref[0].astype(bf16)
        dob_ref[0] = dob
        dl_ref[0] = jnp.broadcast_to(jnp.sum(dob.astype(f32) * o_ref[0], axis=-1, keepdims=True), (tm, 128))

    hb = pl.BlockSpec((1, tm, dh), lambda h, i: (h, i, 0))
    return pl.pallas_call(
        body, name="fox_delta", grid=(FOX_HEADS, t // tm), in_specs=[hb, hb],
        out_specs=[pl.BlockSpec((1, tm, 128), lambda h, i: (h, i, 0)), hb],
        out_shape=[jax.ShapeDtypeStruct((FOX_HEADS, t, 128), f32), jax.ShapeDtypeStruct((FOX_HEADS, t, dh), bf16)],
        compiler_params=_cp(2),
    )(o, do)


def _fox_attn_bwd_kv(qkvn, fcol, fref, lse, delta, dob):
    t = qkvn.shape[2]
    tb = _fox_block(t)
    nq = t // tb
    dh = FOX_HEAD_DIM

    def body(q_ref, k_ref, v_ref, fc_ref, fr_ref, lse_ref, dl_ref, do_ref, dk_ref, dv_ref, df_ref, dq_ref,
             s_ref, dp_ref, p_ref, ds_ref):
        j = pl.program_id(1)
        kj = k_ref[0, 0]
        vj = v_ref[0, 0]
        fcol_j = fc_ref[0]
        dk_ref[...] = jnp.zeros_like(dk_ref)
        dv_ref[...] = jnp.zeros_like(dv_ref)
        df_ref[...] = jnp.zeros_like(df_ref)

        @pl.when(j == 0)
        def _():
            dq_ref[...] = jnp.zeros_like(dq_ref)

        def step(i, diag):
            off = pl.multiple_of(i * tb, tb)
            qi = q_ref[0, 0, pl.ds(off, tb), :]
            doi = do_ref[0, pl.ds(off, tb), :]
            s_ref[...] = lax.dot_general(qi, kj, _DN["nt"], preferred_element_type=f32)
            dp_ref[...] = lax.dot_general(doi, vj, _DN["nt"], preferred_element_type=f32)
            bias = fr_ref[0, pl.ds(i, 1), 0:1] - fcol_j
            df = jnp.zeros((1, tb), f32)
            for row0, rs in _strips(tb):
                rows = pl.ds(off + row0, rs.stop - rs.start)
                s = s_ref[rs, :] + bias
                if diag:
                    s = _fox_mask(s, row0)
                p = jnp.exp(s - lse_ref[0, rows, 0:1])
                p_ref[rs, :] = p.astype(bf16)
                ds = p * (dp_ref[rs, :] - dl_ref[0, rows, 0:1])
                ds_ref[rs, :] = ds.astype(bf16)
                df = df + jnp.sum(ds, axis=0, keepdims=True)
            dv_ref[0] += lax.dot_general(p_ref[...], doi, _DN["tn"], preferred_element_type=f32)
            dk_ref[0] += lax.dot_general(ds_ref[...], qi, _DN["tn"], preferred_element_type=f32)
            dq_ref[0, pl.ds(off, tb), :] += jnp.dot(ds_ref[...], kj, preferred_element_type=f32)
            df_ref[0] -= df

        def loop_body(i, carry):
            step(i, False)
            return carry

        step(j, True)
        lax.fori_loop(j + 1, nq, loop_body, 0)

    scratch = [pltpu.VMEM((tb, tb), f32), pltpu.VMEM((tb, tb), f32), pltpu.VMEM((tb, tb), bf16),
               pltpu.VMEM((tb, tb), bf16)]
    hb = pl.BlockSpec((1, tb, dh), lambda h, j: (h, j, 0))
    full = pl.BlockSpec((1, t, dh), lambda h, j: (h, 0, 0))
    full128 = pl.BlockSpec((1, t, 128), lambda h, j: (h, 0, 0))
    hshape = jax.ShapeDtypeStruct((FOX_HEADS, t, dh), f32)
    return pl.pallas_call(
        body, name="fox_attn_bwd_kv", grid=(FOX_HEADS, nq),
        in_specs=[pl.BlockSpec((1, 1, t, dh), lambda h, j: (0, h, 0, 0)),
                  pl.BlockSpec((1, 1, tb, dh), lambda h, j: (1, h, j, 0)),
                  pl.BlockSpec((1, 1, tb, dh), lambda h, j: (2, h, j, 0)),
                  pl.BlockSpec((1, 1, tb), lambda h, j: (h, 0, j)),
                  pl.BlockSpec((1, nq, 128), lambda h, j: (h, 0, 0)),
                  full128, full128, full],
        out_specs=[hb, hb, pl.BlockSpec((1, 1, tb), lambda h, j: (h, 0, j)), full],
        out_shape=[hshape, hshape, jax.ShapeDtypeStruct((FOX_HEADS, 1, t), f32), hshape],
        scratch_shapes=scratch, compiler_params=_cp(2),
    )(qkvn, qkvn, qkvn, fcol, fref, lse, delta, dob)


def _heads_of(a, n):
    t = a.shape[0]
    return a.reshape(t, n, FOX_HEADS, FOX_HEAD_DIM).transpose(1, 2, 0, 3)


def _fox_fwd(x, ln, w_in, b_f, q_gain, k_gain, w_o):
    g, shift, scale, gate = ln
    t = x.shape[0]
    tb = _fox_block(t)
    h = _ln_fwd(x, g, scale, shift)
    proj = _mm(h, w_in, "nn", f32, "fox_in")
    qkv = _heads_of(proj[:, :3 * D], 3)
    fl = proj[:, 3 * D:3 * D + 128]
    fcum = _fox_cumf(fl, b_f)
    fcol = fcum[:, :FOX_HEADS].T.reshape(FOX_HEADS, 1, t)
    fref = jnp.broadcast_to(fcol[:, 0, ::tb][:, :, None], (FOX_HEADS, t // tb, 128))
    qkvn = _fox_qknorm(qkv, q_gain, k_gain)
    o, lse = _fox_attn_fwd(qkvn, fcol, fref)
    ob = o.transpose(1, 0, 2).reshape(t, D).astype(bf16)
    y = _mm(ob, w_o, "nn", f32, "fox_out")
    return _resid_fwd(x, y, gate, 1.0), (x, h, qkv, fl, fcol, fref, qkvn, o, lse, ob, y)


def _fox_bwd(dxn, saved, ln, w_in, b_f, q_gain, k_gain, w_o):
    x, h, qkv, fl, fcol, fref, qkvn, o, lse, ob, y = saved
    g, shift, scale, gate = ln
    t = x.shape[0]
    dy, dgate = _resid_bwd(dxn, y, gate, 1.0)
    do_flat = _mm(dy, w_o, "nt", f32, "fox_do")
    dw_o = _mm(ob, dy, "tn", f32, "fox_dwo")
    do = do_flat.reshape(t, FOX_HEADS, FOX_HEAD_DIM).transpose(1, 0, 2)
    delta, dob = _fox_delta(o, do)
    dkn, dv, dfcol, dqn = _fox_attn_bwd_kv(qkvn, fcol, fref, lse, delta, dob)
    dqk, dqg, dkg = _fox_qknorm_bwd(qkv, q_gain, k_gain, dqn, dkn)
    df = jnp.pad(dfcol.reshape(FOX_HEADS, t).T, ((0, 0), (0, 128 - FOX_HEADS)))
    dfl, db_f = _fox_cumf_bwd(df, fl, b_f)
    dqkv = jnp.concatenate([dqk, dv[None]], axis=0).transpose(2, 0, 1, 3).reshape(t, 3 * D)
    dproj = jnp.concatenate([dqkv.astype(bf16), dfl.astype(bf16)], axis=1)
    dh = _mm(dproj, w_in, "nt", f32, "fox_dh")
    dw_in = _mm(h, dproj, "tn", f32, "fox_dwin")
    dx, dg, dscale, dshift = _ln_bwd(x, g, scale, shift, dh, dxn)
    return dx, (dg, dshift, dscale, dgate), dw_in, db_f, dqg, dkg, dw_o


def _s5_disc(lam_re, lam_im, log_dt, b_re, b_im):
    dt = jnp.exp(log_dt)
    mag = jnp.exp(lam_re * dt)
    lb_re, lb_im = mag * jnp.cos(lam_im * dt), mag * jnp.sin(lam_im * dt)
    den = lam_re * lam_re + lam_im * lam_im
    nr, ni = lb_re - 1.0, lb_im
    k_re = (nr * lam_re + ni * lam_im) / den
    k_im = (ni * lam_re - nr * lam_im) / den
    return lb_re, lb_im, k_re * b_re - k_im * b_im, k_re * b_im + k_im * b_re


def _s5_prep(lam_re, lam_im, log_dt, b_re, b_im):
    def body(ar_ref, ai_ref, dt_ref, br_ref, bi_ref, lr_ref, li_ref, bbr_ref, bbi_ref):
        lr, li, bbr, bbi = _s5_disc(ar_ref[...], ai_ref[...], dt_ref[...], br_ref[...], bi_ref[...])
        lr_ref[...] = lr
        li_ref[...] = li
        bbr_ref[...] = bbr
        bbi_ref[...] = bbi

    small = jax.ShapeDtypeStruct(lam_re.shape, f32)
    bigs = jax.ShapeDtypeStruct(b_re.shape, f32)
    return pl.pallas_call(body, name="s5_prep", out_shape=[small, small, bigs, bigs])(lam_re, lam_im, log_dt, b_re, b_im)


def _s5_prep_bwd(lam_re, lam_im, log_dt, b_re, b_im, dlr, dli, dbbr, dbbi):
    def body(ar_ref, ai_ref, dt_ref, br_ref, bi_ref, dlr_ref, dli_ref, dbbr_ref, dbbi_ref,
             dar_ref, dai_ref, ddt_ref, dbr_ref, dbi_ref):
        _, vjp = jax.vjp(_s5_disc, ar_ref[...], ai_ref[...], dt_ref[...], br_ref[...], bi_ref[...])
        dar, dai, ddt, dbr, dbi = vjp((dlr_ref[...], dli_ref[...], dbbr_ref[...], dbbi_ref[...]))
        dar_ref[...] = dar
        dai_ref[...] = dai
        ddt_ref[...] = jnp.broadcast_to(jnp.sum(ddt, axis=-1, keepdims=True), ddt.shape)
        dbr_ref[...] = dbr
        dbi_ref[...] = dbi

    small = jax.ShapeDtypeStruct(lam_re.shape, f32)
    bigs = jax.ShapeDtypeStruct(b_re.shape, f32)
    return pl.pallas_call(body, name="s5_prep_bwd", out_shape=[small, small, small, bigs, bigs])(
        lam_re, lam_im, log_dt, b_re, b_im, dlr, dli, dbbr, dbbi)


def _s5_tile(t):
    return min(t, 128)


def _s5_blk(k, width):
    return slice(k * width, (k + 1) * width)


def _s5_in_bd(bb):
    b4 = bb.reshape(S5_BLOCKS, 8, S5_GROUP, S5_STATE)
    return jnp.einsum("kgin,gh->kgihn", b4, jnp.eye(8, dtype=bb.dtype)).reshape(S5_BLOCKS, S5_BCH, S5_BST)


def _s5_in_bd_diag(bd):
    b5 = bd.reshape(S5_BLOCKS, 8, S5_GROUP, 8, S5_STATE)
    return jnp.einsum("kgihn,gh->kgin", b5, jnp.eye(8, dtype=bd.dtype)).reshape(S5_GROUPS, S5_GROUP, S5_STATE)


def _s5_out_bd(c):
    c4 = c.reshape(S5_BLOCKS, 8, S5_GROUP, S5_STATE)
    return jnp.einsum("kgin,gh->kgnhi", c4, jnp.eye(8, dtype=c.dtype)).reshape(S5_BLOCKS, S5_BST, S5_BCH)


def _s5_out_bd_diag(bd):
    c5 = bd.reshape(S5_BLOCKS, 8, S5_STATE, 8, S5_GROUP)
    return jnp.einsum("kgnhi,gh->kgin", c5, jnp.eye(8, dtype=bd.dtype)).reshape(S5_GROUPS, S5_GROUP, S5_STATE)


def _s5_scan_fwd(x, ln, lb_re, lb_im, bbr_bd, bbi_bd, cr_bd, ci_bd, dskip):
    g, shift, scale, _ = ln
    t = x.shape[0]
    tm = _s5_tile(t)
    ns = S5_NSTATE

    def body(x_ref, g_ref, sc_ref, sh_ref, ar_ref, ai_ref, bbr_ref, bbi_ref, cr_ref, ci_ref, d_ref,
             yy_ref, xr_ref, xi_ref, cre_ref, cim_ref):
        @pl.when(pl.program_id(0) == 0)
        def _():
            cre_ref[...] = jnp.zeros_like(cre_ref)
            cim_ref[...] = jnp.zeros_like(cim_ref)

        h = _adaln(x_ref[...], g_ref[...], sc_ref[...], sh_ref[...])
        ub = h.astype(bf16)
        for k in range(S5_BLOCKS):
            uk = ub[:, _s5_blk(k, S5_BCH)]
            xr_ref[:, _s5_blk(k, S5_BST)] = jnp.dot(uk, bbr_ref[k], preferred_element_type=f32)
            xi_ref[:, _s5_blk(k, S5_BST)] = jnp.dot(uk, bbi_ref[k], preferred_element_type=f32)
        ar, ai = ar_ref[...], ai_ref[...]

        def step(tt, carry):
            sr, si = carry
            row = pl.ds(tt, 1)
            nr = (ar * sr - ai * si) + xr_ref[row, :]
            ni = (ar * si + ai * sr) + xi_ref[row, :]
            xr_ref[row, :] = nr
            xi_ref[row, :] = ni
            return nr, ni

        sr, si = lax.fori_loop(0, tm, step, (cre_ref[0:1, :], cim_ref[0:1, :]), unroll=2)
        cre_ref[0:1, :] = sr
        cim_ref[0:1, :] = si
        for k in range(S5_BLOCKS):
            sb = _s5_blk(k, S5_BST)
            yk = (jnp.dot(xr_ref[:, sb].astype(bf16), cr_ref[k], preferred_element_type=f32)
                  - jnp.dot(xi_ref[:, sb].astype(bf16), ci_ref[k], preferred_element_type=f32))
            cb = _s5_blk(k, S5_BCH)
            yy_ref[:, cb] = yk + d_ref[:, cb] * h[:, cb]

    bd_in = pl.BlockSpec((S5_BLOCKS, S5_BCH, S5_BST), lambda i: (0, 0, 0))
    bd_out = pl.BlockSpec((S5_BLOCKS, S5_BST, S5_BCH), lambda i: (0, 0, 0))
    st = jax.ShapeDtypeStruct((t, ns), f32)
    return pl.pallas_call(
        body, name="s5_scan_fwd", grid=(t // tm,),
        in_specs=[_tok_spec(tm, D), _row_spec(D), _row_spec(D), _row_spec(D), _row_spec(ns), _row_spec(ns),
                  bd_in, bd_in, bd_out, bd_out, _row_spec(D)],
        out_specs=[_tok_spec(tm, D), _tok_spec(tm, ns), _tok_spec(tm, ns)],
        out_shape=[jax.ShapeDtypeStruct((t, D), f32), st, st],
        scratch_shapes=[pltpu.VMEM((8, ns), f32), pltpu.VMEM((8, ns), f32)],
        compiler_params=_cp(1),
    )(x, g, scale, shift, lb_re, lb_im, bbr_bd, bbi_bd, cr_bd, ci_bd, dskip)


def _s5_scan_bwd(dyy, x, ln, xr, xi, lb_re, lb_im, bbr_bd, bbi_bd, cr_bd, ci_bd, dskip):
    g, shift, scale, _ = ln
    t = x.shape[0]
    tm = _s5_tile(t)
    nt = t // tm
    ns = S5_NSTATE
    per = tm // 8

    def body(dyy_ref, x_ref, g_ref, sc_ref, sh_ref, xr_ref, xi_ref, xrp_ref, xip_ref, ar_ref, ai_ref,
             bbr_ref, bbi_ref, cr_ref, ci_ref, d_ref,
             du_ref, dar_ref, dai_ref, dbbr_ref, dbbi_ref, dcr_ref, dci_ref, dd_ref,
             gr_ref, gi_ref, cre_ref, cim_ref):
        i = pl.program_id(0)
        first = i == 0

        @pl.when(first)
        def _():
            cre_ref[...] = jnp.zeros_like(cre_ref)
            cim_ref[...] = jnp.zeros_like(cim_ref)

        h = _adaln(x_ref[...], g_ref[...], sc_ref[...], sh_ref[...])
        ub = h.astype(bf16)
        dyy_v = dyy_ref[...]
        dyb = dyy_v.astype(bf16)
        for k in range(S5_BLOCKS):
            dk = dyb[:, _s5_blk(k, S5_BCH)]
            sb = _s5_blk(k, S5_BST)
            gr_ref[:, sb] = lax.dot_general(dk, cr_ref[k], _DN["nt"], preferred_element_type=f32)
            gi_ref[:, sb] = -lax.dot_general(dk, ci_ref[k], _DN["nt"], preferred_element_type=f32)
        ar, ai = ar_ref[...], ai_ref[...]

        def step(s, carry):
            nr_, ni_ = carry
            row = pl.ds(tm - 1 - s, 1)
            nr = gr_ref[row, :] + (ar * nr_ + ai * ni_)
            ni = gi_ref[row, :] + (ar * ni_ - ai * nr_)
            gr_ref[row, :] = nr
            gi_ref[row, :] = ni
            return nr, ni

        nr, ni = lax.fori_loop(0, tm, step, (cre_ref[0:1, :], cim_ref[0:1, :]), unroll=2)
        cre_ref[0:1, :] = nr
        cim_ref[0:1, :] = ni

        is_first_tile = i == nt - 1
        xr_v, xi_v = xr_ref[...], xi_ref[...]
        xrp = jnp.where(is_first_tile, 0.0, xrp_ref[...])
        xip = jnp.where(is_first_tile, 0.0, xip_ref[...])
        xr_s = _roll_rows(jnp.concatenate([xrp, xr_v], axis=0), 1)[8:, :]
        xi_s = _roll_rows(jnp.concatenate([xip, xi_v], axis=0), 1)[8:, :]
        gr, gi = gr_ref[...], gi_ref[...]
        _acc_add(first, dar_ref, jnp.sum(gr * xr_s + gi * xi_s, axis=0, keepdims=True))
        _acc_add(first, dai_ref, jnp.sum(gi * xr_s - gr * xi_s, axis=0, keepdims=True))
        _acc_add(first, dd_ref, jnp.sum(dyy_v * h, axis=0, keepdims=True))
        grb, gib = gr.astype(bf16), gi.astype(bf16)
        xrb, xib = xr_v.astype(bf16), xi_v.astype(bf16)
        for k in range(S5_BLOCKS):
            cb, sb = _s5_blk(k, S5_BCH), _s5_blk(k, S5_BST)
            uk, dk = ub[:, cb], dyb[:, cb]
            tn = lambda a_, b_: lax.dot_general(a_, b_, _DN["tn"], preferred_element_type=f32)
            vals = (tn(uk, grb[:, sb]), tn(uk, gib[:, sb]), tn(xrb[:, sb], dk), -tn(xib[:, sb], dk))
            for ref, val in zip((dbbr_ref, dbbi_ref, dcr_ref, dci_ref), vals):
                @pl.when(first)
                def _():
                    ref[k] = val

                @pl.when(jnp.logical_not(first))
                def _():
                    ref[k] += val
            du_k = (lax.dot_general(grb[:, sb], bbr_ref[k], _DN["nt"], preferred_element_type=f32)
                    + lax.dot_general(gib[:, sb], bbi_ref[k], _DN["nt"], preferred_element_type=f32))
            du_ref[:, cb] = du_k + d_ref[:, cb] * dyy_v[:, cb]

    rev = lambda c: _tok_spec(tm, c, nt, True)
    prev = pl.BlockSpec((8, ns), lambda i: (jnp.maximum((nt - 1 - i) * per - 1, 0), 0))
    bd_in = pl.BlockSpec((S5_BLOCKS, S5_BCH, S5_BST), lambda i: (0, 0, 0))
    bd_out = pl.BlockSpec((S5_BLOCKS, S5_BST, S5_BCH), lambda i: (0, 0, 0))
    row_ns = jax.ShapeDtypeStruct((1, ns), f32)
    bd_in_s = jax.ShapeDtypeStruct((S5_BLOCKS, S5_BCH, S5_BST), f32)
    bd_out_s = jax.ShapeDtypeStruct((S5_BLOCKS, S5_BST, S5_BCH), f32)
    return pl.pallas_call(
        body, name="s5_scan_bwd", grid=(nt,),
        in_specs=[rev(D), rev(D), _row_spec(D), _row_spec(D), _row_spec(D), rev(ns), rev(ns), prev, prev,
                  _row_spec(ns), _row_spec(ns), bd_in, bd_in, bd_out, bd_out, _row_spec(D)],
        out_specs=[rev(D), _row_spec(ns), _row_spec(ns), bd_in, bd_in, bd_out, bd_out, _row_spec(D)],
        out_shape=[jax.ShapeDtypeStruct((t, D), f32), row_ns, row_ns, bd_in_s, bd_in_s, bd_out_s, bd_out_s,
                   jax.ShapeDtypeStruct((1, D), f32)],
        scratch_shapes=[pltpu.VMEM((tm, ns), f32), pltpu.VMEM((tm, ns), f32),
                        pltpu.VMEM((8, ns), f32), pltpu.VMEM((8, ns), f32)],
        compiler_params=_cp(1),
    )(dyy, x, g, scale, shift, xr, xi, xr, xi, lb_re, lb_im, bbr_bd, bbi_bd, cr_bd, ci_bd, dskip)


def _s5_gelu(yy):
    t = yy.shape[0]
    tm = _tok_tile(t)

    def body(y_ref, o_ref):
        o_ref[...] = jax.nn.gelu(y_ref[...]).astype(bf16)

    return pl.pallas_call(
        body, name="s5_gelu", grid=(t // tm,), in_specs=[_tok_spec(tm, D)], out_specs=_tok_spec(tm, D),
        out_shape=jax.ShapeDtypeStruct((t, D), bf16), compiler_params=_cp(1),
    )(yy)


def _s5_glu(gl, z):
    return gl * jax.nn.sigmoid(z)


def _s5_out(x, yy, z, gate):
    t = x.shape[0]
    tm = _tok_tile(t)

    def body(x_ref, y_ref, z_ref, gt_ref, o_ref):
        o_ref[...] = x_ref[...] + (1.0 + gt_ref[...]) * _s5_glu(jax.nn.gelu(y_ref[...]), z_ref[...])

    return pl.pallas_call(
        body, name="s5_out", grid=(t // tm,),
        in_specs=[_tok_spec(tm, D), _tok_spec(tm, D), _tok_spec(tm, D), _row_spec(D)],
        out_specs=_tok_spec(tm, D), out_shape=jax.ShapeDtypeStruct((t, D), f32), compiler_params=_cp(1),
    )(x, yy, z, gate)


def _s5_out_bwd(dxn, yy, z, gate):
    t = dxn.shape[0]
    tm = _tok_tile(t)

    def body(dxn_ref, y_ref, z_ref, gt_ref, dz_ref, dgl_ref, dgt_ref):
        dxn_v = dxn_ref[...]
        gl = jax.nn.gelu(y_ref[...])
        out, vjp = jax.vjp(_s5_glu, gl, z_ref[...])
        dgl, dz = vjp((1.0 + gt_ref[...]) * dxn_v)
        dz_ref[...] = dz.astype(bf16)
        dgl_ref[...] = dgl
        _acc_add(pl.program_id(0) == 0, dgt_ref, jnp.sum(dxn_v * out, axis=0, keepdims=True))

    return pl.pallas_call(
        body, name="s5_out_bwd", grid=(t // tm,),
        in_specs=[_tok_spec(tm, D), _tok_spec(tm, D), _tok_spec(tm, D), _row_spec(D)],
        out_specs=[_tok_spec(tm, D), _tok_spec(tm, D), _row_spec(D)],
        out_shape=[jax.ShapeDtypeStruct((t, D), bf16), jax.ShapeDtypeStruct((t, D), f32),
                   jax.ShapeDtypeStruct((1, D), f32)],
        compiler_params=_cp(1),
    )(dxn, yy, z, gate)


def _s5_gelu_bwd(yy, dgl_a, dgl_b):
    t = yy.shape[0]
    tm = _tok_tile(t)

    def body(y_ref, a_ref, b_ref, o_ref):
        _, vjp = jax.vjp(jax.nn.gelu, y_ref[...])
        o_ref[...] = vjp(a_ref[...] + b_ref[...])[0]

    return pl.pallas_call(
        body, name="s5_gelu_bwd", grid=(t // tm,),
        in_specs=[_tok_spec(tm, D), _tok_spec(tm, D), _tok_spec(tm, D)],
        out_specs=_tok_spec(tm, D), out_shape=jax.ShapeDtypeStruct((t, D), f32), compiler_params=_cp(1),
    )(yy, dgl_a, dgl_b)


def _s5_params(lam_re, lam_im, log_dt, b_re, b_im):
    bc = lambda a: a.reshape(S5_GROUPS, 1, -1)
    return (bc(lam_re), bc(lam_im), jnp.broadcast_to(log_dt.reshape(S5_GROUPS, 1, 1), (S5_GROUPS, 1, S5_STATE)),
            b_re.transpose(0, 2, 1), b_im.transpose(0, 2, 1))


def _s5_fwd(x, ln, raw, c_re, c_im, dskip, w_glu):
    gate = ln[3]
    lb_re, lb_im, bb_re, bb_im = _s5_prep(*raw)
    lbr, lbi = lb_re.reshape(1, S5_NSTATE), lb_im.reshape(1, S5_NSTATE)
    bds = (_s5_in_bd(bb_re).astype(bf16), _s5_in_bd(bb_im).astype(bf16),
           _s5_out_bd(c_re).astype(bf16), _s5_out_bd(c_im).astype(bf16))
    yy, xr, xi = _s5_scan_fwd(x, ln, lbr, lbi, *bds, dskip)
    gl = _s5_gelu(yy)
    z = _mm(gl, w_glu, "nn", f32, "s5_glu_mm")
    return _s5_out(x, yy, z, gate), (x, lbr, lbi, bds, yy, xr, xi, gl, z)


def _s5_bwd(dxn, saved, ln, raw, dskip, w_glu):
    x, lbr, lbi, bds, yy, xr, xi, gl, z = saved
    g, shift, scale, gate = ln
    dz, dgl_a, dgate = _s5_out_bwd(dxn, yy, z, gate)
    dgl_b = _mm(dz, w_glu, "nt", f32, "s5_dgl")
    dw_glu = _mm(gl, dz, "tn", f32, "s5_dwglu")
    dyy = _s5_gelu_bwd(yy, dgl_a, dgl_b)
    du, dar, dai, dbbr_bd, dbbi_bd, dcr_bd, dci_bd, dd = _s5_scan_bwd(dyy, x, ln, xr, xi, lbr, lbi, *bds, dskip)
    shp = (S5_GROUPS, 1, S5_STATE)
    d_lam_re, d_lam_im, d_dt, d_b_re, d_b_im = _s5_prep_bwd(
        *raw, dar.reshape(shp), dai.reshape(shp), _s5_in_bd_diag(dbbr_bd), _s5_in_bd_diag(dbbi_bd))
    dx, dg, dscale, dshift = _ln_bwd(x, g, scale, shift, du, dxn)
    grads = dict(
        s5_lam_re=d_lam_re.reshape(1, S5_GROUPS, S5_STATE), s5_lam_im=d_lam_im.reshape(1, S5_GROUPS, S5_STATE),
        s5_log_dt=d_dt[:, 0, 0].reshape(1, S5_GROUPS),
        s5_b_re=d_b_re.transpose(0, 2, 1)[None], s5_b_im=d_b_im.transpose(0, 2, 1)[None],
        s5_c_re=_s5_out_bd_diag(dcr_bd)[None], s5_c_im=_s5_out_bd_diag(dci_bd)[None],
        s5_d=dd, s5_w_glu=dw_glu)
    return dx, (dg, dshift, dscale, dgate), grads


_MESH = pl.DeviceIdType.MESH
_ANY = pl.BlockSpec(memory_space=pl.ANY)


def _me():
    return lax.axis_index("x"), lax.axis_index("y"), lax.axis_index("c")


def _dev_index(x, y, c):
    return 4 * x + 2 * y + c


def _all_gather(vs, name):
    n = len(vs)

    def body(*refs):
        v_refs, out_refs = refs[:n], refs[n:2 * n]
        send_sems, recv_sems, local_sems = refs[2 * n:]
        x, y, cc = _me()
        me, sibling = (x, y, cc), (x, y, 1 - cc)
        chips = [(1 - x, y), (x, 1 - y), (1 - x, 1 - y)]
        sends, local = [], []

        def copy(a, k, block, to, src=None):
            rows = out_refs[a].at[_dev_index(*block)]
            return pltpu.make_async_remote_copy(
                src_ref=rows if src is None else src, dst_ref=rows,
                send_sem=send_sems.at[7 * a + k], recv_sem=recv_sems.at[7 * a + k], device_id=to, device_id_type=_MESH)

        for a in range(n):
            mine = pltpu.make_async_copy(v_refs[a], out_refs[a].at[_dev_index(*me)], local_sems.at[a])
            mine.start()
            local.append(mine)
            first = [copy(a, 0, me, sibling, src=v_refs[a])]
            first += [copy(a, 1 + j, me, (*chip, cc), src=v_refs[a]) for j, chip in enumerate(chips)]
            for cp in first:
                cp.start()
            sends += first
        for a in range(n):
            for j, chip in enumerate(chips):
                copy(a, 1 + j, (*chip, cc), me).wait_recv()
                passed = copy(a, 4 + j, (*chip, cc), sibling)
                passed.start()
                sends.append(passed)
        for a in range(n):
            copy(a, 0, sibling, me).wait_recv()
            for j, chip in enumerate(chips):
                copy(a, 4 + j, (*chip, 1 - cc), me).wait_recv()
        for cp in sends:
            cp.wait_send()
        for cp in local:
            cp.wait()

    return pl.pallas_call(
        body, name=name, out_shape=[jax.ShapeDtypeStruct((N_DEV,) + v.shape, v.dtype) for v in vs],
        in_specs=[_ANY] * n, out_specs=[_ANY] * n,
        scratch_shapes=[pltpu.SemaphoreType.DMA((7 * n,)), pltpu.SemaphoreType.DMA((7 * n,)),
                        pltpu.SemaphoreType.DMA((n,))],
    )(*vs)


def _exchange_pair(vs, name):
    n = len(vs)

    def body(*refs):
        v_refs, out_refs = refs[:n], refs[n:2 * n]
        send_sems, recv_sems = refs[2 * n:]
        x, y, cc = _me()
        sibling = (x, y, 1 - cc)
        copies = []
        for a in range(n):
            for k in range(4):
                cp = pltpu.make_async_remote_copy(
                    src_ref=v_refs[a].at[2 * k + (1 - cc)], dst_ref=out_refs[a].at[k],
                    send_sem=send_sems.at[4 * a + k], recv_sem=recv_sems.at[4 * a + k],
                    device_id=sibling, device_id_type=_MESH)
                cp.start()
                copies.append(cp)
        for cp in copies:
            cp.wait_recv()
        for cp in copies:
            cp.wait_send()

    return pl.pallas_call(
        body, name=name, out_shape=[jax.ShapeDtypeStruct((4,) + v.shape[1:], v.dtype) for v in vs],
        in_specs=[_ANY] * n, out_specs=[_ANY] * n,
        scratch_shapes=[pltpu.SemaphoreType.DMA((4 * n,)), pltpu.SemaphoreType.DMA((4 * n,))],
    )(*vs)


def _pair_sum(v, got):
    _, r, c = v.shape
    tr = _pick(r, (512, 352, 256, 128))
    core = lax.axis_index("c").astype(jnp.int32).reshape(1)

    def body(c_ref, v_ref, g_ref, o_ref):
        o_ref[...] = (v_ref[...] + g_ref[...]).astype(bf16)

    return pl.pallas_call(
        body, name="pair_sum",
        grid_spec=pltpu.PrefetchScalarGridSpec(
            num_scalar_prefetch=1, grid=(4, r // tr),
            in_specs=[pl.BlockSpec((1, tr, c), lambda k, i, c_ref: (2 * k + c_ref[0], i, 0)),
                      pl.BlockSpec((1, tr, c), lambda k, i, c_ref: (k, i, 0))],
            out_specs=pl.BlockSpec((1, tr, c), lambda k, i, c_ref: (k, i, 0))),
        out_shape=jax.ShapeDtypeStruct((4, r, c), bf16), compiler_params=_cp(2),
    )(core, v, got)


def _exchange_chips(vs, name):
    n = len(vs)

    def body(*refs):
        v_refs, out_refs = refs[:n], refs[n:2 * n]
        send_sems, recv_sems, local_sems = refs[2 * n:]
        x, y, cc = _me()
        mine = 2 * x + y
        peers = []
        for mask in (1, 2, 3):
            px = 1 - x if mask & 2 else x
            py = 1 - y if mask & 1 else y
            peers.append((mask - 1, (px, py, cc), 2 * px + py))
        local, sends = [], []
        for a in range(n):
            own = pltpu.make_async_copy(v_refs[a].at[mine], out_refs[a].at[mine], local_sems.at[a])
            own.start()
            local.append(own)
            for k, peer, pchip in peers:
                cp = pltpu.make_async_remote_copy(
                    src_ref=v_refs[a].at[pchip], dst_ref=out_refs[a].at[mine],
                    send_sem=send_sems.at[3 * a + k], recv_sem=recv_sems.at[3 * a + k],
                    device_id=peer, device_id_type=_MESH)
                cp.start()
                sends.append(cp)
        for a in range(n):
            for k, peer, pchip in peers:
                pltpu.make_async_remote_copy(
                    src_ref=v_refs[a].at[pchip], dst_ref=out_refs[a].at[pchip],
                    send_sem=send_sems.at[3 * a + k], recv_sem=recv_sems.at[3 * a + k],
                    device_id=peer, device_id_type=_MESH).wait_recv()
        for cp in sends:
            cp.wait_send()
        for cp in local:
            cp.wait()

    return pl.pallas_call(
        body, name=name, out_shape=[jax.ShapeDtypeStruct(v.shape, v.dtype) for v in vs],
        in_specs=[_ANY] * n, out_specs=[_ANY] * n,
        scratch_shapes=[pltpu.SemaphoreType.DMA((3 * n,)), pltpu.SemaphoreType.DMA((3 * n,)),
                        pltpu.SemaphoreType.DMA((n,))],
    )(*vs)


def _ada_mod(c_all, ada_w):
    cols = ada_w.shape[2]

    def body(c_ref, w_ref, o_ref):
        cond = jax.nn.silu(c_ref[...]).astype(bf16)
        o_ref[0] = jnp.dot(cond, w_ref[0].astype(bf16), preferred_element_type=f32)

    return pl.pallas_call(
        body, name="ada_mod", grid=(DEPTH,),
        in_specs=[pl.BlockSpec((16, D), lambda i: (0, 0)), pl.BlockSpec((1, D, cols), lambda i: (i, 0, 0))],
        out_specs=pl.BlockSpec((1, 16, cols), lambda i: (i, 0, 0)),
        out_shape=jax.ShapeDtypeStruct((DEPTH, 16, cols), f32), compiler_params=_cp(1),
    )(c_all, ada_w)


def _ada_grad(c_all, dmod):
    cols = dmod.shape[2]

    def body(c_ref, d_ref, o_ref):
        cond = jax.nn.silu(c_ref[...]).astype(bf16)
        o_ref[0] = lax.dot_general(cond, d_ref[0].astype(bf16), _DN["tn"], preferred_element_type=f32)

    return pl.pallas_call(
        body, name="ada_grad", grid=(DEPTH,),
        in_specs=[pl.BlockSpec((16, D), lambda i: (0, 0)), pl.BlockSpec((1, 16, cols), lambda i: (i, 0, 0))],
        out_specs=pl.BlockSpec((1, D, cols), lambda i: (i, 0, 0)),
        out_shape=jax.ShapeDtypeStruct((DEPTH, D, cols), f32), compiler_params=_cp(1),
    )(c_all, dmod)


def _row_tile(r):
    return _pick(r, (512, 352, 256, 128)) if r > 512 else r


def _sum_sources(v, name):
    n, r, c = v.shape
    tr = _row_tile(r)

    def body(v_ref, o_ref):
        acc = v_ref[0]
        for p in range(1, n):
            acc = acc + v_ref[p]
        o_ref[...] = acc.astype(f32)

    return pl.pallas_call(
        body, name=name, grid=(r // tr,),
        in_specs=[pl.BlockSpec((n, tr, c), lambda i: (0, i, 0))], out_specs=pl.BlockSpec((tr, c), lambda i: (i, 0)),
        out_shape=jax.ShapeDtypeStruct((r, c), f32), compiler_params=_cp(1),
    )(v)


def _adamw(parts, w, m, v, name):
    n, r, c = parts.shape
    tr = _row_tile(r)
    c1 = 1.0 - ADAM_B1 ** ADAM_STEP
    c2 = 1.0 - ADAM_B2 ** ADAM_STEP

    def body(p_ref, w_ref, m_ref, v_ref, g_ref, d_ref, mo_ref, vo_ref):
        g_v = p_ref[0].astype(f32)
        for p in range(1, n):
            g_v = g_v + p_ref[p].astype(f32)
        m_n = ADAM_B1 * m_ref[...] + (1.0 - ADAM_B1) * g_v
        v_n = ADAM_B2 * v_ref[...] + (1.0 - ADAM_B2) * (g_v * g_v)
        g_ref[...] = g_v
        d_ref[...] = -ADAM_LR * ((m_n / c1) / (jnp.sqrt(v_n / c2) + ADAM_EPS) + ADAM_WD * w_ref[...])
        mo_ref[...] = m_n
        vo_ref[...] = v_n

    spec = pl.BlockSpec((tr, c), lambda i: (i, 0))
    shp = jax.ShapeDtypeStruct((r, c), f32)
    return pl.pallas_call(
        body, name=name, grid=(r // tr,), in_specs=[pl.BlockSpec((n, tr, c), lambda i: (0, i, 0))] + [spec] * 3,
        out_specs=[spec] * 4, out_shape=[shp] * 4, compiler_params=_cp(1),
    )(parts, w, m, v)


def _two_d(shape):
    return (math.prod(shape[:-1]), shape[-1])


def _pack_rows(a):
    n = a.size
    rows = -(-n // (8 * PACK_C)) * 8
    return jnp.pad(a.reshape(-1), (0, rows * PACK_C - n)).reshape(rows, PACK_C)


def _pack(parts):
    return jnp.concatenate([_pack_rows(p) for p in parts], axis=0)


def _unpack(packed, shapes):
    lead = packed.shape[:-2]
    out, off = [], 0
    for s in shapes:
        n = math.prod(s)
        rows = -(-n // (8 * PACK_C)) * 8
        part = packed[..., off:off + rows, :].reshape(lead + (rows * PACK_C,))
        out.append(part[..., :n].reshape(lead + tuple(s)))
        off += rows
    return out


def _unshard(g8, axis):
    local = g8.shape[1:]
    moved = jnp.moveaxis(g8, 0, axis)
    return moved.reshape(local[:axis] + (N_DEV * local[axis],) + local[axis + 1:])


def _shard8(full, axis):
    s = full.shape
    split = full.reshape(s[:axis] + (N_DEV, s[axis] // N_DEV) + s[axis + 1:])
    return jnp.moveaxis(split, axis, 0)


_BIG = dict(ffn_w_in=3, ffn_w_out=2, pool_w=2, fox_w_in=2, fox_w_o=1, s5_w_glu=1, conv_w_in=2, conv_w_out=1)
_SMALL_SHARDED = dict(norm_g=2, s5_d=1, conv_w=3)
_REPLICATED = ("ada_b", "pool_scale", "fox_b_f", "fox_q_gain", "fox_k_gain", "s5_lam_re", "s5_lam_im", "s5_log_dt",
               "s5_b_re", "s5_b_im", "s5_c_re", "s5_c_im")
_WEIGHTS = ("ada_w", "ada_b", "norm_g", "ffn_w_in", "ffn_w_out", "pool_w", "pool_scale", "fox_w_in", "fox_b_f",
            "fox_q_gain", "fox_k_gain", "fox_w_o", "s5_lam_re", "s5_lam_im", "s5_log_dt", "s5_b_re", "s5_b_im",
            "s5_c_re", "s5_c_im", "s5_d", "s5_w_glu", "conv_w_in", "conv_w", "conv_w_out")


def _step(x, c, target, w, m, v):
    t = x.shape[1]
    xi_, yi_, ci_ = _me()
    me = _dev_index(xi_, yi_, ci_)

    sm_shapes = [w[n].shape for n in _SMALL_SHARDED]
    small_all = _all_gather([_pack([c] + [w[n] for n in _SMALL_SHARDED])], "gather_small")[0]
    gathered = _unpack(small_all, [c.shape] + sm_shapes)
    c_all = gathered[0][:, 0, :]
    full = {n: _unshard(p, ax) for (n, ax), p in zip(_SMALL_SHARDED.items(), gathered[1:])}

    big_all = _all_gather([w[n].astype(bf16).reshape(_two_d(w[n].shape)) for n in _BIG], "gather_weights")
    gw = dict(zip(_BIG, big_all))
    ffn_w_in = gw["ffn_w_in"].reshape(N_DEV, 2 * DEPTH, D, FFN_HS)
    ffn_w_out = gw["ffn_w_out"].reshape(N_DEV, 2 * DEPTH, D_FF // N_DEV, D)
    pool_w = gw["pool_w"].reshape(N_DEV, 4, POOL_GROUP // N_DEV, POOL_GROUP).transpose(1, 0, 2, 3)
    pool_w = pool_w.reshape(4, POOL_GROUP, POOL_GROUP)
    fox_w_in = jnp.pad(gw["fox_w_in"].transpose(1, 0, 2).reshape(D, FOX_PROJ), ((0, 0), (0, FOX_PROJ_PAD - FOX_PROJ)))
    fox_w_o, s5_w_glu, conv_w_out = (gw[n].reshape(D, D) for n in ("fox_w_o", "s5_w_glu", "conv_w_out"))
    conv_w_in = gw["conv_w_in"]

    def ffn_w(i, f):
        return ffn_w_in, ffn_w_out, 2 * i + f

    c16 = jnp.pad(c_all, ((0, 8), (0, 0)))
    cols = w["ada_w"].shape[2]
    mod_sh = _ada_mod(c16, w["ada_w"])
    mod_all = _all_gather([mod_sh.reshape(DEPTH * 16, cols)], "gather_mod")[0].reshape(N_DEV, DEPTH, 16, cols)
    mod_mine = lax.dynamic_index_in_dim(mod_all, me, axis=2, keepdims=False)
    mod = (mod_mine.transpose(1, 0, 2).reshape(DEPTH, N_DEV * cols) + w["ada_b"]).reshape(DEPTH, 3, 3, D)

    norm_g = full["norm_g"]

    def ln_of(i, sub):
        return (norm_g[i, sub][None], mod[i, sub, 0][None], mod[i, sub, 1][None], mod[i, sub, 2][None])

    fox_b_f = jnp.pad(w["fox_b_f"], ((0, 0), (0, 128 - FOX_HEADS)))
    s5_raw = _s5_params(w["s5_lam_re"][0], w["s5_lam_im"][0], w["s5_log_dt"][0], w["s5_b_re"][0], w["s5_b_im"][0])
    s5_c_re, s5_c_im = w["s5_c_re"][0], w["s5_c_im"][0]
    conv_w = jnp.pad(full["conv_w"][0, :, 0, :], ((0, 5), (0, 0)))

    xs = x[0]
    saved = []
    for i in range(DEPTH):
        xs, s0 = _ffn_fwd(xs, ln_of(i, 0), *ffn_w(i, 0), 0.5)
        if i == 0:
            s1 = xs
            xs = _pool_fwd(xs, ln_of(i, 1), pool_w, w["pool_scale"])
        elif i == 1:
            xs, s1 = _fox_fwd(xs, ln_of(i, 1), fox_w_in, fox_b_f, w["fox_q_gain"], w["fox_k_gain"], fox_w_o)
        elif i == 2:
            xs, s1 = _s5_fwd(xs, ln_of(i, 1), s5_raw, s5_c_re, s5_c_im, full["s5_d"], s5_w_glu)
        else:
            xs, s1 = _convmix_fwd(xs, ln_of(i, 1), conv_w_in, conv_w, conv_w_out)
        xs, s2 = _ffn_fwd(xs, ln_of(i, 2), *ffn_w(i, 1), 0.5)
        saved.append((s0, s1, s2))
    dx, lpart = _loss_head(xs, target[0])

    grads = {}
    dmod = [[None] * 3 for _ in range(DEPTH)]
    dnorm = [[None] * 3 for _ in range(DEPTH)]
    dffn_in = lax.empty(ffn_w_in.shape, f32)
    dffn_out = lax.empty(ffn_w_out.shape, f32)

    def put_ln(i, sub, dln):
        dg, dshift, dscale, dgate = dln
        dnorm[i][sub] = dg
        dmod[i][sub] = jnp.concatenate([dshift, dscale, dgate], axis=0)

    for i in reversed(range(DEPTH)):
        s0, s1, s2 = saved[i]
        dx, dln, dffn_in, dffn_out = _ffn_bwd(dx, s2, ln_of(i, 2), *ffn_w(i, 1), 0.5, dffn_in, dffn_out)
        put_ln(i, 2, dln)
        if i == 0:
            dx, dln, dpw, dps = _pool_bwd(dx, s1, ln_of(i, 1), pool_w, w["pool_scale"])
            dpw = dpw.reshape(4, N_DEV, POOL_GROUP // N_DEV, POOL_GROUP).transpose(1, 0, 2, 3)
            grads.update(pool_w=dpw.reshape(N_DEV, 4 * POOL_GROUP // N_DEV, POOL_GROUP), pool_scale=dps)
        elif i == 1:
            dx, dln, dwi, dbf, dqg, dkg, dwo = _fox_bwd(
                dx, s1, ln_of(i, 1), fox_w_in, fox_b_f, w["fox_q_gain"], w["fox_k_gain"], fox_w_o)
            dwi = dwi[:, :FOX_PROJ].reshape(D, N_DEV, FOX_PROJ // N_DEV).transpose(1, 0, 2)
            grads.update(fox_w_in=dwi, fox_b_f=dbf[:, :FOX_HEADS], fox_q_gain=dqg, fox_k_gain=dkg,
                         fox_w_o=dwo.reshape(N_DEV, D // N_DEV, D))
        elif i == 2:
            dx, dln, gs5 = _s5_bwd(dx, s1, ln_of(i, 1), s5_raw, full["s5_d"], s5_w_glu)
            gs5["s5_w_glu"] = gs5["s5_w_glu"].reshape(N_DEV, D // N_DEV, D)
            grads.update(gs5)
        else:
            dx, dln, dwi, dcw, dwo = _convmix_bwd(dx, s1, ln_of(i, 1), conv_w_in, conv_w, conv_w_out)
            grads.update(conv_w_in=dwi, conv_w=dcw[None, :, None, :], conv_w_out=dwo.reshape(N_DEV, D // N_DEV, D))
        put_ln(i, 1, dln)
        dx, dln, dffn_in, dffn_out = _ffn_bwd(dx, s0, ln_of(i, 0), *ffn_w(i, 0), 0.5, dffn_in, dffn_out)
        put_ln(i, 0, dln)
    grads["ffn_w_in"] = dffn_in.reshape(N_DEV, 2 * DEPTH * D, FFN_HS)
    grads["ffn_w_out"] = dffn_out.reshape(N_DEV, 2 * DEPTH * D_FF // N_DEV, D)
    grads["norm_g"] = jnp.stack([jnp.concatenate(r, axis=0) for r in dnorm])
    dmod_mine = jnp.stack([jnp.stack(r) for r in dmod]).reshape(DEPTH, 9 * D)

    small_names = list(_REPLICATED[1:]) + list(_SMALL_SHARDED)
    small_parts = [dmod_mine] + [grads[n] for n in small_names] + [lpart[:, 0:1]]
    small_g = _all_gather([_pack(small_parts)], "gather_grads")[0]
    small_sum = _sum_sources(small_g, "sum_small")
    summed = dict(zip(["ada_b"] + small_names + ["loss"], _unpack(small_sum, [p.shape for p in small_parts])))
    loss = summed.pop("loss")[0, 0]
    for n, ax in _SMALL_SHARDED.items():
        local = w[n].shape[ax]
        summed[n] = lax.dynamic_slice_in_dim(summed[n], me * local, local, axis=ax)

    dmod_all = small_g[:, :DEPTH * 9].reshape(N_DEV, DEPTH, 9 * D)
    dmod_cols = lax.dynamic_slice_in_dim(dmod_all, me * cols, cols, axis=2)
    ada_g = _ada_grad(c16, jnp.pad(dmod_cols.transpose(1, 0, 2), ((0, 0), (0, 8), (0, 0))))

    big_parts = [grads[n] for n in _BIG]
    from_sibling = _exchange_pair(big_parts, "exchange_pair")
    chip_sums = [_pair_sum(p, s) for p, s in zip(big_parts, from_sibling)]
    big_landed = dict(zip(_BIG, _exchange_chips(chip_sums, "exchange_chips")))

    grad, delta, new_m, new_v = {}, {}, {}, {}
    big_landed["ada_w"] = ada_g[None]
    for n, parts in big_landed.items():
        view = _two_d(w[n].shape)
        outs = _adamw(parts.reshape((parts.shape[0],) + view), w[n].reshape(view), m[n].reshape(view),
                      v[n].reshape(view), "adamw_" + n)
        grad[n], delta[n], new_m[n], new_v[n] = (a.reshape(w[n].shape) for a in outs)
    small = [n for n in _WEIGHTS if n not in big_landed]
    small_shapes = [w[n].shape for n in small]
    pk = lambda d: _pack([d[n] for n in small])
    outs = _adamw(pk(summed)[None], pk(w), pk(m), pk(v), "adamw_small")
    for dst, packed in zip((grad, delta, new_m, new_v), outs):
        dst.update(zip(small, _unpack(packed, small_shapes)))
    return (loss, dx[None], *[grad[n] for n in _WEIGHTS], *[delta[n] for n in _WEIGHTS],
            *[new_m[n] for n in _WEIGHTS], *[new_v[n] for n in _WEIGHTS])


def kernel(x, c, ada_w, ada_b, norm_g, ffn_w_in, ffn_w_out, pool_w, pool_scale, fox_w_in, fox_b_f, fox_q_gain, fox_k_gain, fox_w_o, s5_lam_re, s5_lam_im, s5_log_dt, s5_b_re, s5_b_im, s5_c_re, s5_c_im, s5_d, s5_w_glu, conv_w_in, conv_w, conv_w_out, loss_target, m_ada_w, m_ada_b, m_norm_g, m_ffn_w_in, m_ffn_w_out, m_pool_w, m_pool_scale, m_fox_w_in, m_fox_b_f, m_fox_q_gain, m_fox_k_gain, m_fox_w_o, m_s5_lam_re, m_s5_lam_im, m_s5_log_dt, m_s5_b_re, m_s5_b_im, m_s5_c_re, m_s5_c_im, m_s5_d, m_s5_w_glu, m_conv_w_in, m_conv_w, m_conv_w_out, v_ada_w, v_ada_b, v_norm_g, v_ffn_w_in, v_ffn_w_out, v_pool_w, v_pool_scale, v_fox_w_in, v_fox_b_f, v_fox_q_gain, v_fox_k_gain, v_fox_w_o, v_s5_lam_re, v_s5_lam_im, v_s5_log_dt, v_s5_b_re, v_s5_b_im, v_s5_c_re, v_s5_c_im, v_s5_d, v_s5_w_glu, v_conv_w_in, v_conv_w, v_conv_w_out):
    ws = (ada_w, ada_b, norm_g, ffn_w_in, ffn_w_out, pool_w, pool_scale, fox_w_in, fox_b_f, fox_q_gain, fox_k_gain,
          fox_w_o, s5_lam_re, s5_lam_im, s5_log_dt, s5_b_re, s5_b_im, s5_c_re, s5_c_im, s5_d, s5_w_glu, conv_w_in,
          conv_w, conv_w_out)
    ms = (m_ada_w, m_ada_b, m_norm_g, m_ffn_w_in, m_ffn_w_out, m_pool_w, m_pool_scale, m_fox_w_in, m_fox_b_f,
          m_fox_q_gain, m_fox_k_gain, m_fox_w_o, m_s5_lam_re, m_s5_lam_im, m_s5_log_dt, m_s5_b_re, m_s5_b_im,
          m_s5_c_re, m_s5_c_im, m_s5_d, m_s5_w_glu, m_conv_w_in, m_conv_w, m_conv_w_out)
    vs = (v_ada_w, v_ada_b, v_norm_g, v_ffn_w_in, v_ffn_w_out, v_pool_w, v_pool_scale, v_fox_w_in, v_fox_b_f,
          v_fox_q_gain, v_fox_k_gain, v_fox_w_o, v_s5_lam_re, v_s5_lam_im, v_s5_log_dt, v_s5_b_re, v_s5_b_im,
          v_s5_c_re, v_s5_c_im, v_s5_d, v_s5_w_glu, v_conv_w_in, v_conv_w, v_conv_w_out)
    return _step(x, c, loss_target, dict(zip(_WEIGHTS, ws)), dict(zip(_WEIGHTS, ms)), dict(zip(_WEIGHTS, vs)))
```

```python
import math

import jax
import jax.numpy as jnp
from jax import lax
from jax.experimental import pallas as pl
from jax.experimental.pallas import tpu as pltpu

f32 = jnp.float32
bf16 = jnp.bfloat16

D = 1024
D_FF = 2816
FFN_HS = 2 * D_FF // 8
FFN_SLABS = 4
DEPTH = 4
NORM_EPS = 1e-6
N_DEV = 8
AXES = ("x", "y", "c")
POOL_WINDOWS = (2, 4, 8, 16)
POOL_GROUP = 256
POOL_HALO = 16
FOX_HEADS = 16
FOX_HEAD_DIM = 64
FOX_PROJ = 3088
FOX_PROJ_PAD = 3200
S5_GROUPS = 64
S5_GROUP = 16
S5_STATE = 64
S5_NSTATE = S5_GROUPS * S5_STATE
S5_BLOCKS = 8
S5_BCH = 128
S5_BST = 512
CONV_HALO = 8
ADAM_LR = 0.001
ADAM_B1 = 0.9
ADAM_B2 = 0.999
ADAM_EPS = 1e-08
ADAM_WD = 0.01
ADAM_STEP = 10
VMEM_LIMIT = 56 * 1024 * 1024
PACK_C = 1024

_ARB = "arbitrary"


def _cp(n_axes):
    return pltpu.CompilerParams(dimension_semantics=(_ARB,) * n_axes, vmem_limit_bytes=VMEM_LIMIT)


def _pick(n, prefs):
    for c in prefs:
        if n % c == 0:
            return c
    return n


_DN = {"nn": (((1,), (0,)), ((), ())), "nt": (((1,), (1,)), ((), ())), "tn": (((0,), (0,)), ((), ()))}


def _mm(a, b, mode, out_dtype, name):
    if mode == "nn":
        (m, k), (_, n) = a.shape, b.shape
    elif mode == "nt":
        (m, k), (n, _) = a.shape, b.shape
    else:
        (k, m), (_, n) = a.shape, b.shape
    big = (1408, 1024, 640, 512, 384, 256, 128)
    tm = _pick(m, big) if mode == "tn" else _pick(m, (1024, 512, 256, 128))
    tn = _pick(n, big)
    if mode == "tn":
        tk = _pick(k, (512, 256, 128))
    else:
        tk = k if k <= 3200 else _pick(k, (2816, 2048, 1024, 512))
    nk = k // tk
    dn = _DN[mode]

    def body(a_ref, b_ref, o_ref, acc_ref):
        p = lax.dot_general(a_ref[...], b_ref[...], dn, preferred_element_type=f32)
        if nk == 1:
            o_ref[...] = p.astype(out_dtype)
        else:
            kk = pl.program_id(2)

            @pl.when(kk == 0)
            def _():
                acc_ref[...] = p

            @pl.when(kk > 0)
            def _():
                acc_ref[...] += p

            @pl.when(kk == nk - 1)
            def _():
                o_ref[...] = acc_ref[...].astype(out_dtype)

    if mode == "nn":
        a_spec = pl.BlockSpec((tm, tk), lambda i, j, kk: (i, kk))
        b_spec = pl.BlockSpec((tk, tn), lambda i, j, kk: (kk, j))
    elif mode == "nt":
        a_spec = pl.BlockSpec((tm, tk), lambda i, j, kk: (i, kk))
        b_spec = pl.BlockSpec((tn, tk), lambda i, j, kk: (j, kk))
    else:
        a_spec = pl.BlockSpec((tk, tm), lambda i, j, kk: (kk, i))
        b_spec = pl.BlockSpec((tk, tn), lambda i, j, kk: (kk, j))
    acc_shape = (tm, tn) if nk > 1 else (8, 128)
    return pl.pallas_call(
        body, name=name, grid=(m // tm, n // tn, nk),
        in_specs=[a_spec, b_spec], out_specs=pl.BlockSpec((tm, tn), lambda i, j, kk: (i, j)),
        out_shape=jax.ShapeDtypeStruct((m, n), out_dtype),
        scratch_shapes=[pltpu.VMEM(acc_shape, f32)],
        compiler_params=_cp(3),
    )(a, b)


def _mmx(a, b, mode, name, grid, a_spec, b_spec, o_spec, out_shape, into=None):
    nk = grid[2]
    dn = _DN[mode]
    out_dtype = out_shape.dtype
    a_blk = (math.prod(a_spec.block_shape[:-1]), a_spec.block_shape[-1])
    b_blk = (math.prod(b_spec.block_shape[:-1]), b_spec.block_shape[-1])
    o_blk = (math.prod(o_spec.block_shape[:-1]), o_spec.block_shape[-1])

    def body(a_ref, b_ref, *rest):
        o_ref, acc_ref = rest[-2:]
        p = lax.dot_general(a_ref[...].reshape(a_blk), b_ref[...].reshape(b_blk), dn, preferred_element_type=f32)
        if nk == 1:
            o_ref[...] = p.reshape(o_ref.shape).astype(out_dtype)
        else:
            kk = pl.program_id(2)

            @pl.when(kk == 0)
            def _():
                acc_ref[...] = p

            @pl.when(kk > 0)
            def _():
                acc_ref[...] += p

            @pl.when(kk == nk - 1)
            def _():
                o_ref[...] = acc_ref[...].reshape(o_ref.shape).astype(out_dtype)

    extra = {} if into is None else dict(input_output_aliases={2: 0})
    operands = (a, b) if into is None else (a, b, into)
    return pl.pallas_call(
        body, name=name, grid=grid, in_specs=[a_spec, b_spec] + ([] if into is None else [_ANY]),
        out_specs=o_spec, out_shape=out_shape,
        scratch_shapes=[pltpu.VMEM(o_blk if nk > 1 else (8, 128), f32)], compiler_params=_cp(3), **extra,
    )(*operands)


def _tok_tile(t):
    return min(t, 512)


def _tok_spec(tm, c, nt=None, reverse=False):
    if reverse:
        return pl.BlockSpec((tm, c), lambda i: (nt - 1 - i, 0))
    return pl.BlockSpec((tm, c), lambda i: (i, 0))


def _row_spec(c, rows=1):
    return pl.BlockSpec((rows, c), lambda i: (0, 0))


def _acc_add(first, ref, val):
    @pl.when(first)
    def _():
        ref[...] = val

    @pl.when(jnp.logical_not(first))
    def _():
        ref[...] += val


def _adaln(x, g, scale, shift):
    y = x * lax.rsqrt(jnp.mean(x * x, axis=-1, keepdims=True) + NORM_EPS)
    return (y * g) * (1.0 + scale) + shift


def _ln_fwd(x, g, scale, shift):
    t = x.shape[0]
    tm = _tok_tile(t)

    def body(x_ref, g_ref, sc_ref, sh_ref, h_ref):
        h_ref[...] = _adaln(x_ref[...], g_ref[...], sc_ref[...], sh_ref[...]).astype(bf16)

    return pl.pallas_call(
        body, name="ln_fwd", grid=(t // tm,),
        in_specs=[_tok_spec(tm, D), _row_spec(D), _row_spec(D), _row_spec(D)],
        out_specs=_tok_spec(tm, D), out_shape=jax.ShapeDtypeStruct((t, D), bf16),
        compiler_params=_cp(1),
    )(x, g, scale, shift)


def _ln_bwd(x, g, scale, shift, dh, dxn):
    t = x.shape[0]
    tm = _tok_tile(t)

    def body(x_ref, g_ref, sc_ref, sh_ref, dh_ref, dxn_ref, dx_ref, dg_ref, dsc_ref, dsh_ref):
        _, vjp = jax.vjp(_adaln, x_ref[...], g_ref[...], sc_ref[...], sh_ref[...])
        dx, dg, dsc, dsh = vjp(dh_ref[...])
        dx_ref[...] = dxn_ref[...] + dx
        first = pl.program_id(0) == 0
        _acc_add(first, dg_ref, dg)
        _acc_add(first, dsc_ref, dsc)
        _acc_add(first, dsh_ref, dsh)

    row = jax.ShapeDtypeStruct((1, D), f32)
    return pl.pallas_call(
        body, name="ln_bwd", grid=(t // tm,),
        in_specs=[_tok_spec(tm, D), _row_spec(D), _row_spec(D), _row_spec(D), _tok_spec(tm, D), _tok_spec(tm, D)],
        out_specs=[_tok_spec(tm, D), _row_spec(D), _row_spec(D), _row_spec(D)],
        out_shape=[jax.ShapeDtypeStruct((t, D), f32), row, row, row],
        compiler_params=_cp(1),
    )(x, g, scale, shift, dh, dxn)


def _swiglu(g, u):
    return jax.nn.silu(g) * u


def _ffn_tile(t):
    return min(t, 512)


def _ffn_specs(tm, idx, rows):
    return (pl.BlockSpec((2, 1, 1, D, FFN_HS), lambda i, q: (0, q, idx, 0, 0)),
            pl.BlockSpec((2, 1, rows, D), lambda i, q: (q, idx, 0, 0)),
            pl.BlockSpec((2, 1, tm, FFN_HS), lambda i, q: (0, q, i, 0)),
            pl.BlockSpec((tm, D), lambda i, q: (i, 0)),
            pl.BlockSpec((1, D), lambda i, q: (0, 0)))


def _ffn_up(x, h, gate, w_in, w_out, idx, coef):
    t = h.shape[0]
    tm = _ffn_tile(t)
    rows = w_out.shape[2]

    def body(h_ref, wi_ref, wo_ref, x_ref, gt_ref, gu_ref, a_ref, o_ref, xn_ref):
        q = pl.program_id(1)
        hv = h_ref[...]
        g = jnp.dot(hv, wi_ref[0, 0, 0], preferred_element_type=f32)
        u = jnp.dot(hv, wi_ref[1, 0, 0], preferred_element_type=f32)
        gu_ref[0, 0] = g.astype(bf16)
        gu_ref[1, 0] = u.astype(bf16)
        a = _swiglu(g, u).astype(bf16)
        a_ref[0] = a
        _acc_add(q == 0, o_ref, jnp.dot(a, wo_ref[...].reshape(FFN_HS, D), preferred_element_type=f32))

        @pl.when(q == FFN_SLABS - 1)
        def _():
            xn_ref[...] = x_ref[...] + (coef * (1.0 + gt_ref[...])) * o_ref[...]

    wi, wo, slab, tok, row = _ffn_specs(tm, idx, rows)
    tok_f32 = jax.ShapeDtypeStruct((t, D), f32)
    return pl.pallas_call(
        body, name="ffn_up", grid=(t // tm, FFN_SLABS), in_specs=[tok, wi, wo, tok, row],
        out_specs=[slab, pl.BlockSpec((1, tm, FFN_HS), lambda i, q: (q, i, 0)), tok, tok],
        out_shape=[jax.ShapeDtypeStruct((2, FFN_SLABS, t, FFN_HS), bf16),
                   jax.ShapeDtypeStruct((FFN_SLABS, t, FFN_HS), bf16), tok_f32, tok_f32],
        compiler_params=_cp(2),
    )(h, w_in, w_out, x, gate)


def _ffn_dgu(do, w_out, idx, gu):
    t = do.shape[0]
    tm = _mm_tile(t)

    def body(do_ref, wo_ref, gu_ref, o_ref):
        da = lax.dot_general(do_ref[...], wo_ref[...].reshape(FFN_HS, D), _DN["nt"], preferred_element_type=f32)
        _, vjp = jax.vjp(_swiglu, gu_ref[0, 0].astype(f32), gu_ref[1, 0].astype(f32))
        dg, du = vjp(da)
        o_ref[0, 0] = dg.astype(bf16)
        o_ref[1, 0] = du.astype(bf16)

    _, wo, slab, tok, _ = _ffn_specs(tm, idx, w_out.shape[2])
    return pl.pallas_call(
        body, name="ffn_dgu", grid=(t // tm, FFN_SLABS), in_specs=[tok, wo, slab],
        out_specs=slab, out_shape=jax.ShapeDtypeStruct((2, FFN_SLABS, t, FFN_HS), bf16),
        compiler_params=_cp(2),
    )(do, w_out, gu)


def _ffn_dh(dgu, w_in, idx):
    t = dgu.shape[2]
    tm = _mm_tile(t)

    def body(d_ref, wi_ref, o_ref):
        p = (lax.dot_general(d_ref[0, 0], wi_ref[0, 0, 0], _DN["nt"], preferred_element_type=f32)
             + lax.dot_general(d_ref[1, 0], wi_ref[1, 0, 0], _DN["nt"], preferred_element_type=f32))
        _acc_add(pl.program_id(1) == 0, o_ref, p)

    wi, _, slab, tok, _ = _ffn_specs(tm, idx, 8)
    return pl.pallas_call(
        body, name="ffn_dh", grid=(t // tm, FFN_SLABS), in_specs=[slab, wi], out_specs=tok,
        out_shape=jax.ShapeDtypeStruct((t, D), f32), compiler_params=_cp(2),
    )(dgu, w_in)


def _resid_fwd(x, o, gate, coef):
    t = x.shape[0]
    tm = _tok_tile(t)

    def body(x_ref, o_ref, gt_ref, y_ref):
        y_ref[...] = x_ref[...] + (coef * (1.0 + gt_ref[...])) * o_ref[...]

    return pl.pallas_call(
        body, name="resid_fwd", grid=(t // tm,),
        in_specs=[_tok_spec(tm, D), _tok_spec(tm, D), _row_spec(D)],
        out_specs=_tok_spec(tm, D), out_shape=jax.ShapeDtypeStruct((t, D), f32),
        compiler_params=_cp(1),
    )(x, o, gate)


def _resid_bwd(dxn, o, gate, coef):
    t = dxn.shape[0]
    tm = _tok_tile(t)

    def body(dxn_ref, o_ref, gt_ref, do_ref, dgt_ref):
        dxn_v = dxn_ref[...]
        do_ref[...] = ((coef * (1.0 + gt_ref[...])) * dxn_v).astype(bf16)
        _acc_add(pl.program_id(0) == 0, dgt_ref, coef * jnp.sum(dxn_v * o_ref[...], axis=0, keepdims=True))

    return pl.pallas_call(
        body, name="resid_bwd", grid=(t // tm,),
        in_specs=[_tok_spec(tm, D), _tok_spec(tm, D), _row_spec(D)],
        out_specs=[_tok_spec(tm, D), _row_spec(D)],
        out_shape=[jax.ShapeDtypeStruct((t, D), bf16), jax.ShapeDtypeStruct((1, D), f32)],
        compiler_params=_cp(1),
    )(dxn, o, gate)


def _loss_head(y, target):
    t = y.shape[0]
    tm = _tok_tile(t)

    def body(y_ref, t_ref, dy_ref, l_ref):
        err = y_ref[...] - t_ref[...]
        dy_ref[...] = err * (1.0 / D)
        part = jnp.sum(jnp.sum(err * err, axis=0, keepdims=True), axis=1, keepdims=True) * (0.5 / D)
        _acc_add(pl.program_id(0) == 0, l_ref, jnp.broadcast_to(part, (1, 128)))

    return pl.pallas_call(
        body, name="loss_head", grid=(t // tm,),
        in_specs=[_tok_spec(tm, D), _tok_spec(tm, D)],
        out_specs=[_tok_spec(tm, D), _row_spec(128)],
        out_shape=[jax.ShapeDtypeStruct((t, D), f32), jax.ShapeDtypeStruct((1, 128), f32)],
        compiler_params=_cp(1),
    )(y, target)


def _mm_tile(t):
    return min(t, 1024)


def _ffn_fwd(x, ln, w_in, w_out, idx, coef):
    g, shift, scale, gate = ln
    h = _ln_fwd(x, g, scale, shift)
    gu, a, o, xn = _ffn_up(x, h, gate, w_in.reshape((2, FFN_SLABS) + w_in.shape[1:]), w_out, idx, coef)
    return xn, (x, h, gu, a, o)


def _ffn_bwd(dxn, saved, ln, w_in, w_out, idx, coef, dw_in, dw_out):
    x, h, gu, a, o = saved
    g, shift, scale, gate = ln
    t = x.shape[0]
    tm = _mm_tile(t)
    tk = _pick(t, (2048, 1024, 512, 256, 128))
    rows = w_out.shape[2]
    sds = jax.ShapeDtypeStruct
    do, dgate = _resid_bwd(dxn, o, gate, coef)
    dgu = _ffn_dgu(do, w_out, idx, gu)
    dh = _ffn_dh(dgu, w_in.reshape((2, FFN_SLABS) + w_in.shape[1:]), idx)
    dgu = dgu.reshape(N_DEV, t, FFN_HS)
    dw_out = _mmx(a, do, "tn", "ffn_dwout", (FFN_SLABS, 1, t // tk),
                  pl.BlockSpec((1, tk, FFN_HS), lambda i, j, k: (i, k, 0)),
                  pl.BlockSpec((tk, D), lambda i, j, k: (k, 0)),
                  pl.BlockSpec((2, 1, rows, D), lambda i, j, k: (i, idx, 0, 0)),
                  sds(dw_out.shape, f32), into=dw_out)
    dw_in = _mmx(h, dgu, "tn", "ffn_dwin", (N_DEV, 1, t // tk),
                 pl.BlockSpec((tk, D), lambda i, j, k: (k, 0)),
                 pl.BlockSpec((1, tk, FFN_HS), lambda i, j, k: (i, k, 0)),
                 pl.BlockSpec((1, 1, D, FFN_HS), lambda i, j, k: (i, idx, 0, 0)),
                 sds(dw_in.shape, f32), into=dw_in)
    dx, dg, dscale, dshift = _ln_bwd(x, g, scale, shift, dh, dxn)
    return dx, (dg, dshift, dscale, dgate), dw_in, dw_out


def _roll_rows(a, k):
    n = a.shape[0]
    return pltpu.roll(a, k % n, 0)


def _pool_windows(hx, first_row, reverse):
    outs = []
    for gi, w in enumerate(POOL_WINDOWS):
        acc = hx[:, gi * POOL_GROUP:(gi + 1) * POOL_GROUP]
        k = 1
        while k < w:
            acc = acc + _roll_rows(acc, -k if reverse else k)
            k *= 2
        outs.append(acc)
    return outs


def _pool_cnt(t_idx, w):
    return jnp.minimum(t_idx + 1, w).astype(f32)


def _pool_pooled(x_ref, xp_ref, g, scale, shift, i, tm):
    h = _adaln(x_ref[...], g, scale, shift)
    hp = _adaln(xp_ref[...], g, scale, shift)
    hp = jnp.where(i == 0, 0.0, hp)
    hx = jnp.concatenate([hp, h], axis=0)
    sums = _pool_windows(hx, 0, False)
    t_idx = i * tm + lax.broadcasted_iota(jnp.int32, (tm, 1), 0)
    pooled = []
    for gi, w in enumerate(POOL_WINDOWS):
        s = sums[gi][POOL_HALO:, :]
        pooled.append(s / _pool_cnt(t_idx, w) - h[:, gi * POOL_GROUP:(gi + 1) * POOL_GROUP])
    return h, pooled


def _pool_specs(t, tm):
    per = tm // POOL_HALO
    prev = pl.BlockSpec((POOL_HALO, D), lambda i: (jnp.maximum(i * per - 1, 0), 0))
    return [_tok_spec(tm, D), prev, _row_spec(D), _row_spec(D), _row_spec(D),
            pl.BlockSpec((4, POOL_GROUP, POOL_GROUP), lambda i: (0, 0, 0)), _row_spec(D), _row_spec(D)]


def _pool_fwd(x, ln, w, pscale):
    g, shift, scale, gate = ln
    t = x.shape[0]
    tm = _tok_tile(t)

    def body(x_ref, xp_ref, g_ref, sc_ref, sh_ref, w_ref, ps_ref, gt_ref, y_ref):
        i = pl.program_id(0)
        _, pooled = _pool_pooled(x_ref, xp_ref, g_ref[...], sc_ref[...], sh_ref[...], i, tm)
        mixed = [jnp.dot(pooled[gi].astype(bf16), w_ref[gi], preferred_element_type=f32) for gi in range(4)]
        y = jnp.concatenate(mixed, axis=1) * ps_ref[...]
        y_ref[...] = x_ref[...] + (1.0 + gt_ref[...]) * y

    return pl.pallas_call(
        body, name="pool_fwd", grid=(t // tm,), in_specs=_pool_specs(t, tm),
        out_specs=_tok_spec(tm, D), out_shape=jax.ShapeDtypeStruct((t, D), f32),
        compiler_params=_cp(1),
    )(x, x, g, scale, shift, w, pscale, gate)


def _pool_bwd(dxn, x, ln, w, pscale):
    g, shift, scale, gate = ln
    t = x.shape[0]
    tm = _tok_tile(t)
    nt = t // tm
    per = tm // POOL_HALO

    def body_a(x_ref, xp_ref, g_ref, sc_ref, sh_ref, w_ref, ps_ref, gt_ref, dxn_ref,
               dp_ref, dw_ref, dps_ref, dgt_ref):
        i = pl.program_id(0)
        first = i == 0
        _, pooled = _pool_pooled(x_ref, xp_ref, g_ref[...], sc_ref[...], sh_ref[...], i, tm)
        dxn_v = dxn_ref[...]
        dy = (1.0 + gt_ref[...]) * dxn_v
        dmixed = dy * ps_ref[...]
        mixed, dps = [], []
        for gi in range(4):
            sl = slice(gi * POOL_GROUP, (gi + 1) * POOL_GROUP)
            pb = pooled[gi].astype(bf16)
            dmb = dmixed[:, sl].astype(bf16)
            mixed.append(jnp.dot(pb, w_ref[gi], preferred_element_type=f32))
            dp_ref[:, sl] = lax.dot_general(dmb, w_ref[gi], _DN["nt"], preferred_element_type=f32)
            dwg = lax.dot_general(pb, dmb, _DN["tn"], preferred_element_type=f32)

            @pl.when(first)
            def _():
                dw_ref[gi] = dwg

            @pl.when(jnp.logical_not(first))
            def _():
                dw_ref[gi] += dwg
        mixed = jnp.concatenate(mixed, axis=1)
        _acc_add(first, dps_ref, jnp.sum(dy * mixed, axis=0, keepdims=True))
        _acc_add(first, dgt_ref, jnp.sum(dxn_v * (mixed * ps_ref[...]), axis=0, keepdims=True))

    row = jax.ShapeDtypeStruct((1, D), f32)
    dpooled, dw, dps, dgate = pl.pallas_call(
        body_a, name="pool_bwd_a", grid=(nt,), in_specs=_pool_specs(t, tm) + [_tok_spec(tm, D)],
        out_specs=[_tok_spec(tm, D), pl.BlockSpec((4, POOL_GROUP, POOL_GROUP), lambda i: (0, 0, 0)),
                   _row_spec(D), _row_spec(D)],
        out_shape=[jax.ShapeDtypeStruct((t, D), f32), jax.ShapeDtypeStruct((4, POOL_GROUP, POOL_GROUP), f32), row, row],
        compiler_params=_cp(1),
    )(x, x, g, scale, shift, w, pscale, gate, dxn)

    def body_b(dp_ref, dpn_ref, x_ref, g_ref, sc_ref, sh_ref, dxn_ref, dx_ref, dg_ref, dsc_ref, dsh_ref):
        i = pl.program_id(0)
        dp = dp_ref[...]
        dpn = jnp.where(i == nt - 1, 0.0, dpn_ref[...])
        t_idx = i * tm + lax.broadcasted_iota(jnp.int32, (tm + POOL_HALO, 1), 0)
        ex = jnp.concatenate([dp, dpn], axis=0)
        parts = []
        for gi, w_ in enumerate(POOL_WINDOWS):
            parts.append(ex[:, gi * POOL_GROUP:(gi + 1) * POOL_GROUP] / _pool_cnt(t_idx, w_))
        sums = _pool_windows(jnp.concatenate(parts, axis=1), 0, True)
        dh = jnp.concatenate([s[:tm, :] for s in sums], axis=1) - dp
        _, vjp = jax.vjp(_adaln, x_ref[...], g_ref[...], sc_ref[...], sh_ref[...])
        dx, dg, dsc, dsh = vjp(dh)
        dx_ref[...] = dxn_ref[...] + dx
        first = i == 0
        _acc_add(first, dg_ref, dg)
        _acc_add(first, dsc_ref, dsc)
        _acc_add(first, dsh_ref, dsh)

    nxt = pl.BlockSpec((POOL_HALO, D), lambda i: (jnp.minimum((i + 1) * per, t // POOL_HALO - 1), 0))
    dx, dg, dscale, dshift = pl.pallas_call(
        body_b, name="pool_bwd_b", grid=(nt,),
        in_specs=[_tok_spec(tm, D), nxt, _tok_spec(tm, D), _row_spec(D), _row_spec(D), _row_spec(D), _tok_spec(tm, D)],
        out_specs=[_tok_spec(tm, D), _row_spec(D), _row_spec(D), _row_spec(D)],
        out_shape=[jax.ShapeDtypeStruct((t, D), f32), row, row, row],
        compiler_params=_cp(1),
    )(dpooled, dpooled, x, g, scale, shift, dxn)
    return dx, (dg, dshift, dscale, dgate), dw, dps


def _conv_taps(czx, cw):
    return cw[0:1, :] * _roll_rows(czx, 2) + cw[1:2, :] * _roll_rows(czx, 1) + cw[2:3, :] * czx


def _conv_fwd(p, cw):
    t = p.shape[0]
    tm = _tok_tile(t)
    per = tm // CONV_HALO

    def body(p_ref, pp_ref, cw_ref, q_ref):
        i = pl.program_id(0)
        cz = p_ref[:, D:2 * D] * p_ref[:, 2 * D:3 * D]
        czp = jnp.where(i == 0, 0.0, pp_ref[:, D:2 * D] * pp_ref[:, 2 * D:3 * D])
        conv = _conv_taps(jnp.concatenate([czp, cz], axis=0), cw_ref[...])[CONV_HALO:, :]
        q_ref[...] = (p_ref[:, 0:D] * conv).astype(bf16)

    prev = pl.BlockSpec((CONV_HALO, 3 * D), lambda i: (jnp.maximum(i * per - 1, 0), 0))
    return pl.pallas_call(
        body, name="conv_fwd", grid=(t // tm,),
        in_specs=[_tok_spec(tm, 3 * D), prev, _row_spec(D, 8)],
        out_specs=_tok_spec(tm, D), out_shape=jax.ShapeDtypeStruct((t, D), bf16),
        compiler_params=_cp(1),
    )(p, p, cw)


def _conv_bwd(p, cw, dq):
    t = p.shape[0]
    tm = _tok_tile(t)
    nt = t // tm
    per = tm // CONV_HALO

    def body(p_ref, pp_ref, pn_ref, cw_ref, dq_ref, dqn_ref, dp_ref, dcw_ref):
        i = pl.program_id(0)
        cw_v = cw_ref[...]
        b, c, z = p_ref[:, 0:D], p_ref[:, D:2 * D], p_ref[:, 2 * D:3 * D]
        cz = c * z
        czp = jnp.where(i == 0, 0.0, pp_ref[:, D:2 * D] * pp_ref[:, 2 * D:3 * D])
        czx = jnp.concatenate([czp, cz], axis=0)
        conv = _conv_taps(czx, cw_v)[CONV_HALO:, :]
        dq_v = dq_ref[...]
        dconv = dq_v * b
        dconv_n = jnp.where(i == nt - 1, 0.0, dqn_ref[...] * pn_ref[:, 0:D])
        dcx = jnp.concatenate([dconv, dconv_n], axis=0)
        dcz = (cw_v[2:3, :] * dcx + cw_v[1:2, :] * _roll_rows(dcx, -1) + cw_v[0:1, :] * _roll_rows(dcx, -2))[:tm, :]
        dp_ref[:, 0:D] = (dq_v * conv).astype(bf16)
        dp_ref[:, D:2 * D] = (dcz * z).astype(bf16)
        dp_ref[:, 2 * D:3 * D] = (dcz * c).astype(bf16)
        dw2 = jnp.sum(dconv * cz, axis=0, keepdims=True)
        dw1 = jnp.sum(dconv * _roll_rows(czx, 1)[CONV_HALO:, :], axis=0, keepdims=True)
        dw0 = jnp.sum(dconv * _roll_rows(czx, 2)[CONV_HALO:, :], axis=0, keepdims=True)
        _acc_add(i == 0, dcw_ref, jnp.concatenate([dw0, dw1, dw2, jnp.zeros((5, D), f32)], axis=0))

    prev = pl.BlockSpec((CONV_HALO, 3 * D), lambda i: (jnp.maximum(i * per - 1, 0), 0))
    last = t // CONV_HALO - 1
    nxt3 = pl.BlockSpec((CONV_HALO, 3 * D), lambda i: (jnp.minimum((i + 1) * per, last), 0))
    nxt1 = pl.BlockSpec((CONV_HALO, D), lambda i: (jnp.minimum((i + 1) * per, last), 0))
    return pl.pallas_call(
        body, name="conv_bwd", grid=(nt,),
        in_specs=[_tok_spec(tm, 3 * D), prev, nxt3, _row_spec(D, 8), _tok_spec(tm, D), nxt1],
        out_specs=[_tok_spec(tm, 3 * D), _row_spec(D, 8)],
        out_shape=[jax.ShapeDtypeStruct((t, 3 * D), bf16), jax.ShapeDtypeStruct((8, D), f32)],
        compiler_params=_cp(1),
    )(p, p, p, cw, dq, dq)


def _convmix_fwd(x, ln, w_in, cw, w_out):
    g, shift, scale, gate = ln
    t = x.shape[0]
    tm = _mm_tile(t)
    cs = w_in.shape[2]
    h = _ln_fwd(x, g, scale, shift)
    p = _mmx(h, w_in, "nn", "conv_in", (t // tm, N_DEV, 1),
             pl.BlockSpec((tm, D), lambda i, j, k: (i, 0)),
             pl.BlockSpec((1, D, cs), lambda i, j, k: (j, 0, 0)),
             pl.BlockSpec((tm, cs), lambda i, j, k: (i, j)), jax.ShapeDtypeStruct((t, N_DEV * cs), f32))
    q = _conv_fwd(p, cw)
    y = _mm(q, w_out, "nn", f32, "conv_out")
    return _resid_fwd(x, y, gate, 1.0), (x, h, p, q, y)


def _convmix_bwd(dxn, saved, ln, w_in, cw, w_out):
    x, h, p, q, y = saved
    g, shift, scale, gate = ln
    dy, dgate = _resid_bwd(dxn, y, gate, 1.0)
    dq = _mm(dy, w_out, "nt", f32, "conv_dq")
    dw_out = _mm(q, dy, "tn", f32, "conv_dwout")
    dp, dcw = _conv_bwd(p, cw, dq)
    t = x.shape[0]
    tm = _mm_tile(t)
    tk = _pick(t, (512, 256, 128))
    cs = w_in.shape[2]
    dh = _mmx(dp, w_in, "nt", "conv_dh", (t // tm, 1, N_DEV),
              pl.BlockSpec((tm, cs), lambda i, j, k: (i, k)),
              pl.BlockSpec((1, D, cs), lambda i, j, k: (k, 0, 0)),
              pl.BlockSpec((tm, D), lambda i, j, k: (i, 0)), jax.ShapeDtypeStruct((t, D), f32))
    dw_in = _mmx(h, dp, "tn", "conv_dwin", (N_DEV, 1, t // tk),
                 pl.BlockSpec((tk, D), lambda i, j, k: (k, 0)),
                 pl.BlockSpec((tk, cs), lambda i, j, k: (k, i)),
                 pl.BlockSpec((1, D, cs), lambda i, j, k: (i, 0, 0)), jax.ShapeDtypeStruct((N_DEV, D, cs), f32))
    dx, dg, dscale, dshift = _ln_bwd(x, g, scale, shift, dh, dxn)
    return dx, (dg, dshift, dscale, dgate), dw_in, dcw[0:3], dw_out


def _exact_dot(tri, v):
    v1 = v.astype(bf16)
    r1 = v - v1.astype(f32)
    v2 = r1.astype(bf16)
    v3 = (r1 - v2.astype(f32)).astype(bf16)
    d = lambda p: jnp.dot(tri, p, preferred_element_type=f32)
    return d(v1) + d(v2) + d(v3)


def _fox_cumf(fl, b_f):
    t = fl.shape[0]
    tc = min(t, 256)

    def body(fl_ref, b_ref, f_ref, carry_ref):
        i = pl.program_id(0)

        @pl.when(i == 0)
        def _():
            carry_ref[...] = jnp.zeros_like(carry_ref)

        lf = jax.nn.log_sigmoid(fl_ref[...] + b_ref[...])
        r = lax.broadcasted_iota(jnp.int32, (tc, tc), 0)
        c = lax.broadcasted_iota(jnp.int32, (tc, tc), 1)
        tri = (r >= c).astype(bf16)
        fc = _exact_dot(tri, lf) + carry_ref[0:1, :]
        f_ref[...] = fc
        carry_ref[0:1, :] = fc[tc - 1:tc, :]

    return pl.pallas_call(
        body, name="fox_cumf", grid=(t // tc,),
        in_specs=[_tok_spec(tc, 128), _row_spec(128)],
        out_specs=_tok_spec(tc, 128), out_shape=jax.ShapeDtypeStruct((t, 128), f32),
        scratch_shapes=[pltpu.VMEM((8, 128), f32)], compiler_params=_cp(1),
    )(fl, b_f)


def _fox_cumf_bwd(df, fl, b_f):
    t = fl.shape[0]
    tc = min(t, 256)
    nt = t // tc

    def body(df_ref, fl_ref, b_ref, dfl_ref, db_ref, carry_ref):
        i = pl.program_id(0)

        @pl.when(i == 0)
        def _():
            carry_ref[...] = jnp.zeros_like(carry_ref)

        r = lax.broadcasted_iota(jnp.int32, (tc, tc), 0)
        c = lax.broadcasted_iota(jnp.int32, (tc, tc), 1)
        tri = (r <= c).astype(bf16)
        dlf = _exact_dot(tri, df_ref[...]) + carry_ref[0:1, :]
        carry_ref[0:1, :] = dlf[0:1, :]
        dfl = dlf * jax.nn.sigmoid(-(fl_ref[...] + b_ref[...]))
        dfl_ref[...] = dfl
        _acc_add(i == 0, db_ref, jnp.sum(dfl, axis=0, keepdims=True))

    return pl.pallas_call(
        body, name="fox_cumf_bwd", grid=(nt,),
        in_specs=[_tok_spec(tc, 128, nt, True), _tok_spec(tc, 128, nt, True), _row_spec(128)],
        out_specs=[_tok_spec(tc, 128, nt, True), _row_spec(128)],
        out_shape=[jax.ShapeDtypeStruct((t, 128), f32), jax.ShapeDtypeStruct((1, 128), f32)],
        scratch_shapes=[pltpu.VMEM((8, 128), f32)], compiler_params=_cp(1),
    )(df, fl, b_f)


def _head_norm(v, gain, mult):
    return v * lax.rsqrt(jnp.mean(v * v, axis=-1, keepdims=True) + NORM_EPS) * gain * mult


def _fox_qknorm(qkv, q_gain, k_gain):
    t = qkv.shape[2]
    tm = _tok_tile(t)

    def body(q_ref, k_ref, v_ref, qg_ref, kg_ref, o_ref):
        o_ref[0, 0] = _head_norm(q_ref[0, 0], qg_ref[...], FOX_HEAD_DIM ** -0.5).astype(bf16)
        o_ref[1, 0] = _head_norm(k_ref[0, 0], kg_ref[...], 1.0).astype(bf16)
        o_ref[2, 0] = v_ref[0, 0].astype(bf16)

    blk = lambda s: pl.BlockSpec((1, 1, tm, FOX_HEAD_DIM), lambda h, i: (s, h, i, 0))
    gspec = pl.BlockSpec((1, FOX_HEAD_DIM), lambda h, i: (0, 0))
    return pl.pallas_call(
        body, name="fox_qknorm", grid=(FOX_HEADS, t // tm),
        in_specs=[blk(0), blk(1), blk(2), gspec, gspec],
        out_specs=pl.BlockSpec((3, 1, tm, FOX_HEAD_DIM), lambda h, i: (0, h, i, 0)),
        out_shape=jax.ShapeDtypeStruct((3, FOX_HEADS, t, FOX_HEAD_DIM), bf16),
        compiler_params=_cp(2),
    )(qkv, qkv, qkv, q_gain, k_gain)


def _fox_qknorm_bwd(qkv, q_gain, k_gain, dqn, dkn):
    t = qkv.shape[2]
    tm = _tok_tile(t)

    def body(q_ref, k_ref, qg_ref, kg_ref, dqn_ref, dkn_ref, o_ref, dqg_ref, dkg_ref):
        first = jnp.logical_and(pl.program_id(0) == 0, pl.program_id(1) == 0)
        _, vq = jax.vjp(lambda v, gn: _head_norm(v, gn, FOX_HEAD_DIM ** -0.5), q_ref[0, 0], qg_ref[...])
        dq, dqg = vq(dqn_ref[0])
        _, vk = jax.vjp(lambda v, gn: _head_norm(v, gn, 1.0), k_ref[0, 0], kg_ref[...])
        dk, dkg = vk(dkn_ref[0])
        o_ref[0, 0] = dq
        o_ref[1, 0] = dk
        _acc_add(first, dqg_ref, dqg)
        _acc_add(first, dkg_ref, dkg)

    blk = lambda s: pl.BlockSpec((1, 1, tm, FOX_HEAD_DIM), lambda h, i: (s, h, i, 0))
    hblk = pl.BlockSpec((1, tm, FOX_HEAD_DIM), lambda h, i: (h, i, 0))
    gspec = pl.BlockSpec((1, FOX_HEAD_DIM), lambda h, i: (0, 0))
    gshape = jax.ShapeDtypeStruct((1, FOX_HEAD_DIM), f32)
    return pl.pallas_call(
        body, name="fox_qknorm_bwd", grid=(FOX_HEADS, t // tm),
        in_specs=[blk(0), blk(1), gspec, gspec, hblk, hblk],
        out_specs=[pl.BlockSpec((2, 1, tm, FOX_HEAD_DIM), lambda h, i: (0, h, i, 0)), gspec, gspec],
        out_shape=[jax.ShapeDtypeStruct((2, FOX_HEADS, t, FOX_HEAD_DIM), f32), gshape, gshape],
        compiler_params=_cp(2),
    )(qkv, qkv, q_gain, k_gain, dqn, dkn)


def _fox_block(t):
    return min(t, 512)


FOX_STRIP = 64
FOX_HEADS_PER_STEP = 2


def _fox_mask(s, row0):
    r = row0 + lax.broadcasted_iota(jnp.int32, s.shape, 0)
    c = lax.broadcasted_iota(jnp.int32, s.shape, 1)
    return jnp.where(r >= c, s, -jnp.inf)


def _strips(tb):
    ts = min(tb, FOX_STRIP)
    return [(r * ts, slice(r * ts, (r + 1) * ts)) for r in range(tb // ts)]


def _fox_attn_fwd(qkvn, fcol, fref):
    t = qkvn.shape[2]
    tb = _fox_block(t)
    nq = t // tb
    dh = FOX_HEAD_DIM

    hp = FOX_HEADS_PER_STEP

    def body(q_ref, k_ref, v_ref, fc_ref, fr_ref, o_ref, lse_ref):
        i = pl.program_id(1)

        def step(j, carry, diag):
            off = pl.multiple_of(j * tb, tb)
            out = []
            for hh in range(hp):
                m, l, acc = carry[hh]
                kj = k_ref[0, hh, pl.ds(off, tb), :]
                vj = v_ref[0, hh, pl.ds(off, tb), :]
                bias = fr_ref[hh, pl.ds(i, 1), 0:1] - fc_ref[hh, :, pl.ds(off, tb)]
                s = lax.dot_general(q_ref[0, hh], kj, _DN["nt"], preferred_element_type=f32) + bias
                if diag:
                    s = _fox_mask(s, 0)
                m_new = jnp.maximum(m, jnp.max(s, axis=-1, keepdims=True))
                alpha = jnp.exp(m - m_new)
                p = jnp.exp(s - m_new)
                l = alpha * l + jnp.sum(p, axis=-1, keepdims=True)
                p_hi = p.astype(bf16)
                p_lo = (p - p_hi.astype(f32)).astype(bf16)
                pv = jnp.dot(p_hi, vj, preferred_element_type=f32) + jnp.dot(p_lo, vj, preferred_element_type=f32)
                out.append((m_new, l, alpha * acc + pv))
            return tuple(out)

        one = (jnp.full((tb, 1), -jnp.inf, f32), jnp.zeros((tb, 1), f32), jnp.zeros((tb, dh), f32))
        carry = lax.fori_loop(0, i, lambda j, c: step(j, c, False), (one,) * hp)
        for hh, (m, l, acc) in enumerate(step(i, carry, True)):
            o_ref[hh] = acc / l
            lse_ref[hh] = jnp.broadcast_to(m + jnp.log(l), (tb, 128))

    return pl.pallas_call(
        body, name="fox_attn_fwd", grid=(FOX_HEADS // hp, nq),
        in_specs=[pl.BlockSpec((1, hp, tb, dh), lambda h, i: (0, h, i, 0)),
                  pl.BlockSpec((1, hp, t, dh), lambda h, i: (1, h, 0, 0)),
                  pl.BlockSpec((1, hp, t, dh), lambda h, i: (2, h, 0, 0)),
                  pl.BlockSpec((hp, 1, t), lambda h, i: (h, 0, 0)),
                  pl.BlockSpec((hp, nq, 128), lambda h, i: (h, 0, 0))],
        out_specs=[pl.BlockSpec((hp, tb, dh), lambda h, i: (h, i, 0)),
                   pl.BlockSpec((hp, tb, 128), lambda h, i: (h, i, 0))],
        out_shape=[jax.ShapeDtypeStruct((FOX_HEADS, t, dh), f32), jax.ShapeDtypeStruct((FOX_HEADS, t, 128), f32)],
        compiler_params=_cp(2),
    )(qkvn, qkvn, qkvn, fcol, fref)


def _fox_delta(o, do):
    t = o.shape[1]
    tm = _tok_tile(t)
    dh = FOX_HEAD_DIM

    def body(o_ref, do_ref, dl_ref, dob_ref):
        dob = do_ref[0].astype(bf16)
        dob_ref[0] = dob
        dl_ref[0] = jnp.broadcast_to(jnp.sum(dob.astype(f32) * o_ref[0], axis=-1, keepdims=True), (tm, 128))

    hb = pl.BlockSpec((1, tm, dh), lambda h, i: (h, i, 0))
    return pl.pallas_call(
        body, name="fox_delta", grid=(FOX_HEADS, t // tm), in_specs=[hb, hb],
        out_specs=[pl.BlockSpec((1, tm, 128), lambda h, i: (h, i, 0)), hb],
        out_shape=[jax.ShapeDtypeStruct((FOX_HEADS, t, 128), f32), jax.ShapeDtypeStruct((FOX_HEADS, t, dh), bf16)],
        compiler_params=_cp(2),
    )(o, do)


def _fox_attn_bwd_kv(qkvn, fcol, fref, lse, delta, dob):
    t = qkvn.shape[2]
    tb = _fox_block(t)
    nq = t // tb
    dh = FOX_HEAD_DIM

    def body(q_ref, k_ref, v_ref, fc_ref, fr_ref, lse_ref, dl_ref, do_ref, dk_ref, dv_ref, df_ref, dq_ref,
             s_ref, dp_ref, p_ref, ds_ref):
        j = pl.program_id(1)
        kj = k_ref[0, 0]
        vj = v_ref[0, 0]
        fcol_j = fc_ref[0]
        dk_ref[...] = jnp.zeros_like(dk_ref)
        dv_ref[...] = jnp.zeros_like(dv_ref)
        df_ref[...] = jnp.zeros_like(df_ref)

        @pl.when(j == 0)
        def _():
            dq_ref[...] = jnp.zeros_like(dq_ref)

        def step(i, diag):
            off = pl.multiple_of(i * tb, tb)
            qi = q_ref[0, 0, pl.ds(off, tb), :]
            doi = do_ref[0, pl.ds(off, tb), :]
            s_ref[...] = lax.dot_general(qi, kj, _DN["nt"], preferred_element_type=f32)
            dp_ref[...] = lax.dot_general(doi, vj, _DN["nt"], preferred_element_type=f32)
            bias = fr_ref[0, pl.ds(i, 1), 0:1] - fcol_j
            df = jnp.zeros((1, tb), f32)
            for row0, rs in _strips(tb):
                rows = pl.ds(off + row0, rs.stop - rs.start)
                s = s_ref[rs, :] + bias
                if diag:
                    s = _fox_mask(s, row0)
                p = jnp.exp(s - lse_ref[0, rows, 0:1])
                p_ref[rs, :] = p.astype(bf16)
                ds = p * (dp_ref[rs, :] - dl_ref[0, rows, 0:1])
                ds_ref[rs, :] = ds.astype(bf16)
                df = df + jnp.sum(ds, axis=0, keepdims=True)
            dv_ref[0] += lax.dot_general(p_ref[...], doi, _DN["tn"], preferred_element_type=f32)
            dk_ref[0] += lax.dot_general(ds_ref[...], qi, _DN["tn"], preferred_element_type=f32)
            dq_ref[0, pl.ds(off, tb), :] += jnp.dot(ds_ref[...], kj, preferred_element_type=f32)
            df_ref[0] -= df

        def loop_body(i, carry):
            step(i, False)
            return carry

        step(j, True)
        lax.fori_loop(j + 1, nq, loop_body, 0)

    scratch = [pltpu.VMEM((tb, tb), f32), pltpu.VMEM((tb, tb), f32), pltpu.VMEM((tb, tb), bf16),
               pltpu.VMEM((tb, tb), bf16)]
    hb = pl.BlockSpec((1, tb, dh), lambda h, j: (h, j, 0))
    full = pl.BlockSpec((1, t, dh), lambda h, j: (h, 0, 0))
    full128 = pl.BlockSpec((1, t, 128), lambda h, j: (h, 0, 0))
    hshape = jax.ShapeDtypeStruct((FOX_HEADS, t, dh), f32)
    return pl.pallas_call(
        body, name="fox_attn_bwd_kv", grid=(FOX_HEADS, nq),
        in_specs=[pl.BlockSpec((1, 1, t, dh), lambda h, j: (0, h, 0, 0)),
                  pl.BlockSpec((1, 1, tb, dh), lambda h, j: (1, h, j, 0)),
                  pl.BlockSpec((1, 1, tb, dh), lambda h, j: (2, h, j, 0)),
                  pl.BlockSpec((1, 1, tb), lambda h, j: (h, 0, j)),
                  pl.BlockSpec((1, nq, 128), lambda h, j: (h, 0, 0)),
                  full128, full128, full],
        out_specs=[hb, hb, pl.BlockSpec((1, 1, tb), lambda h, j: (h, 0, j)), full],
        out_shape=[hshape, hshape, jax.ShapeDtypeStruct((FOX_HEADS, 1, t), f32), hshape],
        scratch_shapes=scratch, compiler_params=_cp(2),
    )(qkvn, qkvn, qkvn, fcol, fref, lse, delta, dob)


def _heads_of(a, n):
    t = a.shape[0]
    return a.reshape(t, n, FOX_HEADS, FOX_HEAD_DIM).transpose(1, 2, 0, 3)


def _fox_fwd(x, ln, w_in, b_f, q_gain, k_gain, w_o):
    g, shift, scale, gate = ln
    t = x.shape[0]
    tb = _fox_block(t)
    h = _ln_fwd(x, g, scale, shift)
    proj = _mm(h, w_in, "nn", f32, "fox_in")
    qkv = _heads_of(proj[:, :3 * D], 3)
    fl = proj[:, 3 * D:3 * D + 128]
    fcum = _fox_cumf(fl, b_f)
    fcol = fcum[:, :FOX_HEADS].T.reshape(FOX_HEADS, 1, t)
    fref = jnp.broadcast_to(fcol[:, 0, ::tb][:, :, None], (FOX_HEADS, t // tb, 128))
    qkvn = _fox_qknorm(qkv, q_gain, k_gain)
    o, lse = _fox_attn_fwd(qkvn, fcol, fref)
    ob = o.transpose(1, 0, 2).reshape(t, D).astype(bf16)
    y = _mm(ob, w_o, "nn", f32, "fox_out")
    return _resid_fwd(x, y, gate, 1.0), (x, h, qkv, fl, fcol, fref, qkvn, o, lse, ob, y)


def _fox_bwd(dxn, saved, ln, w_in, b_f, q_gain, k_gain, w_o):
    x, h, qkv, fl, fcol, fref, qkvn, o, lse, ob, y = saved
    g, shift, scale, gate = ln
    t = x.shape[0]
    dy, dgate = _resid_bwd(dxn, y, gate, 1.0)
    do_flat = _mm(dy, w_o, "nt", f32, "fox_do")
    dw_o = _mm(ob, dy, "tn", f32, "fox_dwo")
    do = do_flat.reshape(t, FOX_HEADS, FOX_HEAD_DIM).transpose(1, 0, 2)
    delta, dob = _fox_delta(o, do)
    dkn, dv, dfcol, dqn = _fox_attn_bwd_kv(qkvn, fcol, fref, lse, delta, dob)
    dqk, dqg, dkg = _fox_qknorm_bwd(qkv, q_gain, k_gain, dqn, dkn)
    df = jnp.pad(dfcol.reshape(FOX_HEADS, t).T, ((0, 0), (0, 128 - FOX_HEADS)))
    dfl, db_f = _fox_cumf_bwd(df, fl, b_f)
    dqkv = jnp.concatenate([dqk, dv[None]], axis=0).transpose(2, 0, 1, 3).reshape(t, 3 * D)
    dproj = jnp.concatenate([dqkv.astype(bf16), dfl.astype(bf16)], axis=1)
    dh = _mm(dproj, w_in, "nt", f32, "fox_dh")
    dw_in = _mm(h, dproj, "tn", f32, "fox_dwin")
    dx, dg, dscale, dshift = _ln_bwd(x, g, scale, shift, dh, dxn)
    return dx, (dg, dshift, dscale, dgate), dw_in, db_f, dqg, dkg, dw_o


def _s5_disc(lam_re, lam_im, log_dt, b_re, b_im):
    dt = jnp.exp(log_dt)
    mag = jnp.exp(lam_re * dt)
    lb_re, lb_im = mag * jnp.cos(lam_im * dt), mag * jnp.sin(lam_im * dt)
    den = lam_re * lam_re + lam_im * lam_im
    nr, ni = lb_re - 1.0, lb_im
    k_re = (nr * lam_re + ni * lam_im) / den
    k_im = (ni * lam_re - nr * lam_im) / den
    return lb_re, lb_im, k_re * b_re - k_im * b_im, k_re * b_im + k_im * b_re


def _s5_prep(lam_re, lam_im, log_dt, b_re, b_im):
    def body(ar_ref, ai_ref, dt_ref, br_ref, bi_ref, lr_ref, li_ref, bbr_ref, bbi_ref):
        lr, li, bbr, bbi = _s5_disc(ar_ref[...], ai_ref[...], dt_ref[...], br_ref[...], bi_ref[...])
        lr_ref[...] = lr
        li_ref[...] = li
        bbr_ref[...] = bbr
        bbi_ref[...] = bbi

    small = jax.ShapeDtypeStruct(lam_re.shape, f32)
    bigs = jax.ShapeDtypeStruct(b_re.shape, f32)
    return pl.pallas_call(body, name="s5_prep", out_shape=[small, small, bigs, bigs])(lam_re, lam_im, log_dt, b_re, b_im)


def _s5_prep_bwd(lam_re, lam_im, log_dt, b_re, b_im, dlr, dli, dbbr, dbbi):
    def body(ar_ref, ai_ref, dt_ref, br_ref, bi_ref, dlr_ref, dli_ref, dbbr_ref, dbbi_ref,
             dar_ref, dai_ref, ddt_ref, dbr_ref, dbi_ref):
        _, vjp = jax.vjp(_s5_disc, ar_ref[...], ai_ref[...], dt_ref[...], br_ref[...], bi_ref[...])
        dar, dai, ddt, dbr, dbi = vjp((dlr_ref[...], dli_ref[...], dbbr_ref[...], dbbi_ref[...]))
        dar_ref[...] = dar
        dai_ref[...] = dai
        ddt_ref[...] = jnp.broadcast_to(jnp.sum(ddt, axis=-1, keepdims=True), ddt.shape)
        dbr_ref[...] = dbr
        dbi_ref[...] = dbi

    small = jax.ShapeDtypeStruct(lam_re.shape, f32)
    bigs = jax.ShapeDtypeStruct(b_re.shape, f32)
    return pl.pallas_call(body, name="s5_prep_bwd", out_shape=[small, small, small, bigs, bigs])(
        lam_re, lam_im, log_dt, b_re, b_im, dlr, dli, dbbr, dbbi)


def _s5_tile(t):
    return min(t, 128)


def _s5_blk(k, width):
    return slice(k * width, (k + 1) * width)


def _s5_in_bd(bb):
    b4 = bb.reshape(S5_BLOCKS, 8, S5_GROUP, S5_STATE)
    return jnp.einsum("kgin,gh->kgihn", b4, jnp.eye(8, dtype=bb.dtype)).reshape(S5_BLOCKS, S5_BCH, S5_BST)


def _s5_in_bd_diag(bd):
    b5 = bd.reshape(S5_BLOCKS, 8, S5_GROUP, 8, S5_STATE)
    return jnp.einsum("kgihn,gh->kgin", b5, jnp.eye(8, dtype=bd.dtype)).reshape(S5_GROUPS, S5_GROUP, S5_STATE)


def _s5_out_bd(c):
    c4 = c.reshape(S5_BLOCKS, 8, S5_GROUP, S5_STATE)
    return jnp.einsum("kgin,gh->kgnhi", c4, jnp.eye(8, dtype=c.dtype)).reshape(S5_BLOCKS, S5_BST, S5_BCH)


def _s5_out_bd_diag(bd):
    c5 = bd.reshape(S5_BLOCKS, 8, S5_STATE, 8, S5_GROUP)
    return jnp.einsum("kgnhi,gh->kgin", c5, jnp.eye(8, dtype=bd.dtype)).reshape(S5_GROUPS, S5_GROUP, S5_STATE)


def _s5_scan_fwd(x, ln, lb_re, lb_im, bbr_bd, bbi_bd, cr_bd, ci_bd, dskip):
    g, shift, scale, _ = ln
    t = x.shape[0]
    tm = _s5_tile(t)
    ns = S5_NSTATE

    def body(x_ref, g_ref, sc_ref, sh_ref, ar_ref, ai_ref, bbr_ref, bbi_ref, cr_ref, ci_ref, d_ref,
             yy_ref, xr_ref, xi_ref, cre_ref, cim_ref):
        @pl.when(pl.program_id(0) == 0)
        def _():
            cre_ref[...] = jnp.zeros_like(cre_ref)
            cim_ref[...] = jnp.zeros_like(cim_ref)

        h = _adaln(x_ref[...], g_ref[...], sc_ref[...], sh_ref[...])
        ub = h.astype(bf16)
        for k in range(S5_BLOCKS):
            uk = ub[:, _s5_blk(k, S5_BCH)]
            xr_ref[:, _s5_blk(k, S5_BST)] = jnp.dot(uk, bbr_ref[k], preferred_element_type=f32)
            xi_ref[:, _s5_blk(k, S5_BST)] = jnp.dot(uk, bbi_ref[k], preferred_element_type=f32)
        ar, ai = ar_ref[...], ai_ref[...]

        def step(tt, carry):
            sr, si = carry
            row = pl.ds(tt, 1)
            nr = (ar * sr - ai * si) + xr_ref[row, :]
            ni = (ar * si + ai * sr) + xi_ref[row, :]
            xr_ref[row, :] = nr
            xi_ref[row, :] = ni
            return nr, ni

        sr, si = lax.fori_loop(0, tm, step, (cre_ref[0:1, :], cim_ref[0:1, :]), unroll=2)
        cre_ref[0:1, :] = sr
        cim_ref[0:1, :] = si
        for k in range(S5_BLOCKS):
            sb = _s5_blk(k, S5_BST)
            yk = (jnp.dot(xr_ref[:, sb].astype(bf16), cr_ref[k], preferred_element_type=f32)
                  - jnp.dot(xi_ref[:, sb].astype(bf16), ci_ref[k], preferred_element_type=f32))
            cb = _s5_blk(k, S5_BCH)
            yy_ref[:, cb] = yk + d_ref[:, cb] * h[:, cb]

    bd_in = pl.BlockSpec((S5_BLOCKS, S5_BCH, S5_BST), lambda i: (0, 0, 0))
    bd_out = pl.BlockSpec((S5_BLOCKS, S5_BST, S5_BCH), lambda i: (0, 0, 0))
    st = jax.ShapeDtypeStruct((t, ns), f32)
    return pl.pallas_call(
        body, name="s5_scan_fwd", grid=(t // tm,),
        in_specs=[_tok_spec(tm, D), _row_spec(D), _row_spec(D), _row_spec(D), _row_spec(ns), _row_spec(ns),
                  bd_in, bd_in, bd_out, bd_out, _row_spec(D)],
        out_specs=[_tok_spec(tm, D), _tok_spec(tm, ns), _tok_spec(tm, ns)],
        out_shape=[jax.ShapeDtypeStruct((t, D), f32), st, st],
        scratch_shapes=[pltpu.VMEM((8, ns), f32), pltpu.VMEM((8, ns), f32)],
        compiler_params=_cp(1),
    )(x, g, scale, shift, lb_re, lb_im, bbr_bd, bbi_bd, cr_bd, ci_bd, dskip)


def _s5_scan_bwd(dyy, x, ln, xr, xi, lb_re, lb_im, bbr_bd, bbi_bd, cr_bd, ci_bd, dskip):
    g, shift, scale, _ = ln
    t = x.shape[0]
    tm = _s5_tile(t)
    nt = t // tm
    ns = S5_NSTATE
    per = tm // 8

    def body(dyy_ref, x_ref, g_ref, sc_ref, sh_ref, xr_ref, xi_ref, xrp_ref, xip_ref, ar_ref, ai_ref,
             bbr_ref, bbi_ref, cr_ref, ci_ref, d_ref,
             du_ref, dar_ref, dai_ref, dbbr_ref, dbbi_ref, dcr_ref, dci_ref, dd_ref,
             gr_ref, gi_ref, cre_ref, cim_ref):
        i = pl.program_id(0)
        first = i == 0

        @pl.when(first)
        def _():
            cre_ref[...] = jnp.zeros_like(cre_ref)
            cim_ref[...] = jnp.zeros_like(cim_ref)

        h = _adaln(x_ref[...], g_ref[...], sc_ref[...], sh_ref[...])
        ub = h.astype(bf16)
        dyy_v = dyy_ref[...]
        dyb = dyy_v.astype(bf16)
        for k in range(S5_BLOCKS):
            dk = dyb[:, _s5_blk(k, S5_BCH)]
            sb = _s5_blk(k, S5_BST)
            gr_ref[:, sb] = lax.dot_general(dk, cr_ref[k], _DN["nt"], preferred_element_type=f32)
            gi_ref[:, sb] = -lax.dot_general(dk, ci_ref[k], _DN["nt"], preferred_element_type=f32)
        ar, ai = ar_ref[...], ai_ref[...]

        def step(s, carry):
            nr_, ni_ = carry
            row = pl.ds(tm - 1 - s, 1)
            nr = gr_ref[row, :] + (ar * nr_ + ai * ni_)
            ni = gi_ref[row, :] + (ar * ni_ - ai * nr_)
            gr_ref[row, :] = nr
            gi_ref[row, :] = ni
            return nr, ni

        nr, ni = lax.fori_loop(0, tm, step, (cre_ref[0:1, :], cim_ref[0:1, :]), unroll=2)
        cre_ref[0:1, :] = nr
        cim_ref[0:1, :] = ni

        is_first_tile = i == nt - 1
        _acc_add(first, dd_ref, jnp.sum(dyy_v * h, axis=0, keepdims=True))
        for k in range(S5_BLOCKS):
            cb, sb = _s5_blk(k, S5_BCH), _s5_blk(k, S5_BST)
            xr_v, xi_v = xr_ref[:, sb], xi_ref[:, sb]
            xrp = jnp.where(is_first_tile, 0.0, xrp_ref[:, sb])
            xip = jnp.where(is_first_tile, 0.0, xip_ref[:, sb])
            xr_s = _roll_rows(jnp.concatenate([xrp, xr_v], axis=0), 1)[8:, :]
            xi_s = _roll_rows(jnp.concatenate([xip, xi_v], axis=0), 1)[8:, :]
            gr, gi = gr_ref[:, sb], gi_ref[:, sb]
            dar_k = jnp.sum(gr * xr_s + gi * xi_s, axis=0, keepdims=True)
            dai_k = jnp.sum(gi * xr_s - gr * xi_s, axis=0, keepdims=True)

            @pl.when(first)
            def _():
                dar_ref[:, sb] = dar_k
                dai_ref[:, sb] = dai_k

            @pl.when(jnp.logical_not(first))
            def _():
                dar_ref[:, sb] += dar_k
                dai_ref[:, sb] += dai_k

            grb, gib = gr.astype(bf16), gi.astype(bf16)
            uk, dk = ub[:, cb], dyb[:, cb]
            tn = lambda a_, b_: lax.dot_general(a_, b_, _DN["tn"], preferred_element_type=f32)
            vals = (tn(uk, grb), tn(uk, gib), tn(xr_v.astype(bf16), dk), -tn(xi_v.astype(bf16), dk))
            for ref, val in zip((dbbr_ref, dbbi_ref, dcr_ref, dci_ref), vals):
                @pl.when(first)
                def _():
                    ref[k] = val

                @pl.when(jnp.logical_not(first))
                def _():
                    ref[k] += val
            du_k = (lax.dot_general(grb, bbr_ref[k], _DN["nt"], preferred_element_type=f32)
                    + lax.dot_general(gib, bbi_ref[k], _DN["nt"], preferred_element_type=f32))
            du_ref[:, cb] = du_k + d_ref[:, cb] * dyy_v[:, cb]

    rev = lambda c: _tok_spec(tm, c, nt, True)
    prev = pl.BlockSpec((8, ns), lambda i: (jnp.maximum((nt - 1 - i) * per - 1, 0), 0))
    bd_in = pl.BlockSpec((S5_BLOCKS, S5_BCH, S5_BST), lambda i: (0, 0, 0))
    bd_out = pl.BlockSpec((S5_BLOCKS, S5_BST, S5_BCH), lambda i: (0, 0, 0))
    row_ns = jax.ShapeDtypeStruct((1, ns), f32)
    bd_in_s = jax.ShapeDtypeStruct((S5_BLOCKS, S5_BCH, S5_BST), f32)
    bd_out_s = jax.ShapeDtypeStruct((S5_BLOCKS, S5_BST, S5_BCH), f32)
    return pl.pallas_call(
        body, name="s5_scan_bwd", grid=(nt,),
        in_specs=[rev(D), rev(D), _row_spec(D), _row_spec(D), _row_spec(D), rev(ns), rev(ns), prev, prev,
                  _row_spec(ns), _row_spec(ns), bd_in, bd_in, bd_out, bd_out, _row_spec(D)],
        out_specs=[rev(D), _row_spec(ns), _row_spec(ns), bd_in, bd_in, bd_out, bd_out, _row_spec(D)],
        out_shape=[jax.ShapeDtypeStruct((t, D), f32), row_ns, row_ns, bd_in_s, bd_in_s, bd_out_s, bd_out_s,
                   jax.ShapeDtypeStruct((1, D), f32)],
        scratch_shapes=[pltpu.VMEM((tm, ns), f32), pltpu.VMEM((tm, ns), f32),
                        pltpu.VMEM((8, ns), f32), pltpu.VMEM((8, ns), f32)],
        compiler_params=_cp(1),
    )(dyy, x, g, scale, shift, xr, xi, xr, xi, lb_re, lb_im, bbr_bd, bbi_bd, cr_bd, ci_bd, dskip)


def _s5_gelu(yy):
    t = yy.shape[0]
    tm = _tok_tile(t)

    def body(y_ref, o_ref):
        o_ref[...] = jax.nn.gelu(y_ref[...]).astype(bf16)

    return pl.pallas_call(
        body, name="s5_gelu", grid=(t // tm,), in_specs=[_tok_spec(tm, D)], out_specs=_tok_spec(tm, D),
        out_shape=jax.ShapeDtypeStruct((t, D), bf16), compiler_params=_cp(1),
    )(yy)


def _s5_glu(gl, z):
    return gl * jax.nn.sigmoid(z)


def _s5_out(x, yy, z, gate):
    t = x.shape[0]
    tm = _tok_tile(t)

    def body(x_ref, y_ref, z_ref, gt_ref, o_ref):
        o_ref[...] = x_ref[...] + (1.0 + gt_ref[...]) * _s5_glu(jax.nn.gelu(y_ref[...]), z_ref[...])

    return pl.pallas_call(
        body, name="s5_out", grid=(t // tm,),
        in_specs=[_tok_spec(tm, D), _tok_spec(tm, D), _tok_spec(tm, D), _row_spec(D)],
        out_specs=_tok_spec(tm, D), out_shape=jax.ShapeDtypeStruct((t, D), f32), compiler_params=_cp(1),
    )(x, yy, z, gate)


def _s5_out_bwd(dxn, yy, z, gate):
    t = dxn.shape[0]
    tm = _tok_tile(t)

    def body(dxn_ref, y_ref, z_ref, gt_ref, dz_ref, dgl_ref, dgt_ref):
        dxn_v = dxn_ref[...]
        gl = jax.nn.gelu(y_ref[...])
        out, vjp = jax.vjp(_s5_glu, gl, z_ref[...])
        dgl, dz = vjp((1.0 + gt_ref[...]) * dxn_v)
        dz_ref[...] = dz.astype(bf16)
        dgl_ref[...] = dgl
        _acc_add(pl.program_id(0) == 0, dgt_ref, jnp.sum(dxn_v * out, axis=0, keepdims=True))

    return pl.pallas_call(
        body, name="s5_out_bwd", grid=(t // tm,),
        in_specs=[_tok_spec(tm, D), _tok_spec(tm, D), _tok_spec(tm, D), _row_spec(D)],
        out_specs=[_tok_spec(tm, D), _tok_spec(tm, D), _row_spec(D)],
        out_shape=[jax.ShapeDtypeStruct((t, D), bf16), jax.ShapeDtypeStruct((t, D), f32),
                   jax.ShapeDtypeStruct((1, D), f32)],
        compiler_params=_cp(1),
    )(dxn, yy, z, gate)


def _s5_gelu_bwd(yy, dgl_a, dgl_b):
    t = yy.shape[0]
    tm = _tok_tile(t)

    def body(y_ref, a_ref, b_ref, o_ref):
        _, vjp = jax.vjp(jax.nn.gelu, y_ref[...])
        o_ref[...] = vjp(a_ref[...] + b_ref[...])[0]

    return pl.pallas_call(
        body, name="s5_gelu_bwd", grid=(t // tm,),
        in_specs=[_tok_spec(tm, D), _tok_spec(tm, D), _tok_spec(tm, D)],
        out_specs=_tok_spec(tm, D), out_shape=jax.ShapeDtypeStruct((t, D), f32), compiler_params=_cp(1),
    )(yy, dgl_a, dgl_b)


def _s5_params(lam_re, lam_im, log_dt, b_re, b_im):
    bc = lambda a: a.reshape(S5_GROUPS, 1, -1)
    return (bc(lam_re), bc(lam_im), jnp.broadcast_to(log_dt.reshape(S5_GROUPS, 1, 1), (S5_GROUPS, 1, S5_STATE)),
            b_re.transpose(0, 2, 1), b_im.transpose(0, 2, 1))


def _s5_fwd(x, ln, raw, c_re, c_im, dskip, w_glu):
    gate = ln[3]
    lb_re, lb_im, bb_re, bb_im = _s5_prep(*raw)
    lbr, lbi = lb_re.reshape(1, S5_NSTATE), lb_im.reshape(1, S5_NSTATE)
    bds = (_s5_in_bd(bb_re).astype(bf16), _s5_in_bd(bb_im).astype(bf16),
           _s5_out_bd(c_re).astype(bf16), _s5_out_bd(c_im).astype(bf16))
    yy, xr, xi = _s5_scan_fwd(x, ln, lbr, lbi, *bds, dskip)
    gl = _s5_gelu(yy)
    z = _mm(gl, w_glu, "nn", f32, "s5_glu_mm")
    return _s5_out(x, yy, z, gate), (x, lbr, lbi, bds, yy, xr, xi, gl, z)


def _s5_bwd(dxn, saved, ln, raw, dskip, w_glu):
    x, lbr, lbi, bds, yy, xr, xi, gl, z = saved
    g, shift, scale, gate = ln
    dz, dgl_a, dgate = _s5_out_bwd(dxn, yy, z, gate)
    dgl_b = _mm(dz, w_glu, "nt", f32, "s5_dgl")
    dw_glu = _mm(gl, dz, "tn", f32, "s5_dwglu")
    dyy = _s5_gelu_bwd(yy, dgl_a, dgl_b)
    du, dar, dai, dbbr_bd, dbbi_bd, dcr_bd, dci_bd, dd = _s5_scan_bwd(dyy, x, ln, xr, xi, lbr, lbi, *bds, dskip)
    shp = (S5_GROUPS, 1, S5_STATE)
    d_lam_re, d_lam_im, d_dt, d_b_re, d_b_im = _s5_prep_bwd(
        *raw, dar.reshape(shp), dai.reshape(shp), _s5_in_bd_diag(dbbr_bd), _s5_in_bd_diag(dbbi_bd))
    dx, dg, dscale, dshift = _ln_bwd(x, g, scale, shift, du, dxn)
    grads = dict(
        s5_lam_re=d_lam_re.reshape(1, S5_GROUPS, S5_STATE), s5_lam_im=d_lam_im.reshape(1, S5_GROUPS, S5_STATE),
        s5_log_dt=d_dt[:, 0, 0].reshape(1, S5_GROUPS),
        s5_b_re=d_b_re.transpose(0, 2, 1)[None], s5_b_im=d_b_im.transpose(0, 2, 1)[None],
        s5_c_re=_s5_out_bd_diag(dcr_bd)[None], s5_c_im=_s5_out_bd_diag(dci_bd)[None],
        s5_d=dd, s5_w_glu=dw_glu)
    return dx, (dg, dshift, dscale, dgate), grads


_MESH = pl.DeviceIdType.MESH
_ANY = pl.BlockSpec(memory_space=pl.ANY)


def _me():
    return lax.axis_index("x"), lax.axis_index("y"), lax.axis_index("c")


def _dev_index(x, y, c):
    return 4 * x + 2 * y + c


def _all_gather(vs, name):
    n = len(vs)

    def body(*refs):
        v_refs, out_refs = refs[:n], refs[n:2 * n]
        send_sems, recv_sems, local_sems = refs[2 * n:]
        x, y, cc = _me()
        me, sibling = (x, y, cc), (x, y, 1 - cc)
        chips = [(1 - x, y), (x, 1 - y), (1 - x, 1 - y)]
        sends, local = [], []

        def copy(a, k, block, to, src=None):
            rows = out_refs[a].at[_dev_index(*block)]
            return pltpu.make_async_remote_copy(
                src_ref=rows if src is None else src, dst_ref=rows,
                send_sem=send_sems.at[7 * a + k], recv_sem=recv_sems.at[7 * a + k], device_id=to, device_id_type=_MESH)

        for a in range(n):
            mine = pltpu.make_async_copy(v_refs[a], out_refs[a].at[_dev_index(*me)], local_sems.at[a])
            mine.start()
            local.append(mine)
            first = [copy(a, 0, me, sibling, src=v_refs[a])]
            first += [copy(a, 1 + j, me, (*chip, cc), src=v_refs[a]) for j, chip in enumerate(chips)]
            for cp in first:
                cp.start()
            sends += first
        for a in range(n):
            for j, chip in enumerate(chips):
                copy(a, 1 + j, (*chip, cc), me).wait_recv()
                passed = copy(a, 4 + j, (*chip, cc), sibling)
                passed.start()
                sends.append(passed)
        for a in range(n):
            copy(a, 0, sibling, me).wait_recv()
            for j, chip in enumerate(chips):
                copy(a, 4 + j, (*chip, 1 - cc), me).wait_recv()
        for cp in sends:
            cp.wait_send()
        for cp in local:
            cp.wait()

    return pl.pallas_call(
        body, name=name, out_shape=[jax.ShapeDtypeStruct((N_DEV,) + v.shape, v.dtype) for v in vs],
        in_specs=[_ANY] * n, out_specs=[_ANY] * n,
        scratch_shapes=[pltpu.SemaphoreType.DMA((7 * n,)), pltpu.SemaphoreType.DMA((7 * n,)),
                        pltpu.SemaphoreType.DMA((n,))],
    )(*vs)


def _exchange_pair(vs, name):
    n = len(vs)

    def body(*refs):
        v_refs, out_refs = refs[:n], refs[n:2 * n]
        send_sems, recv_sems = refs[2 * n:]
        x, y, cc = _me()
        sibling = (x, y, 1 - cc)
        copies = []
        for a in range(n):
            for k in range(4):
                cp = pltpu.make_async_remote_copy(
                    src_ref=v_refs[a].at[2 * k + (1 - cc)], dst_ref=out_refs[a].at[k],
                    send_sem=send_sems.at[4 * a + k], recv_sem=recv_sems.at[4 * a + k],
                    device_id=sibling, device_id_type=_MESH)
                cp.start()
                copies.append(cp)
        for cp in copies:
            cp.wait_recv()
        for cp in copies:
            cp.wait_send()

    return pl.pallas_call(
        body, name=name, out_shape=[jax.ShapeDtypeStruct((4,) + v.shape[1:], v.dtype) for v in vs],
        in_specs=[_ANY] * n, out_specs=[_ANY] * n,
        scratch_shapes=[pltpu.SemaphoreType.DMA((4 * n,)), pltpu.SemaphoreType.DMA((4 * n,))],
    )(*vs)


def _pair_sum(v, got):
    _, r, c = v.shape
    tr = _pick(r, (512, 352, 256, 128))
    core = lax.axis_index("c").astype(jnp.int32).reshape(1)

    def body(c_ref, v_ref, g_ref, o_ref):
        o_ref[...] = (v_ref[...] + g_ref[...]).astype(bf16)

    return pl.pallas_call(
        body, name="pair_sum",
        grid_spec=pltpu.PrefetchScalarGridSpec(
            num_scalar_prefetch=1, grid=(4, r // tr),
            in_specs=[pl.BlockSpec((1, tr, c), lambda k, i, c_ref: (2 * k + c_ref[0], i, 0)),
                      pl.BlockSpec((1, tr, c), lambda k, i, c_ref: (k, i, 0))],
            out_specs=pl.BlockSpec((1, tr, c), lambda k, i, c_ref: (k, i, 0))),
        out_shape=jax.ShapeDtypeStruct((4, r, c), bf16), compiler_params=_cp(2),
    )(core, v, got)


def _exchange_chips(vs, name):
    n = len(vs)

    def body(*refs):
        v_refs, out_refs = refs[:n], refs[n:2 * n]
        send_sems, recv_sems, local_sems = refs[2 * n:]
        x, y, cc = _me()
        mine = 2 * x + y
        peers = []
        for mask in (1, 2, 3):
            px = 1 - x if mask & 2 else x
            py = 1 - y if mask & 1 else y
            peers.append((mask - 1, (px, py, cc), 2 * px + py))
        local, sends = [], []
        for a in range(n):
            own = pltpu.make_async_copy(v_refs[a].at[mine], out_refs[a].at[mine], local_sems.at[a])
            own.start()
            local.append(own)
            for k, peer, pchip in peers:
                cp = pltpu.make_async_remote_copy(
                    src_ref=v_refs[a].at[pchip], dst_ref=out_refs[a].at[mine],
                    send_sem=send_sems.at[3 * a + k], recv_sem=recv_sems.at[3 * a + k],
                    device_id=peer, device_id_type=_MESH)
                cp.start()
                sends.append(cp)
        for a in range(n):
            for k, peer, pchip in peers:
                pltpu.make_async_remote_copy(
                    src_ref=v_refs[a].at[pchip], dst_ref=out_refs[a].at[pchip],
                    send_sem=send_sems.at[3 * a + k], recv_sem=recv_sems.at[3 * a + k],
                    device_id=peer, device_id_type=_MESH).wait_recv()
        for cp in sends:
            cp.wait_send()
        for cp in local:
            cp.wait()

    return pl.pallas_call(
        body, name=name, out_shape=[jax.ShapeDtypeStruct(v.shape, v.dtype) for v in vs],
        in_specs=[_ANY] * n, out_specs=[_ANY] * n,
        scratch_shapes=[pltpu.SemaphoreType.DMA((3 * n,)), pltpu.SemaphoreType.DMA((3 * n,)),
                        pltpu.SemaphoreType.DMA((n,))],
    )(*vs)


def _ada_mod(c_all, ada_w):
    cols = ada_w.shape[2]

    def body(c_ref, w_ref, o_ref):
        cond = jax.nn.silu(c_ref[...]).astype(bf16)
        o_ref[0] = jnp.dot(cond, w_ref[0].astype(bf16), preferred_element_type=f32)

    return pl.pallas_call(
        body, name="ada_mod", grid=(DEPTH,),
        in_specs=[pl.BlockSpec((16, D), lambda i: (0, 0)), pl.BlockSpec((1, D, cols), lambda i: (i, 0, 0))],
        out_specs=pl.BlockSpec((1, 16, cols), lambda i: (i, 0, 0)),
        out_shape=jax.ShapeDtypeStruct((DEPTH, 16, cols), f32), compiler_params=_cp(1),
    )(c_all, ada_w)


def _ada_grad(c_all, dmod):
    cols = dmod.shape[2]

    def body(c_ref, d_ref, o_ref):
        cond = jax.nn.silu(c_ref[...]).astype(bf16)
        o_ref[0] = lax.dot_general(cond, d_ref[0].astype(bf16), _DN["tn"], preferred_element_type=f32)

    return pl.pallas_call(
        body, name="ada_grad", grid=(DEPTH,),
        in_specs=[pl.BlockSpec((16, D), lambda i: (0, 0)), pl.BlockSpec((1, 16, cols), lambda i: (i, 0, 0))],
        out_specs=pl.BlockSpec((1, D, cols), lambda i: (i, 0, 0)),
        out_shape=jax.ShapeDtypeStruct((DEPTH, D, cols), f32), compiler_params=_cp(1),
    )(c_all, dmod)


def _row_tile(r):
    return _pick(r, (512, 352, 256, 128)) if r > 512 else r


def _sum_sources(v, name):
    n, r, c = v.shape
    tr = _row_tile(r)

    def body(v_ref, o_ref):
        acc = v_ref[0]
        for p in range(1, n):
            acc = acc + v_ref[p]
        o_ref[...] = acc.astype(f32)

    return pl.pallas_call(
        body, name=name, grid=(r // tr,),
        in_specs=[pl.BlockSpec((n, tr, c), lambda i: (0, i, 0))], out_specs=pl.BlockSpec((tr, c), lambda i: (i, 0)),
        out_shape=jax.ShapeDtypeStruct((r, c), f32), compiler_params=_cp(1),
    )(v)


def _adamw(parts, w, m, v, name):
    n, r, c = parts.shape
    tr = _row_tile(r)
    c1 = 1.0 - ADAM_B1 ** ADAM_STEP
    c2 = 1.0 - ADAM_B2 ** ADAM_STEP

    def body(p_ref, w_ref, m_ref, v_ref, g_ref, d_ref, mo_ref, vo_ref):
        g_v = p_ref[0].astype(f32)
        for p in range(1, n):
            g_v = g_v + p_ref[p].astype(f32)
        m_n = ADAM_B1 * m_ref[...] + (1.0 - ADAM_B1) * g_v
        v_n = ADAM_B2 * v_ref[...] + (1.0 - ADAM_B2) * (g_v * g_v)
        g_ref[...] = g_v
        d_ref[...] = -ADAM_LR * ((m_n / c1) / (jnp.sqrt(v_n / c2) + ADAM_EPS) + ADAM_WD * w_ref[...])
        mo_ref[...] = m_n
        vo_ref[...] = v_n

    spec = pl.BlockSpec((tr, c), lambda i: (i, 0))
    shp = jax.ShapeDtypeStruct((r, c), f32)
    return pl.pallas_call(
        body, name=name, grid=(r // tr,), in_specs=[pl.BlockSpec((n, tr, c), lambda i: (0, i, 0))] + [spec] * 3,
        out_specs=[spec] * 4, out_shape=[shp] * 4, compiler_params=_cp(1),
    )(parts, w, m, v)


def _two_d(shape):
    return (math.prod(shape[:-1]), shape[-1])


def _pack_rows(a):
    n = a.size
    rows = -(-n // (8 * PACK_C)) * 8
    return jnp.pad(a.reshape(-1), (0, rows * PACK_C - n)).reshape(rows, PACK_C)


def _pack(parts):
    return jnp.concatenate([_pack_rows(p) for p in parts], axis=0)


def _unpack(packed, shapes):
    lead = packed.shape[:-2]
    out, off = [], 0
    for s in shapes:
        n = math.prod(s)
        rows = -(-n // (8 * PACK_C)) * 8
        part = packed[..., off:off + rows, :].reshape(lead + (rows * PACK_C,))
        out.append(part[..., :n].reshape(lead + tuple(s)))
        off += rows
    return out


def _unshard(g8, axis):
    local = g8.shape[1:]
    moved = jnp.moveaxis(g8, 0, axis)
    return moved.reshape(local[:axis] + (N_DEV * local[axis],) + local[axis + 1:])


def _shard8(full, axis):
    s = full.shape
    split = full.reshape(s[:axis] + (N_DEV, s[axis] // N_DEV) + s[axis + 1:])
    return jnp.moveaxis(split, axis, 0)


_BIG = dict(ffn_w_in=3, ffn_w_out=2, pool_w=2, fox_w_in=2, fox_w_o=1, s5_w_glu=1, conv_w_in=2, conv_w_out=1)
_SMALL_SHARDED = dict(norm_g=2, s5_d=1, conv_w=3)
_REPLICATED = ("ada_b", "pool_scale", "fox_b_f", "fox_q_gain", "fox_k_gain", "s5_lam_re", "s5_lam_im", "s5_log_dt",
               "s5_b_re", "s5_b_im", "s5_c_re", "s5_c_im")
_WEIGHTS = ("ada_w", "ada_b", "norm_g", "ffn_w_in", "ffn_w_out", "pool_w", "pool_scale", "fox_w_in", "fox_b_f",
            "fox_q_gain", "fox_k_gain", "fox_w_o", "s5_lam_re", "s5_lam_im", "s5_log_dt", "s5_b_re", "s5_b_im",
            "s5_c_re", "s5_c_im", "s5_d", "s5_w_glu", "conv_w_in", "conv_w", "conv_w_out")


def _step(x, c, target, w, m, v):
    t = x.shape[1]
    xi_, yi_, ci_ = _me()
    me = _dev_index(xi_, yi_, ci_)

    sm_shapes = [w[n].shape for n in _SMALL_SHARDED]
    small_all = _all_gather([_pack([c] + [w[n] for n in _SMALL_SHARDED])], "gather_small")[0]
    gathered = _unpack(small_all, [c.shape] + sm_shapes)
    c_all = gathered[0][:, 0, :]
    full = {n: _unshard(p, ax) for (n, ax), p in zip(_SMALL_SHARDED.items(), gathered[1:])}

    big_all = _all_gather([w[n].astype(bf16).reshape(_two_d(w[n].shape)) for n in _BIG], "gather_weights")
    gw = dict(zip(_BIG, big_all))
    ffn_w_in = gw["ffn_w_in"].reshape(N_DEV, 2 * DEPTH, D, FFN_HS)
    ffn_w_out = gw["ffn_w_out"].reshape(N_DEV, 2 * DEPTH, D_FF // N_DEV, D)
    pool_w = gw["pool_w"].reshape(N_DEV, 4, POOL_GROUP // N_DEV, POOL_GROUP).transpose(1, 0, 2, 3)
    pool_w = pool_w.reshape(4, POOL_GROUP, POOL_GROUP)
    fox_w_in = jnp.pad(gw["fox_w_in"].transpose(1, 0, 2).reshape(D, FOX_PROJ), ((0, 0), (0, FOX_PROJ_PAD - FOX_PROJ)))
    fox_w_o, s5_w_glu, conv_w_out = (gw[n].reshape(D, D) for n in ("fox_w_o", "s5_w_glu", "conv_w_out"))
    conv_w_in = gw["conv_w_in"]

    def ffn_w(i, f):
        return ffn_w_in, ffn_w_out, 2 * i + f

    c16 = jnp.pad(c_all, ((0, 8), (0, 0)))
    cols = w["ada_w"].shape[2]
    mod_sh = _ada_mod(c16, w["ada_w"])
    mod_all = _all_gather([mod_sh.reshape(DEPTH * 16, cols)], "gather_mod")[0].reshape(N_DEV, DEPTH, 16, cols)
    mod_mine = lax.dynamic_index_in_dim(mod_all, me, axis=2, keepdims=False)
    mod = (mod_mine.transpose(1, 0, 2).reshape(DEPTH, N_DEV * cols) + w["ada_b"]).reshape(DEPTH, 3, 3, D)

    norm_g = full["norm_g"]

    def ln_of(i, sub):
        return (norm_g[i, sub][None], mod[i, sub, 0][None], mod[i, sub, 1][None], mod[i, sub, 2][None])

    fox_b_f = jnp.pad(w["fox_b_f"], ((0, 0), (0, 128 - FOX_HEADS)))
    s5_raw = _s5_params(w["s5_lam_re"][0], w["s5_lam_im"][0], w["s5_log_dt"][0], w["s5_b_re"][0], w["s5_b_im"][0])
    s5_c_re, s5_c_im = w["s5_c_re"][0], w["s5_c_im"][0]
    conv_w = jnp.pad(full["conv_w"][0, :, 0, :], ((0, 5), (0, 0)))

    xs = x[0]
    saved = []
    for i in range(DEPTH):
        xs, s0 = _ffn_fwd(xs, ln_of(i, 0), *ffn_w(i, 0), 0.5)
        if i == 0:
            s1 = xs
            xs = _pool_fwd(xs, ln_of(i, 1), pool_w, w["pool_scale"])
        elif i == 1:
            xs, s1 = _fox_fwd(xs, ln_of(i, 1), fox_w_in, fox_b_f, w["fox_q_gain"], w["fox_k_gain"], fox_w_o)
        elif i == 2:
            xs, s1 = _s5_fwd(xs, ln_of(i, 1), s5_raw, s5_c_re, s5_c_im, full["s5_d"], s5_w_glu)
        else:
            xs, s1 = _convmix_fwd(xs, ln_of(i, 1), conv_w_in, conv_w, conv_w_out)
        xs, s2 = _ffn_fwd(xs, ln_of(i, 2), *ffn_w(i, 1), 0.5)
        saved.append((s0, s1, s2))
    dx, lpart = _loss_head(xs, target[0])

    grads = {}
    dmod = [[None] * 3 for _ in range(DEPTH)]
    dnorm = [[None] * 3 for _ in range(DEPTH)]
    dffn_in = lax.empty(ffn_w_in.shape, f32)
    dffn_out = lax.empty(ffn_w_out.shape, f32)

    def put_ln(i, sub, dln):
        dg, dshift, dscale, dgate = dln
        dnorm[i][sub] = dg
        dmod[i][sub] = jnp.concatenate([dshift, dscale, dgate], axis=0)

    for i in reversed(range(DEPTH)):
        s0, s1, s2 = saved[i]
        dx, dln, dffn_in, dffn_out = _ffn_bwd(dx, s2, ln_of(i, 2), *ffn_w(i, 1), 0.5, dffn_in, dffn_out)
        put_ln(i, 2, dln)
        if i == 0:
            dx, dln, dpw, dps = _pool_bwd(dx, s1, ln_of(i, 1), pool_w, w["pool_scale"])
            dpw = dpw.reshape(4, N_DEV, POOL_GROUP // N_DEV, POOL_GROUP).transpose(1, 0, 2, 3)
            grads.update(pool_w=dpw.reshape(N_DEV, 4 * POOL_GROUP // N_DEV, POOL_GROUP), pool_scale=dps)
        elif i == 1:
            dx, dln, dwi, dbf, dqg, dkg, dwo = _fox_bwd(
                dx, s1, ln_of(i, 1), fox_w_in, fox_b_f, w["fox_q_gain"], w["fox_k_gain"], fox_w_o)
            dwi = dwi[:, :FOX_PROJ].reshape(D, N_DEV, FOX_PROJ // N_DEV).transpose(1, 0, 2)
            grads.update(fox_w_in=dwi, fox_b_f=dbf[:, :FOX_HEADS], fox_q_gain=dqg, fox_k_gain=dkg,
                         fox_w_o=dwo.reshape(N_DEV, D // N_DEV, D))
        elif i == 2:
            dx, dln, gs5 = _s5_bwd(dx, s1, ln_of(i, 1), s5_raw, full["s5_d"], s5_w_glu)
            gs5["s5_w_glu"] = gs5["s5_w_glu"].reshape(N_DEV, D // N_DEV, D)
            grads.update(gs5)
        else:
            dx, dln, dwi, dcw, dwo = _convmix_bwd(dx, s1, ln_of(i, 1), conv_w_in, conv_w, conv_w_out)
            grads.update(conv_w_in=dwi, conv_w=dcw[None, :, None, :], conv_w_out=dwo.reshape(N_DEV, D // N_DEV, D))
        put_ln(i, 1, dln)
        dx, dln, dffn_in, dffn_out = _ffn_bwd(dx, s0, ln_of(i, 0), *ffn_w(i, 0), 0.5, dffn_in, dffn_out)
        put_ln(i, 0, dln)
    grads["ffn_w_in"] = dffn_in.reshape(N_DEV, 2 * DEPTH * D, FFN_HS)
    grads["ffn_w_out"] = dffn_out.reshape(N_DEV, 2 * DEPTH * D_FF // N_DEV, D)
    grads["norm_g"] = jnp.stack([jnp.concatenate(r, axis=0) for r in dnorm])
    dmod_mine = jnp.stack([jnp.stack(r) for r in dmod]).reshape(DEPTH, 9 * D)

    small_names = list(_REPLICATED[1:]) + list(_SMALL_SHARDED)
    small_parts = [dmod_mine] + [grads[n] for n in small_names] + [lpart[:, 0:1]]
    small_g = _all_gather([_pack(small_parts)], "gather_grads")[0]
    small_sum = _sum_sources(small_g, "sum_small")
    summed = dict(zip(["ada_b"] + small_names + ["loss"], _unpack(small_sum, [p.shape for p in small_parts])))
    loss = summed.pop("loss")[0, 0]
    for n, ax in _SMALL_SHARDED.items():
        local = w[n].shape[ax]
        summed[n] = lax.dynamic_slice_in_dim(summed[n], me * local, local, axis=ax)

    dmod_all = small_g[:, :DEPTH * 9].reshape(N_DEV, DEPTH, 9 * D)
    dmod_cols = lax.dynamic_slice_in_dim(dmod_all, me * cols, cols, axis=2)
    ada_g = _ada_grad(c16, jnp.pad(dmod_cols.transpose(1, 0, 2), ((0, 0), (0, 8), (0, 0))))

    big_parts = [grads[n] for n in _BIG]
    from_sibling = _exchange_pair(big_parts, "exchange_pair")
    chip_sums = [_pair_sum(p, s) for p, s in zip(big_parts, from_sibling)]
    big_landed = dict(zip(_BIG, _exchange_chips(chip_sums, "exchange_chips")))

    grad, delta, new_m, new_v = {}, {}, {}, {}
    big_landed["ada_w"] = ada_g[None]
    for n, parts in big_landed.items():
        view = _two_d(w[n].shape)
        outs = _adamw(parts.reshape((parts.shape[0],) + view), w[n].reshape(view), m[n].reshape(view),
                      v[n].reshape(view), "adamw_" + n)
        grad[n], delta[n], new_m[n], new_v[n] = (a.reshape(w[n].shape) for a in outs)
    small = [n for n in _WEIGHTS if n not in big_landed]
    small_shapes = [w[n].shape for n in small]
    pk = lambda d: _pack([d[n] for n in small])
    outs = _adamw(pk(summed)[None], pk(w), pk(m), pk(v), "adamw_small")
    for dst, packed in zip((grad, delta, new_m, new_v), outs):
        dst.update(zip(small, _unpack(packed, small_shapes)))
    return (loss, dx[None], *[grad[n] for n in _WEIGHTS], *[delta[n] for n in _WEIGHTS],
            *[new_m[n] for n in _WEIGHTS], *[new_v[n] for n in _WEIGHTS])


def kernel(x, c, ada_w, ada_b, norm_g, ffn_w_in, ffn_w_out, pool_w, pool_scale, fox_w_in, fox_b_f, fox_q_gain, fox_k_gain, fox_w_o, s5_lam_re, s5_lam_im, s5_log_dt, s5_b_re, s5_b_im, s5_c_re, s5_c_im, s5_d, s5_w_glu, conv_w_in, conv_w, conv_w_out, loss_target, m_ada_w, m_ada_b, m_norm_g, m_ffn_w_in, m_ffn_w_out, m_pool_w, m_pool_scale, m_fox_w_in, m_fox_b_f, m_fox_q_gain, m_fox_k_gain, m_fox_w_o, m_s5_lam_re, m_s5_lam_im, m_s5_log_dt, m_s5_b_re, m_s5_b_im, m_s5_c_re, m_s5_c_im, m_s5_d, m_s5_w_glu, m_conv_w_in, m_conv_w, m_conv_w_out, v_ada_w, v_ada_b, v_norm_g, v_ffn_w_in, v_ffn_w_out, v_pool_w, v_pool_scale, v_fox_w_in, v_fox_b_f, v_fox_q_gain, v_fox_k_gain, v_fox_w_o, v_s5_lam_re, v_s5_lam_im, v_s5_log_dt, v_s5_b_re, v_s5_b_im, v_s5_c_re, v_s5_c_im, v_s5_d, v_s5_w_glu, v_conv_w_in, v_conv_w, v_conv_w_out):
    ws = (ada_w, ada_b, norm_g, ffn_w_in, ffn_w_out, pool_w, pool_scale, fox_w_in, fox_b_f, fox_q_gain, fox_k_gain,
          fox_w_o, s5_lam_re, s5_lam_im, s5_log_dt, s5_b_re, s5_b_im, s5_c_re, s5_c_im, s5_d, s5_w_glu, conv_w_in,
          conv_w, conv_w_out)
    ms = (m_ada_w, m_ada_b, m_norm_g, m_ffn_w_in, m_ffn_w_out, m_pool_w, m_pool_scale, m_fox_w_in, m_fox_b_f,
          m_fox_q_gain, m_fox_k_gain, m_fox_w_o, m_s5_lam_re, m_s5_lam_im, m_s5_log_dt, m_s5_b_re, m_s5_b_im,
          m_s5_c_re, m_s5_c_im, m_s5_d, m_s5_w_glu, m_conv_w_in, m_conv_w, m_conv_w_out)
    vs = (v_ada_w, v_ada_b, v_norm_g, v_ffn_w_in, v_ffn_w_out, v_pool_w, v_pool_scale, v_fox_w_in, v_fox_b_f,
          v_fox_q_gain, v_fox_k_gain, v_fox_w_o, v_s5_lam_re, v_s5_lam_im, v_s5_log_dt, v_s5_b_re, v_s5_b_im,
          v_s5_c_re, v_s5_c_im, v_s5_d, v_s5_w_glu, v_conv_w_in, v_conv_w, v_conv_w_out)
    return _step(x, c, loss_target, dict(zip(_WEIGHTS, ws)), dict(zip(_WEIGHTS, ms)), dict(zip(_WEIGHTS, vs)))
```

```python
import math

import jax
import jax.numpy as jnp
from jax import lax
from jax.experimental import pallas as pl
from jax.experimental.pallas import tpu as pltpu

f32 = jnp.float32
bf16 = jnp.bfloat16

D = 1024
D_FF = 2816
FFN_HS = 2 * D_FF // 8
FFN_SLABS = 4
DEPTH = 4
NORM_EPS = 1e-6
N_DEV = 8
AXES = ("x", "y", "c")
POOL_WINDOWS = (2, 4, 8, 16)
POOL_GROUP = 256
POOL_HALO = 16
FOX_HEADS = 16
FOX_HEAD_DIM = 64
FOX_PROJ = 3088
FOX_PROJ_PAD = 3200
S5_GROUPS = 64
S5_GROUP = 16
S5_STATE = 64
S5_NSTATE = S5_GROUPS * S5_STATE
S5_BLOCKS = 8
S5_BCH = 128
S5_BST = 512
CONV_HALO = 8
ADAM_LR = 0.001
ADAM_B1 = 0.9
ADAM_B2 = 0.999
ADAM_EPS = 1e-08
ADAM_WD = 0.01
ADAM_STEP = 10
VMEM_LIMIT = 56 * 1024 * 1024
PACK_C = 1024

_ARB = "arbitrary"


def _cp(n_axes):
    return pltpu.CompilerParams(dimension_semantics=(_ARB,) * n_axes, vmem_limit_bytes=VMEM_LIMIT)


def _pick(n, prefs):
    for c in prefs:
        if n % c == 0:
            return c
    return n


_DN = {"nn": (((1,), (0,)), ((), ())), "nt": (((1,), (1,)), ((), ())), "tn": (((0,), (0,)), ((), ()))}


def _mm(a, b, mode, out_dtype, name):
    if mode == "nn":
        (m, k), (_, n) = a.shape, b.shape
    elif mode == "nt":
        (m, k), (n, _) = a.shape, b.shape
    else:
        (k, m), (_, n) = a.shape, b.shape
    big = (1408, 1024, 640, 512, 384, 256, 128)
    tm = _pick(m, big) if mode == "tn" else _pick(m, (1024, 512, 256, 128))
    tn = _pick(n, big)
    if mode == "tn":
        tk = _pick(k, (512, 256, 128))
    else:
        tk = k if k <= 3200 else _pick(k, (2816, 2048, 1024, 512))
    nk = k // tk
    dn = _DN[mode]

    def body(a_ref, b_ref, o_ref, acc_ref):
        p = lax.dot_general(a_ref[...], b_ref[...], dn, preferred_element_type=f32)
        if nk == 1:
            o_ref[...] = p.astype(out_dtype)
        else:
            kk = pl.program_id(2)

            @pl.when(kk == 0)
            def _():
                acc_ref[...] = p

            @pl.when(kk > 0)
            def _():
                acc_ref[...] += p

            @pl.when(kk == nk - 1)
            def _():
                o_ref[...] = acc_ref[...].astype(out_dtype)

    if mode == "nn":
        a_spec = pl.BlockSpec((tm, tk), lambda i, j, kk: (i, kk))
        b_spec = pl.BlockSpec((tk, tn), lambda i, j, kk: (kk, j))
    elif mode == "nt":
        a_spec = pl.BlockSpec((tm, tk), lambda i, j, kk: (i, kk))
        b_spec = pl.BlockSpec((tn, tk), lambda i, j, kk: (j, kk))
    else:
        a_spec = pl.BlockSpec((tk, tm), lambda i, j, kk: (kk, i))
        b_spec = pl.BlockSpec((tk, tn), lambda i, j, kk: (kk, j))
    acc_shape = (tm, tn) if nk > 1 else (8, 128)
    return pl.pallas_call(
        body, name=name, grid=(m // tm, n // tn, nk),
        in_specs=[a_spec, b_spec], out_specs=pl.BlockSpec((tm, tn), lambda i, j, kk: (i, j)),
        out_shape=jax.ShapeDtypeStruct((m, n), out_dtype),
        scratch_shapes=[pltpu.VMEM(acc_shape, f32)],
        compiler_params=_cp(3),
    )(a, b)


def _mmx(a, b, mode, name, grid, a_spec, b_spec, o_spec, out_shape, into=None):
    nk = grid[2]
    dn = _DN[mode]
    out_dtype = out_shape.dtype
    a_blk = (math.prod(a_spec.block_shape[:-1]), a_spec.block_shape[-1])
    b_blk = (math.prod(b_spec.block_shape[:-1]), b_spec.block_shape[-1])
    o_blk = (math.prod(o_spec.block_shape[:-1]), o_spec.block_shape[-1])

    def body(a_ref, b_ref, *rest):
        o_ref, acc_ref = rest[-2:]
        p = lax.dot_general(a_ref[...].reshape(a_blk), b_ref[...].reshape(b_blk), dn, preferred_element_type=f32)
        if nk == 1:
            o_ref[...] = p.reshape(o_ref.shape).astype(out_dtype)
        else:
            kk = pl.program_id(2)

            @pl.when(kk == 0)
            def _():
                acc_ref[...] = p

            @pl.when(kk > 0)
            def _():
                acc_ref[...] += p

            @pl.when(kk == nk - 1)
            def _():
                o_ref[...] = acc_ref[...].reshape(o_ref.shape).astype(out_dtype)

    extra = {} if into is None else dict(input_output_aliases={2: 0})
    operands = (a, b) if into is None else (a, b, into)
    return pl.pallas_call(
        body, name=name, grid=grid, in_specs=[a_spec, b_spec] + ([] if into is None else [_ANY]),
        out_specs=o_spec, out_shape=out_shape,
        scratch_shapes=[pltpu.VMEM(o_blk if nk > 1 else (8, 128), f32)], compiler_params=_cp(3), **extra,
    )(*operands)


def _tok_tile(t):
    return min(t, 512)


def _tok_spec(tm, c, nt=None, reverse=False):
    if reverse:
        return pl.BlockSpec((tm, c), lambda i: (nt - 1 - i, 0))
    return pl.BlockSpec((tm, c), lambda i: (i, 0))


def _row_spec(c, rows=1):
    return pl.BlockSpec((rows, c), lambda i: (0, 0))


def _acc_add(first, ref, val):
    @pl.when(first)
    def _():
        ref[...] = val

    @pl.when(jnp.logical_not(first))
    def _():
        ref[...] += val


def _adaln(x, g, scale, shift):
    y = x * lax.rsqrt(jnp.mean(x * x, axis=-1, keepdims=True) + NORM_EPS)
    return (y * g) * (1.0 + scale) + shift


def _ln_fwd(x, g, scale, shift):
    t = x.shape[0]
    tm = _tok_tile(t)

    def body(x_ref, g_ref, sc_ref, sh_ref, h_ref):
        h_ref[...] = _adaln(x_ref[...], g_ref[...], sc_ref[...], sh_ref[...]).astype(bf16)

    return pl.pallas_call(
        body, name="ln_fwd", grid=(t // tm,),
        in_specs=[_tok_spec(tm, D), _row_spec(D), _row_spec(D), _row_spec(D)],
        out_specs=_tok_spec(tm, D), out_shape=jax.ShapeDtypeStruct((t, D), bf16),
        compiler_params=_cp(1),
    )(x, g, scale, shift)


def _ln_bwd(x, g, scale, shift, dh, dxn):
    t = x.shape[0]
    tm = _tok_tile(t)

    def body(x_ref, g_ref, sc_ref, sh_ref, dh_ref, dxn_ref, dx_ref, dg_ref, dsc_ref, dsh_ref):
        _, vjp = jax.vjp(_adaln, x_ref[...], g_ref[...], sc_ref[...], sh_ref[...])
        dx, dg, dsc, dsh = vjp(dh_ref[...])
        dx_ref[...] = dxn_ref[...] + dx
        first = pl.program_id(0) == 0
        _acc_add(first, dg_ref, dg)
        _acc_add(first, dsc_ref, dsc)
        _acc_add(first, dsh_ref, dsh)

    row = jax.ShapeDtypeStruct((1, D), f32)
    return pl.pallas_call(
        body, name="ln_bwd", grid=(t // tm,),
        in_specs=[_tok_spec(tm, D), _row_spec(D), _row_spec(D), _row_spec(D), _tok_spec(tm, D), _tok_spec(tm, D)],
        out_specs=[_tok_spec(tm, D), _row_spec(D), _row_spec(D), _row_spec(D)],
        out_shape=[jax.ShapeDtypeStruct((t, D), f32), row, row, row],
        compiler_params=_cp(1),
    )(x, g, scale, shift, dh, dxn)


def _swiglu(g, u):
    return jax.nn.silu(g) * u


def _ffn_tile(t):
    return min(t, 512)


def _ffn_specs(tm, idx, rows):
    return (pl.BlockSpec((2, 1, 1, D, FFN_HS), lambda i, q: (0, q, idx, 0, 0)),
            pl.BlockSpec((2, 1, rows, D), lambda i, q: (q, idx, 0, 0)),
            pl.BlockSpec((2, 1, tm, FFN_HS), lambda i, q: (0, q, i, 0)),
            pl.BlockSpec((tm, D), lambda i, q: (i, 0)),
            pl.BlockSpec((1, D), lambda i, q: (0, 0)))


def _ffn_up(x, h, gate, w_in, w_out, idx, coef):
    t = h.shape[0]
    tm = _ffn_tile(t)
    rows = w_out.shape[2]

    def body(h_ref, wi_ref, wo_ref, x_ref, gt_ref, gu_ref, a_ref, o_ref, xn_ref):
        q = pl.program_id(1)
        hv = h_ref[...]
        g = jnp.dot(hv, wi_ref[0, 0, 0], preferred_element_type=f32)
        u = jnp.dot(hv, wi_ref[1, 0, 0], preferred_element_type=f32)
        gu_ref[0, 0] = g.astype(bf16)
        gu_ref[1, 0] = u.astype(bf16)
        a = _swiglu(g, u).astype(bf16)
        a_ref[0] = a
        _acc_add(q == 0, o_ref, jnp.dot(a, wo_ref[...].reshape(FFN_HS, D), preferred_element_type=f32))

        @pl.when(q == FFN_SLABS - 1)
        def _():
            xn_ref[...] = x_ref[...] + (coef * (1.0 + gt_ref[...])) * o_ref[...]

    wi, wo, slab, tok, row = _ffn_specs(tm, idx, rows)
    tok_f32 = jax.ShapeDtypeStruct((t, D), f32)
    return pl.pallas_call(
        body, name="ffn_up", grid=(t // tm, FFN_SLABS), in_specs=[tok, wi, wo, tok, row],
        out_specs=[slab, pl.BlockSpec((1, tm, FFN_HS), lambda i, q: (q, i, 0)), tok, tok],
        out_shape=[jax.ShapeDtypeStruct((2, FFN_SLABS, t, FFN_HS), bf16),
                   jax.ShapeDtypeStruct((FFN_SLABS, t, FFN_HS), bf16), tok_f32, tok_f32],
        compiler_params=_cp(2),
    )(h, w_in, w_out, x, gate)


def _ffn_dgu(do, w_out, idx, gu):
    t = do.shape[0]
    tm = _mm_tile(t)

    def body(do_ref, wo_ref, gu_ref, o_ref):
        da = lax.dot_general(do_ref[...], wo_ref[...].reshape(FFN_HS, D), _DN["nt"], preferred_element_type=f32)
        _, vjp = jax.vjp(_swiglu, gu_ref[0, 0].astype(f32), gu_ref[1, 0].astype(f32))
        dg, du = vjp(da)
        o_ref[0, 0] = dg.astype(bf16)
        o_ref[1, 0] = du.astype(bf16)

    _, wo, slab, tok, _ = _ffn_specs(tm, idx, w_out.shape[2])
    return pl.pallas_call(
        body, name="ffn_dgu", grid=(t // tm, FFN_SLABS), in_specs=[tok, wo, slab],
        out_specs=slab, out_shape=jax.ShapeDtypeStruct((2, FFN_SLABS, t, FFN_HS), bf16),
        compiler_params=_cp(2),
    )(do, w_out, gu)


def _ffn_dh(dgu, w_in, idx):
    t = dgu.shape[2]
    tm = _mm_tile(t)

    def body(d_ref, wi_ref, o_ref):
        p = (lax.dot_general(d_ref[0, 0], wi_ref[0, 0, 0], _DN["nt"], preferred_element_type=f32)
             + lax.dot_general(d_ref[1, 0], wi_ref[1, 0, 0], _DN["nt"], preferred_element_type=f32))
        _acc_add(pl.program_id(1) == 0, o_ref, p)

    wi, _, slab, tok, _ = _ffn_specs(tm, idx, 8)
    return pl.pallas_call(
        body, name="ffn_dh", grid=(t // tm, FFN_SLABS), in_specs=[slab, wi], out_specs=tok,
        out_shape=jax.ShapeDtypeStruct((t, D), f32), compiler_params=_cp(2),
    )(dgu, w_in)


def _resid_fwd(x, o, gate, coef):
    t = x.shape[0]
    tm = _tok_tile(t)

    def body(x_ref, o_ref, gt_ref, y_ref):
        y_ref[...] = x_ref[...] + (coef * (1.0 + gt_ref[...])) * o_ref[...]

    return pl.pallas_call(
        body, name="resid_fwd", grid=(t // tm,),
        in_specs=[_tok_spec(tm, D), _tok_spec(tm, D), _row_spec(D)],
        out_specs=_tok_spec(tm, D), out_shape=jax.ShapeDtypeStruct((t, D), f32),
        compiler_params=_cp(1),
    )(x, o, gate)


def _resid_bwd(dxn, o, gate, coef):
    t = dxn.shape[0]
    tm = _tok_tile(t)

    def body(dxn_ref, o_ref, gt_ref, do_ref, dgt_ref):
        dxn_v = dxn_ref[...]
        do_ref[...] = ((coef * (1.0 + gt_ref[...])) * dxn_v).astype(bf16)
        _acc_add(pl.program_id(0) == 0, dgt_ref, coef * jnp.sum(dxn_v * o_ref[...], axis=0, keepdims=True))

    return pl.pallas_call(
        body, name="resid_bwd", grid=(t // tm,),
        in_specs=[_tok_spec(tm, D), _tok_spec(tm, D), _row_spec(D)],
        out_specs=[_tok_spec(tm, D), _row_spec(D)],
        out_shape=[jax.ShapeDtypeStruct((t, D), bf16), jax.ShapeDtypeStruct((1, D), f32)],
        compiler_params=_cp(1),
    )(dxn, o, gate)


def _loss_head(y, target):
    t = y.shape[0]
    tm = _tok_tile(t)

    def body(y_ref, t_ref, dy_ref, l_ref):
        err = y_ref[...] - t_ref[...]
        dy_ref[...] = err * (1.0 / D)
        part = jnp.sum(jnp.sum(err * err, axis=0, keepdims=True), axis=1, keepdims=True) * (0.5 / D)
        _acc_add(pl.program_id(0) == 0, l_ref, jnp.broadcast_to(part, (1, 128)))

    return pl.pallas_call(
        body, name="loss_head", grid=(t // tm,),
        in_specs=[_tok_spec(tm, D), _tok_spec(tm, D)],
        out_specs=[_tok_spec(tm, D), _row_spec(128)],
        out_shape=[jax.ShapeDtypeStruct((t, D), f32), jax.ShapeDtypeStruct((1, 128), f32)],
        compiler_params=_cp(1),
    )(y, target)


def _mm_tile(t):
    return min(t, 1024)


def _ffn_fwd(x, ln, w_in, w_out, idx, coef):
    g, shift, scale, gate = ln
    h = _ln_fwd(x, g, scale, shift)
    gu, a, o, xn = _ffn_up(x, h, gate, w_in.reshape((2, FFN_SLABS) + w_in.shape[1:]), w_out, idx, coef)
    return xn, (x, h, gu, a, o)


def _ffn_bwd(dxn, saved, ln, w_in, w_out, idx, coef, dw_in, dw_out):
    x, h, gu, a, o = saved
    g, shift, scale, gate = ln
    t = x.shape[0]
    tm = _mm_tile(t)
    tk = _pick(t, (2048, 1024, 512, 256, 128))
    rows = w_out.shape[2]
    sds = jax.ShapeDtypeStruct
    do, dgate = _resid_bwd(dxn, o, gate, coef)
    dgu = _ffn_dgu(do, w_out, idx, gu)
    dh = _ffn_dh(dgu, w_in.reshape((2, FFN_SLABS) + w_in.shape[1:]), idx)
    dgu = dgu.reshape(N_DEV, t, FFN_HS)
    dw_out = _mmx(a, do, "tn", "ffn_dwout", (FFN_SLABS, 1, t // tk),
                  pl.BlockSpec((1, tk, FFN_HS), lambda i, j, k: (i, k, 0)),
                  pl.BlockSpec((tk, D), lambda i, j, k: (k, 0)),
                  pl.BlockSpec((2, 1, rows, D), lambda i, j, k: (i, idx, 0, 0)),
                  sds(dw_out.shape, f32), into=dw_out)
    dw_in = _mmx(h, dgu, "tn", "ffn_dwin", (N_DEV, 1, t // tk),
                 pl.BlockSpec((tk, D), lambda i, j, k: (k, 0)),
                 pl.BlockSpec((1, tk, FFN_HS), lambda i, j, k: (i, k, 0)),
                 pl.BlockSpec((1, 1, D, FFN_HS), lambda i, j, k: (i, idx, 0, 0)),
                 sds(dw_in.shape, f32), into=dw_in)
    dx, dg, dscale, dshift = _ln_bwd(x, g, scale, shift, dh, dxn)
    return dx, (dg, dshift, dscale, dgate), dw_in, dw_out


def _roll_rows(a, k):
    n = a.shape[0]
    return pltpu.roll(a, k % n, 0)


def _pool_windows(hx, first_row, reverse):
    outs = []
    for gi, w in enumerate(POOL_WINDOWS):
        acc = hx[:, gi * POOL_GROUP:(gi + 1) * POOL_GROUP]
        k = 1
        while k < w:
            acc = acc + _roll_rows(acc, -k if reverse else k)
            k *= 2
        outs.append(acc)
    return outs


def _pool_cnt(t_idx, w):
    return jnp.minimum(t_idx + 1, w).astype(f32)


def _pool_pooled(x_ref, xp_ref, g, scale, shift, i, tm):
    h = _adaln(x_ref[...], g, scale, shift)
    hp = _adaln(xp_ref[...], g, scale, shift)
    hp = jnp.where(i == 0, 0.0, hp)
    hx = jnp.concatenate([hp, h], axis=0)
    sums = _pool_windows(hx, 0, False)
    t_idx = i * tm + lax.broadcasted_iota(jnp.int32, (tm, 1), 0)
    pooled = []
    for gi, w in enumerate(POOL_WINDOWS):
        s = sums[gi][POOL_HALO:, :]
        pooled.append(s / _pool_cnt(t_idx, w) - h[:, gi * POOL_GROUP:(gi + 1) * POOL_GROUP])
    return h, pooled


def _pool_specs(t, tm):
    per = tm // POOL_HALO
    prev = pl.BlockSpec((POOL_HALO, D), lambda i: (jnp.maximum(i * per - 1, 0), 0))
    return [_tok_spec(tm, D), prev, _row_spec(D), _row_spec(D), _row_spec(D),
            pl.BlockSpec((4, POOL_GROUP, POOL_GROUP), lambda i: (0, 0, 0)), _row_spec(D), _row_spec(D)]


def _pool_fwd(x, ln, w, pscale):
    g, shift, scale, gate = ln
    t = x.shape[0]
    tm = _tok_tile(t)

    def body(x_ref, xp_ref, g_ref, sc_ref, sh_ref, w_ref, ps_ref, gt_ref, y_ref):
        i = pl.program_id(0)
        _, pooled = _pool_pooled(x_ref, xp_ref, g_ref[...], sc_ref[...], sh_ref[...], i, tm)
        mixed = [jnp.dot(pooled[gi].astype(bf16), w_ref[gi], preferred_element_type=f32) for gi in range(4)]
        y = jnp.concatenate(mixed, axis=1) * ps_ref[...]
        y_ref[...] = x_ref[...] + (1.0 + gt_ref[...]) * y

    return pl.pallas_call(
        body, name="pool_fwd", grid=(t // tm,), in_specs=_pool_specs(t, tm),
        out_specs=_tok_spec(tm, D), out_shape=jax.ShapeDtypeStruct((t, D), f32),
        compiler_params=_cp(1),
    )(x, x, g, scale, shift, w, pscale, gate)


def _pool_bwd(dxn, x, ln, w, pscale):
    g, shift, scale, gate = ln
    t = x.shape[0]
    tm = _tok_tile(t)
    nt = t // tm
    per = tm // POOL_HALO

    def body_a(x_ref, xp_ref, g_ref, sc_ref, sh_ref, w_ref, ps_ref, gt_ref, dxn_ref,
               dp_ref, dw_ref, dps_ref, dgt_ref):
        i = pl.program_id(0)
        first = i == 0
        _, pooled = _pool_pooled(x_ref, xp_ref, g_ref[...], sc_ref[...], sh_ref[...], i, tm)
        dxn_v = dxn_ref[...]
        dy = (1.0 + gt_ref[...]) * dxn_v
        dmixed = dy * ps_ref[...]
        mixed, dps = [], []
        for gi in range(4):
            sl = slice(gi * POOL_GROUP, (gi + 1) * POOL_GROUP)
            pb = pooled[gi].astype(bf16)
            dmb = dmixed[:, sl].astype(bf16)
            mixed.append(jnp.dot(pb, w_ref[gi], preferred_element_type=f32))
            dp_ref[:, sl] = lax.dot_general(dmb, w_ref[gi], _DN["nt"], preferred_element_type=f32)
            dwg = lax.dot_general(pb, dmb, _DN["tn"], preferred_element_type=f32)

            @pl.when(first)
            def _():
                dw_ref[gi] = dwg

            @pl.when(jnp.logical_not(first))
            def _():
                dw_ref[gi] += dwg
        mixed = jnp.concatenate(mixed, axis=1)
        _acc_add(first, dps_ref, jnp.sum(dy * mixed, axis=0, keepdims=True))
        _acc_add(first, dgt_ref, jnp.sum(dxn_v * (mixed * ps_ref[...]), axis=0, keepdims=True))

    row = jax.ShapeDtypeStruct((1, D), f32)
    dpooled, dw, dps, dgate = pl.pallas_call(
        body_a, name="pool_bwd_a", grid=(nt,), in_specs=_pool_specs(t, tm) + [_tok_spec(tm, D)],
        out_specs=[_tok_spec(tm, D), pl.BlockSpec((4, POOL_GROUP, POOL_GROUP), lambda i: (0, 0, 0)),
                   _row_spec(D), _row_spec(D)],
        out_shape=[jax.ShapeDtypeStruct((t, D), f32), jax.ShapeDtypeStruct((4, POOL_GROUP, POOL_GROUP), f32), row, row],
        compiler_params=_cp(1),
    )(x, x, g, scale, shift, w, pscale, gate, dxn)

    def body_b(dp_ref, dpn_ref, x_ref, g_ref, sc_ref, sh_ref, dxn_ref, dx_ref, dg_ref, dsc_ref, dsh_ref):
        i = pl.program_id(0)
        dp = dp_ref[...]
        dpn = jnp.where(i == nt - 1, 0.0, dpn_ref[...])
        t_idx = i * tm + lax.broadcasted_iota(jnp.int32, (tm + POOL_HALO, 1), 0)
        ex = jnp.concatenate([dp, dpn], axis=0)
        parts = []
        for gi, w_ in enumerate(POOL_WINDOWS):
            parts.append(ex[:, gi * POOL_GROUP:(gi + 1) * POOL_GROUP] / _pool_cnt(t_idx, w_))
        sums = _pool_windows(jnp.concatenate(parts, axis=1), 0, True)
        dh = jnp.concatenate([s[:tm, :] for s in sums], axis=1) - dp
        _, vjp = jax.vjp(_adaln, x_ref[...], g_ref[...], sc_ref[...], sh_ref[...])
        dx, dg, dsc, dsh = vjp(dh)
        dx_ref[...] = dxn_ref[...] + dx
        first = i == 0
        _acc_add(first, dg_ref, dg)
        _acc_add(first, dsc_ref, dsc)
        _acc_add(first, dsh_ref, dsh)

    nxt = pl.BlockSpec((POOL_HALO, D), lambda i: (jnp.minimum((i + 1) * per, t // POOL_HALO - 1), 0))
    dx, dg, dscale, dshift = pl.pallas_call(
        body_b, name="pool_bwd_b", grid=(nt,),
        in_specs=[_tok_spec(tm, D), nxt, _tok_spec(tm, D), _row_spec(D), _row_spec(D), _row_spec(D), _tok_spec(tm, D)],
        out_specs=[_tok_spec(tm, D), _row_spec(D), _row_spec(D), _row_spec(D)],
        out_shape=[jax.ShapeDtypeStruct((t, D), f32), row, row, row],
        compiler_params=_cp(1),
    )(dpooled, dpooled, x, g, scale, shift, dxn)
    return dx, (dg, dshift, dscale, dgate), dw, dps


def _conv_taps(czx, cw):
    return cw[0:1, :] * _roll_rows(czx, 2) + cw[1:2, :] * _roll_rows(czx, 1) + cw[2:3, :] * czx


def _conv_fwd(p, cw):
    t = p.shape[0]
    tm = _tok_tile(t)
    per = tm // CONV_HALO

    def body(p_ref, pp_ref, cw_ref, q_ref):
        i = pl.program_id(0)
        cz = p_ref[:, D:2 * D] * p_ref[:, 2 * D:3 * D]
        czp = jnp.where(i == 0, 0.0, pp_ref[:, D:2 * D] * pp_ref[:, 2 * D:3 * D])
        conv = _conv_taps(jnp.concatenate([czp, cz], axis=0), cw_ref[...])[CONV_HALO:, :]
        q_ref[...] = (p_ref[:, 0:D] * conv).astype(bf16)

    prev = pl.BlockSpec((CONV_HALO, 3 * D), lambda i: (jnp.maximum(i * per - 1, 0), 0))
    return pl.pallas_call(
        body, name="conv_fwd", grid=(t // tm,),
        in_specs=[_tok_spec(tm, 3 * D), prev, _row_spec(D, 8)],
        out_specs=_tok_spec(tm, D), out_shape=jax.ShapeDtypeStruct((t, D), bf16),
        compiler_params=_cp(1),
    )(p, p, cw)


def _conv_bwd(p, cw, dq):
    t = p.shape[0]
    tm = _tok_tile(t)
    nt = t // tm
    per = tm // CONV_HALO

    def body(p_ref, pp_ref, pn_ref, cw_ref, dq_ref, dqn_ref, dp_ref, dcw_ref):
        i = pl.program_id(0)
        cw_v = cw_ref[...]
        b, c, z = p_ref[:, 0:D], p_ref[:, D:2 * D], p_ref[:, 2 * D:3 * D]
        cz = c * z
        czp = jnp.where(i == 0, 0.0, pp_ref[:, D:2 * D] * pp_ref[:, 2 * D:3 * D])
        czx = jnp.concatenate([czp, cz], axis=0)
        conv = _conv_taps(czx, cw_v)[CONV_HALO:, :]
        dq_v = dq_ref[...]
        dconv = dq_v * b
        dconv_n = jnp.where(i == nt - 1, 0.0, dqn_ref[...] * pn_ref[:, 0:D])
        dcx = jnp.concatenate([dconv, dconv_n], axis=0)
        dcz = (cw_v[2:3, :] * dcx + cw_v[1:2, :] * _roll_rows(dcx, -1) + cw_v[0:1, :] * _roll_rows(dcx, -2))[:tm, :]
        dp_ref[:, 0:D] = (dq_v * conv).astype(bf16)
        dp_ref[:, D:2 * D] = (dcz * z).astype(bf16)
        dp_ref[:, 2 * D:3 * D] = (dcz * c).astype(bf16)
        dw2 = jnp.sum(dconv * cz, axis=0, keepdims=True)
        dw1 = jnp.sum(dconv * _roll_rows(czx, 1)[CONV_HALO:, :], axis=0, keepdims=True)
        dw0 = jnp.sum(dconv * _roll_rows(czx, 2)[CONV_HALO:, :], axis=0, keepdims=True)
        _acc_add(i == 0, dcw_ref, jnp.concatenate([dw0, dw1, dw2, jnp.zeros((5, D), f32)], axis=0))

    prev = pl.BlockSpec((CONV_HALO, 3 * D), lambda i: (jnp.maximum(i * per - 1, 0), 0))
    last = t // CONV_HALO - 1
    nxt3 = pl.BlockSpec((CONV_HALO, 3 * D), lambda i: (jnp.minimum((i + 1) * per, last), 0))
    nxt1 = pl.BlockSpec((CONV_HALO, D), lambda i: (jnp.minimum((i + 1) * per, last), 0))
    return pl.pallas_call(
        body, name="conv_bwd", grid=(nt,),
        in_specs=[_tok_spec(tm, 3 * D), prev, nxt3, _row_spec(D, 8), _tok_spec(tm, D), nxt1],
        out_specs=[_tok_spec(tm, 3 * D), _row_spec(D, 8)],
        out_shape=[jax.ShapeDtypeStruct((t, 3 * D), bf16), jax.ShapeDtypeStruct((8, D), f32)],
        compiler_params=_cp(1),
    )(p, p, p, cw, dq, dq)


def _convmix_fwd(x, ln, w_in, cw, w_out):
    g, shift, scale, gate = ln
    t = x.shape[0]
    tm = _mm_tile(t)
    cs = w_in.shape[2]
    h = _ln_fwd(x, g, scale, shift)
    p = _mmx(h, w_in, "nn", "conv_in", (t // tm, N_DEV, 1),
             pl.BlockSpec((tm, D), lambda i, j, k: (i, 0)),
             pl.BlockSpec((1, D, cs), lambda i, j, k: (j, 0, 0)),
             pl.BlockSpec((tm, cs), lambda i, j, k: (i, j)), jax.ShapeDtypeStruct((t, N_DEV * cs), f32))
    q = _conv_fwd(p, cw)
    y = _mm(q, w_out, "nn", f32, "conv_out")
    return _resid_fwd(x, y, gate, 1.0), (x, h, p, q, y)


def _convmix_bwd(dxn, saved, ln, w_in, cw, w_out):
    x, h, p, q, y = saved
    g, shift, scale, gate = ln
    dy, dgate = _resid_bwd(dxn, y, gate, 1.0)
    dq = _mm(dy, w_out, "nt", f32, "conv_dq")
    dw_out = _mm(q, dy, "tn", f32, "conv_dwout")
    dp, dcw = _conv_bwd(p, cw, dq)
    t = x.shape[0]
    tm = _mm_tile(t)
    tk = _pick(t, (512, 256, 128))
    cs = w_in.shape[2]
    dh = _mmx(dp, w_in, "nt", "conv_dh", (t // tm, 1, N_DEV),
              pl.BlockSpec((tm, cs), lambda i, j, k: (i, k)),
              pl.BlockSpec((1, D, cs), lambda i, j, k: (k, 0, 0)),
              pl.BlockSpec((tm, D), lambda i, j, k: (i, 0)), jax.ShapeDtypeStruct((t, D), f32))
    dw_in = _mmx(h, dp, "tn", "conv_dwin", (N_DEV, 1, t // tk),
                 pl.BlockSpec((tk, D), lambda i, j, k: (k, 0)),
                 pl.BlockSpec((tk, cs), lambda i, j, k: (k, i)),
                 pl.BlockSpec((1, D, cs), lambda i, j, k: (i, 0, 0)), jax.ShapeDtypeStruct((N_DEV, D, cs), f32))
    dx, dg, dscale, dshift = _ln_bwd(x, g, scale, shift, dh, dxn)
    return dx, (dg, dshift, dscale, dgate), dw_in, dcw[0:3], dw_out


def _exact_dot(tri, v):
    v1 = v.astype(bf16)
    r1 = v - v1.astype(f32)
    v2 = r1.astype(bf16)
    v3 = (r1 - v2.astype(f32)).astype(bf16)
    d = lambda p: jnp.dot(tri, p, preferred_element_type=f32)
    return d(v1) + d(v2) + d(v3)


def _fox_cumf(fl, b_f):
    t = fl.shape[0]
    tc = min(t, 256)

    def body(fl_ref, b_ref, f_ref, carry_ref):
        i = pl.program_id(0)

        @pl.when(i == 0)
        def _():
            carry_ref[...] = jnp.zeros_like(carry_ref)

        lf = jax.nn.log_sigmoid(fl_ref[...] + b_ref[...])
        r = lax.broadcasted_iota(jnp.int32, (tc, tc), 0)
        c = lax.broadcasted_iota(jnp.int32, (tc, tc), 1)
        tri = (r >= c).astype(bf16)
        fc = _exact_dot(tri, lf) + carry_ref[0:1, :]
        f_ref[...] = fc
        carry_ref[0:1, :] = fc[tc - 1:tc, :]

    return pl.pallas_call(
        body, name="fox_cumf", grid=(t // tc,),
        in_specs=[_tok_spec(tc, 128), _row_spec(128)],
        out_specs=_tok_spec(tc, 128), out_shape=jax.ShapeDtypeStruct((t, 128), f32),
        scratch_shapes=[pltpu.VMEM((8, 128), f32)], compiler_params=_cp(1),
    )(fl, b_f)


def _fox_cumf_bwd(df, fl, b_f):
    t = fl.shape[0]
    tc = min(t, 256)
    nt = t // tc

    def body(df_ref, fl_ref, b_ref, dfl_ref, db_ref, carry_ref):
        i = pl.program_id(0)

        @pl.when(i == 0)
        def _():
            carry_ref[...] = jnp.zeros_like(carry_ref)

        r = lax.broadcasted_iota(jnp.int32, (tc, tc), 0)
        c = lax.broadcasted_iota(jnp.int32, (tc, tc), 1)
        tri = (r <= c).astype(bf16)
        dlf = _exact_dot(tri, df_ref[...]) + carry_ref[0:1, :]
        carry_ref[0:1, :] = dlf[0:1, :]
        dfl = dlf * jax.nn.sigmoid(-(fl_ref[...] + b_ref[...]))
        dfl_ref[...] = dfl
        _acc_add(i == 0, db_ref, jnp.sum(dfl, axis=0, keepdims=True))

    return pl.pallas_call(
        body, name="fox_cumf_bwd", grid=(nt,),
        in_specs=[_tok_spec(tc, 128, nt, True), _tok_spec(tc, 128, nt, True), _row_spec(128)],
        out_specs=[_tok_spec(tc, 128, nt, True), _row_spec(128)],
        out_shape=[jax.ShapeDtypeStruct((t, 128), f32), jax.ShapeDtypeStruct((1, 128), f32)],
        scratch_shapes=[pltpu.VMEM((8, 128), f32)], compiler_params=_cp(1),
    )(df, fl, b_f)


def _head_norm(v, gain, mult):
    return v * lax.rsqrt(jnp.mean(v * v, axis=-1, keepdims=True) + NORM_EPS) * gain * mult


def _fox_qknorm(qkv, q_gain, k_gain):
    t = qkv.shape[2]
    tm = _tok_tile(t)

    def body(q_ref, k_ref, v_ref, qg_ref, kg_ref, o_ref):
        o_ref[0, 0] = _head_norm(q_ref[0, 0], qg_ref[...], FOX_HEAD_DIM ** -0.5).astype(bf16)
        o_ref[1, 0] = _head_norm(k_ref[0, 0], kg_ref[...], 1.0).astype(bf16)
        o_ref[2, 0] = v_ref[0, 0].astype(bf16)

    blk = lambda s: pl.BlockSpec((1, 1, tm, FOX_HEAD_DIM), lambda h, i: (s, h, i, 0))
    gspec = pl.BlockSpec((1, FOX_HEAD_DIM), lambda h, i: (0, 0))
    return pl.pallas_call(
        body, name="fox_qknorm", grid=(FOX_HEADS, t // tm),
        in_specs=[blk(0), blk(1), blk(2), gspec, gspec],
        out_specs=pl.BlockSpec((3, 1, tm, FOX_HEAD_DIM), lambda h, i: (0, h, i, 0)),
        out_shape=jax.ShapeDtypeStruct((3, FOX_HEADS, t, FOX_HEAD_DIM), bf16),
        compiler_params=_cp(2),
    )(qkv, qkv, qkv, q_gain, k_gain)


def _fox_qknorm_bwd(qkv, q_gain, k_gain, dqn, dkn):
    t = qkv.shape[2]
    tm = _tok_tile(t)

    def body(q_ref, k_ref, qg_ref, kg_ref, dqn_ref, dkn_ref, o_ref, dqg_ref, dkg_ref):
        first = jnp.logical_and(pl.program_id(0) == 0, pl.program_id(1) == 0)
        _, vq = jax.vjp(lambda v, gn: _head_norm(v, gn, FOX_HEAD_DIM ** -0.5), q_ref[0, 0], qg_ref[...])
        dq, dqg = vq(dqn_ref[0])
        _, vk = jax.vjp(lambda v, gn: _head_norm(v, gn, 1.0), k_ref[0, 0], kg_ref[...])
        dk, dkg = vk(dkn_ref[0])
        o_ref[0, 0] = dq
        o_ref[1, 0] = dk
        _acc_add(first, dqg_ref, dqg)
        _acc_add(first, dkg_ref, dkg)

    blk = lambda s: pl.BlockSpec((1, 1, tm, FOX_HEAD_DIM), lambda h, i: (s, h, i, 0))
    hblk = pl.BlockSpec((1, tm, FOX_HEAD_DIM), lambda h, i: (h, i, 0))
    gspec = pl.BlockSpec((1, FOX_HEAD_DIM), lambda h, i: (0, 0))
    gshape = jax.ShapeDtypeStruct((1, FOX_HEAD_DIM), f32)
    return pl.pallas_call(
        body, name="fox_qknorm_bwd", grid=(FOX_HEADS, t // tm),
        in_specs=[blk(0), blk(1), gspec, gspec, hblk, hblk],
        out_specs=[pl.BlockSpec((2, 1, tm, FOX_HEAD_DIM), lambda h, i: (0, h, i, 0)), gspec, gspec],
        out_shape=[jax.ShapeDtypeStruct((2, FOX_HEADS, t, FOX_HEAD_DIM), f32), gshape, gshape],
        compiler_params=_cp(2),
    )(qkv, qkv, q_gain, k_gain, dqn, dkn)


def _fox_block(t):
    return min(t, 512)


FOX_STRIP = 64
FOX_HEADS_PER_STEP = 2


def _fox_mask(s, row0):
    r = row0 + lax.broadcasted_iota(jnp.int32, s.shape, 0)
    c = lax.broadcasted_iota(jnp.int32, s.shape, 1)
    return jnp.where(r >= c, s, -jnp.inf)


def _strips(tb):
    ts = min(tb, FOX_STRIP)
    return [(r * ts, slice(r * ts, (r + 1) * ts)) for r in range(tb // ts)]


def _fox_attn_fwd(qkvn, fcol, fref):
    t = qkvn.shape[2]
    tb = _fox_block(t)
    nq = t // tb
    dh = FOX_HEAD_DIM

    hp = FOX_HEADS_PER_STEP

    def body(q_ref, k_ref, v_ref, fc_ref, fr_ref, o_ref, lse_ref):
        i = pl.program_id(1)

        def step(j, carry, diag):
            off = pl.multiple_of(j * tb, tb)
            out = []
            for hh in range(hp):
                m, l, acc = carry[hh]
                kj = k_ref[0, hh, pl.ds(off, tb), :]
                vj = v_ref[0, hh, pl.ds(off, tb), :]
                bias = fr_ref[hh, pl.ds(i, 1), 0:1] - fc_ref[hh, :, pl.ds(off, tb)]
                s = lax.dot_general(q_ref[0, hh], kj, _DN["nt"], preferred_element_type=f32) + bias
                if diag:
                    s = _fox_mask(s, 0)
                m_new = jnp.maximum(m, jnp.max(s, axis=-1, keepdims=True))
                alpha = jnp.exp(m - m_new)
                p = jnp.exp(s - m_new)
                l = alpha * l + jnp.sum(p, axis=-1, keepdims=True)
                p_hi = p.astype(bf16)
                p_lo = (p - p_hi.astype(f32)).astype(bf16)
                pv = jnp.dot(p_hi, vj, preferred_element_type=f32) + jnp.dot(p_lo, vj, preferred_element_type=f32)
                out.append((m_new, l, alpha * acc + pv))
            return tuple(out)

        one = (jnp.full((tb, 1), -jnp.inf, f32), jnp.zeros((tb, 1), f32), jnp.zeros((tb, dh), f32))
        carry = lax.fori_loop(0, i, lambda j, c: step(j, c, False), (one,) * hp)
        for hh, (m, l, acc) in enumerate(step(i, carry, True)):
            o_ref[hh] = acc / l
            lse_ref[hh] = jnp.broadcast_to(m + jnp.log(l), (tb, 128))

    return pl.pallas_call(
        body, name="fox_attn_fwd", grid=(FOX_HEADS // hp, nq),
        in_specs=[pl.BlockSpec((1, hp, tb, dh), lambda h, i: (0, h, i, 0)),
                  pl.BlockSpec((1, hp, t, dh), lambda h, i: (1, h, 0, 0)),
                  pl.BlockSpec((1, hp, t, dh), lambda h, i: (2, h, 0, 0)),
                  pl.BlockSpec((hp, 1, t), lambda h, i: (h, 0, 0)),
                  pl.BlockSpec((hp, nq, 128), lambda h, i: (h, 0, 0))],
        out_specs=[pl.BlockSpec((hp, tb, dh), lambda h, i: (h, i, 0)),
                   pl.BlockSpec((hp, tb, 128), lambda h, i: (h, i, 0))],
        out_shape=[jax.ShapeDtypeStruct((FOX_HEADS, t, dh), f32), jax.ShapeDtypeStruct((FOX_HEADS, t, 128), f32)],
        compiler_params=_cp(2),
    )(qkvn, qkvn, qkvn, fcol, fref)


def _fox_delta(o, do):
    t = o.shape[1]
    tm = _tok_tile(t)
    dh = FOX_HEAD_DIM

    def body(o_ref, do_ref, dl_ref, dob_ref):
        dob = do_ref[0].astype(bf16)
        dob_ref[0] = dob
        dl_ref[0] = jnp.broadcast_to(jnp.sum(dob.astype(f32) * o_ref[0], axis=-1, keepdims=True), (tm, 128))

    hb = pl.BlockSpec((1, tm, dh), lambda h, i: (h, i, 0))
    return pl.pallas_call(
        body, name="fox_delta", grid=(FOX_HEADS, t // tm), in_specs=[hb, hb],
        out_specs=[pl.BlockSpec((1, tm, 128), lambda h, i: (h, i, 0)), hb],
        out_shape=[jax.ShapeDtypeStruct((FOX_HEADS, t, 128), f32), jax.ShapeDtypeStruct((FOX_HEADS, t, dh), bf16)],
        compiler_params=_cp(2),
    )(o, do)


def _fox_attn_bwd_kv(qkvn, fcol, fref, lse, delta, dob):
    t = qkvn.shape[2]
    tb = _fox_block(t)
    nq = t // tb
    dh = FOX_HEAD_DIM

    def body(q_ref, k_ref, v_ref, fc_ref, fr_ref, lse_ref, dl_ref, do_ref, dk_ref, dv_ref, df_ref, dq_ref,
             s_ref, dp_ref, p_ref, ds_ref):
        j = pl.program_id(1)
        kj = k_ref[0, 0]
        vj = v_ref[0, 0]
        fcol_j = fc_ref[0]
        dk_ref[...] = jnp.zeros_like(dk_ref)
        dv_ref[...] = jnp.zeros_like(dv_ref)
        df_ref[...] = jnp.zeros_like(df_ref)

        @pl.when(j == 0)
        def _():
            dq_ref[...] = jnp.zeros_like(dq_ref)

        def step(i, diag):
            off = pl.multiple_of(i * tb, tb)
            qi = q_ref[0, 0, pl.ds(off, tb), :]
            doi = do_ref[0, pl.ds(off, tb), :]
            s_ref[...] = lax.dot_general(qi, kj, _DN["nt"], preferred_element_type=f32)
            dp_ref[...] = lax.dot_general(doi, vj, _DN["nt"], preferred_element_type=f32)
            bias = fr_ref[0, pl.ds(i, 1), 0:1] - fcol_j
            df = jnp.zeros((1, tb), f32)
            for row0, rs in _strips(tb):
                rows = pl.ds(off + row0, rs.stop - rs.start)
                s = s_ref[rs, :] + bias
                if diag:
                    s = _fox_mask(s, row0)
                p = jnp.exp(s - lse_ref[0, rows, 0:1])
                p_ref[rs, :] = p.astype(bf16)
                ds = p * (dp_ref[rs, :] - dl_ref[0, rows, 0:1])
                ds_ref[rs, :] = ds.astype(bf16)
                df = df + jnp.sum(ds, axis=0, keepdims=True)
            dv_ref[0] += lax.dot_general(p_ref[...], doi, _DN["tn"], preferred_element_type=f32)
            dk_ref[0] += lax.dot_general(ds_ref[...], qi, _DN["tn"], preferred_element_type=f32)
            dq_ref[0, pl.ds(off, tb), :] += jnp.dot(ds_ref[...], kj, preferred_element_type=f32)
            df_ref[0] -= df

        def loop_body(i, carry):
            step(i, False)
            return carry

        step(j, True)
        lax.fori_loop(j + 1, nq, loop_body, 0)

    scratch = [pltpu.VMEM((tb, tb), f32), pltpu.VMEM((tb, tb), f32), pltpu.VMEM((tb, tb), bf16),
               pltpu.VMEM((tb, tb), bf16)]
    hb = pl.BlockSpec((1, tb, dh), lambda h, j: (h, j, 0))
    full = pl.BlockSpec((1, t, dh), lambda h, j: (h, 0, 0))
    full128 = pl.BlockSpec((1, t, 128), lambda h, j: (h, 0, 0))
    hshape = jax.ShapeDtypeStruct((FOX_HEADS, t, dh), f32)
    return pl.pallas_call(
        body, name="fox_attn_bwd_kv", grid=(FOX_HEADS, nq),
        in_specs=[pl.BlockSpec((1, 1, t, dh), lambda h, j: (0, h, 0, 0)),
                  pl.BlockSpec((1, 1, tb, dh), lambda h, j: (1, h, j, 0)),
                  pl.BlockSpec((1, 1, tb, dh), lambda h, j: (2, h, j, 0)),
                  pl.BlockSpec((1, 1, tb), lambda h, j: (h, 0, j)),
                  pl.BlockSpec((1, nq, 128), lambda h, j: (h, 0, 0)),
                  full128, full128, full],
        out_specs=[hb, hb, pl.BlockSpec((1, 1, tb), lambda h, j: (h, 0, j)), full],
        out_shape=[hshape, hshape, jax.ShapeDtypeStruct((FOX_HEADS, 1, t), f32), hshape],
        scratch_shapes=scratch, compiler_params=_cp(2),
    )(qkvn, qkvn, qkvn, fcol, fref, lse, delta, dob)


FOX_PAIRS = FOX_HEADS // 2
FOX_PW = 2 * FOX_HEAD_DIM


def _half(shape, hh):
    lane = lax.broadcasted_iota(jnp.int32, shape, len(shape) - 1)
    return lane < FOX_HEAD_DIM if hh == 0 else lane >= FOX_HEAD_DIM


def _by_half(v0, v1):
    return jnp.where(_half(v0.shape, 0), v0, v1)


def _half_sums(v):
    first = _half(v.shape, 0)
    s0 = jnp.sum(jnp.where(first, v, 0.0), axis=-1, keepdims=True)
    s1 = jnp.sum(jnp.where(first, 0.0, v), axis=-1, keepdims=True)
    return jnp.where(first, s0, s1)


def _pair_norm(v, gain, mult):
    ms = _half_sums(v * v) * (1.0 / FOX_HEAD_DIM)
    return v * lax.rsqrt(ms + NORM_EPS) * gain * mult


def _pair_cols(p):
    return slice(p * FOX_PW, (p + 1) * FOX_PW)


def _pair_qknorm(proj, q_gain, k_gain):
    t = proj.shape[0]
    tm = _tok_tile(t)

    def body(q_ref, k_ref, v_ref, qg_ref, kg_ref, o_ref):
        for p in range(FOX_PAIRS):
            cb = _pair_cols(p)
            o_ref[:, cb] = _pair_norm(q_ref[:, cb], qg_ref[...], FOX_HEAD_DIM ** -0.5).astype(bf16)
            o_ref[:, D + p * FOX_PW:D + (p + 1) * FOX_PW] = _pair_norm(k_ref[:, cb], kg_ref[...], 1.0).astype(bf16)
        o_ref[:, 2 * D:3 * D] = v_ref[...].astype(bf16)

    part = lambda s: pl.BlockSpec((tm, D), lambda i: (i, s))
    return pl.pallas_call(
        body, name="fox_qknorm", grid=(t // tm,),
        in_specs=[part(0), part(1), part(2), _row_spec(FOX_PW), _row_spec(FOX_PW)],
        out_specs=_tok_spec(tm, 3 * D), out_shape=jax.ShapeDtypeStruct((t, 3 * D), bf16), compiler_params=_cp(1),
    )(proj, proj, proj, q_gain, k_gain)


def _pair_qknorm_bwd(proj, q_gain, k_gain, dqn, dkn):
    t = proj.shape[0]
    tm = _tok_tile(t)

    def body(q_ref, k_ref, qg_ref, kg_ref, dqn_ref, dkn_ref, o_ref, dqg_ref, dkg_ref):
        dqg = jnp.zeros((1, FOX_PW), f32)
        dkg = jnp.zeros((1, FOX_PW), f32)
        for p in range(FOX_PAIRS):
            cb = _pair_cols(p)
            _, vq = jax.vjp(lambda v, gn: _pair_norm(v, gn, FOX_HEAD_DIM ** -0.5), q_ref[:, cb], qg_ref[...])
            dq, g1 = vq(dqn_ref[:, cb])
            _, vk = jax.vjp(lambda v, gn: _pair_norm(v, gn, 1.0), k_ref[:, cb], kg_ref[...])
            dk, g2 = vk(dkn_ref[:, cb])
            o_ref[:, cb] = dq.astype(bf16)
            o_ref[:, D + p * FOX_PW:D + (p + 1) * FOX_PW] = dk.astype(bf16)
            dqg, dkg = dqg + g1, dkg + g2
        first = pl.program_id(0) == 0
        _acc_add(first, dqg_ref, dqg + pltpu.roll(dqg, FOX_HEAD_DIM, 1))
        _acc_add(first, dkg_ref, dkg + pltpu.roll(dkg, FOX_HEAD_DIM, 1))

    part = lambda s: pl.BlockSpec((tm, D), lambda i: (i, s))
    gshape = jax.ShapeDtypeStruct((1, FOX_PW), f32)
    return pl.pallas_call(
        body, name="fox_qknorm_bwd", grid=(t // tm,),
        in_specs=[part(0), part(1), _row_spec(FOX_PW), _row_spec(FOX_PW), _tok_spec(tm, D), _tok_spec(tm, D)],
        out_specs=[_tok_spec(tm, 2 * D), _row_spec(FOX_PW), _row_spec(FOX_PW)],
        out_shape=[jax.ShapeDtypeStruct((t, 2 * D), bf16), gshape, gshape], compiler_params=_cp(1),
    )(proj, proj, q_gain, k_gain, dqn, dkn)


def _pair_delta(o, do):
    t = o.shape[0]
    tm = _tok_tile(t)

    def body(o_ref, do_ref, dl_ref, dob_ref):
        dob = do_ref[...].astype(bf16)
        dob_ref[...] = dob
        prod = dob.astype(f32) * o_ref[...]
        for p in range(FOX_PAIRS):
            dl_ref[p] = _half_sums(prod[:, _pair_cols(p)])

    return pl.pallas_call(
        body, name="fox_delta", grid=(t // tm,), in_specs=[_tok_spec(tm, D), _tok_spec(tm, D)],
        out_specs=[pl.BlockSpec((FOX_PAIRS, tm, FOX_PW), lambda i: (0, i, 0)), _tok_spec(tm, D)],
        out_shape=[jax.ShapeDtypeStruct((FOX_PAIRS, t, FOX_PW), f32), jax.ShapeDtypeStruct((t, D), bf16)],
        compiler_params=_cp(1),
    )(o, do)


def _pair_attn_fwd(qkvn, fcol, fref):
    t = qkvn.shape[0]
    tb = _fox_block(t)
    nq = t // tb
    pw = FOX_PW
    kcol, vcol = D // pw, 2 * D // pw

    def body(q_ref, k_ref, v_ref, fc_ref, fr_ref, o_ref, ob_ref, lse_ref):
        i = pl.program_id(1)
        qp = q_ref[...]
        qh = [jnp.where(_half(qp.shape, hh), qp, jnp.zeros_like(qp)) for hh in range(2)]

        def step(j, carry, diag):
            off = pl.multiple_of(j * tb, tb)
            kj = k_ref[pl.ds(off, tb), :]
            vj = v_ref[pl.ds(off, tb), :]
            out = []
            for hh in range(2):
                m, l, acc = carry[hh]
                bias = fr_ref[0, hh, pl.ds(i, 1), 0:1] - fc_ref[0, hh:hh + 1, pl.ds(off, tb)]
                s = lax.dot_general(qh[hh], kj, _DN["nt"], preferred_element_type=f32) + bias
                if diag:
                    s = _fox_mask(s, 0)
                m_new = jnp.maximum(m, jnp.max(s, axis=-1, keepdims=True))
                alpha = jnp.exp(m - m_new)
                p = jnp.exp(s - m_new)
                l = alpha * l + jnp.sum(p, axis=-1, keepdims=True)
                p_hi = p.astype(bf16)
                p_lo = (p - p_hi.astype(f32)).astype(bf16)
                pv = jnp.dot(p_hi, vj, preferred_element_type=f32) + jnp.dot(p_lo, vj, preferred_element_type=f32)
                out.append((m_new, l, alpha * acc + pv))
            return tuple(out)

        one = (jnp.full((tb, 1), -jnp.inf, f32), jnp.zeros((tb, 1), f32), jnp.zeros((tb, pw), f32))
        carry = lax.fori_loop(0, i, lambda j, c: step(j, c, False), (one, one))
        (m0, l0, a0), (m1, l1, a1) = step(i, carry, True)
        o = _by_half(a0 / l0, a1 / l1)
        o_ref[...] = o
        ob_ref[...] = o.astype(bf16)
        lse_ref[0] = _by_half(jnp.broadcast_to(m0 + jnp.log(l0), (tb, pw)), jnp.broadcast_to(m1 + jnp.log(l1), (tb, pw)))

    return pl.pallas_call(
        body, name="fox_attn_fwd", grid=(FOX_PAIRS, nq),
        in_specs=[pl.BlockSpec((tb, pw), lambda p, i: (i, p)),
                  pl.BlockSpec((t, pw), lambda p, i: (0, kcol + p)),
                  pl.BlockSpec((t, pw), lambda p, i: (0, vcol + p)),
                  pl.BlockSpec((1, 2, t), lambda p, i: (p, 0, 0)),
                  pl.BlockSpec((1, 2, nq, 128), lambda p, i: (p, 0, 0, 0))],
        out_specs=[pl.BlockSpec((tb, pw), lambda p, i: (i, p)), pl.BlockSpec((tb, pw), lambda p, i: (i, p)),
                   pl.BlockSpec((1, tb, pw), lambda p, i: (p, i, 0))],
        out_shape=[jax.ShapeDtypeStruct((t, D), f32), jax.ShapeDtypeStruct((t, D), bf16),
                   jax.ShapeDtypeStruct((FOX_PAIRS, t, pw), f32)],
        compiler_params=_cp(2),
    )(qkvn, qkvn, qkvn, fcol, fref)


def _pair_attn_bwd(qkvn, fcol, fref, lse, delta, dob):
    t = qkvn.shape[0]
    tb = _fox_block(t)
    nq = t // tb
    pw = FOX_PW
    kcol, vcol = D // pw, 2 * D // pw

    def body(q_ref, k_ref, v_ref, fc_ref, fr_ref, lse_ref, dl_ref, do_ref, dk_ref, dv_ref, df_ref, dq_ref,
             s_ref, dp_ref, p_ref, ds_ref):
        j = pl.program_id(1)
        kj = k_ref[...]
        vj = v_ref[...]
        dk_ref[...] = jnp.zeros_like(dk_ref)
        dv_ref[...] = jnp.zeros_like(dv_ref)
        df_ref[...] = jnp.zeros_like(df_ref)

        @pl.when(j == 0)
        def _():
            dq_ref[...] = jnp.zeros_like(dq_ref)

        def step(i, diag):
            off = pl.multiple_of(i * tb, tb)
            qi = q_ref[pl.ds(off, tb), :]
            doi = do_ref[pl.ds(off, tb), :]
            parts = []
            for hh in range(2):
                mine = _half(qi.shape, hh)
                c0 = hh * FOX_HEAD_DIM
                s_ref[hh] = lax.dot_general(jnp.where(mine, qi, jnp.zeros_like(qi)), kj, _DN["nt"],
                                            preferred_element_type=f32)
                dp_ref[hh] = lax.dot_general(jnp.where(mine, doi, jnp.zeros_like(doi)), vj, _DN["nt"],
                                             preferred_element_type=f32)
                bias = fr_ref[0, hh, pl.ds(i, 1), 0:1] - fc_ref[0, hh:hh + 1, :]
                df = jnp.zeros((1, tb), f32)
                for row0, rs in _strips(tb):
                    rows = pl.ds(off + row0, rs.stop - rs.start)
                    s = s_ref[hh, rs, :] + bias
                    if diag:
                        s = _fox_mask(s, row0)
                    p = jnp.exp(s - lse_ref[0, rows, c0:c0 + 1])
                    p_ref[hh, rs, :] = p.astype(bf16)
                    ds = p * (dp_ref[hh, rs, :] - dl_ref[0, rows, c0:c0 + 1])
                    ds_ref[hh, rs, :] = ds.astype(bf16)
                    df = df + jnp.sum(ds, axis=0, keepdims=True)
                df_ref[0, hh:hh + 1, :] -= df
                parts.append((lax.dot_general(p_ref[hh], doi, _DN["tn"], preferred_element_type=f32),
                              lax.dot_general(ds_ref[hh], qi, _DN["tn"], preferred_element_type=f32),
                              jnp.dot(ds_ref[hh], kj, preferred_element_type=f32)))
            dv_ref[...] += _by_half(parts[0][0], parts[1][0])
            dk_ref[...] += _by_half(parts[0][1], parts[1][1])
            dq_ref[pl.ds(off, tb), :] += _by_half(parts[0][2], parts[1][2])

        def loop_body(i, carry):
            step(i, False)
            return carry

        step(j, True)
        lax.fori_loop(j + 1, nq, loop_body, 0)

    scratch = [pltpu.VMEM((2, tb, tb), f32), pltpu.VMEM((2, tb, tb), f32), pltpu.VMEM((2, tb, tb), bf16),
               pltpu.VMEM((2, tb, tb), bf16)]
    blk = lambda c0: pl.BlockSpec((tb, pw), lambda p, j: (j, c0 + p))
    full = lambda c0: pl.BlockSpec((t, pw), lambda p, j: (0, c0 + p))
    stat = pl.BlockSpec((1, t, pw), lambda p, j: (p, 0, 0))
    nat = jax.ShapeDtypeStruct((t, D), f32)
    return pl.pallas_call(
        body, name="fox_attn_bwd", grid=(FOX_PAIRS, nq),
        in_specs=[full(0), blk(kcol), blk(vcol), pl.BlockSpec((1, 2, tb), lambda p, j: (p, 0, j)),
                  pl.BlockSpec((1, 2, nq, 128), lambda p, j: (p, 0, 0, 0)), stat, stat, full(0)],
        out_specs=[blk(0), blk(0), pl.BlockSpec((1, 2, tb), lambda p, j: (p, 0, j)), full(0)],
        out_shape=[nat, nat, jax.ShapeDtypeStruct((FOX_PAIRS, 2, t), f32), nat],
        scratch_shapes=scratch, compiler_params=_cp(2),
    )(qkvn, qkvn, qkvn, fcol, fref, lse, delta, dob)


def _fox_fwd(x, ln, w_in, b_f, q_gain, k_gain, w_o):
    g, shift, scale, gate = ln
    t = x.shape[0]
    tb = _fox_block(t)
    h = _ln_fwd(x, g, scale, shift)
    proj = _mm(h, w_in, "nn", f32, "fox_in")
    fl = proj[:, 3 * D:3 * D + 128]
    fcum = _fox_cumf(fl, b_f)
    fcol = fcum[:, :FOX_HEADS].T.reshape(FOX_PAIRS, 2, t)
    fref = jnp.broadcast_to(fcol[:, :, ::tb][..., None], (FOX_PAIRS, 2, t // tb, 128))
    gains = (jnp.tile(q_gain, (1, 2)), jnp.tile(k_gain, (1, 2)))
    qkvn = _pair_qknorm(proj, *gains)
    o, ob, lse = _pair_attn_fwd(qkvn, fcol, fref)
    y = _mm(ob, w_o, "nn", f32, "fox_out")
    return _resid_fwd(x, y, gate, 1.0), (x, h, proj, fl, fcol, fref, gains, qkvn, o, lse, ob, y)


def _fox_bwd(dxn, saved, ln, w_in, b_f, w_o):
    x, h, proj, fl, fcol, fref, gains, qkvn, o, lse, ob, y = saved
    g, shift, scale, gate = ln
    t = x.shape[0]
    dy, dgate = _resid_bwd(dxn, y, gate, 1.0)
    do = _mm(dy, w_o, "nt", f32, "fox_do")
    dw_o = _mm(ob, dy, "tn", f32, "fox_dwo")
    delta, dob = _pair_delta(o, do)
    dkn, dv, dfcol, dqn = _pair_attn_bwd(qkvn, fcol, fref, lse, delta, dob)
    dqk, dqg, dkg = _pair_qknorm_bwd(proj, *gains, dqn, dkn)
    df = jnp.pad(dfcol.reshape(FOX_HEADS, t).T, ((0, 0), (0, 128 - FOX_HEADS)))
    dfl, db_f = _fox_cumf_bwd(df, fl, b_f)
    dproj = jnp.concatenate([dqk, dv.astype(bf16), dfl.astype(bf16)], axis=1)
    dh = _mm(dproj, w_in, "nt", f32, "fox_dh")
    dw_in = _mm(h, dproj, "tn", f32, "fox_dwin")
    dx, dg, dscale, dshift = _ln_bwd(x, g, scale, shift, dh, dxn)
    return (dx, (dg, dshift, dscale, dgate), dw_in, db_f, dqg[:, :FOX_HEAD_DIM], dkg[:, :FOX_HEAD_DIM], dw_o)


def _s5_disc(lam_re, lam_im, log_dt, b_re, b_im):
    dt = jnp.exp(log_dt)
    mag = jnp.exp(lam_re * dt)
    lb_re, lb_im = mag * jnp.cos(lam_im * dt), mag * jnp.sin(lam_im * dt)
    den = lam_re * lam_re + lam_im * lam_im
    nr, ni = lb_re - 1.0, lb_im
    k_re = (nr * lam_re + ni * lam_im) / den
    k_im = (ni * lam_re - nr * lam_im) / den
    return lb_re, lb_im, k_re * b_re - k_im * b_im, k_re * b_im + k_im * b_re


def _s5_prep(lam_re, lam_im, log_dt, b_re, b_im):
    def body(ar_ref, ai_ref, dt_ref, br_ref, bi_ref, lr_ref, li_ref, bbr_ref, bbi_ref):
        lr, li, bbr, bbi = _s5_disc(ar_ref[...], ai_ref[...], dt_ref[...], br_ref[...], bi_ref[...])
        lr_ref[...] = lr
        li_ref[...] = li
        bbr_ref[...] = bbr
        bbi_ref[...] = bbi

    small = jax.ShapeDtypeStruct(lam_re.shape, f32)
    bigs = jax.ShapeDtypeStruct(b_re.shape, f32)
    return pl.pallas_call(body, name="s5_prep", out_shape=[small, small, bigs, bigs])(lam_re, lam_im, log_dt, b_re, b_im)


def _s5_prep_bwd(lam_re, lam_im, log_dt, b_re, b_im, dlr, dli, dbbr, dbbi):
    def body(ar_ref, ai_ref, dt_ref, br_ref, bi_ref, dlr_ref, dli_ref, dbbr_ref, dbbi_ref,
             dar_ref, dai_ref, ddt_ref, dbr_ref, dbi_ref):
        _, vjp = jax.vjp(_s5_disc, ar_ref[...], ai_ref[...], dt_ref[...], br_ref[...], bi_ref[...])
        dar, dai, ddt, dbr, dbi = vjp((dlr_ref[...], dli_ref[...], dbbr_ref[...], dbbi_ref[...]))
        dar_ref[...] = dar
        dai_ref[...] = dai
        ddt_ref[...] = jnp.broadcast_to(jnp.sum(ddt, axis=-1, keepdims=True), ddt.shape)
        dbr_ref[...] = dbr
        dbi_ref[...] = dbi

    small = jax.ShapeDtypeStruct(lam_re.shape, f32)
    bigs = jax.ShapeDtypeStruct(b_re.shape, f32)
    return pl.pallas_call(body, name="s5_prep_bwd", out_shape=[small, small, small, bigs, bigs])(
        lam_re, lam_im, log_dt, b_re, b_im, dlr, dli, dbbr, dbbi)


def _s5_tile(t):
    return min(t, 128)


def _s5_blk(k, width):
    return slice(k * width, (k + 1) * width)


def _s5_in_bd(bb):
    b4 = bb.reshape(S5_BLOCKS, 8, S5_GROUP, S5_STATE)
    return jnp.einsum("kgin,gh->kgihn", b4, jnp.eye(8, dtype=bb.dtype)).reshape(S5_BLOCKS, S5_BCH, S5_BST)


def _s5_in_bd_diag(bd):
    b5 = bd.reshape(S5_BLOCKS, 8, S5_GROUP, 8, S5_STATE)
    return jnp.einsum("kgihn,gh->kgin", b5, jnp.eye(8, dtype=bd.dtype)).reshape(S5_GROUPS, S5_GROUP, S5_STATE)


def _s5_out_bd(c):
    c4 = c.reshape(S5_BLOCKS, 8, S5_GROUP, S5_STATE)
    return jnp.einsum("kgin,gh->kgnhi", c4, jnp.eye(8, dtype=c.dtype)).reshape(S5_BLOCKS, S5_BST, S5_BCH)


def _s5_out_bd_diag(bd):
    c5 = bd.reshape(S5_BLOCKS, 8, S5_STATE, 8, S5_GROUP)
    return jnp.einsum("kgnhi,gh->kgin", c5, jnp.eye(8, dtype=bd.dtype)).reshape(S5_GROUPS, S5_GROUP, S5_STATE)


def _s5_scan_fwd(x, ln, lb_re, lb_im, bbr_bd, bbi_bd, cr_bd, ci_bd, dskip):
    g, shift, scale, _ = ln
    t = x.shape[0]
    tm = _s5_tile(t)
    ns = S5_NSTATE

    def body(x_ref, g_ref, sc_ref, sh_ref, ar_ref, ai_ref, bbr_ref, bbi_ref, cr_ref, ci_ref, d_ref,
             yy_ref, xr_ref, xi_ref, cre_ref, cim_ref):
        @pl.when(pl.program_id(0) == 0)
        def _():
            cre_ref[...] = jnp.zeros_like(cre_ref)
            cim_ref[...] = jnp.zeros_like(cim_ref)

        h = _adaln(x_ref[...], g_ref[...], sc_ref[...], sh_ref[...])
        ub = h.astype(bf16)
        for k in range(S5_BLOCKS):
            uk = ub[:, _s5_blk(k, S5_BCH)]
            xr_ref[:, _s5_blk(k, S5_BST)] = jnp.dot(uk, bbr_ref[k], preferred_element_type=f32)
            xi_ref[:, _s5_blk(k, S5_BST)] = jnp.dot(uk, bbi_ref[k], preferred_element_type=f32)
        ar, ai = ar_ref[...], ai_ref[...]

        def step(tt, carry):
            sr, si = carry
            row = pl.ds(tt, 1)
            nr = (ar * sr - ai * si) + xr_ref[row, :]
            ni = (ar * si + ai * sr) + xi_ref[row, :]
            xr_ref[row, :] = nr
            xi_ref[row, :] = ni
            return nr, ni

        sr, si = lax.fori_loop(0, tm, step, (cre_ref[0:1, :], cim_ref[0:1, :]), unroll=2)
        cre_ref[0:1, :] = sr
        cim_ref[0:1, :] = si
        for k in range(S5_BLOCKS):
            sb = _s5_blk(k, S5_BST)
            yk = (jnp.dot(xr_ref[:, sb].astype(bf16), cr_ref[k], preferred_element_type=f32)
                  - jnp.dot(xi_ref[:, sb].astype(bf16), ci_ref[k], preferred_element_type=f32))
            cb = _s5_blk(k, S5_BCH)
            yy_ref[:, cb] = yk + d_ref[:, cb] * h[:, cb]

    bd_in = pl.BlockSpec((S5_BLOCKS, S5_BCH, S5_BST), lambda i: (0, 0, 0))
    bd_out = pl.BlockSpec((S5_BLOCKS, S5_BST, S5_BCH), lambda i: (0, 0, 0))
    st = jax.ShapeDtypeStruct((t, ns), f32)
    return pl.pallas_call(
        body, name="s5_scan_fwd", grid=(t // tm,),
        in_specs=[_tok_spec(tm, D), _row_spec(D), _row_spec(D), _row_spec(D), _row_spec(ns), _row_spec(ns),
                  bd_in, bd_in, bd_out, bd_out, _row_spec(D)],
        out_specs=[_tok_spec(tm, D), _tok_spec(tm, ns), _tok_spec(tm, ns)],
        out_shape=[jax.ShapeDtypeStruct((t, D), f32), st, st],
        scratch_shapes=[pltpu.VMEM((8, ns), f32), pltpu.VMEM((8, ns), f32)],
        compiler_params=_cp(1),
    )(x, g, scale, shift, lb_re, lb_im, bbr_bd, bbi_bd, cr_bd, ci_bd, dskip)


def _s5_scan_bwd(dyy, x, ln, xr, xi, lb_re, lb_im, bbr_bd, bbi_bd, cr_bd, ci_bd, dskip):
    g, shift, scale, _ = ln
    t = x.shape[0]
    tm = _s5_tile(t)
    nt = t // tm
    ns = S5_NSTATE
    per = tm // 8

    def body(dyy_ref, x_ref, g_ref, sc_ref, sh_ref, xr_ref, xi_ref, xrp_ref, xip_ref, ar_ref, ai_ref,
             bbr_ref, bbi_ref, cr_ref, ci_ref, d_ref,
             du_ref, dar_ref, dai_ref, dbbr_ref, dbbi_ref, dcr_ref, dci_ref, dd_ref,
             gr_ref, gi_ref, cre_ref, cim_ref):
        i = pl.program_id(0)
        first = i == 0

        @pl.when(first)
        def _():
            cre_ref[...] = jnp.zeros_like(cre_ref)
            cim_ref[...] = jnp.zeros_like(cim_ref)

        h = _adaln(x_ref[...], g_ref[...], sc_ref[...], sh_ref[...])
        ub = h.astype(bf16)
        dyy_v = dyy_ref[...]
        dyb = dyy_v.astype(bf16)
        for k in range(S5_BLOCKS):
            dk = dyb[:, _s5_blk(k, S5_BCH)]
            sb = _s5_blk(k, S5_BST)
            gr_ref[:, sb] = lax.dot_general(dk, cr_ref[k], _DN["nt"], preferred_element_type=f32)
            gi_ref[:, sb] = -lax.dot_general(dk, ci_ref[k], _DN["nt"], preferred_element_type=f32)
        ar, ai = ar_ref[...], ai_ref[...]

        def step(s, carry):
            nr_, ni_ = carry
            row = pl.ds(tm - 1 - s, 1)
            nr = gr_ref[row, :] + (ar * nr_ + ai * ni_)
            ni = gi_ref[row, :] + (ar * ni_ - ai * nr_)
            gr_ref[row, :] = nr
            gi_ref[row, :] = ni
            return nr, ni

        nr, ni = lax.fori_loop(0, tm, step, (cre_ref[0:1, :], cim_ref[0:1, :]), unroll=2)
        cre_ref[0:1, :] = nr
        cim_ref[0:1, :] = ni

        is_first_tile = i == nt - 1
        _acc_add(first, dd_ref, jnp.sum(dyy_v * h, axis=0, keepdims=True))
        for k in range(S5_BLOCKS):
            cb, sb = _s5_blk(k, S5_BCH), _s5_blk(k, S5_BST)
            xr_v, xi_v = xr_ref[:, sb], xi_ref[:, sb]
            xrp = jnp.where(is_first_tile, 0.0, xrp_ref[:, sb])
            xip = jnp.where(is_first_tile, 0.0, xip_ref[:, sb])
            xr_s = _roll_rows(jnp.concatenate([xrp, xr_v], axis=0), 1)[8:, :]
            xi_s = _roll_rows(jnp.concatenate([xip, xi_v], axis=0), 1)[8:, :]
            gr, gi = gr_ref[:, sb], gi_ref[:, sb]
            dar_k = jnp.sum(gr * xr_s + gi * xi_s, axis=0, keepdims=True)
            dai_k = jnp.sum(gi * xr_s - gr * xi_s, axis=0, keepdims=True)

            @pl.when(first)
            def _():
                dar_ref[:, sb] = dar_k
                dai_ref[:, sb] = dai_k

            @pl.when(jnp.logical_not(first))
            def _():
                dar_ref[:, sb] += dar_k
                dai_ref[:, sb] += dai_k

            grb, gib = gr.astype(bf16), gi.astype(bf16)
            uk, dk = ub[:, cb], dyb[:, cb]
            tn = lambda a_, b_: lax.dot_general(a_, b_, _DN["tn"], preferred_element_type=f32)
            vals = (tn(uk, grb), tn(uk, gib), tn(xr_v.astype(bf16), dk), -tn(xi_v.astype(bf16), dk))
            for ref, val in zip((dbbr_ref, dbbi_ref, dcr_ref, dci_ref), vals):
                @pl.when(first)
                def _():
                    ref[k] = val

                @pl.when(jnp.logical_not(first))
                def _():
                    ref[k] += val
            du_k = (lax.dot_general(grb, bbr_ref[k], _DN["nt"], preferred_element_type=f32)
                    + lax.dot_general(gib, bbi_ref[k], _DN["nt"], preferred_element_type=f32))
            du_ref[:, cb] = du_k + d_ref[:, cb] * dyy_v[:, cb]

    rev = lambda c: _tok_spec(tm, c, nt, True)
    prev = pl.BlockSpec((8, ns), lambda i: (jnp.maximum((nt - 1 - i) * per - 1, 0), 0))
    bd_in = pl.BlockSpec((S5_BLOCKS, S5_BCH, S5_BST), lambda i: (0, 0, 0))
    bd_out = pl.BlockSpec((S5_BLOCKS, S5_BST, S5_BCH), lambda i: (0, 0, 0))
    row_ns = jax.ShapeDtypeStruct((1, ns), f32)
    bd_in_s = jax.ShapeDtypeStruct((S5_BLOCKS, S5_BCH, S5_BST), f32)
    bd_out_s = jax.ShapeDtypeStruct((S5_BLOCKS, S5_BST, S5_BCH), f32)
    return pl.pallas_call(
        body, name="s5_scan_bwd", grid=(nt,),
        in_specs=[rev(D), rev(D), _row_spec(D), _row_spec(D), _row_spec(D), rev(ns), rev(ns), prev, prev,
                  _row_spec(ns), _row_spec(ns), bd_in, bd_in, bd_out, bd_out, _row_spec(D)],
        out_specs=[rev(D), _row_spec(ns), _row_spec(ns), bd_in, bd_in, bd_out, bd_out, _row_spec(D)],
        out_shape=[jax.ShapeDtypeStruct((t, D), f32), row_ns, row_ns, bd_in_s, bd_in_s, bd_out_s, bd_out_s,
                   jax.ShapeDtypeStruct((1, D), f32)],
        scratch_shapes=[pltpu.VMEM((tm, ns), f32), pltpu.VMEM((tm, ns), f32),
                        pltpu.VMEM((8, ns), f32), pltpu.VMEM((8, ns), f32)],
        compiler_params=_cp(1),
    )(dyy, x, g, scale, shift, xr, xi, xr, xi, lb_re, lb_im, bbr_bd, bbi_bd, cr_bd, ci_bd, dskip)


def _s5_gelu(yy):
    t = yy.shape[0]
    tm = _tok_tile(t)

    def body(y_ref, o_ref):
        o_ref[...] = jax.nn.gelu(y_ref[...]).astype(bf16)

    return pl.pallas_call(
        body, name="s5_gelu", grid=(t // tm,), in_specs=[_tok_spec(tm, D)], out_specs=_tok_spec(tm, D),
        out_shape=jax.ShapeDtypeStruct((t, D), bf16), compiler_params=_cp(1),
    )(yy)


def _s5_glu(gl, z):
    return gl * jax.nn.sigmoid(z)


def _s5_out(x, yy, z, gate):
    t = x.shape[0]
    tm = _tok_tile(t)

    def body(x_ref, y_ref, z_ref, gt_ref, o_ref):
        o_ref[...] = x_ref[...] + (1.0 + gt_ref[...]) * _s5_glu(jax.nn.gelu(y_ref[...]), z_ref[...])

    return pl.pallas_call(
        body, name="s5_out", grid=(t // tm,),
        in_specs=[_tok_spec(tm, D), _tok_spec(tm, D), _tok_spec(tm, D), _row_spec(D)],
        out_specs=_tok_spec(tm, D), out_shape=jax.ShapeDtypeStruct((t, D), f32), compiler_params=_cp(1),
    )(x, yy, z, gate)


def _s5_out_bwd(dxn, yy, z, gate):
    t = dxn.shape[0]
    tm = _tok_tile(t)

    def body(dxn_ref, y_ref, z_ref, gt_ref, dz_ref, dgl_ref, dgt_ref):
        dxn_v = dxn_ref[...]
        gl = jax.nn.gelu(y_ref[...])
        out, vjp = jax.vjp(_s5_glu, gl, z_ref[...])
        dgl, dz = vjp((1.0 + gt_ref[...]) * dxn_v)
        dz_ref[...] = dz.astype(bf16)
        dgl_ref[...] = dgl
        _acc_add(pl.program_id(0) == 0, dgt_ref, jnp.sum(dxn_v * out, axis=0, keepdims=True))

    return pl.pallas_call(
        body, name="s5_out_bwd", grid=(t // tm,),
        in_specs=[_tok_spec(tm, D), _tok_spec(tm, D), _tok_spec(tm, D), _row_spec(D)],
        out_specs=[_tok_spec(tm, D), _tok_spec(tm, D), _row_spec(D)],
        out_shape=[jax.ShapeDtypeStruct((t, D), bf16), jax.ShapeDtypeStruct((t, D), f32),
                   jax.ShapeDtypeStruct((1, D), f32)],
        compiler_params=_cp(1),
    )(dxn, yy, z, gate)


def _s5_gelu_bwd(yy, dgl_a, dgl_b):
    t = yy.shape[0]
    tm = _tok_tile(t)

    def body(y_ref, a_ref, b_ref, o_ref):
        _, vjp = jax.vjp(jax.nn.gelu, y_ref[...])
        o_ref[...] = vjp(a_ref[...] + b_ref[...])[0]

    return pl.pallas_call(
        body, name="s5_gelu_bwd", grid=(t // tm,),
        in_specs=[_tok_spec(tm, D), _tok_spec(tm, D), _tok_spec(tm, D)],
        out_specs=_tok_spec(tm, D), out_shape=jax.ShapeDtypeStruct((t, D), f32), compiler_params=_cp(1),
    )(yy, dgl_a, dgl_b)


def _s5_params(lam_re, lam_im, log_dt, b_re, b_im):
    bc = lambda a: a.reshape(S5_GROUPS, 1, -1)
    return (bc(lam_re), bc(lam_im), jnp.broadcast_to(log_dt.reshape(S5_GROUPS, 1, 1), (S5_GROUPS, 1, S5_STATE)),
            b_re.transpose(0, 2, 1), b_im.transpose(0, 2, 1))


def _s5_fwd(x, ln, raw, c_re, c_im, dskip, w_glu):
    gate = ln[3]
    lb_re, lb_im, bb_re, bb_im = _s5_prep(*raw)
    lbr, lbi = lb_re.reshape(1, S5_NSTATE), lb_im.reshape(1, S5_NSTATE)
    bds = (_s5_in_bd(bb_re).astype(bf16), _s5_in_bd(bb_im).astype(bf16),
           _s5_out_bd(c_re).astype(bf16), _s5_out_bd(c_im).astype(bf16))
    yy, xr, xi = _s5_scan_fwd(x, ln, lbr, lbi, *bds, dskip)
    gl = _s5_gelu(yy)
    z = _mm(gl, w_glu, "nn", f32, "s5_glu_mm")
    return _s5_out(x, yy, z, gate), (x, lbr, lbi, bds, yy, xr, xi, gl, z)


def _s5_bwd(dxn, saved, ln, raw, dskip, w_glu):
    x, lbr, lbi, bds, yy, xr, xi, gl, z = saved
    g, shift, scale, gate = ln
    dz, dgl_a, dgate = _s5_out_bwd(dxn, yy, z, gate)
    dgl_b = _mm(dz, w_glu, "nt", f32, "s5_dgl")
    dw_glu = _mm(gl, dz, "tn", f32, "s5_dwglu")
    dyy = _s5_gelu_bwd(yy, dgl_a, dgl_b)
    du, dar, dai, dbbr_bd, dbbi_bd, dcr_bd, dci_bd, dd = _s5_scan_bwd(dyy, x, ln, xr, xi, lbr, lbi, *bds, dskip)
    shp = (S5_GROUPS, 1, S5_STATE)
    d_lam_re, d_lam_im, d_dt, d_b_re, d_b_im = _s5_prep_bwd(
        *raw, dar.reshape(shp), dai.reshape(shp), _s5_in_bd_diag(dbbr_bd), _s5_in_bd_diag(dbbi_bd))
    dx, dg, dscale, dshift = _ln_bwd(x, g, scale, shift, du, dxn)
    grads = dict(
        s5_lam_re=d_lam_re.reshape(1, S5_GROUPS, S5_STATE), s5_lam_im=d_lam_im.reshape(1, S5_GROUPS, S5_STATE),
        s5_log_dt=d_dt[:, 0, 0].reshape(1, S5_GROUPS),
        s5_b_re=d_b_re.transpose(0, 2, 1)[None], s5_b_im=d_b_im.transpose(0, 2, 1)[None],
        s5_c_re=_s5_out_bd_diag(dcr_bd)[None], s5_c_im=_s5_out_bd_diag(dci_bd)[None],
        s5_d=dd, s5_w_glu=dw_glu)
    return dx, (dg, dshift, dscale, dgate), grads


_MESH = pl.DeviceIdType.MESH
_ANY = pl.BlockSpec(memory_space=pl.ANY)


def _me():
    return lax.axis_index("x"), lax.axis_index("y"), lax.axis_index("c")


def _dev_index(x, y, c):
    return 4 * x + 2 * y + c


def _all_gather(vs, name):
    n = len(vs)

    def body(*refs):
        v_refs, out_refs = refs[:n], refs[n:2 * n]
        send_sems, recv_sems, local_sems = refs[2 * n:]
        x, y, cc = _me()
        me, sibling = (x, y, cc), (x, y, 1 - cc)
        chips = [(1 - x, y), (x, 1 - y), (1 - x, 1 - y)]
        sends, local = [], []

        def copy(a, k, block, to, src=None):
            rows = out_refs[a].at[_dev_index(*block)]
            return pltpu.make_async_remote_copy(
                src_ref=rows if src is None else src, dst_ref=rows,
                send_sem=send_sems.at[7 * a + k], recv_sem=recv_sems.at[7 * a + k], device_id=to, device_id_type=_MESH)

        for a in range(n):
            mine = pltpu.make_async_copy(v_refs[a], out_refs[a].at[_dev_index(*me)], local_sems.at[a])
            mine.start()
            local.append(mine)
            first = [copy(a, 0, me, sibling, src=v_refs[a])]
            first += [copy(a, 1 + j, me, (*chip, cc), src=v_refs[a]) for j, chip in enumerate(chips)]
            for cp in first:
                cp.start()
            sends += first
        for a in range(n):
            for j, chip in enumerate(chips):
                copy(a, 1 + j, (*chip, cc), me).wait_recv()
                passed = copy(a, 4 + j, (*chip, cc), sibling)
                passed.start()
                sends.append(passed)
        for a in range(n):
            copy(a, 0, sibling, me).wait_recv()
            for j, chip in enumerate(chips):
                copy(a, 4 + j, (*chip, 1 - cc), me).wait_recv()
        for cp in sends:
            cp.wait_send()
        for cp in local:
            cp.wait()

    return pl.pallas_call(
        body, name=name, out_shape=[jax.ShapeDtypeStruct((N_DEV,) + v.shape, v.dtype) for v in vs],
        in_specs=[_ANY] * n, out_specs=[_ANY] * n,
        scratch_shapes=[pltpu.SemaphoreType.DMA((7 * n,)), pltpu.SemaphoreType.DMA((7 * n,)),
                        pltpu.SemaphoreType.DMA((n,))],
    )(*vs)


def _exchange_pair(vs, name):
    n = len(vs)

    def body(*refs):
        v_refs, out_refs = refs[:n], refs[n:2 * n]
        send_sems, recv_sems = refs[2 * n:]
        x, y, cc = _me()
        sibling = (x, y, 1 - cc)
        copies = []
        for a in range(n):
            for k in range(4):
                cp = pltpu.make_async_remote_copy(
                    src_ref=v_refs[a].at[2 * k + (1 - cc)], dst_ref=out_refs[a].at[k],
                    send_sem=send_sems.at[4 * a + k], recv_sem=recv_sems.at[4 * a + k],
                    device_id=sibling, device_id_type=_MESH)
                cp.start()
                copies.append(cp)
        for cp in copies:
            cp.wait_recv()
        for cp in copies:
            cp.wait_send()

    return pl.pallas_call(
        body, name=name, out_shape=[jax.ShapeDtypeStruct((4,) + v.shape[1:], v.dtype) for v in vs],
        in_specs=[_ANY] * n, out_specs=[_ANY] * n,
        scratch_shapes=[pltpu.SemaphoreType.DMA((4 * n,)), pltpu.SemaphoreType.DMA((4 * n,))],
    )(*vs)


def _pair_sum(v, got):
    _, r, c = v.shape
    tr = _pick(r, (512, 352, 256, 128))
    core = lax.axis_index("c").astype(jnp.int32).reshape(1)

    def body(c_ref, v_ref, g_ref, o_ref):
        o_ref[...] = (v_ref[...] + g_ref[...]).astype(bf16)

    return pl.pallas_call(
        body, name="pair_sum",
        grid_spec=pltpu.PrefetchScalarGridSpec(
            num_scalar_prefetch=1, grid=(4, r // tr),
            in_specs=[pl.BlockSpec((1, tr, c), lambda k, i, c_ref: (2 * k + c_ref[0], i, 0)),
                      pl.BlockSpec((1, tr, c), lambda k, i, c_ref: (k, i, 0))],
            out_specs=pl.BlockSpec((1, tr, c), lambda k, i, c_ref: (k, i, 0))),
        out_shape=jax.ShapeDtypeStruct((4, r, c), bf16), compiler_params=_cp(2),
    )(core, v, got)


def _exchange_chips(vs, name):
    n = len(vs)

    def body(*refs):
        v_refs, out_refs = refs[:n], refs[n:2 * n]
        send_sems, recv_sems, local_sems = refs[2 * n:]
        x, y, cc = _me()
        mine = 2 * x + y
        peers = []
        for mask in (1, 2, 3):
            px = 1 - x if mask & 2 else x
            py = 1 - y if mask & 1 else y
            peers.append((mask - 1, (px, py, cc), 2 * px + py))
        local, sends = [], []
        for a in range(n):
            own = pltpu.make_async_copy(v_refs[a].at[mine], out_refs[a].at[mine], local_sems.at[a])
            own.start()
            local.append(own)
            for k, peer, pchip in peers:
                cp = pltpu.make_async_remote_copy(
                    src_ref=v_refs[a].at[pchip], dst_ref=out_refs[a].at[mine],
                    send_sem=send_sems.at[3 * a + k], recv_sem=recv_sems.at[3 * a + k],
                    device_id=peer, device_id_type=_MESH)
                cp.start()
                sends.append(cp)
        for a in range(n):
            for k, peer, pchip in peers:
                pltpu.make_async_remote_copy(
                    src_ref=v_refs[a].at[pchip], dst_ref=out_refs[a].at[pchip],
                    send_sem=send_sems.at[3 * a + k], recv_sem=recv_sems.at[3 * a + k],
                    device_id=peer, device_id_type=_MESH).wait_recv()
        for cp in sends:
            cp.wait_send()
        for cp in local:
            cp.wait()

    return pl.pallas_call(
        body, name=name, out_shape=[jax.ShapeDtypeStruct(v.shape, v.dtype) for v in vs],
        in_specs=[_ANY] * n, out_specs=[_ANY] * n,
        scratch_shapes=[pltpu.SemaphoreType.DMA((3 * n,)), pltpu.SemaphoreType.DMA((3 * n,)),
                        pltpu.SemaphoreType.DMA((n,))],
    )(*vs)


def _ada_mod(c_all, ada_w):
    cols = ada_w.shape[2]

    def body(c_ref, w_ref, o_ref):
        cond = jax.nn.silu(c_ref[...]).astype(bf16)
        o_ref[0] = jnp.dot(cond, w_ref[0].astype(bf16), preferred_element_type=f32)

    return pl.pallas_call(
        body, name="ada_mod", grid=(DEPTH,),
        in_specs=[pl.BlockSpec((16, D), lambda i: (0, 0)), pl.BlockSpec((1, D, cols), lambda i: (i, 0, 0))],
        out_specs=pl.BlockSpec((1, 16, cols), lambda i: (i, 0, 0)),
        out_shape=jax.ShapeDtypeStruct((DEPTH, 16, cols), f32), compiler_params=_cp(1),
    )(c_all, ada_w)


def _ada_grad(c_all, dmod):
    cols = dmod.shape[2]

    def body(c_ref, d_ref, o_ref):
        cond = jax.nn.silu(c_ref[...]).astype(bf16)
        o_ref[0] = lax.dot_general(cond, d_ref[0].astype(bf16), _DN["tn"], preferred_element_type=f32)

    return pl.pallas_call(
        body, name="ada_grad", grid=(DEPTH,),
        in_specs=[pl.BlockSpec((16, D), lambda i: (0, 0)), pl.BlockSpec((1, 16, cols), lambda i: (i, 0, 0))],
        out_specs=pl.BlockSpec((1, D, cols), lambda i: (i, 0, 0)),
        out_shape=jax.ShapeDtypeStruct((DEPTH, D, cols), f32), compiler_params=_cp(1),
    )(c_all, dmod)


def _row_tile(r):
    return _pick(r, (512, 352, 256, 128)) if r > 512 else r


def _sum_sources(v, name):
    n, r, c = v.shape
    tr = _row_tile(r)

    def body(v_ref, o_ref):
        acc = v_ref[0]
        for p in range(1, n):
            acc = acc + v_ref[p]
        o_ref[...] = acc.astype(f32)

    return pl.pallas_call(
        body, name=name, grid=(r // tr,),
        in_specs=[pl.BlockSpec((n, tr, c), lambda i: (0, i, 0))], out_specs=pl.BlockSpec((tr, c), lambda i: (i, 0)),
        out_shape=jax.ShapeDtypeStruct((r, c), f32), compiler_params=_cp(1),
    )(v)


def _adamw(parts, w, m, v, name):
    n, r, c = parts.shape
    tr = _row_tile(r)
    c1 = 1.0 - ADAM_B1 ** ADAM_STEP
    c2 = 1.0 - ADAM_B2 ** ADAM_STEP

    def body(p_ref, w_ref, m_ref, v_ref, g_ref, d_ref, mo_ref, vo_ref):
        g_v = p_ref[0].astype(f32)
        for p in range(1, n):
            g_v = g_v + p_ref[p].astype(f32)
        m_n = ADAM_B1 * m_ref[...] + (1.0 - ADAM_B1) * g_v
        v_n = ADAM_B2 * v_ref[...] + (1.0 - ADAM_B2) * (g_v * g_v)
        g_ref[...] = g_v
        d_ref[...] = -ADAM_LR * ((m_n / c1) / (jnp.sqrt(v_n / c2) + ADAM_EPS) + ADAM_WD * w_ref[...])
        mo_ref[...] = m_n
        vo_ref[...] = v_n

    spec = pl.BlockSpec((tr, c), lambda i: (i, 0))
    shp = jax.ShapeDtypeStruct((r, c), f32)
    return pl.pallas_call(
        body, name=name, grid=(r // tr,), in_specs=[pl.BlockSpec((n, tr, c), lambda i: (0, i, 0))] + [spec] * 3,
        out_specs=[spec] * 4, out_shape=[shp] * 4, compiler_params=_cp(1),
    )(parts, w, m, v)


def _two_d(shape):
    return (math.prod(shape[:-1]), shape[-1])


def _pack_rows(a):
    n = a.size
    rows = -(-n // (8 * PACK_C)) * 8
    return jnp.pad(a.reshape(-1), (0, rows * PACK_C - n)).reshape(rows, PACK_C)


def _pack(parts):
    return jnp.concatenate([_pack_rows(p) for p in parts], axis=0)


def _unpack(packed, shapes):
    lead = packed.shape[:-2]
    out, off = [], 0
    for s in shapes:
        n = math.prod(s)
        rows = -(-n // (8 * PACK_C)) * 8
        part = packed[..., off:off + rows, :].reshape(lead + (rows * PACK_C,))
        out.append(part[..., :n].reshape(lead + tuple(s)))
        off += rows
    return out


def _unshard(g8, axis):
    local = g8.shape[1:]
    moved = jnp.moveaxis(g8, 0, axis)
    return moved.reshape(local[:axis] + (N_DEV * local[axis],) + local[axis + 1:])


def _shard8(full, axis):
    s = full.shape
    split = full.reshape(s[:axis] + (N_DEV, s[axis] // N_DEV) + s[axis + 1:])
    return jnp.moveaxis(split, axis, 0)


_BIG = dict(ffn_w_in=3, ffn_w_out=2, pool_w=2, fox_w_in=2, fox_w_o=1, s5_w_glu=1, conv_w_in=2, conv_w_out=1)
_SMALL_SHARDED = dict(norm_g=2, s5_d=1, conv_w=3)
_REPLICATED = ("ada_b", "pool_scale", "fox_b_f", "fox_q_gain", "fox_k_gain", "s5_lam_re", "s5_lam_im", "s5_log_dt",
               "s5_b_re", "s5_b_im", "s5_c_re", "s5_c_im")
_WEIGHTS = ("ada_w", "ada_b", "norm_g", "ffn_w_in", "ffn_w_out", "pool_w", "pool_scale", "fox_w_in", "fox_b_f",
            "fox_q_gain", "fox_k_gain", "fox_w_o", "s5_lam_re", "s5_lam_im", "s5_log_dt", "s5_b_re", "s5_b_im",
            "s5_c_re", "s5_c_im", "s5_d", "s5_w_glu", "conv_w_in", "conv_w", "conv_w_out")


def _step(x, c, target, w, m, v):
    t = x.shape[1]
    xi_, yi_, ci_ = _me()
    me = _dev_index(xi_, yi_, ci_)

    sm_shapes = [w[n].shape for n in _SMALL_SHARDED]
    small_all = _all_gather([_pack([c] + [w[n] for n in _SMALL_SHARDED])], "gather_small")[0]
    gathered = _unpack(small_all, [c.shape] + sm_shapes)
    c_all = gathered[0][:, 0, :]
    full = {n: _unshard(p, ax) for (n, ax), p in zip(_SMALL_SHARDED.items(), gathered[1:])}

    big_all = _all_gather([w[n].astype(bf16).reshape(_two_d(w[n].shape)) for n in _BIG], "gather_weights")
    gw = dict(zip(_BIG, big_all))
    ffn_w_in = gw["ffn_w_in"].reshape(N_DEV, 2 * DEPTH, D, FFN_HS)
    ffn_w_out = gw["ffn_w_out"].reshape(N_DEV, 2 * DEPTH, D_FF // N_DEV, D)
    pool_w = gw["pool_w"].reshape(N_DEV, 4, POOL_GROUP // N_DEV, POOL_GROUP).transpose(1, 0, 2, 3)
    pool_w = pool_w.reshape(4, POOL_GROUP, POOL_GROUP)
    fox_w_in = jnp.pad(gw["fox_w_in"].transpose(1, 0, 2).reshape(D, FOX_PROJ), ((0, 0), (0, FOX_PROJ_PAD - FOX_PROJ)))
    fox_w_o, s5_w_glu, conv_w_out = (gw[n].reshape(D, D) for n in ("fox_w_o", "s5_w_glu", "conv_w_out"))
    conv_w_in = gw["conv_w_in"]

    def ffn_w(i, f):
        return ffn_w_in, ffn_w_out, 2 * i + f

    c16 = jnp.pad(c_all, ((0, 8), (0, 0)))
    cols = w["ada_w"].shape[2]
    mod_sh = _ada_mod(c16, w["ada_w"])
    mod_all = _all_gather([mod_sh.reshape(DEPTH * 16, cols)], "gather_mod")[0].reshape(N_DEV, DEPTH, 16, cols)
    mod_mine = lax.dynamic_index_in_dim(mod_all, me, axis=2, keepdims=False)
    mod = (mod_mine.transpose(1, 0, 2).reshape(DEPTH, N_DEV * cols) + w["ada_b"]).reshape(DEPTH, 3, 3, D)

    norm_g = full["norm_g"]

    def ln_of(i, sub):
        return (norm_g[i, sub][None], mod[i, sub, 0][None], mod[i, sub, 1][None], mod[i, sub, 2][None])

    fox_b_f = jnp.pad(w["fox_b_f"], ((0, 0), (0, 128 - FOX_HEADS)))
    s5_raw = _s5_params(w["s5_lam_re"][0], w["s5_lam_im"][0], w["s5_log_dt"][0], w["s5_b_re"][0], w["s5_b_im"][0])
    s5_c_re, s5_c_im = w["s5_c_re"][0], w["s5_c_im"][0]
    conv_w = jnp.pad(full["conv_w"][0, :, 0, :], ((0, 5), (0, 0)))

    xs = x[0]
    saved = []
    for i in range(DEPTH):
        xs, s0 = _ffn_fwd(xs, ln_of(i, 0), *ffn_w(i, 0), 0.5)
        if i == 0:
            s1 = xs
            xs = _pool_fwd(xs, ln_of(i, 1), pool_w, w["pool_scale"])
        elif i == 1:
            xs, s1 = _fox_fwd(xs, ln_of(i, 1), fox_w_in, fox_b_f, w["fox_q_gain"], w["fox_k_gain"], fox_w_o)
        elif i == 2:
            xs, s1 = _s5_fwd(xs, ln_of(i, 1), s5_raw, s5_c_re, s5_c_im, full["s5_d"], s5_w_glu)
        else:
            xs, s1 = _convmix_fwd(xs, ln_of(i, 1), conv_w_in, conv_w, conv_w_out)
        xs, s2 = _ffn_fwd(xs, ln_of(i, 2), *ffn_w(i, 1), 0.5)
        saved.append((s0, s1, s2))
    dx, lpart = _loss_head(xs, target[0])

    grads = {}
    dmod = [[None] * 3 for _ in range(DEPTH)]
    dnorm = [[None] * 3 for _ in range(DEPTH)]
    dffn_in = lax.empty(ffn_w_in.shape, f32)
    dffn_out = lax.empty(ffn_w_out.shape, f32)

    def put_ln(i, sub, dln):
        dg, dshift, dscale, dgate = dln
        dnorm[i][sub] = dg
        dmod[i][sub] = jnp.concatenate([dshift, dscale, dgate], axis=0)

    for i in reversed(range(DEPTH)):
        s0, s1, s2 = saved[i]
        dx, dln, dffn_in, dffn_out = _ffn_bwd(dx, s2, ln_of(i, 2), *ffn_w(i, 1), 0.5, dffn_in, dffn_out)
        put_ln(i, 2, dln)
        if i == 0:
            dx, dln, dpw, dps = _pool_bwd(dx, s1, ln_of(i, 1), pool_w, w["pool_scale"])
            dpw = dpw.reshape(4, N_DEV, POOL_GROUP // N_DEV, POOL_GROUP).transpose(1, 0, 2, 3)
            grads.update(pool_w=dpw.reshape(N_DEV, 4 * POOL_GROUP // N_DEV, POOL_GROUP), pool_scale=dps)
        elif i == 1:
            dx, dln, dwi, dbf, dqg, dkg, dwo = _fox_bwd(dx, s1, ln_of(i, 1), fox_w_in, fox_b_f, fox_w_o)
            dwi = dwi[:, :FOX_PROJ].reshape(D, N_DEV, FOX_PROJ // N_DEV).transpose(1, 0, 2)
            grads.update(fox_w_in=dwi, fox_b_f=dbf[:, :FOX_HEADS], fox_q_gain=dqg, fox_k_gain=dkg,
                         fox_w_o=dwo.reshape(N_DEV, D // N_DEV, D))
        elif i == 2:
            dx, dln, gs5 = _s5_bwd(dx, s1, ln_of(i, 1), s5_raw, full["s5_d"], s5_w_glu)
            gs5["s5_w_glu"] = gs5["s5_w_glu"].reshape(N_DEV, D // N_DEV, D)
            grads.update(gs5)
        else:
            dx, dln, dwi, dcw, dwo = _convmix_bwd(dx, s1, ln_of(i, 1), conv_w_in, conv_w, conv_w_out)
            grads.update(conv_w_in=dwi, conv_w=dcw[None, :, None, :], conv_w_out=dwo.reshape(N_DEV, D // N_DEV, D))
        put_ln(i, 1, dln)
        dx, dln, dffn_in, dffn_out = _ffn_bwd(dx, s0, ln_of(i, 0), *ffn_w(i, 0), 0.5, dffn_in, dffn_out)
        put_ln(i, 0, dln)
    grads["ffn_w_in"] = dffn_in.reshape(N_DEV, 2 * DEPTH * D, FFN_HS)
    grads["ffn_w_out"] = dffn_out.reshape(N_DEV, 2 * DEPTH * D_FF // N_DEV, D)
    grads["norm_g"] = jnp.stack([jnp.concatenate(r, axis=0) for r in dnorm])
    dmod_mine = jnp.stack([jnp.stack(r) for r in dmod]).reshape(DEPTH, 9 * D)

    small_names = list(_REPLICATED[1:]) + list(_SMALL_SHARDED)
    small_parts = [dmod_mine] + [grads[n] for n in small_names] + [lpart[:, 0:1]]
    small_g = _all_gather([_pack(small_parts)], "gather_grads")[0]
    small_sum = _sum_sources(small_g, "sum_small")
    summed = dict(zip(["ada_b"] + small_names + ["loss"], _unpack(small_sum, [p.shape for p in small_parts])))
    loss = summed.pop("loss")[0, 0]
    for n, ax in _SMALL_SHARDED.items():
        local = w[n].shape[ax]
        summed[n] = lax.dynamic_slice_in_dim(summed[n], me * local, local, axis=ax)

    dmod_all = small_g[:, :DEPTH * 9].reshape(N_DEV, DEPTH, 9 * D)
    dmod_cols = lax.dynamic_slice_in_dim(dmod_all, me * cols, cols, axis=2)
    ada_g = _ada_grad(c16, jnp.pad(dmod_cols.transpose(1, 0, 2), ((0, 0), (0, 8), (0, 0))))

    big_parts = [grads[n] for n in _BIG]
    from_sibling = _exchange_pair(big_parts, "exchange_pair")
    chip_sums = [_pair_sum(p, s) for p, s in zip(big_parts, from_sibling)]
    big_landed = dict(zip(_BIG, _exchange_chips(chip_sums, "exchange_chips")))

    grad, delta, new_m, new_v = {}, {}, {}, {}
    big_landed["ada_w"] = ada_g[None]
    for n, parts in big_landed.items():
        view = _two_d(w[n].shape)
        outs = _adamw(parts.reshape((parts.shape[0],) + view), w[n].reshape(view), m[n].reshape(view),
                      v[n].reshape(view), "adamw_" + n)
        grad[n], delta[n], new_m[n], new_v[n] = (a.reshape(w[n].shape) for a in outs)
    small = [n for n in _WEIGHTS if n not in big_landed]
    small_shapes = [w[n].shape for n in small]
    pk = lambda d: _pack([d[n] for n in small])
    outs = _adamw(pk(summed)[None], pk(w), pk(m), pk(v), "adamw_small")
    for dst, packed in zip((grad, delta, new_m, new_v), outs):
        dst.update(zip(small, _unpack(packed, small_shapes)))
    return (loss, dx[None], *[grad[n] for n in _WEIGHTS], *[delta[n] for n in _WEIGHTS],
            *[new_m[n] for n in _WEIGHTS], *[new_v[n] for n in _WEIGHTS])


def kernel(x, c, ada_w, ada_b, norm_g, ffn_w_in, ffn_w_out, pool_w, pool_scale, fox_w_in, fox_b_f, fox_q_gain, fox_k_gain, fox_w_o, s5_lam_re, s5_lam_im, s5_log_dt, s5_b_re, s5_b_im, s5_c_re, s5_c_im, s5_d, s5_w_glu, conv_w_in, conv_w, conv_w_out, loss_target, m_ada_w, m_ada_b, m_norm_g, m_ffn_w_in, m_ffn_w_out, m_pool_w, m_pool_scale, m_fox_w_in, m_fox_b_f, m_fox_q_gain, m_fox_k_gain, m_fox_w_o, m_s5_lam_re, m_s5_lam_im, m_s5_log_dt, m_s5_b_re, m_s5_b_im, m_s5_c_re, m_s5_c_im, m_s5_d, m_s5_w_glu, m_conv_w_in, m_conv_w, m_conv_w_out, v_ada_w, v_ada_b, v_norm_g, v_ffn_w_in, v_ffn_w_out, v_pool_w, v_pool_scale, v_fox_w_in, v_fox_b_f, v_fox_q_gain, v_fox_k_gain, v_fox_w_o, v_s5_lam_re, v_s5_lam_im, v_s5_log_dt, v_s5_b_re, v_s5_b_im, v_s5_c_re, v_s5_c_im, v_s5_d, v_s5_w_glu, v_conv_w_in, v_conv_w, v_conv_w_out):
    ws = (ada_w, ada_b, norm_g, ffn_w_in, ffn_w_out, pool_w, pool_scale, fox_w_in, fox_b_f, fox_q_gain, fox_k_gain,
          fox_w_o, s5_lam_re, s5_lam_im, s5_log_dt, s5_b_re, s5_b_im, s5_c_re, s5_c_im, s5_d, s5_w_glu, conv_w_in,
          conv_w, conv_w_out)
    ms = (m_ada_w, m_ada_b, m_norm_g, m_ffn_w_in, m_ffn_w_out, m_pool_w, m_pool_scale, m_fox_w_in, m_fox_b_f,
          m_fox_q_gain, m_fox_k_gain, m_fox_w_o, m_s5_lam_re, m_s5_lam_im, m_s5_log_dt, m_s5_b_re, m_s5_b_im,
          m_s5_c_re, m_s5_c_im, m_s5_d, m_s5_w_glu, m_conv_w_in, m_conv_w, m_conv_w_out)
    vs = (v_ada_w, v_ada_b, v_norm_g, v_ffn_w_in, v_ffn_w_out, v_pool_w, v_pool_scale, v_fox_w_in, v_fox_b_f,
          v_fox_q_gain, v_fox_k_gain, v_fox_w_o, v_s5_lam_re, v_s5_lam_im, v_s5_log_dt, v_s5_b_re, v_s5_b_im,
          v_s5_c_re, v_s5_c_im, v_s5_d, v_s5_w_glu, v_conv_w_in, v_conv_w, v_conv_w_out)
    return _step(x, c, loss_target, dict(zip(_WEIGHTS, ws)), dict(zip(_WEIGHTS, ms)), dict(zip(_WEIGHTS, vs)))
```

```python
import math

import jax
import jax.numpy as jnp
from jax import lax
from jax.experimental import pallas as pl
from jax.experimental.pallas import tpu as pltpu

f32 = jnp.float32
bf16 = jnp.bfloat16

D = 1024
D_FF = 2816
FFN_HS = 2 * D_FF // 8
FFN_SLABS = 4
DEPTH = 4
NORM_EPS = 1e-6
N_DEV = 8
POOL_WINDOWS = (2, 4, 8, 16)
POOL_GROUP = 256
POOL_HALO = 16
FOX_HEADS = 16
FOX_HEAD_DIM = 64
FOX_PROJ = 3088
FOX_PROJ_PAD = 3200
S5_GROUPS = 64
S5_GROUP = 16
S5_STATE = 64
S5_NSTATE = S5_GROUPS * S5_STATE
S5_BLOCKS = 8
S5_BCH = 128
S5_BST = 512
CONV_HALO = 8
ADAM_LR = 0.001
ADAM_B1 = 0.9
ADAM_B2 = 0.999
ADAM_EPS = 1e-08
ADAM_WD = 0.01
ADAM_STEP = 10
VMEM_LIMIT = 56 * 1024 * 1024
PACK_C = 1024

_ARB = "arbitrary"


def _cp(n_axes):
    return pltpu.CompilerParams(dimension_semantics=(_ARB,) * n_axes, vmem_limit_bytes=VMEM_LIMIT)


def _pick(n, prefs):
    for c in prefs:
        if n % c == 0:
            return c
    return n


_DN = {"nn": (((1,), (0,)), ((), ())), "nt": (((1,), (1,)), ((), ())), "tn": (((0,), (0,)), ((), ()))}


def _mm(a, b, mode, out_dtype, name):
    if mode == "nn":
        (m, k), (_, n) = a.shape, b.shape
    elif mode == "nt":
        (m, k), (n, _) = a.shape, b.shape
    else:
        (k, m), (_, n) = a.shape, b.shape
    big = (1408, 1024, 640, 512, 384, 256, 128)
    tm = _pick(m, big) if mode == "tn" else _pick(m, (1024, 512, 256, 128))
    tn = _pick(n, big)
    if mode == "tn":
        tk = _pick(k, (512, 256, 128))
    else:
        tk = k if k <= 3200 else _pick(k, (2816, 2048, 1024, 512))
    nk = k // tk
    dn = _DN[mode]

    def body(a_ref, b_ref, o_ref, acc_ref):
        p = lax.dot_general(a_ref[...], b_ref[...], dn, preferred_element_type=f32)
        if nk == 1:
            o_ref[...] = p.astype(out_dtype)
        else:
            kk = pl.program_id(2)

            @pl.when(kk == 0)
            def _():
                acc_ref[...] = p

            @pl.when(kk > 0)
            def _():
                acc_ref[...] += p

            @pl.when(kk == nk - 1)
            def _():
                o_ref[...] = acc_ref[...].astype(out_dtype)

    if mode == "nn":
        a_spec = pl.BlockSpec((tm, tk), lambda i, j, kk: (i, kk))
        b_spec = pl.BlockSpec((tk, tn), lambda i, j, kk: (kk, j))
    elif mode == "nt":
        a_spec = pl.BlockSpec((tm, tk), lambda i, j, kk: (i, kk))
        b_spec = pl.BlockSpec((tn, tk), lambda i, j, kk: (j, kk))
    else:
        a_spec = pl.BlockSpec((tk, tm), lambda i, j, kk: (kk, i))
        b_spec = pl.BlockSpec((tk, tn), lambda i, j, kk: (kk, j))
    acc_shape = (tm, tn) if nk > 1 else (8, 128)
    return pl.pallas_call(
        body, name=name, grid=(m // tm, n // tn, nk),
        in_specs=[a_spec, b_spec], out_specs=pl.BlockSpec((tm, tn), lambda i, j, kk: (i, j)),
        out_shape=jax.ShapeDtypeStruct((m, n), out_dtype),
        scratch_shapes=[pltpu.VMEM(acc_shape, f32)],
        compiler_params=_cp(3),
    )(a, b)


def _mmx(a, b, mode, name, grid, a_spec, b_spec, o_spec, out_shape, into=None):
    nk = grid[2]
    dn = _DN[mode]
    out_dtype = out_shape.dtype
    a_blk = (math.prod(a_spec.block_shape[:-1]), a_spec.block_shape[-1])
    b_blk = (math.prod(b_spec.block_shape[:-1]), b_spec.block_shape[-1])
    o_blk = (math.prod(o_spec.block_shape[:-1]), o_spec.block_shape[-1])

    def body(a_ref, b_ref, *rest):
        o_ref, acc_ref = rest[-2:]
        p = lax.dot_general(a_ref[...].reshape(a_blk), b_ref[...].reshape(b_blk), dn, preferred_element_type=f32)
        if nk == 1:
            o_ref[...] = p.reshape(o_ref.shape).astype(out_dtype)
        else:
            kk = pl.program_id(2)

            @pl.when(kk == 0)
            def _():
                acc_ref[...] = p

            @pl.when(kk > 0)
            def _():
                acc_ref[...] += p

            @pl.when(kk == nk - 1)
            def _():
                o_ref[...] = acc_ref[...].reshape(o_ref.shape).astype(out_dtype)

    extra = {} if into is None else dict(input_output_aliases={2: 0})
    operands = (a, b) if into is None else (a, b, into)
    return pl.pallas_call(
        body, name=name, grid=grid, in_specs=[a_spec, b_spec] + ([] if into is None else [_ANY]),
        out_specs=o_spec, out_shape=out_shape,
        scratch_shapes=[pltpu.VMEM(o_blk if nk > 1 else (8, 128), f32)], compiler_params=_cp(3), **extra,
    )(*operands)


def _tok_tile(t):
    return min(t, 512)


def _tok_spec(tm, c, nt=None, reverse=False):
    if reverse:
        return pl.BlockSpec((tm, c), lambda i: (nt - 1 - i, 0))
    return pl.BlockSpec((tm, c), lambda i: (i, 0))


def _row_spec(c, rows=1):
    return pl.BlockSpec((rows, c), lambda i: (0, 0))


def _acc_add(first, ref, val):
    @pl.when(first)
    def _():
        ref[...] = val

    @pl.when(jnp.logical_not(first))
    def _():
        ref[...] += val


def _adaln(x, g, scale, shift):
    y = x * lax.rsqrt(jnp.mean(x * x, axis=-1, keepdims=True) + NORM_EPS)
    return (y * g) * (1.0 + scale) + shift


def _ln_fwd(x, g, scale, shift):
    t = x.shape[0]
    tm = _tok_tile(t)

    def body(x_ref, g_ref, sc_ref, sh_ref, h_ref):
        h_ref[...] = _adaln(x_ref[...], g_ref[...], sc_ref[...], sh_ref[...]).astype(bf16)

    return pl.pallas_call(
        body, name="ln_fwd", grid=(t // tm,),
        in_specs=[_tok_spec(tm, D), _row_spec(D), _row_spec(D), _row_spec(D)],
        out_specs=_tok_spec(tm, D), out_shape=jax.ShapeDtypeStruct((t, D), bf16),
        compiler_params=_cp(1),
    )(x, g, scale, shift)


def _ln_bwd(x, g, scale, shift, dh, dxn):
    t = x.shape[0]
    tm = _tok_tile(t)

    def body(x_ref, g_ref, sc_ref, sh_ref, dh_ref, dxn_ref, dx_ref, dg_ref, dsc_ref, dsh_ref):
        _, vjp = jax.vjp(_adaln, x_ref[...], g_ref[...], sc_ref[...], sh_ref[...])
        dx, dg, dsc, dsh = vjp(dh_ref[...])
        dx_ref[...] = dxn_ref[...] + dx
        first = pl.program_id(0) == 0
        _acc_add(first, dg_ref, dg)
        _acc_add(first, dsc_ref, dsc)
        _acc_add(first, dsh_ref, dsh)

    row = jax.ShapeDtypeStruct((1, D), f32)
    return pl.pallas_call(
        body, name="ln_bwd", grid=(t // tm,),
        in_specs=[_tok_spec(tm, D), _row_spec(D), _row_spec(D), _row_spec(D), _tok_spec(tm, D), _tok_spec(tm, D)],
        out_specs=[_tok_spec(tm, D), _row_spec(D), _row_spec(D), _row_spec(D)],
        out_shape=[jax.ShapeDtypeStruct((t, D), f32), row, row, row],
        compiler_params=_cp(1),
    )(x, g, scale, shift, dh, dxn)


def _swiglu(g, u):
    return jax.nn.silu(g) * u


def _ffn_tile(t):
    return min(t, 512)


def _ffn_specs(tm, idx, rows):
    once = pl.Buffered(1)
    return (pl.BlockSpec((2, FFN_SLABS, 1, D, FFN_HS), lambda i, q: (0, 0, idx, 0, 0), pipeline_mode=once),
            pl.BlockSpec((N_DEV, 1, rows, D), lambda i, q: (0, idx, 0, 0), pipeline_mode=once),
            pl.BlockSpec((2, 1, tm, FFN_HS), lambda i, q: (0, q, i, 0)),
            pl.BlockSpec((tm, D), lambda i, q: (i, 0)),
            pl.BlockSpec((1, D), lambda i, q: (0, 0)))


def _out_rows(wo_ref, q):
    return wo_ref[pl.ds(2 * q, 2), 0].reshape(FFN_HS, D)


def _ffn_up(x, h, gate, w_in, w_out, idx, coef):
    t = h.shape[0]
    tm = _ffn_tile(t)
    rows = w_out.shape[2]

    def body(h_ref, wi_ref, wo_ref, x_ref, gt_ref, gu_ref, a_ref, o_ref, xn_ref):
        q = pl.program_id(1)
        hv = h_ref[...]
        g = jnp.dot(hv, wi_ref[0, q, 0], preferred_element_type=f32)
        u = jnp.dot(hv, wi_ref[1, q, 0], preferred_element_type=f32)
        gu_ref[0, 0] = g.astype(bf16)
        gu_ref[1, 0] = u.astype(bf16)
        a = _swiglu(g, u).astype(bf16)
        a_ref[0] = a
        _acc_add(q == 0, o_ref, jnp.dot(a, _out_rows(wo_ref, q), preferred_element_type=f32))

        @pl.when(q == FFN_SLABS - 1)
        def _():
            xn_ref[...] = x_ref[...] + (coef * (1.0 + gt_ref[...])) * o_ref[...]

    wi, wo, slab, tok, row = _ffn_specs(tm, idx, rows)
    tok_f32 = jax.ShapeDtypeStruct((t, D), f32)
    return pl.pallas_call(
        body, name="ffn_up", grid=(t // tm, FFN_SLABS), in_specs=[tok, wi, wo, tok, row],
        out_specs=[slab, pl.BlockSpec((1, tm, FFN_HS), lambda i, q: (q, i, 0)), tok, tok],
        out_shape=[jax.ShapeDtypeStruct((2, FFN_SLABS, t, FFN_HS), bf16),
                   jax.ShapeDtypeStruct((FFN_SLABS, t, FFN_HS), bf16), tok_f32, tok_f32],
        compiler_params=_cp(2),
    )(h, w_in, w_out, x, gate)


def _ffn_dgu(do, w_out, idx, gu):
    t = do.shape[0]
    tm = _mm_tile(t)

    def body(do_ref, wo_ref, gu_ref, o_ref):
        da = lax.dot_general(do_ref[...], _out_rows(wo_ref, pl.program_id(1)), _DN["nt"], preferred_element_type=f32)
        _, vjp = jax.vjp(_swiglu, gu_ref[0, 0].astype(f32), gu_ref[1, 0].astype(f32))
        dg, du = vjp(da)
        o_ref[0, 0] = dg.astype(bf16)
        o_ref[1, 0] = du.astype(bf16)

    _, wo, slab, tok, _ = _ffn_specs(tm, idx, w_out.shape[2])
    return pl.pallas_call(
        body, name="ffn_dgu", grid=(t // tm, FFN_SLABS), in_specs=[tok, wo, slab],
        out_specs=slab, out_shape=jax.ShapeDtypeStruct((2, FFN_SLABS, t, FFN_HS), bf16),
        compiler_params=_cp(2),
    )(do, w_out, gu)


def _ffn_dh(dgu, w_in, idx):
    t = dgu.shape[2]
    tm = _mm_tile(t)

    def body(d_ref, wi_ref, o_ref):
        q = pl.program_id(1)
        p = (lax.dot_general(d_ref[0, 0], wi_ref[0, q, 0], _DN["nt"], preferred_element_type=f32)
             + lax.dot_general(d_ref[1, 0], wi_ref[1, q, 0], _DN["nt"], preferred_element_type=f32))
        _acc_add(pl.program_id(1) == 0, o_ref, p)

    wi, _, slab, tok, _ = _ffn_specs(tm, idx, 8)
    return pl.pallas_call(
        body, name="ffn_dh", grid=(t // tm, FFN_SLABS), in_specs=[slab, wi], out_specs=tok,
        out_shape=jax.ShapeDtypeStruct((t, D), f32), compiler_params=_cp(2),
    )(dgu, w_in)


def _resid_fwd(x, o, gate, coef):
    t = x.shape[0]
    tm = _tok_tile(t)

    def body(x_ref, o_ref, gt_ref, y_ref):
        y_ref[...] = x_ref[...] + (coef * (1.0 + gt_ref[...])) * o_ref[...]

    return pl.pallas_call(
        body, name="resid_fwd", grid=(t // tm,),
        in_specs=[_tok_spec(tm, D), _tok_spec(tm, D), _row_spec(D)],
        out_specs=_tok_spec(tm, D), out_shape=jax.ShapeDtypeStruct((t, D), f32),
        compiler_params=_cp(1),
    )(x, o, gate)


def _resid_bwd(dxn, o, gate, coef):
    t = dxn.shape[0]
    tm = _tok_tile(t)

    def body(dxn_ref, o_ref, gt_ref, do_ref, dgt_ref):
        dxn_v = dxn_ref[...]
        do_ref[...] = ((coef * (1.0 + gt_ref[...])) * dxn_v).astype(bf16)
        _acc_add(pl.program_id(0) == 0, dgt_ref, coef * jnp.sum(dxn_v * o_ref[...], axis=0, keepdims=True))

    return pl.pallas_call(
        body, name="resid_bwd", grid=(t // tm,),
        in_specs=[_tok_spec(tm, D), _tok_spec(tm, D), _row_spec(D)],
        out_specs=[_tok_spec(tm, D), _row_spec(D)],
        out_shape=[jax.ShapeDtypeStruct((t, D), bf16), jax.ShapeDtypeStruct((1, D), f32)],
        compiler_params=_cp(1),
    )(dxn, o, gate)


def _loss_head(y, target):
    t = y.shape[0]
    tm = _tok_tile(t)

    def body(y_ref, t_ref, dy_ref, l_ref):
        err = y_ref[...] - t_ref[...]
        dy_ref[...] = err * (1.0 / D)
        part = jnp.sum(jnp.sum(err * err, axis=0, keepdims=True), axis=1, keepdims=True) * (0.5 / D)
        _acc_add(pl.program_id(0) == 0, l_ref, jnp.broadcast_to(part, (1, 128)))

    return pl.pallas_call(
        body, name="loss_head", grid=(t // tm,),
        in_specs=[_tok_spec(tm, D), _tok_spec(tm, D)],
        out_specs=[_tok_spec(tm, D), _row_spec(128)],
        out_shape=[jax.ShapeDtypeStruct((t, D), f32), jax.ShapeDtypeStruct((1, 128), f32)],
        compiler_params=_cp(1),
    )(y, target)


def _mm_tile(t):
    return min(t, 1024)


def _ffn_fwd(x, ln, w_in, w_out, idx, coef):
    g, shift, scale, gate = ln
    h = _ln_fwd(x, g, scale, shift)
    gu, a, o, xn = _ffn_up(x, h, gate, w_in.reshape((2, FFN_SLABS) + w_in.shape[1:]), w_out, idx, coef)
    return xn, (x, h, gu, a, o)


def _ffn_bwd(dxn, saved, ln, w_in, w_out, idx, coef, dw_in, dw_out):
    x, h, gu, a, o = saved
    g, shift, scale, gate = ln
    t = x.shape[0]
    tm = _mm_tile(t)
    tk = _pick(t, (2048, 1024, 512, 256, 128))
    rows = w_out.shape[2]
    sds = jax.ShapeDtypeStruct
    do, dgate = _resid_bwd(dxn, o, gate, coef)
    dgu = _ffn_dgu(do, w_out, idx, gu)
    dh = _ffn_dh(dgu, w_in.reshape((2, FFN_SLABS) + w_in.shape[1:]), idx)
    dgu = dgu.reshape(N_DEV, t, FFN_HS)
    dw_out = _mmx(a, do, "tn", "ffn_dwout", (FFN_SLABS, 1, t // tk),
                  pl.BlockSpec((1, tk, FFN_HS), lambda i, j, k: (i, k, 0)),
                  pl.BlockSpec((tk, D), lambda i, j, k: (k, 0)),
                  pl.BlockSpec((2, 1, rows, D), lambda i, j, k: (i, idx, 0, 0)),
                  sds(dw_out.shape, f32), into=dw_out)
    dw_in = _mmx(h, dgu, "tn", "ffn_dwin", (N_DEV, 1, t // tk),
                 pl.BlockSpec((tk, D), lambda i, j, k: (k, 0)),
                 pl.BlockSpec((1, tk, FFN_HS), lambda i, j, k: (i, k, 0)),
                 pl.BlockSpec((1, 1, D, FFN_HS), lambda i, j, k: (i, idx, 0, 0)),
                 sds(dw_in.shape, f32), into=dw_in)
    dx, dg, dscale, dshift = _ln_bwd(x, g, scale, shift, dh, dxn)
    return dx, (dg, dshift, dscale, dgate), dw_in, dw_out


def _roll_rows(a, k):
    n = a.shape[0]
    return pltpu.roll(a, k % n, 0)


def _pool_windows(hx, first_row, reverse):
    outs = []
    for gi, w in enumerate(POOL_WINDOWS):
        acc = hx[:, gi * POOL_GROUP:(gi + 1) * POOL_GROUP]
        k = 1
        while k < w:
            acc = acc + _roll_rows(acc, -k if reverse else k)
            k *= 2
        outs.append(acc)
    return outs


def _pool_cnt(t_idx, w):
    return jnp.minimum(t_idx + 1, w).astype(f32)


def _pool_pooled(x_ref, xp_ref, g, scale, shift, i, tm):
    h = _adaln(x_ref[...], g, scale, shift)
    hp = _adaln(xp_ref[...], g, scale, shift)
    hp = jnp.where(i == 0, 0.0, hp)
    hx = jnp.concatenate([hp, h], axis=0)
    sums = _pool_windows(hx, 0, False)
    t_idx = i * tm + lax.broadcasted_iota(jnp.int32, (tm, 1), 0)
    pooled = []
    for gi, w in enumerate(POOL_WINDOWS):
        s = sums[gi][POOL_HALO:, :]
        pooled.append(s / _pool_cnt(t_idx, w) - h[:, gi * POOL_GROUP:(gi + 1) * POOL_GROUP])
    return h, pooled


def _pool_specs(t, tm):
    per = tm // POOL_HALO
    prev = pl.BlockSpec((POOL_HALO, D), lambda i: (jnp.maximum(i * per - 1, 0), 0))
    return [_tok_spec(tm, D), prev, _row_spec(D), _row_spec(D), _row_spec(D),
            pl.BlockSpec((4, POOL_GROUP, POOL_GROUP), lambda i: (0, 0, 0)), _row_spec(D), _row_spec(D)]


def _pool_fwd(x, ln, w, pscale):
    g, shift, scale, gate = ln
    t = x.shape[0]
    tm = _tok_tile(t)

    def body(x_ref, xp_ref, g_ref, sc_ref, sh_ref, w_ref, ps_ref, gt_ref, y_ref):
        i = pl.program_id(0)
        _, pooled = _pool_pooled(x_ref, xp_ref, g_ref[...], sc_ref[...], sh_ref[...], i, tm)
        mixed = [jnp.dot(pooled[gi].astype(bf16), w_ref[gi], preferred_element_type=f32) for gi in range(4)]
        y = jnp.concatenate(mixed, axis=1) * ps_ref[...]
        y_ref[...] = x_ref[...] + (1.0 + gt_ref[...]) * y

    return pl.pallas_call(
        body, name="pool_fwd", grid=(t // tm,), in_specs=_pool_specs(t, tm),
        out_specs=_tok_spec(tm, D), out_shape=jax.ShapeDtypeStruct((t, D), f32),
        compiler_params=_cp(1),
    )(x, x, g, scale, shift, w, pscale, gate)


def _pool_bwd(dxn, x, ln, w, pscale):
    g, shift, scale, gate = ln
    t = x.shape[0]
    tm = _tok_tile(t)
    nt = t // tm
    per = tm // POOL_HALO

    def body_a(x_ref, xp_ref, g_ref, sc_ref, sh_ref, w_ref, ps_ref, gt_ref, dxn_ref,
               dp_ref, dw_ref, dps_ref, dgt_ref):
        i = pl.program_id(0)
        first = i == 0
        _, pooled = _pool_pooled(x_ref, xp_ref, g_ref[...], sc_ref[...], sh_ref[...], i, tm)
        dxn_v = dxn_ref[...]
        dy = (1.0 + gt_ref[...]) * dxn_v
        dmixed = dy * ps_ref[...]
        mixed, dps = [], []
        for gi in range(4):
            sl = slice(gi * POOL_GROUP, (gi + 1) * POOL_GROUP)
            pb = pooled[gi].astype(bf16)
            dmb = dmixed[:, sl].astype(bf16)
            mixed.append(jnp.dot(pb, w_ref[gi], preferred_element_type=f32))
            dp_ref[:, sl] = lax.dot_general(dmb, w_ref[gi], _DN["nt"], preferred_element_type=f32)
            dwg = lax.dot_general(pb, dmb, _DN["tn"], preferred_element_type=f32)

            @pl.when(first)
            def _():
                dw_ref[gi] = dwg

            @pl.when(jnp.logical_not(first))
            def _():
                dw_ref[gi] += dwg
        mixed = jnp.concatenate(mixed, axis=1)
        _acc_add(first, dps_ref, jnp.sum(dy * mixed, axis=0, keepdims=True))
        _acc_add(first, dgt_ref, jnp.sum(dxn_v * (mixed * ps_ref[...]), axis=0, keepdims=True))

    row = jax.ShapeDtypeStruct((1, D), f32)
    dpooled, dw, dps, dgate = pl.pallas_call(
        body_a, name="pool_bwd_a", grid=(nt,), in_specs=_pool_specs(t, tm) + [_tok_spec(tm, D)],
        out_specs=[_tok_spec(tm, D), pl.BlockSpec((4, POOL_GROUP, POOL_GROUP), lambda i: (0, 0, 0)),
                   _row_spec(D), _row_spec(D)],
        out_shape=[jax.ShapeDtypeStruct((t, D), f32), jax.ShapeDtypeStruct((4, POOL_GROUP, POOL_GROUP), f32), row, row],
        compiler_params=_cp(1),
    )(x, x, g, scale, shift, w, pscale, gate, dxn)

    def body_b(dp_ref, dpn_ref, x_ref, g_ref, sc_ref, sh_ref, dxn_ref, dx_ref, dg_ref, dsc_ref, dsh_ref):
        i = pl.program_id(0)
        dp = dp_ref[...]
        dpn = jnp.where(i == nt - 1, 0.0, dpn_ref[...])
        t_idx = i * tm + lax.broadcasted_iota(jnp.int32, (tm + POOL_HALO, 1), 0)
        ex = jnp.concatenate([dp, dpn], axis=0)
        parts = []
        for gi, w_ in enumerate(POOL_WINDOWS):
            parts.append(ex[:, gi * POOL_GROUP:(gi + 1) * POOL_GROUP] / _pool_cnt(t_idx, w_))
        sums = _pool_windows(jnp.concatenate(parts, axis=1), 0, True)
        dh = jnp.concatenate([s[:tm, :] for s in sums], axis=1) - dp
        _, vjp = jax.vjp(_adaln, x_ref[...], g_ref[...], sc_ref[...], sh_ref[...])
        dx, dg, dsc, dsh = vjp(dh)
        dx_ref[...] = dxn_ref[...] + dx
        first = i == 0
        _acc_add(first, dg_ref, dg)
        _acc_add(first, dsc_ref, dsc)
        _acc_add(first, dsh_ref, dsh)

    nxt = pl.BlockSpec((POOL_HALO, D), lambda i: (jnp.minimum((i + 1) * per, t // POOL_HALO - 1), 0))
    dx, dg, dscale, dshift = pl.pallas_call(
        body_b, name="pool_bwd_b", grid=(nt,),
        in_specs=[_tok_spec(tm, D), nxt, _tok_spec(tm, D), _row_spec(D), _row_spec(D), _row_spec(D), _tok_spec(tm, D)],
        out_specs=[_tok_spec(tm, D), _row_spec(D), _row_spec(D), _row_spec(D)],
        out_shape=[jax.ShapeDtypeStruct((t, D), f32), row, row, row],
        compiler_params=_cp(1),
    )(dpooled, dpooled, x, g, scale, shift, dxn)
    return dx, (dg, dshift, dscale, dgate), dw, dps


def _conv_taps(czx, cw):
    return cw[0:1, :] * _roll_rows(czx, 2) + cw[1:2, :] * _roll_rows(czx, 1) + cw[2:3, :] * czx


def _conv_fwd(p, cw):
    t = p.shape[0]
    tm = _tok_tile(t)
    per = tm // CONV_HALO

    def body(p_ref, pp_ref, cw_ref, q_ref):
        i = pl.program_id(0)
        cz = p_ref[:, D:2 * D] * p_ref[:, 2 * D:3 * D]
        czp = jnp.where(i == 0, 0.0, pp_ref[:, D:2 * D] * pp_ref[:, 2 * D:3 * D])
        conv = _conv_taps(jnp.concatenate([czp, cz], axis=0), cw_ref[...])[CONV_HALO:, :]
        q_ref[...] = (p_ref[:, 0:D] * conv).astype(bf16)

    prev = pl.BlockSpec((CONV_HALO, 3 * D), lambda i: (jnp.maximum(i * per - 1, 0), 0))
    return pl.pallas_call(
        body, name="conv_fwd", grid=(t // tm,),
        in_specs=[_tok_spec(tm, 3 * D), prev, _row_spec(D, 8)],
        out_specs=_tok_spec(tm, D), out_shape=jax.ShapeDtypeStruct((t, D), bf16),
        compiler_params=_cp(1),
    )(p, p, cw)


def _conv_bwd(p, cw, dq):
    t = p.shape[0]
    tm = _tok_tile(t)
    nt = t // tm
    per = tm // CONV_HALO

    def body(p_ref, pp_ref, pn_ref, cw_ref, dq_ref, dqn_ref, dp_ref, dcw_ref):
        i = pl.program_id(0)
        cw_v = cw_ref[...]
        b, c, z = p_ref[:, 0:D], p_ref[:, D:2 * D], p_ref[:, 2 * D:3 * D]
        cz = c * z
        czp = jnp.where(i == 0, 0.0, pp_ref[:, D:2 * D] * pp_ref[:, 2 * D:3 * D])
        czx = jnp.concatenate([czp, cz], axis=0)
        conv = _conv_taps(czx, cw_v)[CONV_HALO:, :]
        dq_v = dq_ref[...]
        dconv = dq_v * b
        dconv_n = jnp.where(i == nt - 1, 0.0, dqn_ref[...] * pn_ref[:, 0:D])
        dcx = jnp.concatenate([dconv, dconv_n], axis=0)
        dcz = (cw_v[2:3, :] * dcx + cw_v[1:2, :] * _roll_rows(dcx, -1) + cw_v[0:1, :] * _roll_rows(dcx, -2))[:tm, :]
        dp_ref[:, 0:D] = (dq_v * conv).astype(bf16)
        dp_ref[:, D:2 * D] = (dcz * z).astype(bf16)
        dp_ref[:, 2 * D:3 * D] = (dcz * c).astype(bf16)
        dw2 = jnp.sum(dconv * cz, axis=0, keepdims=True)
        dw1 = jnp.sum(dconv * _roll_rows(czx, 1)[CONV_HALO:, :], axis=0, keepdims=True)
        dw0 = jnp.sum(dconv * _roll_rows(czx, 2)[CONV_HALO:, :], axis=0, keepdims=True)
        _acc_add(i == 0, dcw_ref, jnp.concatenate([dw0, dw1, dw2, jnp.zeros((5, D), f32)], axis=0))

    prev = pl.BlockSpec((CONV_HALO, 3 * D), lambda i: (jnp.maximum(i * per - 1, 0), 0))
    last = t // CONV_HALO - 1
    nxt3 = pl.BlockSpec((CONV_HALO, 3 * D), lambda i: (jnp.minimum((i + 1) * per, last), 0))
    nxt1 = pl.BlockSpec((CONV_HALO, D), lambda i: (jnp.minimum((i + 1) * per, last), 0))
    return pl.pallas_call(
        body, name="conv_bwd", grid=(nt,),
        in_specs=[_tok_spec(tm, 3 * D), prev, nxt3, _row_spec(D, 8), _tok_spec(tm, D), nxt1],
        out_specs=[_tok_spec(tm, 3 * D), _row_spec(D, 8)],
        out_shape=[jax.ShapeDtypeStruct((t, 3 * D), bf16), jax.ShapeDtypeStruct((8, D), f32)],
        compiler_params=_cp(1),
    )(p, p, p, cw, dq, dq)


def _convmix_fwd(x, ln, w_in, cw, w_out):
    g, shift, scale, gate = ln
    t = x.shape[0]
    tm = _mm_tile(t)
    cs = w_in.shape[2]
    h = _ln_fwd(x, g, scale, shift)
    p = _mmx(h, w_in, "nn", "conv_in", (t // tm, N_DEV, 1),
             pl.BlockSpec((tm, D), lambda i, j, k: (i, 0)),
             pl.BlockSpec((1, D, cs), lambda i, j, k: (j, 0, 0)),
             pl.BlockSpec((tm, cs), lambda i, j, k: (i, j)), jax.ShapeDtypeStruct((t, N_DEV * cs), f32))
    q = _conv_fwd(p, cw)
    y = _mm(q, w_out, "nn", f32, "conv_out")
    return _resid_fwd(x, y, gate, 1.0), (x, h, p, q, y)


def _convmix_bwd(dxn, saved, ln, w_in, cw, w_out):
    x, h, p, q, y = saved
    g, shift, scale, gate = ln
    dy, dgate = _resid_bwd(dxn, y, gate, 1.0)
    dq = _mm(dy, w_out, "nt", f32, "conv_dq")
    dw_out = _mm(q, dy, "tn", f32, "conv_dwout")
    dp, dcw = _conv_bwd(p, cw, dq)
    t = x.shape[0]
    tm = _mm_tile(t)
    tk = _pick(t, (512, 256, 128))
    cs = w_in.shape[2]
    dh = _mmx(dp, w_in, "nt", "conv_dh", (t // tm, 1, N_DEV),
              pl.BlockSpec((tm, cs), lambda i, j, k: (i, k)),
              pl.BlockSpec((1, D, cs), lambda i, j, k: (k, 0, 0)),
              pl.BlockSpec((tm, D), lambda i, j, k: (i, 0)), jax.ShapeDtypeStruct((t, D), f32))
    dw_in = _mmx(h, dp, "tn", "conv_dwin", (N_DEV, 1, t // tk),
                 pl.BlockSpec((tk, D), lambda i, j, k: (k, 0)),
                 pl.BlockSpec((tk, cs), lambda i, j, k: (k, i)),
                 pl.BlockSpec((1, D, cs), lambda i, j, k: (i, 0, 0)), jax.ShapeDtypeStruct((N_DEV, D, cs), f32))
    dx, dg, dscale, dshift = _ln_bwd(x, g, scale, shift, dh, dxn)
    return dx, (dg, dshift, dscale, dgate), dw_in, dcw[0:3], dw_out


def _exact_dot(tri, v):
    v1 = v.astype(bf16)
    r1 = v - v1.astype(f32)
    v2 = r1.astype(bf16)
    v3 = (r1 - v2.astype(f32)).astype(bf16)
    d = lambda p: jnp.dot(tri, p, preferred_element_type=f32)
    return d(v1) + d(v2) + d(v3)


def _fox_cumf(fl, b_f):
    t = fl.shape[0]
    tc = min(t, 256)

    def body(fl_ref, b_ref, f_ref, carry_ref):
        i = pl.program_id(0)

        @pl.when(i == 0)
        def _():
            carry_ref[...] = jnp.zeros_like(carry_ref)

        lf = jax.nn.log_sigmoid(fl_ref[...] + b_ref[...])
        r = lax.broadcasted_iota(jnp.int32, (tc, tc), 0)
        c = lax.broadcasted_iota(jnp.int32, (tc, tc), 1)
        tri = (r >= c).astype(bf16)
        fc = _exact_dot(tri, lf) + carry_ref[0:1, :]
        f_ref[...] = fc
        carry_ref[0:1, :] = fc[tc - 1:tc, :]

    return pl.pallas_call(
        body, name="fox_cumf", grid=(t // tc,),
        in_specs=[_tok_spec(tc, 128), _row_spec(128)],
        out_specs=_tok_spec(tc, 128), out_shape=jax.ShapeDtypeStruct((t, 128), f32),
        scratch_shapes=[pltpu.VMEM((8, 128), f32)], compiler_params=_cp(1),
    )(fl, b_f)


def _fox_cumf_bwd(df, fl, b_f):
    t = fl.shape[0]
    tc = min(t, 256)
    nt = t // tc

    def body(df_ref, fl_ref, b_ref, dfl_ref, db_ref, carry_ref):
        i = pl.program_id(0)

        @pl.when(i == 0)
        def _():
            carry_ref[...] = jnp.zeros_like(carry_ref)

        r = lax.broadcasted_iota(jnp.int32, (tc, tc), 0)
        c = lax.broadcasted_iota(jnp.int32, (tc, tc), 1)
        tri = (r <= c).astype(bf16)
        dlf = _exact_dot(tri, df_ref[...]) + carry_ref[0:1, :]
        carry_ref[0:1, :] = dlf[0:1, :]
        dfl = dlf * jax.nn.sigmoid(-(fl_ref[...] + b_ref[...]))
        dfl_ref[...] = dfl
        _acc_add(i == 0, db_ref, jnp.sum(dfl, axis=0, keepdims=True))

    return pl.pallas_call(
        body, name="fox_cumf_bwd", grid=(nt,),
        in_specs=[_tok_spec(tc, 128, nt, True), _tok_spec(tc, 128, nt, True), _row_spec(128)],
        out_specs=[_tok_spec(tc, 128, nt, True), _row_spec(128)],
        out_shape=[jax.ShapeDtypeStruct((t, 128), f32), jax.ShapeDtypeStruct((1, 128), f32)],
        scratch_shapes=[pltpu.VMEM((8, 128), f32)], compiler_params=_cp(1),
    )(df, fl, b_f)


def _fox_block(t):
    return min(t, 512)


FOX_STRIP = 64


def _fox_mask(s, row0):
    r = row0 + lax.broadcasted_iota(jnp.int32, s.shape, 0)
    c = lax.broadcasted_iota(jnp.int32, s.shape, 1)
    return jnp.where(r >= c, s, -jnp.inf)


def _strips(tb):
    ts = min(tb, FOX_STRIP)
    return [(r * ts, slice(r * ts, (r + 1) * ts)) for r in range(tb // ts)]


FOX_PAIRS = FOX_HEADS // 2
FOX_PW = 2 * FOX_HEAD_DIM


def _half(shape, hh):
    lane = lax.broadcasted_iota(jnp.int32, shape, len(shape) - 1)
    return lane < FOX_HEAD_DIM if hh == 0 else lane >= FOX_HEAD_DIM


def _by_half(v0, v1):
    return jnp.where(_half(v0.shape, 0), v0, v1)


def _half_sums(v):
    first = _half(v.shape, 0)
    s0 = jnp.sum(jnp.where(first, v, 0.0), axis=-1, keepdims=True)
    s1 = jnp.sum(jnp.where(first, 0.0, v), axis=-1, keepdims=True)
    return jnp.where(first, s0, s1)


def _pair_norm(v, gain, mult):
    ms = _half_sums(v * v) * (1.0 / FOX_HEAD_DIM)
    return v * lax.rsqrt(ms + NORM_EPS) * gain * mult


def _pair_cols(p):
    return slice(p * FOX_PW, (p + 1) * FOX_PW)


def _pair_qknorm(proj, q_gain, k_gain):
    t = proj.shape[0]
    tm = _tok_tile(t)

    def body(q_ref, k_ref, v_ref, qg_ref, kg_ref, o_ref):
        for p in range(FOX_PAIRS):
            cb = _pair_cols(p)
            o_ref[:, cb] = _pair_norm(q_ref[:, cb], qg_ref[...], FOX_HEAD_DIM ** -0.5).astype(bf16)
            o_ref[:, D + p * FOX_PW:D + (p + 1) * FOX_PW] = _pair_norm(k_ref[:, cb], kg_ref[...], 1.0).astype(bf16)
        o_ref[:, 2 * D:3 * D] = v_ref[...].astype(bf16)

    part = lambda s: pl.BlockSpec((tm, D), lambda i: (i, s))
    return pl.pallas_call(
        body, name="fox_qknorm", grid=(t // tm,),
        in_specs=[part(0), part(1), part(2), _row_spec(FOX_PW), _row_spec(FOX_PW)],
        out_specs=_tok_spec(tm, 3 * D), out_shape=jax.ShapeDtypeStruct((t, 3 * D), bf16), compiler_params=_cp(1),
    )(proj, proj, proj, q_gain, k_gain)


def _pair_qknorm_bwd(proj, q_gain, k_gain, dqn, dkn):
    t = proj.shape[0]
    tm = _tok_tile(t)

    def body(q_ref, k_ref, qg_ref, kg_ref, dqn_ref, dkn_ref, o_ref, dqg_ref, dkg_ref):
        dqg = jnp.zeros((1, FOX_PW), f32)
        dkg = jnp.zeros((1, FOX_PW), f32)
        for p in range(FOX_PAIRS):
            cb = _pair_cols(p)
            _, vq = jax.vjp(lambda v, gn: _pair_norm(v, gn, FOX_HEAD_DIM ** -0.5), q_ref[:, cb], qg_ref[...])
            dq, g1 = vq(dqn_ref[:, cb])
            _, vk = jax.vjp(lambda v, gn: _pair_norm(v, gn, 1.0), k_ref[:, cb], kg_ref[...])
            dk, g2 = vk(dkn_ref[:, cb])
            o_ref[:, cb] = dq.astype(bf16)
            o_ref[:, D + p * FOX_PW:D + (p + 1) * FOX_PW] = dk.astype(bf16)
            dqg, dkg = dqg + g1, dkg + g2
        first = pl.program_id(0) == 0
        _acc_add(first, dqg_ref, dqg + pltpu.roll(dqg, FOX_HEAD_DIM, 1))
        _acc_add(first, dkg_ref, dkg + pltpu.roll(dkg, FOX_HEAD_DIM, 1))

    part = lambda s: pl.BlockSpec((tm, D), lambda i: (i, s))
    gshape = jax.ShapeDtypeStruct((1, FOX_PW), f32)
    return pl.pallas_call(
        body, name="fox_qknorm_bwd", grid=(t // tm,),
        in_specs=[part(0), part(1), _row_spec(FOX_PW), _row_spec(FOX_PW), _tok_spec(tm, D), _tok_spec(tm, D)],
        out_specs=[_tok_spec(tm, 2 * D), _row_spec(FOX_PW), _row_spec(FOX_PW)],
        out_shape=[jax.ShapeDtypeStruct((t, 2 * D), bf16), gshape, gshape], compiler_params=_cp(1),
    )(proj, proj, q_gain, k_gain, dqn, dkn)


def _pair_delta(o, do):
    t = o.shape[0]
    tm = _tok_tile(t)

    def body(o_ref, do_ref, dl_ref, dob_ref):
        dob = do_ref[...].astype(bf16)
        dob_ref[...] = dob
        prod = dob.astype(f32) * o_ref[...]
        for p in range(FOX_PAIRS):
            dl_ref[p] = _half_sums(prod[:, _pair_cols(p)])

    return pl.pallas_call(
        body, name="fox_delta", grid=(t // tm,), in_specs=[_tok_spec(tm, D), _tok_spec(tm, D)],
        out_specs=[pl.BlockSpec((FOX_PAIRS, tm, FOX_PW), lambda i: (0, i, 0)), _tok_spec(tm, D)],
        out_shape=[jax.ShapeDtypeStruct((FOX_PAIRS, t, FOX_PW), f32), jax.ShapeDtypeStruct((t, D), bf16)],
        compiler_params=_cp(1),
    )(o, do)


def _pair_attn_fwd(qkvn, fcol, fref):
    t = qkvn.shape[0]
    tb = _fox_block(t)
    nq = t // tb
    pw = FOX_PW
    kcol, vcol = D // pw, 2 * D // pw

    def body(q_ref, k_ref, v_ref, fc_ref, fr_ref, o_ref, ob_ref, lse_ref):
        i = pl.program_id(1)
        qp = q_ref[...]
        qh = [jnp.where(_half(qp.shape, hh), qp, jnp.zeros_like(qp)) for hh in range(2)]

        def step(j, carry, diag):
            off = pl.multiple_of(j * tb, tb)
            kj = k_ref[pl.ds(off, tb), :]
            vj = v_ref[pl.ds(off, tb), :]
            out = []
            for hh in range(2):
                m, l, acc = carry[hh]
                bias = fr_ref[0, hh, pl.ds(i, 1), 0:1] - fc_ref[0, hh:hh + 1, pl.ds(off, tb)]
                s = lax.dot_general(qh[hh], kj, _DN["nt"], preferred_element_type=f32) + bias
                if diag:
                    s = _fox_mask(s, 0)
                m_new = jnp.maximum(m, jnp.max(s, axis=-1, keepdims=True))
                alpha = jnp.exp(m - m_new)
                p = jnp.exp(s - m_new)
                l = alpha * l + jnp.sum(p, axis=-1, keepdims=True)
                p_hi = p.astype(bf16)
                p_lo = (p - p_hi.astype(f32)).astype(bf16)
                pv = jnp.dot(p_hi, vj, preferred_element_type=f32) + jnp.dot(p_lo, vj, preferred_element_type=f32)
                out.append((m_new, l, alpha * acc + pv))
            return tuple(out)

        one = (jnp.full((tb, 1), -jnp.inf, f32), jnp.zeros((tb, 1), f32), jnp.zeros((tb, pw), f32))
        carry = lax.fori_loop(0, i, lambda j, c: step(j, c, False), (one, one))
        (m0, l0, a0), (m1, l1, a1) = step(i, carry, True)
        o = _by_half(a0 / l0, a1 / l1)
        o_ref[...] = o
        ob_ref[...] = o.astype(bf16)
        lse_ref[0] = _by_half(jnp.broadcast_to(m0 + jnp.log(l0), (tb, pw)), jnp.broadcast_to(m1 + jnp.log(l1), (tb, pw)))

    return pl.pallas_call(
        body, name="fox_attn_fwd", grid=(FOX_PAIRS, nq),
        in_specs=[pl.BlockSpec((tb, pw), lambda p, i: (i, p)),
                  pl.BlockSpec((t, pw), lambda p, i: (0, kcol + p)),
                  pl.BlockSpec((t, pw), lambda p, i: (0, vcol + p)),
                  pl.BlockSpec((1, 2, t), lambda p, i: (p, 0, 0)),
                  pl.BlockSpec((1, 2, nq, 128), lambda p, i: (p, 0, 0, 0))],
        out_specs=[pl.BlockSpec((tb, pw), lambda p, i: (i, p)), pl.BlockSpec((tb, pw), lambda p, i: (i, p)),
                   pl.BlockSpec((1, tb, pw), lambda p, i: (p, i, 0))],
        out_shape=[jax.ShapeDtypeStruct((t, D), f32), jax.ShapeDtypeStruct((t, D), bf16),
                   jax.ShapeDtypeStruct((FOX_PAIRS, t, pw), f32)],
        compiler_params=_cp(2),
    )(qkvn, qkvn, qkvn, fcol, fref)


def _pair_attn_bwd(qkvn, fcol, fref, lse, delta, dob):
    t = qkvn.shape[0]
    tb = _fox_block(t)
    nq = t // tb
    pw = FOX_PW
    kcol, vcol = D // pw, 2 * D // pw

    def body(q_ref, k_ref, v_ref, fc_ref, fr_ref, lse_ref, dl_ref, do_ref, dk_ref, dv_ref, df_ref, dq_ref,
             s_ref, dp_ref, p_ref, ds_ref):
        j = pl.program_id(1)
        kj = k_ref[...]
        vj = v_ref[...]
        dk_ref[...] = jnp.zeros_like(dk_ref)
        dv_ref[...] = jnp.zeros_like(dv_ref)
        df_ref[...] = jnp.zeros_like(df_ref)

        @pl.when(j == 0)
        def _():
            dq_ref[...] = jnp.zeros_like(dq_ref)

        def step(i, diag):
            off = pl.multiple_of(i * tb, tb)
            qi = q_ref[pl.ds(off, tb), :]
            doi = do_ref[pl.ds(off, tb), :]
            parts = []
            for hh in range(2):
                mine = _half(qi.shape, hh)
                c0 = hh * FOX_HEAD_DIM
                s_ref[hh] = lax.dot_general(jnp.where(mine, qi, jnp.zeros_like(qi)), kj, _DN["nt"],
                                            preferred_element_type=f32)
                dp_ref[hh] = lax.dot_general(jnp.where(mine, doi, jnp.zeros_like(doi)), vj, _DN["nt"],
                                             preferred_element_type=f32)
                bias = fr_ref[0, hh, pl.ds(i, 1), 0:1] - fc_ref[0, hh:hh + 1, :]
                df = jnp.zeros((1, tb), f32)
                for row0, rs in _strips(tb):
                    rows = pl.ds(off + row0, rs.stop - rs.start)
                    s = s_ref[hh, rs, :] + bias
                    if diag:
                        s = _fox_mask(s, row0)
                    p = jnp.exp(s - lse_ref[0, rows, c0:c0 + 1])
                    p_ref[hh, rs, :] = p.astype(bf16)
                    ds = p * (dp_ref[hh, rs, :] - dl_ref[0, rows, c0:c0 + 1])
                    ds_ref[hh, rs, :] = ds.astype(bf16)
                    df = df + jnp.sum(ds, axis=0, keepdims=True)
                df_ref[0, hh:hh + 1, :] -= df
                parts.append((lax.dot_general(p_ref[hh], doi, _DN["tn"], preferred_element_type=f32),
                              lax.dot_general(ds_ref[hh], qi, _DN["tn"], preferred_element_type=f32),
                              jnp.dot(ds_ref[hh], kj, preferred_element_type=f32)))
            dv_ref[...] += _by_half(parts[0][0], parts[1][0])
            dk_ref[...] += _by_half(parts[0][1], parts[1][1])
            dq_ref[pl.ds(off, tb), :] += _by_half(parts[0][2], parts[1][2])

        def loop_body(i, carry):
            step(i, False)
            return carry

        step(j, True)
        lax.fori_loop(j + 1, nq, loop_body, 0)

    scratch = [pltpu.VMEM((2, tb, tb), f32), pltpu.VMEM((2, tb, tb), f32), pltpu.VMEM((2, tb, tb), bf16),
               pltpu.VMEM((2, tb, tb), bf16)]
    blk = lambda c0: pl.BlockSpec((tb, pw), lambda p, j: (j, c0 + p))
    full = lambda c0: pl.BlockSpec((t, pw), lambda p, j: (0, c0 + p))
    stat = pl.BlockSpec((1, t, pw), lambda p, j: (p, 0, 0))
    nat = jax.ShapeDtypeStruct((t, D), f32)
    return pl.pallas_call(
        body, name="fox_attn_bwd", grid=(FOX_PAIRS, nq),
        in_specs=[full(0), blk(kcol), blk(vcol), pl.BlockSpec((1, 2, tb), lambda p, j: (p, 0, j)),
                  pl.BlockSpec((1, 2, nq, 128), lambda p, j: (p, 0, 0, 0)), stat, stat, full(0)],
        out_specs=[blk(0), blk(0), pl.BlockSpec((1, 2, tb), lambda p, j: (p, 0, j)), full(0)],
        out_shape=[nat, nat, jax.ShapeDtypeStruct((FOX_PAIRS, 2, t), f32), nat],
        scratch_shapes=scratch, compiler_params=_cp(2),
    )(qkvn, qkvn, qkvn, fcol, fref, lse, delta, dob)


def _fox_fwd(x, ln, w_in, b_f, q_gain, k_gain, w_o):
    g, shift, scale, gate = ln
    t = x.shape[0]
    tb = _fox_block(t)
    h = _ln_fwd(x, g, scale, shift)
    proj = _mm(h, w_in, "nn", f32, "fox_in")
    fl = proj[:, 3 * D:3 * D + 128]
    fcum = _fox_cumf(fl, b_f)
    fcol = fcum[:, :FOX_HEADS].T.reshape(FOX_PAIRS, 2, t)
    fref = jnp.broadcast_to(fcol[:, :, ::tb][..., None], (FOX_PAIRS, 2, t // tb, 128))
    gains = (jnp.tile(q_gain, (1, 2)), jnp.tile(k_gain, (1, 2)))
    qkvn = _pair_qknorm(proj, *gains)
    o, ob, lse = _pair_attn_fwd(qkvn, fcol, fref)
    y = _mm(ob, w_o, "nn", f32, "fox_out")
    return _resid_fwd(x, y, gate, 1.0), (x, h, proj, fl, fcol, fref, gains, qkvn, o, lse, ob, y)


def _fox_bwd(dxn, saved, ln, w_in, b_f, w_o):
    x, h, proj, fl, fcol, fref, gains, qkvn, o, lse, ob, y = saved
    g, shift, scale, gate = ln
    t = x.shape[0]
    dy, dgate = _resid_bwd(dxn, y, gate, 1.0)
    do = _mm(dy, w_o, "nt", f32, "fox_do")
    dw_o = _mm(ob, dy, "tn", f32, "fox_dwo")
    delta, dob = _pair_delta(o, do)
    dkn, dv, dfcol, dqn = _pair_attn_bwd(qkvn, fcol, fref, lse, delta, dob)
    dqk, dqg, dkg = _pair_qknorm_bwd(proj, *gains, dqn, dkn)
    df = jnp.pad(dfcol.reshape(FOX_HEADS, t).T, ((0, 0), (0, 128 - FOX_HEADS)))
    dfl, db_f = _fox_cumf_bwd(df, fl, b_f)
    dproj = jnp.concatenate([dqk, dv.astype(bf16), dfl.astype(bf16)], axis=1)
    dh = _mm(dproj, w_in, "nt", f32, "fox_dh")
    dw_in = _mm(h, dproj, "tn", f32, "fox_dwin")
    dx, dg, dscale, dshift = _ln_bwd(x, g, scale, shift, dh, dxn)
    return (dx, (dg, dshift, dscale, dgate), dw_in, db_f, dqg[:, :FOX_HEAD_DIM], dkg[:, :FOX_HEAD_DIM], dw_o)


def _s5_disc(lam_re, lam_im, log_dt, b_re, b_im):
    dt = jnp.exp(log_dt)
    mag = jnp.exp(lam_re * dt)
    lb_re, lb_im = mag * jnp.cos(lam_im * dt), mag * jnp.sin(lam_im * dt)
    den = lam_re * lam_re + lam_im * lam_im
    nr, ni = lb_re - 1.0, lb_im
    k_re = (nr * lam_re + ni * lam_im) / den
    k_im = (ni * lam_re - nr * lam_im) / den
    return lb_re, lb_im, k_re * b_re - k_im * b_im, k_re * b_im + k_im * b_re


def _s5_prep(lam_re, lam_im, log_dt, b_re, b_im):
    def body(ar_ref, ai_ref, dt_ref, br_ref, bi_ref, lr_ref, li_ref, bbr_ref, bbi_ref):
        lr, li, bbr, bbi = _s5_disc(ar_ref[...], ai_ref[...], dt_ref[...], br_ref[...], bi_ref[...])
        lr_ref[...] = lr
        li_ref[...] = li
        bbr_ref[...] = bbr
        bbi_ref[...] = bbi

    small = jax.ShapeDtypeStruct(lam_re.shape, f32)
    bigs = jax.ShapeDtypeStruct(b_re.shape, f32)
    return pl.pallas_call(body, name="s5_prep", out_shape=[small, small, bigs, bigs])(lam_re, lam_im, log_dt, b_re, b_im)


def _s5_prep_bwd(lam_re, lam_im, log_dt, b_re, b_im, dlr, dli, dbbr, dbbi):
    def body(ar_ref, ai_ref, dt_ref, br_ref, bi_ref, dlr_ref, dli_ref, dbbr_ref, dbbi_ref,
             dar_ref, dai_ref, ddt_ref, dbr_ref, dbi_ref):
        _, vjp = jax.vjp(_s5_disc, ar_ref[...], ai_ref[...], dt_ref[...], br_ref[...], bi_ref[...])
        dar, dai, ddt, dbr, dbi = vjp((dlr_ref[...], dli_ref[...], dbbr_ref[...], dbbi_ref[...]))
        dar_ref[...] = dar
        dai_ref[...] = dai
        ddt_ref[...] = jnp.broadcast_to(jnp.sum(ddt, axis=-1, keepdims=True), ddt.shape)
        dbr_ref[...] = dbr
        dbi_ref[...] = dbi

    small = jax.ShapeDtypeStruct(lam_re.shape, f32)
    bigs = jax.ShapeDtypeStruct(b_re.shape, f32)
    return pl.pallas_call(body, name="s5_prep_bwd", out_shape=[small, small, small, bigs, bigs])(
        lam_re, lam_im, log_dt, b_re, b_im, dlr, dli, dbbr, dbbi)


def _s5_tile(t):
    return min(t, 128)


def _s5_blk(k, width):
    return slice(k * width, (k + 1) * width)


def _s5_in_bd(bb):
    b4 = bb.reshape(S5_BLOCKS, 8, S5_GROUP, S5_STATE)
    return jnp.einsum("kgin,gh->kgihn", b4, jnp.eye(8, dtype=bb.dtype)).reshape(S5_BLOCKS, S5_BCH, S5_BST)


def _s5_in_bd_diag(bd):
    b5 = bd.reshape(S5_BLOCKS, 8, S5_GROUP, 8, S5_STATE)
    return jnp.einsum("kgihn,gh->kgin", b5, jnp.eye(8, dtype=bd.dtype)).reshape(S5_GROUPS, S5_GROUP, S5_STATE)


def _s5_out_bd(c):
    c4 = c.reshape(S5_BLOCKS, 8, S5_GROUP, S5_STATE)
    return jnp.einsum("kgin,gh->kgnhi", c4, jnp.eye(8, dtype=c.dtype)).reshape(S5_BLOCKS, S5_BST, S5_BCH)


def _s5_out_bd_diag(bd):
    c5 = bd.reshape(S5_BLOCKS, 8, S5_STATE, 8, S5_GROUP)
    return jnp.einsum("kgnhi,gh->kgin", c5, jnp.eye(8, dtype=bd.dtype)).reshape(S5_GROUPS, S5_GROUP, S5_STATE)


def _s5_scan_fwd(x, ln, lb_re, lb_im, bbr_bd, bbi_bd, cr_bd, ci_bd, dskip):
    g, shift, scale, _ = ln
    t = x.shape[0]
    tm = _s5_tile(t)
    ns = S5_NSTATE

    def body(x_ref, g_ref, sc_ref, sh_ref, ar_ref, ai_ref, bbr_ref, bbi_ref, cr_ref, ci_ref, d_ref,
             yy_ref, xr_ref, xi_ref, cre_ref, cim_ref):
        @pl.when(pl.program_id(0) == 0)
        def _():
            cre_ref[...] = jnp.zeros_like(cre_ref)
            cim_ref[...] = jnp.zeros_like(cim_ref)

        h = _adaln(x_ref[...], g_ref[...], sc_ref[...], sh_ref[...])
        ub = h.astype(bf16)
        for k in range(S5_BLOCKS):
            uk = ub[:, _s5_blk(k, S5_BCH)]
            xr_ref[:, _s5_blk(k, S5_BST)] = jnp.dot(uk, bbr_ref[k], preferred_element_type=f32)
            xi_ref[:, _s5_blk(k, S5_BST)] = jnp.dot(uk, bbi_ref[k], preferred_element_type=f32)
        ar, ai = ar_ref[...], ai_ref[...]

        def step(tt, carry):
            sr, si = carry
            row = pl.ds(tt, 1)
            nr = (ar * sr - ai * si) + xr_ref[row, :]
            ni = (ar * si + ai * sr) + xi_ref[row, :]
            xr_ref[row, :] = nr
            xi_ref[row, :] = ni
            return nr, ni

        sr, si = lax.fori_loop(0, tm, step, (cre_ref[0:1, :], cim_ref[0:1, :]), unroll=2)
        cre_ref[0:1, :] = sr
        cim_ref[0:1, :] = si
        for k in range(S5_BLOCKS):
            sb = _s5_blk(k, S5_BST)
            yk = (jnp.dot(xr_ref[:, sb].astype(bf16), cr_ref[k], preferred_element_type=f32)
                  - jnp.dot(xi_ref[:, sb].astype(bf16), ci_ref[k], preferred_element_type=f32))
            cb = _s5_blk(k, S5_BCH)
            yy_ref[:, cb] = yk + d_ref[:, cb] * h[:, cb]

    bd_in = pl.BlockSpec((S5_BLOCKS, S5_BCH, S5_BST), lambda i: (0, 0, 0))
    bd_out = pl.BlockSpec((S5_BLOCKS, S5_BST, S5_BCH), lambda i: (0, 0, 0))
    st = jax.ShapeDtypeStruct((t, ns), f32)
    return pl.pallas_call(
        body, name="s5_scan_fwd", grid=(t // tm,),
        in_specs=[_tok_spec(tm, D), _row_spec(D), _row_spec(D), _row_spec(D), _row_spec(ns), _row_spec(ns),
                  bd_in, bd_in, bd_out, bd_out, _row_spec(D)],
        out_specs=[_tok_spec(tm, D), _tok_spec(tm, ns), _tok_spec(tm, ns)],
        out_shape=[jax.ShapeDtypeStruct((t, D), f32), st, st],
        scratch_shapes=[pltpu.VMEM((8, ns), f32), pltpu.VMEM((8, ns), f32)],
        compiler_params=_cp(1),
    )(x, g, scale, shift, lb_re, lb_im, bbr_bd, bbi_bd, cr_bd, ci_bd, dskip)


def _s5_scan_bwd(dyy, x, ln, xr, xi, lb_re, lb_im, bbr_bd, bbi_bd, cr_bd, ci_bd, dskip):
    g, shift, scale, _ = ln
    t = x.shape[0]
    tm = _s5_tile(t)
    nt = t // tm
    ns = S5_NSTATE
    per = tm // 8

    def body(dyy_ref, x_ref, g_ref, sc_ref, sh_ref, xr_ref, xi_ref, xrp_ref, xip_ref, ar_ref, ai_ref,
             bbr_ref, bbi_ref, cr_ref, ci_ref, d_ref,
             du_ref, dar_ref, dai_ref, dbbr_ref, dbbi_ref, dcr_ref, dci_ref, dd_ref,
             gr_ref, gi_ref, cre_ref, cim_ref):
        i = pl.program_id(0)
        first = i == 0

        @pl.when(first)
        def _():
            cre_ref[...] = jnp.zeros_like(cre_ref)
            cim_ref[...] = jnp.zeros_like(cim_ref)

        h = _adaln(x_ref[...], g_ref[...], sc_ref[...], sh_ref[...])
        ub = h.astype(bf16)
        dyy_v = dyy_ref[...]
        dyb = dyy_v.astype(bf16)
        for k in range(S5_BLOCKS):
            dk = dyb[:, _s5_blk(k, S5_BCH)]
            sb = _s5_blk(k, S5_BST)
            gr_ref[:, sb] = lax.dot_general(dk, cr_ref[k], _DN["nt"], preferred_element_type=f32)
            gi_ref[:, sb] = -lax.dot_general(dk, ci_ref[k], _DN["nt"], preferred_element_type=f32)
        ar, ai = ar_ref[...], ai_ref[...]

        def step(s, carry):
            nr_, ni_ = carry
            row = pl.ds(tm - 1 - s, 1)
            nr = gr_ref[row, :] + (ar * nr_ + ai * ni_)
            ni = gi_ref[row, :] + (ar * ni_ - ai * nr_)
            gr_ref[row, :] = nr
            gi_ref[row, :] = ni
            return nr, ni

        nr, ni = lax.fori_loop(0, tm, step, (cre_ref[0:1, :], cim_ref[0:1, :]), unroll=2)
        cre_ref[0:1, :] = nr
        cim_ref[0:1, :] = ni

        is_first_tile = i == nt - 1
        _acc_add(first, dd_ref, jnp.sum(dyy_v * h, axis=0, keepdims=True))
        for k in range(S5_BLOCKS):
            cb, sb = _s5_blk(k, S5_BCH), _s5_blk(k, S5_BST)
            xr_v, xi_v = xr_ref[:, sb], xi_ref[:, sb]
            xrp = jnp.where(is_first_tile, 0.0, xrp_ref[:, sb])
            xip = jnp.where(is_first_tile, 0.0, xip_ref[:, sb])
            xr_s = _roll_rows(jnp.concatenate([xrp, xr_v], axis=0), 1)[8:, :]
            xi_s = _roll_rows(jnp.concatenate([xip, xi_v], axis=0), 1)[8:, :]
            gr, gi = gr_ref[:, sb], gi_ref[:, sb]
            dar_k = jnp.sum(gr * xr_s + gi * xi_s, axis=0, keepdims=True)
            dai_k = jnp.sum(gi * xr_s - gr * xi_s, axis=0, keepdims=True)

            @pl.when(first)
            def _():
                dar_ref[:, sb] = dar_k
                dai_ref[:, sb] = dai_k

            @pl.when(jnp.logical_not(first))
            def _():
                dar_ref[:, sb] += dar_k
                dai_ref[:, sb] += dai_k

            grb, gib = gr.astype(bf16), gi.astype(bf16)
            uk, dk = ub[:, cb], dyb[:, cb]
            tn = lambda a_, b_: lax.dot_general(a_, b_, _DN["tn"], preferred_element_type=f32)
            vals = (tn(uk, grb), tn(uk, gib), tn(xr_v.astype(bf16), dk), -tn(xi_v.astype(bf16), dk))
            for ref, val in zip((dbbr_ref, dbbi_ref, dcr_ref, dci_ref), vals):
                @pl.when(first)
                def _():
                    ref[k] = val

                @pl.when(jnp.logical_not(first))
                def _():
                    ref[k] += val
            du_k = (lax.dot_general(grb, bbr_ref[k], _DN["nt"], preferred_element_type=f32)
                    + lax.dot_general(gib, bbi_ref[k], _DN["nt"], preferred_element_type=f32))
            du_ref[:, cb] = du_k + d_ref[:, cb] * dyy_v[:, cb]

    rev = lambda c: _tok_spec(tm, c, nt, True)
    prev = pl.BlockSpec((8, ns), lambda i: (jnp.maximum((nt - 1 - i) * per - 1, 0), 0))
    bd_in = pl.BlockSpec((S5_BLOCKS, S5_BCH, S5_BST), lambda i: (0, 0, 0))
    bd_out = pl.BlockSpec((S5_BLOCKS, S5_BST, S5_BCH), lambda i: (0, 0, 0))
    row_ns = jax.ShapeDtypeStruct((1, ns), f32)
    bd_in_s = jax.ShapeDtypeStruct((S5_BLOCKS, S5_BCH, S5_BST), f32)
    bd_out_s = jax.ShapeDtypeStruct((S5_BLOCKS, S5_BST, S5_BCH), f32)
    return pl.pallas_call(
        body, name="s5_scan_bwd", grid=(nt,),
        in_specs=[rev(D), rev(D), _row_spec(D), _row_spec(D), _row_spec(D), rev(ns), rev(ns), prev, prev,
                  _row_spec(ns), _row_spec(ns), bd_in, bd_in, bd_out, bd_out, _row_spec(D)],
        out_specs=[rev(D), _row_spec(ns), _row_spec(ns), bd_in, bd_in, bd_out, bd_out, _row_spec(D)],
        out_shape=[jax.ShapeDtypeStruct((t, D), f32), row_ns, row_ns, bd_in_s, bd_in_s, bd_out_s, bd_out_s,
                   jax.ShapeDtypeStruct((1, D), f32)],
        scratch_shapes=[pltpu.VMEM((tm, ns), f32), pltpu.VMEM((tm, ns), f32),
                        pltpu.VMEM((8, ns), f32), pltpu.VMEM((8, ns), f32)],
        compiler_params=_cp(1),
    )(dyy, x, g, scale, shift, xr, xi, xr, xi, lb_re, lb_im, bbr_bd, bbi_bd, cr_bd, ci_bd, dskip)


def _s5_gelu(yy):
    t = yy.shape[0]
    tm = _tok_tile(t)

    def body(y_ref, o_ref):
        o_ref[...] = jax.nn.gelu(y_ref[...]).astype(bf16)

    return pl.pallas_call(
        body, name="s5_gelu", grid=(t // tm,), in_specs=[_tok_spec(tm, D)], out_specs=_tok_spec(tm, D),
        out_shape=jax.ShapeDtypeStruct((t, D), bf16), compiler_params=_cp(1),
    )(yy)


def _s5_glu(gl, z):
    return gl * jax.nn.sigmoid(z)


def _s5_out(x, yy, z, gate):
    t = x.shape[0]
    tm = _tok_tile(t)

    def body(x_ref, y_ref, z_ref, gt_ref, o_ref):
        o_ref[...] = x_ref[...] + (1.0 + gt_ref[...]) * _s5_glu(jax.nn.gelu(y_ref[...]), z_ref[...])

    return pl.pallas_call(
        body, name="s5_out", grid=(t // tm,),
        in_specs=[_tok_spec(tm, D), _tok_spec(tm, D), _tok_spec(tm, D), _row_spec(D)],
        out_specs=_tok_spec(tm, D), out_shape=jax.ShapeDtypeStruct((t, D), f32), compiler_params=_cp(1),
    )(x, yy, z, gate)


def _s5_out_bwd(dxn, yy, z, gate):
    t = dxn.shape[0]
    tm = _tok_tile(t)

    def body(dxn_ref, y_ref, z_ref, gt_ref, dz_ref, dgl_ref, dgt_ref):
        dxn_v = dxn_ref[...]
        gl = jax.nn.gelu(y_ref[...])
        out, vjp = jax.vjp(_s5_glu, gl, z_ref[...])
        dgl, dz = vjp((1.0 + gt_ref[...]) * dxn_v)
        dz_ref[...] = dz.astype(bf16)
        dgl_ref[...] = dgl
        _acc_add(pl.program_id(0) == 0, dgt_ref, jnp.sum(dxn_v * out, axis=0, keepdims=True))

    return pl.pallas_call(
        body, name="s5_out_bwd", grid=(t // tm,),
        in_specs=[_tok_spec(tm, D), _tok_spec(tm, D), _tok_spec(tm, D), _row_spec(D)],
        out_specs=[_tok_spec(tm, D), _tok_spec(tm, D), _row_spec(D)],
        out_shape=[jax.ShapeDtypeStruct((t, D), bf16), jax.ShapeDtypeStruct((t, D), f32),
                   jax.ShapeDtypeStruct((1, D), f32)],
        compiler_params=_cp(1),
    )(dxn, yy, z, gate)


def _s5_gelu_bwd(yy, dgl_a, dgl_b):
    t = yy.shape[0]
    tm = _tok_tile(t)

    def body(y_ref, a_ref, b_ref, o_ref):
        _, vjp = jax.vjp(jax.nn.gelu, y_ref[...])
        o_ref[...] = vjp(a_ref[...] + b_ref[...])[0]

    return pl.pallas_call(
        body, name="s5_gelu_bwd", grid=(t // tm,),
        in_specs=[_tok_spec(tm, D), _tok_spec(tm, D), _tok_spec(tm, D)],
        out_specs=_tok_spec(tm, D), out_shape=jax.ShapeDtypeStruct((t, D), f32), compiler_params=_cp(1),
    )(yy, dgl_a, dgl_b)


def _s5_params(lam_re, lam_im, log_dt, b_re, b_im):
    bc = lambda a: a.reshape(S5_GROUPS, 1, -1)
    return (bc(lam_re), bc(lam_im), jnp.broadcast_to(log_dt.reshape(S5_GROUPS, 1, 1), (S5_GROUPS, 1, S5_STATE)),
            b_re.transpose(0, 2, 1), b_im.transpose(0, 2, 1))


def _s5_fwd(x, ln, raw, c_re, c_im, dskip, w_glu):
    gate = ln[3]
    lb_re, lb_im, bb_re, bb_im = _s5_prep(*raw)
    lbr, lbi = lb_re.reshape(1, S5_NSTATE), lb_im.reshape(1, S5_NSTATE)
    bds = (_s5_in_bd(bb_re).astype(bf16), _s5_in_bd(bb_im).astype(bf16),
           _s5_out_bd(c_re).astype(bf16), _s5_out_bd(c_im).astype(bf16))
    yy, xr, xi = _s5_scan_fwd(x, ln, lbr, lbi, *bds, dskip)
    gl = _s5_gelu(yy)
    z = _mm(gl, w_glu, "nn", f32, "s5_glu_mm")
    return _s5_out(x, yy, z, gate), (x, lbr, lbi, bds, yy, xr, xi, gl, z)


def _s5_bwd(dxn, saved, ln, raw, dskip, w_glu):
    x, lbr, lbi, bds, yy, xr, xi, gl, z = saved
    g, shift, scale, gate = ln
    dz, dgl_a, dgate = _s5_out_bwd(dxn, yy, z, gate)
    dgl_b = _mm(dz, w_glu, "nt", f32, "s5_dgl")
    dw_glu = _mm(gl, dz, "tn", f32, "s5_dwglu")
    dyy = _s5_gelu_bwd(yy, dgl_a, dgl_b)
    du, dar, dai, dbbr_bd, dbbi_bd, dcr_bd, dci_bd, dd = _s5_scan_bwd(dyy, x, ln, xr, xi, lbr, lbi, *bds, dskip)
    shp = (S5_GROUPS, 1, S5_STATE)
    d_lam_re, d_lam_im, d_dt, d_b_re, d_b_im = _s5_prep_bwd(
        *raw, dar.reshape(shp), dai.reshape(shp), _s5_in_bd_diag(dbbr_bd), _s5_in_bd_diag(dbbi_bd))
    dx, dg, dscale, dshift = _ln_bwd(x, g, scale, shift, du, dxn)
    grads = dict(
        s5_lam_re=d_lam_re.reshape(1, S5_GROUPS, S5_STATE), s5_lam_im=d_lam_im.reshape(1, S5_GROUPS, S5_STATE),
        s5_log_dt=d_dt[:, 0, 0].reshape(1, S5_GROUPS),
        s5_b_re=d_b_re.transpose(0, 2, 1)[None], s5_b_im=d_b_im.transpose(0, 2, 1)[None],
        s5_c_re=_s5_out_bd_diag(dcr_bd)[None], s5_c_im=_s5_out_bd_diag(dci_bd)[None],
        s5_d=dd, s5_w_glu=dw_glu)
    return dx, (dg, dshift, dscale, dgate), grads


_MESH = pl.DeviceIdType.MESH
_ANY = pl.BlockSpec(memory_space=pl.ANY)


def _me():
    return lax.axis_index("x"), lax.axis_index("y"), lax.axis_index("c")


def _dev_index(x, y, c):
    return 4 * x + 2 * y + c


def _all_gather(vs, name):
    n = len(vs)

    def body(*refs):
        v_refs, out_refs = refs[:n], refs[n:2 * n]
        send_sems, recv_sems, local_sems = refs[2 * n:]
        x, y, cc = _me()
        me, sibling = (x, y, cc), (x, y, 1 - cc)
        chips = [(1 - x, y), (x, 1 - y), (1 - x, 1 - y)]
        sends, local = [], []

        def copy(a, k, block, to, src=None):
            rows = out_refs[a].at[_dev_index(*block)]
            return pltpu.make_async_remote_copy(
                src_ref=rows if src is None else src, dst_ref=rows,
                send_sem=send_sems.at[7 * a + k], recv_sem=recv_sems.at[7 * a + k], device_id=to, device_id_type=_MESH)

        for a in range(n):
            mine = pltpu.make_async_copy(v_refs[a], out_refs[a].at[_dev_index(*me)], local_sems.at[a])
            mine.start()
            local.append(mine)
            first = [copy(a, 0, me, sibling, src=v_refs[a])]
            first += [copy(a, 1 + j, me, (*chip, cc), src=v_refs[a]) for j, chip in enumerate(chips)]
            for cp in first:
                cp.start()
            sends += first
        for a in range(n):
            for j, chip in enumerate(chips):
                copy(a, 1 + j, (*chip, cc), me).wait_recv()
                passed = copy(a, 4 + j, (*chip, cc), sibling)
                passed.start()
                sends.append(passed)
        for a in range(n):
            copy(a, 0, sibling, me).wait_recv()
            for j, chip in enumerate(chips):
                copy(a, 4 + j, (*chip, 1 - cc), me).wait_recv()
        for cp in sends:
            cp.wait_send()
        for cp in local:
            cp.wait()

    return pl.pallas_call(
        body, name=name, out_shape=[jax.ShapeDtypeStruct((N_DEV,) + v.shape, v.dtype) for v in vs],
        in_specs=[_ANY] * n, out_specs=[_ANY] * n,
        scratch_shapes=[pltpu.SemaphoreType.DMA((7 * n,)), pltpu.SemaphoreType.DMA((7 * n,)),
                        pltpu.SemaphoreType.DMA((n,))],
    )(*vs)


def _exchange_pair(vs, name):
    n = len(vs)

    def body(*refs):
        v_refs, out_refs = refs[:n], refs[n:2 * n]
        send_sems, recv_sems = refs[2 * n:]
        x, y, cc = _me()
        sibling = (x, y, 1 - cc)
        copies = []
        for a in range(n):
            for k in range(4):
                cp = pltpu.make_async_remote_copy(
                    src_ref=v_refs[a].at[2 * k + (1 - cc)], dst_ref=out_refs[a].at[k],
                    send_sem=send_sems.at[4 * a + k], recv_sem=recv_sems.at[4 * a + k],
                    device_id=sibling, device_id_type=_MESH)
                cp.start()
                copies.append(cp)
        for cp in copies:
            cp.wait_recv()
        for cp in copies:
            cp.wait_send()

    return pl.pallas_call(
        body, name=name, out_shape=[jax.ShapeDtypeStruct((4,) + v.shape[1:], v.dtype) for v in vs],
        in_specs=[_ANY] * n, out_specs=[_ANY] * n,
        scratch_shapes=[pltpu.SemaphoreType.DMA((4 * n,)), pltpu.SemaphoreType.DMA((4 * n,))],
    )(*vs)


def _pair_sum(v, got):
    _, r, c = v.shape
    tr = _pick(r, (512, 352, 256, 128))
    core = lax.axis_index("c").astype(jnp.int32).reshape(1)

    def body(c_ref, v_ref, g_ref, o_ref):
        o_ref[...] = (v_ref[...] + g_ref[...]).astype(bf16)

    return pl.pallas_call(
        body, name="pair_sum",
        grid_spec=pltpu.PrefetchScalarGridSpec(
            num_scalar_prefetch=1, grid=(4, r // tr),
            in_specs=[pl.BlockSpec((1, tr, c), lambda k, i, c_ref: (2 * k + c_ref[0], i, 0)),
                      pl.BlockSpec((1, tr, c), lambda k, i, c_ref: (k, i, 0))],
            out_specs=pl.BlockSpec((1, tr, c), lambda k, i, c_ref: (k, i, 0))),
        out_shape=jax.ShapeDtypeStruct((4, r, c), bf16), compiler_params=_cp(2),
    )(core, v, got)


def _exchange_chips(vs, name):
    n = len(vs)

    def body(*refs):
        v_refs, out_refs = refs[:n], refs[n:2 * n]
        send_sems, recv_sems, local_sems = refs[2 * n:]
        x, y, cc = _me()
        mine = 2 * x + y
        peers = []
        for mask in (1, 2, 3):
            px = 1 - x if mask & 2 else x
            py = 1 - y if mask & 1 else y
            peers.append((mask - 1, (px, py, cc), 2 * px + py))
        local, sends = [], []
        for a in range(n):
            own = pltpu.make_async_copy(v_refs[a].at[mine], out_refs[a].at[mine], local_sems.at[a])
            own.start()
            local.append(own)
            for k, peer, pchip in peers:
                cp = pltpu.make_async_remote_copy(
                    src_ref=v_refs[a].at[pchip], dst_ref=out_refs[a].at[mine],
                    send_sem=send_sems.at[3 * a + k], recv_sem=recv_sems.at[3 * a + k],
                    device_id=peer, device_id_type=_MESH)
                cp.start()
                sends.append(cp)
        for a in range(n):
            for k, peer, pchip in peers:
                pltpu.make_async_remote_copy(
                    src_ref=v_refs[a].at[pchip], dst_ref=out_refs[a].at[pchip],
                    send_sem=send_sems.at[3 * a + k], recv_sem=recv_sems.at[3 * a + k],
                    device_id=peer, device_id_type=_MESH).wait_recv()
        for cp in sends:
            cp.wait_send()
        for cp in local:
            cp.wait()

    return pl.pallas_call(
        body, name=name, out_shape=[jax.ShapeDtypeStruct(v.shape, v.dtype) for v in vs],
        in_specs=[_ANY] * n, out_specs=[_ANY] * n,
        scratch_shapes=[pltpu.SemaphoreType.DMA((3 * n,)), pltpu.SemaphoreType.DMA((3 * n,)),
                        pltpu.SemaphoreType.DMA((n,))],
    )(*vs)


def _ada_mod(c_all, ada_w):
    cols = ada_w.shape[2]

    def body(c_ref, w_ref, o_ref):
        cond = jax.nn.silu(c_ref[...]).astype(bf16)
        o_ref[0] = jnp.dot(cond, w_ref[0].astype(bf16), preferred_element_type=f32)

    return pl.pallas_call(
        body, name="ada_mod", grid=(DEPTH,),
        in_specs=[pl.BlockSpec((16, D), lambda i: (0, 0)), pl.BlockSpec((1, D, cols), lambda i: (i, 0, 0))],
        out_specs=pl.BlockSpec((1, 16, cols), lambda i: (i, 0, 0)),
        out_shape=jax.ShapeDtypeStruct((DEPTH, 16, cols), f32), compiler_params=_cp(1),
    )(c_all, ada_w)


def _ada_grad(c_all, dmod):
    cols = dmod.shape[2]

    def body(c_ref, d_ref, o_ref):
        cond = jax.nn.silu(c_ref[...]).astype(bf16)
        o_ref[0] = lax.dot_general(cond, d_ref[0].astype(bf16), _DN["tn"], preferred_element_type=f32)

    return pl.pallas_call(
        body, name="ada_grad", grid=(DEPTH,),
        in_specs=[pl.BlockSpec((16, D), lambda i: (0, 0)), pl.BlockSpec((1, 16, cols), lambda i: (i, 0, 0))],
        out_specs=pl.BlockSpec((1, D, cols), lambda i: (i, 0, 0)),
        out_shape=jax.ShapeDtypeStruct((DEPTH, D, cols), f32), compiler_params=_cp(1),
    )(c_all, dmod)


def _row_tile(r):
    return _pick(r, (512, 352, 256, 128)) if r > 512 else r


def _sum_sources(v, name):
    n, r, c = v.shape
    tr = _row_tile(r)

    def body(v_ref, o_ref):
        acc = v_ref[0]
        for p in range(1, n):
            acc = acc + v_ref[p]
        o_ref[...] = acc.astype(f32)

    return pl.pallas_call(
        body, name=name, grid=(r // tr,),
        in_specs=[pl.BlockSpec((n, tr, c), lambda i: (0, i, 0))], out_specs=pl.BlockSpec((tr, c), lambda i: (i, 0)),
        out_shape=jax.ShapeDtypeStruct((r, c), f32), compiler_params=_cp(1),
    )(v)


def _adamw(parts, w, m, v, name):
    n, r, c = parts.shape
    tr = _row_tile(r)
    c1 = 1.0 - ADAM_B1 ** ADAM_STEP
    c2 = 1.0 - ADAM_B2 ** ADAM_STEP

    def body(p_ref, w_ref, m_ref, v_ref, g_ref, d_ref, mo_ref, vo_ref):
        g_v = p_ref[0].astype(f32)
        for p in range(1, n):
            g_v = g_v + p_ref[p].astype(f32)
        m_n = ADAM_B1 * m_ref[...] + (1.0 - ADAM_B1) * g_v
        v_n = ADAM_B2 * v_ref[...] + (1.0 - ADAM_B2) * (g_v * g_v)
        g_ref[...] = g_v
        d_ref[...] = -ADAM_LR * ((m_n / c1) / (jnp.sqrt(v_n / c2) + ADAM_EPS) + ADAM_WD * w_ref[...])
        mo_ref[...] = m_n
        vo_ref[...] = v_n

    spec = pl.BlockSpec((tr, c), lambda i: (i, 0))
    shp = jax.ShapeDtypeStruct((r, c), f32)
    return pl.pallas_call(
        body, name=name, grid=(r // tr,), in_specs=[pl.BlockSpec((n, tr, c), lambda i: (0, i, 0))] + [spec] * 3,
        out_specs=[spec] * 4, out_shape=[shp] * 4, compiler_params=_cp(1),
    )(parts, w, m, v)


def _two_d(shape):
    return (math.prod(shape[:-1]), shape[-1])


def _pack_rows(a):
    n = a.size
    rows = -(-n // (8 * PACK_C)) * 8
    return jnp.pad(a.reshape(-1), (0, rows * PACK_C - n)).reshape(rows, PACK_C)


def _pack(parts):
    return jnp.concatenate([_pack_rows(p) for p in parts], axis=0)


def _unpack(packed, shapes):
    lead = packed.shape[:-2]
    out, off = [], 0
    for s in shapes:
        n = math.prod(s)
        rows = -(-n // (8 * PACK_C)) * 8
        part = packed[..., off:off + rows, :].reshape(lead + (rows * PACK_C,))
        out.append(part[..., :n].reshape(lead + tuple(s)))
        off += rows
    return out


def _unshard(g8, axis):
    local = g8.shape[1:]
    moved = jnp.moveaxis(g8, 0, axis)
    return moved.reshape(local[:axis] + (N_DEV * local[axis],) + local[axis + 1:])


_BIG = dict(ffn_w_in=3, ffn_w_out=2, pool_w=2, fox_w_in=2, fox_w_o=1, s5_w_glu=1, conv_w_in=2, conv_w_out=1)
_SMALL_SHARDED = dict(norm_g=2, s5_d=1, conv_w=3)
_REPLICATED = ("ada_b", "pool_scale", "fox_b_f", "fox_q_gain", "fox_k_gain", "s5_lam_re", "s5_lam_im", "s5_log_dt",
               "s5_b_re", "s5_b_im", "s5_c_re", "s5_c_im")
_WEIGHTS = ("ada_w", "ada_b", "norm_g", "ffn_w_in", "ffn_w_out", "pool_w", "pool_scale", "fox_w_in", "fox_b_f",
            "fox_q_gain", "fox_k_gain", "fox_w_o", "s5_lam_re", "s5_lam_im", "s5_log_dt", "s5_b_re", "s5_b_im",
            "s5_c_re", "s5_c_im", "s5_d", "s5_w_glu", "conv_w_in", "conv_w", "conv_w_out")


def _step(x, c, target, w, m, v):
    t = x.shape[1]
    xi_, yi_, ci_ = _me()
    me = _dev_index(xi_, yi_, ci_)

    sm_shapes = [w[n].shape for n in _SMALL_SHARDED]
    small_all = _all_gather([_pack([c] + [w[n] for n in _SMALL_SHARDED])], "gather_small")[0]
    gathered = _unpack(small_all, [c.shape] + sm_shapes)
    c_all = gathered[0][:, 0, :]
    full = {n: _unshard(p, ax) for (n, ax), p in zip(_SMALL_SHARDED.items(), gathered[1:])}

    big_all = _all_gather([w[n].astype(bf16).reshape(_two_d(w[n].shape)) for n in _BIG], "gather_weights")
    gw = dict(zip(_BIG, big_all))
    ffn_w_in = gw["ffn_w_in"].reshape(N_DEV, 2 * DEPTH, D, FFN_HS)
    ffn_w_out = gw["ffn_w_out"].reshape(N_DEV, 2 * DEPTH, D_FF // N_DEV, D)
    pool_w = gw["pool_w"].reshape(N_DEV, 4, POOL_GROUP // N_DEV, POOL_GROUP).transpose(1, 0, 2, 3)
    pool_w = pool_w.reshape(4, POOL_GROUP, POOL_GROUP)
    fox_w_in = jnp.pad(gw["fox_w_in"].transpose(1, 0, 2).reshape(D, FOX_PROJ), ((0, 0), (0, FOX_PROJ_PAD - FOX_PROJ)))
    fox_w_o, s5_w_glu, conv_w_out = (gw[n].reshape(D, D) for n in ("fox_w_o", "s5_w_glu", "conv_w_out"))
    conv_w_in = gw["conv_w_in"]

    def ffn_w(i, f):
        return ffn_w_in, ffn_w_out, 2 * i + f

    c16 = jnp.pad(c_all, ((0, 8), (0, 0)))
    cols = w["ada_w"].shape[2]
    mod_sh = _ada_mod(c16, w["ada_w"])
    mod_all = _all_gather([mod_sh.reshape(DEPTH * 16, cols)], "gather_mod")[0].reshape(N_DEV, DEPTH, 16, cols)
    mod_mine = lax.dynamic_index_in_dim(mod_all, me, axis=2, keepdims=False)
    mod = (mod_mine.transpose(1, 0, 2).reshape(DEPTH, N_DEV * cols) + w["ada_b"]).reshape(DEPTH, 3, 3, D)

    norm_g = full["norm_g"]

    def ln_of(i, sub):
        return (norm_g[i, sub][None], mod[i, sub, 0][None], mod[i, sub, 1][None], mod[i, sub, 2][None])

    fox_b_f = jnp.pad(w["fox_b_f"], ((0, 0), (0, 128 - FOX_HEADS)))
    s5_raw = _s5_params(w["s5_lam_re"][0], w["s5_lam_im"][0], w["s5_log_dt"][0], w["s5_b_re"][0], w["s5_b_im"][0])
    s5_c_re, s5_c_im = w["s5_c_re"][0], w["s5_c_im"][0]
    conv_w = jnp.pad(full["conv_w"][0, :, 0, :], ((0, 5), (0, 0)))

    xs = x[0]
    saved = []
    for i in range(DEPTH):
        xs, s0 = _ffn_fwd(xs, ln_of(i, 0), *ffn_w(i, 0), 0.5)
        if i == 0:
            s1 = xs
            xs = _pool_fwd(xs, ln_of(i, 1), pool_w, w["pool_scale"])
        elif i == 1:
            xs, s1 = _fox_fwd(xs, ln_of(i, 1), fox_w_in, fox_b_f, w["fox_q_gain"], w["fox_k_gain"], fox_w_o)
        elif i == 2:
            xs, s1 = _s5_fwd(xs, ln_of(i, 1), s5_raw, s5_c_re, s5_c_im, full["s5_d"], s5_w_glu)
        else:
            xs, s1 = _convmix_fwd(xs, ln_of(i, 1), conv_w_in, conv_w, conv_w_out)
        xs, s2 = _ffn_fwd(xs, ln_of(i, 2), *ffn_w(i, 1), 0.5)
        saved.append((s0, s1, s2))
    dx, lpart = _loss_head(xs, target[0])

    grads = {}
    dmod = [[None] * 3 for _ in range(DEPTH)]
    dnorm = [[None] * 3 for _ in range(DEPTH)]
    dffn_in = lax.empty(ffn_w_in.shape, f32)
    dffn_out = lax.empty(ffn_w_out.shape, f32)

    def put_ln(i, sub, dln):
        dg, dshift, dscale, dgate = dln
        dnorm[i][sub] = dg
        dmod[i][sub] = jnp.concatenate([dshift, dscale, dgate], axis=0)

    for i in reversed(range(DEPTH)):
        s0, s1, s2 = saved[i]
        dx, dln, dffn_in, dffn_out = _ffn_bwd(dx, s2, ln_of(i, 2), *ffn_w(i, 1), 0.5, dffn_in, dffn_out)
        put_ln(i, 2, dln)
        if i == 0:
            dx, dln, dpw, dps = _pool_bwd(dx, s1, ln_of(i, 1), pool_w, w["pool_scale"])
            dpw = dpw.reshape(4, N_DEV, POOL_GROUP // N_DEV, POOL_GROUP).transpose(1, 0, 2, 3)
            grads.update(pool_w=dpw.reshape(N_DEV, 4 * POOL_GROUP // N_DEV, POOL_GROUP), pool_scale=dps)
        elif i == 1:
            dx, dln, dwi, dbf, dqg, dkg, dwo = _fox_bwd(dx, s1, ln_of(i, 1), fox_w_in, fox_b_f, fox_w_o)
            dwi = dwi[:, :FOX_PROJ].reshape(D, N_DEV, FOX_PROJ // N_DEV).transpose(1, 0, 2)
            grads.update(fox_w_in=dwi, fox_b_f=dbf[:, :FOX_HEADS], fox_q_gain=dqg, fox_k_gain=dkg,
                         fox_w_o=dwo.reshape(N_DEV, D // N_DEV, D))
        elif i == 2:
            dx, dln, gs5 = _s5_bwd(dx, s1, ln_of(i, 1), s5_raw, full["s5_d"], s5_w_glu)
            gs5["s5_w_glu"] = gs5["s5_w_glu"].reshape(N_DEV, D // N_DEV, D)
            grads.update(gs5)
        else:
            dx, dln, dwi, dcw, dwo = _convmix_bwd(dx, s1, ln_of(i, 1), conv_w_in, conv_w, conv_w_out)
            grads.update(conv_w_in=dwi, conv_w=dcw[None, :, None, :], conv_w_out=dwo.reshape(N_DEV, D // N_DEV, D))
        put_ln(i, 1, dln)
        dx, dln, dffn_in, dffn_out = _ffn_bwd(dx, s0, ln_of(i, 0), *ffn_w(i, 0), 0.5, dffn_in, dffn_out)
        put_ln(i, 0, dln)
    grads["ffn_w_in"] = dffn_in.reshape(N_DEV, 2 * DEPTH * D, FFN_HS)
    grads["ffn_w_out"] = dffn_out.reshape(N_DEV, 2 * DEPTH * D_FF // N_DEV, D)
    grads["norm_g"] = jnp.stack([jnp.concatenate(r, axis=0) for r in dnorm])
    dmod_mine = jnp.stack([jnp.stack(r) for r in dmod]).reshape(DEPTH, 9 * D)

    small_names = list(_REPLICATED[1:]) + list(_SMALL_SHARDED)
    small_parts = [dmod_mine] + [grads[n] for n in small_names] + [lpart[:, 0:1]]
    small_g = _all_gather([_pack(small_parts)], "gather_grads")[0]
    small_sum = _sum_sources(small_g, "sum_small")
    summed = dict(zip(["ada_b"] + small_names + ["loss"], _unpack(small_sum, [p.shape for p in small_parts])))
    loss = summed.pop("loss")[0, 0]
    for n, ax in _SMALL_SHARDED.items():
        local = w[n].shape[ax]
        summed[n] = lax.dynamic_slice_in_dim(summed[n], me * local, local, axis=ax)

    dmod_all = small_g[:, :DEPTH * 9].reshape(N_DEV, DEPTH, 9 * D)
    dmod_cols = lax.dynamic_slice_in_dim(dmod_all, me * cols, cols, axis=2)
    ada_g = _ada_grad(c16, jnp.pad(dmod_cols.transpose(1, 0, 2), ((0, 0), (0, 8), (0, 0))))

    big_parts = [grads[n] for n in _BIG]
    from_sibling = _exchange_pair(big_parts, "exchange_pair")
    chip_sums = [_pair_sum(p, s) for p, s in zip(big_parts, from_sibling)]
    big_landed = dict(zip(_BIG, _exchange_chips(chip_sums, "exchange_chips")))

    grad, delta, new_m, new_v = {}, {}, {}, {}
    big_landed["ada_w"] = ada_g[None]
    for n, parts in big_landed.items():
        view = _two_d(w[n].shape)
        outs = _adamw(parts.reshape((parts.shape[0],) + view), w[n].reshape(view), m[n].reshape(view),
                      v[n].reshape(view), "adamw_" + n)
        grad[n], delta[n], new_m[n], new_v[n] = (a.reshape(w[n].shape) for a in outs)
    small = [n for n in _WEIGHTS if n not in big_landed]
    small_shapes = [w[n].shape for n in small]
    pk = lambda d: _pack([d[n] for n in small])
    outs = _adamw(pk(summed)[None], pk(w), pk(m), pk(v), "adamw_small")
    for dst, packed in zip((grad, delta, new_m, new_v), outs):
        dst.update(zip(small, _unpack(packed, small_shapes)))
    return (loss, dx[None], *[grad[n] for n in _WEIGHTS], *[delta[n] for n in _WEIGHTS],
            *[new_m[n] for n in _WEIGHTS], *[new_v[n] for n in _WEIGHTS])


def kernel(x, c, ada_w, ada_b, norm_g, ffn_w_in, ffn_w_out, pool_w, pool_scale, fox_w_in, fox_b_f, fox_q_gain, fox_k_gain, fox_w_o, s5_lam_re, s5_lam_im, s5_log_dt, s5_b_re, s5_b_im, s5_c_re, s5_c_im, s5_d, s5_w_glu, conv_w_in, conv_w, conv_w_out, loss_target, m_ada_w, m_ada_b, m_norm_g, m_ffn_w_in, m_ffn_w_out, m_pool_w, m_pool_scale, m_fox_w_in, m_fox_b_f, m_fox_q_gain, m_fox_k_gain, m_fox_w_o, m_s5_lam_re, m_s5_lam_im, m_s5_log_dt, m_s5_b_re, m_s5_b_im, m_s5_c_re, m_s5_c_im, m_s5_d, m_s5_w_glu, m_conv_w_in, m_conv_w, m_conv_w_out, v_ada_w, v_ada_b, v_norm_g, v_ffn_w_in, v_ffn_w_out, v_pool_w, v_pool_scale, v_fox_w_in, v_fox_b_f, v_fox_q_gain, v_fox_k_gain, v_fox_w_o, v_s5_lam_re, v_s5_lam_im, v_s5_log_dt, v_s5_b_re, v_s5_b_im, v_s5_c_re, v_s5_c_im, v_s5_d, v_s5_w_glu, v_conv_w_in, v_conv_w, v_conv_w_out):
    ws = (ada_w, ada_b, norm_g, ffn_w_in, ffn_w_out, pool_w, pool_scale, fox_w_in, fox_b_f, fox_q_gain, fox_k_gain,
          fox_w_o, s5_lam_re, s5_lam_im, s5_log_dt, s5_b_re, s5_b_im, s5_c_re, s5_c_im, s5_d, s5_w_glu, conv_w_in,
          conv_w, conv_w_out)
    ms = (m_ada_w, m_ada_b, m_norm_g, m_ffn_w_in, m_ffn_w_out, m_pool_w, m_pool_scale, m_fox_w_in, m_fox_b_f,
          m_fox_q_gain, m_fox_k_gain, m_fox_w_o, m_s5_lam_re, m_s5_lam_im, m_s5_log_dt, m_s5_b_re, m_s5_b_im,
          m_s5_c_re, m_s5_c_im, m_s5_d, m_s5_w_glu, m_conv_w_in, m_conv_w, m_conv_w_out)
    vs = (v_ada_w, v_ada_b, v_norm_g, v_ffn_w_in, v_ffn_w_out, v_pool_w, v_pool_scale, v_fox_w_in, v_fox_b_f,
          v_fox_q_gain, v_fox_k_gain, v_fox_w_o, v_s5_lam_re, v_s5_lam_im, v_s5_log_dt, v_s5_b_re, v_s5_b_im,
          v_s5_c_re, v_s5_c_im, v_s5_d, v_s5_w_glu, v_conv_w_in, v_conv_w, v_conv_w_out)
    return _step(x, c, loss_target, dict(zip(_WEIGHTS, ws)), dict(zip(_WEIGHTS, ms)), dict(zip(_WEIGHTS, vs)))
```

```python
import math

import jax
import jax.numpy as jnp
from jax import lax
from jax.experimental import pallas as pl
from jax.experimental.pallas import tpu as pltpu

f32 = jnp.float32
bf16 = jnp.bfloat16

D = 1024
D_FF = 2816
FFN_HS = 2 * D_FF // 8
FFN_SLABS = 4
DEPTH = 4
NORM_EPS = 1e-6
N_DEV = 8
POOL_WINDOWS = (2, 4, 8, 16)
POOL_GROUP = 256
POOL_HALO = 16
FOX_HEADS = 16
FOX_HEAD_DIM = 64
FOX_PROJ = 3088
FOX_PROJ_PAD = 3200
S5_GROUPS = 64
S5_GROUP = 16
S5_STATE = 64
S5_NSTATE = S5_GROUPS * S5_STATE
S5_BLOCKS = 8
S5_BCH = 128
S5_BST = 512
CONV_HALO = 8
ADAM_LR = 0.001
ADAM_B1 = 0.9
ADAM_B2 = 0.999
ADAM_EPS = 1e-08
ADAM_WD = 0.01
ADAM_STEP = 10
VMEM_LIMIT = 56 * 1024 * 1024
PACK_C = 1024

_ARB = "arbitrary"


def _cp(n_axes):
    return pltpu.CompilerParams(dimension_semantics=(_ARB,) * n_axes, vmem_limit_bytes=VMEM_LIMIT)


def _pick(n, prefs):
    for c in prefs:
        if n % c == 0:
            return c
    return n


_DN = {"nn": (((1,), (0,)), ((), ())), "nt": (((1,), (1,)), ((), ())), "tn": (((0,), (0,)), ((), ()))}


def _mm(a, b, mode, out_dtype, name):
    if mode == "nn":
        (m, k), (_, n) = a.shape, b.shape
    elif mode == "nt":
        (m, k), (n, _) = a.shape, b.shape
    else:
        (k, m), (_, n) = a.shape, b.shape
    big = (1408, 1024, 640, 512, 384, 256, 128)
    tm = _pick(m, big) if mode == "tn" else _pick(m, (1024, 512, 256, 128))
    tn = _pick(n, big)
    if mode == "tn":
        tk = _pick(k, (2048, 1024, 512, 256, 128))
    else:
        tk = k if k <= 3200 else _pick(k, (2816, 2048, 1024, 512))
    nk = k // tk
    dn = _DN[mode]

    def body(a_ref, b_ref, o_ref, acc_ref):
        p = lax.dot_general(a_ref[...], b_ref[...], dn, preferred_element_type=f32)
        if nk == 1:
            o_ref[...] = p.astype(out_dtype)
        else:
            kk = pl.program_id(2)

            @pl.when(kk == 0)
            def _():
                acc_ref[...] = p

            @pl.when(kk > 0)
            def _():
                acc_ref[...] += p

            @pl.when(kk == nk - 1)
            def _():
                o_ref[...] = acc_ref[...].astype(out_dtype)

    if mode == "nn":
        a_spec = pl.BlockSpec((tm, tk), lambda i, j, kk: (i, kk))
        b_spec = pl.BlockSpec((tk, tn), lambda i, j, kk: (kk, j))
    elif mode == "nt":
        a_spec = pl.BlockSpec((tm, tk), lambda i, j, kk: (i, kk))
        b_spec = pl.BlockSpec((tn, tk), lambda i, j, kk: (j, kk))
    else:
        a_spec = pl.BlockSpec((tk, tm), lambda i, j, kk: (kk, i))
        b_spec = pl.BlockSpec((tk, tn), lambda i, j, kk: (kk, j))
    acc_shape = (tm, tn) if nk > 1 else (8, 128)
    return pl.pallas_call(
        body, name=name, grid=(m // tm, n // tn, nk),
        in_specs=[a_spec, b_spec], out_specs=pl.BlockSpec((tm, tn), lambda i, j, kk: (i, j)),
        out_shape=jax.ShapeDtypeStruct((m, n), out_dtype),
        scratch_shapes=[pltpu.VMEM(acc_shape, f32)],
        compiler_params=_cp(3),
    )(a, b)


def _mmx(a, b, mode, name, grid, a_spec, b_spec, o_spec, out_shape, into=None):
    nk = grid[2]
    dn = _DN[mode]
    out_dtype = out_shape.dtype
    a_blk = (math.prod(a_spec.block_shape[:-1]), a_spec.block_shape[-1])
    b_blk = (math.prod(b_spec.block_shape[:-1]), b_spec.block_shape[-1])
    o_blk = (math.prod(o_spec.block_shape[:-1]), o_spec.block_shape[-1])

    def body(a_ref, b_ref, *rest):
        o_ref, acc_ref = rest[-2:]
        p = lax.dot_general(a_ref[...].reshape(a_blk), b_ref[...].reshape(b_blk), dn, preferred_element_type=f32)
        if nk == 1:
            o_ref[...] = p.reshape(o_ref.shape).astype(out_dtype)
        else:
            kk = pl.program_id(2)

            @pl.when(kk == 0)
            def _():
                acc_ref[...] = p

            @pl.when(kk > 0)
            def _():
                acc_ref[...] += p

            @pl.when(kk == nk - 1)
            def _():
                o_ref[...] = acc_ref[...].reshape(o_ref.shape).astype(out_dtype)

    extra = {} if into is None else dict(input_output_aliases={2: 0})
    operands = (a, b) if into is None else (a, b, into)
    return pl.pallas_call(
        body, name=name, grid=grid, in_specs=[a_spec, b_spec] + ([] if into is None else [_ANY]),
        out_specs=o_spec, out_shape=out_shape,
        scratch_shapes=[pltpu.VMEM(o_blk if nk > 1 else (8, 128), f32)], compiler_params=_cp(3), **extra,
    )(*operands)


def _tok_tile(t):
    return min(t, 512)


def _tok_spec(tm, c, nt=None, reverse=False):
    if reverse:
        return pl.BlockSpec((tm, c), lambda i: (nt - 1 - i, 0))
    return pl.BlockSpec((tm, c), lambda i: (i, 0))


def _row_spec(c, rows=1):
    return pl.BlockSpec((rows, c), lambda i: (0, 0))


def _acc_add(first, ref, val):
    @pl.when(first)
    def _():
        ref[...] = val

    @pl.when(jnp.logical_not(first))
    def _():
        ref[...] += val


def _adaln(x, g, scale, shift):
    y = x * lax.rsqrt(jnp.mean(x * x, axis=-1, keepdims=True) + NORM_EPS)
    return (y * g) * (1.0 + scale) + shift


def _ln_fwd(x, g, scale, shift):
    t = x.shape[0]
    tm = _tok_tile(t)

    def body(x_ref, g_ref, sc_ref, sh_ref, h_ref):
        h_ref[...] = _adaln(x_ref[...], g_ref[...], sc_ref[...], sh_ref[...]).astype(bf16)

    return pl.pallas_call(
        body, name="ln_fwd", grid=(t // tm,),
        in_specs=[_tok_spec(tm, D), _row_spec(D), _row_spec(D), _row_spec(D)],
        out_specs=_tok_spec(tm, D), out_shape=jax.ShapeDtypeStruct((t, D), bf16),
        compiler_params=_cp(1),
    )(x, g, scale, shift)


def _ln_bwd(x, g, scale, shift, dh, dxn):
    t = x.shape[0]
    tm = _tok_tile(t)

    def body(x_ref, g_ref, sc_ref, sh_ref, dh_ref, dxn_ref, dx_ref, dg_ref, dsc_ref, dsh_ref):
        _, vjp = jax.vjp(_adaln, x_ref[...], g_ref[...], sc_ref[...], sh_ref[...])
        dx, dg, dsc, dsh = vjp(dh_ref[...])
        dx_ref[...] = dxn_ref[...] + dx
        first = pl.program_id(0) == 0
        _acc_add(first, dg_ref, dg)
        _acc_add(first, dsc_ref, dsc)
        _acc_add(first, dsh_ref, dsh)

    row = jax.ShapeDtypeStruct((1, D), f32)
    return pl.pallas_call(
        body, name="ln_bwd", grid=(t // tm,),
        in_specs=[_tok_spec(tm, D), _row_spec(D), _row_spec(D), _row_spec(D), _tok_spec(tm, D), _tok_spec(tm, D)],
        out_specs=[_tok_spec(tm, D), _row_spec(D), _row_spec(D), _row_spec(D)],
        out_shape=[jax.ShapeDtypeStruct((t, D), f32), row, row, row],
        compiler_params=_cp(1),
    )(x, g, scale, shift, dh, dxn)


def _swiglu(g, u):
    return jax.nn.silu(g) * u


def _ffn_tile(t):
    return min(t, 512)


def _ffn_specs(tm, idx, rows):
    once = pl.Buffered(1)
    return (pl.BlockSpec((2, FFN_SLABS, 1, D, FFN_HS), lambda i, q: (0, 0, idx, 0, 0), pipeline_mode=once),
            pl.BlockSpec((N_DEV, 1, rows, D), lambda i, q: (0, idx, 0, 0), pipeline_mode=once),
            pl.BlockSpec((2, 1, tm, FFN_HS), lambda i, q: (0, q, i, 0)),
            pl.BlockSpec((tm, D), lambda i, q: (i, 0)),
            pl.BlockSpec((1, D), lambda i, q: (0, 0)))


def _out_rows(wo_ref, q):
    return wo_ref[pl.ds(2 * q, 2), 0].reshape(FFN_HS, D)


def _ffn_up(x, h, gate, w_in, w_out, idx, coef):
    t = h.shape[0]
    tm = _ffn_tile(t)
    rows = w_out.shape[2]

    def body(h_ref, wi_ref, wo_ref, x_ref, gt_ref, gu_ref, a_ref, o_ref, xn_ref):
        q = pl.program_id(1)
        hv = h_ref[...]
        g = jnp.dot(hv, wi_ref[0, q, 0], preferred_element_type=f32)
        u = jnp.dot(hv, wi_ref[1, q, 0], preferred_element_type=f32)
        gu_ref[0, 0] = g.astype(bf16)
        gu_ref[1, 0] = u.astype(bf16)
        a = _swiglu(g, u).astype(bf16)
        a_ref[0] = a
        _acc_add(q == 0, o_ref, jnp.dot(a, _out_rows(wo_ref, q), preferred_element_type=f32))

        @pl.when(q == FFN_SLABS - 1)
        def _():
            xn_ref[...] = x_ref[...] + (coef * (1.0 + gt_ref[...])) * o_ref[...]

    wi, wo, slab, tok, row = _ffn_specs(tm, idx, rows)
    tok_f32 = jax.ShapeDtypeStruct((t, D), f32)
    return pl.pallas_call(
        body, name="ffn_up", grid=(t // tm, FFN_SLABS), in_specs=[tok, wi, wo, tok, row],
        out_specs=[slab, pl.BlockSpec((1, tm, FFN_HS), lambda i, q: (q, i, 0)), tok, tok],
        out_shape=[jax.ShapeDtypeStruct((2, FFN_SLABS, t, FFN_HS), bf16),
                   jax.ShapeDtypeStruct((FFN_SLABS, t, FFN_HS), bf16), tok_f32, tok_f32],
        compiler_params=_cp(2),
    )(h, w_in, w_out, x, gate)


def _ffn_dgu(do, w_out, idx, gu):
    t = do.shape[0]
    tm = _mm_tile(t)

    def body(do_ref, wo_ref, gu_ref, o_ref):
        da = lax.dot_general(do_ref[...], _out_rows(wo_ref, pl.program_id(1)), _DN["nt"], preferred_element_type=f32)
        _, vjp = jax.vjp(_swiglu, gu_ref[0, 0].astype(f32), gu_ref[1, 0].astype(f32))
        dg, du = vjp(da)
        o_ref[0, 0] = dg.astype(bf16)
        o_ref[1, 0] = du.astype(bf16)

    _, wo, slab, tok, _ = _ffn_specs(tm, idx, w_out.shape[2])
    return pl.pallas_call(
        body, name="ffn_dgu", grid=(t // tm, FFN_SLABS), in_specs=[tok, wo, slab],
        out_specs=slab, out_shape=jax.ShapeDtypeStruct((2, FFN_SLABS, t, FFN_HS), bf16),
        compiler_params=_cp(2),
    )(do, w_out, gu)


def _ffn_dh(dgu, w_in, idx):
    t = dgu.shape[2]
    tm = _mm_tile(t)

    def body(d_ref, wi_ref, o_ref):
        q = pl.program_id(1)
        p = (lax.dot_general(d_ref[0, 0], wi_ref[0, q, 0], _DN["nt"], preferred_element_type=f32)
             + lax.dot_general(d_ref[1, 0], wi_ref[1, q, 0], _DN["nt"], preferred_element_type=f32))
        _acc_add(pl.program_id(1) == 0, o_ref, p)

    wi, _, slab, tok, _ = _ffn_specs(tm, idx, 8)
    return pl.pallas_call(
        body, name="ffn_dh", grid=(t // tm, FFN_SLABS), in_specs=[slab, wi], out_specs=tok,
        out_shape=jax.ShapeDtypeStruct((t, D), f32), compiler_params=_cp(2),
    )(dgu, w_in)


def _resid_fwd(x, o, gate, coef):
    t = x.shape[0]
    tm = _tok_tile(t)

    def body(x_ref, o_ref, gt_ref, y_ref):
        y_ref[...] = x_ref[...] + (coef * (1.0 + gt_ref[...])) * o_ref[...]

    return pl.pallas_call(
        body, name="resid_fwd", grid=(t // tm,),
        in_specs=[_tok_spec(tm, D), _tok_spec(tm, D), _row_spec(D)],
        out_specs=_tok_spec(tm, D), out_shape=jax.ShapeDtypeStruct((t, D), f32),
        compiler_params=_cp(1),
    )(x, o, gate)


def _resid_bwd(dxn, o, gate, coef):
    t = dxn.shape[0]
    tm = _tok_tile(t)

    def body(dxn_ref, o_ref, gt_ref, do_ref, dgt_ref):
        dxn_v = dxn_ref[...]
        do_ref[...] = ((coef * (1.0 + gt_ref[...])) * dxn_v).astype(bf16)
        _acc_add(pl.program_id(0) == 0, dgt_ref, coef * jnp.sum(dxn_v * o_ref[...], axis=0, keepdims=True))

    return pl.pallas_call(
        body, name="resid_bwd", grid=(t // tm,),
        in_specs=[_tok_spec(tm, D), _tok_spec(tm, D), _row_spec(D)],
        out_specs=[_tok_spec(tm, D), _row_spec(D)],
        out_shape=[jax.ShapeDtypeStruct((t, D), bf16), jax.ShapeDtypeStruct((1, D), f32)],
        compiler_params=_cp(1),
    )(dxn, o, gate)


def _loss_head(y, target):
    t = y.shape[0]
    tm = _tok_tile(t)

    def body(y_ref, t_ref, dy_ref, l_ref):
        err = y_ref[...] - t_ref[...]
        dy_ref[...] = err * (1.0 / D)
        part = jnp.sum(jnp.sum(err * err, axis=0, keepdims=True), axis=1, keepdims=True) * (0.5 / D)
        _acc_add(pl.program_id(0) == 0, l_ref, jnp.broadcast_to(part, (1, 128)))

    return pl.pallas_call(
        body, name="loss_head", grid=(t // tm,),
        in_specs=[_tok_spec(tm, D), _tok_spec(tm, D)],
        out_specs=[_tok_spec(tm, D), _row_spec(128)],
        out_shape=[jax.ShapeDtypeStruct((t, D), f32), jax.ShapeDtypeStruct((1, 128), f32)],
        compiler_params=_cp(1),
    )(y, target)


def _mm_tile(t):
    return min(t, 1024)


def _ffn_fwd(x, ln, w_in, w_out, idx, coef):
    g, shift, scale, gate = ln
    h = _ln_fwd(x, g, scale, shift)
    gu, a, o, xn = _ffn_up(x, h, gate, w_in.reshape((2, FFN_SLABS) + w_in.shape[1:]), w_out, idx, coef)
    return xn, (x, h, gu, a, o)


def _ffn_bwd(dxn, saved, ln, w_in, w_out, idx, coef, dw_in, dw_out):
    x, h, gu, a, o = saved
    g, shift, scale, gate = ln
    t = x.shape[0]
    tm = _mm_tile(t)
    tk = _pick(t, (2048, 1024, 512, 256, 128))
    rows = w_out.shape[2]
    sds = jax.ShapeDtypeStruct
    do, dgate = _resid_bwd(dxn, o, gate, coef)
    dgu = _ffn_dgu(do, w_out, idx, gu)
    dh = _ffn_dh(dgu, w_in.reshape((2, FFN_SLABS) + w_in.shape[1:]), idx)
    dgu = dgu.reshape(N_DEV, t, FFN_HS)
    dw_out = _mmx(a, do, "tn", "ffn_dwout", (FFN_SLABS, 1, t // tk),
                  pl.BlockSpec((1, tk, FFN_HS), lambda i, j, k: (i, k, 0)),
                  pl.BlockSpec((tk, D), lambda i, j, k: (k, 0)),
                  pl.BlockSpec((2, 1, rows, D), lambda i, j, k: (i, idx, 0, 0)),
                  sds(dw_out.shape, f32), into=dw_out)
    dw_in = _mmx(h, dgu, "tn", "ffn_dwin", (N_DEV, 1, t // tk),
                 pl.BlockSpec((tk, D), lambda i, j, k: (k, 0)),
                 pl.BlockSpec((1, tk, FFN_HS), lambda i, j, k: (i, k, 0)),
                 pl.BlockSpec((1, 1, D, FFN_HS), lambda i, j, k: (i, idx, 0, 0)),
                 sds(dw_in.shape, f32), into=dw_in)
    dx, dg, dscale, dshift = _ln_bwd(x, g, scale, shift, dh, dxn)
    return dx, (dg, dshift, dscale, dgate), dw_in, dw_out


def _roll_rows(a, k):
    n = a.shape[0]
    return pltpu.roll(a, k % n, 0)


def _pool_windows(hx, first_row, reverse):
    outs = []
    for gi, w in enumerate(POOL_WINDOWS):
        acc = hx[:, gi * POOL_GROUP:(gi + 1) * POOL_GROUP]
        k = 1
        while k < w:
            acc = acc + _roll_rows(acc, -k if reverse else k)
            k *= 2
        outs.append(acc)
    return outs


def _pool_cnt(t_idx, w):
    return jnp.minimum(t_idx + 1, w).astype(f32)


def _pool_pooled(x_ref, xp_ref, g, scale, shift, i, tm):
    h = _adaln(x_ref[...], g, scale, shift)
    hp = _adaln(xp_ref[...], g, scale, shift)
    hp = jnp.where(i == 0, 0.0, hp)
    hx = jnp.concatenate([hp, h], axis=0)
    sums = _pool_windows(hx, 0, False)
    t_idx = i * tm + lax.broadcasted_iota(jnp.int32, (tm, 1), 0)
    pooled = []
    for gi, w in enumerate(POOL_WINDOWS):
        s = sums[gi][POOL_HALO:, :]
        pooled.append(s / _pool_cnt(t_idx, w) - h[:, gi * POOL_GROUP:(gi + 1) * POOL_GROUP])
    return h, pooled


def _pool_specs(t, tm):
    per = tm // POOL_HALO
    prev = pl.BlockSpec((POOL_HALO, D), lambda i: (jnp.maximum(i * per - 1, 0), 0))
    return [_tok_spec(tm, D), prev, _row_spec(D), _row_spec(D), _row_spec(D),
            pl.BlockSpec((4, POOL_GROUP, POOL_GROUP), lambda i: (0, 0, 0)), _row_spec(D), _row_spec(D)]


def _pool_fwd(x, ln, w, pscale):
    g, shift, scale, gate = ln
    t = x.shape[0]
    tm = _tok_tile(t)

    def body(x_ref, xp_ref, g_ref, sc_ref, sh_ref, w_ref, ps_ref, gt_ref, y_ref):
        i = pl.program_id(0)
        _, pooled = _pool_pooled(x_ref, xp_ref, g_ref[...], sc_ref[...], sh_ref[...], i, tm)
        mixed = [jnp.dot(pooled[gi].astype(bf16), w_ref[gi], preferred_element_type=f32) for gi in range(4)]
        y = jnp.concatenate(mixed, axis=1) * ps_ref[...]
        y_ref[...] = x_ref[...] + (1.0 + gt_ref[...]) * y

    return pl.pallas_call(
        body, name="pool_fwd", grid=(t // tm,), in_specs=_pool_specs(t, tm),
        out_specs=_tok_spec(tm, D), out_shape=jax.ShapeDtypeStruct((t, D), f32),
        compiler_params=_cp(1),
    )(x, x, g, scale, shift, w, pscale, gate)


def _pool_bwd(dxn, x, ln, w, pscale):
    g, shift, scale, gate = ln
    t = x.shape[0]
    tm = _tok_tile(t)
    nt = t // tm
    per = tm // POOL_HALO

    def body_a(x_ref, xp_ref, g_ref, sc_ref, sh_ref, w_ref, ps_ref, gt_ref, dxn_ref,
               dp_ref, dw_ref, dps_ref, dgt_ref):
        i = pl.program_id(0)
        first = i == 0
        _, pooled = _pool_pooled(x_ref, xp_ref, g_ref[...], sc_ref[...], sh_ref[...], i, tm)
        dxn_v = dxn_ref[...]
        dy = (1.0 + gt_ref[...]) * dxn_v
        dmixed = dy * ps_ref[...]
        mixed, dps = [], []
        for gi in range(4):
            sl = slice(gi * POOL_GROUP, (gi + 1) * POOL_GROUP)
            pb = pooled[gi].astype(bf16)
            dmb = dmixed[:, sl].astype(bf16)
            mixed.append(jnp.dot(pb, w_ref[gi], preferred_element_type=f32))
            dp_ref[:, sl] = lax.dot_general(dmb, w_ref[gi], _DN["nt"], preferred_element_type=f32)
            dwg = lax.dot_general(pb, dmb, _DN["tn"], preferred_element_type=f32)

            @pl.when(first)
            def _():
                dw_ref[gi] = dwg

            @pl.when(jnp.logical_not(first))
            def _():
                dw_ref[gi] += dwg
        mixed = jnp.concatenate(mixed, axis=1)
        _acc_add(first, dps_ref, jnp.sum(dy * mixed, axis=0, keepdims=True))
        _acc_add(first, dgt_ref, jnp.sum(dxn_v * (mixed * ps_ref[...]), axis=0, keepdims=True))

    row = jax.ShapeDtypeStruct((1, D), f32)
    dpooled, dw, dps, dgate = pl.pallas_call(
        body_a, name="pool_bwd_a", grid=(nt,), in_specs=_pool_specs(t, tm) + [_tok_spec(tm, D)],
        out_specs=[_tok_spec(tm, D), pl.BlockSpec((4, POOL_GROUP, POOL_GROUP), lambda i: (0, 0, 0)),
                   _row_spec(D), _row_spec(D)],
        out_shape=[jax.ShapeDtypeStruct((t, D), f32), jax.ShapeDtypeStruct((4, POOL_GROUP, POOL_GROUP), f32), row, row],
        compiler_params=_cp(1),
    )(x, x, g, scale, shift, w, pscale, gate, dxn)

    def body_b(dp_ref, dpn_ref, x_ref, g_ref, sc_ref, sh_ref, dxn_ref, dx_ref, dg_ref, dsc_ref, dsh_ref):
        i = pl.program_id(0)
        dp = dp_ref[...]
        dpn = jnp.where(i == nt - 1, 0.0, dpn_ref[...])
        t_idx = i * tm + lax.broadcasted_iota(jnp.int32, (tm + POOL_HALO, 1), 0)
        ex = jnp.concatenate([dp, dpn], axis=0)
        parts = []
        for gi, w_ in enumerate(POOL_WINDOWS):
            parts.append(ex[:, gi * POOL_GROUP:(gi + 1) * POOL_GROUP] / _pool_cnt(t_idx, w_))
        sums = _pool_windows(jnp.concatenate(parts, axis=1), 0, True)
        dh = jnp.concatenate([s[:tm, :] for s in sums], axis=1) - dp
        _, vjp = jax.vjp(_adaln, x_ref[...], g_ref[...], sc_ref[...], sh_ref[...])
        dx, dg, dsc, dsh = vjp(dh)
        dx_ref[...] = dxn_ref[...] + dx
        first = i == 0
        _acc_add(first, dg_ref, dg)
        _acc_add(first, dsc_ref, dsc)
        _acc_add(first, dsh_ref, dsh)

    nxt = pl.BlockSpec((POOL_HALO, D), lambda i: (jnp.minimum((i + 1) * per, t // POOL_HALO - 1), 0))
    dx, dg, dscale, dshift = pl.pallas_call(
        body_b, name="pool_bwd_b", grid=(nt,),
        in_specs=[_tok_spec(tm, D), nxt, _tok_spec(tm, D), _row_spec(D), _row_spec(D), _row_spec(D), _tok_spec(tm, D)],
        out_specs=[_tok_spec(tm, D), _row_spec(D), _row_spec(D), _row_spec(D)],
        out_shape=[jax.ShapeDtypeStruct((t, D), f32), row, row, row],
        compiler_params=_cp(1),
    )(dpooled, dpooled, x, g, scale, shift, dxn)
    return dx, (dg, dshift, dscale, dgate), dw, dps


def _conv_taps(czx, cw):
    return cw[0:1, :] * _roll_rows(czx, 2) + cw[1:2, :] * _roll_rows(czx, 1) + cw[2:3, :] * czx


def _conv_fwd(p, cw):
    t = p.shape[0]
    tm = _tok_tile(t)
    per = tm // CONV_HALO

    def body(p_ref, pp_ref, cw_ref, q_ref):
        i = pl.program_id(0)
        cz = p_ref[:, D:2 * D] * p_ref[:, 2 * D:3 * D]
        czp = jnp.where(i == 0, 0.0, pp_ref[:, D:2 * D] * pp_ref[:, 2 * D:3 * D])
        conv = _conv_taps(jnp.concatenate([czp, cz], axis=0), cw_ref[...])[CONV_HALO:, :]
        q_ref[...] = (p_ref[:, 0:D] * conv).astype(bf16)

    prev = pl.BlockSpec((CONV_HALO, 3 * D), lambda i: (jnp.maximum(i * per - 1, 0), 0))
    return pl.pallas_call(
        body, name="conv_fwd", grid=(t // tm,),
        in_specs=[_tok_spec(tm, 3 * D), prev, _row_spec(D, 8)],
        out_specs=_tok_spec(tm, D), out_shape=jax.ShapeDtypeStruct((t, D), bf16),
        compiler_params=_cp(1),
    )(p, p, cw)


def _conv_bwd(p, cw, dq):
    t = p.shape[0]
    tm = _tok_tile(t)
    nt = t // tm
    per = tm // CONV_HALO

    def body(p_ref, pp_ref, pn_ref, cw_ref, dq_ref, dqn_ref, dp_ref, dcw_ref):
        i = pl.program_id(0)
        cw_v = cw_ref[...]
        b, c, z = p_ref[:, 0:D], p_ref[:, D:2 * D], p_ref[:, 2 * D:3 * D]
        cz = c * z
        czp = jnp.where(i == 0, 0.0, pp_ref[:, D:2 * D] * pp_ref[:, 2 * D:3 * D])
        czx = jnp.concatenate([czp, cz], axis=0)
        conv = _conv_taps(czx, cw_v)[CONV_HALO:, :]
        dq_v = dq_ref[...]
        dconv = dq_v * b
        dconv_n = jnp.where(i == nt - 1, 0.0, dqn_ref[...] * pn_ref[:, 0:D])
        dcx = jnp.concatenate([dconv, dconv_n], axis=0)
        dcz = (cw_v[2:3, :] * dcx + cw_v[1:2, :] * _roll_rows(dcx, -1) + cw_v[0:1, :] * _roll_rows(dcx, -2))[:tm, :]
        dp_ref[:, 0:D] = (dq_v * conv).astype(bf16)
        dp_ref[:, D:2 * D] = (dcz * z).astype(bf16)
        dp_ref[:, 2 * D:3 * D] = (dcz * c).astype(bf16)
        dw2 = jnp.sum(dconv * cz, axis=0, keepdims=True)
        dw1 = jnp.sum(dconv * _roll_rows(czx, 1)[CONV_HALO:, :], axis=0, keepdims=True)
        dw0 = jnp.sum(dconv * _roll_rows(czx, 2)[CONV_HALO:, :], axis=0, keepdims=True)
        _acc_add(i == 0, dcw_ref, jnp.concatenate([dw0, dw1, dw2, jnp.zeros((5, D), f32)], axis=0))

    prev = pl.BlockSpec((CONV_HALO, 3 * D), lambda i: (jnp.maximum(i * per - 1, 0), 0))
    last = t // CONV_HALO - 1
    nxt3 = pl.BlockSpec((CONV_HALO, 3 * D), lambda i: (jnp.minimum((i + 1) * per, last), 0))
    nxt1 = pl.BlockSpec((CONV_HALO, D), lambda i: (jnp.minimum((i + 1) * per, last), 0))
    return pl.pallas_call(
        body, name="conv_bwd", grid=(nt,),
        in_specs=[_tok_spec(tm, 3 * D), prev, nxt3, _row_spec(D, 8), _tok_spec(tm, D), nxt1],
        out_specs=[_tok_spec(tm, 3 * D), _row_spec(D, 8)],
        out_shape=[jax.ShapeDtypeStruct((t, 3 * D), bf16), jax.ShapeDtypeStruct((8, D), f32)],
        compiler_params=_cp(1),
    )(p, p, p, cw, dq, dq)


def _conv_tile(t):
    return min(t, 2048)


def _convmix_fwd(x, ln, w_in, cw, w_out):
    g, shift, scale, gate = ln
    t = x.shape[0]
    tm = _conv_tile(t)
    cs = w_in.shape[2]
    h = _ln_fwd(x, g, scale, shift)
    p = _mmx(h, w_in, "nn", "conv_in", (t // tm, N_DEV, 1),
             pl.BlockSpec((tm, D), lambda i, j, k: (i, 0)),
             pl.BlockSpec((1, D, cs), lambda i, j, k: (j, 0, 0)),
             pl.BlockSpec((tm, cs), lambda i, j, k: (i, j)), jax.ShapeDtypeStruct((t, N_DEV * cs), f32))
    q = _conv_fwd(p, cw)
    y = _mm(q, w_out, "nn", f32, "conv_out")
    return _resid_fwd(x, y, gate, 1.0), (x, h, p, q, y)


def _convmix_bwd(dxn, saved, ln, w_in, cw, w_out):
    x, h, p, q, y = saved
    g, shift, scale, gate = ln
    dy, dgate = _resid_bwd(dxn, y, gate, 1.0)
    dq = _mm(dy, w_out, "nt", f32, "conv_dq")
    dw_out = _mm(q, dy, "tn", f32, "conv_dwout")
    dp, dcw = _conv_bwd(p, cw, dq)
    t = x.shape[0]
    tm = _conv_tile(t)
    tk = _pick(t, (2048, 1024, 512, 256, 128))
    cs = w_in.shape[2]
    dh = _mmx(dp, w_in, "nt", "conv_dh", (t // tm, 1, N_DEV),
              pl.BlockSpec((tm, cs), lambda i, j, k: (i, k)),
              pl.BlockSpec((1, D, cs), lambda i, j, k: (k, 0, 0)),
              pl.BlockSpec((tm, D), lambda i, j, k: (i, 0)), jax.ShapeDtypeStruct((t, D), f32))
    dw_in = _mmx(h, dp, "tn", "conv_dwin", (N_DEV, 1, t // tk),
                 pl.BlockSpec((tk, D), lambda i, j, k: (k, 0)),
                 pl.BlockSpec((tk, cs), lambda i, j, k: (k, i)),
                 pl.BlockSpec((1, D, cs), lambda i, j, k: (i, 0, 0)), jax.ShapeDtypeStruct((N_DEV, D, cs), f32))
    dx, dg, dscale, dshift = _ln_bwd(x, g, scale, shift, dh, dxn)
    return dx, (dg, dshift, dscale, dgate), dw_in, dcw[0:3], dw_out


def _exact_dot(tri, v):
    v1 = v.astype(bf16)
    r1 = v - v1.astype(f32)
    v2 = r1.astype(bf16)
    v3 = (r1 - v2.astype(f32)).astype(bf16)
    d = lambda p: jnp.dot(tri, p, preferred_element_type=f32)
    return d(v1) + d(v2) + d(v3)


def _fox_cumf(fl, b_f):
    t = fl.shape[0]
    tc = min(t, 256)

    def body(fl_ref, b_ref, f_ref, carry_ref):
        i = pl.program_id(0)

        @pl.when(i == 0)
        def _():
            carry_ref[...] = jnp.zeros_like(carry_ref)

        lf = jax.nn.log_sigmoid(fl_ref[...] + b_ref[...])
        r = lax.broadcasted_iota(jnp.int32, (tc, tc), 0)
        c = lax.broadcasted_iota(jnp.int32, (tc, tc), 1)
        tri = (r >= c).astype(bf16)
        fc = _exact_dot(tri, lf) + carry_ref[0:1, :]
        f_ref[...] = fc
        carry_ref[0:1, :] = fc[tc - 1:tc, :]

    return pl.pallas_call(
        body, name="fox_cumf", grid=(t // tc,),
        in_specs=[_tok_spec(tc, 128), _row_spec(128)],
        out_specs=_tok_spec(tc, 128), out_shape=jax.ShapeDtypeStruct((t, 128), f32),
        scratch_shapes=[pltpu.VMEM((8, 128), f32)], compiler_params=_cp(1),
    )(fl, b_f)


def _fox_cumf_bwd(df, fl, b_f):
    t = fl.shape[0]
    tc = min(t, 256)
    nt = t // tc

    def body(df_ref, fl_ref, b_ref, dfl_ref, db_ref, carry_ref):
        i = pl.program_id(0)

        @pl.when(i == 0)
        def _():
            carry_ref[...] = jnp.zeros_like(carry_ref)

        r = lax.broadcasted_iota(jnp.int32, (tc, tc), 0)
        c = lax.broadcasted_iota(jnp.int32, (tc, tc), 1)
        tri = (r <= c).astype(bf16)
        dlf = _exact_dot(tri, df_ref[...]) + carry_ref[0:1, :]
        carry_ref[0:1, :] = dlf[0:1, :]
        dfl = dlf * jax.nn.sigmoid(-(fl_ref[...] + b_ref[...]))
        dfl_ref[...] = dfl
        _acc_add(i == 0, db_ref, jnp.sum(dfl, axis=0, keepdims=True))

    return pl.pallas_call(
        body, name="fox_cumf_bwd", grid=(nt,),
        in_specs=[_tok_spec(tc, 128, nt, True), _tok_spec(tc, 128, nt, True), _row_spec(128)],
        out_specs=[_tok_spec(tc, 128, nt, True), _row_spec(128)],
        out_shape=[jax.ShapeDtypeStruct((t, 128), f32), jax.ShapeDtypeStruct((1, 128), f32)],
        scratch_shapes=[pltpu.VMEM((8, 128), f32)], compiler_params=_cp(1),
    )(df, fl, b_f)


def _fox_block(t):
    return min(t, 512)


FOX_STRIP = 64


def _fox_mask(s, row0):
    r = row0 + lax.broadcasted_iota(jnp.int32, s.shape, 0)
    c = lax.broadcasted_iota(jnp.int32, s.shape, 1)
    return jnp.where(r >= c, s, -jnp.inf)


def _strips(tb):
    ts = min(tb, FOX_STRIP)
    return [(r * ts, slice(r * ts, (r + 1) * ts)) for r in range(tb // ts)]


FOX_PAIRS = FOX_HEADS // 2
FOX_PW = 2 * FOX_HEAD_DIM


def _half(shape, hh):
    lane = lax.broadcasted_iota(jnp.int32, shape, len(shape) - 1)
    return lane < FOX_HEAD_DIM if hh == 0 else lane >= FOX_HEAD_DIM


def _by_half(v0, v1):
    return jnp.where(_half(v0.shape, 0), v0, v1)


def _half_sums(v):
    first = _half(v.shape, 0)
    s0 = jnp.sum(jnp.where(first, v, 0.0), axis=-1, keepdims=True)
    s1 = jnp.sum(jnp.where(first, 0.0, v), axis=-1, keepdims=True)
    return jnp.where(first, s0, s1)


def _pair_norm(v, gain, mult):
    ms = _half_sums(v * v) * (1.0 / FOX_HEAD_DIM)
    return v * lax.rsqrt(ms + NORM_EPS) * gain * mult


def _pair_cols(p):
    return slice(p * FOX_PW, (p + 1) * FOX_PW)


def _pair_qknorm(proj, q_gain, k_gain):
    t = proj.shape[0]
    tm = _tok_tile(t)

    def body(q_ref, k_ref, v_ref, qg_ref, kg_ref, o_ref):
        for p in range(FOX_PAIRS):
            cb = _pair_cols(p)
            o_ref[:, cb] = _pair_norm(q_ref[:, cb], qg_ref[...], FOX_HEAD_DIM ** -0.5).astype(bf16)
            o_ref[:, D + p * FOX_PW:D + (p + 1) * FOX_PW] = _pair_norm(k_ref[:, cb], kg_ref[...], 1.0).astype(bf16)
        o_ref[:, 2 * D:3 * D] = v_ref[...].astype(bf16)

    part = lambda s: pl.BlockSpec((tm, D), lambda i: (i, s))
    return pl.pallas_call(
        body, name="fox_qknorm", grid=(t // tm,),
        in_specs=[part(0), part(1), part(2), _row_spec(FOX_PW), _row_spec(FOX_PW)],
        out_specs=_tok_spec(tm, 3 * D), out_shape=jax.ShapeDtypeStruct((t, 3 * D), bf16), compiler_params=_cp(1),
    )(proj, proj, proj, q_gain, k_gain)


def _pair_qknorm_bwd(proj, q_gain, k_gain, dqn, dkn):
    t = proj.shape[0]
    tm = _tok_tile(t)

    def body(q_ref, k_ref, qg_ref, kg_ref, dqn_ref, dkn_ref, o_ref, dqg_ref, dkg_ref):
        dqg = jnp.zeros((1, FOX_PW), f32)
        dkg = jnp.zeros((1, FOX_PW), f32)
        for p in range(FOX_PAIRS):
            cb = _pair_cols(p)
            _, vq = jax.vjp(lambda v, gn: _pair_norm(v, gn, FOX_HEAD_DIM ** -0.5), q_ref[:, cb], qg_ref[...])
            dq, g1 = vq(dqn_ref[:, cb])
            _, vk = jax.vjp(lambda v, gn: _pair_norm(v, gn, 1.0), k_ref[:, cb], kg_ref[...])
            dk, g2 = vk(dkn_ref[:, cb])
            o_ref[:, cb] = dq.astype(bf16)
            o_ref[:, D + p * FOX_PW:D + (p + 1) * FOX_PW] = dk.astype(bf16)
            dqg, dkg = dqg + g1, dkg + g2
        first = pl.program_id(0) == 0
        _acc_add(first, dqg_ref, dqg + pltpu.roll(dqg, FOX_HEAD_DIM, 1))
        _acc_add(first, dkg_ref, dkg + pltpu.roll(dkg, FOX_HEAD_DIM, 1))

    part = lambda s: pl.BlockSpec((tm, D), lambda i: (i, s))
    gshape = jax.ShapeDtypeStruct((1, FOX_PW), f32)
    return pl.pallas_call(
        body, name="fox_qknorm_bwd", grid=(t // tm,),
        in_specs=[part(0), part(1), _row_spec(FOX_PW), _row_spec(FOX_PW), _tok_spec(tm, D), _tok_spec(tm, D)],
        out_specs=[_tok_spec(tm, 2 * D), _row_spec(FOX_PW), _row_spec(FOX_PW)],
        out_shape=[jax.ShapeDtypeStruct((t, 2 * D), bf16), gshape, gshape], compiler_params=_cp(1),
    )(proj, proj, q_gain, k_gain, dqn, dkn)


def _pair_delta(o, do):
    t = o.shape[0]
    tm = _tok_tile(t)

    def body(o_ref, do_ref, dl_ref, dob_ref):
        dob = do_ref[...].astype(bf16)
        dob_ref[...] = dob
        prod = dob.astype(f32) * o_ref[...]
        for p in range(FOX_PAIRS):
            dl_ref[p] = _half_sums(prod[:, _pair_cols(p)])

    return pl.pallas_call(
        body, name="fox_delta", grid=(t // tm,), in_specs=[_tok_spec(tm, D), _tok_spec(tm, D)],
        out_specs=[pl.BlockSpec((FOX_PAIRS, tm, FOX_PW), lambda i: (0, i, 0)), _tok_spec(tm, D)],
        out_shape=[jax.ShapeDtypeStruct((FOX_PAIRS, t, FOX_PW), f32), jax.ShapeDtypeStruct((t, D), bf16)],
        compiler_params=_cp(1),
    )(o, do)


def _pair_attn_fwd(qkvn, fcol, fref):
    t = qkvn.shape[0]
    tb = _fox_block(t)
    nq = t // tb
    pw = FOX_PW
    kcol, vcol = D // pw, 2 * D // pw

    def body(q_ref, k_ref, v_ref, fc_ref, fr_ref, o_ref, ob_ref, lse_ref):
        i = pl.program_id(1)
        qp = q_ref[...]
        qh = [jnp.where(_half(qp.shape, hh), qp, jnp.zeros_like(qp)) for hh in range(2)]

        def step(j, carry, diag):
            off = pl.multiple_of(j * tb, tb)
            kj = k_ref[pl.ds(off, tb), :]
            vj = v_ref[pl.ds(off, tb), :]
            out = []
            for hh in range(2):
                m, l, acc = carry[hh]
                bias = fr_ref[0, hh, pl.ds(i, 1), 0:1] - fc_ref[0, hh:hh + 1, pl.ds(off, tb)]
                s = lax.dot_general(qh[hh], kj, _DN["nt"], preferred_element_type=f32) + bias
                if diag:
                    s = _fox_mask(s, 0)
                m_new = jnp.maximum(m, jnp.max(s, axis=-1, keepdims=True))
                alpha = jnp.exp(m - m_new)
                p = jnp.exp(s - m_new)
                l = alpha * l + jnp.sum(p, axis=-1, keepdims=True)
                p_hi = p.astype(bf16)
                p_lo = (p - p_hi.astype(f32)).astype(bf16)
                pv = jnp.dot(p_hi, vj, preferred_element_type=f32) + jnp.dot(p_lo, vj, preferred_element_type=f32)
                out.append((m_new, l, alpha * acc + pv))
            return tuple(out)

        one = (jnp.full((tb, 1), -jnp.inf, f32), jnp.zeros((tb, 1), f32), jnp.zeros((tb, pw), f32))
        carry = lax.fori_loop(0, i, lambda j, c: step(j, c, False), (one, one))
        (m0, l0, a0), (m1, l1, a1) = step(i, carry, True)
        o = _by_half(a0 / l0, a1 / l1)
        o_ref[...] = o
        ob_ref[...] = o.astype(bf16)
        lse_ref[0] = _by_half(jnp.broadcast_to(m0 + jnp.log(l0), (tb, pw)), jnp.broadcast_to(m1 + jnp.log(l1), (tb, pw)))

    return pl.pallas_call(
        body, name="fox_attn_fwd", grid=(FOX_PAIRS, nq),
        in_specs=[pl.BlockSpec((tb, pw), lambda p, i: (i, p)),
                  pl.BlockSpec((t, pw), lambda p, i: (0, kcol + p)),
                  pl.BlockSpec((t, pw), lambda p, i: (0, vcol + p)),
                  pl.BlockSpec((1, 2, t), lambda p, i: (p, 0, 0)),
                  pl.BlockSpec((1, 2, nq, 128), lambda p, i: (p, 0, 0, 0))],
        out_specs=[pl.BlockSpec((tb, pw), lambda p, i: (i, p)), pl.BlockSpec((tb, pw), lambda p, i: (i, p)),
                   pl.BlockSpec((1, tb, pw), lambda p, i: (p, i, 0))],
        out_shape=[jax.ShapeDtypeStruct((t, D), f32), jax.ShapeDtypeStruct((t, D), bf16),
                   jax.ShapeDtypeStruct((FOX_PAIRS, t, pw), f32)],
        compiler_params=_cp(2),
    )(qkvn, qkvn, qkvn, fcol, fref)


def _pair_attn_bwd(qkvn, fcol, fref, lse, delta, dob):
    t = qkvn.shape[0]
    tb = _fox_block(t)
    nq = t // tb
    pw = FOX_PW
    kcol, vcol = D // pw, 2 * D // pw

    def body(q_ref, k_ref, v_ref, fc_ref, fr_ref, lse_ref, dl_ref, do_ref, dk_ref, dv_ref, df_ref, dq_ref,
             s_ref, dp_ref, p_ref, ds_ref):
        j = pl.program_id(1)
        kj = k_ref[...]
        vj = v_ref[...]
        dk_ref[...] = jnp.zeros_like(dk_ref)
        dv_ref[...] = jnp.zeros_like(dv_ref)
        df_ref[...] = jnp.zeros_like(df_ref)

        @pl.when(j == 0)
        def _():
            dq_ref[...] = jnp.zeros_like(dq_ref)

        def step(i, diag):
            off = pl.multiple_of(i * tb, tb)
            qi = q_ref[pl.ds(off, tb), :]
            doi = do_ref[pl.ds(off, tb), :]
            parts = []
            for hh in range(2):
                mine = _half(qi.shape, hh)
                c0 = hh * FOX_HEAD_DIM
                s_ref[hh] = lax.dot_general(jnp.where(mine, qi, jnp.zeros_like(qi)), kj, _DN["nt"],
                                            preferred_element_type=f32)
                dp_ref[hh] = lax.dot_general(jnp.where(mine, doi, jnp.zeros_like(doi)), vj, _DN["nt"],
                                             preferred_element_type=f32)
                bias = fr_ref[0, hh, pl.ds(i, 1), 0:1] - fc_ref[0, hh:hh + 1, :]
                df = jnp.zeros((1, tb), f32)
                for row0, rs in _strips(tb):
                    rows = pl.ds(off + row0, rs.stop - rs.start)
                    s = s_ref[hh, rs, :] + bias
                    if diag:
                        s = _fox_mask(s, row0)
                    p = jnp.exp(s - lse_ref[0, rows, c0:c0 + 1])
                    p_ref[hh, rs, :] = p.astype(bf16)
                    ds = p * (dp_ref[hh, rs, :] - dl_ref[0, rows, c0:c0 + 1])
                    ds_ref[hh, rs, :] = ds.astype(bf16)
                    df = df + jnp.sum(ds, axis=0, keepdims=True)
                df_ref[0, hh:hh + 1, :] -= df
                parts.append((lax.dot_general(p_ref[hh], doi, _DN["tn"], preferred_element_type=f32),
                              lax.dot_general(ds_ref[hh], qi, _DN["tn"], preferred_element_type=f32),
                              jnp.dot(ds_ref[hh], kj, preferred_element_type=f32)))
            dv_ref[...] += _by_half(parts[0][0], parts[1][0])
            dk_ref[...] += _by_half(parts[0][1], parts[1][1])
            dq_ref[pl.ds(off, tb), :] += _by_half(parts[0][2], parts[1][2])

        def loop_body(i, carry):
            step(i, False)
            return carry

        step(j, True)
        lax.fori_loop(j + 1, nq, loop_body, 0)

    scratch = [pltpu.VMEM((2, tb, tb), f32), pltpu.VMEM((2, tb, tb), f32), pltpu.VMEM((2, tb, tb), bf16),
               pltpu.VMEM((2, tb, tb), bf16)]
    blk = lambda c0: pl.BlockSpec((tb, pw), lambda p, j: (j, c0 + p))
    full = lambda c0: pl.BlockSpec((t, pw), lambda p, j: (0, c0 + p))
    stat = pl.BlockSpec((1, t, pw), lambda p, j: (p, 0, 0))
    nat = jax.ShapeDtypeStruct((t, D), f32)
    return pl.pallas_call(
        body, name="fox_attn_bwd", grid=(FOX_PAIRS, nq),
        in_specs=[full(0), blk(kcol), blk(vcol), pl.BlockSpec((1, 2, tb), lambda p, j: (p, 0, j)),
                  pl.BlockSpec((1, 2, nq, 128), lambda p, j: (p, 0, 0, 0)), stat, stat, full(0)],
        out_specs=[blk(0), blk(0), pl.BlockSpec((1, 2, tb), lambda p, j: (p, 0, j)), full(0)],
        out_shape=[nat, nat, jax.ShapeDtypeStruct((FOX_PAIRS, 2, t), f32), nat],
        scratch_shapes=scratch, compiler_params=_cp(2),
    )(qkvn, qkvn, qkvn, fcol, fref, lse, delta, dob)


def _fox_fwd(x, ln, w_in, b_f, q_gain, k_gain, w_o):
    g, shift, scale, gate = ln
    t = x.shape[0]
    tb = _fox_block(t)
    h = _ln_fwd(x, g, scale, shift)
    proj = _mm(h, w_in, "nn", f32, "fox_in")
    fl = proj[:, 3 * D:3 * D + 128]
    fcum = _fox_cumf(fl, b_f)
    fcol = fcum[:, :FOX_HEADS].T.reshape(FOX_PAIRS, 2, t)
    fref = jnp.broadcast_to(fcol[:, :, ::tb][..., None], (FOX_PAIRS, 2, t // tb, 128))
    gains = (jnp.tile(q_gain, (1, 2)), jnp.tile(k_gain, (1, 2)))
    qkvn = _pair_qknorm(proj, *gains)
    o, ob, lse = _pair_attn_fwd(qkvn, fcol, fref)
    y = _mm(ob, w_o, "nn", f32, "fox_out")
    return _resid_fwd(x, y, gate, 1.0), (x, h, proj, fl, fcol, fref, gains, qkvn, o, lse, ob, y)


def _fox_bwd(dxn, saved, ln, w_in, b_f, w_o):
    x, h, proj, fl, fcol, fref, gains, qkvn, o, lse, ob, y = saved
    g, shift, scale, gate = ln
    t = x.shape[0]
    dy, dgate = _resid_bwd(dxn, y, gate, 1.0)
    do = _mm(dy, w_o, "nt", f32, "fox_do")
    dw_o = _mm(ob, dy, "tn", f32, "fox_dwo")
    delta, dob = _pair_delta(o, do)
    dkn, dv, dfcol, dqn = _pair_attn_bwd(qkvn, fcol, fref, lse, delta, dob)
    dqk, dqg, dkg = _pair_qknorm_bwd(proj, *gains, dqn, dkn)
    df = jnp.pad(dfcol.reshape(FOX_HEADS, t).T, ((0, 0), (0, 128 - FOX_HEADS)))
    dfl, db_f = _fox_cumf_bwd(df, fl, b_f)
    dproj = jnp.concatenate([dqk, dv.astype(bf16), dfl.astype(bf16)], axis=1)
    dh = _mm(dproj, w_in, "nt", f32, "fox_dh")
    dw_in = _mm(h, dproj, "tn", f32, "fox_dwin")
    dx, dg, dscale, dshift = _ln_bwd(x, g, scale, shift, dh, dxn)
    return (dx, (dg, dshift, dscale, dgate), dw_in, db_f, dqg[:, :FOX_HEAD_DIM], dkg[:, :FOX_HEAD_DIM], dw_o)


def _s5_disc(lam_re, lam_im, log_dt, b_re, b_im):
    dt = jnp.exp(log_dt)
    mag = jnp.exp(lam_re * dt)
    lb_re, lb_im = mag * jnp.cos(lam_im * dt), mag * jnp.sin(lam_im * dt)
    den = lam_re * lam_re + lam_im * lam_im
    nr, ni = lb_re - 1.0, lb_im
    k_re = (nr * lam_re + ni * lam_im) / den
    k_im = (ni * lam_re - nr * lam_im) / den
    return lb_re, lb_im, k_re * b_re - k_im * b_im, k_re * b_im + k_im * b_re


def _s5_prep(lam_re, lam_im, log_dt, b_re, b_im):
    def body(ar_ref, ai_ref, dt_ref, br_ref, bi_ref, lr_ref, li_ref, bbr_ref, bbi_ref):
        lr, li, bbr, bbi = _s5_disc(ar_ref[...], ai_ref[...], dt_ref[...], br_ref[...], bi_ref[...])
        lr_ref[...] = lr
        li_ref[...] = li
        bbr_ref[...] = bbr
        bbi_ref[...] = bbi

    small = jax.ShapeDtypeStruct(lam_re.shape, f32)
    bigs = jax.ShapeDtypeStruct(b_re.shape, f32)
    return pl.pallas_call(body, name="s5_prep", out_shape=[small, small, bigs, bigs])(lam_re, lam_im, log_dt, b_re, b_im)


def _s5_prep_bwd(lam_re, lam_im, log_dt, b_re, b_im, dlr, dli, dbbr, dbbi):
    def body(ar_ref, ai_ref, dt_ref, br_ref, bi_ref, dlr_ref, dli_ref, dbbr_ref, dbbi_ref,
             dar_ref, dai_ref, ddt_ref, dbr_ref, dbi_ref):
        _, vjp = jax.vjp(_s5_disc, ar_ref[...], ai_ref[...], dt_ref[...], br_ref[...], bi_ref[...])
        dar, dai, ddt, dbr, dbi = vjp((dlr_ref[...], dli_ref[...], dbbr_ref[...], dbbi_ref[...]))
        dar_ref[...] = dar
        dai_ref[...] = dai
        ddt_ref[...] = jnp.broadcast_to(jnp.sum(ddt, axis=-1, keepdims=True), ddt.shape)
        dbr_ref[...] = dbr
        dbi_ref[...] = dbi

    small = jax.ShapeDtypeStruct(lam_re.shape, f32)
    bigs = jax.ShapeDtypeStruct(b_re.shape, f32)
    return pl.pallas_call(body, name="s5_prep_bwd", out_shape=[small, small, small, bigs, bigs])(
        lam_re, lam_im, log_dt, b_re, b_im, dlr, dli, dbbr, dbbi)


def _s5_tile(t):
    return min(t, 128)


def _s5_fwd_tile(t):
    return min(t, 256)


def _s5_blk(k, width):
    return slice(k * width, (k + 1) * width)


def _s5_in_bd(bb):
    b4 = bb.reshape(S5_BLOCKS, 8, S5_GROUP, S5_STATE)
    return jnp.einsum("kgin,gh->kgihn", b4, jnp.eye(8, dtype=bb.dtype)).reshape(S5_BLOCKS, S5_BCH, S5_BST)


def _s5_in_bd_diag(bd):
    b5 = bd.reshape(S5_BLOCKS, 8, S5_GROUP, 8, S5_STATE)
    return jnp.einsum("kgihn,gh->kgin", b5, jnp.eye(8, dtype=bd.dtype)).reshape(S5_GROUPS, S5_GROUP, S5_STATE)


def _s5_out_bd(c):
    c4 = c.reshape(S5_BLOCKS, 8, S5_GROUP, S5_STATE)
    return jnp.einsum("kgin,gh->kgnhi", c4, jnp.eye(8, dtype=c.dtype)).reshape(S5_BLOCKS, S5_BST, S5_BCH)


def _s5_out_bd_diag(bd):
    c5 = bd.reshape(S5_BLOCKS, 8, S5_STATE, 8, S5_GROUP)
    return jnp.einsum("kgnhi,gh->kgin", c5, jnp.eye(8, dtype=bd.dtype)).reshape(S5_GROUPS, S5_GROUP, S5_STATE)


def _s5_scan_fwd(x, ln, lb_re, lb_im, bbr_bd, bbi_bd, cr_bd, ci_bd, dskip):
    g, shift, scale, _ = ln
    t = x.shape[0]
    tm = _s5_fwd_tile(t)
    ns = S5_NSTATE

    def body(x_ref, g_ref, sc_ref, sh_ref, ar_ref, ai_ref, bbr_ref, bbi_ref, cr_ref, ci_ref, d_ref,
             yy_ref, xr_ref, xi_ref, cre_ref, cim_ref):
        @pl.when(pl.program_id(0) == 0)
        def _():
            cre_ref[...] = jnp.zeros_like(cre_ref)
            cim_ref[...] = jnp.zeros_like(cim_ref)

        h = _adaln(x_ref[...], g_ref[...], sc_ref[...], sh_ref[...])
        ub = h.astype(bf16)
        for k in range(S5_BLOCKS):
            uk = ub[:, _s5_blk(k, S5_BCH)]
            xr_ref[:, _s5_blk(k, S5_BST)] = jnp.dot(uk, bbr_ref[k], preferred_element_type=f32)
            xi_ref[:, _s5_blk(k, S5_BST)] = jnp.dot(uk, bbi_ref[k], preferred_element_type=f32)
        ar, ai = ar_ref[...], ai_ref[...]

        def step(tt, carry):
            sr, si = carry
            row = pl.ds(tt, 1)
            nr = (ar * sr - ai * si) + xr_ref[row, :]
            ni = (ar * si + ai * sr) + xi_ref[row, :]
            xr_ref[row, :] = nr
            xi_ref[row, :] = ni
            return nr, ni

        sr, si = lax.fori_loop(0, tm, step, (cre_ref[0:1, :], cim_ref[0:1, :]), unroll=2)
        cre_ref[0:1, :] = sr
        cim_ref[0:1, :] = si
        for k in range(S5_BLOCKS):
            sb = _s5_blk(k, S5_BST)
            yk = (jnp.dot(xr_ref[:, sb].astype(bf16), cr_ref[k], preferred_element_type=f32)
                  - jnp.dot(xi_ref[:, sb].astype(bf16), ci_ref[k], preferred_element_type=f32))
            cb = _s5_blk(k, S5_BCH)
            yy_ref[:, cb] = yk + d_ref[:, cb] * h[:, cb]

    bd_in = pl.BlockSpec((S5_BLOCKS, S5_BCH, S5_BST), lambda i: (0, 0, 0))
    bd_out = pl.BlockSpec((S5_BLOCKS, S5_BST, S5_BCH), lambda i: (0, 0, 0))
    st = jax.ShapeDtypeStruct((t, ns), f32)
    return pl.pallas_call(
        body, name="s5_scan_fwd", grid=(t // tm,),
        in_specs=[_tok_spec(tm, D), _row_spec(D), _row_spec(D), _row_spec(D), _row_spec(ns), _row_spec(ns),
                  bd_in, bd_in, bd_out, bd_out, _row_spec(D)],
        out_specs=[_tok_spec(tm, D), _tok_spec(tm, ns), _tok_spec(tm, ns)],
        out_shape=[jax.ShapeDtypeStruct((t, D), f32), st, st],
        scratch_shapes=[pltpu.VMEM((8, ns), f32), pltpu.VMEM((8, ns), f32)],
        compiler_params=_cp(1),
    )(x, g, scale, shift, lb_re, lb_im, bbr_bd, bbi_bd, cr_bd, ci_bd, dskip)


def _s5_scan_bwd(dyy, x, ln, xr, xi, lb_re, lb_im, bbr_bd, bbi_bd, cr_bd, ci_bd, dskip):
    g, shift, scale, _ = ln
    t = x.shape[0]
    tm = _s5_tile(t)
    nt = t // tm
    ns = S5_NSTATE
    per = tm // 8

    def body(dyy_ref, x_ref, g_ref, sc_ref, sh_ref, xr_ref, xi_ref, xrp_ref, xip_ref, ar_ref, ai_ref,
             bbr_ref, bbi_ref, cr_ref, ci_ref, d_ref,
             du_ref, dar_ref, dai_ref, dbbr_ref, dbbi_ref, dcr_ref, dci_ref, dd_ref,
             gr_ref, gi_ref, cre_ref, cim_ref):
        i = pl.program_id(0)
        first = i == 0

        @pl.when(first)
        def _():
            cre_ref[...] = jnp.zeros_like(cre_ref)
            cim_ref[...] = jnp.zeros_like(cim_ref)

        h = _adaln(x_ref[...], g_ref[...], sc_ref[...], sh_ref[...])
        ub = h.astype(bf16)
        dyy_v = dyy_ref[...]
        dyb = dyy_v.astype(bf16)
        for k in range(S5_BLOCKS):
            dk = dyb[:, _s5_blk(k, S5_BCH)]
            sb = _s5_blk(k, S5_BST)
            gr_ref[:, sb] = lax.dot_general(dk, cr_ref[k], _DN["nt"], preferred_element_type=f32)
            gi_ref[:, sb] = -lax.dot_general(dk, ci_ref[k], _DN["nt"], preferred_element_type=f32)
        ar, ai = ar_ref[...], ai_ref[...]

        def step(s, carry):
            nr_, ni_ = carry
            row = pl.ds(tm - 1 - s, 1)
            nr = gr_ref[row, :] + (ar * nr_ + ai * ni_)
            ni = gi_ref[row, :] + (ar * ni_ - ai * nr_)
            gr_ref[row, :] = nr
            gi_ref[row, :] = ni
            return nr, ni

        nr, ni = lax.fori_loop(0, tm, step, (cre_ref[0:1, :], cim_ref[0:1, :]), unroll=2)
        cre_ref[0:1, :] = nr
        cim_ref[0:1, :] = ni

        is_first_tile = i == nt - 1
        _acc_add(first, dd_ref, jnp.sum(dyy_v * h, axis=0, keepdims=True))
        for k in range(S5_BLOCKS):
            cb, sb = _s5_blk(k, S5_BCH), _s5_blk(k, S5_BST)
            xr_v, xi_v = xr_ref[:, sb], xi_ref[:, sb]
            xrp = jnp.where(is_first_tile, 0.0, xrp_ref[:, sb])
            xip = jnp.where(is_first_tile, 0.0, xip_ref[:, sb])
            xr_s = _roll_rows(jnp.concatenate([xrp, xr_v], axis=0), 1)[8:, :]
            xi_s = _roll_rows(jnp.concatenate([xip, xi_v], axis=0), 1)[8:, :]
            gr, gi = gr_ref[:, sb], gi_ref[:, sb]
            dar_k = jnp.sum(gr * xr_s + gi * xi_s, axis=0, keepdims=True)
            dai_k = jnp.sum(gi * xr_s - gr * xi_s, axis=0, keepdims=True)

            @pl.when(first)
            def _():
                dar_ref[:, sb] = dar_k
                dai_ref[:, sb] = dai_k

            @pl.when(jnp.logical_not(first))
            def _():
                dar_ref[:, sb] += dar_k
                dai_ref[:, sb] += dai_k

            grb, gib = gr.astype(bf16), gi.astype(bf16)
            uk, dk = ub[:, cb], dyb[:, cb]
            tn = lambda a_, b_: lax.dot_general(a_, b_, _DN["tn"], preferred_element_type=f32)
            vals = (tn(uk, grb), tn(uk, gib), tn(xr_v.astype(bf16), dk), -tn(xi_v.astype(bf16), dk))
            for ref, val in zip((dbbr_ref, dbbi_ref, dcr_ref, dci_ref), vals):
                @pl.when(first)
                def _():
                    ref[k] = val

                @pl.when(jnp.logical_not(first))
                def _():
                    ref[k] += val
            du_k = (lax.dot_general(grb, bbr_ref[k], _DN["nt"], preferred_element_type=f32)
                    + lax.dot_general(gib, bbi_ref[k], _DN["nt"], preferred_element_type=f32))
            du_ref[:, cb] = du_k + d_ref[:, cb] * dyy_v[:, cb]

    rev = lambda c: _tok_spec(tm, c, nt, True)
    prev = pl.BlockSpec((8, ns), lambda i: (jnp.maximum((nt - 1 - i) * per - 1, 0), 0))
    bd_in = pl.BlockSpec((S5_BLOCKS, S5_BCH, S5_BST), lambda i: (0, 0, 0))
    bd_out = pl.BlockSpec((S5_BLOCKS, S5_BST, S5_BCH), lambda i: (0, 0, 0))
    row_ns = jax.ShapeDtypeStruct((1, ns), f32)
    bd_in_s = jax.ShapeDtypeStruct((S5_BLOCKS, S5_BCH, S5_BST), f32)
    bd_out_s = jax.ShapeDtypeStruct((S5_BLOCKS, S5_BST, S5_BCH), f32)
    return pl.pallas_call(
        body, name="s5_scan_bwd", grid=(nt,),
        in_specs=[rev(D), rev(D), _row_spec(D), _row_spec(D), _row_spec(D), rev(ns), rev(ns), prev, prev,
                  _row_spec(ns), _row_spec(ns), bd_in, bd_in, bd_out, bd_out, _row_spec(D)],
        out_specs=[rev(D), _row_spec(ns), _row_spec(ns), bd_in, bd_in, bd_out, bd_out, _row_spec(D)],
        out_shape=[jax.ShapeDtypeStruct((t, D), f32), row_ns, row_ns, bd_in_s, bd_in_s, bd_out_s, bd_out_s,
                   jax.ShapeDtypeStruct((1, D), f32)],
        scratch_shapes=[pltpu.VMEM((tm, ns), f32), pltpu.VMEM((tm, ns), f32),
                        pltpu.VMEM((8, ns), f32), pltpu.VMEM((8, ns), f32)],
        compiler_params=_cp(1),
    )(dyy, x, g, scale, shift, xr, xi, xr, xi, lb_re, lb_im, bbr_bd, bbi_bd, cr_bd, ci_bd, dskip)


def _s5_gelu(yy):
    t = yy.shape[0]
    tm = _tok_tile(t)

    def body(y_ref, o_ref):
        o_ref[...] = jax.nn.gelu(y_ref[...]).astype(bf16)

    return pl.pallas_call(
        body, name="s5_gelu", grid=(t // tm,), in_specs=[_tok_spec(tm, D)], out_specs=_tok_spec(tm, D),
        out_shape=jax.ShapeDtypeStruct((t, D), bf16), compiler_params=_cp(1),
    )(yy)


def _s5_glu(gl, z):
    return gl * jax.nn.sigmoid(z)


def _s5_out(x, yy, z, gate):
    t = x.shape[0]
    tm = _tok_tile(t)

    def body(x_ref, y_ref, z_ref, gt_ref, o_ref):
        o_ref[...] = x_ref[...] + (1.0 + gt_ref[...]) * _s5_glu(jax.nn.gelu(y_ref[...]), z_ref[...])

    return pl.pallas_call(
        body, name="s5_out", grid=(t // tm,),
        in_specs=[_tok_spec(tm, D), _tok_spec(tm, D), _tok_spec(tm, D), _row_spec(D)],
        out_specs=_tok_spec(tm, D), out_shape=jax.ShapeDtypeStruct((t, D), f32), compiler_params=_cp(1),
    )(x, yy, z, gate)


def _s5_out_bwd(dxn, yy, z, gate):
    t = dxn.shape[0]
    tm = _tok_tile(t)

    def body(dxn_ref, y_ref, z_ref, gt_ref, dz_ref, dgl_ref, dgt_ref):
        dxn_v = dxn_ref[...]
        gl = jax.nn.gelu(y_ref[...])
        out, vjp = jax.vjp(_s5_glu, gl, z_ref[...])
        dgl, dz = vjp((1.0 + gt_ref[...]) * dxn_v)
        dz_ref[...] = dz.astype(bf16)
        dgl_ref[...] = dgl
        _acc_add(pl.program_id(0) == 0, dgt_ref, jnp.sum(dxn_v * out, axis=0, keepdims=True))

    return pl.pallas_call(
        body, name="s5_out_bwd", grid=(t // tm,),
        in_specs=[_tok_spec(tm, D), _tok_spec(tm, D), _tok_spec(tm, D), _row_spec(D)],
        out_specs=[_tok_spec(tm, D), _tok_spec(tm, D), _row_spec(D)],
        out_shape=[jax.ShapeDtypeStruct((t, D), bf16), jax.ShapeDtypeStruct((t, D), f32),
                   jax.ShapeDtypeStruct((1, D), f32)],
        compiler_params=_cp(1),
    )(dxn, yy, z, gate)


def _s5_gelu_bwd(yy, dgl_a, dgl_b):
    t = yy.shape[0]
    tm = _tok_tile(t)

    def body(y_ref, a_ref, b_ref, o_ref):
        _, vjp = jax.vjp(jax.nn.gelu, y_ref[...])
        o_ref[...] = vjp(a_ref[...] + b_ref[...])[0]

    return pl.pallas_call(
        body, name="s5_gelu_bwd", grid=(t // tm,),
        in_specs=[_tok_spec(tm, D), _tok_spec(tm, D), _tok_spec(tm, D)],
        out_specs=_tok_spec(tm, D), out_shape=jax.ShapeDtypeStruct((t, D), f32), compiler_params=_cp(1),
    )(yy, dgl_a, dgl_b)


def _s5_params(lam_re, lam_im, log_dt, b_re, b_im):
    bc = lambda a: a.reshape(S5_GROUPS, 1, -1)
    return (bc(lam_re), bc(lam_im), jnp.broadcast_to(log_dt.reshape(S5_GROUPS, 1, 1), (S5_GROUPS, 1, S5_STATE)),
            b_re.transpose(0, 2, 1), b_im.transpose(0, 2, 1))


def _s5_fwd(x, ln, raw, c_re, c_im, dskip, w_glu):
    gate = ln[3]
    lb_re, lb_im, bb_re, bb_im = _s5_prep(*raw)
    lbr, lbi = lb_re.reshape(1, S5_NSTATE), lb_im.reshape(1, S5_NSTATE)
    bds = (_s5_in_bd(bb_re).astype(bf16), _s5_in_bd(bb_im).astype(bf16),
           _s5_out_bd(c_re).astype(bf16), _s5_out_bd(c_im).astype(bf16))
    yy, xr, xi = _s5_scan_fwd(x, ln, lbr, lbi, *bds, dskip)
    gl = _s5_gelu(yy)
    z = _mm(gl, w_glu, "nn", f32, "s5_glu_mm")
    return _s5_out(x, yy, z, gate), (x, lbr, lbi, bds, yy, xr, xi, gl, z)


def _s5_bwd(dxn, saved, ln, raw, dskip, w_glu):
    x, lbr, lbi, bds, yy, xr, xi, gl, z = saved
    g, shift, scale, gate = ln
    dz, dgl_a, dgate = _s5_out_bwd(dxn, yy, z, gate)
    dgl_b = _mm(dz, w_glu, "nt", f32, "s5_dgl")
    dw_glu = _mm(gl, dz, "tn", f32, "s5_dwglu")
    dyy = _s5_gelu_bwd(yy, dgl_a, dgl_b)
    du, dar, dai, dbbr_bd, dbbi_bd, dcr_bd, dci_bd, dd = _s5_scan_bwd(dyy, x, ln, xr, xi, lbr, lbi, *bds, dskip)
    shp = (S5_GROUPS, 1, S5_STATE)
    d_lam_re, d_lam_im, d_dt, d_b_re, d_b_im = _s5_prep_bwd(
        *raw, dar.reshape(shp), dai.reshape(shp), _s5_in_bd_diag(dbbr_bd), _s5_in_bd_diag(dbbi_bd))
    dx, dg, dscale, dshift = _ln_bwd(x, g, scale, shift, du, dxn)
    grads = dict(
        s5_lam_re=d_lam_re.reshape(1, S5_GROUPS, S5_STATE), s5_lam_im=d_lam_im.reshape(1, S5_GROUPS, S5_STATE),
        s5_log_dt=d_dt[:, 0, 0].reshape(1, S5_GROUPS),
        s5_b_re=d_b_re.transpose(0, 2, 1)[None], s5_b_im=d_b_im.transpose(0, 2, 1)[None],
        s5_c_re=_s5_out_bd_diag(dcr_bd)[None], s5_c_im=_s5_out_bd_diag(dci_bd)[None],
        s5_d=dd, s5_w_glu=dw_glu)
    return dx, (dg, dshift, dscale, dgate), grads


_MESH = pl.DeviceIdType.MESH
_ANY = pl.BlockSpec(memory_space=pl.ANY)


def _me():
    return lax.axis_index("x"), lax.axis_index("y"), lax.axis_index("c")


def _dev_index(x, y, c):
    return 4 * x + 2 * y + c


def _all_gather(vs, name):
    n = len(vs)

    def body(*refs):
        v_refs, out_refs = refs[:n], refs[n:2 * n]
        send_sems, recv_sems, local_sems = refs[2 * n:]
        x, y, cc = _me()
        me, sibling = (x, y, cc), (x, y, 1 - cc)
        chips = [(1 - x, y), (x, 1 - y), (1 - x, 1 - y)]
        sends, local = [], []

        def copy(a, k, block, to, src=None):
            rows = out_refs[a].at[_dev_index(*block)]
            return pltpu.make_async_remote_copy(
                src_ref=rows if src is None else src, dst_ref=rows,
                send_sem=send_sems.at[7 * a + k], recv_sem=recv_sems.at[7 * a + k], device_id=to, device_id_type=_MESH)

        for a in range(n):
            mine = pltpu.make_async_copy(v_refs[a], out_refs[a].at[_dev_index(*me)], local_sems.at[a])
            mine.start()
            local.append(mine)
            first = [copy(a, 0, me, sibling, src=v_refs[a])]
            first += [copy(a, 1 + j, me, (*chip, cc), src=v_refs[a]) for j, chip in enumerate(chips)]
            for cp in first:
                cp.start()
            sends += first
        for a in range(n):
            for j, chip in enumerate(chips):
                copy(a, 1 + j, (*chip, cc), me).wait_recv()
                passed = copy(a, 4 + j, (*chip, cc), sibling)
                passed.start()
                sends.append(passed)
        for a in range(n):
            copy(a, 0, sibling, me).wait_recv()
            for j, chip in enumerate(chips):
                copy(a, 4 + j, (*chip, 1 - cc), me).wait_recv()
        for cp in sends:
            cp.wait_send()
        for cp in local:
            cp.wait()

    return pl.pallas_call(
        body, name=name, out_shape=[jax.ShapeDtypeStruct((N_DEV,) + v.shape, v.dtype) for v in vs],
        in_specs=[_ANY] * n, out_specs=[_ANY] * n,
        scratch_shapes=[pltpu.SemaphoreType.DMA((7 * n,)), pltpu.SemaphoreType.DMA((7 * n,)),
                        pltpu.SemaphoreType.DMA((n,))],
    )(*vs)


def _exchange_pair(vs, name):
    n = len(vs)

    def body(*refs):
        v_refs, out_refs = refs[:n], refs[n:2 * n]
        send_sems, recv_sems = refs[2 * n:]
        x, y, cc = _me()
        sibling = (x, y, 1 - cc)
        copies = []
        for a in range(n):
            for k in range(4):
                cp = pltpu.make_async_remote_copy(
                    src_ref=v_refs[a].at[2 * k + (1 - cc)], dst_ref=out_refs[a].at[k],
                    send_sem=send_sems.at[4 * a + k], recv_sem=recv_sems.at[4 * a + k],
                    device_id=sibling, device_id_type=_MESH)
                cp.start()
                copies.append(cp)
        for cp in copies:
            cp.wait_recv()
        for cp in copies:
            cp.wait_send()

    return pl.pallas_call(
        body, name=name, out_shape=[jax.ShapeDtypeStruct((4,) + v.shape[1:], v.dtype) for v in vs],
        in_specs=[_ANY] * n, out_specs=[_ANY] * n,
        scratch_shapes=[pltpu.SemaphoreType.DMA((4 * n,)), pltpu.SemaphoreType.DMA((4 * n,))],
    )(*vs)


def _pair_sum(v, got):
    _, r, c = v.shape
    tr = _pick(r, (512, 352, 256, 128))
    core = lax.axis_index("c").astype(jnp.int32).reshape(1)

    def body(c_ref, v_ref, g_ref, o_ref):
        o_ref[...] = (v_ref[...] + g_ref[...]).astype(bf16)

    return pl.pallas_call(
        body, name="pair_sum",
        grid_spec=pltpu.PrefetchScalarGridSpec(
            num_scalar_prefetch=1, grid=(4, r // tr),
            in_specs=[pl.BlockSpec((1, tr, c), lambda k, i, c_ref: (2 * k + c_ref[0], i, 0)),
                      pl.BlockSpec((1, tr, c), lambda k, i, c_ref: (k, i, 0))],
            out_specs=pl.BlockSpec((1, tr, c), lambda k, i, c_ref: (k, i, 0))),
        out_shape=jax.ShapeDtypeStruct((4, r, c), bf16), compiler_params=_cp(2),
    )(core, v, got)


def _exchange_chips(vs, name):
    n = len(vs)

    def body(*refs):
        v_refs, out_refs = refs[:n], refs[n:2 * n]
        send_sems, recv_sems, local_sems = refs[2 * n:]
        x, y, cc = _me()
        mine = 2 * x + y
        peers = []
        for mask in (1, 2, 3):
            px = 1 - x if mask & 2 else x
            py = 1 - y if mask & 1 else y
            peers.append((mask - 1, (px, py, cc), 2 * px + py))
        local, sends = [], []
        for a in range(n):
            own = pltpu.make_async_copy(v_refs[a].at[mine], out_refs[a].at[mine], local_sems.at[a])
            own.start()
            local.append(own)
            for k, peer, pchip in peers:
                cp = pltpu.make_async_remote_copy(
                    src_ref=v_refs[a].at[pchip], dst_ref=out_refs[a].at[mine],
                    send_sem=send_sems.at[3 * a + k], recv_sem=recv_sems.at[3 * a + k],
                    device_id=peer, device_id_type=_MESH)
                cp.start()
                sends.append(cp)
        for a in range(n):
            for k, peer, pchip in peers:
                pltpu.make_async_remote_copy(
                    src_ref=v_refs[a].at[pchip], dst_ref=out_refs[a].at[pchip],
                    send_sem=send_sems.at[3 * a + k], recv_sem=recv_sems.at[3 * a + k],
                    device_id=peer, device_id_type=_MESH).wait_recv()
        for cp in sends:
            cp.wait_send()
        for cp in local:
            cp.wait()

    return pl.pallas_call(
        body, name=name, out_shape=[jax.ShapeDtypeStruct(v.shape, v.dtype) for v in vs],
        in_specs=[_ANY] * n, out_specs=[_ANY] * n,
        scratch_shapes=[pltpu.SemaphoreType.DMA((3 * n,)), pltpu.SemaphoreType.DMA((3 * n,)),
                        pltpu.SemaphoreType.DMA((n,))],
    )(*vs)


def _ada_mod(c_all, ada_w):
    cols = ada_w.shape[2]

    def body(c_ref, w_ref, o_ref):
        cond = jax.nn.silu(c_ref[...]).astype(bf16)
        o_ref[0] = jnp.dot(cond, w_ref[0].astype(bf16), preferred_element_type=f32)

    return pl.pallas_call(
        body, name="ada_mod", grid=(DEPTH,),
        in_specs=[pl.BlockSpec((16, D), lambda i: (0, 0)), pl.BlockSpec((1, D, cols), lambda i: (i, 0, 0))],
        out_specs=pl.BlockSpec((1, 16, cols), lambda i: (i, 0, 0)),
        out_shape=jax.ShapeDtypeStruct((DEPTH, 16, cols), f32), compiler_params=_cp(1),
    )(c_all, ada_w)


def _ada_grad(c_all, dmod):
    cols = dmod.shape[2]

    def body(c_ref, d_ref, o_ref):
        cond = jax.nn.silu(c_ref[...]).astype(bf16)
        o_ref[0] = lax.dot_general(cond, d_ref[0].astype(bf16), _DN["tn"], preferred_element_type=f32)

    return pl.pallas_call(
        body, name="ada_grad", grid=(DEPTH,),
        in_specs=[pl.BlockSpec((16, D), lambda i: (0, 0)), pl.BlockSpec((1, 16, cols), lambda i: (i, 0, 0))],
        out_specs=pl.BlockSpec((1, D, cols), lambda i: (i, 0, 0)),
        out_shape=jax.ShapeDtypeStruct((DEPTH, D, cols), f32), compiler_params=_cp(1),
    )(c_all, dmod)


def _row_tile(r):
    return _pick(r, (512, 352, 256, 128)) if r > 512 else r


def _sum_sources(v, name):
    n, r, c = v.shape
    tr = _row_tile(r)

    def body(v_ref, o_ref):
        acc = v_ref[0]
        for p in range(1, n):
            acc = acc + v_ref[p]
        o_ref[...] = acc.astype(f32)

    return pl.pallas_call(
        body, name=name, grid=(r // tr,),
        in_specs=[pl.BlockSpec((n, tr, c), lambda i: (0, i, 0))], out_specs=pl.BlockSpec((tr, c), lambda i: (i, 0)),
        out_shape=jax.ShapeDtypeStruct((r, c), f32), compiler_params=_cp(1),
    )(v)


def _adamw(parts, w, m, v, name):
    n, r, c = parts.shape
    tr = _row_tile(r)
    c1 = 1.0 - ADAM_B1 ** ADAM_STEP
    c2 = 1.0 - ADAM_B2 ** ADAM_STEP

    def body(p_ref, w_ref, m_ref, v_ref, g_ref, d_ref, mo_ref, vo_ref):
        g_v = p_ref[0].astype(f32)
        for p in range(1, n):
            g_v = g_v + p_ref[p].astype(f32)
        m_n = ADAM_B1 * m_ref[...] + (1.0 - ADAM_B1) * g_v
        v_n = ADAM_B2 * v_ref[...] + (1.0 - ADAM_B2) * (g_v * g_v)
        g_ref[...] = g_v
        d_ref[...] = -ADAM_LR * ((m_n / c1) / (jnp.sqrt(v_n / c2) + ADAM_EPS) + ADAM_WD * w_ref[...])
        mo_ref[...] = m_n
        vo_ref[...] = v_n

    spec = pl.BlockSpec((tr, c), lambda i: (i, 0))
    shp = jax.ShapeDtypeStruct((r, c), f32)
    return pl.pallas_call(
        body, name=name, grid=(r // tr,), in_specs=[pl.BlockSpec((n, tr, c), lambda i: (0, i, 0))] + [spec] * 3,
        out_specs=[spec] * 4, out_shape=[shp] * 4, compiler_params=_cp(1),
    )(parts, w, m, v)


def _two_d(shape):
    return (math.prod(shape[:-1]), shape[-1])


def _pack_rows(a):
    n = a.size
    rows = -(-n // (8 * PACK_C)) * 8
    return jnp.pad(a.reshape(-1), (0, rows * PACK_C - n)).reshape(rows, PACK_C)


def _pack(parts):
    return jnp.concatenate([_pack_rows(p) for p in parts], axis=0)


def _unpack(packed, shapes):
    lead = packed.shape[:-2]
    out, off = [], 0
    for s in shapes:
        n = math.prod(s)
        rows = -(-n // (8 * PACK_C)) * 8
        part = packed[..., off:off + rows, :].reshape(lead + (rows * PACK_C,))
        out.append(part[..., :n].reshape(lead + tuple(s)))
        off += rows
    return out


def _unshard(g8, axis):
    local = g8.shape[1:]
    moved = jnp.moveaxis(g8, 0, axis)
    return moved.reshape(local[:axis] + (N_DEV * local[axis],) + local[axis + 1:])


_BIG = dict(ffn_w_in=3, ffn_w_out=2, pool_w=2, fox_w_in=2, fox_w_o=1, s5_w_glu=1, conv_w_in=2, conv_w_out=1)
_SMALL_SHARDED = dict(norm_g=2, s5_d=1, conv_w=3)
_REPLICATED = ("ada_b", "pool_scale", "fox_b_f", "fox_q_gain", "fox_k_gain", "s5_lam_re", "s5_lam_im", "s5_log_dt",
               "s5_b_re", "s5_b_im", "s5_c_re", "s5_c_im")
_WEIGHTS = ("ada_w", "ada_b", "norm_g", "ffn_w_in", "ffn_w_out", "pool_w", "pool_scale", "fox_w_in", "fox_b_f",
            "fox_q_gain", "fox_k_gain", "fox_w_o", "s5_lam_re", "s5_lam_im", "s5_log_dt", "s5_b_re", "s5_b_im",
            "s5_c_re", "s5_c_im", "s5_d", "s5_w_glu", "conv_w_in", "conv_w", "conv_w_out")


def _step(x, c, target, w, m, v):
    t = x.shape[1]
    xi_, yi_, ci_ = _me()
    me = _dev_index(xi_, yi_, ci_)

    sm_shapes = [w[n].shape for n in _SMALL_SHARDED]
    small_all = _all_gather([_pack([c] + [w[n] for n in _SMALL_SHARDED])], "gather_small")[0]
    gathered = _unpack(small_all, [c.shape] + sm_shapes)
    c_all = gathered[0][:, 0, :]
    full = {n: _unshard(p, ax) for (n, ax), p in zip(_SMALL_SHARDED.items(), gathered[1:])}

    big_all = _all_gather([w[n].astype(bf16).reshape(_two_d(w[n].shape)) for n in _BIG], "gather_weights")
    gw = dict(zip(_BIG, big_all))
    ffn_w_in = gw["ffn_w_in"].reshape(N_DEV, 2 * DEPTH, D, FFN_HS)
    ffn_w_out = gw["ffn_w_out"].reshape(N_DEV, 2 * DEPTH, D_FF // N_DEV, D)
    pool_w = gw["pool_w"].reshape(N_DEV, 4, POOL_GROUP // N_DEV, POOL_GROUP).transpose(1, 0, 2, 3)
    pool_w = pool_w.reshape(4, POOL_GROUP, POOL_GROUP)
    fox_w_in = jnp.pad(gw["fox_w_in"].transpose(1, 0, 2).reshape(D, FOX_PROJ), ((0, 0), (0, FOX_PROJ_PAD - FOX_PROJ)))
    fox_w_o, s5_w_glu, conv_w_out = (gw[n].reshape(D, D) for n in ("fox_w_o", "s5_w_glu", "conv_w_out"))
    conv_w_in = gw["conv_w_in"]

    def ffn_w(i, f):
        return ffn_w_in, ffn_w_out, 2 * i + f

    c16 = jnp.pad(c_all, ((0, 8), (0, 0)))
    cols = w["ada_w"].shape[2]
    mod_sh = _ada_mod(c16, w["ada_w"])
    mod_all = _all_gather([mod_sh.reshape(DEPTH * 16, cols)], "gather_mod")[0].reshape(N_DEV, DEPTH, 16, cols)
    mod_mine = lax.dynamic_index_in_dim(mod_all, me, axis=2, keepdims=False)
    mod = (mod_mine.transpose(1, 0, 2).reshape(DEPTH, N_DEV * cols) + w["ada_b"]).reshape(DEPTH, 3, 3, D)

    norm_g = full["norm_g"]

    def ln_of(i, sub):
        return (norm_g[i, sub][None], mod[i, sub, 0][None], mod[i, sub, 1][None], mod[i, sub, 2][None])

    fox_b_f = jnp.pad(w["fox_b_f"], ((0, 0), (0, 128 - FOX_HEADS)))
    s5_raw = _s5_params(w["s5_lam_re"][0], w["s5_lam_im"][0], w["s5_log_dt"][0], w["s5_b_re"][0], w["s5_b_im"][0])
    s5_c_re, s5_c_im = w["s5_c_re"][0], w["s5_c_im"][0]
    conv_w = jnp.pad(full["conv_w"][0, :, 0, :], ((0, 5), (0, 0)))

    xs = x[0]
    saved = []
    for i in range(DEPTH):
        xs, s0 = _ffn_fwd(xs, ln_of(i, 0), *ffn_w(i, 0), 0.5)
        if i == 0:
            s1 = xs
            xs = _pool_fwd(xs, ln_of(i, 1), pool_w, w["pool_scale"])
        elif i == 1:
            xs, s1 = _fox_fwd(xs, ln_of(i, 1), fox_w_in, fox_b_f, w["fox_q_gain"], w["fox_k_gain"], fox_w_o)
        elif i == 2:
            xs, s1 = _s5_fwd(xs, ln_of(i, 1), s5_raw, s5_c_re, s5_c_im, full["s5_d"], s5_w_glu)
        else:
            xs, s1 = _convmix_fwd(xs, ln_of(i, 1), conv_w_in, conv_w, conv_w_out)
        xs, s2 = _ffn_fwd(xs, ln_of(i, 2), *ffn_w(i, 1), 0.5)
        saved.append((s0, s1, s2))
    dx, lpart = _loss_head(xs, target[0])

    grads = {}
    dmod = [[None] * 3 for _ in range(DEPTH)]
    dnorm = [[None] * 3 for _ in range(DEPTH)]
    dffn_in = lax.empty(ffn_w_in.shape, f32)
    dffn_out = lax.empty(ffn_w_out.shape, f32)

    def put_ln(i, sub, dln):
        dg, dshift, dscale, dgate = dln
        dnorm[i][sub] = dg
        dmod[i][sub] = jnp.concatenate([dshift, dscale, dgate], axis=0)

    for i in reversed(range(DEPTH)):
        s0, s1, s2 = saved[i]
        dx, dln, dffn_in, dffn_out = _ffn_bwd(dx, s2, ln_of(i, 2), *ffn_w(i, 1), 0.5, dffn_in, dffn_out)
        put_ln(i, 2, dln)
        if i == 0:
            dx, dln, dpw, dps = _pool_bwd(dx, s1, ln_of(i, 1), pool_w, w["pool_scale"])
            dpw = dpw.reshape(4, N_DEV, POOL_GROUP // N_DEV, POOL_GROUP).transpose(1, 0, 2, 3)
            grads.update(pool_w=dpw.reshape(N_DEV, 4 * POOL_GROUP // N_DEV, POOL_GROUP), pool_scale=dps)
        elif i == 1:
            dx, dln, dwi, dbf, dqg, dkg, dwo = _fox_bwd(dx, s1, ln_of(i, 1), fox_w_in, fox_b_f, fox_w_o)
            dwi = dwi[:, :FOX_PROJ].reshape(D, N_DEV, FOX_PROJ // N_DEV).transpose(1, 0, 2)
            grads.update(fox_w_in=dwi, fox_b_f=dbf[:, :FOX_HEADS], fox_q_gain=dqg, fox_k_gain=dkg,
                         fox_w_o=dwo.reshape(N_DEV, D // N_DEV, D))
        elif i == 2:
            dx, dln, gs5 = _s5_bwd(dx, s1, ln_of(i, 1), s5_raw, full["s5_d"], s5_w_glu)
            gs5["s5_w_glu"] = gs5["s5_w_glu"].reshape(N_DEV, D // N_DEV, D)
            grads.update(gs5)
        else:
            dx, dln, dwi, dcw, dwo = _convmix_bwd(dx, s1, ln_of(i, 1), conv_w_in, conv_w, conv_w_out)
            grads.update(conv_w_in=dwi, conv_w=dcw[None, :, None, :], conv_w_out=dwo.reshape(N_DEV, D // N_DEV, D))
        put_ln(i, 1, dln)
        dx, dln, dffn_in, dffn_out = _ffn_bwd(dx, s0, ln_of(i, 0), *ffn_w(i, 0), 0.5, dffn_in, dffn_out)
        put_ln(i, 0, dln)
    grads["ffn_w_in"] = dffn_in.reshape(N_DEV, 2 * DEPTH * D, FFN_HS)
    grads["ffn_w_out"] = dffn_out.reshape(N_DEV, 2 * DEPTH * D_FF // N_DEV, D)
    grads["norm_g"] = jnp.stack([jnp.concatenate(r, axis=0) for r in dnorm])
    dmod_mine = jnp.stack([jnp.stack(r) for r in dmod]).reshape(DEPTH, 9 * D)

    small_names = list(_REPLICATED[1:]) + list(_SMALL_SHARDED)
    small_parts = [dmod_mine] + [grads[n] for n in small_names] + [lpart[:, 0:1]]
    small_g = _all_gather([_pack(small_parts)], "gather_grads")[0]
    small_sum = _sum_sources(small_g, "sum_small")
    summed = dict(zip(["ada_b"] + small_names + ["loss"], _unpack(small_sum, [p.shape for p in small_parts])))
    loss = summed.pop("loss")[0, 0]
    for n, ax in _SMALL_SHARDED.items():
        local = w[n].shape[ax]
        summed[n] = lax.dynamic_slice_in_dim(summed[n], me * local, local, axis=ax)

    dmod_all = small_g[:, :DEPTH * 9].reshape(N_DEV, DEPTH, 9 * D)
    dmod_cols = lax.dynamic_slice_in_dim(dmod_all, me * cols, cols, axis=2)
    ada_g = _ada_grad(c16, jnp.pad(dmod_cols.transpose(1, 0, 2), ((0, 0), (0, 8), (0, 0))))

    big_parts = [grads[n] for n in _BIG]
    from_sibling = _exchange_pair(big_parts, "exchange_pair")
    chip_sums = [_pair_sum(p, s) for p, s in zip(big_parts, from_sibling)]
    big_landed = dict(zip(_BIG, _exchange_chips(chip_sums, "exchange_chips")))

    grad, delta, new_m, new_v = {}, {}, {}, {}
    big_landed["ada_w"] = ada_g[None]
    for n, parts in big_landed.items():
        view = _two_d(w[n].shape)
        outs = _adamw(parts.reshape((parts.shape[0],) + view), w[n].reshape(view), m[n].reshape(view),
                      v[n].reshape(view), "adamw_" + n)
        grad[n], delta[n], new_m[n], new_v[n] = (a.reshape(w[n].shape) for a in outs)
    small = [n for n in _WEIGHTS if n not in big_landed]
    small_shapes = [w[n].shape for n in small]
    pk = lambda d: _pack([d[n] for n in small])
    outs = _adamw(pk(summed)[None], pk(w), pk(m), pk(v), "adamw_small")
    for dst, packed in zip((grad, delta, new_m, new_v), outs):
        dst.update(zip(small, _unpack(packed, small_shapes)))
    return (loss, dx[None], *[grad[n] for n in _WEIGHTS], *[delta[n] for n in _WEIGHTS],
            *[new_m[n] for n in _WEIGHTS], *[new_v[n] for n in _WEIGHTS])


def kernel(x, c, ada_w, ada_b, norm_g, ffn_w_in, ffn_w_out, pool_w, pool_scale, fox_w_in, fox_b_f, fox_q_gain, fox_k_gain, fox_w_o, s5_lam_re, s5_lam_im, s5_log_dt, s5_b_re, s5_b_im, s5_c_re, s5_c_im, s5_d, s5_w_glu, conv_w_in, conv_w, conv_w_out, loss_target, m_ada_w, m_ada_b, m_norm_g, m_ffn_w_in, m_ffn_w_out, m_pool_w, m_pool_scale, m_fox_w_in, m_fox_b_f, m_fox_q_gain, m_fox_k_gain, m_fox_w_o, m_s5_lam_re, m_s5_lam_im, m_s5_log_dt, m_s5_b_re, m_s5_b_im, m_s5_c_re, m_s5_c_im, m_s5_d, m_s5_w_glu, m_conv_w_in, m_conv_w, m_conv_w_out, v_ada_w, v_ada_b, v_norm_g, v_ffn_w_in, v_ffn_w_out, v_pool_w, v_pool_scale, v_fox_w_in, v_fox_b_f, v_fox_q_gain, v_fox_k_gain, v_fox_w_o, v_s5_lam_re, v_s5_lam_im, v_s5_log_dt, v_s5_b_re, v_s5_b_im, v_s5_c_re, v_s5_c_im, v_s5_d, v_s5_w_glu, v_conv_w_in, v_conv_w, v_conv_w_out):
    ws = (ada_w, ada_b, norm_g, ffn_w_in, ffn_w_out, pool_w, pool_scale, fox_w_in, fox_b_f, fox_q_gain, fox_k_gain,
          fox_w_o, s5_lam_re, s5_lam_im, s5_log_dt, s5_b_re, s5_b_im, s5_c_re, s5_c_im, s5_d, s5_w_glu, conv_w_in,
          conv_w, conv_w_out)
    ms = (m_ada_w, m_ada_b, m_norm_g, m_ffn_w_in, m_ffn_w_out, m_pool_w, m_pool_scale, m_fox_w_in, m_fox_b_f,
          m_fox_q_gain, m_fox_k_gain, m_fox_w_o, m_s5_lam_re, m_s5_lam_im, m_s5_log_dt, m_s5_b_re, m_s5_b_im,
          m_s5_c_re, m_s5_c_im, m_s5_d, m_s5_w_glu, m_conv_w_in, m_conv_w, m_conv_w_out)
    vs = (v_ada_w, v_ada_b, v_norm_g, v_ffn_w_in, v_ffn_w_out, v_pool_w, v_pool_scale, v_fox_w_in, v_fox_b_f,
          v_fox_q_gain, v_fox_k_gain, v_fox_w_o, v_s5_lam_re, v_s5_lam_im, v_s5_log_dt, v_s5_b_re, v_s5_b_im,
          v_s5_c_re, v_s5_c_im, v_s5_d, v_s5_w_glu, v_conv_w_in, v_conv_w, v_conv_w_out)
    return _step(x, c, loss_target, dict(zip(_WEIGHTS, ws)), dict(zip(_WEIGHTS, ms)), dict(zip(_WEIGHTS, vs)))
```

```python
import math

import jax
import jax.numpy as jnp
from jax import lax
from jax.experimental import pallas as pl
from jax.experimental.pallas import tpu as pltpu

f32 = jnp.float32
bf16 = jnp.bfloat16

D = 1024
D_FF = 2816
FFN_HS = 2 * D_FF // 8
FFN_SLABS = 4
DEPTH = 4
NORM_EPS = 1e-6
N_DEV = 8
POOL_WINDOWS = (2, 4, 8, 16)
POOL_GROUP = 256
POOL_HALO = 16
FOX_HEADS = 16
FOX_HEAD_DIM = 64
FOX_PROJ = 3088
FOX_PROJ_PAD = 3200
S5_GROUPS = 64
S5_GROUP = 16
S5_STATE = 64
S5_NSTATE = S5_GROUPS * S5_STATE
S5_BLOCKS = 8
S5_BCH = 128
S5_BST = 512
CONV_HALO = 8
ADAM_LR = 0.001
ADAM_B1 = 0.9
ADAM_B2 = 0.999
ADAM_EPS = 1e-08
ADAM_WD = 0.01
ADAM_STEP = 10
VMEM_LIMIT = 56 * 1024 * 1024
PACK_C = 1024

_ARB = "arbitrary"


def _cp(n_axes):
    return pltpu.CompilerParams(dimension_semantics=(_ARB,) * n_axes, vmem_limit_bytes=VMEM_LIMIT)


def _pick(n, prefs):
    for c in prefs:
        if n % c == 0:
            return c
    return n


_DN = {"nn": (((1,), (0,)), ((), ())), "nt": (((1,), (1,)), ((), ())), "tn": (((0,), (0,)), ((), ()))}


def _mm(a, b, mode, out_dtype, name):
    if mode == "nn":
        (m, k), (_, n) = a.shape, b.shape
    elif mode == "nt":
        (m, k), (n, _) = a.shape, b.shape
    else:
        (k, m), (_, n) = a.shape, b.shape
    big = (1408, 1024, 640, 512, 384, 256, 128)
    tm = _pick(m, big) if mode == "tn" else _pick(m, (1024, 512, 256, 128))
    tn = _pick(n, big)
    if mode == "tn":
        tk = _pick(k, (4096, 2048, 1024, 512, 256, 128))
    else:
        tk = k if k <= 3200 else _pick(k, (2816, 2048, 1024, 512))
    nk = k // tk
    dn = _DN[mode]

    def body(a_ref, b_ref, o_ref, acc_ref):
        p = lax.dot_general(a_ref[...], b_ref[...], dn, preferred_element_type=f32)
        if nk == 1:
            o_ref[...] = p.astype(out_dtype)
        else:
            kk = pl.program_id(2)

            @pl.when(kk == 0)
            def _():
                acc_ref[...] = p

            @pl.when(kk > 0)
            def _():
                acc_ref[...] += p

            @pl.when(kk == nk - 1)
            def _():
                o_ref[...] = acc_ref[...].astype(out_dtype)

    if mode == "nn":
        a_spec = pl.BlockSpec((tm, tk), lambda i, j, kk: (i, kk))
        b_spec = pl.BlockSpec((tk, tn), lambda i, j, kk: (kk, j))
    elif mode == "nt":
        a_spec = pl.BlockSpec((tm, tk), lambda i, j, kk: (i, kk))
        b_spec = pl.BlockSpec((tn, tk), lambda i, j, kk: (j, kk))
    else:
        a_spec = pl.BlockSpec((tk, tm), lambda i, j, kk: (kk, i))
        b_spec = pl.BlockSpec((tk, tn), lambda i, j, kk: (kk, j))
    acc_shape = (tm, tn) if nk > 1 else (8, 128)
    return pl.pallas_call(
        body, name=name, grid=(m // tm, n // tn, nk),
        in_specs=[a_spec, b_spec], out_specs=pl.BlockSpec((tm, tn), lambda i, j, kk: (i, j)),
        out_shape=jax.ShapeDtypeStruct((m, n), out_dtype),
        scratch_shapes=[pltpu.VMEM(acc_shape, f32)],
        compiler_params=_cp(3),
    )(a, b)


def _mmx(a, b, mode, name, grid, a_spec, b_spec, o_spec, out_shape, into=None):
    nk = grid[2]
    dn = _DN[mode]
    out_dtype = out_shape.dtype
    a_blk = (math.prod(a_spec.block_shape[:-1]), a_spec.block_shape[-1])
    b_blk = (math.prod(b_spec.block_shape[:-1]), b_spec.block_shape[-1])
    o_blk = (math.prod(o_spec.block_shape[:-1]), o_spec.block_shape[-1])

    def body(a_ref, b_ref, *rest):
        o_ref, acc_ref = rest[-2:]
        p = lax.dot_general(a_ref[...].reshape(a_blk), b_ref[...].reshape(b_blk), dn, preferred_element_type=f32)
        if nk == 1:
            o_ref[...] = p.reshape(o_ref.shape).astype(out_dtype)
        else:
            kk = pl.program_id(2)

            @pl.when(kk == 0)
            def _():
                acc_ref[...] = p

            @pl.when(kk > 0)
            def _():
                acc_ref[...] += p

            @pl.when(kk == nk - 1)
            def _():
                o_ref[...] = acc_ref[...].reshape(o_ref.shape).astype(out_dtype)

    extra = {} if into is None else dict(input_output_aliases={2: 0})
    operands = (a, b) if into is None else (a, b, into)
    return pl.pallas_call(
        body, name=name, grid=grid, in_specs=[a_spec, b_spec] + ([] if into is None else [_ANY]),
        out_specs=o_spec, out_shape=out_shape,
        scratch_shapes=[pltpu.VMEM(o_blk if nk > 1 else (8, 128), f32)], compiler_params=_cp(3), **extra,
    )(*operands)


def _tok_tile(t):
    return min(t, 512)


def _tok_spec(tm, c, nt=None, reverse=False):
    if reverse:
        return pl.BlockSpec((tm, c), lambda i: (nt - 1 - i, 0))
    return pl.BlockSpec((tm, c), lambda i: (i, 0))


def _row_spec(c, rows=1):
    return pl.BlockSpec((rows, c), lambda i: (0, 0))


def _acc_add(first, ref, val):
    @pl.when(first)
    def _():
        ref[...] = val

    @pl.when(jnp.logical_not(first))
    def _():
        ref[...] += val


def _adaln(x, g, scale, shift):
    y = x * lax.rsqrt(jnp.mean(x * x, axis=-1, keepdims=True) + NORM_EPS)
    return (y * g) * (1.0 + scale) + shift


def _ln_fwd(x, g, scale, shift):
    t = x.shape[0]
    tm = _tok_tile(t)

    def body(x_ref, g_ref, sc_ref, sh_ref, h_ref):
        h_ref[...] = _adaln(x_ref[...], g_ref[...], sc_ref[...], sh_ref[...]).astype(bf16)

    return pl.pallas_call(
        body, name="ln_fwd", grid=(t // tm,),
        in_specs=[_tok_spec(tm, D), _row_spec(D), _row_spec(D), _row_spec(D)],
        out_specs=_tok_spec(tm, D), out_shape=jax.ShapeDtypeStruct((t, D), bf16),
        compiler_params=_cp(1),
    )(x, g, scale, shift)


def _ln_bwd(x, g, scale, shift, dh, dxn):
    t = x.shape[0]
    tm = _tok_tile(t)

    def body(x_ref, g_ref, sc_ref, sh_ref, dh_ref, dxn_ref, dx_ref, dg_ref, dsc_ref, dsh_ref):
        _, vjp = jax.vjp(_adaln, x_ref[...], g_ref[...], sc_ref[...], sh_ref[...])
        dx, dg, dsc, dsh = vjp(dh_ref[...])
        dx_ref[...] = dxn_ref[...] + dx
        first = pl.program_id(0) == 0
        _acc_add(first, dg_ref, dg)
        _acc_add(first, dsc_ref, dsc)
        _acc_add(first, dsh_ref, dsh)

    row = jax.ShapeDtypeStruct((1, D), f32)
    return pl.pallas_call(
        body, name="ln_bwd", grid=(t // tm,),
        in_specs=[_tok_spec(tm, D), _row_spec(D), _row_spec(D), _row_spec(D), _tok_spec(tm, D), _tok_spec(tm, D)],
        out_specs=[_tok_spec(tm, D), _row_spec(D), _row_spec(D), _row_spec(D)],
        out_shape=[jax.ShapeDtypeStruct((t, D), f32), row, row, row],
        compiler_params=_cp(1),
    )(x, g, scale, shift, dh, dxn)


def _swiglu(g, u):
    return jax.nn.silu(g) * u


def _ffn_tile(t):
    return min(t, 512)


def _ffn_specs(tm, idx, rows):
    once = pl.Buffered(1)
    return (pl.BlockSpec((2, FFN_SLABS, 1, D, FFN_HS), lambda i, q: (0, 0, idx, 0, 0), pipeline_mode=once),
            pl.BlockSpec((N_DEV, 1, rows, D), lambda i, q: (0, idx, 0, 0), pipeline_mode=once),
            pl.BlockSpec((2, 1, tm, FFN_HS), lambda i, q: (0, q, i, 0)),
            pl.BlockSpec((tm, D), lambda i, q: (i, 0)),
            pl.BlockSpec((1, D), lambda i, q: (0, 0)))


def _out_rows(wo_ref, q):
    return wo_ref[pl.ds(2 * q, 2), 0].reshape(FFN_HS, D)


def _ffn_up(x, h, gate, w_in, w_out, idx, coef):
    t = h.shape[0]
    tm = _ffn_tile(t)
    rows = w_out.shape[2]

    def body(h_ref, wi_ref, wo_ref, x_ref, gt_ref, gu_ref, a_ref, o_ref, xn_ref):
        q = pl.program_id(1)
        hv = h_ref[...]
        g = jnp.dot(hv, wi_ref[0, q, 0], preferred_element_type=f32)
        u = jnp.dot(hv, wi_ref[1, q, 0], preferred_element_type=f32)
        gu_ref[0, 0] = g.astype(bf16)
        gu_ref[1, 0] = u.astype(bf16)
        a = _swiglu(g, u).astype(bf16)
        a_ref[0] = a
        _acc_add(q == 0, o_ref, jnp.dot(a, _out_rows(wo_ref, q), preferred_element_type=f32))

        @pl.when(q == FFN_SLABS - 1)
        def _():
            xn_ref[...] = x_ref[...] + (coef * (1.0 + gt_ref[...])) * o_ref[...]

    wi, wo, slab, tok, row = _ffn_specs(tm, idx, rows)
    tok_f32 = jax.ShapeDtypeStruct((t, D), f32)
    return pl.pallas_call(
        body, name="ffn_up", grid=(t // tm, FFN_SLABS), in_specs=[tok, wi, wo, tok, row],
        out_specs=[slab, pl.BlockSpec((1, tm, FFN_HS), lambda i, q: (q, i, 0)), tok, tok],
        out_shape=[jax.ShapeDtypeStruct((2, FFN_SLABS, t, FFN_HS), bf16),
                   jax.ShapeDtypeStruct((FFN_SLABS, t, FFN_HS), bf16), tok_f32, tok_f32],
        compiler_params=_cp(2),
    )(h, w_in, w_out, x, gate)


def _ffn_dgu(do, w_out, idx, gu):
    t = do.shape[0]
    tm = _mm_tile(t)

    def body(do_ref, wo_ref, gu_ref, o_ref):
        da = lax.dot_general(do_ref[...], _out_rows(wo_ref, pl.program_id(1)), _DN["nt"], preferred_element_type=f32)
        _, vjp = jax.vjp(_swiglu, gu_ref[0, 0].astype(f32), gu_ref[1, 0].astype(f32))
        dg, du = vjp(da)
        o_ref[0, 0] = dg.astype(bf16)
        o_ref[1, 0] = du.astype(bf16)

    _, wo, slab, tok, _ = _ffn_specs(tm, idx, w_out.shape[2])
    return pl.pallas_call(
        body, name="ffn_dgu", grid=(t // tm, FFN_SLABS), in_specs=[tok, wo, slab],
        out_specs=slab, out_shape=jax.ShapeDtypeStruct((2, FFN_SLABS, t, FFN_HS), bf16),
        compiler_params=_cp(2),
    )(do, w_out, gu)


def _ffn_dh(dgu, w_in, idx):
    t = dgu.shape[2]
    tm = _mm_tile(t)

    def body(d_ref, wi_ref, o_ref):
        q = pl.program_id(1)
        p = (lax.dot_general(d_ref[0, 0], wi_ref[0, q, 0], _DN["nt"], preferred_element_type=f32)
             + lax.dot_general(d_ref[1, 0], wi_ref[1, q, 0], _DN["nt"], preferred_element_type=f32))
        _acc_add(pl.program_id(1) == 0, o_ref, p)

    wi, _, slab, tok, _ = _ffn_specs(tm, idx, 8)
    return pl.pallas_call(
        body, name="ffn_dh", grid=(t // tm, FFN_SLABS), in_specs=[slab, wi], out_specs=tok,
        out_shape=jax.ShapeDtypeStruct((t, D), f32), compiler_params=_cp(2),
    )(dgu, w_in)


def _resid_fwd(x, o, gate, coef):
    t = x.shape[0]
    tm = _tok_tile(t)

    def body(x_ref, o_ref, gt_ref, y_ref):
        y_ref[...] = x_ref[...] + (coef * (1.0 + gt_ref[...])) * o_ref[...]

    return pl.pallas_call(
        body, name="resid_fwd", grid=(t // tm,),
        in_specs=[_tok_spec(tm, D), _tok_spec(tm, D), _row_spec(D)],
        out_specs=_tok_spec(tm, D), out_shape=jax.ShapeDtypeStruct((t, D), f32),
        compiler_params=_cp(1),
    )(x, o, gate)


def _resid_bwd(dxn, o, gate, coef):
    t = dxn.shape[0]
    tm = _tok_tile(t)

    def body(dxn_ref, o_ref, gt_ref, do_ref, dgt_ref):
        dxn_v = dxn_ref[...]
        do_ref[...] = ((coef * (1.0 + gt_ref[...])) * dxn_v).astype(bf16)
        _acc_add(pl.program_id(0) == 0, dgt_ref, coef * jnp.sum(dxn_v * o_ref[...], axis=0, keepdims=True))

    return pl.pallas_call(
        body, name="resid_bwd", grid=(t // tm,),
        in_specs=[_tok_spec(tm, D), _tok_spec(tm, D), _row_spec(D)],
        out_specs=[_tok_spec(tm, D), _row_spec(D)],
        out_shape=[jax.ShapeDtypeStruct((t, D), bf16), jax.ShapeDtypeStruct((1, D), f32)],
        compiler_params=_cp(1),
    )(dxn, o, gate)


def _loss_head(y, target):
    t = y.shape[0]
    tm = _tok_tile(t)

    def body(y_ref, t_ref, dy_ref, l_ref):
        err = y_ref[...] - t_ref[...]
        dy_ref[...] = err * (1.0 / D)
        part = jnp.sum(jnp.sum(err * err, axis=0, keepdims=True), axis=1, keepdims=True) * (0.5 / D)
        _acc_add(pl.program_id(0) == 0, l_ref, jnp.broadcast_to(part, (1, 128)))

    return pl.pallas_call(
        body, name="loss_head", grid=(t // tm,),
        in_specs=[_tok_spec(tm, D), _tok_spec(tm, D)],
        out_specs=[_tok_spec(tm, D), _row_spec(128)],
        out_shape=[jax.ShapeDtypeStruct((t, D), f32), jax.ShapeDtypeStruct((1, 128), f32)],
        compiler_params=_cp(1),
    )(y, target)


def _mm_tile(t):
    return min(t, 1024)


def _ffn_fwd(x, ln, w_in, w_out, idx, coef):
    g, shift, scale, gate = ln
    h = _ln_fwd(x, g, scale, shift)
    gu, a, o, xn = _ffn_up(x, h, gate, w_in.reshape((2, FFN_SLABS) + w_in.shape[1:]), w_out, idx, coef)
    return xn, (x, h, gu, a, o)


def _ffn_bwd(dxn, saved, ln, w_in, w_out, idx, coef, dw_in, dw_out):
    x, h, gu, a, o = saved
    g, shift, scale, gate = ln
    t = x.shape[0]
    tm = _mm_tile(t)
    tk = _pick(t, (4096, 2048, 1024, 512, 256, 128))
    rows = w_out.shape[2]
    sds = jax.ShapeDtypeStruct
    do, dgate = _resid_bwd(dxn, o, gate, coef)
    dgu = _ffn_dgu(do, w_out, idx, gu)
    dh = _ffn_dh(dgu, w_in.reshape((2, FFN_SLABS) + w_in.shape[1:]), idx)
    dgu = dgu.reshape(N_DEV, t, FFN_HS)
    dw_out = _mmx(a, do, "tn", "ffn_dwout", (FFN_SLABS, 1, t // tk),
                  pl.BlockSpec((1, tk, FFN_HS), lambda i, j, k: (i, k, 0)),
                  pl.BlockSpec((tk, D), lambda i, j, k: (k, 0)),
                  pl.BlockSpec((2, 1, rows, D), lambda i, j, k: (i, idx, 0, 0)),
                  sds(dw_out.shape, f32), into=dw_out)
    dw_in = _mmx(h, dgu, "tn", "ffn_dwin", (N_DEV, 1, t // tk),
                 pl.BlockSpec((tk, D), lambda i, j, k: (k, 0)),
                 pl.BlockSpec((1, tk, FFN_HS), lambda i, j, k: (i, k, 0)),
                 pl.BlockSpec((1, 1, D, FFN_HS), lambda i, j, k: (i, idx, 0, 0)),
                 sds(dw_in.shape, f32), into=dw_in)
    dx, dg, dscale, dshift = _ln_bwd(x, g, scale, shift, dh, dxn)
    return dx, (dg, dshift, dscale, dgate), dw_in, dw_out


def _roll_rows(a, k):
    n = a.shape[0]
    return pltpu.roll(a, k % n, 0)


def _pool_windows(hx, first_row, reverse):
    outs = []
    for gi, w in enumerate(POOL_WINDOWS):
        acc = hx[:, gi * POOL_GROUP:(gi + 1) * POOL_GROUP]
        k = 1
        while k < w:
            acc = acc + _roll_rows(acc, -k if reverse else k)
            k *= 2
        outs.append(acc)
    return outs


def _pool_cnt(t_idx, w):
    return jnp.minimum(t_idx + 1, w).astype(f32)


def _pool_pooled(x_ref, xp_ref, g, scale, shift, i, tm):
    h = _adaln(x_ref[...], g, scale, shift)
    hp = _adaln(xp_ref[...], g, scale, shift)
    hp = jnp.where(i == 0, 0.0, hp)
    hx = jnp.concatenate([hp, h], axis=0)
    sums = _pool_windows(hx, 0, False)
    t_idx = i * tm + lax.broadcasted_iota(jnp.int32, (tm, 1), 0)
    pooled = []
    for gi, w in enumerate(POOL_WINDOWS):
        s = sums[gi][POOL_HALO:, :]
        pooled.append(s / _pool_cnt(t_idx, w) - h[:, gi * POOL_GROUP:(gi + 1) * POOL_GROUP])
    return h, pooled


def _pool_specs(t, tm):
    per = tm // POOL_HALO
    prev = pl.BlockSpec((POOL_HALO, D), lambda i: (jnp.maximum(i * per - 1, 0), 0))
    return [_tok_spec(tm, D), prev, _row_spec(D), _row_spec(D), _row_spec(D),
            pl.BlockSpec((4, POOL_GROUP, POOL_GROUP), lambda i: (0, 0, 0)), _row_spec(D), _row_spec(D)]


def _pool_fwd(x, ln, w, pscale):
    g, shift, scale, gate = ln
    t = x.shape[0]
    tm = _tok_tile(t)

    def body(x_ref, xp_ref, g_ref, sc_ref, sh_ref, w_ref, ps_ref, gt_ref, y_ref):
        i = pl.program_id(0)
        _, pooled = _pool_pooled(x_ref, xp_ref, g_ref[...], sc_ref[...], sh_ref[...], i, tm)
        mixed = [jnp.dot(pooled[gi].astype(bf16), w_ref[gi], preferred_element_type=f32) for gi in range(4)]
        y = jnp.concatenate(mixed, axis=1) * ps_ref[...]
        y_ref[...] = x_ref[...] + (1.0 + gt_ref[...]) * y

    return pl.pallas_call(
        body, name="pool_fwd", grid=(t // tm,), in_specs=_pool_specs(t, tm),
        out_specs=_tok_spec(tm, D), out_shape=jax.ShapeDtypeStruct((t, D), f32),
        compiler_params=_cp(1),
    )(x, x, g, scale, shift, w, pscale, gate)


def _pool_bwd(dxn, x, ln, w, pscale):
    g, shift, scale, gate = ln
    t = x.shape[0]
    tm = _tok_tile(t)
    nt = t // tm
    per = tm // POOL_HALO

    def body_a(x_ref, xp_ref, g_ref, sc_ref, sh_ref, w_ref, ps_ref, gt_ref, dxn_ref,
               dp_ref, dw_ref, dps_ref, dgt_ref):
        i = pl.program_id(0)
        first = i == 0
        _, pooled = _pool_pooled(x_ref, xp_ref, g_ref[...], sc_ref[...], sh_ref[...], i, tm)
        dxn_v = dxn_ref[...]
        dy = (1.0 + gt_ref[...]) * dxn_v
        dmixed = dy * ps_ref[...]
        mixed, dps = [], []
        for gi in range(4):
            sl = slice(gi * POOL_GROUP, (gi + 1) * POOL_GROUP)
            pb = pooled[gi].astype(bf16)
            dmb = dmixed[:, sl].astype(bf16)
            mixed.append(jnp.dot(pb, w_ref[gi], preferred_element_type=f32))
            dp_ref[:, sl] = lax.dot_general(dmb, w_ref[gi], _DN["nt"], preferred_element_type=f32)
            dwg = lax.dot_general(pb, dmb, _DN["tn"], preferred_element_type=f32)

            @pl.when(first)
            def _():
                dw_ref[gi] = dwg

            @pl.when(jnp.logical_not(first))
            def _():
                dw_ref[gi] += dwg
        mixed = jnp.concatenate(mixed, axis=1)
        _acc_add(first, dps_ref, jnp.sum(dy * mixed, axis=0, keepdims=True))
        _acc_add(first, dgt_ref, jnp.sum(dxn_v * (mixed * ps_ref[...]), axis=0, keepdims=True))

    row = jax.ShapeDtypeStruct((1, D), f32)
    dpooled, dw, dps, dgate = pl.pallas_call(
        body_a, name="pool_bwd_a", grid=(nt,), in_specs=_pool_specs(t, tm) + [_tok_spec(tm, D)],
        out_specs=[_tok_spec(tm, D), pl.BlockSpec((4, POOL_GROUP, POOL_GROUP), lambda i: (0, 0, 0)),
                   _row_spec(D), _row_spec(D)],
        out_shape=[jax.ShapeDtypeStruct((t, D), f32), jax.ShapeDtypeStruct((4, POOL_GROUP, POOL_GROUP), f32), row, row],
        compiler_params=_cp(1),
    )(x, x, g, scale, shift, w, pscale, gate, dxn)

    def body_b(dp_ref, dpn_ref, x_ref, g_ref, sc_ref, sh_ref, dxn_ref, dx_ref, dg_ref, dsc_ref, dsh_ref):
        i = pl.program_id(0)
        dp = dp_ref[...]
        dpn = jnp.where(i == nt - 1, 0.0, dpn_ref[...])
        t_idx = i * tm + lax.broadcasted_iota(jnp.int32, (tm + POOL_HALO, 1), 0)
        ex = jnp.concatenate([dp, dpn], axis=0)
        parts = []
        for gi, w_ in enumerate(POOL_WINDOWS):
            parts.append(ex[:, gi * POOL_GROUP:(gi + 1) * POOL_GROUP] / _pool_cnt(t_idx, w_))
        sums = _pool_windows(jnp.concatenate(parts, axis=1), 0, True)
        dh = jnp.concatenate([s[:tm, :] for s in sums], axis=1) - dp
        _, vjp = jax.vjp(_adaln, x_ref[...], g_ref[...], sc_ref[...], sh_ref[...])
        dx, dg, dsc, dsh = vjp(dh)
        dx_ref[...] = dxn_ref[...] + dx
        first = i == 0
        _acc_add(first, dg_ref, dg)
        _acc_add(first, dsc_ref, dsc)
        _acc_add(first, dsh_ref, dsh)

    nxt = pl.BlockSpec((POOL_HALO, D), lambda i: (jnp.minimum((i + 1) * per, t // POOL_HALO - 1), 0))
    dx, dg, dscale, dshift = pl.pallas_call(
        body_b, name="pool_bwd_b", grid=(nt,),
        in_specs=[_tok_spec(tm, D), nxt, _tok_spec(tm, D), _row_spec(D), _row_spec(D), _row_spec(D), _tok_spec(tm, D)],
        out_specs=[_tok_spec(tm, D), _row_spec(D), _row_spec(D), _row_spec(D)],
        out_shape=[jax.ShapeDtypeStruct((t, D), f32), row, row, row],
        compiler_params=_cp(1),
    )(dpooled, dpooled, x, g, scale, shift, dxn)
    return dx, (dg, dshift, dscale, dgate), dw, dps


def _conv_taps(czx, cw):
    return cw[0:1, :] * _roll_rows(czx, 2) + cw[1:2, :] * _roll_rows(czx, 1) + cw[2:3, :] * czx


def _conv_fwd(p, cw):
    t = p.shape[0]
    tm = _tok_tile(t)
    per = tm // CONV_HALO

    def body(p_ref, pp_ref, cw_ref, q_ref):
        i = pl.program_id(0)
        cz = p_ref[:, D:2 * D] * p_ref[:, 2 * D:3 * D]
        czp = jnp.where(i == 0, 0.0, pp_ref[:, D:2 * D] * pp_ref[:, 2 * D:3 * D])
        conv = _conv_taps(jnp.concatenate([czp, cz], axis=0), cw_ref[...])[CONV_HALO:, :]
        q_ref[...] = (p_ref[:, 0:D] * conv).astype(bf16)

    prev = pl.BlockSpec((CONV_HALO, 3 * D), lambda i: (jnp.maximum(i * per - 1, 0), 0))
    return pl.pallas_call(
        body, name="conv_fwd", grid=(t // tm,),
        in_specs=[_tok_spec(tm, 3 * D), prev, _row_spec(D, 8)],
        out_specs=_tok_spec(tm, D), out_shape=jax.ShapeDtypeStruct((t, D), bf16),
        compiler_params=_cp(1),
    )(p, p, cw)


def _conv_bwd(p, cw, dq):
    t = p.shape[0]
    tm = _tok_tile(t)
    nt = t // tm
    per = tm // CONV_HALO

    def body(p_ref, pp_ref, pn_ref, cw_ref, dq_ref, dqn_ref, dp_ref, dcw_ref):
        i = pl.program_id(0)
        cw_v = cw_ref[...]
        b, c, z = p_ref[:, 0:D], p_ref[:, D:2 * D], p_ref[:, 2 * D:3 * D]
        cz = c * z
        czp = jnp.where(i == 0, 0.0, pp_ref[:, D:2 * D] * pp_ref[:, 2 * D:3 * D])
        czx = jnp.concatenate([czp, cz], axis=0)
        conv = _conv_taps(czx, cw_v)[CONV_HALO:, :]
        dq_v = dq_ref[...]
        dconv = dq_v * b
        dconv_n = jnp.where(i == nt - 1, 0.0, dqn_ref[...] * pn_ref[:, 0:D])
        dcx = jnp.concatenate([dconv, dconv_n], axis=0)
        dcz = (cw_v[2:3, :] * dcx + cw_v[1:2, :] * _roll_rows(dcx, -1) + cw_v[0:1, :] * _roll_rows(dcx, -2))[:tm, :]
        dp_ref[:, 0:D] = (dq_v * conv).astype(bf16)
        dp_ref[:, D:2 * D] = (dcz * z).astype(bf16)
        dp_ref[:, 2 * D:3 * D] = (dcz * c).astype(bf16)
        dw2 = jnp.sum(dconv * cz, axis=0, keepdims=True)
        dw1 = jnp.sum(dconv * _roll_rows(czx, 1)[CONV_HALO:, :], axis=0, keepdims=True)
        dw0 = jnp.sum(dconv * _roll_rows(czx, 2)[CONV_HALO:, :], axis=0, keepdims=True)
        _acc_add(i == 0, dcw_ref, jnp.concatenate([dw0, dw1, dw2, jnp.zeros((5, D), f32)], axis=0))

    prev = pl.BlockSpec((CONV_HALO, 3 * D), lambda i: (jnp.maximum(i * per - 1, 0), 0))
    last = t // CONV_HALO - 1
    nxt3 = pl.BlockSpec((CONV_HALO, 3 * D), lambda i: (jnp.minimum((i + 1) * per, last), 0))
    nxt1 = pl.BlockSpec((CONV_HALO, D), lambda i: (jnp.minimum((i + 1) * per, last), 0))
    return pl.pallas_call(
        body, name="conv_bwd", grid=(nt,),
        in_specs=[_tok_spec(tm, 3 * D), prev, nxt3, _row_spec(D, 8), _tok_spec(tm, D), nxt1],
        out_specs=[_tok_spec(tm, 3 * D), _row_spec(D, 8)],
        out_shape=[jax.ShapeDtypeStruct((t, 3 * D), bf16), jax.ShapeDtypeStruct((8, D), f32)],
        compiler_params=_cp(1),
    )(p, p, p, cw, dq, dq)


def _conv_tile(t):
    return min(t, 2048)


def _convmix_fwd(x, ln, w_in, cw, w_out):
    g, shift, scale, gate = ln
    t = x.shape[0]
    tm = _conv_tile(t)
    cs = w_in.shape[2]
    h = _ln_fwd(x, g, scale, shift)
    p = _mmx(h, w_in, "nn", "conv_in", (t // tm, N_DEV, 1),
             pl.BlockSpec((tm, D), lambda i, j, k: (i, 0)),
             pl.BlockSpec((1, D, cs), lambda i, j, k: (j, 0, 0)),
             pl.BlockSpec((tm, cs), lambda i, j, k: (i, j)), jax.ShapeDtypeStruct((t, N_DEV * cs), f32))
    q = _conv_fwd(p, cw)
    y = _mm(q, w_out, "nn", f32, "conv_out")
    return _resid_fwd(x, y, gate, 1.0), (x, h, p, q, y)


def _convmix_bwd(dxn, saved, ln, w_in, cw, w_out):
    x, h, p, q, y = saved
    g, shift, scale, gate = ln
    dy, dgate = _resid_bwd(dxn, y, gate, 1.0)
    dq = _mm(dy, w_out, "nt", f32, "conv_dq")
    dw_out = _mm(q, dy, "tn", f32, "conv_dwout")
    dp, dcw = _conv_bwd(p, cw, dq)
    t = x.shape[0]
    tm = _conv_tile(t)
    tk = _pick(t, (2048, 1024, 512, 256, 128))
    cs = w_in.shape[2]
    dh = _mmx(dp, w_in, "nt", "conv_dh", (t // tm, 1, N_DEV),
              pl.BlockSpec((tm, cs), lambda i, j, k: (i, k)),
              pl.BlockSpec((1, D, cs), lambda i, j, k: (k, 0, 0)),
              pl.BlockSpec((tm, D), lambda i, j, k: (i, 0)), jax.ShapeDtypeStruct((t, D), f32))
    dw_in = _mmx(h, dp, "tn", "conv_dwin", (N_DEV, 1, t // tk),
                 pl.BlockSpec((tk, D), lambda i, j, k: (k, 0)),
                 pl.BlockSpec((tk, cs), lambda i, j, k: (k, i)),
                 pl.BlockSpec((1, D, cs), lambda i, j, k: (i, 0, 0)), jax.ShapeDtypeStruct((N_DEV, D, cs), f32))
    dx, dg, dscale, dshift = _ln_bwd(x, g, scale, shift, dh, dxn)
    return dx, (dg, dshift, dscale, dgate), dw_in, dcw[0:3], dw_out


def _exact_dot(tri, v):
    v1 = v.astype(bf16)
    r1 = v - v1.astype(f32)
    v2 = r1.astype(bf16)
    v3 = (r1 - v2.astype(f32)).astype(bf16)
    d = lambda p: jnp.dot(tri, p, preferred_element_type=f32)
    return d(v1) + d(v2) + d(v3)


def _fox_cumf(fl, b_f):
    t = fl.shape[0]
    tc = min(t, 256)

    def body(fl_ref, b_ref, f_ref, carry_ref):
        i = pl.program_id(0)

        @pl.when(i == 0)
        def _():
            carry_ref[...] = jnp.zeros_like(carry_ref)

        lf = jax.nn.log_sigmoid(fl_ref[...] + b_ref[...])
        r = lax.broadcasted_iota(jnp.int32, (tc, tc), 0)
        c = lax.broadcasted_iota(jnp.int32, (tc, tc), 1)
        tri = (r >= c).astype(bf16)
        fc = _exact_dot(tri, lf) + carry_ref[0:1, :]
        f_ref[...] = fc
        carry_ref[0:1, :] = fc[tc - 1:tc, :]

    return pl.pallas_call(
        body, name="fox_cumf", grid=(t // tc,),
        in_specs=[_tok_spec(tc, 128), _row_spec(128)],
        out_specs=_tok_spec(tc, 128), out_shape=jax.ShapeDtypeStruct((t, 128), f32),
        scratch_shapes=[pltpu.VMEM((8, 128), f32)], compiler_params=_cp(1),
    )(fl, b_f)


def _fox_cumf_bwd(df, fl, b_f):
    t = fl.shape[0]
    tc = min(t, 256)
    nt = t // tc

    def body(df_ref, fl_ref, b_ref, dfl_ref, db_ref, carry_ref):
        i = pl.program_id(0)

        @pl.when(i == 0)
        def _():
            carry_ref[...] = jnp.zeros_like(carry_ref)

        r = lax.broadcasted_iota(jnp.int32, (tc, tc), 0)
        c = lax.broadcasted_iota(jnp.int32, (tc, tc), 1)
        tri = (r <= c).astype(bf16)
        dlf = _exact_dot(tri, df_ref[...]) + carry_ref[0:1, :]
        carry_ref[0:1, :] = dlf[0:1, :]
        dfl = dlf * jax.nn.sigmoid(-(fl_ref[...] + b_ref[...]))
        dfl_ref[...] = dfl
        _acc_add(i == 0, db_ref, jnp.sum(dfl, axis=0, keepdims=True))

    return pl.pallas_call(
        body, name="fox_cumf_bwd", grid=(nt,),
        in_specs=[_tok_spec(tc, 128, nt, True), _tok_spec(tc, 128, nt, True), _row_spec(128)],
        out_specs=[_tok_spec(tc, 128, nt, True), _row_spec(128)],
        out_shape=[jax.ShapeDtypeStruct((t, 128), f32), jax.ShapeDtypeStruct((1, 128), f32)],
        scratch_shapes=[pltpu.VMEM((8, 128), f32)], compiler_params=_cp(1),
    )(df, fl, b_f)


def _fox_block(t):
    return min(t, 512)


FOX_STRIP = 64


def _fox_mask(s, row0):
    r = row0 + lax.broadcasted_iota(jnp.int32, s.shape, 0)
    c = lax.broadcasted_iota(jnp.int32, s.shape, 1)
    return jnp.where(r >= c, s, -jnp.inf)


def _strips(tb):
    ts = min(tb, FOX_STRIP)
    return [(r * ts, slice(r * ts, (r + 1) * ts)) for r in range(tb // ts)]


FOX_PAIRS = FOX_HEADS // 2
FOX_PW = 2 * FOX_HEAD_DIM


def _half(shape, hh):
    lane = lax.broadcasted_iota(jnp.int32, shape, len(shape) - 1)
    return lane < FOX_HEAD_DIM if hh == 0 else lane >= FOX_HEAD_DIM


def _by_half(v0, v1):
    return jnp.where(_half(v0.shape, 0), v0, v1)


def _half_sums(v):
    first = _half(v.shape, 0)
    s0 = jnp.sum(jnp.where(first, v, 0.0), axis=-1, keepdims=True)
    s1 = jnp.sum(jnp.where(first, 0.0, v), axis=-1, keepdims=True)
    return jnp.where(first, s0, s1)


def _pair_norm(v, gain, mult):
    ms = _half_sums(v * v) * (1.0 / FOX_HEAD_DIM)
    return v * lax.rsqrt(ms + NORM_EPS) * gain * mult


def _pair_cols(p):
    return slice(p * FOX_PW, (p + 1) * FOX_PW)


def _pair_qknorm(proj, q_gain, k_gain):
    t = proj.shape[0]
    tm = _tok_tile(t)

    def body(q_ref, k_ref, v_ref, qg_ref, kg_ref, o_ref):
        for p in range(FOX_PAIRS):
            cb = _pair_cols(p)
            o_ref[:, cb] = _pair_norm(q_ref[:, cb], qg_ref[...], FOX_HEAD_DIM ** -0.5).astype(bf16)
            o_ref[:, D + p * FOX_PW:D + (p + 1) * FOX_PW] = _pair_norm(k_ref[:, cb], kg_ref[...], 1.0).astype(bf16)
        o_ref[:, 2 * D:3 * D] = v_ref[...].astype(bf16)

    part = lambda s: pl.BlockSpec((tm, D), lambda i: (i, s))
    return pl.pallas_call(
        body, name="fox_qknorm", grid=(t // tm,),
        in_specs=[part(0), part(1), part(2), _row_spec(FOX_PW), _row_spec(FOX_PW)],
        out_specs=_tok_spec(tm, 3 * D), out_shape=jax.ShapeDtypeStruct((t, 3 * D), bf16), compiler_params=_cp(1),
    )(proj, proj, proj, q_gain, k_gain)


def _pair_qknorm_bwd(proj, q_gain, k_gain, dqn, dkn):
    t = proj.shape[0]
    tm = _tok_tile(t)

    def body(q_ref, k_ref, qg_ref, kg_ref, dqn_ref, dkn_ref, o_ref, dqg_ref, dkg_ref):
        dqg = jnp.zeros((1, FOX_PW), f32)
        dkg = jnp.zeros((1, FOX_PW), f32)
        for p in range(FOX_PAIRS):
            cb = _pair_cols(p)
            _, vq = jax.vjp(lambda v, gn: _pair_norm(v, gn, FOX_HEAD_DIM ** -0.5), q_ref[:, cb], qg_ref[...])
            dq, g1 = vq(dqn_ref[:, cb])
            _, vk = jax.vjp(lambda v, gn: _pair_norm(v, gn, 1.0), k_ref[:, cb], kg_ref[...])
            dk, g2 = vk(dkn_ref[:, cb])
            o_ref[:, cb] = dq.astype(bf16)
            o_ref[:, D + p * FOX_PW:D + (p + 1) * FOX_PW] = dk.astype(bf16)
            dqg, dkg = dqg + g1, dkg + g2
        first = pl.program_id(0) == 0
        _acc_add(first, dqg_ref, dqg + pltpu.roll(dqg, FOX_HEAD_DIM, 1))
        _acc_add(first, dkg_ref, dkg + pltpu.roll(dkg, FOX_HEAD_DIM, 1))

    part = lambda s: pl.BlockSpec((tm, D), lambda i: (i, s))
    gshape = jax.ShapeDtypeStruct((1, FOX_PW), f32)
    return pl.pallas_call(
        body, name="fox_qknorm_bwd", grid=(t // tm,),
        in_specs=[part(0), part(1), _row_spec(FOX_PW), _row_spec(FOX_PW), _tok_spec(tm, D), _tok_spec(tm, D)],
        out_specs=[_tok_spec(tm, 2 * D), _row_spec(FOX_PW), _row_spec(FOX_PW)],
        out_shape=[jax.ShapeDtypeStruct((t, 2 * D), bf16), gshape, gshape], compiler_params=_cp(1),
    )(proj, proj, q_gain, k_gain, dqn, dkn)


def _pair_delta(o, do):
    t = o.shape[0]
    tm = _tok_tile(t)

    def body(o_ref, do_ref, dl_ref, dob_ref):
        dob = do_ref[...].astype(bf16)
        dob_ref[...] = dob
        prod = dob.astype(f32) * o_ref[...]
        for p in range(FOX_PAIRS):
            dl_ref[p] = _half_sums(prod[:, _pair_cols(p)])

    return pl.pallas_call(
        body, name="fox_delta", grid=(t // tm,), in_specs=[_tok_spec(tm, D), _tok_spec(tm, D)],
        out_specs=[pl.BlockSpec((FOX_PAIRS, tm, FOX_PW), lambda i: (0, i, 0)), _tok_spec(tm, D)],
        out_shape=[jax.ShapeDtypeStruct((FOX_PAIRS, t, FOX_PW), f32), jax.ShapeDtypeStruct((t, D), bf16)],
        compiler_params=_cp(1),
    )(o, do)


def _pair_attn_fwd(qkvn, fcol, fref):
    t = qkvn.shape[0]
    tb = _fox_block(t)
    nq = t // tb
    pw = FOX_PW
    kcol, vcol = D // pw, 2 * D // pw

    def body(q_ref, k_ref, v_ref, fc_ref, fr_ref, o_ref, ob_ref, lse_ref):
        i = pl.program_id(1)
        qp = q_ref[...]
        qh = [jnp.where(_half(qp.shape, hh), qp, jnp.zeros_like(qp)) for hh in range(2)]

        def step(j, carry, diag):
            off = pl.multiple_of(j * tb, tb)
            kj = k_ref[pl.ds(off, tb), :]
            vj = v_ref[pl.ds(off, tb), :]
            out = []
            for hh in range(2):
                m, l, acc = carry[hh]
                bias = fr_ref[0, hh, pl.ds(i, 1), 0:1] - fc_ref[0, hh:hh + 1, pl.ds(off, tb)]
                s = lax.dot_general(qh[hh], kj, _DN["nt"], preferred_element_type=f32) + bias
                if diag:
                    s = _fox_mask(s, 0)
                m_new = jnp.maximum(m, jnp.max(s, axis=-1, keepdims=True))
                alpha = jnp.exp(m - m_new)
                p = jnp.exp(s - m_new)
                l = alpha * l + jnp.sum(p, axis=-1, keepdims=True)
                p_hi = p.astype(bf16)
                p_lo = (p - p_hi.astype(f32)).astype(bf16)
                pv = jnp.dot(p_hi, vj, preferred_element_type=f32) + jnp.dot(p_lo, vj, preferred_element_type=f32)
                out.append((m_new, l, alpha * acc + pv))
            return tuple(out)

        one = (jnp.full((tb, 1), -jnp.inf, f32), jnp.zeros((tb, 1), f32), jnp.zeros((tb, pw), f32))
        carry = lax.fori_loop(0, i, lambda j, c: step(j, c, False), (one, one))
        (m0, l0, a0), (m1, l1, a1) = step(i, carry, True)
        o = _by_half(a0 / l0, a1 / l1)
        o_ref[...] = o
        ob_ref[...] = o.astype(bf16)
        lse_ref[0] = _by_half(jnp.broadcast_to(m0 + jnp.log(l0), (tb, pw)), jnp.broadcast_to(m1 + jnp.log(l1), (tb, pw)))

    return pl.pallas_call(
        body, name="fox_attn_fwd", grid=(FOX_PAIRS, nq),
        in_specs=[pl.BlockSpec((tb, pw), lambda p, i: (i, p)),
                  pl.BlockSpec((t, pw), lambda p, i: (0, kcol + p)),
                  pl.BlockSpec((t, pw), lambda p, i: (0, vcol + p)),
                  pl.BlockSpec((1, 2, t), lambda p, i: (p, 0, 0)),
                  pl.BlockSpec((1, 2, nq, 128), lambda p, i: (p, 0, 0, 0))],
        out_specs=[pl.BlockSpec((tb, pw), lambda p, i: (i, p)), pl.BlockSpec((tb, pw), lambda p, i: (i, p)),
                   pl.BlockSpec((1, tb, pw), lambda p, i: (p, i, 0))],
        out_shape=[jax.ShapeDtypeStruct((t, D), f32), jax.ShapeDtypeStruct((t, D), bf16),
                   jax.ShapeDtypeStruct((FOX_PAIRS, t, pw), f32)],
        compiler_params=_cp(2),
    )(qkvn, qkvn, qkvn, fcol, fref)


def _pair_attn_bwd(qkvn, fcol, fref, lse, delta, dob):
    t = qkvn.shape[0]
    tb = _fox_block(t)
    nq = t // tb
    pw = FOX_PW
    kcol, vcol = D // pw, 2 * D // pw

    def body(q_ref, k_ref, v_ref, fc_ref, fr_ref, lse_ref, dl_ref, do_ref, dk_ref, dv_ref, df_ref, dq_ref,
             s_ref, dp_ref, p_ref, ds_ref):
        j = pl.program_id(1)
        kj = k_ref[...]
        vj = v_ref[...]
        dk_ref[...] = jnp.zeros_like(dk_ref)
        dv_ref[...] = jnp.zeros_like(dv_ref)
        df_ref[...] = jnp.zeros_like(df_ref)

        @pl.when(j == 0)
        def _():
            dq_ref[...] = jnp.zeros_like(dq_ref)

        def step(i, diag):
            off = pl.multiple_of(i * tb, tb)
            qi = q_ref[pl.ds(off, tb), :]
            doi = do_ref[pl.ds(off, tb), :]
            parts = []
            for hh in range(2):
                mine = _half(qi.shape, hh)
                c0 = hh * FOX_HEAD_DIM
                s_ref[hh] = lax.dot_general(jnp.where(mine, qi, jnp.zeros_like(qi)), kj, _DN["nt"],
                                            preferred_element_type=f32)
                dp_ref[hh] = lax.dot_general(jnp.where(mine, doi, jnp.zeros_like(doi)), vj, _DN["nt"],
                                             preferred_element_type=f32)
                bias = fr_ref[0, hh, pl.ds(i, 1), 0:1] - fc_ref[0, hh:hh + 1, :]
                df = jnp.zeros((1, tb), f32)
                for row0, rs in _strips(tb):
                    rows = pl.ds(off + row0, rs.stop - rs.start)
                    s = s_ref[hh, rs, :] + bias
                    if diag:
                        s = _fox_mask(s, row0)
                    p = jnp.exp(s - lse_ref[0, rows, c0:c0 + 1])
                    p_ref[hh, rs, :] = p.astype(bf16)
                    ds = p * (dp_ref[hh, rs, :] - dl_ref[0, rows, c0:c0 + 1])
                    ds_ref[hh, rs, :] = ds.astype(bf16)
                    df = df + jnp.sum(ds, axis=0, keepdims=True)
                df_ref[0, hh:hh + 1, :] -= df
                parts.append((lax.dot_general(p_ref[hh], doi, _DN["tn"], preferred_element_type=f32),
                              lax.dot_general(ds_ref[hh], qi, _DN["tn"], preferred_element_type=f32),
                              jnp.dot(ds_ref[hh], kj, preferred_element_type=f32)))
            dv_ref[...] += _by_half(parts[0][0], parts[1][0])
            dk_ref[...] += _by_half(parts[0][1], parts[1][1])
            dq_ref[pl.ds(off, tb), :] += _by_half(parts[0][2], parts[1][2])

        def loop_body(i, carry):
            step(i, False)
            return carry

        step(j, True)
        lax.fori_loop(j + 1, nq, loop_body, 0)

    scratch = [pltpu.VMEM((2, tb, tb), f32), pltpu.VMEM((2, tb, tb), f32), pltpu.VMEM((2, tb, tb), bf16),
               pltpu.VMEM((2, tb, tb), bf16)]
    blk = lambda c0: pl.BlockSpec((tb, pw), lambda p, j: (j, c0 + p))
    full = lambda c0: pl.BlockSpec((t, pw), lambda p, j: (0, c0 + p))
    stat = pl.BlockSpec((1, t, pw), lambda p, j: (p, 0, 0))
    nat = jax.ShapeDtypeStruct((t, D), f32)
    return pl.pallas_call(
        body, name="fox_attn_bwd", grid=(FOX_PAIRS, nq),
        in_specs=[full(0), blk(kcol), blk(vcol), pl.BlockSpec((1, 2, tb), lambda p, j: (p, 0, j)),
                  pl.BlockSpec((1, 2, nq, 128), lambda p, j: (p, 0, 0, 0)), stat, stat, full(0)],
        out_specs=[blk(0), blk(0), pl.BlockSpec((1, 2, tb), lambda p, j: (p, 0, j)), full(0)],
        out_shape=[nat, nat, jax.ShapeDtypeStruct((FOX_PAIRS, 2, t), f32), nat],
        scratch_shapes=scratch, compiler_params=_cp(2),
    )(qkvn, qkvn, qkvn, fcol, fref, lse, delta, dob)


def _fox_fwd(x, ln, w_in, b_f, q_gain, k_gain, w_o):
    g, shift, scale, gate = ln
    t = x.shape[0]
    tb = _fox_block(t)
    h = _ln_fwd(x, g, scale, shift)
    proj = _mm(h, w_in, "nn", f32, "fox_in")
    fl = proj[:, 3 * D:3 * D + 128]
    fcum = _fox_cumf(fl, b_f)
    fcol = fcum[:, :FOX_HEADS].T.reshape(FOX_PAIRS, 2, t)
    fref = jnp.broadcast_to(fcol[:, :, ::tb][..., None], (FOX_PAIRS, 2, t // tb, 128))
    gains = (jnp.tile(q_gain, (1, 2)), jnp.tile(k_gain, (1, 2)))
    qkvn = _pair_qknorm(proj, *gains)
    o, ob, lse = _pair_attn_fwd(qkvn, fcol, fref)
    y = _mm(ob, w_o, "nn", f32, "fox_out")
    return _resid_fwd(x, y, gate, 1.0), (x, h, proj, fl, fcol, fref, gains, qkvn, o, lse, ob, y)


def _fox_bwd(dxn, saved, ln, w_in, b_f, w_o):
    x, h, proj, fl, fcol, fref, gains, qkvn, o, lse, ob, y = saved
    g, shift, scale, gate = ln
    t = x.shape[0]
    dy, dgate = _resid_bwd(dxn, y, gate, 1.0)
    do = _mm(dy, w_o, "nt", f32, "fox_do")
    dw_o = _mm(ob, dy, "tn", f32, "fox_dwo")
    delta, dob = _pair_delta(o, do)
    dkn, dv, dfcol, dqn = _pair_attn_bwd(qkvn, fcol, fref, lse, delta, dob)
    dqk, dqg, dkg = _pair_qknorm_bwd(proj, *gains, dqn, dkn)
    df = jnp.pad(dfcol.reshape(FOX_HEADS, t).T, ((0, 0), (0, 128 - FOX_HEADS)))
    dfl, db_f = _fox_cumf_bwd(df, fl, b_f)
    dproj = jnp.concatenate([dqk, dv.astype(bf16), dfl.astype(bf16)], axis=1)
    dh = _mm(dproj, w_in, "nt", f32, "fox_dh")
    dw_in = _mm(h, dproj, "tn", f32, "fox_dwin")
    dx, dg, dscale, dshift = _ln_bwd(x, g, scale, shift, dh, dxn)
    return (dx, (dg, dshift, dscale, dgate), dw_in, db_f, dqg[:, :FOX_HEAD_DIM], dkg[:, :FOX_HEAD_DIM], dw_o)


def _s5_disc(lam_re, lam_im, log_dt, b_re, b_im):
    dt = jnp.exp(log_dt)
    mag = jnp.exp(lam_re * dt)
    lb_re, lb_im = mag * jnp.cos(lam_im * dt), mag * jnp.sin(lam_im * dt)
    den = lam_re * lam_re + lam_im * lam_im
    nr, ni = lb_re - 1.0, lb_im
    k_re = (nr * lam_re + ni * lam_im) / den
    k_im = (ni * lam_re - nr * lam_im) / den
    return lb_re, lb_im, k_re * b_re - k_im * b_im, k_re * b_im + k_im * b_re


def _s5_prep(lam_re, lam_im, log_dt, b_re, b_im):
    def body(ar_ref, ai_ref, dt_ref, br_ref, bi_ref, lr_ref, li_ref, bbr_ref, bbi_ref):
        lr, li, bbr, bbi = _s5_disc(ar_ref[...], ai_ref[...], dt_ref[...], br_ref[...], bi_ref[...])
        lr_ref[...] = lr
        li_ref[...] = li
        bbr_ref[...] = bbr
        bbi_ref[...] = bbi

    small = jax.ShapeDtypeStruct(lam_re.shape, f32)
    bigs = jax.ShapeDtypeStruct(b_re.shape, f32)
    return pl.pallas_call(body, name="s5_prep", out_shape=[small, small, bigs, bigs])(lam_re, lam_im, log_dt, b_re, b_im)


def _s5_prep_bwd(lam_re, lam_im, log_dt, b_re, b_im, dlr, dli, dbbr, dbbi):
    def body(ar_ref, ai_ref, dt_ref, br_ref, bi_ref, dlr_ref, dli_ref, dbbr_ref, dbbi_ref,
             dar_ref, dai_ref, ddt_ref, dbr_ref, dbi_ref):
        _, vjp = jax.vjp(_s5_disc, ar_ref[...], ai_ref[...], dt_ref[...], br_ref[...], bi_ref[...])
        dar, dai, ddt, dbr, dbi = vjp((dlr_ref[...], dli_ref[...], dbbr_ref[...], dbbi_ref[...]))
        dar_ref[...] = dar
        dai_ref[...] = dai
        ddt_ref[...] = jnp.broadcast_to(jnp.sum(ddt, axis=-1, keepdims=True), ddt.shape)
        dbr_ref[...] = dbr
        dbi_ref[...] = dbi

    small = jax.ShapeDtypeStruct(lam_re.shape, f32)
    bigs = jax.ShapeDtypeStruct(b_re.shape, f32)
    return pl.pallas_call(body, name="s5_prep_bwd", out_shape=[small, small, small, bigs, bigs])(
        lam_re, lam_im, log_dt, b_re, b_im, dlr, dli, dbbr, dbbi)


def _s5_tile(t):
    return min(t, 128)


def _s5_fwd_tile(t):
    return min(t, 256)


def _s5_blk(k, width):
    return slice(k * width, (k + 1) * width)


def _s5_in_bd(bb):
    b4 = bb.reshape(S5_BLOCKS, 8, S5_GROUP, S5_STATE)
    return jnp.einsum("kgin,gh->kgihn", b4, jnp.eye(8, dtype=bb.dtype)).reshape(S5_BLOCKS, S5_BCH, S5_BST)


def _s5_in_bd_diag(bd):
    b5 = bd.reshape(S5_BLOCKS, 8, S5_GROUP, 8, S5_STATE)
    return jnp.einsum("kgihn,gh->kgin", b5, jnp.eye(8, dtype=bd.dtype)).reshape(S5_GROUPS, S5_GROUP, S5_STATE)


def _s5_out_bd(c):
    c4 = c.reshape(S5_BLOCKS, 8, S5_GROUP, S5_STATE)
    return jnp.einsum("kgin,gh->kgnhi", c4, jnp.eye(8, dtype=c.dtype)).reshape(S5_BLOCKS, S5_BST, S5_BCH)


def _s5_out_bd_diag(bd):
    c5 = bd.reshape(S5_BLOCKS, 8, S5_STATE, 8, S5_GROUP)
    return jnp.einsum("kgnhi,gh->kgin", c5, jnp.eye(8, dtype=bd.dtype)).reshape(S5_GROUPS, S5_GROUP, S5_STATE)


def _s5_scan_fwd(x, ln, lb_re, lb_im, bbr_bd, bbi_bd, cr_bd, ci_bd, dskip):
    g, shift, scale, _ = ln
    t = x.shape[0]
    tm = _s5_fwd_tile(t)
    ns = S5_NSTATE

    def body(x_ref, g_ref, sc_ref, sh_ref, ar_ref, ai_ref, bbr_ref, bbi_ref, cr_ref, ci_ref, d_ref,
             yy_ref, xr_ref, xi_ref, cre_ref, cim_ref):
        @pl.when(pl.program_id(0) == 0)
        def _():
            cre_ref[...] = jnp.zeros_like(cre_ref)
            cim_ref[...] = jnp.zeros_like(cim_ref)

        h = _adaln(x_ref[...], g_ref[...], sc_ref[...], sh_ref[...])
        ub = h.astype(bf16)
        for k in range(S5_BLOCKS):
            uk = ub[:, _s5_blk(k, S5_BCH)]
            xr_ref[:, _s5_blk(k, S5_BST)] = jnp.dot(uk, bbr_ref[k], preferred_element_type=f32)
            xi_ref[:, _s5_blk(k, S5_BST)] = jnp.dot(uk, bbi_ref[k], preferred_element_type=f32)
        ar, ai = ar_ref[...], ai_ref[...]

        def step(tt, carry):
            sr, si = carry
            row = pl.ds(tt, 1)
            nr = (ar * sr - ai * si) + xr_ref[row, :]
            ni = (ar * si + ai * sr) + xi_ref[row, :]
            xr_ref[row, :] = nr
            xi_ref[row, :] = ni
            return nr, ni

        sr, si = lax.fori_loop(0, tm, step, (cre_ref[0:1, :], cim_ref[0:1, :]), unroll=2)
        cre_ref[0:1, :] = sr
        cim_ref[0:1, :] = si
        for k in range(S5_BLOCKS):
            sb = _s5_blk(k, S5_BST)
            yk = (jnp.dot(xr_ref[:, sb].astype(bf16), cr_ref[k], preferred_element_type=f32)
                  - jnp.dot(xi_ref[:, sb].astype(bf16), ci_ref[k], preferred_element_type=f32))
            cb = _s5_blk(k, S5_BCH)
            yy_ref[:, cb] = yk + d_ref[:, cb] * h[:, cb]

    bd_in = pl.BlockSpec((S5_BLOCKS, S5_BCH, S5_BST), lambda i: (0, 0, 0))
    bd_out = pl.BlockSpec((S5_BLOCKS, S5_BST, S5_BCH), lambda i: (0, 0, 0))
    st = jax.ShapeDtypeStruct((t, ns), f32)
    return pl.pallas_call(
        body, name="s5_scan_fwd", grid=(t // tm,),
        in_specs=[_tok_spec(tm, D), _row_spec(D), _row_spec(D), _row_spec(D), _row_spec(ns), _row_spec(ns),
                  bd_in, bd_in, bd_out, bd_out, _row_spec(D)],
        out_specs=[_tok_spec(tm, D), _tok_spec(tm, ns), _tok_spec(tm, ns)],
        out_shape=[jax.ShapeDtypeStruct((t, D), f32), st, st],
        scratch_shapes=[pltpu.VMEM((8, ns), f32), pltpu.VMEM((8, ns), f32)],
        compiler_params=_cp(1),
    )(x, g, scale, shift, lb_re, lb_im, bbr_bd, bbi_bd, cr_bd, ci_bd, dskip)


def _s5_scan_bwd(dyy, x, ln, xr, xi, lb_re, lb_im, bbr_bd, bbi_bd, cr_bd, ci_bd, dskip):
    g, shift, scale, _ = ln
    t = x.shape[0]
    tm = _s5_tile(t)
    nt = t // tm
    ns = S5_NSTATE
    per = tm // 8

    def body(dyy_ref, x_ref, g_ref, sc_ref, sh_ref, xr_ref, xi_ref, xrp_ref, xip_ref, ar_ref, ai_ref,
             bbr_ref, bbi_ref, cr_ref, ci_ref, d_ref,
             du_ref, dar_ref, dai_ref, dbbr_ref, dbbi_ref, dcr_ref, dci_ref, dd_ref,
             gr_ref, gi_ref, cre_ref, cim_ref):
        i = pl.program_id(0)
        first = i == 0

        @pl.when(first)
        def _():
            cre_ref[...] = jnp.zeros_like(cre_ref)
            cim_ref[...] = jnp.zeros_like(cim_ref)

        h = _adaln(x_ref[...], g_ref[...], sc_ref[...], sh_ref[...])
        ub = h.astype(bf16)
        dyy_v = dyy_ref[...]
        dyb = dyy_v.astype(bf16)
        for k in range(S5_BLOCKS):
            dk = dyb[:, _s5_blk(k, S5_BCH)]
            sb = _s5_blk(k, S5_BST)
            gr_ref[:, sb] = lax.dot_general(dk, cr_ref[k], _DN["nt"], preferred_element_type=f32)
            gi_ref[:, sb] = -lax.dot_general(dk, ci_ref[k], _DN["nt"], preferred_element_type=f32)
        ar, ai = ar_ref[...], ai_ref[...]

        def step(s, carry):
            nr_, ni_ = carry
            row = pl.ds(tm - 1 - s, 1)
            nr = gr_ref[row, :] + (ar * nr_ + ai * ni_)
            ni = gi_ref[row, :] + (ar * ni_ - ai * nr_)
            gr_ref[row, :] = nr
            gi_ref[row, :] = ni
            return nr, ni

        nr, ni = lax.fori_loop(0, tm, step, (cre_ref[0:1, :], cim_ref[0:1, :]), unroll=2)
        cre_ref[0:1, :] = nr
        cim_ref[0:1, :] = ni

        is_first_tile = i == nt - 1
        _acc_add(first, dd_ref, jnp.sum(dyy_v * h, axis=0, keepdims=True))
        for k in range(S5_BLOCKS):
            cb, sb = _s5_blk(k, S5_BCH), _s5_blk(k, S5_BST)
            xr_v, xi_v = xr_ref[:, sb], xi_ref[:, sb]
            xrp = jnp.where(is_first_tile, 0.0, xrp_ref[:, sb])
            xip = jnp.where(is_first_tile, 0.0, xip_ref[:, sb])
            xr_s = _roll_rows(jnp.concatenate([xrp, xr_v], axis=0), 1)[8:, :]
            xi_s = _roll_rows(jnp.concatenate([xip, xi_v], axis=0), 1)[8:, :]
            gr, gi = gr_ref[:, sb], gi_ref[:, sb]
            dar_k = jnp.sum(gr * xr_s + gi * xi_s, axis=0, keepdims=True)
            dai_k = jnp.sum(gi * xr_s - gr * xi_s, axis=0, keepdims=True)

            @pl.when(first)
            def _():
                dar_ref[:, sb] = dar_k
                dai_ref[:, sb] = dai_k

            @pl.when(jnp.logical_not(first))
            def _():
                dar_ref[:, sb] += dar_k
                dai_ref[:, sb] += dai_k

            grb, gib = gr.astype(bf16), gi.astype(bf16)
            uk, dk = ub[:, cb], dyb[:, cb]
            tn = lambda a_, b_: lax.dot_general(a_, b_, _DN["tn"], preferred_element_type=f32)
            vals = (tn(uk, grb), tn(uk, gib), tn(xr_v.astype(bf16), dk), -tn(xi_v.astype(bf16), dk))
            for ref, val in zip((dbbr_ref, dbbi_ref, dcr_ref, dci_ref), vals):
                @pl.when(first)
                def _():
                    ref[k] = val

                @pl.when(jnp.logical_not(first))
                def _():
                    ref[k] += val
            du_k = (lax.dot_general(grb, bbr_ref[k], _DN["nt"], preferred_element_type=f32)
                    + lax.dot_general(gib, bbi_ref[k], _DN["nt"], preferred_element_type=f32))
            du_ref[:, cb] = du_k + d_ref[:, cb] * dyy_v[:, cb]

    rev = lambda c: _tok_spec(tm, c, nt, True)
    prev = pl.BlockSpec((8, ns), lambda i: (jnp.maximum((nt - 1 - i) * per - 1, 0), 0))
    bd_in = pl.BlockSpec((S5_BLOCKS, S5_BCH, S5_BST), lambda i: (0, 0, 0))
    bd_out = pl.BlockSpec((S5_BLOCKS, S5_BST, S5_BCH), lambda i: (0, 0, 0))
    row_ns = jax.ShapeDtypeStruct((1, ns), f32)
    bd_in_s = jax.ShapeDtypeStruct((S5_BLOCKS, S5_BCH, S5_BST), f32)
    bd_out_s = jax.ShapeDtypeStruct((S5_BLOCKS, S5_BST, S5_BCH), f32)
    return pl.pallas_call(
        body, name="s5_scan_bwd", grid=(nt,),
        in_specs=[rev(D), rev(D), _row_spec(D), _row_spec(D), _row_spec(D), rev(ns), rev(ns), prev, prev,
                  _row_spec(ns), _row_spec(ns), bd_in, bd_in, bd_out, bd_out, _row_spec(D)],
        out_specs=[rev(D), _row_spec(ns), _row_spec(ns), bd_in, bd_in, bd_out, bd_out, _row_spec(D)],
        out_shape=[jax.ShapeDtypeStruct((t, D), f32), row_ns, row_ns, bd_in_s, bd_in_s, bd_out_s, bd_out_s,
                   jax.ShapeDtypeStruct((1, D), f32)],
        scratch_shapes=[pltpu.VMEM((tm, ns), f32), pltpu.VMEM((tm, ns), f32),
                        pltpu.VMEM((8, ns), f32), pltpu.VMEM((8, ns), f32)],
        compiler_params=_cp(1),
    )(dyy, x, g, scale, shift, xr, xi, xr, xi, lb_re, lb_im, bbr_bd, bbi_bd, cr_bd, ci_bd, dskip)


def _s5_gelu(yy):
    t = yy.shape[0]
    tm = _tok_tile(t)

    def body(y_ref, o_ref):
        o_ref[...] = jax.nn.gelu(y_ref[...]).astype(bf16)

    return pl.pallas_call(
        body, name="s5_gelu", grid=(t // tm,), in_specs=[_tok_spec(tm, D)], out_specs=_tok_spec(tm, D),
        out_shape=jax.ShapeDtypeStruct((t, D), bf16), compiler_params=_cp(1),
    )(yy)


def _s5_glu(gl, z):
    return gl * jax.nn.sigmoid(z)


def _s5_out(x, yy, z, gate):
    t = x.shape[0]
    tm = _tok_tile(t)

    def body(x_ref, y_ref, z_ref, gt_ref, o_ref):
        o_ref[...] = x_ref[...] + (1.0 + gt_ref[...]) * _s5_glu(jax.nn.gelu(y_ref[...]), z_ref[...])

    return pl.pallas_call(
        body, name="s5_out", grid=(t // tm,),
        in_specs=[_tok_spec(tm, D), _tok_spec(tm, D), _tok_spec(tm, D), _row_spec(D)],
        out_specs=_tok_spec(tm, D), out_shape=jax.ShapeDtypeStruct((t, D), f32), compiler_params=_cp(1),
    )(x, yy, z, gate)


def _s5_out_bwd(dxn, yy, z, gate):
    t = dxn.shape[0]
    tm = _tok_tile(t)

    def body(dxn_ref, y_ref, z_ref, gt_ref, dz_ref, dgl_ref, dgt_ref):
        dxn_v = dxn_ref[...]
        gl = jax.nn.gelu(y_ref[...])
        out, vjp = jax.vjp(_s5_glu, gl, z_ref[...])
        dgl, dz = vjp((1.0 + gt_ref[...]) * dxn_v)
        dz_ref[...] = dz.astype(bf16)
        dgl_ref[...] = dgl
        _acc_add(pl.program_id(0) == 0, dgt_ref, jnp.sum(dxn_v * out, axis=0, keepdims=True))

    return pl.pallas_call(
        body, name="s5_out_bwd", grid=(t // tm,),
        in_specs=[_tok_spec(tm, D), _tok_spec(tm, D), _tok_spec(tm, D), _row_spec(D)],
        out_specs=[_tok_spec(tm, D), _tok_spec(tm, D), _row_spec(D)],
        out_shape=[jax.ShapeDtypeStruct((t, D), bf16), jax.ShapeDtypeStruct((t, D), f32),
                   jax.ShapeDtypeStruct((1, D), f32)],
        compiler_params=_cp(1),
    )(dxn, yy, z, gate)


def _s5_gelu_bwd(yy, dgl_a, dgl_b):
    t = yy.shape[0]
    tm = _tok_tile(t)

    def body(y_ref, a_ref, b_ref, o_ref):
        _, vjp = jax.vjp(jax.nn.gelu, y_ref[...])
        o_ref[...] = vjp(a_ref[...] + b_ref[...])[0]

    return pl.pallas_call(
        body, name="s5_gelu_bwd", grid=(t // tm,),
        in_specs=[_tok_spec(tm, D), _tok_spec(tm, D), _tok_spec(tm, D)],
        out_specs=_tok_spec(tm, D), out_shape=jax.ShapeDtypeStruct((t, D), f32), compiler_params=_cp(1),
    )(yy, dgl_a, dgl_b)


def _s5_params(lam_re, lam_im, log_dt, b_re, b_im):
    bc = lambda a: a.reshape(S5_GROUPS, 1, -1)
    return (bc(lam_re), bc(lam_im), jnp.broadcast_to(log_dt.reshape(S5_GROUPS, 1, 1), (S5_GROUPS, 1, S5_STATE)),
            b_re.transpose(0, 2, 1), b_im.transpose(0, 2, 1))


def _s5_fwd(x, ln, raw, c_re, c_im, dskip, w_glu):
    gate = ln[3]
    lb_re, lb_im, bb_re, bb_im = _s5_prep(*raw)
    lbr, lbi = lb_re.reshape(1, S5_NSTATE), lb_im.reshape(1, S5_NSTATE)
    bds = (_s5_in_bd(bb_re).astype(bf16), _s5_in_bd(bb_im).astype(bf16),
           _s5_out_bd(c_re).astype(bf16), _s5_out_bd(c_im).astype(bf16))
    yy, xr, xi = _s5_scan_fwd(x, ln, lbr, lbi, *bds, dskip)
    gl = _s5_gelu(yy)
    z = _mm(gl, w_glu, "nn", f32, "s5_glu_mm")
    return _s5_out(x, yy, z, gate), (x, lbr, lbi, bds, yy, xr, xi, gl, z)


def _s5_bwd(dxn, saved, ln, raw, dskip, w_glu):
    x, lbr, lbi, bds, yy, xr, xi, gl, z = saved
    g, shift, scale, gate = ln
    dz, dgl_a, dgate = _s5_out_bwd(dxn, yy, z, gate)
    dgl_b = _mm(dz, w_glu, "nt", f32, "s5_dgl")
    dw_glu = _mm(gl, dz, "tn", f32, "s5_dwglu")
    dyy = _s5_gelu_bwd(yy, dgl_a, dgl_b)
    du, dar, dai, dbbr_bd, dbbi_bd, dcr_bd, dci_bd, dd = _s5_scan_bwd(dyy, x, ln, xr, xi, lbr, lbi, *bds, dskip)
    shp = (S5_GROUPS, 1, S5_STATE)
    d_lam_re, d_lam_im, d_dt, d_b_re, d_b_im = _s5_prep_bwd(
        *raw, dar.reshape(shp), dai.reshape(shp), _s5_in_bd_diag(dbbr_bd), _s5_in_bd_diag(dbbi_bd))
    dx, dg, dscale, dshift = _ln_bwd(x, g, scale, shift, du, dxn)
    grads = dict(
        s5_lam_re=d_lam_re.reshape(1, S5_GROUPS, S5_STATE), s5_lam_im=d_lam_im.reshape(1, S5_GROUPS, S5_STATE),
        s5_log_dt=d_dt[:, 0, 0].reshape(1, S5_GROUPS),
        s5_b_re=d_b_re.transpose(0, 2, 1)[None], s5_b_im=d_b_im.transpose(0, 2, 1)[None],
        s5_c_re=_s5_out_bd_diag(dcr_bd)[None], s5_c_im=_s5_out_bd_diag(dci_bd)[None],
        s5_d=dd, s5_w_glu=dw_glu)
    return dx, (dg, dshift, dscale, dgate), grads


_MESH = pl.DeviceIdType.MESH
_ANY = pl.BlockSpec(memory_space=pl.ANY)


def _me():
    return lax.axis_index("x"), lax.axis_index("y"), lax.axis_index("c")


def _dev_index(x, y, c):
    return 4 * x + 2 * y + c


def _all_gather(vs, name):
    n = len(vs)

    def body(*refs):
        v_refs, out_refs = refs[:n], refs[n:2 * n]
        send_sems, recv_sems, local_sems = refs[2 * n:]
        x, y, cc = _me()
        me, sibling = (x, y, cc), (x, y, 1 - cc)
        chips = [(1 - x, y), (x, 1 - y), (1 - x, 1 - y)]
        sends, local = [], []

        def copy(a, k, block, to, src=None):
            rows = out_refs[a].at[_dev_index(*block)]
            return pltpu.make_async_remote_copy(
                src_ref=rows if src is None else src, dst_ref=rows,
                send_sem=send_sems.at[7 * a + k], recv_sem=recv_sems.at[7 * a + k], device_id=to, device_id_type=_MESH)

        for a in range(n):
            mine = pltpu.make_async_copy(v_refs[a], out_refs[a].at[_dev_index(*me)], local_sems.at[a])
            mine.start()
            local.append(mine)
            first = [copy(a, 0, me, sibling, src=v_refs[a])]
            first += [copy(a, 1 + j, me, (*chip, cc), src=v_refs[a]) for j, chip in enumerate(chips)]
            for cp in first:
                cp.start()
            sends += first
        for a in range(n):
            for j, chip in enumerate(chips):
                copy(a, 1 + j, (*chip, cc), me).wait_recv()
                passed = copy(a, 4 + j, (*chip, cc), sibling)
                passed.start()
                sends.append(passed)
        for a in range(n):
            copy(a, 0, sibling, me).wait_recv()
            for j, chip in enumerate(chips):
                copy(a, 4 + j, (*chip, 1 - cc), me).wait_recv()
        for cp in sends:
            cp.wait_send()
        for cp in local:
            cp.wait()

    return pl.pallas_call(
        body, name=name, out_shape=[jax.ShapeDtypeStruct((N_DEV,) + v.shape, v.dtype) for v in vs],
        in_specs=[_ANY] * n, out_specs=[_ANY] * n,
        scratch_shapes=[pltpu.SemaphoreType.DMA((7 * n,)), pltpu.SemaphoreType.DMA((7 * n,)),
                        pltpu.SemaphoreType.DMA((n,))],
    )(*vs)


def _exchange_pair(vs, name):
    n = len(vs)

    def body(*refs):
        v_refs, out_refs = refs[:n], refs[n:2 * n]
        send_sems, recv_sems = refs[2 * n:]
        x, y, cc = _me()
        sibling = (x, y, 1 - cc)
        copies = []
        for a in range(n):
            for k in range(4):
                cp = pltpu.make_async_remote_copy(
                    src_ref=v_refs[a].at[2 * k + (1 - cc)], dst_ref=out_refs[a].at[k],
                    send_sem=send_sems.at[4 * a + k], recv_sem=recv_sems.at[4 * a + k],
                    device_id=sibling, device_id_type=_MESH)
                cp.start()
                copies.append(cp)
        for cp in copies:
            cp.wait_recv()
        for cp in copies:
            cp.wait_send()

    return pl.pallas_call(
        body, name=name, out_shape=[jax.ShapeDtypeStruct((4,) + v.shape[1:], v.dtype) for v in vs],
        in_specs=[_ANY] * n, out_specs=[_ANY] * n,
        scratch_shapes=[pltpu.SemaphoreType.DMA((4 * n,)), pltpu.SemaphoreType.DMA((4 * n,))],
    )(*vs)


def _pair_sum(v, got):
    _, r, c = v.shape
    tr = _pick(r, (512, 352, 256, 128))
    core = lax.axis_index("c").astype(jnp.int32).reshape(1)

    def body(c_ref, v_ref, g_ref, o_ref):
        o_ref[...] = (v_ref[...] + g_ref[...]).astype(bf16)

    return pl.pallas_call(
        body, name="pair_sum",
        grid_spec=pltpu.PrefetchScalarGridSpec(
            num_scalar_prefetch=1, grid=(4, r // tr),
            in_specs=[pl.BlockSpec((1, tr, c), lambda k, i, c_ref: (2 * k + c_ref[0], i, 0)),
                      pl.BlockSpec((1, tr, c), lambda k, i, c_ref: (k, i, 0))],
            out_specs=pl.BlockSpec((1, tr, c), lambda k, i, c_ref: (k, i, 0))),
        out_shape=jax.ShapeDtypeStruct((4, r, c), bf16), compiler_params=_cp(2),
    )(core, v, got)


def _exchange_chips(vs, name):
    n = len(vs)

    def body(*refs):
        v_refs, out_refs = refs[:n], refs[n:2 * n]
        send_sems, recv_sems, local_sems = refs[2 * n:]
        x, y, cc = _me()
        mine = 2 * x + y
        peers = []
        for mask in (1, 2, 3):
            px = 1 - x if mask & 2 else x
            py = 1 - y if mask & 1 else y
            peers.append((mask - 1, (px, py, cc), 2 * px + py))
        local, sends = [], []
        for a in range(n):
            own = pltpu.make_async_copy(v_refs[a].at[mine], out_refs[a].at[mine], local_sems.at[a])
            own.start()
            local.append(own)
            for k, peer, pchip in peers:
                cp = pltpu.make_async_remote_copy(
                    src_ref=v_refs[a].at[pchip], dst_ref=out_refs[a].at[mine],
                    send_sem=send_sems.at[3 * a + k], recv_sem=recv_sems.at[3 * a + k],
                    device_id=peer, device_id_type=_MESH)
                cp.start()
                sends.append(cp)
        for a in range(n):
            for k, peer, pchip in peers:
                pltpu.make_async_remote_copy(
                    src_ref=v_refs[a].at[pchip], dst_ref=out_refs[a].at[pchip],
                    send_sem=send_sems.at[3 * a + k], recv_sem=recv_sems.at[3 * a + k],
                    device_id=peer, device_id_type=_MESH).wait_recv()
        for cp in sends:
            cp.wait_send()
        for cp in local:
            cp.wait()

    return pl.pallas_call(
        body, name=name, out_shape=[jax.ShapeDtypeStruct(v.shape, v.dtype) for v in vs],
        in_specs=[_ANY] * n, out_specs=[_ANY] * n,
        scratch_shapes=[pltpu.SemaphoreType.DMA((3 * n,)), pltpu.SemaphoreType.DMA((3 * n,)),
                        pltpu.SemaphoreType.DMA((n,))],
    )(*vs)


def _ada_mod(c_all, ada_w):
    cols = ada_w.shape[2]

    def body(c_ref, w_ref, o_ref):
        cond = jax.nn.silu(c_ref[...]).astype(bf16)
        o_ref[0] = jnp.dot(cond, w_ref[0].astype(bf16), preferred_element_type=f32)

    return pl.pallas_call(
        body, name="ada_mod", grid=(DEPTH,),
        in_specs=[pl.BlockSpec((16, D), lambda i: (0, 0)), pl.BlockSpec((1, D, cols), lambda i: (i, 0, 0))],
        out_specs=pl.BlockSpec((1, 16, cols), lambda i: (i, 0, 0)),
        out_shape=jax.ShapeDtypeStruct((DEPTH, 16, cols), f32), compiler_params=_cp(1),
    )(c_all, ada_w)


def _ada_grad(c_all, dmod):
    cols = dmod.shape[2]

    def body(c_ref, d_ref, o_ref):
        cond = jax.nn.silu(c_ref[...]).astype(bf16)
        o_ref[0] = lax.dot_general(cond, d_ref[0].astype(bf16), _DN["tn"], preferred_element_type=f32)

    return pl.pallas_call(
        body, name="ada_grad", grid=(DEPTH,),
        in_specs=[pl.BlockSpec((16, D), lambda i: (0, 0)), pl.BlockSpec((1, 16, cols), lambda i: (i, 0, 0))],
        out_specs=pl.BlockSpec((1, D, cols), lambda i: (i, 0, 0)),
        out_shape=jax.ShapeDtypeStruct((DEPTH, D, cols), f32), compiler_params=_cp(1),
    )(c_all, dmod)


def _row_tile(r):
    return _pick(r, (512, 352, 256, 128)) if r > 512 else r


def _sum_sources(v, name):
    n, r, c = v.shape
    tr = _row_tile(r)

    def body(v_ref, o_ref):
        acc = v_ref[0]
        for p in range(1, n):
            acc = acc + v_ref[p]
        o_ref[...] = acc.astype(f32)

    return pl.pallas_call(
        body, name=name, grid=(r // tr,),
        in_specs=[pl.BlockSpec((n, tr, c), lambda i: (0, i, 0))], out_specs=pl.BlockSpec((tr, c), lambda i: (i, 0)),
        out_shape=jax.ShapeDtypeStruct((r, c), f32), compiler_params=_cp(1),
    )(v)


def _adamw(parts, w, m, v, name):
    n, r, c = parts.shape
    tr = _row_tile(r)
    c1 = 1.0 - ADAM_B1 ** ADAM_STEP
    c2 = 1.0 - ADAM_B2 ** ADAM_STEP

    def body(p_ref, w_ref, m_ref, v_ref, g_ref, d_ref, mo_ref, vo_ref):
        g_v = p_ref[0].astype(f32)
        for p in range(1, n):
            g_v = g_v + p_ref[p].astype(f32)
        m_n = ADAM_B1 * m_ref[...] + (1.0 - ADAM_B1) * g_v
        v_n = ADAM_B2 * v_ref[...] + (1.0 - ADAM_B2) * (g_v * g_v)
        g_ref[...] = g_v
        d_ref[...] = -ADAM_LR * ((m_n / c1) / (jnp.sqrt(v_n / c2) + ADAM_EPS) + ADAM_WD * w_ref[...])
        mo_ref[...] = m_n
        vo_ref[...] = v_n

    spec = pl.BlockSpec((tr, c), lambda i: (i, 0))
    shp = jax.ShapeDtypeStruct((r, c), f32)
    return pl.pallas_call(
        body, name=name, grid=(r // tr,), in_specs=[pl.BlockSpec((n, tr, c), lambda i: (0, i, 0))] + [spec] * 3,
        out_specs=[spec] * 4, out_shape=[shp] * 4, compiler_params=_cp(1),
    )(parts, w, m, v)


def _two_d(shape):
    return (math.prod(shape[:-1]), shape[-1])


def _pack_rows(a):
    n = a.size
    rows = -(-n // (8 * PACK_C)) * 8
    return jnp.pad(a.reshape(-1), (0, rows * PACK_C - n)).reshape(rows, PACK_C)


def _pack(parts):
    return jnp.concatenate([_pack_rows(p) for p in parts], axis=0)


def _unpack(packed, shapes):
    lead = packed.shape[:-2]
    out, off = [], 0
    for s in shapes:
        n = math.prod(s)
        rows = -(-n // (8 * PACK_C)) * 8
        part = packed[..., off:off + rows, :].reshape(lead + (rows * PACK_C,))
        out.append(part[..., :n].reshape(lead + tuple(s)))
        off += rows
    return out


def _unshard(g8, axis):
    local = g8.shape[1:]
    moved = jnp.moveaxis(g8, 0, axis)
    return moved.reshape(local[:axis] + (N_DEV * local[axis],) + local[axis + 1:])


_BIG = dict(ffn_w_in=3, ffn_w_out=2, pool_w=2, fox_w_in=2, fox_w_o=1, s5_w_glu=1, conv_w_in=2, conv_w_out=1)
_SMALL_SHARDED = dict(norm_g=2, s5_d=1, conv_w=3)
_REPLICATED = ("ada_b", "pool_scale", "fox_b_f", "fox_q_gain", "fox_k_gain", "s5_lam_re", "s5_lam_im", "s5_log_dt",
               "s5_b_re", "s5_b_im", "s5_c_re", "s5_c_im")
_WEIGHTS = ("ada_w", "ada_b", "norm_g", "ffn_w_in", "ffn_w_out", "pool_w", "pool_scale", "fox_w_in", "fox_b_f",
            "fox_q_gain", "fox_k_gain", "fox_w_o", "s5_lam_re", "s5_lam_im", "s5_log_dt", "s5_b_re", "s5_b_im",
            "s5_c_re", "s5_c_im", "s5_d", "s5_w_glu", "conv_w_in", "conv_w", "conv_w_out")


def _step(x, c, target, w, m, v):
    t = x.shape[1]
    xi_, yi_, ci_ = _me()
    me = _dev_index(xi_, yi_, ci_)

    sm_shapes = [w[n].shape for n in _SMALL_SHARDED]
    small_all = _all_gather([_pack([c] + [w[n] for n in _SMALL_SHARDED])], "gather_small")[0]
    gathered = _unpack(small_all, [c.shape] + sm_shapes)
    c_all = gathered[0][:, 0, :]
    full = {n: _unshard(p, ax) for (n, ax), p in zip(_SMALL_SHARDED.items(), gathered[1:])}

    big_all = _all_gather([w[n].astype(bf16).reshape(_two_d(w[n].shape)) for n in _BIG], "gather_weights")
    gw = dict(zip(_BIG, big_all))
    ffn_w_in = gw["ffn_w_in"].reshape(N_DEV, 2 * DEPTH, D, FFN_HS)
    ffn_w_out = gw["ffn_w_out"].reshape(N_DEV, 2 * DEPTH, D_FF // N_DEV, D)
    pool_w = gw["pool_w"].reshape(N_DEV, 4, POOL_GROUP // N_DEV, POOL_GROUP).transpose(1, 0, 2, 3)
    pool_w = pool_w.reshape(4, POOL_GROUP, POOL_GROUP)
    fox_w_in = jnp.pad(gw["fox_w_in"].transpose(1, 0, 2).reshape(D, FOX_PROJ), ((0, 0), (0, FOX_PROJ_PAD - FOX_PROJ)))
    fox_w_o, s5_w_glu, conv_w_out = (gw[n].reshape(D, D) for n in ("fox_w_o", "s5_w_glu", "conv_w_out"))
    conv_w_in = gw["conv_w_in"]

    def ffn_w(i, f):
        return ffn_w_in, ffn_w_out, 2 * i + f

    c16 = jnp.pad(c_all, ((0, 8), (0, 0)))
    cols = w["ada_w"].shape[2]
    mod_sh = _ada_mod(c16, w["ada_w"])
    mod_all = _all_gather([mod_sh.reshape(DEPTH * 16, cols)], "gather_mod")[0].reshape(N_DEV, DEPTH, 16, cols)
    mod_mine = lax.dynamic_index_in_dim(mod_all, me, axis=2, keepdims=False)
    mod = (mod_mine.transpose(1, 0, 2).reshape(DEPTH, N_DEV * cols) + w["ada_b"]).reshape(DEPTH, 3, 3, D)

    norm_g = full["norm_g"]

    def ln_of(i, sub):
        return (norm_g[i, sub][None], mod[i, sub, 0][None], mod[i, sub, 1][None], mod[i, sub, 2][None])

    fox_b_f = jnp.pad(w["fox_b_f"], ((0, 0), (0, 128 - FOX_HEADS)))
    s5_raw = _s5_params(w["s5_lam_re"][0], w["s5_lam_im"][0], w["s5_log_dt"][0], w["s5_b_re"][0], w["s5_b_im"][0])
    s5_c_re, s5_c_im = w["s5_c_re"][0], w["s5_c_im"][0]
    conv_w = jnp.pad(full["conv_w"][0, :, 0, :], ((0, 5), (0, 0)))

    xs = x[0]
    saved = []
    for i in range(DEPTH):
        xs, s0 = _ffn_fwd(xs, ln_of(i, 0), *ffn_w(i, 0), 0.5)
        if i == 0:
            s1 = xs
            xs = _pool_fwd(xs, ln_of(i, 1), pool_w, w["pool_scale"])
        elif i == 1:
            xs, s1 = _fox_fwd(xs, ln_of(i, 1), fox_w_in, fox_b_f, w["fox_q_gain"], w["fox_k_gain"], fox_w_o)
        elif i == 2:
            xs, s1 = _s5_fwd(xs, ln_of(i, 1), s5_raw, s5_c_re, s5_c_im, full["s5_d"], s5_w_glu)
        else:
            xs, s1 = _convmix_fwd(xs, ln_of(i, 1), conv_w_in, conv_w, conv_w_out)
        xs, s2 = _ffn_fwd(xs, ln_of(i, 2), *ffn_w(i, 1), 0.5)
        saved.append((s0, s1, s2))
    dx, lpart = _loss_head(xs, target[0])

    grads = {}
    dmod = [[None] * 3 for _ in range(DEPTH)]
    dnorm = [[None] * 3 for _ in range(DEPTH)]
    dffn_in = lax.empty(ffn_w_in.shape, f32)
    dffn_out = lax.empty(ffn_w_out.shape, f32)

    def put_ln(i, sub, dln):
        dg, dshift, dscale, dgate = dln
        dnorm[i][sub] = dg
        dmod[i][sub] = jnp.concatenate([dshift, dscale, dgate], axis=0)

    for i in reversed(range(DEPTH)):
        s0, s1, s2 = saved[i]
        dx, dln, dffn_in, dffn_out = _ffn_bwd(dx, s2, ln_of(i, 2), *ffn_w(i, 1), 0.5, dffn_in, dffn_out)
        put_ln(i, 2, dln)
        if i == 0:
            dx, dln, dpw, dps = _pool_bwd(dx, s1, ln_of(i, 1), pool_w, w["pool_scale"])
            dpw = dpw.reshape(4, N_DEV, POOL_GROUP // N_DEV, POOL_GROUP).transpose(1, 0, 2, 3)
            grads.update(pool_w=dpw.reshape(N_DEV, 4 * POOL_GROUP // N_DEV, POOL_GROUP), pool_scale=dps)
        elif i == 1:
            dx, dln, dwi, dbf, dqg, dkg, dwo = _fox_bwd(dx, s1, ln_of(i, 1), fox_w_in, fox_b_f, fox_w_o)
            dwi = dwi[:, :FOX_PROJ].reshape(D, N_DEV, FOX_PROJ // N_DEV).transpose(1, 0, 2)
            grads.update(fox_w_in=dwi, fox_b_f=dbf[:, :FOX_HEADS], fox_q_gain=dqg, fox_k_gain=dkg,
                         fox_w_o=dwo.reshape(N_DEV, D // N_DEV, D))
        elif i == 2:
            dx, dln, gs5 = _s5_bwd(dx, s1, ln_of(i, 1), s5_raw, full["s5_d"], s5_w_glu)
            gs5["s5_w_glu"] = gs5["s5_w_glu"].reshape(N_DEV, D // N_DEV, D)
            grads.update(gs5)
        else:
            dx, dln, dwi, dcw, dwo = _convmix_bwd(dx, s1, ln_of(i, 1), conv_w_in, conv_w, conv_w_out)
            grads.update(conv_w_in=dwi, conv_w=dcw[None, :, None, :], conv_w_out=dwo.reshape(N_DEV, D // N_DEV, D))
        put_ln(i, 1, dln)
        dx, dln, dffn_in, dffn_out = _ffn_bwd(dx, s0, ln_of(i, 0), *ffn_w(i, 0), 0.5, dffn_in, dffn_out)
        put_ln(i, 0, dln)
    grads["ffn_w_in"] = dffn_in.reshape(N_DEV, 2 * DEPTH * D, FFN_HS)
    grads["ffn_w_out"] = dffn_out.reshape(N_DEV, 2 * DEPTH * D_FF // N_DEV, D)
    grads["norm_g"] = jnp.stack([jnp.concatenate(r, axis=0) for r in dnorm])
    dmod_mine = jnp.stack([jnp.stack(r) for r in dmod]).reshape(DEPTH, 9 * D)

    small_names = list(_REPLICATED[1:]) + list(_SMALL_SHARDED)
    small_parts = [dmod_mine] + [grads[n] for n in small_names] + [lpart[:, 0:1]]
    small_g = _all_gather([_pack(small_parts)], "gather_grads")[0]
    small_sum = _sum_sources(small_g, "sum_small")
    summed = dict(zip(["ada_b"] + small_names + ["loss"], _unpack(small_sum, [p.shape for p in small_parts])))
    loss = summed.pop("loss")[0, 0]
    for n, ax in _SMALL_SHARDED.items():
        local = w[n].shape[ax]
        summed[n] = lax.dynamic_slice_in_dim(summed[n], me * local, local, axis=ax)

    dmod_all = small_g[:, :DEPTH * 9].reshape(N_DEV, DEPTH, 9 * D)
    dmod_cols = lax.dynamic_slice_in_dim(dmod_all, me * cols, cols, axis=2)
    ada_g = _ada_grad(c16, jnp.pad(dmod_cols.transpose(1, 0, 2), ((0, 0), (0, 8), (0, 0))))

    big_parts = [grads[n] for n in _BIG]
    from_sibling = _exchange_pair(big_parts, "exchange_pair")
    chip_sums = [_pair_sum(p, s) for p, s in zip(big_parts, from_sibling)]
    big_landed = dict(zip(_BIG, _exchange_chips(chip_sums, "exchange_chips")))

    grad, delta, new_m, new_v = {}, {}, {}, {}
    big_landed["ada_w"] = ada_g[None]
    for n, parts in big_landed.items():
        view = _two_d(w[n].shape)
        outs = _adamw(parts.reshape((parts.shape[0],) + view), w[n].reshape(view), m[n].reshape(view),
                      v[n].reshape(view), "adamw_" + n)
        grad[n], delta[n], new_m[n], new_v[n] = (a.reshape(w[n].shape) for a in outs)
    small = [n for n in _WEIGHTS if n not in big_landed]
    small_shapes = [w[n].shape for n in small]
    pk = lambda d: _pack([d[n] for n in small])
    outs = _adamw(pk(summed)[None], pk(w), pk(m), pk(v), "adamw_small")
    for dst, packed in zip((grad, delta, new_m, new_v), outs):
        dst.update(zip(small, _unpack(packed, small_shapes)))
    return (loss, dx[None], *[grad[n] for n in _WEIGHTS], *[delta[n] for n in _WEIGHTS],
            *[new_m[n] for n in _WEIGHTS], *[new_v[n] for n in _WEIGHTS])


def kernel(x, c, ada_w, ada_b, norm_g, ffn_w_in, ffn_w_out, pool_w, pool_scale, fox_w_in, fox_b_f, fox_q_gain, fox_k_gain, fox_w_o, s5_lam_re, s5_lam_im, s5_log_dt, s5_b_re, s5_b_im, s5_c_re, s5_c_im, s5_d, s5_w_glu, conv_w_in, conv_w, conv_w_out, loss_target, m_ada_w, m_ada_b, m_norm_g, m_ffn_w_in, m_ffn_w_out, m_pool_w, m_pool_scale, m_fox_w_in, m_fox_b_f, m_fox_q_gain, m_fox_k_gain, m_fox_w_o, m_s5_lam_re, m_s5_lam_im, m_s5_log_dt, m_s5_b_re, m_s5_b_im, m_s5_c_re, m_s5_c_im, m_s5_d, m_s5_w_glu, m_conv_w_in, m_conv_w, m_conv_w_out, v_ada_w, v_ada_b, v_norm_g, v_ffn_w_in, v_ffn_w_out, v_pool_w, v_pool_scale, v_fox_w_in, v_fox_b_f, v_fox_q_gain, v_fox_k_gain, v_fox_w_o, v_s5_lam_re, v_s5_lam_im, v_s5_log_dt, v_s5_b_re, v_s5_b_im, v_s5_c_re, v_s5_c_im, v_s5_d, v_s5_w_glu, v_conv_w_in, v_conv_w, v_conv_w_out):
    ws = (ada_w, ada_b, norm_g, ffn_w_in, ffn_w_out, pool_w, pool_scale, fox_w_in, fox_b_f, fox_q_gain, fox_k_gain,
          fox_w_o, s5_lam_re, s5_lam_im, s5_log_dt, s5_b_re, s5_b_im, s5_c_re, s5_c_im, s5_d, s5_w_glu, conv_w_in,
          conv_w, conv_w_out)
    ms = (m_ada_w, m_ada_b, m_norm_g, m_ffn_w_in, m_ffn_w_out, m_pool_w, m_pool_scale, m_fox_w_in, m_fox_b_f,
          m_fox_q_gain, m_fox_k_gain, m_fox_w_o, m_s5_lam_re, m_s5_lam_im, m_s5_log_dt, m_s5_b_re, m_s5_b_im,
          m_s5_c_re, m_s5_c_im, m_s5_d, m_s5_w_glu, m_conv_w_in, m_conv_w, m_conv_w_out)
    vs = (v_ada_w, v_ada_b, v_norm_g, v_ffn_w_in, v_ffn_w_out, v_pool_w, v_pool_scale, v_fox_w_in, v_fox_b_f,
          v_fox_q_gain, v_fox_k_gain, v_fox_w_o, v_s5_lam_re, v_s5_lam_im, v_s5_log_dt, v_s5_b_re, v_s5_b_im,
          v_s5_c_re, v_s5_c_im, v_s5_d, v_s5_w_glu, v_conv_w_in, v_conv_w, v_conv_w_out)
    return _step(x, c, loss_target, dict(zip(_WEIGHTS, ws)), dict(zip(_WEIGHTS, ms)), dict(zip(_WEIGHTS, vs)))
```

```python
import math

import jax
import jax.numpy as jnp
from jax import lax
from jax.experimental import pallas as pl
from jax.experimental.pallas import tpu as pltpu

f32 = jnp.float32
bf16 = jnp.bfloat16

D = 1024
D_FF = 2816
FFN_HS = 2 * D_FF // 8
FFN_SLABS = 4
DEPTH = 4
NORM_EPS = 1e-6
N_DEV = 8
POOL_WINDOWS = (2, 4, 8, 16)
POOL_GROUP = 256
POOL_HALO = 16
FOX_HEADS = 16
FOX_HEAD_DIM = 64
FOX_PROJ = 3088
FOX_PROJ_PAD = 3200
S5_GROUPS = 64
S5_GROUP = 16
S5_STATE = 64
S5_NSTATE = S5_GROUPS * S5_STATE
S5_BLOCKS = 8
S5_BCH = 128
S5_BST = 512
CONV_HALO = 8
ADAM_LR = 0.001
ADAM_B1 = 0.9
ADAM_B2 = 0.999
ADAM_EPS = 1e-08
ADAM_WD = 0.01
ADAM_STEP = 10
VMEM_LIMIT = 56 * 1024 * 1024
PACK_C = 1024

_ARB = "arbitrary"


def _cp(n_axes):
    return pltpu.CompilerParams(dimension_semantics=(_ARB,) * n_axes, vmem_limit_bytes=VMEM_LIMIT)


def _pick(n, prefs):
    for c in prefs:
        if n % c == 0:
            return c
    return n


_DN = {"nn": (((1,), (0,)), ((), ())), "nt": (((1,), (1,)), ((), ())), "tn": (((0,), (0,)), ((), ()))}


def _mm(a, b, mode, out_dtype, name):
    if mode == "nn":
        (m, k), (_, n) = a.shape, b.shape
    elif mode == "nt":
        (m, k), (n, _) = a.shape, b.shape
    else:
        (k, m), (_, n) = a.shape, b.shape
    big = (1408, 1024, 640, 512, 384, 256, 128)
    tm = _pick(m, big) if mode == "tn" else _pick(m, (1024, 512, 256, 128))
    tn = _pick(n, big)
    if mode == "tn":
        tk = _pick(k, (4096, 2048, 1024, 512, 256, 128))
    else:
        tk = k if k <= 3200 else _pick(k, (2816, 2048, 1024, 512))
    nk = k // tk
    dn = _DN[mode]

    def body(a_ref, b_ref, o_ref, acc_ref):
        p = lax.dot_general(a_ref[...], b_ref[...], dn, preferred_element_type=f32)
        if nk == 1:
            o_ref[...] = p.astype(out_dtype)
        else:
            kk = pl.program_id(2)

            @pl.when(kk == 0)
            def _():
                acc_ref[...] = p

            @pl.when(kk > 0)
            def _():
                acc_ref[...] += p

            @pl.when(kk == nk - 1)
            def _():
                o_ref[...] = acc_ref[...].astype(out_dtype)

    if mode == "nn":
        a_spec = pl.BlockSpec((tm, tk), lambda i, j, kk: (i, kk))
        b_spec = pl.BlockSpec((tk, tn), lambda i, j, kk: (kk, j))
    elif mode == "nt":
        a_spec = pl.BlockSpec((tm, tk), lambda i, j, kk: (i, kk))
        b_spec = pl.BlockSpec((tn, tk), lambda i, j, kk: (j, kk))
    else:
        a_spec = pl.BlockSpec((tk, tm), lambda i, j, kk: (kk, i))
        b_spec = pl.BlockSpec((tk, tn), lambda i, j, kk: (kk, j))
    acc_shape = (tm, tn) if nk > 1 else (8, 128)
    return pl.pallas_call(
        body, name=name, grid=(m // tm, n // tn, nk),
        in_specs=[a_spec, b_spec], out_specs=pl.BlockSpec((tm, tn), lambda i, j, kk: (i, j)),
        out_shape=jax.ShapeDtypeStruct((m, n), out_dtype),
        scratch_shapes=[pltpu.VMEM(acc_shape, f32)],
        compiler_params=_cp(3),
    )(a, b)


def _mmx(a, b, mode, name, grid, a_spec, b_spec, o_spec, out_shape, into=None):
    nk = grid[2]
    dn = _DN[mode]
    out_dtype = out_shape.dtype
    a_blk = (math.prod(a_spec.block_shape[:-1]), a_spec.block_shape[-1])
    b_blk = (math.prod(b_spec.block_shape[:-1]), b_spec.block_shape[-1])
    o_blk = (math.prod(o_spec.block_shape[:-1]), o_spec.block_shape[-1])

    def body(a_ref, b_ref, *rest):
        o_ref, acc_ref = rest[-2:]
        p = lax.dot_general(a_ref[...].reshape(a_blk), b_ref[...].reshape(b_blk), dn, preferred_element_type=f32)
        if nk == 1:
            o_ref[...] = p.reshape(o_ref.shape).astype(out_dtype)
        else:
            kk = pl.program_id(2)

            @pl.when(kk == 0)
            def _():
                acc_ref[...] = p

            @pl.when(kk > 0)
            def _():
                acc_ref[...] += p

            @pl.when(kk == nk - 1)
            def _():
                o_ref[...] = acc_ref[...].reshape(o_ref.shape).astype(out_dtype)

    extra = {} if into is None else dict(input_output_aliases={2: 0})
    operands = (a, b) if into is None else (a, b, into)
    return pl.pallas_call(
        body, name=name, grid=grid, in_specs=[a_spec, b_spec] + ([] if into is None else [_ANY]),
        out_specs=o_spec, out_shape=out_shape,
        scratch_shapes=[pltpu.VMEM(o_blk if nk > 1 else (8, 128), f32)], compiler_params=_cp(3), **extra,
    )(*operands)


def _tok_tile(t):
    return min(t, 512)


def _tok_spec(tm, c, nt=None, reverse=False):
    if reverse:
        return pl.BlockSpec((tm, c), lambda i: (nt - 1 - i, 0))
    return pl.BlockSpec((tm, c), lambda i: (i, 0))


def _row_spec(c, rows=1):
    return pl.BlockSpec((rows, c), lambda i: (0, 0))


def _acc_add(first, ref, val):
    @pl.when(first)
    def _():
        ref[...] = val

    @pl.when(jnp.logical_not(first))
    def _():
        ref[...] += val


def _adaln(x, g, scale, shift):
    y = x * lax.rsqrt(jnp.mean(x * x, axis=-1, keepdims=True) + NORM_EPS)
    return (y * g) * (1.0 + scale) + shift


def _ln_fwd(x, g, scale, shift):
    t = x.shape[0]
    tm = _tok_tile(t)

    def body(x_ref, g_ref, sc_ref, sh_ref, h_ref):
        h_ref[...] = _adaln(x_ref[...], g_ref[...], sc_ref[...], sh_ref[...]).astype(bf16)

    return pl.pallas_call(
        body, name="ln_fwd", grid=(t // tm,),
        in_specs=[_tok_spec(tm, D), _row_spec(D), _row_spec(D), _row_spec(D)],
        out_specs=_tok_spec(tm, D), out_shape=jax.ShapeDtypeStruct((t, D), bf16),
        compiler_params=_cp(1),
    )(x, g, scale, shift)


def _ln_bwd(x, g, scale, shift, dh, dxn):
    t = x.shape[0]
    tm = _tok_tile(t)

    def body(x_ref, g_ref, sc_ref, sh_ref, dh_ref, dxn_ref, dx_ref, dg_ref, dsc_ref, dsh_ref):
        _, vjp = jax.vjp(_adaln, x_ref[...], g_ref[...], sc_ref[...], sh_ref[...])
        dx, dg, dsc, dsh = vjp(dh_ref[...])
        dx_ref[...] = dxn_ref[...] + dx
        first = pl.program_id(0) == 0
        _acc_add(first, dg_ref, dg)
        _acc_add(first, dsc_ref, dsc)
        _acc_add(first, dsh_ref, dsh)

    row = jax.ShapeDtypeStruct((1, D), f32)
    return pl.pallas_call(
        body, name="ln_bwd", grid=(t // tm,),
        in_specs=[_tok_spec(tm, D), _row_spec(D), _row_spec(D), _row_spec(D), _tok_spec(tm, D), _tok_spec(tm, D)],
        out_specs=[_tok_spec(tm, D), _row_spec(D), _row_spec(D), _row_spec(D)],
        out_shape=[jax.ShapeDtypeStruct((t, D), f32), row, row, row],
        compiler_params=_cp(1),
    )(x, g, scale, shift, dh, dxn)


def _swiglu(g, u):
    return jax.nn.silu(g) * u


def _ffn_tile(t):
    return min(t, 512)


def _ffn_specs(tm, idx, rows):
    once = pl.Buffered(1)
    return (pl.BlockSpec((2, FFN_SLABS, 1, D, FFN_HS), lambda i, q: (0, 0, idx, 0, 0), pipeline_mode=once),
            pl.BlockSpec((N_DEV, 1, rows, D), lambda i, q: (0, idx, 0, 0), pipeline_mode=once),
            pl.BlockSpec((2, 1, tm, FFN_HS), lambda i, q: (0, q, i, 0)),
            pl.BlockSpec((tm, D), lambda i, q: (i, 0)),
            pl.BlockSpec((1, D), lambda i, q: (0, 0)))


def _out_rows(wo_ref, q):
    return wo_ref[pl.ds(2 * q, 2), 0].reshape(FFN_HS, D)


def _ffn_up(x, ln, w_in, w_out, idx, coef):
    g_, shift, scale, gate = ln
    t = x.shape[0]
    tm = _ffn_tile(t)
    rows = w_out.shape[2]

    def body(wi_ref, wo_ref, x_ref, g_ref, sc_ref, sh_ref, gt_ref, h_ref, gu_ref, a_ref, o_ref, xn_ref):
        q = pl.program_id(1)

        @pl.when(q == 0)
        def _():
            h_ref[...] = _adaln(x_ref[...], g_ref[...], sc_ref[...], sh_ref[...]).astype(bf16)

        hv = h_ref[...]
        g = jnp.dot(hv, wi_ref[0, q, 0], preferred_element_type=f32)
        u = jnp.dot(hv, wi_ref[1, q, 0], preferred_element_type=f32)
        gu_ref[0, 0] = g.astype(bf16)
        gu_ref[1, 0] = u.astype(bf16)
        a = _swiglu(g, u).astype(bf16)
        a_ref[0] = a
        _acc_add(q == 0, o_ref, jnp.dot(a, _out_rows(wo_ref, q), preferred_element_type=f32))

        @pl.when(q == FFN_SLABS - 1)
        def _():
            xn_ref[...] = x_ref[...] + (coef * (1.0 + gt_ref[...])) * o_ref[...]

    wi, wo, slab, tok, row = _ffn_specs(tm, idx, rows)
    tok_f32 = jax.ShapeDtypeStruct((t, D), f32)
    return pl.pallas_call(
        body, name="ffn_up", grid=(t // tm, FFN_SLABS), in_specs=[wi, wo, tok, row, row, row, row],
        out_specs=[tok, slab, pl.BlockSpec((1, tm, FFN_HS), lambda i, q: (q, i, 0)), tok, tok],
        out_shape=[jax.ShapeDtypeStruct((t, D), bf16), jax.ShapeDtypeStruct((2, FFN_SLABS, t, FFN_HS), bf16),
                   jax.ShapeDtypeStruct((FFN_SLABS, t, FFN_HS), bf16), tok_f32, tok_f32],
        compiler_params=_cp(2),
    )(w_in, w_out, x, g_, scale, shift, gate)


def _ffn_dgu(do, w_out, idx, gu):
    t = do.shape[0]
    tm = _mm_tile(t)

    def body(do_ref, wo_ref, gu_ref, o_ref):
        da = lax.dot_general(do_ref[...], _out_rows(wo_ref, pl.program_id(1)), _DN["nt"], preferred_element_type=f32)
        _, vjp = jax.vjp(_swiglu, gu_ref[0, 0].astype(f32), gu_ref[1, 0].astype(f32))
        dg, du = vjp(da)
        o_ref[0, 0] = dg.astype(bf16)
        o_ref[1, 0] = du.astype(bf16)

    _, wo, slab, tok, _ = _ffn_specs(tm, idx, w_out.shape[2])
    return pl.pallas_call(
        body, name="ffn_dgu", grid=(t // tm, FFN_SLABS), in_specs=[tok, wo, slab],
        out_specs=slab, out_shape=jax.ShapeDtypeStruct((2, FFN_SLABS, t, FFN_HS), bf16),
        compiler_params=_cp(2),
    )(do, w_out, gu)


def _ffn_dh(dgu, w_in, idx):
    t = dgu.shape[2]
    tm = _mm_tile(t)

    def body(d_ref, wi_ref, o_ref):
        q = pl.program_id(1)
        p = (lax.dot_general(d_ref[0, 0], wi_ref[0, q, 0], _DN["nt"], preferred_element_type=f32)
             + lax.dot_general(d_ref[1, 0], wi_ref[1, q, 0], _DN["nt"], preferred_element_type=f32))
        _acc_add(pl.program_id(1) == 0, o_ref, p)

    wi, _, slab, tok, _ = _ffn_specs(tm, idx, 8)
    return pl.pallas_call(
        body, name="ffn_dh", grid=(t // tm, FFN_SLABS), in_specs=[slab, wi], out_specs=tok,
        out_shape=jax.ShapeDtypeStruct((t, D), f32), compiler_params=_cp(2),
    )(dgu, w_in)


def _resid_fwd(x, o, gate, coef):
    t = x.shape[0]
    tm = _tok_tile(t)

    def body(x_ref, o_ref, gt_ref, y_ref):
        y_ref[...] = x_ref[...] + (coef * (1.0 + gt_ref[...])) * o_ref[...]

    return pl.pallas_call(
        body, name="resid_fwd", grid=(t // tm,),
        in_specs=[_tok_spec(tm, D), _tok_spec(tm, D), _row_spec(D)],
        out_specs=_tok_spec(tm, D), out_shape=jax.ShapeDtypeStruct((t, D), f32),
        compiler_params=_cp(1),
    )(x, o, gate)


def _resid_bwd(dxn, o, gate, coef):
    t = dxn.shape[0]
    tm = _tok_tile(t)

    def body(dxn_ref, o_ref, gt_ref, do_ref, dgt_ref):
        dxn_v = dxn_ref[...]
        do_ref[...] = ((coef * (1.0 + gt_ref[...])) * dxn_v).astype(bf16)
        _acc_add(pl.program_id(0) == 0, dgt_ref, coef * jnp.sum(dxn_v * o_ref[...], axis=0, keepdims=True))

    return pl.pallas_call(
        body, name="resid_bwd", grid=(t // tm,),
        in_specs=[_tok_spec(tm, D), _tok_spec(tm, D), _row_spec(D)],
        out_specs=[_tok_spec(tm, D), _row_spec(D)],
        out_shape=[jax.ShapeDtypeStruct((t, D), bf16), jax.ShapeDtypeStruct((1, D), f32)],
        compiler_params=_cp(1),
    )(dxn, o, gate)


def _loss_head(y, target):
    t = y.shape[0]
    tm = _tok_tile(t)

    def body(y_ref, t_ref, dy_ref, l_ref):
        err = y_ref[...] - t_ref[...]
        dy_ref[...] = err * (1.0 / D)
        part = jnp.sum(jnp.sum(err * err, axis=0, keepdims=True), axis=1, keepdims=True) * (0.5 / D)
        _acc_add(pl.program_id(0) == 0, l_ref, jnp.broadcast_to(part, (1, 128)))

    return pl.pallas_call(
        body, name="loss_head", grid=(t // tm,),
        in_specs=[_tok_spec(tm, D), _tok_spec(tm, D)],
        out_specs=[_tok_spec(tm, D), _row_spec(128)],
        out_shape=[jax.ShapeDtypeStruct((t, D), f32), jax.ShapeDtypeStruct((1, 128), f32)],
        compiler_params=_cp(1),
    )(y, target)


def _mm_tile(t):
    return min(t, 1024)


def _ffn_fwd(x, ln, w_in, w_out, idx, coef):
    h, gu, a, o, xn = _ffn_up(x, ln, w_in.reshape((2, FFN_SLABS) + w_in.shape[1:]), w_out, idx, coef)
    return xn, (x, h, gu, a, o)


def _ffn_bwd(dxn, saved, ln, w_in, w_out, idx, coef, dw_in, dw_out):
    x, h, gu, a, o = saved
    g, shift, scale, gate = ln
    t = x.shape[0]
    tm = _mm_tile(t)
    tk = _pick(t, (4096, 2048, 1024, 512, 256, 128))
    rows = w_out.shape[2]
    sds = jax.ShapeDtypeStruct
    do, dgate = _resid_bwd(dxn, o, gate, coef)
    dgu = _ffn_dgu(do, w_out, idx, gu)
    dh = _ffn_dh(dgu, w_in.reshape((2, FFN_SLABS) + w_in.shape[1:]), idx)
    dgu = dgu.reshape(N_DEV, t, FFN_HS)
    dw_out = _mmx(a, do, "tn", "ffn_dwout", (FFN_SLABS, 1, t // tk),
                  pl.BlockSpec((1, tk, FFN_HS), lambda i, j, k: (i, k, 0)),
                  pl.BlockSpec((tk, D), lambda i, j, k: (k, 0)),
                  pl.BlockSpec((2, 1, rows, D), lambda i, j, k: (i, idx, 0, 0)),
                  sds(dw_out.shape, f32), into=dw_out)
    dw_in = _mmx(h, dgu, "tn", "ffn_dwin", (N_DEV, 1, t // tk),
                 pl.BlockSpec((tk, D), lambda i, j, k: (k, 0)),
                 pl.BlockSpec((1, tk, FFN_HS), lambda i, j, k: (i, k, 0)),
                 pl.BlockSpec((1, 1, D, FFN_HS), lambda i, j, k: (i, idx, 0, 0)),
                 sds(dw_in.shape, f32), into=dw_in)
    dx, dg, dscale, dshift = _ln_bwd(x, g, scale, shift, dh, dxn)
    return dx, (dg, dshift, dscale, dgate), dw_in, dw_out


def _roll_rows(a, k):
    n = a.shape[0]
    return pltpu.roll(a, k % n, 0)


def _pool_windows(hx, first_row, reverse):
    outs = []
    for gi, w in enumerate(POOL_WINDOWS):
        acc = hx[:, gi * POOL_GROUP:(gi + 1) * POOL_GROUP]
        k = 1
        while k < w:
            acc = acc + _roll_rows(acc, -k if reverse else k)
            k *= 2
        outs.append(acc)
    return outs


def _pool_cnt(t_idx, w):
    return jnp.minimum(t_idx + 1, w).astype(f32)


def _pool_pooled(x_ref, xp_ref, g, scale, shift, i, tm):
    h = _adaln(x_ref[...], g, scale, shift)
    hp = _adaln(xp_ref[...], g, scale, shift)
    hp = jnp.where(i == 0, 0.0, hp)
    hx = jnp.concatenate([hp, h], axis=0)
    sums = _pool_windows(hx, 0, False)
    t_idx = i * tm + lax.broadcasted_iota(jnp.int32, (tm, 1), 0)
    pooled = []
    for gi, w in enumerate(POOL_WINDOWS):
        s = sums[gi][POOL_HALO:, :]
        pooled.append(s / _pool_cnt(t_idx, w) - h[:, gi * POOL_GROUP:(gi + 1) * POOL_GROUP])
    return h, pooled


def _pool_specs(t, tm):
    per = tm // POOL_HALO
    prev = pl.BlockSpec((POOL_HALO, D), lambda i: (jnp.maximum(i * per - 1, 0), 0))
    return [_tok_spec(tm, D), prev, _row_spec(D), _row_spec(D), _row_spec(D),
            pl.BlockSpec((4, POOL_GROUP, POOL_GROUP), lambda i: (0, 0, 0)), _row_spec(D), _row_spec(D)]


def _pool_fwd(x, ln, w, pscale):
    g, shift, scale, gate = ln
    t = x.shape[0]
    tm = _tok_tile(t)

    def body(x_ref, xp_ref, g_ref, sc_ref, sh_ref, w_ref, ps_ref, gt_ref, y_ref):
        i = pl.program_id(0)
        _, pooled = _pool_pooled(x_ref, xp_ref, g_ref[...], sc_ref[...], sh_ref[...], i, tm)
        mixed = [jnp.dot(pooled[gi].astype(bf16), w_ref[gi], preferred_element_type=f32) for gi in range(4)]
        y = jnp.concatenate(mixed, axis=1) * ps_ref[...]
        y_ref[...] = x_ref[...] + (1.0 + gt_ref[...]) * y

    return pl.pallas_call(
        body, name="pool_fwd", grid=(t // tm,), in_specs=_pool_specs(t, tm),
        out_specs=_tok_spec(tm, D), out_shape=jax.ShapeDtypeStruct((t, D), f32),
        compiler_params=_cp(1),
    )(x, x, g, scale, shift, w, pscale, gate)


def _pool_bwd(dxn, x, ln, w, pscale):
    g, shift, scale, gate = ln
    t = x.shape[0]
    tm = _tok_tile(t)
    nt = t // tm
    per = tm // POOL_HALO

    def body_a(x_ref, xp_ref, g_ref, sc_ref, sh_ref, w_ref, ps_ref, gt_ref, dxn_ref,
               dp_ref, dw_ref, dps_ref, dgt_ref):
        i = pl.program_id(0)
        first = i == 0
        _, pooled = _pool_pooled(x_ref, xp_ref, g_ref[...], sc_ref[...], sh_ref[...], i, tm)
        dxn_v = dxn_ref[...]
        dy = (1.0 + gt_ref[...]) * dxn_v
        dmixed = dy * ps_ref[...]
        mixed, dps = [], []
        for gi in range(4):
            sl = slice(gi * POOL_GROUP, (gi + 1) * POOL_GROUP)
            pb = pooled[gi].astype(bf16)
            dmb = dmixed[:, sl].astype(bf16)
            mixed.append(jnp.dot(pb, w_ref[gi], preferred_element_type=f32))
            dp_ref[:, sl] = lax.dot_general(dmb, w_ref[gi], _DN["nt"], preferred_element_type=f32)
            dwg = lax.dot_general(pb, dmb, _DN["tn"], preferred_element_type=f32)

            @pl.when(first)
            def _():
                dw_ref[gi] = dwg

            @pl.when(jnp.logical_not(first))
            def _():
                dw_ref[gi] += dwg
        mixed = jnp.concatenate(mixed, axis=1)
        _acc_add(first, dps_ref, jnp.sum(dy * mixed, axis=0, keepdims=True))
        _acc_add(first, dgt_ref, jnp.sum(dxn_v * (mixed * ps_ref[...]), axis=0, keepdims=True))

    row = jax.ShapeDtypeStruct((1, D), f32)
    dpooled, dw, dps, dgate = pl.pallas_call(
        body_a, name="pool_bwd_a", grid=(nt,), in_specs=_pool_specs(t, tm) + [_tok_spec(tm, D)],
        out_specs=[_tok_spec(tm, D), pl.BlockSpec((4, POOL_GROUP, POOL_GROUP), lambda i: (0, 0, 0)),
                   _row_spec(D), _row_spec(D)],
        out_shape=[jax.ShapeDtypeStruct((t, D), f32), jax.ShapeDtypeStruct((4, POOL_GROUP, POOL_GROUP), f32), row, row],
        compiler_params=_cp(1),
    )(x, x, g, scale, shift, w, pscale, gate, dxn)

    def body_b(dp_ref, dpn_ref, x_ref, g_ref, sc_ref, sh_ref, dxn_ref, dx_ref, dg_ref, dsc_ref, dsh_ref):
        i = pl.program_id(0)
        dp = dp_ref[...]
        dpn = jnp.where(i == nt - 1, 0.0, dpn_ref[...])
        t_idx = i * tm + lax.broadcasted_iota(jnp.int32, (tm + POOL_HALO, 1), 0)
        ex = jnp.concatenate([dp, dpn], axis=0)
        parts = []
        for gi, w_ in enumerate(POOL_WINDOWS):
            parts.append(ex[:, gi * POOL_GROUP:(gi + 1) * POOL_GROUP] / _pool_cnt(t_idx, w_))
        sums = _pool_windows(jnp.concatenate(parts, axis=1), 0, True)
        dh = jnp.concatenate([s[:tm, :] for s in sums], axis=1) - dp
        _, vjp = jax.vjp(_adaln, x_ref[...], g_ref[...], sc_ref[...], sh_ref[...])
        dx, dg, dsc, dsh = vjp(dh)
        dx_ref[...] = dxn_ref[...] + dx
        first = i == 0
        _acc_add(first, dg_ref, dg)
        _acc_add(first, dsc_ref, dsc)
        _acc_add(first, dsh_ref, dsh)

    nxt = pl.BlockSpec((POOL_HALO, D), lambda i: (jnp.minimum((i + 1) * per, t // POOL_HALO - 1), 0))
    dx, dg, dscale, dshift = pl.pallas_call(
        body_b, name="pool_bwd_b", grid=(nt,),
        in_specs=[_tok_spec(tm, D), nxt, _tok_spec(tm, D), _row_spec(D), _row_spec(D), _row_spec(D), _tok_spec(tm, D)],
        out_specs=[_tok_spec(tm, D), _row_spec(D), _row_spec(D), _row_spec(D)],
        out_shape=[jax.ShapeDtypeStruct((t, D), f32), row, row, row],
        compiler_params=_cp(1),
    )(dpooled, dpooled, x, g, scale, shift, dxn)
    return dx, (dg, dshift, dscale, dgate), dw, dps


def _conv_taps(czx, cw):
    return cw[0:1, :] * _roll_rows(czx, 2) + cw[1:2, :] * _roll_rows(czx, 1) + cw[2:3, :] * czx


def _conv_fwd(p, cw):
    t = p.shape[0]
    tm = _tok_tile(t)
    per = tm // CONV_HALO

    def body(p_ref, pp_ref, cw_ref, q_ref):
        i = pl.program_id(0)
        cz = p_ref[:, D:2 * D] * p_ref[:, 2 * D:3 * D]
        czp = jnp.where(i == 0, 0.0, pp_ref[:, D:2 * D] * pp_ref[:, 2 * D:3 * D])
        conv = _conv_taps(jnp.concatenate([czp, cz], axis=0), cw_ref[...])[CONV_HALO:, :]
        q_ref[...] = (p_ref[:, 0:D] * conv).astype(bf16)

    prev = pl.BlockSpec((CONV_HALO, 3 * D), lambda i: (jnp.maximum(i * per - 1, 0), 0))
    return pl.pallas_call(
        body, name="conv_fwd", grid=(t // tm,),
        in_specs=[_tok_spec(tm, 3 * D), prev, _row_spec(D, 8)],
        out_specs=_tok_spec(tm, D), out_shape=jax.ShapeDtypeStruct((t, D), bf16),
        compiler_params=_cp(1),
    )(p, p, cw)


def _conv_bwd(p, cw, dq):
    t = p.shape[0]
    tm = _tok_tile(t)
    nt = t // tm
    per = tm // CONV_HALO

    def body(p_ref, pp_ref, pn_ref, cw_ref, dq_ref, dqn_ref, dp_ref, dcw_ref):
        i = pl.program_id(0)
        cw_v = cw_ref[...]
        b, c, z = p_ref[:, 0:D], p_ref[:, D:2 * D], p_ref[:, 2 * D:3 * D]
        cz = c * z
        czp = jnp.where(i == 0, 0.0, pp_ref[:, D:2 * D] * pp_ref[:, 2 * D:3 * D])
        czx = jnp.concatenate([czp, cz], axis=0)
        conv = _conv_taps(czx, cw_v)[CONV_HALO:, :]
        dq_v = dq_ref[...]
        dconv = dq_v * b
        dconv_n = jnp.where(i == nt - 1, 0.0, dqn_ref[...] * pn_ref[:, 0:D])
        dcx = jnp.concatenate([dconv, dconv_n], axis=0)
        dcz = (cw_v[2:3, :] * dcx + cw_v[1:2, :] * _roll_rows(dcx, -1) + cw_v[0:1, :] * _roll_rows(dcx, -2))[:tm, :]
        dp_ref[:, 0:D] = (dq_v * conv).astype(bf16)
        dp_ref[:, D:2 * D] = (dcz * z).astype(bf16)
        dp_ref[:, 2 * D:3 * D] = (dcz * c).astype(bf16)
        dw2 = jnp.sum(dconv * cz, axis=0, keepdims=True)
        dw1 = jnp.sum(dconv * _roll_rows(czx, 1)[CONV_HALO:, :], axis=0, keepdims=True)
        dw0 = jnp.sum(dconv * _roll_rows(czx, 2)[CONV_HALO:, :], axis=0, keepdims=True)
        _acc_add(i == 0, dcw_ref, jnp.concatenate([dw0, dw1, dw2, jnp.zeros((5, D), f32)], axis=0))

    prev = pl.BlockSpec((CONV_HALO, 3 * D), lambda i: (jnp.maximum(i * per - 1, 0), 0))
    last = t // CONV_HALO - 1
    nxt3 = pl.BlockSpec((CONV_HALO, 3 * D), lambda i: (jnp.minimum((i + 1) * per, last), 0))
    nxt1 = pl.BlockSpec((CONV_HALO, D), lambda i: (jnp.minimum((i + 1) * per, last), 0))
    return pl.pallas_call(
        body, name="conv_bwd", grid=(nt,),
        in_specs=[_tok_spec(tm, 3 * D), prev, nxt3, _row_spec(D, 8), _tok_spec(tm, D), nxt1],
        out_specs=[_tok_spec(tm, 3 * D), _row_spec(D, 8)],
        out_shape=[jax.ShapeDtypeStruct((t, 3 * D), bf16), jax.ShapeDtypeStruct((8, D), f32)],
        compiler_params=_cp(1),
    )(p, p, p, cw, dq, dq)


def _conv_tile(t):
    return min(t, 2048)


def _convmix_fwd(x, ln, w_in, cw, w_out):
    g, shift, scale, gate = ln
    t = x.shape[0]
    tm = _conv_tile(t)
    cs = w_in.shape[2]
    h = _ln_fwd(x, g, scale, shift)
    p = _mmx(h, w_in, "nn", "conv_in", (t // tm, N_DEV, 1),
             pl.BlockSpec((tm, D), lambda i, j, k: (i, 0)),
             pl.BlockSpec((1, D, cs), lambda i, j, k: (j, 0, 0)),
             pl.BlockSpec((tm, cs), lambda i, j, k: (i, j)), jax.ShapeDtypeStruct((t, N_DEV * cs), f32))
    q = _conv_fwd(p, cw)
    y = _mm(q, w_out, "nn", f32, "conv_out")
    return _resid_fwd(x, y, gate, 1.0), (x, h, p, q, y)


def _convmix_bwd(dxn, saved, ln, w_in, cw, w_out):
    x, h, p, q, y = saved
    g, shift, scale, gate = ln
    dy, dgate = _resid_bwd(dxn, y, gate, 1.0)
    dq = _mm(dy, w_out, "nt", f32, "conv_dq")
    dw_out = _mm(q, dy, "tn", f32, "conv_dwout")
    dp, dcw = _conv_bwd(p, cw, dq)
    t = x.shape[0]
    tm = _conv_tile(t)
    tk = _pick(t, (2048, 1024, 512, 256, 128))
    cs = w_in.shape[2]
    dh = _mmx(dp, w_in, "nt", "conv_dh", (t // tm, 1, N_DEV),
              pl.BlockSpec((tm, cs), lambda i, j, k: (i, k)),
              pl.BlockSpec((1, D, cs), lambda i, j, k: (k, 0, 0)),
              pl.BlockSpec((tm, D), lambda i, j, k: (i, 0)), jax.ShapeDtypeStruct((t, D), f32))
    dw_in = _mmx(h, dp, "tn", "conv_dwin", (N_DEV, 1, t // tk),
                 pl.BlockSpec((tk, D), lambda i, j, k: (k, 0)),
                 pl.BlockSpec((tk, cs), lambda i, j, k: (k, i)),
                 pl.BlockSpec((1, D, cs), lambda i, j, k: (i, 0, 0)), jax.ShapeDtypeStruct((N_DEV, D, cs), f32))
    dx, dg, dscale, dshift = _ln_bwd(x, g, scale, shift, dh, dxn)
    return dx, (dg, dshift, dscale, dgate), dw_in, dcw[0:3], dw_out


def _exact_dot(tri, v):
    v1 = v.astype(bf16)
    r1 = v - v1.astype(f32)
    v2 = r1.astype(bf16)
    v3 = (r1 - v2.astype(f32)).astype(bf16)
    d = lambda p: jnp.dot(tri, p, preferred_element_type=f32)
    return d(v1) + d(v2) + d(v3)


def _fox_cumf(fl, b_f):
    t = fl.shape[0]
    tc = min(t, 256)

    def body(fl_ref, b_ref, f_ref, carry_ref):
        i = pl.program_id(0)

        @pl.when(i == 0)
        def _():
            carry_ref[...] = jnp.zeros_like(carry_ref)

        lf = jax.nn.log_sigmoid(fl_ref[...] + b_ref[...])
        r = lax.broadcasted_iota(jnp.int32, (tc, tc), 0)
        c = lax.broadcasted_iota(jnp.int32, (tc, tc), 1)
        tri = (r >= c).astype(bf16)
        fc = _exact_dot(tri, lf) + carry_ref[0:1, :]
        f_ref[...] = fc
        carry_ref[0:1, :] = fc[tc - 1:tc, :]

    return pl.pallas_call(
        body, name="fox_cumf", grid=(t // tc,),
        in_specs=[_tok_spec(tc, 128), _row_spec(128)],
        out_specs=_tok_spec(tc, 128), out_shape=jax.ShapeDtypeStruct((t, 128), f32),
        scratch_shapes=[pltpu.VMEM((8, 128), f32)], compiler_params=_cp(1),
    )(fl, b_f)


def _fox_cumf_bwd(df, fl, b_f):
    t = fl.shape[0]
    tc = min(t, 256)
    nt = t // tc

    def body(df_ref, fl_ref, b_ref, dfl_ref, db_ref, carry_ref):
        i = pl.program_id(0)

        @pl.when(i == 0)
        def _():
            carry_ref[...] = jnp.zeros_like(carry_ref)

        r = lax.broadcasted_iota(jnp.int32, (tc, tc), 0)
        c = lax.broadcasted_iota(jnp.int32, (tc, tc), 1)
        tri = (r <= c).astype(bf16)
        dlf = _exact_dot(tri, df_ref[...]) + carry_ref[0:1, :]
        carry_ref[0:1, :] = dlf[0:1, :]
        dfl = dlf * jax.nn.sigmoid(-(fl_ref[...] + b_ref[...]))
        dfl_ref[...] = dfl
        _acc_add(i == 0, db_ref, jnp.sum(dfl, axis=0, keepdims=True))

    return pl.pallas_call(
        body, name="fox_cumf_bwd", grid=(nt,),
        in_specs=[_tok_spec(tc, 128, nt, True), _tok_spec(tc, 128, nt, True), _row_spec(128)],
        out_specs=[_tok_spec(tc, 128, nt, True), _row_spec(128)],
        out_shape=[jax.ShapeDtypeStruct((t, 128), f32), jax.ShapeDtypeStruct((1, 128), f32)],
        scratch_shapes=[pltpu.VMEM((8, 128), f32)], compiler_params=_cp(1),
    )(df, fl, b_f)


def _fox_block(t):
    return min(t, 512)


FOX_STRIP = 64


def _fox_mask(s, row0):
    r = row0 + lax.broadcasted_iota(jnp.int32, s.shape, 0)
    c = lax.broadcasted_iota(jnp.int32, s.shape, 1)
    return jnp.where(r >= c, s, -jnp.inf)


def _strips(tb):
    ts = min(tb, FOX_STRIP)
    return [(r * ts, slice(r * ts, (r + 1) * ts)) for r in range(tb // ts)]


FOX_PAIRS = FOX_HEADS // 2
FOX_PW = 2 * FOX_HEAD_DIM


def _half(shape, hh):
    lane = lax.broadcasted_iota(jnp.int32, shape, len(shape) - 1)
    return lane < FOX_HEAD_DIM if hh == 0 else lane >= FOX_HEAD_DIM


def _by_half(v0, v1):
    return jnp.where(_half(v0.shape, 0), v0, v1)


def _half_sums(v):
    first = _half(v.shape, 0)
    s0 = jnp.sum(jnp.where(first, v, 0.0), axis=-1, keepdims=True)
    s1 = jnp.sum(jnp.where(first, 0.0, v), axis=-1, keepdims=True)
    return jnp.where(first, s0, s1)


def _pair_norm(v, gain, mult):
    ms = _half_sums(v * v) * (1.0 / FOX_HEAD_DIM)
    return v * lax.rsqrt(ms + NORM_EPS) * gain * mult


def _pair_cols(p):
    return slice(p * FOX_PW, (p + 1) * FOX_PW)


def _pair_qknorm(proj, q_gain, k_gain):
    t = proj.shape[0]
    tm = _tok_tile(t)

    def body(q_ref, k_ref, v_ref, qg_ref, kg_ref, o_ref):
        for p in range(FOX_PAIRS):
            cb = _pair_cols(p)
            o_ref[:, cb] = _pair_norm(q_ref[:, cb], qg_ref[...], FOX_HEAD_DIM ** -0.5).astype(bf16)
            o_ref[:, D + p * FOX_PW:D + (p + 1) * FOX_PW] = _pair_norm(k_ref[:, cb], kg_ref[...], 1.0).astype(bf16)
        o_ref[:, 2 * D:3 * D] = v_ref[...].astype(bf16)

    part = lambda s: pl.BlockSpec((tm, D), lambda i: (i, s))
    return pl.pallas_call(
        body, name="fox_qknorm", grid=(t // tm,),
        in_specs=[part(0), part(1), part(2), _row_spec(FOX_PW), _row_spec(FOX_PW)],
        out_specs=_tok_spec(tm, 3 * D), out_shape=jax.ShapeDtypeStruct((t, 3 * D), bf16), compiler_params=_cp(1),
    )(proj, proj, proj, q_gain, k_gain)


def _pair_qknorm_bwd(proj, q_gain, k_gain, dqn, dkn):
    t = proj.shape[0]
    tm = _tok_tile(t)

    def body(q_ref, k_ref, qg_ref, kg_ref, dqn_ref, dkn_ref, o_ref, dqg_ref, dkg_ref):
        dqg = jnp.zeros((1, FOX_PW), f32)
        dkg = jnp.zeros((1, FOX_PW), f32)
        for p in range(FOX_PAIRS):
            cb = _pair_cols(p)
            _, vq = jax.vjp(lambda v, gn: _pair_norm(v, gn, FOX_HEAD_DIM ** -0.5), q_ref[:, cb], qg_ref[...])
            dq, g1 = vq(dqn_ref[:, cb])
            _, vk = jax.vjp(lambda v, gn: _pair_norm(v, gn, 1.0), k_ref[:, cb], kg_ref[...])
            dk, g2 = vk(dkn_ref[:, cb])
            o_ref[:, cb] = dq.astype(bf16)
            o_ref[:, D + p * FOX_PW:D + (p + 1) * FOX_PW] = dk.astype(bf16)
            dqg, dkg = dqg + g1, dkg + g2
        first = pl.program_id(0) == 0
        _acc_add(first, dqg_ref, dqg + pltpu.roll(dqg, FOX_HEAD_DIM, 1))
        _acc_add(first, dkg_ref, dkg + pltpu.roll(dkg, FOX_HEAD_DIM, 1))

    part = lambda s: pl.BlockSpec((tm, D), lambda i: (i, s))
    gshape = jax.ShapeDtypeStruct((1, FOX_PW), f32)
    return pl.pallas_call(
        body, name="fox_qknorm_bwd", grid=(t // tm,),
        in_specs=[part(0), part(1), _row_spec(FOX_PW), _row_spec(FOX_PW), _tok_spec(tm, D), _tok_spec(tm, D)],
        out_specs=[_tok_spec(tm, 2 * D), _row_spec(FOX_PW), _row_spec(FOX_PW)],
        out_shape=[jax.ShapeDtypeStruct((t, 2 * D), bf16), gshape, gshape], compiler_params=_cp(1),
    )(proj, proj, q_gain, k_gain, dqn, dkn)


def _pair_delta(o, do):
    t = o.shape[0]
    tm = _tok_tile(t)

    def body(o_ref, do_ref, dl_ref, dob_ref):
        dob = do_ref[...].astype(bf16)
        dob_ref[...] = dob
        prod = dob.astype(f32) * o_ref[...]
        for p in range(FOX_PAIRS):
            dl_ref[p] = _half_sums(prod[:, _pair_cols(p)])

    return pl.pallas_call(
        body, name="fox_delta", grid=(t // tm,), in_specs=[_tok_spec(tm, D), _tok_spec(tm, D)],
        out_specs=[pl.BlockSpec((FOX_PAIRS, tm, FOX_PW), lambda i: (0, i, 0)), _tok_spec(tm, D)],
        out_shape=[jax.ShapeDtypeStruct((FOX_PAIRS, t, FOX_PW), f32), jax.ShapeDtypeStruct((t, D), bf16)],
        compiler_params=_cp(1),
    )(o, do)


def _pair_attn_fwd(qkvn, fcol, fref):
    t = qkvn.shape[0]
    tb = _fox_block(t)
    nq = t // tb
    pw = FOX_PW
    kcol, vcol = D // pw, 2 * D // pw

    def body(q_ref, k_ref, v_ref, fc_ref, fr_ref, o_ref, ob_ref, lse_ref):
        i = pl.program_id(1)
        qp = q_ref[...]
        qh = [jnp.where(_half(qp.shape, hh), qp, jnp.zeros_like(qp)) for hh in range(2)]

        def step(j, carry, diag):
            off = pl.multiple_of(j * tb, tb)
            kj = k_ref[pl.ds(off, tb), :]
            vj = v_ref[pl.ds(off, tb), :]
            out = []
            for hh in range(2):
                m, l, acc = carry[hh]
                bias = fr_ref[0, hh, pl.ds(i, 1), 0:1] - fc_ref[0, hh:hh + 1, pl.ds(off, tb)]
                s = lax.dot_general(qh[hh], kj, _DN["nt"], preferred_element_type=f32) + bias
                if diag:
                    s = _fox_mask(s, 0)
                m_new = jnp.maximum(m, jnp.max(s, axis=-1, keepdims=True))
                alpha = jnp.exp(m - m_new)
                p = jnp.exp(s - m_new)
                l = alpha * l + jnp.sum(p, axis=-1, keepdims=True)
                p_hi = p.astype(bf16)
                p_lo = (p - p_hi.astype(f32)).astype(bf16)
                pv = jnp.dot(p_hi, vj, preferred_element_type=f32) + jnp.dot(p_lo, vj, preferred_element_type=f32)
                out.append((m_new, l, alpha * acc + pv))
            return tuple(out)

        one = (jnp.full((tb, 1), -jnp.inf, f32), jnp.zeros((tb, 1), f32), jnp.zeros((tb, pw), f32))
        carry = lax.fori_loop(0, i, lambda j, c: step(j, c, False), (one, one))
        (m0, l0, a0), (m1, l1, a1) = step(i, carry, True)
        o = _by_half(a0 / l0, a1 / l1)
        o_ref[...] = o
        ob_ref[...] = o.astype(bf16)
        lse_ref[0] = _by_half(jnp.broadcast_to(m0 + jnp.log(l0), (tb, pw)), jnp.broadcast_to(m1 + jnp.log(l1), (tb, pw)))

    return pl.pallas_call(
        body, name="fox_attn_fwd", grid=(FOX_PAIRS, nq),
        in_specs=[pl.BlockSpec((tb, pw), lambda p, i: (i, p)),
                  pl.BlockSpec((t, pw), lambda p, i: (0, kcol + p)),
                  pl.BlockSpec((t, pw), lambda p, i: (0, vcol + p)),
                  pl.BlockSpec((1, 2, t), lambda p, i: (p, 0, 0)),
                  pl.BlockSpec((1, 2, nq, 128), lambda p, i: (p, 0, 0, 0))],
        out_specs=[pl.BlockSpec((tb, pw), lambda p, i: (i, p)), pl.BlockSpec((tb, pw), lambda p, i: (i, p)),
                   pl.BlockSpec((1, tb, pw), lambda p, i: (p, i, 0))],
        out_shape=[jax.ShapeDtypeStruct((t, D), f32), jax.ShapeDtypeStruct((t, D), bf16),
                   jax.ShapeDtypeStruct((FOX_PAIRS, t, pw), f32)],
        compiler_params=_cp(2),
    )(qkvn, qkvn, qkvn, fcol, fref)


def _pair_attn_bwd(qkvn, fcol, fref, lse, delta, dob):
    t = qkvn.shape[0]
    tb = _fox_block(t)
    nq = t // tb
    pw = FOX_PW
    kcol, vcol = D // pw, 2 * D // pw

    def body(q_ref, k_ref, v_ref, fc_ref, fr_ref, lse_ref, dl_ref, do_ref, dk_ref, dv_ref, df_ref, dq_ref,
             s_ref, dp_ref, p_ref, ds_ref):
        j = pl.program_id(1)
        kj = k_ref[...]
        vj = v_ref[...]
        dk_ref[...] = jnp.zeros_like(dk_ref)
        dv_ref[...] = jnp.zeros_like(dv_ref)
        df_ref[...] = jnp.zeros_like(df_ref)

        @pl.when(j == 0)
        def _():
            dq_ref[...] = jnp.zeros_like(dq_ref)

        def step(i, diag):
            off = pl.multiple_of(i * tb, tb)
            qi = q_ref[pl.ds(off, tb), :]
            doi = do_ref[pl.ds(off, tb), :]
            parts = []
            for hh in range(2):
                mine = _half(qi.shape, hh)
                c0 = hh * FOX_HEAD_DIM
                s_ref[hh] = lax.dot_general(jnp.where(mine, qi, jnp.zeros_like(qi)), kj, _DN["nt"],
                                            preferred_element_type=f32)
                dp_ref[hh] = lax.dot_general(jnp.where(mine, doi, jnp.zeros_like(doi)), vj, _DN["nt"],
                                             preferred_element_type=f32)
                bias = fr_ref[0, hh, pl.ds(i, 1), 0:1] - fc_ref[0, hh:hh + 1, :]
                df = jnp.zeros((1, tb), f32)
                for row0, rs in _strips(tb):
                    rows = pl.ds(off + row0, rs.stop - rs.start)
                    s = s_ref[hh, rs, :] + bias
                    if diag:
                        s = _fox_mask(s, row0)
                    p = jnp.exp(s - lse_ref[0, rows, c0:c0 + 1])
                    p_ref[hh, rs, :] = p.astype(bf16)
                    ds = p * (dp_ref[hh, rs, :] - dl_ref[0, rows, c0:c0 + 1])
                    ds_ref[hh, rs, :] = ds.astype(bf16)
                    df = df + jnp.sum(ds, axis=0, keepdims=True)
                df_ref[0, hh:hh + 1, :] -= df
                parts.append((lax.dot_general(p_ref[hh], doi, _DN["tn"], preferred_element_type=f32),
                              lax.dot_general(ds_ref[hh], qi, _DN["tn"], preferred_element_type=f32),
                              jnp.dot(ds_ref[hh], kj, preferred_element_type=f32)))
            dv_ref[...] += _by_half(parts[0][0], parts[1][0])
            dk_ref[...] += _by_half(parts[0][1], parts[1][1])
            dq_ref[pl.ds(off, tb), :] += _by_half(parts[0][2], parts[1][2])

        def loop_body(i, carry):
            step(i, False)
            return carry

        step(j, True)
        lax.fori_loop(j + 1, nq, loop_body, 0)

    scratch = [pltpu.VMEM((2, tb, tb), f32), pltpu.VMEM((2, tb, tb), f32), pltpu.VMEM((2, tb, tb), bf16),
               pltpu.VMEM((2, tb, tb), bf16)]
    blk = lambda c0: pl.BlockSpec((tb, pw), lambda p, j: (j, c0 + p))
    full = lambda c0: pl.BlockSpec((t, pw), lambda p, j: (0, c0 + p))
    stat = pl.BlockSpec((1, t, pw), lambda p, j: (p, 0, 0))
    nat = jax.ShapeDtypeStruct((t, D), f32)
    return pl.pallas_call(
        body, name="fox_attn_bwd", grid=(FOX_PAIRS, nq),
        in_specs=[full(0), blk(kcol), blk(vcol), pl.BlockSpec((1, 2, tb), lambda p, j: (p, 0, j)),
                  pl.BlockSpec((1, 2, nq, 128), lambda p, j: (p, 0, 0, 0)), stat, stat, full(0)],
        out_specs=[blk(0), blk(0), pl.BlockSpec((1, 2, tb), lambda p, j: (p, 0, j)), full(0)],
        out_shape=[nat, nat, jax.ShapeDtypeStruct((FOX_PAIRS, 2, t), f32), nat],
        scratch_shapes=scratch, compiler_params=_cp(2),
    )(qkvn, qkvn, qkvn, fcol, fref, lse, delta, dob)


def _fox_fwd(x, ln, w_in, b_f, q_gain, k_gain, w_o):
    g, shift, scale, gate = ln
    t = x.shape[0]
    tb = _fox_block(t)
    h = _ln_fwd(x, g, scale, shift)
    proj = _mm(h, w_in, "nn", f32, "fox_in")
    fl = proj[:, 3 * D:3 * D + 128]
    fcum = _fox_cumf(fl, b_f)
    fcol = fcum[:, :FOX_HEADS].T.reshape(FOX_PAIRS, 2, t)
    fref = jnp.broadcast_to(fcol[:, :, ::tb][..., None], (FOX_PAIRS, 2, t // tb, 128))
    gains = (jnp.tile(q_gain, (1, 2)), jnp.tile(k_gain, (1, 2)))
    qkvn = _pair_qknorm(proj, *gains)
    o, ob, lse = _pair_attn_fwd(qkvn, fcol, fref)
    y = _mm(ob, w_o, "nn", f32, "fox_out")
    return _resid_fwd(x, y, gate, 1.0), (x, h, proj, fl, fcol, fref, gains, qkvn, o, lse, ob, y)


def _fox_bwd(dxn, saved, ln, w_in, b_f, w_o):
    x, h, proj, fl, fcol, fref, gains, qkvn, o, lse, ob, y = saved
    g, shift, scale, gate = ln
    t = x.shape[0]
    dy, dgate = _resid_bwd(dxn, y, gate, 1.0)
    do = _mm(dy, w_o, "nt", f32, "fox_do")
    dw_o = _mm(ob, dy, "tn", f32, "fox_dwo")
    delta, dob = _pair_delta(o, do)
    dkn, dv, dfcol, dqn = _pair_attn_bwd(qkvn, fcol, fref, lse, delta, dob)
    dqk, dqg, dkg = _pair_qknorm_bwd(proj, *gains, dqn, dkn)
    df = jnp.pad(dfcol.reshape(FOX_HEADS, t).T, ((0, 0), (0, 128 - FOX_HEADS)))
    dfl, db_f = _fox_cumf_bwd(df, fl, b_f)
    dproj = jnp.concatenate([dqk, dv.astype(bf16), dfl.astype(bf16)], axis=1)
    dh = _mm(dproj, w_in, "nt", f32, "fox_dh")
    dw_in = _mm(h, dproj, "tn", f32, "fox_dwin")
    dx, dg, dscale, dshift = _ln_bwd(x, g, scale, shift, dh, dxn)
    return (dx, (dg, dshift, dscale, dgate), dw_in, db_f, dqg[:, :FOX_HEAD_DIM], dkg[:, :FOX_HEAD_DIM], dw_o)


def _s5_disc(lam_re, lam_im, log_dt, b_re, b_im):
    dt = jnp.exp(log_dt)
    mag = jnp.exp(lam_re * dt)
    lb_re, lb_im = mag * jnp.cos(lam_im * dt), mag * jnp.sin(lam_im * dt)
    den = lam_re * lam_re + lam_im * lam_im
    nr, ni = lb_re - 1.0, lb_im
    k_re = (nr * lam_re + ni * lam_im) / den
    k_im = (ni * lam_re - nr * lam_im) / den
    return lb_re, lb_im, k_re * b_re - k_im * b_im, k_re * b_im + k_im * b_re


def _s5_prep(lam_re, lam_im, log_dt, b_re, b_im):
    def body(ar_ref, ai_ref, dt_ref, br_ref, bi_ref, lr_ref, li_ref, bbr_ref, bbi_ref):
        lr, li, bbr, bbi = _s5_disc(ar_ref[...], ai_ref[...], dt_ref[...], br_ref[...], bi_ref[...])
        lr_ref[...] = lr
        li_ref[...] = li
        bbr_ref[...] = bbr
        bbi_ref[...] = bbi

    small = jax.ShapeDtypeStruct(lam_re.shape, f32)
    bigs = jax.ShapeDtypeStruct(b_re.shape, f32)
    return pl.pallas_call(body, name="s5_prep", out_shape=[small, small, bigs, bigs])(lam_re, lam_im, log_dt, b_re, b_im)


def _s5_prep_bwd(lam_re, lam_im, log_dt, b_re, b_im, dlr, dli, dbbr, dbbi):
    def body(ar_ref, ai_ref, dt_ref, br_ref, bi_ref, dlr_ref, dli_ref, dbbr_ref, dbbi_ref,
             dar_ref, dai_ref, ddt_ref, dbr_ref, dbi_ref):
        _, vjp = jax.vjp(_s5_disc, ar_ref[...], ai_ref[...], dt_ref[...], br_ref[...], bi_ref[...])
        dar, dai, ddt, dbr, dbi = vjp((dlr_ref[...], dli_ref[...], dbbr_ref[...], dbbi_ref[...]))
        dar_ref[...] = dar
        dai_ref[...] = dai
        ddt_ref[...] = jnp.broadcast_to(jnp.sum(ddt, axis=-1, keepdims=True), ddt.shape)
        dbr_ref[...] = dbr
        dbi_ref[...] = dbi

    small = jax.ShapeDtypeStruct(lam_re.shape, f32)
    bigs = jax.ShapeDtypeStruct(b_re.shape, f32)
    return pl.pallas_call(body, name="s5_prep_bwd", out_shape=[small, small, small, bigs, bigs])(
        lam_re, lam_im, log_dt, b_re, b_im, dlr, dli, dbbr, dbbi)


def _s5_tile(t):
    return min(t, 128)


def _s5_fwd_tile(t):
    return min(t, 256)


def _s5_blk(k, width):
    return slice(k * width, (k + 1) * width)


def _s5_in_bd(bb):
    b4 = bb.reshape(S5_BLOCKS, 8, S5_GROUP, S5_STATE)
    return jnp.einsum("kgin,gh->kgihn", b4, jnp.eye(8, dtype=bb.dtype)).reshape(S5_BLOCKS, S5_BCH, S5_BST)


def _s5_in_bd_diag(bd):
    b5 = bd.reshape(S5_BLOCKS, 8, S5_GROUP, 8, S5_STATE)
    return jnp.einsum("kgihn,gh->kgin", b5, jnp.eye(8, dtype=bd.dtype)).reshape(S5_GROUPS, S5_GROUP, S5_STATE)


def _s5_out_bd(c):
    c4 = c.reshape(S5_BLOCKS, 8, S5_GROUP, S5_STATE)
    return jnp.einsum("kgin,gh->kgnhi", c4, jnp.eye(8, dtype=c.dtype)).reshape(S5_BLOCKS, S5_BST, S5_BCH)


def _s5_out_bd_diag(bd):
    c5 = bd.reshape(S5_BLOCKS, 8, S5_STATE, 8, S5_GROUP)
    return jnp.einsum("kgnhi,gh->kgin", c5, jnp.eye(8, dtype=bd.dtype)).reshape(S5_GROUPS, S5_GROUP, S5_STATE)


def _s5_scan_fwd(x, ln, lb_re, lb_im, bbr_bd, bbi_bd, cr_bd, ci_bd, dskip):
    g, shift, scale, _ = ln
    t = x.shape[0]
    tm = _s5_fwd_tile(t)
    ns = S5_NSTATE

    def body(x_ref, g_ref, sc_ref, sh_ref, ar_ref, ai_ref, bbr_ref, bbi_ref, cr_ref, ci_ref, d_ref,
             yy_ref, xr_ref, xi_ref, cre_ref, cim_ref):
        @pl.when(pl.program_id(0) == 0)
        def _():
            cre_ref[...] = jnp.zeros_like(cre_ref)
            cim_ref[...] = jnp.zeros_like(cim_ref)

        h = _adaln(x_ref[...], g_ref[...], sc_ref[...], sh_ref[...])
        ub = h.astype(bf16)
        for k in range(S5_BLOCKS):
            uk = ub[:, _s5_blk(k, S5_BCH)]
            xr_ref[:, _s5_blk(k, S5_BST)] = jnp.dot(uk, bbr_ref[k], preferred_element_type=f32)
            xi_ref[:, _s5_blk(k, S5_BST)] = jnp.dot(uk, bbi_ref[k], preferred_element_type=f32)
        ar, ai = ar_ref[...], ai_ref[...]

        def step(tt, carry):
            sr, si = carry
            row = pl.ds(tt, 1)
            nr = (ar * sr - ai * si) + xr_ref[row, :]
            ni = (ar * si + ai * sr) + xi_ref[row, :]
            xr_ref[row, :] = nr
            xi_ref[row, :] = ni
            return nr, ni

        sr, si = lax.fori_loop(0, tm, step, (cre_ref[0:1, :], cim_ref[0:1, :]), unroll=2)
        cre_ref[0:1, :] = sr
        cim_ref[0:1, :] = si
        for k in range(S5_BLOCKS):
            sb = _s5_blk(k, S5_BST)
            yk = (jnp.dot(xr_ref[:, sb].astype(bf16), cr_ref[k], preferred_element_type=f32)
                  - jnp.dot(xi_ref[:, sb].astype(bf16), ci_ref[k], preferred_element_type=f32))
            cb = _s5_blk(k, S5_BCH)
            yy_ref[:, cb] = yk + d_ref[:, cb] * h[:, cb]

    bd_in = pl.BlockSpec((S5_BLOCKS, S5_BCH, S5_BST), lambda i: (0, 0, 0))
    bd_out = pl.BlockSpec((S5_BLOCKS, S5_BST, S5_BCH), lambda i: (0, 0, 0))
    st = jax.ShapeDtypeStruct((t, ns), f32)
    return pl.pallas_call(
        body, name="s5_scan_fwd", grid=(t // tm,),
        in_specs=[_tok_spec(tm, D), _row_spec(D), _row_spec(D), _row_spec(D), _row_spec(ns), _row_spec(ns),
                  bd_in, bd_in, bd_out, bd_out, _row_spec(D)],
        out_specs=[_tok_spec(tm, D), _tok_spec(tm, ns), _tok_spec(tm, ns)],
        out_shape=[jax.ShapeDtypeStruct((t, D), f32), st, st],
        scratch_shapes=[pltpu.VMEM((8, ns), f32), pltpu.VMEM((8, ns), f32)],
        compiler_params=_cp(1),
    )(x, g, scale, shift, lb_re, lb_im, bbr_bd, bbi_bd, cr_bd, ci_bd, dskip)


def _s5_scan_bwd(dyy, x, ln, xr, xi, lb_re, lb_im, bbr_bd, bbi_bd, cr_bd, ci_bd, dskip):
    g, shift, scale, _ = ln
    t = x.shape[0]
    tm = _s5_tile(t)
    nt = t // tm
    ns = S5_NSTATE
    per = tm // 8

    def body(dyy_ref, x_ref, g_ref, sc_ref, sh_ref, xr_ref, xi_ref, xrp_ref, xip_ref, ar_ref, ai_ref,
             bbr_ref, bbi_ref, cr_ref, ci_ref, d_ref,
             du_ref, dar_ref, dai_ref, dbbr_ref, dbbi_ref, dcr_ref, dci_ref, dd_ref,
             gr_ref, gi_ref, cre_ref, cim_ref):
        i = pl.program_id(0)
        first = i == 0

        @pl.when(first)
        def _():
            cre_ref[...] = jnp.zeros_like(cre_ref)
            cim_ref[...] = jnp.zeros_like(cim_ref)

        h = _adaln(x_ref[...], g_ref[...], sc_ref[...], sh_ref[...])
        ub = h.astype(bf16)
        dyy_v = dyy_ref[...]
        dyb = dyy_v.astype(bf16)
        for k in range(S5_BLOCKS):
            dk = dyb[:, _s5_blk(k, S5_BCH)]
            sb = _s5_blk(k, S5_BST)
            gr_ref[:, sb] = lax.dot_general(dk, cr_ref[k], _DN["nt"], preferred_element_type=f32)
            gi_ref[:, sb] = -lax.dot_general(dk, ci_ref[k], _DN["nt"], preferred_element_type=f32)
        ar, ai = ar_ref[...], ai_ref[...]

        def step(s, carry):
            nr_, ni_ = carry
            row = pl.ds(tm - 1 - s, 1)
            nr = gr_ref[row, :] + (ar * nr_ + ai * ni_)
            ni = gi_ref[row, :] + (ar * ni_ - ai * nr_)
            gr_ref[row, :] = nr
            gi_ref[row, :] = ni
            return nr, ni

        nr, ni = lax.fori_loop(0, tm, step, (cre_ref[0:1, :], cim_ref[0:1, :]), unroll=2)
        cre_ref[0:1, :] = nr
        cim_ref[0:1, :] = ni

        is_first_tile = i == nt - 1
        _acc_add(first, dd_ref, jnp.sum(dyy_v * h, axis=0, keepdims=True))
        for k in range(S5_BLOCKS):
            cb, sb = _s5_blk(k, S5_BCH), _s5_blk(k, S5_BST)
            xr_v, xi_v = xr_ref[:, sb], xi_ref[:, sb]
            xrp = jnp.where(is_first_tile, 0.0, xrp_ref[:, sb])
            xip = jnp.where(is_first_tile, 0.0, xip_ref[:, sb])
            xr_s = _roll_rows(jnp.concatenate([xrp, xr_v], axis=0), 1)[8:, :]
            xi_s = _roll_rows(jnp.concatenate([xip, xi_v], axis=0), 1)[8:, :]
            gr, gi = gr_ref[:, sb], gi_ref[:, sb]
            dar_k = jnp.sum(gr * xr_s + gi * xi_s, axis=0, keepdims=True)
            dai_k = jnp.sum(gi * xr_s - gr * xi_s, axis=0, keepdims=True)

            @pl.when(first)
            def _():
                dar_ref[:, sb] = dar_k
                dai_ref[:, sb] = dai_k

            @pl.when(jnp.logical_not(first))
            def _():
                dar_ref[:, sb] += dar_k
                dai_ref[:, sb] += dai_k

            grb, gib = gr.astype(bf16), gi.astype(bf16)
            uk, dk = ub[:, cb], dyb[:, cb]
            tn = lambda a_, b_: lax.dot_general(a_, b_, _DN["tn"], preferred_element_type=f32)
            vals = (tn(uk, grb), tn(uk, gib), tn(xr_v.astype(bf16), dk), -tn(xi_v.astype(bf16), dk))
            for ref, val in zip((dbbr_ref, dbbi_ref, dcr_ref, dci_ref), vals):
                @pl.when(first)
                def _():
                    ref[k] = val

                @pl.when(jnp.logical_not(first))
                def _():
                    ref[k] += val
            du_k = (lax.dot_general(grb, bbr_ref[k], _DN["nt"], preferred_element_type=f32)
                    + lax.dot_general(gib, bbi_ref[k], _DN["nt"], preferred_element_type=f32))
            du_ref[:, cb] = du_k + d_ref[:, cb] * dyy_v[:, cb]

    rev = lambda c: _tok_spec(tm, c, nt, True)
    prev = pl.BlockSpec((8, ns), lambda i: (jnp.maximum((nt - 1 - i) * per - 1, 0), 0))
    bd_in = pl.BlockSpec((S5_BLOCKS, S5_BCH, S5_BST), lambda i: (0, 0, 0))
    bd_out = pl.BlockSpec((S5_BLOCKS, S5_BST, S5_BCH), lambda i: (0, 0, 0))
    row_ns = jax.ShapeDtypeStruct((1, ns), f32)
    bd_in_s = jax.ShapeDtypeStruct((S5_BLOCKS, S5_BCH, S5_BST), f32)
    bd_out_s = jax.ShapeDtypeStruct((S5_BLOCKS, S5_BST, S5_BCH), f32)
    return pl.pallas_call(
        body, name="s5_scan_bwd", grid=(nt,),
        in_specs=[rev(D), rev(D), _row_spec(D), _row_spec(D), _row_spec(D), rev(ns), rev(ns), prev, prev,
                  _row_spec(ns), _row_spec(ns), bd_in, bd_in, bd_out, bd_out, _row_spec(D)],
        out_specs=[rev(D), _row_spec(ns), _row_spec(ns), bd_in, bd_in, bd_out, bd_out, _row_spec(D)],
        out_shape=[jax.ShapeDtypeStruct((t, D), f32), row_ns, row_ns, bd_in_s, bd_in_s, bd_out_s, bd_out_s,
                   jax.ShapeDtypeStruct((1, D), f32)],
        scratch_shapes=[pltpu.VMEM((tm, ns), f32), pltpu.VMEM((tm, ns), f32),
                        pltpu.VMEM((8, ns), f32), pltpu.VMEM((8, ns), f32)],
        compiler_params=_cp(1),
    )(dyy, x, g, scale, shift, xr, xi, xr, xi, lb_re, lb_im, bbr_bd, bbi_bd, cr_bd, ci_bd, dskip)


def _s5_gelu(yy):
    t = yy.shape[0]
    tm = _tok_tile(t)

    def body(y_ref, o_ref):
        o_ref[...] = jax.nn.gelu(y_ref[...]).astype(bf16)

    return pl.pallas_call(
        body, name="s5_gelu", grid=(t // tm,), in_specs=[_tok_spec(tm, D)], out_specs=_tok_spec(tm, D),
        out_shape=jax.ShapeDtypeStruct((t, D), bf16), compiler_params=_cp(1),
    )(yy)


def _s5_glu(gl, z):
    return gl * jax.nn.sigmoid(z)


def _s5_out(x, yy, z, gate):
    t = x.shape[0]
    tm = _tok_tile(t)

    def body(x_ref, y_ref, z_ref, gt_ref, o_ref):
        o_ref[...] = x_ref[...] + (1.0 + gt_ref[...]) * _s5_glu(jax.nn.gelu(y_ref[...]), z_ref[...])

    return pl.pallas_call(
        body, name="s5_out", grid=(t // tm,),
        in_specs=[_tok_spec(tm, D), _tok_spec(tm, D), _tok_spec(tm, D), _row_spec(D)],
        out_specs=_tok_spec(tm, D), out_shape=jax.ShapeDtypeStruct((t, D), f32), compiler_params=_cp(1),
    )(x, yy, z, gate)


def _s5_out_bwd(dxn, yy, z, gate):
    t = dxn.shape[0]
    tm = _tok_tile(t)

    def body(dxn_ref, y_ref, z_ref, gt_ref, dz_ref, dgl_ref, dgt_ref):
        dxn_v = dxn_ref[...]
        gl = jax.nn.gelu(y_ref[...])
        out, vjp = jax.vjp(_s5_glu, gl, z_ref[...])
        dgl, dz = vjp((1.0 + gt_ref[...]) * dxn_v)
        dz_ref[...] = dz.astype(bf16)
        dgl_ref[...] = dgl
        _acc_add(pl.program_id(0) == 0, dgt_ref, jnp.sum(dxn_v * out, axis=0, keepdims=True))

    return pl.pallas_call(
        body, name="s5_out_bwd", grid=(t // tm,),
        in_specs=[_tok_spec(tm, D), _tok_spec(tm, D), _tok_spec(tm, D), _row_spec(D)],
        out_specs=[_tok_spec(tm, D), _tok_spec(tm, D), _row_spec(D)],
        out_shape=[jax.ShapeDtypeStruct((t, D), bf16), jax.ShapeDtypeStruct((t, D), f32),
                   jax.ShapeDtypeStruct((1, D), f32)],
        compiler_params=_cp(1),
    )(dxn, yy, z, gate)


def _s5_gelu_bwd(yy, dgl_a, dgl_b):
    t = yy.shape[0]
    tm = _tok_tile(t)

    def body(y_ref, a_ref, b_ref, o_ref):
        _, vjp = jax.vjp(jax.nn.gelu, y_ref[...])
        o_ref[...] = vjp(a_ref[...] + b_ref[...])[0]

    return pl.pallas_call(
        body, name="s5_gelu_bwd", grid=(t // tm,),
        in_specs=[_tok_spec(tm, D), _tok_spec(tm, D), _tok_spec(tm, D)],
        out_specs=_tok_spec(tm, D), out_shape=jax.ShapeDtypeStruct((t, D), f32), compiler_params=_cp(1),
    )(yy, dgl_a, dgl_b)


def _s5_params(lam_re, lam_im, log_dt, b_re, b_im):
    bc = lambda a: a.reshape(S5_GROUPS, 1, -1)
    return (bc(lam_re), bc(lam_im), jnp.broadcast_to(log_dt.reshape(S5_GROUPS, 1, 1), (S5_GROUPS, 1, S5_STATE)),
            b_re.transpose(0, 2, 1), b_im.transpose(0, 2, 1))


def _s5_fwd(x, ln, raw, c_re, c_im, dskip, w_glu):
    gate = ln[3]
    lb_re, lb_im, bb_re, bb_im = _s5_prep(*raw)
    lbr, lbi = lb_re.reshape(1, S5_NSTATE), lb_im.reshape(1, S5_NSTATE)
    bds = (_s5_in_bd(bb_re).astype(bf16), _s5_in_bd(bb_im).astype(bf16),
           _s5_out_bd(c_re).astype(bf16), _s5_out_bd(c_im).astype(bf16))
    yy, xr, xi = _s5_scan_fwd(x, ln, lbr, lbi, *bds, dskip)
    gl = _s5_gelu(yy)
    z = _mm(gl, w_glu, "nn", f32, "s5_glu_mm")
    return _s5_out(x, yy, z, gate), (x, lbr, lbi, bds, yy, xr, xi, gl, z)


def _s5_bwd(dxn, saved, ln, raw, dskip, w_glu):
    x, lbr, lbi, bds, yy, xr, xi, gl, z = saved
    g, shift, scale, gate = ln
    dz, dgl_a, dgate = _s5_out_bwd(dxn, yy, z, gate)
    dgl_b = _mm(dz, w_glu, "nt", f32, "s5_dgl")
    dw_glu = _mm(gl, dz, "tn", f32, "s5_dwglu")
    dyy = _s5_gelu_bwd(yy, dgl_a, dgl_b)
    du, dar, dai, dbbr_bd, dbbi_bd, dcr_bd, dci_bd, dd = _s5_scan_bwd(dyy, x, ln, xr, xi, lbr, lbi, *bds, dskip)
    shp = (S5_GROUPS, 1, S5_STATE)
    d_lam_re, d_lam_im, d_dt, d_b_re, d_b_im = _s5_prep_bwd(
        *raw, dar.reshape(shp), dai.reshape(shp), _s5_in_bd_diag(dbbr_bd), _s5_in_bd_diag(dbbi_bd))
    dx, dg, dscale, dshift = _ln_bwd(x, g, scale, shift, du, dxn)
    grads = dict(
        s5_lam_re=d_lam_re.reshape(1, S5_GROUPS, S5_STATE), s5_lam_im=d_lam_im.reshape(1, S5_GROUPS, S5_STATE),
        s5_log_dt=d_dt[:, 0, 0].reshape(1, S5_GROUPS),
        s5_b_re=d_b_re.transpose(0, 2, 1)[None], s5_b_im=d_b_im.transpose(0, 2, 1)[None],
        s5_c_re=_s5_out_bd_diag(dcr_bd)[None], s5_c_im=_s5_out_bd_diag(dci_bd)[None],
        s5_d=dd, s5_w_glu=dw_glu)
    return dx, (dg, dshift, dscale, dgate), grads


_MESH = pl.DeviceIdType.MESH
_ANY = pl.BlockSpec(memory_space=pl.ANY)


def _me():
    return lax.axis_index("x"), lax.axis_index("y"), lax.axis_index("c")


def _dev_index(x, y, c):
    return 4 * x + 2 * y + c


def _all_gather(vs, name):
    n = len(vs)

    def body(*refs):
        v_refs, out_refs = refs[:n], refs[n:2 * n]
        send_sems, recv_sems, local_sems = refs[2 * n:]
        x, y, cc = _me()
        me, sibling = (x, y, cc), (x, y, 1 - cc)
        chips = [(1 - x, y), (x, 1 - y), (1 - x, 1 - y)]
        sends, local = [], []

        def copy(a, k, block, to, src=None):
            rows = out_refs[a].at[_dev_index(*block)]
            return pltpu.make_async_remote_copy(
                src_ref=rows if src is None else src, dst_ref=rows,
                send_sem=send_sems.at[7 * a + k], recv_sem=recv_sems.at[7 * a + k], device_id=to, device_id_type=_MESH)

        for a in range(n):
            mine = pltpu.make_async_copy(v_refs[a], out_refs[a].at[_dev_index(*me)], local_sems.at[a])
            mine.start()
            local.append(mine)
            first = [copy(a, 0, me, sibling, src=v_refs[a])]
            first += [copy(a, 1 + j, me, (*chip, cc), src=v_refs[a]) for j, chip in enumerate(chips)]
            for cp in first:
                cp.start()
            sends += first
        for a in range(n):
            for j, chip in enumerate(chips):
                copy(a, 1 + j, (*chip, cc), me).wait_recv()
                passed = copy(a, 4 + j, (*chip, cc), sibling)
                passed.start()
                sends.append(passed)
        for a in range(n):
            copy(a, 0, sibling, me).wait_recv()
            for j, chip in enumerate(chips):
                copy(a, 4 + j, (*chip, 1 - cc), me).wait_recv()
        for cp in sends:
            cp.wait_send()
        for cp in local:
            cp.wait()

    return pl.pallas_call(
        body, name=name, out_shape=[jax.ShapeDtypeStruct((N_DEV,) + v.shape, v.dtype) for v in vs],
        in_specs=[_ANY] * n, out_specs=[_ANY] * n,
        scratch_shapes=[pltpu.SemaphoreType.DMA((7 * n,)), pltpu.SemaphoreType.DMA((7 * n,)),
                        pltpu.SemaphoreType.DMA((n,))],
    )(*vs)


def _exchange_pair(vs, name):
    n = len(vs)

    def body(*refs):
        v_refs, out_refs = refs[:n], refs[n:2 * n]
        send_sems, recv_sems = refs[2 * n:]
        x, y, cc = _me()
        sibling = (x, y, 1 - cc)
        copies = []
        for a in range(n):
            for k in range(4):
                cp = pltpu.make_async_remote_copy(
                    src_ref=v_refs[a].at[2 * k + (1 - cc)], dst_ref=out_refs[a].at[k],
                    send_sem=send_sems.at[4 * a + k], recv_sem=recv_sems.at[4 * a + k],
                    device_id=sibling, device_id_type=_MESH)
                cp.start()
                copies.append(cp)
        for cp in copies:
            cp.wait_recv()
        for cp in copies:
            cp.wait_send()

    return pl.pallas_call(
        body, name=name, out_shape=[jax.ShapeDtypeStruct((4,) + v.shape[1:], v.dtype) for v in vs],
        in_specs=[_ANY] * n, out_specs=[_ANY] * n,
        scratch_shapes=[pltpu.SemaphoreType.DMA((4 * n,)), pltpu.SemaphoreType.DMA((4 * n,))],
    )(*vs)


def _pair_sum(v, got):
    _, r, c = v.shape
    tr = _pick(r, (512, 352, 256, 128))
    core = lax.axis_index("c").astype(jnp.int32).reshape(1)

    def body(c_ref, v_ref, g_ref, o_ref):
        o_ref[...] = (v_ref[...] + g_ref[...]).astype(bf16)

    return pl.pallas_call(
        body, name="pair_sum",
        grid_spec=pltpu.PrefetchScalarGridSpec(
            num_scalar_prefetch=1, grid=(4, r // tr),
            in_specs=[pl.BlockSpec((1, tr, c), lambda k, i, c_ref: (2 * k + c_ref[0], i, 0)),
                      pl.BlockSpec((1, tr, c), lambda k, i, c_ref: (k, i, 0))],
            out_specs=pl.BlockSpec((1, tr, c), lambda k, i, c_ref: (k, i, 0))),
        out_shape=jax.ShapeDtypeStruct((4, r, c), bf16), compiler_params=_cp(2),
    )(core, v, got)


def _exchange_chips(vs, name):
    n = len(vs)

    def body(*refs):
        v_refs, out_refs = refs[:n], refs[n:2 * n]
        send_sems, recv_sems, local_sems = refs[2 * n:]
        x, y, cc = _me()
        mine = 2 * x + y
        peers = []
        for mask in (1, 2, 3):
            px = 1 - x if mask & 2 else x
            py = 1 - y if mask & 1 else y
            peers.append((mask - 1, (px, py, cc), 2 * px + py))
        local, sends = [], []
        for a in range(n):
            own = pltpu.make_async_copy(v_refs[a].at[mine], out_refs[a].at[mine], local_sems.at[a])
            own.start()
            local.append(own)
            for k, peer, pchip in peers:
                cp = pltpu.make_async_remote_copy(
                    src_ref=v_refs[a].at[pchip], dst_ref=out_refs[a].at[mine],
                    send_sem=send_sems.at[3 * a + k], recv_sem=recv_sems.at[3 * a + k],
                    device_id=peer, device_id_type=_MESH)
                cp.start()
                sends.append(cp)
        for a in range(n):
            for k, peer, pchip in peers:
                pltpu.make_async_remote_copy(
                    src_ref=v_refs[a].at[pchip], dst_ref=out_refs[a].at[pchip],
                    send_sem=send_sems.at[3 * a + k], recv_sem=recv_sems.at[3 * a + k],
                    device_id=peer, device_id_type=_MESH).wait_recv()
        for cp in sends:
            cp.wait_send()
        for cp in local:
            cp.wait()

    return pl.pallas_call(
        body, name=name, out_shape=[jax.ShapeDtypeStruct(v.shape, v.dtype) for v in vs],
        in_specs=[_ANY] * n, out_specs=[_ANY] * n,
        scratch_shapes=[pltpu.SemaphoreType.DMA((3 * n,)), pltpu.SemaphoreType.DMA((3 * n,)),
                        pltpu.SemaphoreType.DMA((n,))],
    )(*vs)


def _ada_mod(c_all, ada_w):
    cols = ada_w.shape[2]

    def body(c_ref, w_ref, o_ref):
        cond = jax.nn.silu(c_ref[...]).astype(bf16)
        o_ref[0] = jnp.dot(cond, w_ref[0].astype(bf16), preferred_element_type=f32)

    return pl.pallas_call(
        body, name="ada_mod", grid=(DEPTH,),
        in_specs=[pl.BlockSpec((16, D), lambda i: (0, 0)), pl.BlockSpec((1, D, cols), lambda i: (i, 0, 0))],
        out_specs=pl.BlockSpec((1, 16, cols), lambda i: (i, 0, 0)),
        out_shape=jax.ShapeDtypeStruct((DEPTH, 16, cols), f32), compiler_params=_cp(1),
    )(c_all, ada_w)


def _ada_grad(c_all, dmod):
    cols = dmod.shape[2]

    def body(c_ref, d_ref, o_ref):
        cond = jax.nn.silu(c_ref[...]).astype(bf16)
        o_ref[0] = lax.dot_general(cond, d_ref[0].astype(bf16), _DN["tn"], preferred_element_type=f32)

    return pl.pallas_call(
        body, name="ada_grad", grid=(DEPTH,),
        in_specs=[pl.BlockSpec((16, D), lambda i: (0, 0)), pl.BlockSpec((1, 16, cols), lambda i: (i, 0, 0))],
        out_specs=pl.BlockSpec((1, D, cols), lambda i: (i, 0, 0)),
        out_shape=jax.ShapeDtypeStruct((DEPTH, D, cols), f32), compiler_params=_cp(1),
    )(c_all, dmod)


def _row_tile(r):
    return _pick(r, (512, 352, 256, 128)) if r > 512 else r


def _sum_sources(v, name):
    n, r, c = v.shape
    tr = _row_tile(r)

    def body(v_ref, o_ref):
        acc = v_ref[0]
        for p in range(1, n):
            acc = acc + v_ref[p]
        o_ref[...] = acc.astype(f32)

    return pl.pallas_call(
        body, name=name, grid=(r // tr,),
        in_specs=[pl.BlockSpec((n, tr, c), lambda i: (0, i, 0))], out_specs=pl.BlockSpec((tr, c), lambda i: (i, 0)),
        out_shape=jax.ShapeDtypeStruct((r, c), f32), compiler_params=_cp(1),
    )(v)


def _adamw(parts, w, m, v, name):
    n, r, c = parts.shape
    tr = _row_tile(r)
    c1 = 1.0 - ADAM_B1 ** ADAM_STEP
    c2 = 1.0 - ADAM_B2 ** ADAM_STEP

    def body(p_ref, w_ref, m_ref, v_ref, g_ref, d_ref, mo_ref, vo_ref):
        g_v = p_ref[0].astype(f32)
        for p in range(1, n):
            g_v = g_v + p_ref[p].astype(f32)
        m_n = ADAM_B1 * m_ref[...] + (1.0 - ADAM_B1) * g_v
        v_n = ADAM_B2 * v_ref[...] + (1.0 - ADAM_B2) * (g_v * g_v)
        g_ref[...] = g_v
        d_ref[...] = -ADAM_LR * ((m_n / c1) / (jnp.sqrt(v_n / c2) + ADAM_EPS) + ADAM_WD * w_ref[...])
        mo_ref[...] = m_n
        vo_ref[...] = v_n

    spec = pl.BlockSpec((tr, c), lambda i: (i, 0))
    shp = jax.ShapeDtypeStruct((r, c), f32)
    return pl.pallas_call(
        body, name=name, grid=(r // tr,), in_specs=[pl.BlockSpec((n, tr, c), lambda i: (0, i, 0))] + [spec] * 3,
        out_specs=[spec] * 4, out_shape=[shp] * 4, compiler_params=_cp(1),
    )(parts, w, m, v)


def _two_d(shape):
    return (math.prod(shape[:-1]), shape[-1])


def _pack_rows(a):
    n = a.size
    rows = -(-n // (8 * PACK_C)) * 8
    return jnp.pad(a.reshape(-1), (0, rows * PACK_C - n)).reshape(rows, PACK_C)


def _pack(parts):
    return jnp.concatenate([_pack_rows(p) for p in parts], axis=0)


def _unpack(packed, shapes):
    lead = packed.shape[:-2]
    out, off = [], 0
    for s in shapes:
        n = math.prod(s)
        rows = -(-n // (8 * PACK_C)) * 8
        part = packed[..., off:off + rows, :].reshape(lead + (rows * PACK_C,))
        out.append(part[..., :n].reshape(lead + tuple(s)))
        off += rows
    return out


def _unshard(g8, axis):
    local = g8.shape[1:]
    moved = jnp.moveaxis(g8, 0, axis)
    return moved.reshape(local[:axis] + (N_DEV * local[axis],) + local[axis + 1:])


_BIG = dict(ffn_w_in=3, ffn_w_out=2, pool_w=2, fox_w_in=2, fox_w_o=1, s5_w_glu=1, conv_w_in=2, conv_w_out=1)
_SMALL_SHARDED = dict(norm_g=2, s5_d=1, conv_w=3)
_REPLICATED = ("ada_b", "pool_scale", "fox_b_f", "fox_q_gain", "fox_k_gain", "s5_lam_re", "s5_lam_im", "s5_log_dt",
               "s5_b_re", "s5_b_im", "s5_c_re", "s5_c_im")
_WEIGHTS = ("ada_w", "ada_b", "norm_g", "ffn_w_in", "ffn_w_out", "pool_w", "pool_scale", "fox_w_in", "fox_b_f",
            "fox_q_gain", "fox_k_gain", "fox_w_o", "s5_lam_re", "s5_lam_im", "s5_log_dt", "s5_b_re", "s5_b_im",
            "s5_c_re", "s5_c_im", "s5_d", "s5_w_glu", "conv_w_in", "conv_w", "conv_w_out")


def _step(x, c, target, w, m, v):
    t = x.shape[1]
    xi_, yi_, ci_ = _me()
    me = _dev_index(xi_, yi_, ci_)

    sm_shapes = [w[n].shape for n in _SMALL_SHARDED]
    small_all = _all_gather([_pack([c] + [w[n] for n in _SMALL_SHARDED])], "gather_small")[0]
    gathered = _unpack(small_all, [c.shape] + sm_shapes)
    c_all = gathered[0][:, 0, :]
    full = {n: _unshard(p, ax) for (n, ax), p in zip(_SMALL_SHARDED.items(), gathered[1:])}

    big_all = _all_gather([w[n].astype(bf16).reshape(_two_d(w[n].shape)) for n in _BIG], "gather_weights")
    gw = dict(zip(_BIG, big_all))
    ffn_w_in = gw["ffn_w_in"].reshape(N_DEV, 2 * DEPTH, D, FFN_HS)
    ffn_w_out = gw["ffn_w_out"].reshape(N_DEV, 2 * DEPTH, D_FF // N_DEV, D)
    pool_w = gw["pool_w"].reshape(N_DEV, 4, POOL_GROUP // N_DEV, POOL_GROUP).transpose(1, 0, 2, 3)
    pool_w = pool_w.reshape(4, POOL_GROUP, POOL_GROUP)
    fox_w_in = jnp.pad(gw["fox_w_in"].transpose(1, 0, 2).reshape(D, FOX_PROJ), ((0, 0), (0, FOX_PROJ_PAD - FOX_PROJ)))
    fox_w_o, s5_w_glu, conv_w_out = (gw[n].reshape(D, D) for n in ("fox_w_o", "s5_w_glu", "conv_w_out"))
    conv_w_in = gw["conv_w_in"]

    def ffn_w(i, f):
        return ffn_w_in, ffn_w_out, 2 * i + f

    c16 = jnp.pad(c_all, ((0, 8), (0, 0)))
    cols = w["ada_w"].shape[2]
    mod_sh = _ada_mod(c16, w["ada_w"])
    mod_all = _all_gather([mod_sh.reshape(DEPTH * 16, cols)], "gather_mod")[0].reshape(N_DEV, DEPTH, 16, cols)
    mod_mine = lax.dynamic_index_in_dim(mod_all, me, axis=2, keepdims=False)
    mod = (mod_mine.transpose(1, 0, 2).reshape(DEPTH, N_DEV * cols) + w["ada_b"]).reshape(DEPTH, 3, 3, D)

    norm_g = full["norm_g"]

    def ln_of(i, sub):
        return (norm_g[i, sub][None], mod[i, sub, 0][None], mod[i, sub, 1][None], mod[i, sub, 2][None])

    fox_b_f = jnp.pad(w["fox_b_f"], ((0, 0), (0, 128 - FOX_HEADS)))
    s5_raw = _s5_params(w["s5_lam_re"][0], w["s5_lam_im"][0], w["s5_log_dt"][0], w["s5_b_re"][0], w["s5_b_im"][0])
    s5_c_re, s5_c_im = w["s5_c_re"][0], w["s5_c_im"][0]
    conv_w = jnp.pad(full["conv_w"][0, :, 0, :], ((0, 5), (0, 0)))

    xs = x[0]
    saved = []
    for i in range(DEPTH):
        xs, s0 = _ffn_fwd(xs, ln_of(i, 0), *ffn_w(i, 0), 0.5)
        if i == 0:
            s1 = xs
            xs = _pool_fwd(xs, ln_of(i, 1), pool_w, w["pool_scale"])
        elif i == 1:
            xs, s1 = _fox_fwd(xs, ln_of(i, 1), fox_w_in, fox_b_f, w["fox_q_gain"], w["fox_k_gain"], fox_w_o)
        elif i == 2:
            xs, s1 = _s5_fwd(xs, ln_of(i, 1), s5_raw, s5_c_re, s5_c_im, full["s5_d"], s5_w_glu)
        else:
            xs, s1 = _convmix_fwd(xs, ln_of(i, 1), conv_w_in, conv_w, conv_w_out)
        xs, s2 = _ffn_fwd(xs, ln_of(i, 2), *ffn_w(i, 1), 0.5)
        saved.append((s0, s1, s2))
    dx, lpart = _loss_head(xs, target[0])

    grads = {}
    dmod = [[None] * 3 for _ in range(DEPTH)]
    dnorm = [[None] * 3 for _ in range(DEPTH)]
    dffn_in = lax.empty(ffn_w_in.shape, f32)
    dffn_out = lax.empty(ffn_w_out.shape, f32)

    def put_ln(i, sub, dln):
        dg, dshift, dscale, dgate = dln
        dnorm[i][sub] = dg
        dmod[i][sub] = jnp.concatenate([dshift, dscale, dgate], axis=0)

    for i in reversed(range(DEPTH)):
        s0, s1, s2 = saved[i]
        dx, dln, dffn_in, dffn_out = _ffn_bwd(dx, s2, ln_of(i, 2), *ffn_w(i, 1), 0.5, dffn_in, dffn_out)
        put_ln(i, 2, dln)
        if i == 0:
            dx, dln, dpw, dps = _pool_bwd(dx, s1, ln_of(i, 1), pool_w, w["pool_scale"])
            dpw = dpw.reshape(4, N_DEV, POOL_GROUP // N_DEV, POOL_GROUP).transpose(1, 0, 2, 3)
            grads.update(pool_w=dpw.reshape(N_DEV, 4 * POOL_GROUP // N_DEV, POOL_GROUP), pool_scale=dps)
        elif i == 1:
            dx, dln, dwi, dbf, dqg, dkg, dwo = _fox_bwd(dx, s1, ln_of(i, 1), fox_w_in, fox_b_f, fox_w_o)
            dwi = dwi[:, :FOX_PROJ].reshape(D, N_DEV, FOX_PROJ // N_DEV).transpose(1, 0, 2)
            grads.update(fox_w_in=dwi, fox_b_f=dbf[:, :FOX_HEADS], fox_q_gain=dqg, fox_k_gain=dkg,
                         fox_w_o=dwo.reshape(N_DEV, D // N_DEV, D))
        elif i == 2:
            dx, dln, gs5 = _s5_bwd(dx, s1, ln_of(i, 1), s5_raw, full["s5_d"], s5_w_glu)
            gs5["s5_w_glu"] = gs5["s5_w_glu"].reshape(N_DEV, D // N_DEV, D)
            grads.update(gs5)
        else:
            dx, dln, dwi, dcw, dwo = _convmix_bwd(dx, s1, ln_of(i, 1), conv_w_in, conv_w, conv_w_out)
            grads.update(conv_w_in=dwi, conv_w=dcw[None, :, None, :], conv_w_out=dwo.reshape(N_DEV, D // N_DEV, D))
        put_ln(i, 1, dln)
        dx, dln, dffn_in, dffn_out = _ffn_bwd(dx, s0, ln_of(i, 0), *ffn_w(i, 0), 0.5, dffn_in, dffn_out)
        put_ln(i, 0, dln)
    grads["ffn_w_in"] = dffn_in.reshape(N_DEV, 2 * DEPTH * D, FFN_HS)
    grads["ffn_w_out"] = dffn_out.reshape(N_DEV, 2 * DEPTH * D_FF // N_DEV, D)
    grads["norm_g"] = jnp.stack([jnp.concatenate(r, axis=0) for r in dnorm])
    dmod_mine = jnp.stack([jnp.stack(r) for r in dmod]).reshape(DEPTH, 9 * D)

    small_names = list(_REPLICATED[1:]) + list(_SMALL_SHARDED)
    small_parts = [dmod_mine] + [grads[n] for n in small_names] + [lpart[:, 0:1]]
    small_g = _all_gather([_pack(small_parts)], "gather_grads")[0]
    small_sum = _sum_sources(small_g, "sum_small")
    summed = dict(zip(["ada_b"] + small_names + ["loss"], _unpack(small_sum, [p.shape for p in small_parts])))
    loss = summed.pop("loss")[0, 0]
    for n, ax in _SMALL_SHARDED.items():
        local = w[n].shape[ax]
        summed[n] = lax.dynamic_slice_in_dim(summed[n], me * local, local, axis=ax)

    dmod_all = small_g[:, :DEPTH * 9].reshape(N_DEV, DEPTH, 9 * D)
    dmod_cols = lax.dynamic_slice_in_dim(dmod_all, me * cols, cols, axis=2)
    ada_g = _ada_grad(c16, jnp.pad(dmod_cols.transpose(1, 0, 2), ((0, 0), (0, 8), (0, 0))))

    big_parts = [grads[n] for n in _BIG]
    from_sibling = _exchange_pair(big_parts, "exchange_pair")
    chip_sums = [_pair_sum(p, s) for p, s in zip(big_parts, from_sibling)]
    big_landed = dict(zip(_BIG, _exchange_chips(chip_sums, "exchange_chips")))

    grad, delta, new_m, new_v = {}, {}, {}, {}
    big_landed["ada_w"] = ada_g[None]
    for n, parts in big_landed.items():
        view = _two_d(w[n].shape)
        outs = _adamw(parts.reshape((parts.shape[0],) + view), w[n].reshape(view), m[n].reshape(view),
                      v[n].reshape(view), "adamw_" + n)
        grad[n], delta[n], new_m[n], new_v[n] = (a.reshape(w[n].shape) for a in outs)
    small = [n for n in _WEIGHTS if n not in big_landed]
    small_shapes = [w[n].shape for n in small]
    pk = lambda d: _pack([d[n] for n in small])
    outs = _adamw(pk(summed)[None], pk(w), pk(m), pk(v), "adamw_small")
    for dst, packed in zip((grad, delta, new_m, new_v), outs):
        dst.update(zip(small, _unpack(packed, small_shapes)))
    return (loss, dx[None], *[grad[n] for n in _WEIGHTS], *[delta[n] for n in _WEIGHTS],
            *[new_m[n] for n in _WEIGHTS], *[new_v[n] for n in _WEIGHTS])


def kernel(x, c, ada_w, ada_b, norm_g, ffn_w_in, ffn_w_out, pool_w, pool_scale, fox_w_in, fox_b_f, fox_q_gain, fox_k_gain, fox_w_o, s5_lam_re, s5_lam_im, s5_log_dt, s5_b_re, s5_b_im, s5_c_re, s5_c_im, s5_d, s5_w_glu, conv_w_in, conv_w, conv_w_out, loss_target, m_ada_w, m_ada_b, m_norm_g, m_ffn_w_in, m_ffn_w_out, m_pool_w, m_pool_scale, m_fox_w_in, m_fox_b_f, m_fox_q_gain, m_fox_k_gain, m_fox_w_o, m_s5_lam_re, m_s5_lam_im, m_s5_log_dt, m_s5_b_re, m_s5_b_im, m_s5_c_re, m_s5_c_im, m_s5_d, m_s5_w_glu, m_conv_w_in, m_conv_w, m_conv_w_out, v_ada_w, v_ada_b, v_norm_g, v_ffn_w_in, v_ffn_w_out, v_pool_w, v_pool_scale, v_fox_w_in, v_fox_b_f, v_fox_q_gain, v_fox_k_gain, v_fox_w_o, v_s5_lam_re, v_s5_lam_im, v_s5_log_dt, v_s5_b_re, v_s5_b_im, v_s5_c_re, v_s5_c_im, v_s5_d, v_s5_w_glu, v_conv_w_in, v_conv_w, v_conv_w_out):
    ws = (ada_w, ada_b, norm_g, ffn_w_in, ffn_w_out, pool_w, pool_scale, fox_w_in, fox_b_f, fox_q_gain, fox_k_gain,
          fox_w_o, s5_lam_re, s5_lam_im, s5_log_dt, s5_b_re, s5_b_im, s5_c_re, s5_c_im, s5_d, s5_w_glu, conv_w_in,
          conv_w, conv_w_out)
    ms = (m_ada_w, m_ada_b, m_norm_g, m_ffn_w_in, m_ffn_w_out, m_pool_w, m_pool_scale, m_fox_w_in, m_fox_b_f,
          m_fox_q_gain, m_fox_k_gain, m_fox_w_o, m_s5_lam_re, m_s5_lam_im, m_s5_log_dt, m_s5_b_re, m_s5_b_im,
          m_s5_c_re, m_s5_c_im, m_s5_d, m_s5_w_glu, m_conv_w_in, m_conv_w, m_conv_w_out)
    vs = (v_ada_w, v_ada_b, v_norm_g, v_ffn_w_in, v_ffn_w_out, v_pool_w, v_pool_scale, v_fox_w_in, v_fox_b_f,
          v_fox_q_gain, v_fox_k_gain, v_fox_w_o, v_s5_lam_re, v_s5_lam_im, v_s5_log_dt, v_s5_b_re, v_s5_b_im,
          v_s5_c_re, v_s5_c_im, v_s5_d, v_s5_w_glu, v_conv_w_in, v_conv_w, v_conv_w_out)
    return _step(x, c, loss_target, dict(zip(_WEIGHTS, ws)), dict(zip(_WEIGHTS, ms)), dict(zip(_WEIGHTS, vs)))
```

```python
import math

import jax
import jax.numpy as jnp
from jax import lax
from jax.experimental import pallas as pl
from jax.experimental.pallas import tpu as pltpu

f32 = jnp.float32
bf16 = jnp.bfloat16

D = 1024
D_FF = 2816
FFN_HS = 2 * D_FF // 8
FFN_SLABS = 4
DEPTH = 4
NORM_EPS = 1e-6
N_DEV = 8
POOL_WINDOWS = (2, 4, 8, 16)
POOL_GROUP = 256
POOL_HALO = 16
FOX_HEADS = 16
FOX_HEAD_DIM = 64
FOX_PROJ = 3088
FOX_PROJ_PAD = 3200
S5_GROUPS = 64
S5_GROUP = 16
S5_STATE = 64
S5_NSTATE = S5_GROUPS * S5_STATE
S5_BLOCKS = 8
S5_BCH = 128
S5_BST = 512
CONV_HALO = 8
ADAM_LR = 0.001
ADAM_B1 = 0.9
ADAM_B2 = 0.999
ADAM_EPS = 1e-08
ADAM_WD = 0.01
ADAM_STEP = 10
VMEM_LIMIT = 56 * 1024 * 1024
PACK_C = 1024

_ARB = "arbitrary"


def _cp(n_axes):
    return pltpu.CompilerParams(dimension_semantics=(_ARB,) * n_axes, vmem_limit_bytes=VMEM_LIMIT)


def _pick(n, prefs):
    for c in prefs:
        if n % c == 0:
            return c
    return n


_DN = {"nn": (((1,), (0,)), ((), ())), "nt": (((1,), (1,)), ((), ())), "tn": (((0,), (0,)), ((), ()))}


def _mm(a, b, mode, out_dtype, name):
    if mode == "nn":
        (m, k), (_, n) = a.shape, b.shape
    elif mode == "nt":
        (m, k), (n, _) = a.shape, b.shape
    else:
        (k, m), (_, n) = a.shape, b.shape
    big = (1408, 1024, 640, 512, 384, 256, 128)
    tm = _pick(m, big) if mode == "tn" else _pick(m, (1024, 512, 256, 128))
    tn = _pick(n, big)
    if mode == "tn":
        tk = _pick(k, (4096, 2048, 1024, 512, 256, 128))
    else:
        tk = k if k <= 3200 else _pick(k, (2816, 2048, 1024, 512))
    nk = k // tk
    dn = _DN[mode]

    def body(a_ref, b_ref, o_ref, acc_ref):
        p = lax.dot_general(a_ref[...], b_ref[...], dn, preferred_element_type=f32)
        if nk == 1:
            o_ref[...] = p.astype(out_dtype)
        else:
            kk = pl.program_id(2)

            @pl.when(kk == 0)
            def _():
                acc_ref[...] = p

            @pl.when(kk > 0)
            def _():
                acc_ref[...] += p

            @pl.when(kk == nk - 1)
            def _():
                o_ref[...] = acc_ref[...].astype(out_dtype)

    if mode == "nn":
        a_spec = pl.BlockSpec((tm, tk), lambda i, j, kk: (i, kk))
        b_spec = pl.BlockSpec((tk, tn), lambda i, j, kk: (kk, j))
    elif mode == "nt":
        a_spec = pl.BlockSpec((tm, tk), lambda i, j, kk: (i, kk))
        b_spec = pl.BlockSpec((tn, tk), lambda i, j, kk: (j, kk))
    else:
        a_spec = pl.BlockSpec((tk, tm), lambda i, j, kk: (kk, i))
        b_spec = pl.BlockSpec((tk, tn), lambda i, j, kk: (kk, j))
    acc_shape = (tm, tn) if nk > 1 else (8, 128)
    return pl.pallas_call(
        body, name=name, grid=(m // tm, n // tn, nk),
        in_specs=[a_spec, b_spec], out_specs=pl.BlockSpec((tm, tn), lambda i, j, kk: (i, j)),
        out_shape=jax.ShapeDtypeStruct((m, n), out_dtype),
        scratch_shapes=[pltpu.VMEM(acc_shape, f32)],
        compiler_params=_cp(3),
    )(a, b)


def _mmx(a, b, mode, name, grid, a_spec, b_spec, o_spec, out_shape, into=None):
    nk = grid[2]
    dn = _DN[mode]
    out_dtype = out_shape.dtype
    a_blk = (math.prod(a_spec.block_shape[:-1]), a_spec.block_shape[-1])
    b_blk = (math.prod(b_spec.block_shape[:-1]), b_spec.block_shape[-1])
    o_blk = (math.prod(o_spec.block_shape[:-1]), o_spec.block_shape[-1])

    def body(a_ref, b_ref, *rest):
        o_ref, acc_ref = rest[-2:]
        p = lax.dot_general(a_ref[...].reshape(a_blk), b_ref[...].reshape(b_blk), dn, preferred_element_type=f32)
        if nk == 1:
            o_ref[...] = p.reshape(o_ref.shape).astype(out_dtype)
        else:
            kk = pl.program_id(2)

            @pl.when(kk == 0)
            def _():
                acc_ref[...] = p

            @pl.when(kk > 0)
            def _():
                acc_ref[...] += p

            @pl.when(kk == nk - 1)
            def _():
                o_ref[...] = acc_ref[...].reshape(o_ref.shape).astype(out_dtype)

    extra = {} if into is None else dict(input_output_aliases={2: 0})
    operands = (a, b) if into is None else (a, b, into)
    return pl.pallas_call(
        body, name=name, grid=grid, in_specs=[a_spec, b_spec] + ([] if into is None else [_ANY]),
        out_specs=o_spec, out_shape=out_shape,
        scratch_shapes=[pltpu.VMEM(o_blk if nk > 1 else (8, 128), f32)], compiler_params=_cp(3), **extra,
    )(*operands)


def _tok_tile(t):
    return min(t, 512)


def _tok_spec(tm, c, nt=None, reverse=False):
    if reverse:
        return pl.BlockSpec((tm, c), lambda i: (nt - 1 - i, 0))
    return pl.BlockSpec((tm, c), lambda i: (i, 0))


def _row_spec(c, rows=1):
    return pl.BlockSpec((rows, c), lambda i: (0, 0))


def _acc_add(first, ref, val):
    @pl.when(first)
    def _():
        ref[...] = val

    @pl.when(jnp.logical_not(first))
    def _():
        ref[...] += val


def _adaln(x, g, scale, shift):
    y = x * lax.rsqrt(jnp.mean(x * x, axis=-1, keepdims=True) + NORM_EPS)
    return (y * g) * (1.0 + scale) + shift


def _ln_fwd(x, g, scale, shift):
    t = x.shape[0]
    tm = _tok_tile(t)

    def body(x_ref, g_ref, sc_ref, sh_ref, h_ref):
        h_ref[...] = _adaln(x_ref[...], g_ref[...], sc_ref[...], sh_ref[...]).astype(bf16)

    return pl.pallas_call(
        body, name="ln_fwd", grid=(t // tm,),
        in_specs=[_tok_spec(tm, D), _row_spec(D), _row_spec(D), _row_spec(D)],
        out_specs=_tok_spec(tm, D), out_shape=jax.ShapeDtypeStruct((t, D), bf16),
        compiler_params=_cp(1),
    )(x, g, scale, shift)


def _ln_bwd(x, g, scale, shift, dh, dxn):
    t = x.shape[0]
    tm = _tok_tile(t)

    def body(x_ref, g_ref, sc_ref, sh_ref, dh_ref, dxn_ref, dx_ref, dg_ref, dsc_ref, dsh_ref):
        _, vjp = jax.vjp(_adaln, x_ref[...], g_ref[...], sc_ref[...], sh_ref[...])
        dx, dg, dsc, dsh = vjp(dh_ref[...])
        dx_ref[...] = dxn_ref[...] + dx
        first = pl.program_id(0) == 0
        _acc_add(first, dg_ref, dg)
        _acc_add(first, dsc_ref, dsc)
        _acc_add(first, dsh_ref, dsh)

    row = jax.ShapeDtypeStruct((1, D), f32)
    return pl.pallas_call(
        body, name="ln_bwd", grid=(t // tm,),
        in_specs=[_tok_spec(tm, D), _row_spec(D), _row_spec(D), _row_spec(D), _tok_spec(tm, D), _tok_spec(tm, D)],
        out_specs=[_tok_spec(tm, D), _row_spec(D), _row_spec(D), _row_spec(D)],
        out_shape=[jax.ShapeDtypeStruct((t, D), f32), row, row, row],
        compiler_params=_cp(1),
    )(x, g, scale, shift, dh, dxn)


def _swiglu(g, u):
    return jax.nn.silu(g) * u


def _ffn_tile(t):
    return min(t, 512)


def _ffn_specs(tm, idx, rows):
    once = pl.Buffered(1)
    return (pl.BlockSpec((2, FFN_SLABS, 1, D, FFN_HS), lambda i, q: (0, 0, idx, 0, 0), pipeline_mode=once),
            pl.BlockSpec((N_DEV, 1, rows, D), lambda i, q: (0, idx, 0, 0), pipeline_mode=once),
            pl.BlockSpec((2, 1, tm, FFN_HS), lambda i, q: (0, q, i, 0)),
            pl.BlockSpec((tm, D), lambda i, q: (i, 0)),
            pl.BlockSpec((1, D), lambda i, q: (0, 0)))


def _out_rows(wo_ref, q):
    return wo_ref[pl.ds(2 * q, 2), 0].reshape(FFN_HS, D)


def _ffn_up(x, ln, w_in, w_out, idx, coef):
    g_, shift, scale, gate = ln
    t = x.shape[0]
    tm = _ffn_tile(t)
    rows = w_out.shape[2]

    def body(wi_ref, wo_ref, x_ref, g_ref, sc_ref, sh_ref, gt_ref, h_ref, gu_ref, a_ref, o_ref, xn_ref):
        q = pl.program_id(1)

        @pl.when(q == 0)
        def _():
            h_ref[...] = _adaln(x_ref[...], g_ref[...], sc_ref[...], sh_ref[...]).astype(bf16)

        hv = h_ref[...]
        g = jnp.dot(hv, wi_ref[0, q, 0], preferred_element_type=f32)
        u = jnp.dot(hv, wi_ref[1, q, 0], preferred_element_type=f32)
        gu_ref[0, 0] = g.astype(bf16)
        gu_ref[1, 0] = u.astype(bf16)
        a = _swiglu(g, u).astype(bf16)
        a_ref[0] = a
        _acc_add(q == 0, o_ref, jnp.dot(a, _out_rows(wo_ref, q), preferred_element_type=f32))

        @pl.when(q == FFN_SLABS - 1)
        def _():
            xn_ref[...] = x_ref[...] + (coef * (1.0 + gt_ref[...])) * o_ref[...]

    wi, wo, slab, tok, row = _ffn_specs(tm, idx, rows)
    tok_f32 = jax.ShapeDtypeStruct((t, D), f32)
    return pl.pallas_call(
        body, name="ffn_up", grid=(t // tm, FFN_SLABS), in_specs=[wi, wo, tok, row, row, row, row],
        out_specs=[tok, slab, pl.BlockSpec((1, tm, FFN_HS), lambda i, q: (q, i, 0)), tok, tok],
        out_shape=[jax.ShapeDtypeStruct((t, D), bf16), jax.ShapeDtypeStruct((2, FFN_SLABS, t, FFN_HS), bf16),
                   jax.ShapeDtypeStruct((FFN_SLABS, t, FFN_HS), bf16), tok_f32, tok_f32],
        compiler_params=_cp(2),
    )(w_in, w_out, x, g_, scale, shift, gate)


def _ffn_dgu(do, w_out, idx, gu):
    t = do.shape[0]
    tm = _mm_tile(t)

    def body(do_ref, wo_ref, gu_ref, o_ref):
        w = _out_rows(wo_ref, pl.program_id(1))
        half = tm // 2
        for r in range(2):
            rs = slice(r * half, (r + 1) * half)
            da = lax.dot_general(do_ref[rs, :], w, _DN["nt"], preferred_element_type=f32)
            _, vjp = jax.vjp(_swiglu, gu_ref[0, 0, rs, :].astype(f32), gu_ref[1, 0, rs, :].astype(f32))
            dg, du = vjp(da)
            o_ref[0, 0, rs, :] = dg.astype(bf16)
            o_ref[1, 0, rs, :] = du.astype(bf16)

    _, wo, slab, tok, _ = _ffn_specs(tm, idx, w_out.shape[2])
    return pl.pallas_call(
        body, name="ffn_dgu", grid=(t // tm, FFN_SLABS), in_specs=[tok, wo, slab],
        out_specs=slab, out_shape=jax.ShapeDtypeStruct((2, FFN_SLABS, t, FFN_HS), bf16),
        compiler_params=_cp(2),
    )(do, w_out, gu)


def _ffn_dh(dgu, w_in, idx):
    t = dgu.shape[2]
    tm = _mm_tile(t)

    def body(d_ref, wi_ref, o_ref):
        q = pl.program_id(1)
        p = (lax.dot_general(d_ref[0, 0], wi_ref[0, q, 0], _DN["nt"], preferred_element_type=f32)
             + lax.dot_general(d_ref[1, 0], wi_ref[1, q, 0], _DN["nt"], preferred_element_type=f32))
        _acc_add(pl.program_id(1) == 0, o_ref, p)

    wi, _, slab, tok, _ = _ffn_specs(tm, idx, 8)
    return pl.pallas_call(
        body, name="ffn_dh", grid=(t // tm, FFN_SLABS), in_specs=[slab, wi], out_specs=tok,
        out_shape=jax.ShapeDtypeStruct((t, D), f32), compiler_params=_cp(2),
    )(dgu, w_in)


def _resid_fwd(x, o, gate, coef):
    t = x.shape[0]
    tm = _tok_tile(t)

    def body(x_ref, o_ref, gt_ref, y_ref):
        y_ref[...] = x_ref[...] + (coef * (1.0 + gt_ref[...])) * o_ref[...]

    return pl.pallas_call(
        body, name="resid_fwd", grid=(t // tm,),
        in_specs=[_tok_spec(tm, D), _tok_spec(tm, D), _row_spec(D)],
        out_specs=_tok_spec(tm, D), out_shape=jax.ShapeDtypeStruct((t, D), f32),
        compiler_params=_cp(1),
    )(x, o, gate)


def _resid_bwd(dxn, o, gate, coef):
    t = dxn.shape[0]
    tm = _tok_tile(t)

    def body(dxn_ref, o_ref, gt_ref, do_ref, dgt_ref):
        dxn_v = dxn_ref[...]
        do_ref[...] = ((coef * (1.0 + gt_ref[...])) * dxn_v).astype(bf16)
        _acc_add(pl.program_id(0) == 0, dgt_ref, coef * jnp.sum(dxn_v * o_ref[...], axis=0, keepdims=True))

    return pl.pallas_call(
        body, name="resid_bwd", grid=(t // tm,),
        in_specs=[_tok_spec(tm, D), _tok_spec(tm, D), _row_spec(D)],
        out_specs=[_tok_spec(tm, D), _row_spec(D)],
        out_shape=[jax.ShapeDtypeStruct((t, D), bf16), jax.ShapeDtypeStruct((1, D), f32)],
        compiler_params=_cp(1),
    )(dxn, o, gate)


def _loss_head(y, target):
    t = y.shape[0]
    tm = _tok_tile(t)

    def body(y_ref, t_ref, dy_ref, l_ref):
        err = y_ref[...] - t_ref[...]
        dy_ref[...] = err * (1.0 / D)
        part = jnp.sum(jnp.sum(err * err, axis=0, keepdims=True), axis=1, keepdims=True) * (0.5 / D)
        _acc_add(pl.program_id(0) == 0, l_ref, jnp.broadcast_to(part, (1, 128)))

    return pl.pallas_call(
        body, name="loss_head", grid=(t // tm,),
        in_specs=[_tok_spec(tm, D), _tok_spec(tm, D)],
        out_specs=[_tok_spec(tm, D), _row_spec(128)],
        out_shape=[jax.ShapeDtypeStruct((t, D), f32), jax.ShapeDtypeStruct((1, 128), f32)],
        compiler_params=_cp(1),
    )(y, target)


def _mm_tile(t):
    return min(t, 1024)


def _ffn_fwd(x, ln, w_in, w_out, idx, coef):
    h, gu, a, o, xn = _ffn_up(x, ln, w_in.reshape((2, FFN_SLABS) + w_in.shape[1:]), w_out, idx, coef)
    return xn, (x, h, gu, a, o)


def _ffn_bwd(dxn, saved, ln, w_in, w_out, idx, coef, dw_in, dw_out):
    x, h, gu, a, o = saved
    g, shift, scale, gate = ln
    t = x.shape[0]
    tm = _mm_tile(t)
    tk = _pick(t, (4096, 2048, 1024, 512, 256, 128))
    rows = w_out.shape[2]
    sds = jax.ShapeDtypeStruct
    do, dgate = _resid_bwd(dxn, o, gate, coef)
    dgu = _ffn_dgu(do, w_out, idx, gu)
    dh = _ffn_dh(dgu, w_in.reshape((2, FFN_SLABS) + w_in.shape[1:]), idx)
    dgu = dgu.reshape(N_DEV, t, FFN_HS)
    dw_out = _mmx(a, do, "tn", "ffn_dwout", (FFN_SLABS, 1, t // tk),
                  pl.BlockSpec((1, tk, FFN_HS), lambda i, j, k: (i, k, 0)),
                  pl.BlockSpec((tk, D), lambda i, j, k: (k, 0)),
                  pl.BlockSpec((2, 1, rows, D), lambda i, j, k: (i, idx, 0, 0)),
                  sds(dw_out.shape, f32), into=dw_out)
    dw_in = _mmx(h, dgu, "tn", "ffn_dwin", (N_DEV, 1, t // tk),
                 pl.BlockSpec((tk, D), lambda i, j, k: (k, 0)),
                 pl.BlockSpec((1, tk, FFN_HS), lambda i, j, k: (i, k, 0)),
                 pl.BlockSpec((1, 1, D, FFN_HS), lambda i, j, k: (i, idx, 0, 0)),
                 sds(dw_in.shape, f32), into=dw_in)
    dx, dg, dscale, dshift = _ln_bwd(x, g, scale, shift, dh, dxn)
    return dx, (dg, dshift, dscale, dgate), dw_in, dw_out


def _roll_rows(a, k):
    n = a.shape[0]
    return pltpu.roll(a, k % n, 0)


def _pool_windows(hx, first_row, reverse):
    outs = []
    for gi, w in enumerate(POOL_WINDOWS):
        acc = hx[:, gi * POOL_GROUP:(gi + 1) * POOL_GROUP]
        k = 1
        while k < w:
            acc = acc + _roll_rows(acc, -k if reverse else k)
            k *= 2
        outs.append(acc)
    return outs


def _pool_cnt(t_idx, w):
    return jnp.minimum(t_idx + 1, w).astype(f32)


def _pool_pooled(x_ref, xp_ref, g, scale, shift, i, tm):
    h = _adaln(x_ref[...], g, scale, shift)
    hp = _adaln(xp_ref[...], g, scale, shift)
    hp = jnp.where(i == 0, 0.0, hp)
    hx = jnp.concatenate([hp, h], axis=0)
    sums = _pool_windows(hx, 0, False)
    t_idx = i * tm + lax.broadcasted_iota(jnp.int32, (tm, 1), 0)
    pooled = []
    for gi, w in enumerate(POOL_WINDOWS):
        s = sums[gi][POOL_HALO:, :]
        pooled.append(s / _pool_cnt(t_idx, w) - h[:, gi * POOL_GROUP:(gi + 1) * POOL_GROUP])
    return h, pooled


def _pool_specs(t, tm):
    per = tm // POOL_HALO
    prev = pl.BlockSpec((POOL_HALO, D), lambda i: (jnp.maximum(i * per - 1, 0), 0))
    return [_tok_spec(tm, D), prev, _row_spec(D), _row_spec(D), _row_spec(D),
            pl.BlockSpec((4, POOL_GROUP, POOL_GROUP), lambda i: (0, 0, 0)), _row_spec(D), _row_spec(D)]


def _pool_fwd(x, ln, w, pscale):
    g, shift, scale, gate = ln
    t = x.shape[0]
    tm = _tok_tile(t)

    def body(x_ref, xp_ref, g_ref, sc_ref, sh_ref, w_ref, ps_ref, gt_ref, y_ref):
        i = pl.program_id(0)
        _, pooled = _pool_pooled(x_ref, xp_ref, g_ref[...], sc_ref[...], sh_ref[...], i, tm)
        mixed = [jnp.dot(pooled[gi].astype(bf16), w_ref[gi], preferred_element_type=f32) for gi in range(4)]
        y = jnp.concatenate(mixed, axis=1) * ps_ref[...]
        y_ref[...] = x_ref[...] + (1.0 + gt_ref[...]) * y

    return pl.pallas_call(
        body, name="pool_fwd", grid=(t // tm,), in_specs=_pool_specs(t, tm),
        out_specs=_tok_spec(tm, D), out_shape=jax.ShapeDtypeStruct((t, D), f32),
        compiler_params=_cp(1),
    )(x, x, g, scale, shift, w, pscale, gate)


def _pool_bwd(dxn, x, ln, w, pscale):
    g, shift, scale, gate = ln
    t = x.shape[0]
    tm = _tok_tile(t)
    nt = t // tm
    per = tm // POOL_HALO

    def body_a(x_ref, xp_ref, g_ref, sc_ref, sh_ref, w_ref, ps_ref, gt_ref, dxn_ref,
               dp_ref, dw_ref, dps_ref, dgt_ref):
        i = pl.program_id(0)
        first = i == 0
        _, pooled = _pool_pooled(x_ref, xp_ref, g_ref[...], sc_ref[...], sh_ref[...], i, tm)
        dxn_v = dxn_ref[...]
        dy = (1.0 + gt_ref[...]) * dxn_v
        dmixed = dy * ps_ref[...]
        mixed, dps = [], []
        for gi in range(4):
            sl = slice(gi * POOL_GROUP, (gi + 1) * POOL_GROUP)
            pb = pooled[gi].astype(bf16)
            dmb = dmixed[:, sl].astype(bf16)
            mixed.append(jnp.dot(pb, w_ref[gi], preferred_element_type=f32))
            dp_ref[:, sl] = lax.dot_general(dmb, w_ref[gi], _DN["nt"], preferred_element_type=f32)
            dwg = lax.dot_general(pb, dmb, _DN["tn"], preferred_element_type=f32)

            @pl.when(first)
            def _():
                dw_ref[gi] = dwg

            @pl.when(jnp.logical_not(first))
            def _():
                dw_ref[gi] += dwg
        mixed = jnp.concatenate(mixed, axis=1)
        _acc_add(first, dps_ref, jnp.sum(dy * mixed, axis=0, keepdims=True))
        _acc_add(first, dgt_ref, jnp.sum(dxn_v * (mixed * ps_ref[...]), axis=0, keepdims=True))

    row = jax.ShapeDtypeStruct((1, D), f32)
    dpooled, dw, dps, dgate = pl.pallas_call(
        body_a, name="pool_bwd_a", grid=(nt,), in_specs=_pool_specs(t, tm) + [_tok_spec(tm, D)],
        out_specs=[_tok_spec(tm, D), pl.BlockSpec((4, POOL_GROUP, POOL_GROUP), lambda i: (0, 0, 0)),
                   _row_spec(D), _row_spec(D)],
        out_shape=[jax.ShapeDtypeStruct((t, D), f32), jax.ShapeDtypeStruct((4, POOL_GROUP, POOL_GROUP), f32), row, row],
        compiler_params=_cp(1),
    )(x, x, g, scale, shift, w, pscale, gate, dxn)

    def body_b(dp_ref, dpn_ref, x_ref, g_ref, sc_ref, sh_ref, dxn_ref, dx_ref, dg_ref, dsc_ref, dsh_ref):
        i = pl.program_id(0)
        dp = dp_ref[...]
        dpn = jnp.where(i == nt - 1, 0.0, dpn_ref[...])
        t_idx = i * tm + lax.broadcasted_iota(jnp.int32, (tm + POOL_HALO, 1), 0)
        ex = jnp.concatenate([dp, dpn], axis=0)
        parts = []
        for gi, w_ in enumerate(POOL_WINDOWS):
            parts.append(ex[:, gi * POOL_GROUP:(gi + 1) * POOL_GROUP] / _pool_cnt(t_idx, w_))
        sums = _pool_windows(jnp.concatenate(parts, axis=1), 0, True)
        dh = jnp.concatenate([s[:tm, :] for s in sums], axis=1) - dp
        _, vjp = jax.vjp(_adaln, x_ref[...], g_ref[...], sc_ref[...], sh_ref[...])
        dx, dg, dsc, dsh = vjp(dh)
        dx_ref[...] = dxn_ref[...] + dx
        first = i == 0
        _acc_add(first, dg_ref, dg)
        _acc_add(first, dsc_ref, dsc)
        _acc_add(first, dsh_ref, dsh)

    nxt = pl.BlockSpec((POOL_HALO, D), lambda i: (jnp.minimum((i + 1) * per, t // POOL_HALO - 1), 0))
    dx, dg, dscale, dshift = pl.pallas_call(
        body_b, name="pool_bwd_b", grid=(nt,),
        in_specs=[_tok_spec(tm, D), nxt, _tok_spec(tm, D), _row_spec(D), _row_spec(D), _row_spec(D), _tok_spec(tm, D)],
        out_specs=[_tok_spec(tm, D), _row_spec(D), _row_spec(D), _row_spec(D)],
        out_shape=[jax.ShapeDtypeStruct((t, D), f32), row, row, row],
        compiler_params=_cp(1),
    )(dpooled, dpooled, x, g, scale, shift, dxn)
    return dx, (dg, dshift, dscale, dgate), dw, dps


def _conv_taps(czx, cw):
    return cw[0:1, :] * _roll_rows(czx, 2) + cw[1:2, :] * _roll_rows(czx, 1) + cw[2:3, :] * czx


def _conv_fwd(p, cw):
    t = p.shape[0]
    tm = _tok_tile(t)
    per = tm // CONV_HALO

    def body(p_ref, pp_ref, cw_ref, q_ref):
        i = pl.program_id(0)
        cz = p_ref[:, D:2 * D] * p_ref[:, 2 * D:3 * D]
        czp = jnp.where(i == 0, 0.0, pp_ref[:, D:2 * D] * pp_ref[:, 2 * D:3 * D])
        conv = _conv_taps(jnp.concatenate([czp, cz], axis=0), cw_ref[...])[CONV_HALO:, :]
        q_ref[...] = (p_ref[:, 0:D] * conv).astype(bf16)

    prev = pl.BlockSpec((CONV_HALO, 3 * D), lambda i: (jnp.maximum(i * per - 1, 0), 0))
    return pl.pallas_call(
        body, name="conv_fwd", grid=(t // tm,),
        in_specs=[_tok_spec(tm, 3 * D), prev, _row_spec(D, 8)],
        out_specs=_tok_spec(tm, D), out_shape=jax.ShapeDtypeStruct((t, D), bf16),
        compiler_params=_cp(1),
    )(p, p, cw)


def _conv_bwd(p, cw, dq):
    t = p.shape[0]
    tm = _tok_tile(t)
    nt = t // tm
    per = tm // CONV_HALO

    def body(p_ref, pp_ref, pn_ref, cw_ref, dq_ref, dqn_ref, dp_ref, dcw_ref):
        i = pl.program_id(0)
        cw_v = cw_ref[...]
        b, c, z = p_ref[:, 0:D], p_ref[:, D:2 * D], p_ref[:, 2 * D:3 * D]
        cz = c * z
        czp = jnp.where(i == 0, 0.0, pp_ref[:, D:2 * D] * pp_ref[:, 2 * D:3 * D])
        czx = jnp.concatenate([czp, cz], axis=0)
        conv = _conv_taps(czx, cw_v)[CONV_HALO:, :]
        dq_v = dq_ref[...]
        dconv = dq_v * b
        dconv_n = jnp.where(i == nt - 1, 0.0, dqn_ref[...] * pn_ref[:, 0:D])
        dcx = jnp.concatenate([dconv, dconv_n], axis=0)
        dcz = (cw_v[2:3, :] * dcx + cw_v[1:2, :] * _roll_rows(dcx, -1) + cw_v[0:1, :] * _roll_rows(dcx, -2))[:tm, :]
        dp_ref[:, 0:D] = (dq_v * conv).astype(bf16)
        dp_ref[:, D:2 * D] = (dcz * z).astype(bf16)
        dp_ref[:, 2 * D:3 * D] = (dcz * c).astype(bf16)
        dw2 = jnp.sum(dconv * cz, axis=0, keepdims=True)
        dw1 = jnp.sum(dconv * _roll_rows(czx, 1)[CONV_HALO:, :], axis=0, keepdims=True)
        dw0 = jnp.sum(dconv * _roll_rows(czx, 2)[CONV_HALO:, :], axis=0, keepdims=True)
        _acc_add(i == 0, dcw_ref, jnp.concatenate([dw0, dw1, dw2, jnp.zeros((5, D), f32)], axis=0))

    prev = pl.BlockSpec((CONV_HALO, 3 * D), lambda i: (jnp.maximum(i * per - 1, 0), 0))
    last = t // CONV_HALO - 1
    nxt3 = pl.BlockSpec((CONV_HALO, 3 * D), lambda i: (jnp.minimum((i + 1) * per, last), 0))
    nxt1 = pl.BlockSpec((CONV_HALO, D), lambda i: (jnp.minimum((i + 1) * per, last), 0))
    return pl.pallas_call(
        body, name="conv_bwd", grid=(nt,),
        in_specs=[_tok_spec(tm, 3 * D), prev, nxt3, _row_spec(D, 8), _tok_spec(tm, D), nxt1],
        out_specs=[_tok_spec(tm, 3 * D), _row_spec(D, 8)],
        out_shape=[jax.ShapeDtypeStruct((t, 3 * D), bf16), jax.ShapeDtypeStruct((8, D), f32)],
        compiler_params=_cp(1),
    )(p, p, p, cw, dq, dq)


def _conv_tile(t):
    return min(t, 2048)


def _convmix_fwd(x, ln, w_in, cw, w_out):
    g, shift, scale, gate = ln
    t = x.shape[0]
    tm = _conv_tile(t)
    cs = w_in.shape[2]
    h = _ln_fwd(x, g, scale, shift)
    p = _mmx(h, w_in, "nn", "conv_in", (t // tm, N_DEV, 1),
             pl.BlockSpec((tm, D), lambda i, j, k: (i, 0)),
             pl.BlockSpec((1, D, cs), lambda i, j, k: (j, 0, 0)),
             pl.BlockSpec((tm, cs), lambda i, j, k: (i, j)), jax.ShapeDtypeStruct((t, N_DEV * cs), f32))
    q = _conv_fwd(p, cw)
    y = _mm(q, w_out, "nn", f32, "conv_out")
    return _resid_fwd(x, y, gate, 1.0), (x, h, p, q, y)


def _convmix_bwd(dxn, saved, ln, w_in, cw, w_out):
    x, h, p, q, y = saved
    g, shift, scale, gate = ln
    dy, dgate = _resid_bwd(dxn, y, gate, 1.0)
    dq = _mm(dy, w_out, "nt", f32, "conv_dq")
    dw_out = _mm(q, dy, "tn", f32, "conv_dwout")
    dp, dcw = _conv_bwd(p, cw, dq)
    t = x.shape[0]
    tm = _conv_tile(t)
    tk = _pick(t, (2048, 1024, 512, 256, 128))
    cs = w_in.shape[2]
    dh = _mmx(dp, w_in, "nt", "conv_dh", (t // tm, 1, N_DEV),
              pl.BlockSpec((tm, cs), lambda i, j, k: (i, k)),
              pl.BlockSpec((1, D, cs), lambda i, j, k: (k, 0, 0)),
              pl.BlockSpec((tm, D), lambda i, j, k: (i, 0)), jax.ShapeDtypeStruct((t, D), f32))
    dw_in = _mmx(h, dp, "tn", "conv_dwin", (N_DEV, 1, t // tk),
                 pl.BlockSpec((tk, D), lambda i, j, k: (k, 0)),
                 pl.BlockSpec((tk, cs), lambda i, j, k: (k, i)),
                 pl.BlockSpec((1, D, cs), lambda i, j, k: (i, 0, 0)), jax.ShapeDtypeStruct((N_DEV, D, cs), f32))
    dx, dg, dscale, dshift = _ln_bwd(x, g, scale, shift, dh, dxn)
    return dx, (dg, dshift, dscale, dgate), dw_in, dcw[0:3], dw_out


def _exact_dot(tri, v):
    v1 = v.astype(bf16)
    r1 = v - v1.astype(f32)
    v2 = r1.astype(bf16)
    v3 = (r1 - v2.astype(f32)).astype(bf16)
    d = lambda p: jnp.dot(tri, p, preferred_element_type=f32)
    return d(v1) + d(v2) + d(v3)


def _fox_cumf(fl, b_f):
    t = fl.shape[0]
    tc = min(t, 256)

    def body(fl_ref, b_ref, f_ref, carry_ref):
        i = pl.program_id(0)

        @pl.when(i == 0)
        def _():
            carry_ref[...] = jnp.zeros_like(carry_ref)

        lf = jax.nn.log_sigmoid(fl_ref[...] + b_ref[...])
        r = lax.broadcasted_iota(jnp.int32, (tc, tc), 0)
        c = lax.broadcasted_iota(jnp.int32, (tc, tc), 1)
        tri = (r >= c).astype(bf16)
        fc = _exact_dot(tri, lf) + carry_ref[0:1, :]
        f_ref[...] = fc
        carry_ref[0:1, :] = fc[tc - 1:tc, :]

    return pl.pallas_call(
        body, name="fox_cumf", grid=(t // tc,),
        in_specs=[_tok_spec(tc, 128), _row_spec(128)],
        out_specs=_tok_spec(tc, 128), out_shape=jax.ShapeDtypeStruct((t, 128), f32),
        scratch_shapes=[pltpu.VMEM((8, 128), f32)], compiler_params=_cp(1),
    )(fl, b_f)


def _fox_cumf_bwd(df, fl, b_f):
    t = fl.shape[0]
    tc = min(t, 256)
    nt = t // tc

    def body(df_ref, fl_ref, b_ref, dfl_ref, db_ref, carry_ref):
        i = pl.program_id(0)

        @pl.when(i == 0)
        def _():
            carry_ref[...] = jnp.zeros_like(carry_ref)

        r = lax.broadcasted_iota(jnp.int32, (tc, tc), 0)
        c = lax.broadcasted_iota(jnp.int32, (tc, tc), 1)
        tri = (r <= c).astype(bf16)
        dlf = _exact_dot(tri, df_ref[...]) + carry_ref[0:1, :]
        carry_ref[0:1, :] = dlf[0:1, :]
        dfl = dlf * jax.nn.sigmoid(-(fl_ref[...] + b_ref[...]))
        dfl_ref[...] = dfl
        _acc_add(i == 0, db_ref, jnp.sum(dfl, axis=0, keepdims=True))

    return pl.pallas_call(
        body, name="fox_cumf_bwd", grid=(nt,),
        in_specs=[_tok_spec(tc, 128, nt, True), _tok_spec(tc, 128, nt, True), _row_spec(128)],
        out_specs=[_tok_spec(tc, 128, nt, True), _row_spec(128)],
        out_shape=[jax.ShapeDtypeStruct((t, 128), f32), jax.ShapeDtypeStruct((1, 128), f32)],
        scratch_shapes=[pltpu.VMEM((8, 128), f32)], compiler_params=_cp(1),
    )(df, fl, b_f)


def _fox_block(t):
    return min(t, 512)


FOX_STRIP = 64


def _fox_mask(s, row0):
    r = row0 + lax.broadcasted_iota(jnp.int32, s.shape, 0)
    c = lax.broadcasted_iota(jnp.int32, s.shape, 1)
    return jnp.where(r >= c, s, -jnp.inf)


def _strips(tb):
    ts = min(tb, FOX_STRIP)
    return [(r * ts, slice(r * ts, (r + 1) * ts)) for r in range(tb // ts)]


FOX_PAIRS = FOX_HEADS // 2
FOX_PW = 2 * FOX_HEAD_DIM


def _half(shape, hh):
    lane = lax.broadcasted_iota(jnp.int32, shape, len(shape) - 1)
    return lane < FOX_HEAD_DIM if hh == 0 else lane >= FOX_HEAD_DIM


def _by_half(v0, v1):
    return jnp.where(_half(v0.shape, 0), v0, v1)


def _half_sums(v):
    first = _half(v.shape, 0)
    s0 = jnp.sum(jnp.where(first, v, 0.0), axis=-1, keepdims=True)
    s1 = jnp.sum(jnp.where(first, 0.0, v), axis=-1, keepdims=True)
    return jnp.where(first, s0, s1)


def _pair_norm(v, gain, mult):
    ms = _half_sums(v * v) * (1.0 / FOX_HEAD_DIM)
    return v * lax.rsqrt(ms + NORM_EPS) * gain * mult


def _pair_cols(p):
    return slice(p * FOX_PW, (p + 1) * FOX_PW)


def _pair_qknorm(proj, q_gain, k_gain):
    t = proj.shape[0]
    tm = _tok_tile(t)

    def body(q_ref, k_ref, v_ref, qg_ref, kg_ref, o_ref):
        for p in range(FOX_PAIRS):
            cb = _pair_cols(p)
            o_ref[:, cb] = _pair_norm(q_ref[:, cb], qg_ref[...], FOX_HEAD_DIM ** -0.5).astype(bf16)
            o_ref[:, D + p * FOX_PW:D + (p + 1) * FOX_PW] = _pair_norm(k_ref[:, cb], kg_ref[...], 1.0).astype(bf16)
        o_ref[:, 2 * D:3 * D] = v_ref[...].astype(bf16)

    part = lambda s: pl.BlockSpec((tm, D), lambda i: (i, s))
    return pl.pallas_call(
        body, name="fox_qknorm", grid=(t // tm,),
        in_specs=[part(0), part(1), part(2), _row_spec(FOX_PW), _row_spec(FOX_PW)],
        out_specs=_tok_spec(tm, 3 * D), out_shape=jax.ShapeDtypeStruct((t, 3 * D), bf16), compiler_params=_cp(1),
    )(proj, proj, proj, q_gain, k_gain)


def _pair_qknorm_bwd(proj, q_gain, k_gain, dqn, dkn):
    t = proj.shape[0]
    tm = _tok_tile(t)

    def body(q_ref, k_ref, qg_ref, kg_ref, dqn_ref, dkn_ref, o_ref, dqg_ref, dkg_ref):
        dqg = jnp.zeros((1, FOX_PW), f32)
        dkg = jnp.zeros((1, FOX_PW), f32)
        for p in range(FOX_PAIRS):
            cb = _pair_cols(p)
            _, vq = jax.vjp(lambda v, gn: _pair_norm(v, gn, FOX_HEAD_DIM ** -0.5), q_ref[:, cb], qg_ref[...])
            dq, g1 = vq(dqn_ref[:, cb])
            _, vk = jax.vjp(lambda v, gn: _pair_norm(v, gn, 1.0), k_ref[:, cb], kg_ref[...])
            dk, g2 = vk(dkn_ref[:, cb])
            o_ref[:, cb] = dq.astype(bf16)
            o_ref[:, D + p * FOX_PW:D + (p + 1) * FOX_PW] = dk.astype(bf16)
            dqg, dkg = dqg + g1, dkg + g2
        first = pl.program_id(0) == 0
        _acc_add(first, dqg_ref, dqg + pltpu.roll(dqg, FOX_HEAD_DIM, 1))
        _acc_add(first, dkg_ref, dkg + pltpu.roll(dkg, FOX_HEAD_DIM, 1))

    part = lambda s: pl.BlockSpec((tm, D), lambda i: (i, s))
    gshape = jax.ShapeDtypeStruct((1, FOX_PW), f32)
    return pl.pallas_call(
        body, name="fox_qknorm_bwd", grid=(t // tm,),
        in_specs=[part(0), part(1), _row_spec(FOX_PW), _row_spec(FOX_PW), _tok_spec(tm, D), _tok_spec(tm, D)],
        out_specs=[_tok_spec(tm, 2 * D), _row_spec(FOX_PW), _row_spec(FOX_PW)],
        out_shape=[jax.ShapeDtypeStruct((t, 2 * D), bf16), gshape, gshape], compiler_params=_cp(1),
    )(proj, proj, q_gain, k_gain, dqn, dkn)


def _pair_delta(o, do):
    t = o.shape[0]
    tm = _tok_tile(t)

    def body(o_ref, do_ref, dl_ref, dob_ref):
        dob = do_ref[...].astype(bf16)
        dob_ref[...] = dob
        prod = dob.astype(f32) * o_ref[...]
        for p in range(FOX_PAIRS):
            dl_ref[p] = _half_sums(prod[:, _pair_cols(p)])

    return pl.pallas_call(
        body, name="fox_delta", grid=(t // tm,), in_specs=[_tok_spec(tm, D), _tok_spec(tm, D)],
        out_specs=[pl.BlockSpec((FOX_PAIRS, tm, FOX_PW), lambda i: (0, i, 0)), _tok_spec(tm, D)],
        out_shape=[jax.ShapeDtypeStruct((FOX_PAIRS, t, FOX_PW), f32), jax.ShapeDtypeStruct((t, D), bf16)],
        compiler_params=_cp(1),
    )(o, do)


def _pair_attn_fwd(qkvn, fcol, fref):
    t = qkvn.shape[0]
    tb = _fox_block(t)
    nq = t // tb
    pw = FOX_PW
    kcol, vcol = D // pw, 2 * D // pw

    def body(q_ref, k_ref, v_ref, fc_ref, fr_ref, o_ref, ob_ref, lse_ref):
        i = pl.program_id(1)
        qp = q_ref[...]
        qh = [jnp.where(_half(qp.shape, hh), qp, jnp.zeros_like(qp)) for hh in range(2)]

        def step(j, carry, diag):
            off = pl.multiple_of(j * tb, tb)
            kj = k_ref[pl.ds(off, tb), :]
            vj = v_ref[pl.ds(off, tb), :]
            out = []
            for hh in range(2):
                m, l, acc = carry[hh]
                bias = fr_ref[0, hh, pl.ds(i, 1), 0:1] - fc_ref[0, hh:hh + 1, pl.ds(off, tb)]
                s = lax.dot_general(qh[hh], kj, _DN["nt"], preferred_element_type=f32) + bias
                if diag:
                    s = _fox_mask(s, 0)
                m_new = jnp.maximum(m, jnp.max(s, axis=-1, keepdims=True))
                alpha = jnp.exp(m - m_new)
                p = jnp.exp(s - m_new)
                l = alpha * l + jnp.sum(p, axis=-1, keepdims=True)
                p_hi = p.astype(bf16)
                p_lo = (p - p_hi.astype(f32)).astype(bf16)
                pv = jnp.dot(p_hi, vj, preferred_element_type=f32) + jnp.dot(p_lo, vj, preferred_element_type=f32)
                out.append((m_new, l, alpha * acc + pv))
            return tuple(out)

        one = (jnp.full((tb, 1), -jnp.inf, f32), jnp.zeros((tb, 1), f32), jnp.zeros((tb, pw), f32))
        carry = lax.fori_loop(0, i, lambda j, c: step(j, c, False), (one, one))
        (m0, l0, a0), (m1, l1, a1) = step(i, carry, True)
        o = _by_half(a0 / l0, a1 / l1)
        o_ref[...] = o
        ob_ref[...] = o.astype(bf16)
        lse_ref[0] = _by_half(jnp.broadcast_to(m0 + jnp.log(l0), (tb, pw)), jnp.broadcast_to(m1 + jnp.log(l1), (tb, pw)))

    return pl.pallas_call(
        body, name="fox_attn_fwd", grid=(FOX_PAIRS, nq),
        in_specs=[pl.BlockSpec((tb, pw), lambda p, i: (i, p)),
                  pl.BlockSpec((t, pw), lambda p, i: (0, kcol + p)),
                  pl.BlockSpec((t, pw), lambda p, i: (0, vcol + p)),
                  pl.BlockSpec((1, 2, t), lambda p, i: (p, 0, 0)),
                  pl.BlockSpec((1, 2, nq, 128), lambda p, i: (p, 0, 0, 0))],
        out_specs=[pl.BlockSpec((tb, pw), lambda p, i: (i, p)), pl.BlockSpec((tb, pw), lambda p, i: (i, p)),
                   pl.BlockSpec((1, tb, pw), lambda p, i: (p, i, 0))],
        out_shape=[jax.ShapeDtypeStruct((t, D), f32), jax.ShapeDtypeStruct((t, D), bf16),
                   jax.ShapeDtypeStruct((FOX_PAIRS, t, pw), f32)],
        compiler_params=_cp(2),
    )(qkvn, qkvn, qkvn, fcol, fref)


def _pair_attn_bwd(qkvn, fcol, fref, lse, delta, dob):
    t = qkvn.shape[0]
    tb = _fox_block(t)
    nq = t // tb
    pw = FOX_PW
    kcol, vcol = D // pw, 2 * D // pw

    def body(q_ref, k_ref, v_ref, fc_ref, fr_ref, lse_ref, dl_ref, do_ref, dk_ref, dv_ref, df_ref, dq_ref,
             s_ref, dp_ref, p_ref, ds_ref):
        j = pl.program_id(1)
        kj = k_ref[...]
        vj = v_ref[...]
        dk_ref[...] = jnp.zeros_like(dk_ref)
        dv_ref[...] = jnp.zeros_like(dv_ref)
        df_ref[...] = jnp.zeros_like(df_ref)

        @pl.when(j == 0)
        def _():
            dq_ref[...] = jnp.zeros_like(dq_ref)

        def step(i, diag):
            off = pl.multiple_of(i * tb, tb)
            qi = q_ref[pl.ds(off, tb), :]
            doi = do_ref[pl.ds(off, tb), :]
            parts = []
            for hh in range(2):
                mine = _half(qi.shape, hh)
                c0 = hh * FOX_HEAD_DIM
                s_ref[hh] = lax.dot_general(jnp.where(mine, qi, jnp.zeros_like(qi)), kj, _DN["nt"],
                                            preferred_element_type=f32)
                dp_ref[hh] = lax.dot_general(jnp.where(mine, doi, jnp.zeros_like(doi)), vj, _DN["nt"],
                                             preferred_element_type=f32)
                bias = fr_ref[0, hh, pl.ds(i, 1), 0:1] - fc_ref[0, hh:hh + 1, :]
                df = jnp.zeros((1, tb), f32)
                for row0, rs in _strips(tb):
                    rows = pl.ds(off + row0, rs.stop - rs.start)
                    s = s_ref[hh, rs, :] + bias
                    if diag:
                        s = _fox_mask(s, row0)
                    p = jnp.exp(s - lse_ref[0, rows, c0:c0 + 1])
                    p_ref[hh, rs, :] = p.astype(bf16)
                    ds = p * (dp_ref[hh, rs, :] - dl_ref[0, rows, c0:c0 + 1])
                    ds_ref[hh, rs, :] = ds.astype(bf16)
                    df = df + jnp.sum(ds, axis=0, keepdims=True)
                df_ref[0, hh:hh + 1, :] -= df
                parts.append((lax.dot_general(p_ref[hh], doi, _DN["tn"], preferred_element_type=f32),
                              lax.dot_general(ds_ref[hh], qi, _DN["tn"], preferred_element_type=f32),
                              jnp.dot(ds_ref[hh], kj, preferred_element_type=f32)))
            dv_ref[...] += _by_half(parts[0][0], parts[1][0])
            dk_ref[...] += _by_half(parts[0][1], parts[1][1])
            dq_ref[pl.ds(off, tb), :] += _by_half(parts[0][2], parts[1][2])

        def loop_body(i, carry):
            step(i, False)
            return carry

        step(j, True)
        lax.fori_loop(j + 1, nq, loop_body, 0)

    scratch = [pltpu.VMEM((2, tb, tb), f32), pltpu.VMEM((2, tb, tb), f32), pltpu.VMEM((2, tb, tb), bf16),
               pltpu.VMEM((2, tb, tb), bf16)]
    blk = lambda c0: pl.BlockSpec((tb, pw), lambda p, j: (j, c0 + p))
    full = lambda c0: pl.BlockSpec((t, pw), lambda p, j: (0, c0 + p))
    stat = pl.BlockSpec((1, t, pw), lambda p, j: (p, 0, 0))
    nat = jax.ShapeDtypeStruct((t, D), f32)
    return pl.pallas_call(
        body, name="fox_attn_bwd", grid=(FOX_PAIRS, nq),
        in_specs=[full(0), blk(kcol), blk(vcol), pl.BlockSpec((1, 2, tb), lambda p, j: (p, 0, j)),
                  pl.BlockSpec((1, 2, nq, 128), lambda p, j: (p, 0, 0, 0)), stat, stat, full(0)],
        out_specs=[blk(0), blk(0), pl.BlockSpec((1, 2, tb), lambda p, j: (p, 0, j)), full(0)],
        out_shape=[nat, nat, jax.ShapeDtypeStruct((FOX_PAIRS, 2, t), f32), nat],
        scratch_shapes=scratch, compiler_params=_cp(2),
    )(qkvn, qkvn, qkvn, fcol, fref, lse, delta, dob)


def _fox_fwd(x, ln, w_in, b_f, q_gain, k_gain, w_o):
    g, shift, scale, gate = ln
    t = x.shape[0]
    tb = _fox_block(t)
    h = _ln_fwd(x, g, scale, shift)
    proj = _mm(h, w_in, "nn", f32, "fox_in")
    fl = proj[:, 3 * D:3 * D + 128]
    fcum = _fox_cumf(fl, b_f)
    fcol = fcum[:, :FOX_HEADS].T.reshape(FOX_PAIRS, 2, t)
    fref = jnp.broadcast_to(fcol[:, :, ::tb][..., None], (FOX_PAIRS, 2, t // tb, 128))
    gains = (jnp.tile(q_gain, (1, 2)), jnp.tile(k_gain, (1, 2)))
    qkvn = _pair_qknorm(proj, *gains)
    o, ob, lse = _pair_attn_fwd(qkvn, fcol, fref)
    y = _mm(ob, w_o, "nn", f32, "fox_out")
    return _resid_fwd(x, y, gate, 1.0), (x, h, proj, fl, fcol, fref, gains, qkvn, o, lse, ob, y)


def _fox_bwd(dxn, saved, ln, w_in, b_f, w_o):
    x, h, proj, fl, fcol, fref, gains, qkvn, o, lse, ob, y = saved
    g, shift, scale, gate = ln
    t = x.shape[0]
    dy, dgate = _resid_bwd(dxn, y, gate, 1.0)
    do = _mm(dy, w_o, "nt", f32, "fox_do")
    dw_o = _mm(ob, dy, "tn", f32, "fox_dwo")
    delta, dob = _pair_delta(o, do)
    dkn, dv, dfcol, dqn = _pair_attn_bwd(qkvn, fcol, fref, lse, delta, dob)
    dqk, dqg, dkg = _pair_qknorm_bwd(proj, *gains, dqn, dkn)
    df = jnp.pad(dfcol.reshape(FOX_HEADS, t).T, ((0, 0), (0, 128 - FOX_HEADS)))
    dfl, db_f = _fox_cumf_bwd(df, fl, b_f)
    dproj = jnp.concatenate([dqk, dv.astype(bf16), dfl.astype(bf16)], axis=1)
    dh = _mm(dproj, w_in, "nt", f32, "fox_dh")
    dw_in = _mm(h, dproj, "tn", f32, "fox_dwin")
    dx, dg, dscale, dshift = _ln_bwd(x, g, scale, shift, dh, dxn)
    return (dx, (dg, dshift, dscale, dgate), dw_in, db_f, dqg[:, :FOX_HEAD_DIM], dkg[:, :FOX_HEAD_DIM], dw_o)


def _s5_disc(lam_re, lam_im, log_dt, b_re, b_im):
    dt = jnp.exp(log_dt)
    mag = jnp.exp(lam_re * dt)
    lb_re, lb_im = mag * jnp.cos(lam_im * dt), mag * jnp.sin(lam_im * dt)
    den = lam_re * lam_re + lam_im * lam_im
    nr, ni = lb_re - 1.0, lb_im
    k_re = (nr * lam_re + ni * lam_im) / den
    k_im = (ni * lam_re - nr * lam_im) / den
    return lb_re, lb_im, k_re * b_re - k_im * b_im, k_re * b_im + k_im * b_re


def _s5_prep(lam_re, lam_im, log_dt, b_re, b_im):
    def body(ar_ref, ai_ref, dt_ref, br_ref, bi_ref, lr_ref, li_ref, bbr_ref, bbi_ref):
        lr, li, bbr, bbi = _s5_disc(ar_ref[...], ai_ref[...], dt_ref[...], br_ref[...], bi_ref[...])
        lr_ref[...] = lr
        li_ref[...] = li
        bbr_ref[...] = bbr
        bbi_ref[...] = bbi

    small = jax.ShapeDtypeStruct(lam_re.shape, f32)
    bigs = jax.ShapeDtypeStruct(b_re.shape, f32)
    return pl.pallas_call(body, name="s5_prep", out_shape=[small, small, bigs, bigs])(lam_re, lam_im, log_dt, b_re, b_im)


def _s5_prep_bwd(lam_re, lam_im, log_dt, b_re, b_im, dlr, dli, dbbr, dbbi):
    def body(ar_ref, ai_ref, dt_ref, br_ref, bi_ref, dlr_ref, dli_ref, dbbr_ref, dbbi_ref,
             dar_ref, dai_ref, ddt_ref, dbr_ref, dbi_ref):
        _, vjp = jax.vjp(_s5_disc, ar_ref[...], ai_ref[...], dt_ref[...], br_ref[...], bi_ref[...])
        dar, dai, ddt, dbr, dbi = vjp((dlr_ref[...], dli_ref[...], dbbr_ref[...], dbbi_ref[...]))
        dar_ref[...] = dar
        dai_ref[...] = dai
        ddt_ref[...] = jnp.broadcast_to(jnp.sum(ddt, axis=-1, keepdims=True), ddt.shape)
        dbr_ref[...] = dbr
        dbi_ref[...] = dbi

    small = jax.ShapeDtypeStruct(lam_re.shape, f32)
    bigs = jax.ShapeDtypeStruct(b_re.shape, f32)
    return pl.pallas_call(body, name="s5_prep_bwd", out_shape=[small, small, small, bigs, bigs])(
        lam_re, lam_im, log_dt, b_re, b_im, dlr, dli, dbbr, dbbi)


def _s5_tile(t):
    return min(t, 128)


def _s5_fwd_tile(t):
    return min(t, 256)


def _s5_blk(k, width):
    return slice(k * width, (k + 1) * width)


def _s5_in_bd(bb):
    b4 = bb.reshape(S5_BLOCKS, 8, S5_GROUP, S5_STATE)
    return jnp.einsum("kgin,gh->kgihn", b4, jnp.eye(8, dtype=bb.dtype)).reshape(S5_BLOCKS, S5_BCH, S5_BST)


def _s5_in_bd_diag(bd):
    b5 = bd.reshape(S5_BLOCKS, 8, S5_GROUP, 8, S5_STATE)
    return jnp.einsum("kgihn,gh->kgin", b5, jnp.eye(8, dtype=bd.dtype)).reshape(S5_GROUPS, S5_GROUP, S5_STATE)


def _s5_out_bd(c):
    c4 = c.reshape(S5_BLOCKS, 8, S5_GROUP, S5_STATE)
    return jnp.einsum("kgin,gh->kgnhi", c4, jnp.eye(8, dtype=c.dtype)).reshape(S5_BLOCKS, S5_BST, S5_BCH)


def _s5_out_bd_diag(bd):
    c5 = bd.reshape(S5_BLOCKS, 8, S5_STATE, 8, S5_GROUP)
    return jnp.einsum("kgnhi,gh->kgin", c5, jnp.eye(8, dtype=bd.dtype)).reshape(S5_GROUPS, S5_GROUP, S5_STATE)


def _s5_scan_fwd(x, ln, lb_re, lb_im, bbr_bd, bbi_bd, cr_bd, ci_bd, dskip):
    g, shift, scale, _ = ln
    t = x.shape[0]
    tm = _s5_fwd_tile(t)
    ns = S5_NSTATE

    def body(x_ref, g_ref, sc_ref, sh_ref, ar_ref, ai_ref, bbr_ref, bbi_ref, cr_ref, ci_ref, d_ref,
             yy_ref, xr_ref, xi_ref, cre_ref, cim_ref):
        @pl.when(pl.program_id(0) == 0)
        def _():
            cre_ref[...] = jnp.zeros_like(cre_ref)
            cim_ref[...] = jnp.zeros_like(cim_ref)

        h = _adaln(x_ref[...], g_ref[...], sc_ref[...], sh_ref[...])
        ub = h.astype(bf16)
        for k in range(S5_BLOCKS):
            uk = ub[:, _s5_blk(k, S5_BCH)]
            xr_ref[:, _s5_blk(k, S5_BST)] = jnp.dot(uk, bbr_ref[k], preferred_element_type=f32)
            xi_ref[:, _s5_blk(k, S5_BST)] = jnp.dot(uk, bbi_ref[k], preferred_element_type=f32)
        ar, ai = ar_ref[...], ai_ref[...]

        def step(tt, carry):
            sr, si = carry
            row = pl.ds(tt, 1)
            nr = (ar * sr - ai * si) + xr_ref[row, :]
            ni = (ar * si + ai * sr) + xi_ref[row, :]
            xr_ref[row, :] = nr
            xi_ref[row, :] = ni
            return nr, ni

        sr, si = lax.fori_loop(0, tm, step, (cre_ref[0:1, :], cim_ref[0:1, :]), unroll=2)
        cre_ref[0:1, :] = sr
        cim_ref[0:1, :] = si
        for k in range(S5_BLOCKS):
            sb = _s5_blk(k, S5_BST)
            yk = (jnp.dot(xr_ref[:, sb].astype(bf16), cr_ref[k], preferred_element_type=f32)
                  - jnp.dot(xi_ref[:, sb].astype(bf16), ci_ref[k], preferred_element_type=f32))
            cb = _s5_blk(k, S5_BCH)
            yy_ref[:, cb] = yk + d_ref[:, cb] * h[:, cb]

    bd_in = pl.BlockSpec((S5_BLOCKS, S5_BCH, S5_BST), lambda i: (0, 0, 0))
    bd_out = pl.BlockSpec((S5_BLOCKS, S5_BST, S5_BCH), lambda i: (0, 0, 0))
    st = jax.ShapeDtypeStruct((t, ns), f32)
    return pl.pallas_call(
        body, name="s5_scan_fwd", grid=(t // tm,),
        in_specs=[_tok_spec(tm, D), _row_spec(D), _row_spec(D), _row_spec(D), _row_spec(ns), _row_spec(ns),
                  bd_in, bd_in, bd_out, bd_out, _row_spec(D)],
        out_specs=[_tok_spec(tm, D), _tok_spec(tm, ns), _tok_spec(tm, ns)],
        out_shape=[jax.ShapeDtypeStruct((t, D), f32), st, st],
        scratch_shapes=[pltpu.VMEM((8, ns), f32), pltpu.VMEM((8, ns), f32)],
        compiler_params=_cp(1),
    )(x, g, scale, shift, lb_re, lb_im, bbr_bd, bbi_bd, cr_bd, ci_bd, dskip)


def _s5_scan_bwd(dyy, x, ln, xr, xi, lb_re, lb_im, bbr_bd, bbi_bd, cr_bd, ci_bd, dskip):
    g, shift, scale, _ = ln
    t = x.shape[0]
    tm = _s5_tile(t)
    nt = t // tm
    ns = S5_NSTATE
    per = tm // 8

    def body(dyy_ref, x_ref, g_ref, sc_ref, sh_ref, xr_ref, xi_ref, xrp_ref, xip_ref, ar_ref, ai_ref,
             bbr_ref, bbi_ref, cr_ref, ci_ref, d_ref,
             du_ref, dar_ref, dai_ref, dbbr_ref, dbbi_ref, dcr_ref, dci_ref, dd_ref,
             gr_ref, gi_ref, cre_ref, cim_ref):
        i = pl.program_id(0)
        first = i == 0

        @pl.when(first)
        def _():
            cre_ref[...] = jnp.zeros_like(cre_ref)
            cim_ref[...] = jnp.zeros_like(cim_ref)

        h = _adaln(x_ref[...], g_ref[...], sc_ref[...], sh_ref[...])
        ub = h.astype(bf16)
        dyy_v = dyy_ref[...]
        dyb = dyy_v.astype(bf16)
        for k in range(S5_BLOCKS):
            dk = dyb[:, _s5_blk(k, S5_BCH)]
            sb = _s5_blk(k, S5_BST)
            gr_ref[:, sb] = lax.dot_general(dk, cr_ref[k], _DN["nt"], preferred_element_type=f32)
            gi_ref[:, sb] = -lax.dot_general(dk, ci_ref[k], _DN["nt"], preferred_element_type=f32)
        ar, ai = ar_ref[...], ai_ref[...]

        def step(s, carry):
            nr_, ni_ = carry
            row = pl.ds(tm - 1 - s, 1)
            nr = gr_ref[row, :] + (ar * nr_ + ai * ni_)
            ni = gi_ref[row, :] + (ar * ni_ - ai * nr_)
            gr_ref[row, :] = nr
            gi_ref[row, :] = ni
            return nr, ni

        nr, ni = lax.fori_loop(0, tm, step, (cre_ref[0:1, :], cim_ref[0:1, :]), unroll=2)
        cre_ref[0:1, :] = nr
        cim_ref[0:1, :] = ni

        is_first_tile = i == nt - 1
        _acc_add(first, dd_ref, jnp.sum(dyy_v * h, axis=0, keepdims=True))
        for k in range(S5_BLOCKS):
            cb, sb = _s5_blk(k, S5_BCH), _s5_blk(k, S5_BST)
            xr_v, xi_v = xr_ref[:, sb], xi_ref[:, sb]
            xrp = jnp.where(is_first_tile, 0.0, xrp_ref[:, sb])
            xip = jnp.where(is_first_tile, 0.0, xip_ref[:, sb])
            xr_s = _roll_rows(jnp.concatenate([xrp, xr_v], axis=0), 1)[8:, :]
            xi_s = _roll_rows(jnp.concatenate([xip, xi_v], axis=0), 1)[8:, :]
            gr, gi = gr_ref[:, sb], gi_ref[:, sb]
            dar_k = jnp.sum(gr * xr_s + gi * xi_s, axis=0, keepdims=True)
            dai_k = jnp.sum(gi * xr_s - gr * xi_s, axis=0, keepdims=True)

            @pl.when(first)
            def _():
                dar_ref[:, sb] = dar_k
                dai_ref[:, sb] = dai_k

            @pl.when(jnp.logical_not(first))
            def _():
                dar_ref[:, sb] += dar_k
                dai_ref[:, sb] += dai_k

            grb, gib = gr.astype(bf16), gi.astype(bf16)
            uk, dk = ub[:, cb], dyb[:, cb]
            tn = lambda a_, b_: lax.dot_general(a_, b_, _DN["tn"], preferred_element_type=f32)
            vals = (tn(uk, grb), tn(uk, gib), tn(xr_v.astype(bf16), dk), -tn(xi_v.astype(bf16), dk))
            for ref, val in zip((dbbr_ref, dbbi_ref, dcr_ref, dci_ref), vals):
                @pl.when(first)
                def _():
                    ref[k] = val

                @pl.when(jnp.logical_not(first))
                def _():
                    ref[k] += val
            du_k = (lax.dot_general(grb, bbr_ref[k], _DN["nt"], preferred_element_type=f32)
                    + lax.dot_general(gib, bbi_ref[k], _DN["nt"], preferred_element_type=f32))
            du_ref[:, cb] = du_k + d_ref[:, cb] * dyy_v[:, cb]

    rev = lambda c: _tok_spec(tm, c, nt, True)
    prev = pl.BlockSpec((8, ns), lambda i: (jnp.maximum((nt - 1 - i) * per - 1, 0), 0))
    bd_in = pl.BlockSpec((S5_BLOCKS, S5_BCH, S5_BST), lambda i: (0, 0, 0))
    bd_out = pl.BlockSpec((S5_BLOCKS, S5_BST, S5_BCH), lambda i: (0, 0, 0))
    row_ns = jax.ShapeDtypeStruct((1, ns), f32)
    bd_in_s = jax.ShapeDtypeStruct((S5_BLOCKS, S5_BCH, S5_BST), f32)
    bd_out_s = jax.ShapeDtypeStruct((S5_BLOCKS, S5_BST, S5_BCH), f32)
    return pl.pallas_call(
        body, name="s5_scan_bwd", grid=(nt,),
        in_specs=[rev(D), rev(D), _row_spec(D), _row_spec(D), _row_spec(D), rev(ns), rev(ns), prev, prev,
                  _row_spec(ns), _row_spec(ns), bd_in, bd_in, bd_out, bd_out, _row_spec(D)],
        out_specs=[rev(D), _row_spec(ns), _row_spec(ns), bd_in, bd_in, bd_out, bd_out, _row_spec(D)],
        out_shape=[jax.ShapeDtypeStruct((t, D), f32), row_ns, row_ns, bd_in_s, bd_in_s, bd_out_s, bd_out_s,
                   jax.ShapeDtypeStruct((1, D), f32)],
        scratch_shapes=[pltpu.VMEM((tm, ns), f32), pltpu.VMEM((tm, ns), f32),
                        pltpu.VMEM((8, ns), f32), pltpu.VMEM((8, ns), f32)],
        compiler_params=_cp(1),
    )(dyy, x, g, scale, shift, xr, xi, xr, xi, lb_re, lb_im, bbr_bd, bbi_bd, cr_bd, ci_bd, dskip)


def _s5_gelu(yy):
    t = yy.shape[0]
    tm = _tok_tile(t)

    def body(y_ref, o_ref):
        o_ref[...] = jax.nn.gelu(y_ref[...]).astype(bf16)

    return pl.pallas_call(
        body, name="s5_gelu", grid=(t // tm,), in_specs=[_tok_spec(tm, D)], out_specs=_tok_spec(tm, D),
        out_shape=jax.ShapeDtypeStruct((t, D), bf16), compiler_params=_cp(1),
    )(yy)


def _s5_glu(gl, z):
    return gl * jax.nn.sigmoid(z)


def _s5_out(x, yy, z, gate):
    t = x.shape[0]
    tm = _tok_tile(t)

    def body(x_ref, y_ref, z_ref, gt_ref, o_ref):
        o_ref[...] = x_ref[...] + (1.0 + gt_ref[...]) * _s5_glu(jax.nn.gelu(y_ref[...]), z_ref[...])

    return pl.pallas_call(
        body, name="s5_out", grid=(t // tm,),
        in_specs=[_tok_spec(tm, D), _tok_spec(tm, D), _tok_spec(tm, D), _row_spec(D)],
        out_specs=_tok_spec(tm, D), out_shape=jax.ShapeDtypeStruct((t, D), f32), compiler_params=_cp(1),
    )(x, yy, z, gate)


def _s5_out_bwd(dxn, yy, z, gate):
    t = dxn.shape[0]
    tm = _tok_tile(t)

    def body(dxn_ref, y_ref, z_ref, gt_ref, dz_ref, dgl_ref, dgt_ref):
        dxn_v = dxn_ref[...]
        gl = jax.nn.gelu(y_ref[...])
        out, vjp = jax.vjp(_s5_glu, gl, z_ref[...])
        dgl, dz = vjp((1.0 + gt_ref[...]) * dxn_v)
        dz_ref[...] = dz.astype(bf16)
        dgl_ref[...] = dgl
        _acc_add(pl.program_id(0) == 0, dgt_ref, jnp.sum(dxn_v * out, axis=0, keepdims=True))

    return pl.pallas_call(
        body, name="s5_out_bwd", grid=(t // tm,),
        in_specs=[_tok_spec(tm, D), _tok_spec(tm, D), _tok_spec(tm, D), _row_spec(D)],
        out_specs=[_tok_spec(tm, D), _tok_spec(tm, D), _row_spec(D)],
        out_shape=[jax.ShapeDtypeStruct((t, D), bf16), jax.ShapeDtypeStruct((t, D), f32),
                   jax.ShapeDtypeStruct((1, D), f32)],
        compiler_params=_cp(1),
    )(dxn, yy, z, gate)


def _s5_gelu_bwd(yy, dgl_a, dgl_b):
    t = yy.shape[0]
    tm = _tok_tile(t)

    def body(y_ref, a_ref, b_ref, o_ref):
        _, vjp = jax.vjp(jax.nn.gelu, y_ref[...])
        o_ref[...] = vjp(a_ref[...] + b_ref[...])[0]

    return pl.pallas_call(
        body, name="s5_gelu_bwd", grid=(t // tm,),
        in_specs=[_tok_spec(tm, D), _tok_spec(tm, D), _tok_spec(tm, D)],
        out_specs=_tok_spec(tm, D), out_shape=jax.ShapeDtypeStruct((t, D), f32), compiler_params=_cp(1),
    )(yy, dgl_a, dgl_b)


def _s5_params(lam_re, lam_im, log_dt, b_re, b_im):
    bc = lambda a: a.reshape(S5_GROUPS, 1, -1)
    return (bc(lam_re), bc(lam_im), jnp.broadcast_to(log_dt.reshape(S5_GROUPS, 1, 1), (S5_GROUPS, 1, S5_STATE)),
            b_re.transpose(0, 2, 1), b_im.transpose(0, 2, 1))


def _s5_fwd(x, ln, raw, c_re, c_im, dskip, w_glu):
    gate = ln[3]
    lb_re, lb_im, bb_re, bb_im = _s5_prep(*raw)
    lbr, lbi = lb_re.reshape(1, S5_NSTATE), lb_im.reshape(1, S5_NSTATE)
    bds = (_s5_in_bd(bb_re).astype(bf16), _s5_in_bd(bb_im).astype(bf16),
           _s5_out_bd(c_re).astype(bf16), _s5_out_bd(c_im).astype(bf16))
    yy, xr, xi = _s5_scan_fwd(x, ln, lbr, lbi, *bds, dskip)
    gl = _s5_gelu(yy)
    z = _mm(gl, w_glu, "nn", f32, "s5_glu_mm")
    return _s5_out(x, yy, z, gate), (x, lbr, lbi, bds, yy, xr, xi, gl, z)


def _s5_bwd(dxn, saved, ln, raw, dskip, w_glu):
    x, lbr, lbi, bds, yy, xr, xi, gl, z = saved
    g, shift, scale, gate = ln
    dz, dgl_a, dgate = _s5_out_bwd(dxn, yy, z, gate)
    dgl_b = _mm(dz, w_glu, "nt", f32, "s5_dgl")
    dw_glu = _mm(gl, dz, "tn", f32, "s5_dwglu")
    dyy = _s5_gelu_bwd(yy, dgl_a, dgl_b)
    du, dar, dai, dbbr_bd, dbbi_bd, dcr_bd, dci_bd, dd = _s5_scan_bwd(dyy, x, ln, xr, xi, lbr, lbi, *bds, dskip)
    shp = (S5_GROUPS, 1, S5_STATE)
    d_lam_re, d_lam_im, d_dt, d_b_re, d_b_im = _s5_prep_bwd(
        *raw, dar.reshape(shp), dai.reshape(shp), _s5_in_bd_diag(dbbr_bd), _s5_in_bd_diag(dbbi_bd))
    dx, dg, dscale, dshift = _ln_bwd(x, g, scale, shift, du, dxn)
    grads = dict(
        s5_lam_re=d_lam_re.reshape(1, S5_GROUPS, S5_STATE), s5_lam_im=d_lam_im.reshape(1, S5_GROUPS, S5_STATE),
        s5_log_dt=d_dt[:, 0, 0].reshape(1, S5_GROUPS),
        s5_b_re=d_b_re.transpose(0, 2, 1)[None], s5_b_im=d_b_im.transpose(0, 2, 1)[None],
        s5_c_re=_s5_out_bd_diag(dcr_bd)[None], s5_c_im=_s5_out_bd_diag(dci_bd)[None],
        s5_d=dd, s5_w_glu=dw_glu)
    return dx, (dg, dshift, dscale, dgate), grads


_MESH = pl.DeviceIdType.MESH
_ANY = pl.BlockSpec(memory_space=pl.ANY)


def _me():
    return lax.axis_index("x"), lax.axis_index("y"), lax.axis_index("c")


def _dev_index(x, y, c):
    return 4 * x + 2 * y + c


def _all_gather(vs, name):
    n = len(vs)

    def body(*refs):
        v_refs, out_refs = refs[:n], refs[n:2 * n]
        send_sems, recv_sems, local_sems = refs[2 * n:]
        x, y, cc = _me()
        me, sibling = (x, y, cc), (x, y, 1 - cc)
        chips = [(1 - x, y), (x, 1 - y), (1 - x, 1 - y)]
        sends, local = [], []

        def copy(a, k, block, to, src=None):
            rows = out_refs[a].at[_dev_index(*block)]
            return pltpu.make_async_remote_copy(
                src_ref=rows if src is None else src, dst_ref=rows,
                send_sem=send_sems.at[7 * a + k], recv_sem=recv_sems.at[7 * a + k], device_id=to, device_id_type=_MESH)

        for a in range(n):
            mine = pltpu.make_async_copy(v_refs[a], out_refs[a].at[_dev_index(*me)], local_sems.at[a])
            mine.start()
            local.append(mine)
            first = [copy(a, 0, me, sibling, src=v_refs[a])]
            first += [copy(a, 1 + j, me, (*chip, cc), src=v_refs[a]) for j, chip in enumerate(chips)]
            for cp in first:
                cp.start()
            sends += first
        for a in range(n):
            for j, chip in enumerate(chips):
                copy(a, 1 + j, (*chip, cc), me).wait_recv()
                passed = copy(a, 4 + j, (*chip, cc), sibling)
                passed.start()
                sends.append(passed)
        for a in range(n):
            copy(a, 0, sibling, me).wait_recv()
            for j, chip in enumerate(chips):
                copy(a, 4 + j, (*chip, 1 - cc), me).wait_recv()
        for cp in sends:
            cp.wait_send()
        for cp in local:
            cp.wait()

    return pl.pallas_call(
        body, name=name, out_shape=[jax.ShapeDtypeStruct((N_DEV,) + v.shape, v.dtype) for v in vs],
        in_specs=[_ANY] * n, out_specs=[_ANY] * n,
        scratch_shapes=[pltpu.SemaphoreType.DMA((7 * n,)), pltpu.SemaphoreType.DMA((7 * n,)),
                        pltpu.SemaphoreType.DMA((n,))],
    )(*vs)


def _exchange_pair(vs, name):
    n = len(vs)

    def body(*refs):
        v_refs, out_refs = refs[:n], refs[n:2 * n]
        send_sems, recv_sems = refs[2 * n:]
        x, y, cc = _me()
        sibling = (x, y, 1 - cc)
        copies = []
        for a in range(n):
            for k in range(4):
                cp = pltpu.make_async_remote_copy(
                    src_ref=v_refs[a].at[2 * k + (1 - cc)], dst_ref=out_refs[a].at[k],
                    send_sem=send_sems.at[4 * a + k], recv_sem=recv_sems.at[4 * a + k],
                    device_id=sibling, device_id_type=_MESH)
                cp.start()
                copies.append(cp)
        for cp in copies:
            cp.wait_recv()
        for cp in copies:
            cp.wait_send()

    return pl.pallas_call(
        body, name=name, out_shape=[jax.ShapeDtypeStruct((4,) + v.shape[1:], v.dtype) for v in vs],
        in_specs=[_ANY] * n, out_specs=[_ANY] * n,
        scratch_shapes=[pltpu.SemaphoreType.DMA((4 * n,)), pltpu.SemaphoreType.DMA((4 * n,))],
    )(*vs)


def _pair_sum(v, got):
    _, r, c = v.shape
    tr = _pick(r, (512, 352, 256, 128))
    core = lax.axis_index("c").astype(jnp.int32).reshape(1)

    def body(c_ref, v_ref, g_ref, o_ref):
        o_ref[...] = (v_ref[...] + g_ref[...]).astype(bf16)

    return pl.pallas_call(
        body, name="pair_sum",
        grid_spec=pltpu.PrefetchScalarGridSpec(
            num_scalar_prefetch=1, grid=(4, r // tr),
            in_specs=[pl.BlockSpec((1, tr, c), lambda k, i, c_ref: (2 * k + c_ref[0], i, 0)),
                      pl.BlockSpec((1, tr, c), lambda k, i, c_ref: (k, i, 0))],
            out_specs=pl.BlockSpec((1, tr, c), lambda k, i, c_ref: (k, i, 0))),
        out_shape=jax.ShapeDtypeStruct((4, r, c), bf16), compiler_params=_cp(2),
    )(core, v, got)


def _exchange_chips(vs, name):
    n = len(vs)

    def body(*refs):
        v_refs, out_refs = refs[:n], refs[n:2 * n]
        send_sems, recv_sems, local_sems = refs[2 * n:]
        x, y, cc = _me()
        mine = 2 * x + y
        peers = []
        for mask in (1, 2, 3):
            px = 1 - x if mask & 2 else x
            py = 1 - y if mask & 1 else y
            peers.append((mask - 1, (px, py, cc), 2 * px + py))
        local, sends = [], []
        for a in range(n):
            own = pltpu.make_async_copy(v_refs[a].at[mine], out_refs[a].at[mine], local_sems.at[a])
            own.start()
            local.append(own)
            for k, peer, pchip in peers:
                cp = pltpu.make_async_remote_copy(
                    src_ref=v_refs[a].at[pchip], dst_ref=out_refs[a].at[mine],
                    send_sem=send_sems.at[3 * a + k], recv_sem=recv_sems.at[3 * a + k],
                    device_id=peer, device_id_type=_MESH)
                cp.start()
                sends.append(cp)
        for a in range(n):
            for k, peer, pchip in peers:
                pltpu.make_async_remote_copy(
                    src_ref=v_refs[a].at[pchip], dst_ref=out_refs[a].at[pchip],
                    send_sem=send_sems.at[3 * a + k], recv_sem=recv_sems.at[3 * a + k],
                    device_id=peer, device_id_type=_MESH).wait_recv()
        for cp in sends:
            cp.wait_send()
        for cp in local:
            cp.wait()

    return pl.pallas_call(
        body, name=name, out_shape=[jax.ShapeDtypeStruct(v.shape, v.dtype) for v in vs],
        in_specs=[_ANY] * n, out_specs=[_ANY] * n,
        scratch_shapes=[pltpu.SemaphoreType.DMA((3 * n,)), pltpu.SemaphoreType.DMA((3 * n,)),
                        pltpu.SemaphoreType.DMA((n,))],
    )(*vs)


def _ada_mod(c_all, ada_w):
    cols = ada_w.shape[2]

    def body(c_ref, w_ref, o_ref):
        cond = jax.nn.silu(c_ref[...]).astype(bf16)
        o_ref[0] = jnp.dot(cond, w_ref[0].astype(bf16), preferred_element_type=f32)

    return pl.pallas_call(
        body, name="ada_mod", grid=(DEPTH,),
        in_specs=[pl.BlockSpec((16, D), lambda i: (0, 0)), pl.BlockSpec((1, D, cols), lambda i: (i, 0, 0))],
        out_specs=pl.BlockSpec((1, 16, cols), lambda i: (i, 0, 0)),
        out_shape=jax.ShapeDtypeStruct((DEPTH, 16, cols), f32), compiler_params=_cp(1),
    )(c_all, ada_w)


def _ada_grad(c_all, dmod):
    cols = dmod.shape[2]

    def body(c_ref, d_ref, o_ref):
        cond = jax.nn.silu(c_ref[...]).astype(bf16)
        o_ref[0] = lax.dot_general(cond, d_ref[0].astype(bf16), _DN["tn"], preferred_element_type=f32)

    return pl.pallas_call(
        body, name="ada_grad", grid=(DEPTH,),
        in_specs=[pl.BlockSpec((16, D), lambda i: (0, 0)), pl.BlockSpec((1, 16, cols), lambda i: (i, 0, 0))],
        out_specs=pl.BlockSpec((1, D, cols), lambda i: (i, 0, 0)),
        out_shape=jax.ShapeDtypeStruct((DEPTH, D, cols), f32), compiler_params=_cp(1),
    )(c_all, dmod)


def _row_tile(r):
    return _pick(r, (512, 352, 256, 128)) if r > 512 else r


def _sum_sources(v, name):
    n, r, c = v.shape
    tr = _row_tile(r)

    def body(v_ref, o_ref):
        acc = v_ref[0]
        for p in range(1, n):
            acc = acc + v_ref[p]
        o_ref[...] = acc.astype(f32)

    return pl.pallas_call(
        body, name=name, grid=(r // tr,),
        in_specs=[pl.BlockSpec((n, tr, c), lambda i: (0, i, 0))], out_specs=pl.BlockSpec((tr, c), lambda i: (i, 0)),
        out_shape=jax.ShapeDtypeStruct((r, c), f32), compiler_params=_cp(1),
    )(v)


def _adamw(parts, w, m, v, name):
    n, r, c = parts.shape
    tr = _row_tile(r)
    c1 = 1.0 - ADAM_B1 ** ADAM_STEP
    c2 = 1.0 - ADAM_B2 ** ADAM_STEP

    def body(p_ref, w_ref, m_ref, v_ref, g_ref, d_ref, mo_ref, vo_ref):
        g_v = p_ref[0].astype(f32)
        for p in range(1, n):
            g_v = g_v + p_ref[p].astype(f32)
        m_n = ADAM_B1 * m_ref[...] + (1.0 - ADAM_B1) * g_v
        v_n = ADAM_B2 * v_ref[...] + (1.0 - ADAM_B2) * (g_v * g_v)
        g_ref[...] = g_v
        d_ref[...] = -ADAM_LR * ((m_n / c1) / (jnp.sqrt(v_n / c2) + ADAM_EPS) + ADAM_WD * w_ref[...])
        mo_ref[...] = m_n
        vo_ref[...] = v_n

    spec = pl.BlockSpec((tr, c), lambda i: (i, 0))
    shp = jax.ShapeDtypeStruct((r, c), f32)
    return pl.pallas_call(
        body, name=name, grid=(r // tr,), in_specs=[pl.BlockSpec((n, tr, c), lambda i: (0, i, 0))] + [spec] * 3,
        out_specs=[spec] * 4, out_shape=[shp] * 4, compiler_params=_cp(1),
    )(parts, w, m, v)


def _two_d(shape):
    return (math.prod(shape[:-1]), shape[-1])


def _pack_rows(a):
    n = a.size
    rows = -(-n // (8 * PACK_C)) * 8
    return jnp.pad(a.reshape(-1), (0, rows * PACK_C - n)).reshape(rows, PACK_C)


def _pack(parts):
    return jnp.concatenate([_pack_rows(p) for p in parts], axis=0)


def _unpack(packed, shapes):
    lead = packed.shape[:-2]
    out, off = [], 0
    for s in shapes:
        n = math.prod(s)
        rows = -(-n // (8 * PACK_C)) * 8
        part = packed[..., off:off + rows, :].reshape(lead + (rows * PACK_C,))
        out.append(part[..., :n].reshape(lead + tuple(s)))
        off += rows
    return out


def _unshard(g8, axis):
    local = g8.shape[1:]
    moved = jnp.moveaxis(g8, 0, axis)
    return moved.reshape(local[:axis] + (N_DEV * local[axis],) + local[axis + 1:])


_BIG = dict(ffn_w_in=3, ffn_w_out=2, pool_w=2, fox_w_in=2, fox_w_o=1, s5_w_glu=1, conv_w_in=2, conv_w_out=1)
_SMALL_SHARDED = dict(norm_g=2, s5_d=1, conv_w=3)
_REPLICATED = ("ada_b", "pool_scale", "fox_b_f", "fox_q_gain", "fox_k_gain", "s5_lam_re", "s5_lam_im", "s5_log_dt",
               "s5_b_re", "s5_b_im", "s5_c_re", "s5_c_im")
_WEIGHTS = ("ada_w", "ada_b", "norm_g", "ffn_w_in", "ffn_w_out", "pool_w", "pool_scale", "fox_w_in", "fox_b_f",
            "fox_q_gain", "fox_k_gain", "fox_w_o", "s5_lam_re", "s5_lam_im", "s5_log_dt", "s5_b_re", "s5_b_im",
            "s5_c_re", "s5_c_im", "s5_d", "s5_w_glu", "conv_w_in", "conv_w", "conv_w_out")


def _step(x, c, target, w, m, v):
    t = x.shape[1]
    xi_, yi_, ci_ = _me()
    me = _dev_index(xi_, yi_, ci_)

    sm_shapes = [w[n].shape for n in _SMALL_SHARDED]
    small_all = _all_gather([_pack([c] + [w[n] for n in _SMALL_SHARDED])], "gather_small")[0]
    gathered = _unpack(small_all, [c.shape] + sm_shapes)
    c_all = gathered[0][:, 0, :]
    full = {n: _unshard(p, ax) for (n, ax), p in zip(_SMALL_SHARDED.items(), gathered[1:])}

    big_all = _all_gather([w[n].astype(bf16).reshape(_two_d(w[n].shape)) for n in _BIG], "gather_weights")
    gw = dict(zip(_BIG, big_all))
    ffn_w_in = gw["ffn_w_in"].reshape(N_DEV, 2 * DEPTH, D, FFN_HS)
    ffn_w_out = gw["ffn_w_out"].reshape(N_DEV, 2 * DEPTH, D_FF // N_DEV, D)
    pool_w = gw["pool_w"].reshape(N_DEV, 4, POOL_GROUP // N_DEV, POOL_GROUP).transpose(1, 0, 2, 3)
    pool_w = pool_w.reshape(4, POOL_GROUP, POOL_GROUP)
    fox_w_in = jnp.pad(gw["fox_w_in"].transpose(1, 0, 2).reshape(D, FOX_PROJ), ((0, 0), (0, FOX_PROJ_PAD - FOX_PROJ)))
    fox_w_o, s5_w_glu, conv_w_out = (gw[n].reshape(D, D) for n in ("fox_w_o", "s5_w_glu", "conv_w_out"))
    conv_w_in = gw["conv_w_in"]

    def ffn_w(i, f):
        return ffn_w_in, ffn_w_out, 2 * i + f

    c16 = jnp.pad(c_all, ((0, 8), (0, 0)))
    cols = w["ada_w"].shape[2]
    mod_sh = _ada_mod(c16, w["ada_w"])
    mod_all = _all_gather([mod_sh.reshape(DEPTH * 16, cols)], "gather_mod")[0].reshape(N_DEV, DEPTH, 16, cols)
    mod_mine = lax.dynamic_index_in_dim(mod_all, me, axis=2, keepdims=False)
    mod = (mod_mine.transpose(1, 0, 2).reshape(DEPTH, N_DEV * cols) + w["ada_b"]).reshape(DEPTH, 3, 3, D)

    norm_g = full["norm_g"]

    def ln_of(i, sub):
        return (norm_g[i, sub][None], mod[i, sub, 0][None], mod[i, sub, 1][None], mod[i, sub, 2][None])

    fox_b_f = jnp.pad(w["fox_b_f"], ((0, 0), (0, 128 - FOX_HEADS)))
    s5_raw = _s5_params(w["s5_lam_re"][0], w["s5_lam_im"][0], w["s5_log_dt"][0], w["s5_b_re"][0], w["s5_b_im"][0])
    s5_c_re, s5_c_im = w["s5_c_re"][0], w["s5_c_im"][0]
    conv_w = jnp.pad(full["conv_w"][0, :, 0, :], ((0, 5), (0, 0)))

    xs = x[0]
    saved = []
    for i in range(DEPTH):
        xs, s0 = _ffn_fwd(xs, ln_of(i, 0), *ffn_w(i, 0), 0.5)
        if i == 0:
            s1 = xs
            xs = _pool_fwd(xs, ln_of(i, 1), pool_w, w["pool_scale"])
        elif i == 1:
            xs, s1 = _fox_fwd(xs, ln_of(i, 1), fox_w_in, fox_b_f, w["fox_q_gain"], w["fox_k_gain"], fox_w_o)
        elif i == 2:
            xs, s1 = _s5_fwd(xs, ln_of(i, 1), s5_raw, s5_c_re, s5_c_im, full["s5_d"], s5_w_glu)
        else:
            xs, s1 = _convmix_fwd(xs, ln_of(i, 1), conv_w_in, conv_w, conv_w_out)
        xs, s2 = _ffn_fwd(xs, ln_of(i, 2), *ffn_w(i, 1), 0.5)
        saved.append((s0, s1, s2))
    dx, lpart = _loss_head(xs, target[0])

    grads = {}
    dmod = [[None] * 3 for _ in range(DEPTH)]
    dnorm = [[None] * 3 for _ in range(DEPTH)]
    dffn_in = lax.empty(ffn_w_in.shape, f32)
    dffn_out = lax.empty(ffn_w_out.shape, f32)

    def put_ln(i, sub, dln):
        dg, dshift, dscale, dgate = dln
        dnorm[i][sub] = dg
        dmod[i][sub] = jnp.concatenate([dshift, dscale, dgate], axis=0)

    for i in reversed(range(DEPTH)):
        s0, s1, s2 = saved[i]
        dx, dln, dffn_in, dffn_out = _ffn_bwd(dx, s2, ln_of(i, 2), *ffn_w(i, 1), 0.5, dffn_in, dffn_out)
        put_ln(i, 2, dln)
        if i == 0:
            dx, dln, dpw, dps = _pool_bwd(dx, s1, ln_of(i, 1), pool_w, w["pool_scale"])
            dpw = dpw.reshape(4, N_DEV, POOL_GROUP // N_DEV, POOL_GROUP).transpose(1, 0, 2, 3)
            grads.update(pool_w=dpw.reshape(N_DEV, 4 * POOL_GROUP // N_DEV, POOL_GROUP), pool_scale=dps)
        elif i == 1:
            dx, dln, dwi, dbf, dqg, dkg, dwo = _fox_bwd(dx, s1, ln_of(i, 1), fox_w_in, fox_b_f, fox_w_o)
            dwi = dwi[:, :FOX_PROJ].reshape(D, N_DEV, FOX_PROJ // N_DEV).transpose(1, 0, 2)
            grads.update(fox_w_in=dwi, fox_b_f=dbf[:, :FOX_HEADS], fox_q_gain=dqg, fox_k_gain=dkg,
                         fox_w_o=dwo.reshape(N_DEV, D // N_DEV, D))
        elif i == 2:
            dx, dln, gs5 = _s5_bwd(dx, s1, ln_of(i, 1), s5_raw, full["s5_d"], s5_w_glu)
            gs5["s5_w_glu"] = gs5["s5_w_glu"].reshape(N_DEV, D // N_DEV, D)
            grads.update(gs5)
        else:
            dx, dln, dwi, dcw, dwo = _convmix_bwd(dx, s1, ln_of(i, 1), conv_w_in, conv_w, conv_w_out)
            grads.update(conv_w_in=dwi, conv_w=dcw[None, :, None, :], conv_w_out=dwo.reshape(N_DEV, D // N_DEV, D))
        put_ln(i, 1, dln)
        dx, dln, dffn_in, dffn_out = _ffn_bwd(dx, s0, ln_of(i, 0), *ffn_w(i, 0), 0.5, dffn_in, dffn_out)
        put_ln(i, 0, dln)
    grads["ffn_w_in"] = dffn_in.reshape(N_DEV, 2 * DEPTH * D, FFN_HS)
    grads["ffn_w_out"] = dffn_out.reshape(N_DEV, 2 * DEPTH * D_FF // N_DEV, D)
    grads["norm_g"] = jnp.stack([jnp.concatenate(r, axis=0) for r in dnorm])
    dmod_mine = jnp.stack([jnp.stack(r) for r in dmod]).reshape(DEPTH, 9 * D)

    small_names = list(_REPLICATED[1:]) + list(_SMALL_SHARDED)
    small_parts = [dmod_mine] + [grads[n] for n in small_names] + [lpart[:, 0:1]]
    small_g = _all_gather([_pack(small_parts)], "gather_grads")[0]
    small_sum = _sum_sources(small_g, "sum_small")
    summed = dict(zip(["ada_b"] + small_names + ["loss"], _unpack(small_sum, [p.shape for p in small_parts])))
    loss = summed.pop("loss")[0, 0]
    for n, ax in _SMALL_SHARDED.items():
        local = w[n].shape[ax]
        summed[n] = lax.dynamic_slice_in_dim(summed[n], me * local, local, axis=ax)

    dmod_all = small_g[:, :DEPTH * 9].reshape(N_DEV, DEPTH, 9 * D)
    dmod_cols = lax.dynamic_slice_in_dim(dmod_all, me * cols, cols, axis=2)
    ada_g = _ada_grad(c16, jnp.pad(dmod_cols.transpose(1, 0, 2), ((0, 0), (0, 8), (0, 0))))

    big_parts = [grads[n] for n in _BIG]
    from_sibling = _exchange_pair(big_parts, "exchange_pair")
    chip_sums = [_pair_sum(p, s) for p, s in zip(big_parts, from_sibling)]
    big_landed = dict(zip(_BIG, _exchange_chips(chip_sums, "exchange_chips")))

    grad, delta, new_m, new_v = {}, {}, {}, {}
    big_landed["ada_w"] = ada_g[None]
    for n, parts in big_landed.items():
        view = _two_d(w[n].shape)
        outs = _adamw(parts.reshape((parts.shape[0],) + view), w[n].reshape(view), m[n].reshape(view),
                      v[n].reshape(view), "adamw_" + n)
        grad[n], delta[n], new_m[n], new_v[n] = (a.reshape(w[n].shape) for a in outs)
    small = [n for n in _WEIGHTS if n not in big_landed]
    small_shapes = [w[n].shape for n in small]
    pk = lambda d: _pack([d[n] for n in small])
    outs = _adamw(pk(summed)[None], pk(w), pk(m), pk(v), "adamw_small")
    for dst, packed in zip((grad, delta, new_m, new_v), outs):
        dst.update(zip(small, _unpack(packed, small_shapes)))
    return (loss, dx[None], *[grad[n] for n in _WEIGHTS], *[delta[n] for n in _WEIGHTS],
            *[new_m[n] for n in _WEIGHTS], *[new_v[n] for n in _WEIGHTS])


def kernel(x, c, ada_w, ada_b, norm_g, ffn_w_in, ffn_w_out, pool_w, pool_scale, fox_w_in, fox_b_f, fox_q_gain, fox_k_gain, fox_w_o, s5_lam_re, s5_lam_im, s5_log_dt, s5_b_re, s5_b_im, s5_c_re, s5_c_im, s5_d, s5_w_glu, conv_w_in, conv_w, conv_w_out, loss_target, m_ada_w, m_ada_b, m_norm_g, m_ffn_w_in, m_ffn_w_out, m_pool_w, m_pool_scale, m_fox_w_in, m_fox_b_f, m_fox_q_gain, m_fox_k_gain, m_fox_w_o, m_s5_lam_re, m_s5_lam_im, m_s5_log_dt, m_s5_b_re, m_s5_b_im, m_s5_c_re, m_s5_c_im, m_s5_d, m_s5_w_glu, m_conv_w_in, m_conv_w, m_conv_w_out, v_ada_w, v_ada_b, v_norm_g, v_ffn_w_in, v_ffn_w_out, v_pool_w, v_pool_scale, v_fox_w_in, v_fox_b_f, v_fox_q_gain, v_fox_k_gain, v_fox_w_o, v_s5_lam_re, v_s5_lam_im, v_s5_log_dt, v_s5_b_re, v_s5_b_im, v_s5_c_re, v_s5_c_im, v_s5_d, v_s5_w_glu, v_conv_w_in, v_conv_w, v_conv_w_out):
    ws = (ada_w, ada_b, norm_g, ffn_w_in, ffn_w_out, pool_w, pool_scale, fox_w_in, fox_b_f, fox_q_gain, fox_k_gain,
          fox_w_o, s5_lam_re, s5_lam_im, s5_log_dt, s5_b_re, s5_b_im, s5_c_re, s5_c_im, s5_d, s5_w_glu, conv_w_in,
          conv_w, conv_w_out)
    ms = (m_ada_w, m_ada_b, m_norm_g, m_ffn_w_in, m_ffn_w_out, m_pool_w, m_pool_scale, m_fox_w_in, m_fox_b_f,
          m_fox_q_gain, m_fox_k_gain, m_fox_w_o, m_s5_lam_re, m_s5_lam_im, m_s5_log_dt, m_s5_b_re, m_s5_b_im,
          m_s5_c_re, m_s5_c_im, m_s5_d, m_s5_w_glu, m_conv_w_in, m_conv_w, m_conv_w_out)
    vs = (v_ada_w, v_ada_b, v_norm_g, v_ffn_w_in, v_ffn_w_out, v_pool_w, v_pool_scale, v_fox_w_in, v_fox_b_f,
          v_fox_q_gain, v_fox_k_gain, v_fox_w_o, v_s5_lam_re, v_s5_lam_im, v_s5_log_dt, v_s5_b_re, v_s5_b_im,
          v_s5_c_re, v_s5_c_im, v_s5_d, v_s5_w_glu, v_conv_w_in, v_conv_w, v_conv_w_out)
    return _step(x, c, loss_target, dict(zip(_WEIGHTS, ws)), dict(zip(_WEIGHTS, ms)), dict(zip(_WEIGHTS, vs)))
```
